```python
import math
import jax, jax.numpy as jnp
from jax import lax
import numpy as np

D_MODEL = 1024
BATCH = 8
SEQ = 8192
DEPTH = 1

N_META = 16
MIX_WIDTH = D_MODEL
ATTN_WIDTH = MIX_WIDTH // 2
CONV_WIDTH = MIX_WIDTH - ATTN_WIDTH
HEAD_DIM = 64
N_HEADS = ATTN_WIDTH // HEAD_DIM
N_KV_HEADS = 2
GROUP = N_HEADS // N_KV_HEADS
KV_WIDTH = N_KV_HEADS * HEAD_DIM
WINDOW = 128
BLOCK = 128
CONV_K = 31
N_CONV_GROUPS = CONV_WIDTH // HEAD_DIM
D_FF = int(math.ceil(8 * D_MODEL / 3 / 256) * 256)
IN_COLS = ATTN_WIDTH + 2 * KV_WIDTH + 2 * CONV_WIDTH
SPLITS = (ATTN_WIDTH,
          ATTN_WIDTH + KV_WIDTH,
          ATTN_WIDTH + 2 * KV_WIDTH,
          ATTN_WIDTH + 2 * KV_WIDTH + CONV_WIDTH)
NORM_EPS = 1e-5

kernel_name = "hymba_conformer_swa_sink_alibi_layer"


def rmsnorm(x, g):
    xf = x.astype(jnp.float32)
    y = xf * lax.rsqrt(jnp.mean(xf * xf, axis=-1, keepdims=True) + NORM_EPS)
    return (y * g.astype(jnp.float32)).astype(x.dtype)


def alibi_slopes():
    return jnp.power(2.0, -(8.0 / N_HEADS) * jnp.arange(1, N_HEADS + 1, dtype=jnp.float32))


def swa_sink_alibi_attention(q, k, v, sinks):
    B, L = q.shape[0], q.shape[1]
    lead = BLOCK - N_META
    P = L + lead
    nb = P // BLOCK
    pad = ((0, 0), (lead, 0), (0, 0), (0, 0))
    qb = jnp.pad(q, pad).reshape(B, nb, BLOCK, N_KV_HEADS, GROUP, HEAD_DIM)
    kb = jnp.pad(k, pad).reshape(B, nb, BLOCK, N_KV_HEADS, HEAD_DIM)
    vb = jnp.pad(v, pad).reshape(B, nb, BLOCK, N_KV_HEADS, HEAD_DIM)
    shift = ((0, 0), (1, 0), (0, 0), (0, 0), (0, 0))
    k_band = jnp.concatenate([jnp.pad(kb, shift)[:, :-1], kb], axis=2)
    v_band = jnp.concatenate([jnp.pad(vb, shift)[:, :-1], vb], axis=2)
    k_meta = jnp.broadcast_to(k[:, None, :N_META], (B, nb, N_META, N_KV_HEADS, HEAD_DIM))
    v_meta = jnp.broadcast_to(v[:, None, :N_META], (B, nb, N_META, N_KV_HEADS, HEAD_DIM))
    k_all = jnp.concatenate([k_meta, k_band], axis=2)
    v_all = jnp.concatenate([v_meta, v_band], axis=2)

    scale = 1.0 / math.sqrt(HEAD_DIM)
    s = jnp.einsum('bnqkgd,bnskd->bnkgqs', qb, k_all).astype(jnp.float32) * scale

    qi = jnp.arange(nb)[:, None] * BLOCK + jnp.arange(BLOCK)[None, :]
    kj = (jnp.arange(nb)[:, None] - 1) * BLOCK + jnp.arange(2 * BLOCK)[None, :]
    dist = qi[:, :, None] - kj[:, None, :]
    band_ok = (dist >= 0) & (dist < WINDOW) & (kj[:, None, :] >= BLOCK)
    meta_j = lead + jnp.arange(N_META)
    meta_ok = meta_j[None, None, :] <= qi[:, :, None]
    ok = jnp.concatenate([meta_ok, band_ok], axis=-1)
    dist_all = jnp.concatenate(
        [jnp.zeros((nb, BLOCK, N_META), jnp.float32), dist.astype(jnp.float32)], axis=-1)
    slopes = alibi_slopes().reshape(N_KV_HEADS, GROUP)
    bias = -slopes[None, :, :, None, None] * dist_all[:, None, None, :, :]
    s = jnp.where(ok[:, None, None], s + bias[None], -jnp.inf)

    sink = sinks.astype(jnp.float32).reshape(N_KV_HEADS, GROUP)[None, None, :, :, None, None]
    sink = jnp.broadcast_to(sink, s.shape[:-1] + (1,))
    p = jax.nn.softmax(jnp.concatenate([s, sink], axis=-1), axis=-1)[..., :-1]
    o = jnp.einsum('bnkgqs,bnskd->bnqkgd', p.astype(v.dtype), v_all)
    return o.reshape(B, P, N_HEADS * HEAD_DIM)[:, lead:]


def conformer_conv(a, gate, conv_w, conv_b, ln_g, ln_b):
    u = a * jax.nn.sigmoid(gate)
    y = lax.conv_general_dilated(
        u, conv_w[:, None, :].astype(u.dtype), window_strides=(1,),
        padding=((CONV_K - 1, 0),), dimension_numbers=('NWC', 'WIO', 'NWC'),
        feature_group_count=CONV_WIDTH) + conv_b
    yf = y.astype(jnp.float32)
    mu = jnp.mean(yf, axis=-1, keepdims=True)
    var = jnp.mean(jnp.square(yf - mu), axis=-1, keepdims=True)
    yn = (yf - mu) * lax.rsqrt(var + NORM_EPS) * ln_g.astype(jnp.float32) + ln_b.astype(jnp.float32)
    return jax.nn.silu(yn).astype(a.dtype)


def _fwd_setup_inputs(seed: int = 0) -> dict:
    key = jax.random.key(seed)
    ks = jax.random.split(key, 20)
    f32 = jnp.float32
    nrm = lambda k, shape, s: jax.random.normal(k, shape, f32) * s
    return {
        "x": nrm(ks[0], (BATCH, SEQ, D_MODEL), 1.0),
        "meta_tokens": nrm(ks[1], (N_META, D_MODEL), 1.0),
        "attn_norm_g": 1.0 + nrm(ks[2], (DEPTH, D_MODEL), 0.02),
        "w_in": nrm(ks[3], (DEPTH, D_MODEL, IN_COLS), D_MODEL ** -0.5),
        "attn_sinks": nrm(ks[4], (DEPTH, N_HEADS), 0.5),
        "conv_w": nrm(ks[5], (DEPTH, CONV_K, CONV_WIDTH), CONV_K ** -0.5),
        "conv_b": nrm(ks[6], (DEPTH, CONV_WIDTH), 0.02),
        "conv_ln_g": 1.0 + nrm(ks[7], (DEPTH, CONV_WIDTH), 0.02),
        "conv_ln_b": nrm(ks[8], (DEPTH, CONV_WIDTH), 0.02),
        "attn_out_g": 1.0 + nrm(ks[9], (DEPTH, ATTN_WIDTH), 0.02),
        "conv_out_g": 1.0 + nrm(ks[10], (DEPTH, CONV_WIDTH), 0.02),
        "w_out": nrm(ks[11], (DEPTH, MIX_WIDTH, D_MODEL), MIX_WIDTH ** -0.5),
        "ffn_norm_g": 1.0 + nrm(ks[12], (DEPTH, D_MODEL), 0.02),
        "w_gate": nrm(ks[13], (DEPTH, D_MODEL, D_FF), D_MODEL ** -0.5),
        "w_up": nrm(ks[14], (DEPTH, D_MODEL, D_FF), D_MODEL ** -0.5),
        "w_down": nrm(ks[15], (DEPTH, D_FF, D_MODEL), D_FF ** -0.5),
        "final_norm_g": 1.0 + nrm(ks[16], (D_MODEL,), 0.02),
    }


def _fwd_reference(x, meta_tokens, attn_norm_g, w_in, attn_sinks, conv_w, conv_b, conv_ln_g,
              conv_ln_b, attn_out_g, conv_out_g, w_out, ffn_norm_g, w_gate, w_up, w_down,
              final_norm_g):
    B = x.shape[0]
    meta = jnp.broadcast_to(meta_tokens[None].astype(x.dtype), (B, N_META, D_MODEL))
    h = jnp.concatenate([meta, x], axis=1)
    L = h.shape[1]
    for l in range(DEPTH):
        hn = rmsnorm(h, attn_norm_g[l])
        proj = hn @ w_in[l]
        q, k, v, ca, cg = jnp.split(proj, SPLITS, axis=-1)
        q = q.reshape(B, L, N_HEADS, HEAD_DIM)
        k = k.reshape(B, L, N_KV_HEADS, HEAD_DIM)
        v = v.reshape(B, L, N_KV_HEADS, HEAD_DIM)
        o_attn = swa_sink_alibi_attention(q, k, v, attn_sinks[l])
        o_conv = conformer_conv(ca, cg, conv_w[l], conv_b[l], conv_ln_g[l], conv_ln_b[l])
        mixed = jnp.concatenate([rmsnorm(o_attn, attn_out_g[l]),
                                 rmsnorm(o_conv, conv_out_g[l])], axis=-1)
        h = h + mixed @ w_out[l]
        hn = rmsnorm(h, ffn_norm_g[l])
        h = h + (jax.nn.silu(hn @ w_gate[l]) * (hn @ w_up[l])) @ w_down[l]
    return rmsnorm(h, final_norm_g)[:, N_META:]


import jax as _jax
import jax.numpy as _jnp

TWIN_FORMAT = 'train_step'
FWD_PARAMS = ['x', 'meta_tokens', 'attn_norm_g', 'w_in', 'attn_sinks', 'conv_w', 'conv_b', 'conv_ln_g', 'conv_ln_b', 'attn_out_g', 'conv_out_g', 'w_out', 'ffn_norm_g', 'w_gate', 'w_up', 'w_down', 'final_norm_g']
TWIN_WEIGHTS = ['meta_tokens', 'attn_norm_g', 'w_in', 'attn_sinks', 'conv_w', 'conv_b', 'conv_ln_g', 'conv_ln_b', 'attn_out_g', 'conv_out_g', 'w_out', 'ffn_norm_g', 'w_gate', 'w_up', 'w_down', 'final_norm_g']
TWIN_DIFF_INPUT = 'x'
TWIN_INPUTS = ['x', 'meta_tokens', 'attn_norm_g', 'w_in', 'attn_sinks', 'conv_w', 'conv_b', 'conv_ln_g', 'conv_ln_b', 'attn_out_g', 'conv_out_g', 'w_out', 'ffn_norm_g', 'w_gate', 'w_up', 'w_down', 'final_norm_g', 'loss_target', 'm_meta_tokens', 'm_attn_norm_g', 'm_w_in', 'm_attn_sinks', 'm_conv_w', 'm_conv_b', 'm_conv_ln_g', 'm_conv_ln_b', 'm_attn_out_g', 'm_conv_out_g', 'm_w_out', 'm_ffn_norm_g', 'm_w_gate', 'm_w_up', 'm_w_down', 'm_final_norm_g', 'v_meta_tokens', 'v_attn_norm_g', 'v_w_in', 'v_attn_sinks', 'v_conv_w', 'v_conv_b', 'v_conv_ln_g', 'v_conv_ln_b', 'v_attn_out_g', 'v_conv_out_g', 'v_w_out', 'v_ffn_norm_g', 'v_w_gate', 'v_w_up', 'v_w_down', 'v_final_norm_g']
TWIN_OUTPUTS = ['loss', 'grad_x', 'grad_meta_tokens', 'grad_attn_norm_g', 'grad_w_in', 'grad_attn_sinks', 'grad_conv_w', 'grad_conv_b', 'grad_conv_ln_g', 'grad_conv_ln_b', 'grad_attn_out_g', 'grad_conv_out_g', 'grad_w_out', 'grad_ffn_norm_g', 'grad_w_gate', 'grad_w_up', 'grad_w_down', 'grad_final_norm_g', 'delta_meta_tokens', 'delta_attn_norm_g', 'delta_w_in', 'delta_attn_sinks', 'delta_conv_w', 'delta_conv_b', 'delta_conv_ln_g', 'delta_conv_ln_b', 'delta_attn_out_g', 'delta_conv_out_g', 'delta_w_out', 'delta_ffn_norm_g', 'delta_w_gate', 'delta_w_up', 'delta_w_down', 'delta_final_norm_g', 'new_m_meta_tokens', 'new_m_attn_norm_g', 'new_m_w_in', 'new_m_attn_sinks', 'new_m_conv_w', 'new_m_conv_b', 'new_m_conv_ln_g', 'new_m_conv_ln_b', 'new_m_attn_out_g', 'new_m_conv_out_g', 'new_m_w_out', 'new_m_ffn_norm_g', 'new_m_w_gate', 'new_m_w_up', 'new_m_w_down', 'new_m_final_norm_g', 'new_v_meta_tokens', 'new_v_attn_norm_g', 'new_v_w_in', 'new_v_attn_sinks', 'new_v_conv_w', 'new_v_conv_b', 'new_v_conv_ln_g', 'new_v_conv_ln_b', 'new_v_attn_out_g', 'new_v_conv_out_g', 'new_v_w_out', 'new_v_ffn_norm_g', 'new_v_w_gate', 'new_v_w_up', 'new_v_w_down', 'new_v_final_norm_g']
TWIN_LEAF_KINDS = {'loss': 'loss', 'grad_x': 'grad_x', 'grad_meta_tokens': 'grad_w', 'grad_attn_norm_g': 'grad_w', 'grad_w_in': 'grad_w', 'grad_attn_sinks': 'grad_w', 'grad_conv_w': 'grad_w', 'grad_conv_b': 'grad_w', 'grad_conv_ln_g': 'grad_w', 'grad_conv_ln_b': 'grad_w', 'grad_attn_out_g': 'grad_w', 'grad_conv_out_g': 'grad_w', 'grad_w_out': 'grad_w', 'grad_ffn_norm_g': 'grad_w', 'grad_w_gate': 'grad_w', 'grad_w_up': 'grad_w', 'grad_w_down': 'grad_w', 'grad_final_norm_g': 'grad_w', 'delta_meta_tokens': 'delta_w', 'delta_attn_norm_g': 'delta_w', 'delta_w_in': 'delta_w', 'delta_attn_sinks': 'delta_w', 'delta_conv_w': 'delta_w', 'delta_conv_b': 'delta_w', 'delta_conv_ln_g': 'delta_w', 'delta_conv_ln_b': 'delta_w', 'delta_attn_out_g': 'delta_w', 'delta_conv_out_g': 'delta_w', 'delta_w_out': 'delta_w', 'delta_ffn_norm_g': 'delta_w', 'delta_w_gate': 'delta_w', 'delta_w_up': 'delta_w', 'delta_w_down': 'delta_w', 'delta_final_norm_g': 'delta_w', 'new_m_meta_tokens': 'new_m', 'new_m_attn_norm_g': 'new_m', 'new_m_w_in': 'new_m', 'new_m_attn_sinks': 'new_m', 'new_m_conv_w': 'new_m', 'new_m_conv_b': 'new_m', 'new_m_conv_ln_g': 'new_m', 'new_m_conv_ln_b': 'new_m', 'new_m_attn_out_g': 'new_m', 'new_m_conv_out_g': 'new_m', 'new_m_w_out': 'new_m', 'new_m_ffn_norm_g': 'new_m', 'new_m_w_gate': 'new_m', 'new_m_w_up': 'new_m', 'new_m_w_down': 'new_m', 'new_m_final_norm_g': 'new_m', 'new_v_meta_tokens': 'new_v', 'new_v_attn_norm_g': 'new_v', 'new_v_w_in': 'new_v', 'new_v_attn_sinks': 'new_v', 'new_v_conv_w': 'new_v', 'new_v_conv_b': 'new_v', 'new_v_conv_ln_g': 'new_v', 'new_v_conv_ln_b': 'new_v', 'new_v_attn_out_g': 'new_v', 'new_v_conv_out_g': 'new_v', 'new_v_w_out': 'new_v', 'new_v_ffn_norm_g': 'new_v', 'new_v_w_gate': 'new_v', 'new_v_w_up': 'new_v', 'new_v_w_down': 'new_v', 'new_v_final_norm_g': 'new_v'}


def _forward(args):
    return _fwd_reference(*[args[k] for k in FWD_PARAMS])


def _output_shape():
    def fwd():
        inp = _fwd_setup_inputs(0)
        return _fwd_reference(*[inp[k] for k in FWD_PARAMS])
    out = _jax.eval_shape(fwd)
    return out.shape, out.dtype

N_MICROBATCH = 1
ADAM_LR = 0.001
ADAM_B1 = 0.9
ADAM_B2 = 0.999
ADAM_EPS = 1e-08
ADAM_WD = 0.01
ADAM_STEP = 10
PER_EXAMPLE_BATCH_AXIS = {'x': 0, 'loss_target': 0}
SHARED_INPUTS = []
_WEIGHT_DTYPES = {'meta_tokens': _jnp.float32, 'attn_norm_g': _jnp.float32, 'w_in': _jnp.float32, 'attn_sinks': _jnp.float32, 'conv_w': _jnp.float32, 'conv_b': _jnp.float32, 'conv_ln_g': _jnp.float32, 'conv_ln_b': _jnp.float32, 'attn_out_g': _jnp.float32, 'conv_out_g': _jnp.float32, 'w_out': _jnp.float32, 'ffn_norm_g': _jnp.float32, 'w_gate': _jnp.float32, 'w_up': _jnp.float32, 'w_down': _jnp.float32, 'final_norm_g': _jnp.float32}
MOMENT_SCALE = {'meta_tokens': 5.106234e-02, 'attn_norm_g': 2.961152e-01, 'w_in': 2.247528e-01, 'attn_sinks': 3.564025e-02, 'conv_w': 1.979777e-01, 'conv_b': 5.780026e-01, 'conv_ln_g': 2.536014e-01, 'conv_ln_b': 3.136900e-01, 'attn_out_g': 2.510504e-01, 'conv_out_g': 1.925224e-01, 'w_out': 2.341336e-01, 'ffn_norm_g': 1.450775e-01, 'w_gate': 6.172155e-02, 'w_up': 6.036336e-02, 'w_down': 1.005793e-01, 'final_norm_g': 6.462338e+01}


def _to_microbatches(a, axis):
    t = _jnp.moveaxis(a, axis, 0)
    t = t.reshape((N_MICROBATCH, t.shape[0] // N_MICROBATCH) + t.shape[1:])
    return _jnp.moveaxis(t, 1, axis + 1)


def setup_inputs(seed: int = 0) -> dict:
    inp = _fwd_setup_inputs(seed)
    key = _jax.random.fold_in(_jax.random.key(seed), 7919)
    shape, _ = _output_shape()
    out = dict(inp)
    out["loss_target"] = _jax.random.normal(_jax.random.fold_in(key, 0), shape, _jnp.float32)
    for i, name in enumerate(TWIN_WEIGHTS):
        w = inp[name].astype(_jnp.float32)
        if MOMENT_SCALE is None:
            s = _jnp.sqrt(_jnp.mean(_jnp.square(w)) + 1e-30)
        else:
            s = MOMENT_SCALE[name]
        km, kv = _jax.random.split(_jax.random.fold_in(key, i + 1))
        out[name] = w
        out["m_" + name] = s * _jax.random.normal(km, w.shape, _jnp.float32)
        out["v_" + name] = (s * s) * _jax.random.uniform(kv, w.shape, _jnp.float32, 0.5, 1.5)
    if N_MICROBATCH > 1:
        for name, axis in PER_EXAMPLE_BATCH_AXIS.items():
            out[name] = _to_microbatches(out[name], axis)
    return {'x': out['x'], 'meta_tokens': out['meta_tokens'], 'attn_norm_g': out['attn_norm_g'], 'w_in': out['w_in'], 'attn_sinks': out['attn_sinks'], 'conv_w': out['conv_w'], 'conv_b': out['conv_b'], 'conv_ln_g': out['conv_ln_g'], 'conv_ln_b': out['conv_ln_b'], 'attn_out_g': out['attn_out_g'], 'conv_out_g': out['conv_out_g'], 'w_out': out['w_out'], 'ffn_norm_g': out['ffn_norm_g'], 'w_gate': out['w_gate'], 'w_up': out['w_up'], 'w_down': out['w_down'], 'final_norm_g': out['final_norm_g'], 'loss_target': out['loss_target'], 'm_meta_tokens': out['m_meta_tokens'], 'm_attn_norm_g': out['m_attn_norm_g'], 'm_w_in': out['m_w_in'], 'm_attn_sinks': out['m_attn_sinks'], 'm_conv_w': out['m_conv_w'], 'm_conv_b': out['m_conv_b'], 'm_conv_ln_g': out['m_conv_ln_g'], 'm_conv_ln_b': out['m_conv_ln_b'], 'm_attn_out_g': out['m_attn_out_g'], 'm_conv_out_g': out['m_conv_out_g'], 'm_w_out': out['m_w_out'], 'm_ffn_norm_g': out['m_ffn_norm_g'], 'm_w_gate': out['m_w_gate'], 'm_w_up': out['m_w_up'], 'm_w_down': out['m_w_down'], 'm_final_norm_g': out['m_final_norm_g'], 'v_meta_tokens': out['v_meta_tokens'], 'v_attn_norm_g': out['v_attn_norm_g'], 'v_w_in': out['v_w_in'], 'v_attn_sinks': out['v_attn_sinks'], 'v_conv_w': out['v_conv_w'], 'v_conv_b': out['v_conv_b'], 'v_conv_ln_g': out['v_conv_ln_g'], 'v_conv_ln_b': out['v_conv_ln_b'], 'v_attn_out_g': out['v_attn_out_g'], 'v_conv_out_g': out['v_conv_out_g'], 'v_w_out': out['v_w_out'], 'v_ffn_norm_g': out['v_ffn_norm_g'], 'v_w_gate': out['v_w_gate'], 'v_w_up': out['v_w_up'], 'v_w_down': out['v_w_down'], 'v_final_norm_g': out['v_final_norm_g']}


def _loss(weights, diff, rest, loss_target):
    with _jax.named_scope("forward"):
        args = {**rest, TWIN_DIFF_INPUT: diff, **{k: w.astype(_WEIGHT_DTYPES[k]) for k, w in weights.items()}}
        y = _forward(args)
    with _jax.named_scope("loss_head"):
        err = _jnp.square(y.astype(_jnp.float32) - loss_target)
        return 0.5 * _jnp.sum(_jnp.mean(err, axis=-1)) if err.ndim else 0.5 * err


def _adamw(w, g, m, v):
    m = ADAM_B1 * m + (1.0 - ADAM_B1) * g
    v = ADAM_B2 * v + (1.0 - ADAM_B2) * _jnp.square(g)
    m_hat = m / (1.0 - ADAM_B1 ** ADAM_STEP)
    v_hat = v / (1.0 - ADAM_B2 ** ADAM_STEP)
    delta = -ADAM_LR * (m_hat / (_jnp.sqrt(v_hat) + ADAM_EPS) + ADAM_WD * w)
    return delta, m, v


def reference(x, meta_tokens, attn_norm_g, w_in, attn_sinks, conv_w, conv_b, conv_ln_g, conv_ln_b, attn_out_g, conv_out_g, w_out, ffn_norm_g, w_gate, w_up, w_down, final_norm_g, loss_target, m_meta_tokens, m_attn_norm_g, m_w_in, m_attn_sinks, m_conv_w, m_conv_b, m_conv_ln_g, m_conv_ln_b, m_attn_out_g, m_conv_out_g, m_w_out, m_ffn_norm_g, m_w_gate, m_w_up, m_w_down, m_final_norm_g, v_meta_tokens, v_attn_norm_g, v_w_in, v_attn_sinks, v_conv_w, v_conv_b, v_conv_ln_g, v_conv_ln_b, v_attn_out_g, v_conv_out_g, v_w_out, v_ffn_norm_g, v_w_gate, v_w_up, v_w_down, v_final_norm_g):
    given = dict(x=x, meta_tokens=meta_tokens, attn_norm_g=attn_norm_g, w_in=w_in, attn_sinks=attn_sinks, conv_w=conv_w, conv_b=conv_b, conv_ln_g=conv_ln_g, conv_ln_b=conv_ln_b, attn_out_g=attn_out_g, conv_out_g=conv_out_g, w_out=w_out, ffn_norm_g=ffn_norm_g, w_gate=w_gate, w_up=w_up, w_down=w_down, final_norm_g=final_norm_g, loss_target=loss_target, m_meta_tokens=m_meta_tokens, m_attn_norm_g=m_attn_norm_g, m_w_in=m_w_in, m_attn_sinks=m_attn_sinks, m_conv_w=m_conv_w, m_conv_b=m_conv_b, m_conv_ln_g=m_conv_ln_g, m_conv_ln_b=m_conv_ln_b, m_attn_out_g=m_attn_out_g, m_conv_out_g=m_conv_out_g, m_w_out=m_w_out, m_ffn_norm_g=m_ffn_norm_g, m_w_gate=m_w_gate, m_w_up=m_w_up, m_w_down=m_w_down, m_final_norm_g=m_final_norm_g, v_meta_tokens=v_meta_tokens, v_attn_norm_g=v_attn_norm_g, v_w_in=v_w_in, v_attn_sinks=v_attn_sinks, v_conv_w=v_conv_w, v_conv_b=v_conv_b, v_conv_ln_g=v_conv_ln_g, v_conv_ln_b=v_conv_ln_b, v_attn_out_g=v_attn_out_g, v_conv_out_g=v_conv_out_g, v_w_out=v_w_out, v_ffn_norm_g=v_ffn_norm_g, v_w_gate=v_w_gate, v_w_up=v_w_up, v_w_down=v_w_down, v_final_norm_g=v_final_norm_g)
    weights = {n: given[n] for n in TWIN_WEIGHTS}
    shared = {n: given[n] for n in SHARED_INPUTS}
    per_example = {n: given[n] for n in ['x']}
    grad_fn = _jax.value_and_grad(_loss, argnums=(0, 1))

    def one_microbatch(ex, loss_target):
        ex = dict(ex)
        diff = ex.pop(TWIN_DIFF_INPUT)
        return grad_fn(weights, diff, {**shared, **ex}, loss_target)

    if N_MICROBATCH == 1:
        loss, (grad_w, grad_x) = one_microbatch(per_example, given["loss_target"])
    else:
        def body(carry, xs):
            loss_sum, grad_sum = carry
            l_k, (gw_k, gx_k) = one_microbatch(xs[0], xs[1])
            with _jax.named_scope("update"):
                return (loss_sum + l_k, _jax.tree.map(_jnp.add, grad_sum, gw_k)), gx_k

        init = (_jnp.zeros((), _jnp.float32), _jax.tree.map(_jnp.zeros_like, weights))
        (loss, grad_w), grad_x = _jax.lax.scan(body, init, (per_example, given["loss_target"]))
    with _jax.named_scope("update"):
        delta_w, new_m, new_v = {}, {}, {}
        for n in TWIN_WEIGHTS:
            delta_w[n], new_m[n], new_v[n] = _adamw(weights[n], grad_w[n], given["m_" + n], given["v_" + n])
    return (loss, grad_x, *[grad_w[n] for n in TWIN_WEIGHTS], *[delta_w[n] for n in TWIN_WEIGHTS],
            *[new_m[n] for n in TWIN_WEIGHTS], *[new_v[n] for n in TWIN_WEIGHTS])
```

```python
import functools
import math

import jax
import jax.numpy as jnp
from jax import lax
from jax.experimental import pallas as pl
from jax.experimental.pallas import tpu as pltpu

F32 = jnp.float32
BF16 = jnp.bfloat16

D_MODEL = 1024
N_META = 16
ATTN_W = 512
CONV_W = 512
HEAD_DIM = 64
N_HEADS = 8
N_KV = 2
GROUP = N_HEADS // N_KV
KV_W = N_KV * HEAD_DIM
BLOCK = 128
LEAD = BLOCK - N_META
CONV_K = 31
D_FF = 2816
IN_COLS = ATTN_W + 2 * KV_W + 2 * CONV_W
Q0, KV0, C0 = 0, ATTN_W, ATTN_W + 2 * KV_W
NORM_EPS = 1e-5
SCALE = 1.0 / math.sqrt(HEAD_DIM)
SLOPES = tuple(2.0 ** (-(8.0 / N_HEADS) * (h + 1)) for h in range(N_HEADS))
NEG = -1e30

ADAM_LR, ADAM_B1, ADAM_B2, ADAM_EPS, ADAM_WD, ADAM_STEP = 0.001, 0.9, 0.999, 1e-08, 0.01, 10

N_SHARD = 4
N_DEV = 8
ROW_QUANTUM = 768
HALO = 32
CONV_CHUNK = 32
FF_CHUNK = 256
VMEM_LIMIT = 60 * 1024 * 1024


def _cparams(n_axes=1):
    return pltpu.CompilerParams(dimension_semantics=("arbitrary",) * n_axes, vmem_limit_bytes=VMEM_LIMIT)


def _dot(a, b):
    return jnp.dot(a, b, preferred_element_type=F32)


def _dot_nt(a, b):
    return lax.dot_general(a, b, (((1,), (1,)), ((), ())), preferred_element_type=F32)


def _dot_tn(a, b):
    return lax.dot_general(a, b, (((0,), (0,)), ((), ())), preferred_element_type=F32)


def _sigmoid(x):
    return 1.0 / (1.0 + jnp.exp(-x))


def _row(tm, n):
    return pl.BlockSpec((tm, n), lambda i: (i, 0))


def _const(shape):
    return pl.BlockSpec(shape, lambda i: (0,) * len(shape))


def _resident(shape):
    return pl.BlockSpec(shape, lambda i: (0,) * len(shape), pipeline_mode=pl.Buffered(1))


def _rms_fwd(x, g):
    rstd = lax.rsqrt(jnp.mean(x * x, axis=-1, keepdims=True) + NORM_EPS)
    xhat = x * rstd
    return xhat * g, xhat, rstd


def _rms_bwd(dy, xhat, rstd, g):
    dxh = dy * g
    dx = rstd * (dxh - xhat * jnp.mean(dxh * xhat, axis=-1, keepdims=True))
    return dx, dy * xhat


def _in_proj(h0, g1, w_in_b, tm):
    r = h0.shape[0]

    def body(h_ref, g_ref, w_ref, q_ref, kv_ref, c_ref):
        hn = _rms_fwd(h_ref[...], g_ref[...])[0].astype(BF16)
        q_ref[...] = _dot(hn, w_ref[:, Q0:KV0]).astype(BF16)
        kv_ref[...] = _dot(hn, w_ref[:, KV0:C0]).astype(BF16)
        c_ref[...] = _dot(hn, w_ref[:, C0:IN_COLS])

    return pl.pallas_call(
        body, name="in_proj", grid=(r // tm,),
        in_specs=[_row(tm, D_MODEL), _const((1, D_MODEL)), _const((D_MODEL, IN_COLS))],
        out_specs=[_row(tm, ATTN_W), _row(tm, 2 * KV_W), _row(tm, 2 * CONV_W)],
        out_shape=[jax.ShapeDtypeStruct((r, ATTN_W), BF16), jax.ShapeDtypeStruct((r, 2 * KV_W), BF16),
                   jax.ShapeDtypeStruct((r, 2 * CONV_W), F32)],
        compiler_params=_cparams(),
    )(h0, g1, w_in_b)


def _attn_masks(i):
    row = lax.broadcasted_iota(jnp.int32, (BLOCK, BLOCK), 0)
    col = lax.broadcasted_iota(jnp.int32, (BLOCK, BLOCK), 1)
    blk = jnp.zeros((BLOCK, BLOCK), jnp.int32) + i
    ok_meta = (col >= LEAD) & ((blk >= 1) | (col <= row))
    ok_prev = (col > row) & (blk >= 2)
    ok_cur = (col <= row) & (blk >= 1)
    ok = jnp.concatenate([ok_meta, ok_prev, ok_cur], axis=1)
    dist = jnp.concatenate([jnp.zeros((BLOCK, BLOCK), F32), (row - col + BLOCK).astype(F32), (row - col).astype(F32)], axis=1)
    return ok, dist


def _kv_cat(kvm_ref, kvp_ref, kvc_ref, g):
    ks = slice(g * HEAD_DIM, (g + 1) * HEAD_DIM)
    vs = slice(KV_W + g * HEAD_DIM, KV_W + (g + 1) * HEAD_DIM)
    kcat = jnp.concatenate([kvm_ref[:, ks], kvp_ref[:, ks], kvc_ref[:, ks]], axis=0)
    vcat = jnp.concatenate([kvm_ref[:, vs], kvp_ref[:, vs], kvc_ref[:, vs]], axis=0)
    return kcat, vcat


def _attn_fwd(q, kv, sinks):
    r = q.shape[0]
    nb = r // BLOCK

    def body(sink_ref, q_ref, kvc_ref, kvp_ref, kvm_ref, o_ref, lse_ref):
        i = pl.program_id(0)
        ok, dist = _attn_masks(i)
        lane = lax.broadcasted_iota(jnp.int32, (BLOCK, BLOCK), 1)
        lse_tile = jnp.zeros((BLOCK, BLOCK), F32)
        for g in range(N_KV):
            kcat, vcat = _kv_cat(kvm_ref, kvp_ref, kvc_ref, g)
            for j in range(GROUP):
                h = g * GROUP + j
                hs = slice(h * HEAD_DIM, (h + 1) * HEAD_DIM)
                s = _dot_nt(q_ref[:, hs], kcat) * SCALE - SLOPES[h] * dist
                s = jnp.where(ok, s, NEG)
                sink = sink_ref[h]
                m = jnp.maximum(jnp.max(s, axis=-1, keepdims=True), sink)
                p = jnp.exp(s - m)
                l = jnp.sum(p, axis=-1, keepdims=True) + jnp.exp(sink - m)
                o_ref[:, hs] = _dot((p / l).astype(BF16), vcat)
                lse_tile = jnp.where(lane == h, m + jnp.log(l), lse_tile)
        lse_ref[...] = lse_tile

    return pl.pallas_call(
        body, name="attn_fwd", grid=(nb,),
        in_specs=[pl.BlockSpec(memory_space=pltpu.SMEM), _row(BLOCK, ATTN_W), _row(BLOCK, 2 * KV_W),
                  pl.BlockSpec((BLOCK, 2 * KV_W), lambda i: (jnp.maximum(i - 1, 0), 0)), _const((BLOCK, 2 * KV_W))],
        out_specs=[_row(BLOCK, ATTN_W), _row(BLOCK, BLOCK)],
        out_shape=[jax.ShapeDtypeStruct((r, ATTN_W), F32), jax.ShapeDtypeStruct((r, BLOCK), F32)],
        compiler_params=_cparams(),
    )(sinks, q, kv, kv, kv)


def _shifted_copies(ub_ref, win):
    w = win.shape[0]
    ub_ref[0] = win
    for b in range(1, 8):
        ub_ref[b] = pltpu.roll(win, shift=w - b, axis=0)


def _conv_chunk(ub_ref, w_ref, r0, shifts):
    acc = jnp.zeros((CONV_CHUNK, CONV_W), F32)
    for j in range(CONV_K):
        a, b = divmod(shifts[j], 8)
        acc = acc + w_ref[j:j + 1, :] * ub_ref[b, pl.ds(r0 + 8 * a, CONV_CHUNK), :]
    return acc


FWD_SHIFTS = tuple(HALO - (CONV_K - 1) + j for j in range(CONV_K))
BWD_SHIFTS = tuple(CONV_K - 1 - j for j in range(CONV_K))


def _glu_window(cp_ref, c_ref, i):
    tile = c_ref[:, 0:CONV_W] * _sigmoid(c_ref[:, CONV_W:2 * CONV_W])
    halo = cp_ref[:, 0:CONV_W] * _sigmoid(cp_ref[:, CONV_W:2 * CONV_W])
    first = (jnp.zeros((HALO, CONV_W), jnp.int32) + i) == 0
    return jnp.concatenate([jnp.where(first, 0.0, halo), tile], axis=0)


def _halo_before(tm, n):
    return pl.BlockSpec((HALO, n), lambda i: (jnp.maximum(i * (tm // HALO) - 1, 0), 0))


def _conv_fwd(cacg, cw, cb, lg, lb, tm):
    r = cacg.shape[0]

    def body(c_ref, cp_ref, w_ref, cb_ref, lg_ref, lb_ref, o_ref, ub_ref):
        _shifted_copies(ub_ref, _glu_window(cp_ref, c_ref, pl.program_id(0)))

        def chunk(ci, carry):
            r0 = pl.multiple_of(ci * CONV_CHUNK, CONV_CHUNK)
            y = _conv_chunk(ub_ref, w_ref, r0, FWD_SHIFTS) + cb_ref[...]
            yc = y - jnp.mean(y, axis=-1, keepdims=True)
            rs = lax.rsqrt(jnp.mean(yc * yc, axis=-1, keepdims=True) + NORM_EPS)
            yn = yc * rs * lg_ref[...] + lb_ref[...]
            o_ref[pl.ds(r0, CONV_CHUNK), :] = yn * _sigmoid(yn)
            return carry

        lax.fori_loop(0, tm // CONV_CHUNK, chunk, 0)

    return pl.pallas_call(
        body, name="conv_fwd", grid=(r // tm,),
        in_specs=[_row(tm, 2 * CONV_W), _halo_before(tm, 2 * CONV_W), _const((32, CONV_W)), _const((1, CONV_W)),
                  _const((1, CONV_W)), _const((1, CONV_W))],
        out_specs=_row(tm, CONV_W),
        out_shape=jax.ShapeDtypeStruct((r, CONV_W), F32),
        scratch_shapes=[pltpu.VMEM((8, tm + HALO, CONV_W), F32)],
        compiler_params=_cparams(),
    )(cacg, cacg, cw, cb, lg, lb)


def _out_proj(oa, oc, h0, ga, gc, g2, w_out_b, tm):
    r = h0.shape[0]

    def body(oa_ref, oc_ref, h_ref, ga_ref, gc_ref, g2_ref, w_ref, h1_ref, hn2_ref):
        ma = _rms_fwd(oa_ref[...], ga_ref[...])[0].astype(BF16)
        mc = _rms_fwd(oc_ref[...], gc_ref[...])[0].astype(BF16)
        h1 = h_ref[...] + _dot(ma, w_ref[0:ATTN_W, :]) + _dot(mc, w_ref[ATTN_W:ATTN_W + CONV_W, :])
        h1_ref[...] = h1
        hn2_ref[...] = _rms_fwd(h1, g2_ref[...])[0].astype(BF16)

    return pl.pallas_call(
        body, name="out_proj", grid=(r // tm,),
        in_specs=[_row(tm, ATTN_W), _row(tm, CONV_W), _row(tm, D_MODEL), _const((1, ATTN_W)), _const((1, CONV_W)),
                  _const((1, D_MODEL)), _const((D_MODEL, D_MODEL))],
        out_specs=[_row(tm, D_MODEL), _row(tm, D_MODEL)],
        out_shape=[jax.ShapeDtypeStruct((r, D_MODEL), F32), jax.ShapeDtypeStruct((r, D_MODEL), BF16)],
        compiler_params=_cparams(),
    )(oa, oc, h0, ga, gc, g2, w_out_b)


def _ffn_fwd(hn2, h1, tgt, gf, wg_b, wu_b, wd_b, seq, tm):
    r = h1.shape[0]

    def body(hn_ref, h1_ref, t_ref, gf_ref, wg_ref, wu_ref, wd_ref, gate_ref, up_ref, dh2_ref, dh2b_ref, loss_ref, dgf_ref, acc_ref):
        i = pl.program_id(0)

        @pl.when(i == 0)
        def _():
            loss_ref[...] = jnp.zeros_like(loss_ref)
            dgf_ref[...] = jnp.zeros_like(dgf_ref)

        hn = hn_ref[...]
        acc_ref[...] = h1_ref[...]
        for ch in range(D_FF // FF_CHUNK):
            cs = slice(ch * FF_CHUNK, (ch + 1) * FF_CHUNK)
            gate = _dot(hn, wg_ref[:, cs])
            up = _dot(hn, wu_ref[:, cs])
            gate_ref[:, cs] = gate.astype(BF16)
            up_ref[:, cs] = up.astype(BF16)
            act = (gate * _sigmoid(gate) * up).astype(BF16)
            acc_ref[...] += _dot(act, wd_ref[cs, :])
        y, xhat, rstd = _rms_fwd(acc_ref[...], gf_ref[...])
        rows = lax.broadcasted_iota(jnp.int32, (tm, D_MODEL), 0) + i * tm
        real = (rows >= BLOCK) & (rows < BLOCK + seq)
        err = jnp.where(real, y - t_ref[...], 0.0)
        loss_ref[...] += jnp.sum(err * err) * (0.5 / D_MODEL)
        dy = err * (1.0 / D_MODEL)
        dh2, dg_rows = _rms_bwd(dy, xhat, rstd, gf_ref[...])
        dgf_ref[...] += jnp.sum(dg_rows, axis=0, keepdims=True)
        dh2_ref[...] = dh2
        dh2b_ref[...] = dh2.astype(BF16)

    return pl.pallas_call(
        body, name="ffn_fwd", grid=(r // tm,),
        in_specs=[_row(tm, D_MODEL), _row(tm, D_MODEL), _row(tm, D_MODEL), _const((1, D_MODEL)),
                  _resident((D_MODEL, D_FF)), _resident((D_MODEL, D_FF)), _resident((D_FF, D_MODEL))],
        out_specs=[_row(tm, D_FF), _row(tm, D_FF), _row(tm, D_MODEL), _row(tm, D_MODEL), _const((1, BLOCK)), _const((1, D_MODEL))],
        out_shape=[jax.ShapeDtypeStruct((r, D_FF), BF16), jax.ShapeDtypeStruct((r, D_FF), BF16),
                   jax.ShapeDtypeStruct((r, D_MODEL), F32), jax.ShapeDtypeStruct((r, D_MODEL), BF16),
                   jax.ShapeDtypeStruct((1, BLOCK), F32), jax.ShapeDtypeStruct((1, D_MODEL), F32)],
        scratch_shapes=[pltpu.VMEM((tm, D_MODEL), F32)],
        compiler_params=_cparams(),
    )(hn2, h1, tgt, gf, wg_b, wu_b, wd_b)


def _ffn_bwd(dh2, dh2b, gate, up, h1, g2, wg_b, wu_b, wd_b, tm):
    r = h1.shape[0]

    def body(dh2_ref, dh2b_ref, gate_ref, up_ref, h1_ref, g2_ref, wg_ref, wu_ref, wd_ref, dgate_ref, dup_ref, dh1_ref, dg2_ref, acc_ref):
        @pl.when(pl.program_id(0) == 0)
        def _():
            dg2_ref[...] = jnp.zeros_like(dg2_ref)

        dyb = dh2b_ref[...]
        acc_ref[...] = jnp.zeros_like(acc_ref)
        for ch in range(D_FF // FF_CHUNK):
            cs = slice(ch * FF_CHUNK, (ch + 1) * FF_CHUNK)
            dact = _dot_nt(dyb, wd_ref[cs, :])
            gate = gate_ref[:, cs].astype(F32)
            up = up_ref[:, cs].astype(F32)
            sg = _sigmoid(gate)
            silu = gate * sg
            dgate = (dact * up * (sg * (1.0 + gate * (1.0 - sg)))).astype(BF16)
            dup = (dact * silu).astype(BF16)
            dgate_ref[:, cs] = dgate
            dup_ref[:, cs] = dup
            acc_ref[...] += _dot_nt(dgate, wg_ref[:, cs]) + _dot_nt(dup, wu_ref[:, cs])
        _, xhat, rstd = _rms_fwd(h1_ref[...], g2_ref[...])
        dx, dg_rows = _rms_bwd(acc_ref[...], xhat, rstd, g2_ref[...])
        dg2_ref[...] += jnp.sum(dg_rows, axis=0, keepdims=True)
        dh1_ref[...] = dh2_ref[...] + dx

    return pl.pallas_call(
        body, name="ffn_bwd", grid=(r // tm,),
        in_specs=[_row(tm, D_MODEL), _row(tm, D_MODEL), _row(tm, D_FF), _row(tm, D_FF), _row(tm, D_MODEL), _const((1, D_MODEL)),
                  _resident((D_MODEL, D_FF)), _resident((D_MODEL, D_FF)), _resident((D_FF, D_MODEL))],
        out_specs=[_row(tm, D_FF), _row(tm, D_FF), _row(tm, D_MODEL), _const((1, D_MODEL))],
        out_shape=[jax.ShapeDtypeStruct((r, D_FF), BF16), jax.ShapeDtypeStruct((r, D_FF), BF16),
                   jax.ShapeDtypeStruct((r, D_MODEL), F32), jax.ShapeDtypeStruct((1, D_MODEL), F32)],
        scratch_shapes=[pltpu.VMEM((tm, D_MODEL), F32)],
        compiler_params=_cparams(),
    )(dh2, dh2b, gate, up, h1, g2, wg_b, wu_b, wd_b)


FF_HALF = D_FF // 2


def _ffn_wgrad_gu(hn2, dgate, dup, tk):
    r = hn2.shape[0]

    def body(hn_ref, dg_ref, du_ref, wg_ref, wu_ref):
        @pl.when(pl.program_id(1) == 0)
        def _():
            wg_ref[...] = jnp.zeros_like(wg_ref)
            wu_ref[...] = jnp.zeros_like(wu_ref)

        hn = hn_ref[...]
        wg_ref[...] += _dot_tn(hn, dg_ref[...])
        wu_ref[...] += _dot_tn(hn, du_ref[...])

    col = pl.BlockSpec((tk, FF_HALF), lambda j, k: (k, j))
    out = pl.BlockSpec((D_MODEL, FF_HALF), lambda j, k: (0, j))
    return pl.pallas_call(
        body, name="ffn_wgrad_gu", grid=(2, r // tk),
        in_specs=[pl.BlockSpec((tk, D_MODEL), lambda j, k: (k, 0)), col, col],
        out_specs=[out, out],
        out_shape=[jax.ShapeDtypeStruct((D_MODEL, D_FF), F32)] * 2,
        compiler_params=_cparams(2),
    )(hn2, dgate, dup)


def _ffn_wgrad_d(gate, up, dh2b, tk):
    r = gate.shape[0]

    def body(g_ref, u_ref, dy_ref, wd_ref):
        @pl.when(pl.program_id(1) == 0)
        def _():
            wd_ref[...] = jnp.zeros_like(wd_ref)

        gate = g_ref[...].astype(F32)
        act = (gate * _sigmoid(gate) * u_ref[...].astype(F32)).astype(BF16)
        wd_ref[...] += _dot_tn(act, dy_ref[...])

    col = pl.BlockSpec((tk, FF_HALF), lambda j, k: (k, j))
    return pl.pallas_call(
        body, name="ffn_wgrad_d", grid=(2, r // tk),
        in_specs=[col, col, pl.BlockSpec((tk, D_MODEL), lambda j, k: (k, 0))],
        out_specs=pl.BlockSpec((FF_HALF, D_MODEL), lambda j, k: (j, 0)),
        out_shape=jax.ShapeDtypeStruct((D_FF, D_MODEL), F32),
        compiler_params=_cparams(2),
    )(gate, up, dh2b)


def _out_proj_bwd(dh1, oa, oc, ga, gc, w_out_b, tm):
    r = dh1.shape[0]

    def body(dh_ref, oa_ref, oc_ref, ga_ref, gc_ref, w_ref, doa_ref, doc_ref, dw_ref, dga_ref, dgc_ref):
        @pl.when(pl.program_id(0) == 0)
        def _():
            dw_ref[...] = jnp.zeros_like(dw_ref)
            dga_ref[...] = jnp.zeros_like(dga_ref)
            dgc_ref[...] = jnp.zeros_like(dgc_ref)

        dhb = dh_ref[...].astype(BF16)
        dmix = _dot_nt(dhb, w_ref[...])
        ma, xa, ra = _rms_fwd(oa_ref[...], ga_ref[...])
        mc, xc, rc = _rms_fwd(oc_ref[...], gc_ref[...])
        dw_ref[0:ATTN_W, :] += _dot_tn(ma.astype(BF16), dhb)
        dw_ref[ATTN_W:ATTN_W + CONV_W, :] += _dot_tn(mc.astype(BF16), dhb)
        doa, dga_rows = _rms_bwd(dmix[:, 0:ATTN_W], xa, ra, ga_ref[...])
        doc, dgc_rows = _rms_bwd(dmix[:, ATTN_W:ATTN_W + CONV_W], xc, rc, gc_ref[...])
        doa_ref[...] = doa
        doc_ref[...] = doc
        dga_ref[...] += jnp.sum(dga_rows, axis=0, keepdims=True)
        dgc_ref[...] += jnp.sum(dgc_rows, axis=0, keepdims=True)

    return pl.pallas_call(
        body, name="out_proj_bwd", grid=(r // tm,),
        in_specs=[_row(tm, D_MODEL), _row(tm, ATTN_W), _row(tm, CONV_W), _const((1, ATTN_W)), _const((1, CONV_W)),
                  _const((D_MODEL, D_MODEL))],
        out_specs=[_row(tm, ATTN_W), _row(tm, CONV_W), _const((D_MODEL, D_MODEL)), _const((1, ATTN_W)), _const((1, CONV_W))],
        out_shape=[jax.ShapeDtypeStruct((r, ATTN_W), F32), jax.ShapeDtypeStruct((r, CONV_W), F32),
                   jax.ShapeDtypeStruct((D_MODEL, D_MODEL), F32), jax.ShapeDtypeStruct((1, ATTN_W), F32),
                   jax.ShapeDtypeStruct((1, CONV_W), F32)],
        compiler_params=_cparams(),
    )(dh1, oa, oc, ga, gc, w_out_b)


def _conv_bwd_params(doc, cacg, cw, cb, lg, lb, tm):
    r = cacg.shape[0]
    n_steps = r // tm

    def body(do_ref, c_ref, cp_ref, w_ref, cb_ref, lg_ref, lb_ref, dy_ref, dcw_ref, dcb_ref, dlg_ref, dlb_ref, ub_ref, accw_ref):
        i = pl.program_id(0)

        @pl.when(i == 0)
        def _():
            accw_ref[...] = jnp.zeros_like(accw_ref)
            dcb_ref[...] = jnp.zeros_like(dcb_ref)
            dlg_ref[...] = jnp.zeros_like(dlg_ref)
            dlb_ref[...] = jnp.zeros_like(dlb_ref)

        _shifted_copies(ub_ref, _glu_window(cp_ref, c_ref, i))

        def chunk(ci, carry):
            r0 = pl.multiple_of(ci * CONV_CHUNK, CONV_CHUNK)
            y = _conv_chunk(ub_ref, w_ref, r0, FWD_SHIFTS) + cb_ref[...]
            yc = y - jnp.mean(y, axis=-1, keepdims=True)
            rs = lax.rsqrt(jnp.mean(yc * yc, axis=-1, keepdims=True) + NORM_EPS)
            xhat = yc * rs
            yn = xhat * lg_ref[...] + lb_ref[...]
            sg = _sigmoid(yn)
            dyn = do_ref[pl.ds(r0, CONV_CHUNK), :] * (sg * (1.0 + yn * (1.0 - sg)))
            dlg_ref[...] += jnp.sum(dyn * xhat, axis=0, keepdims=True)
            dlb_ref[...] += jnp.sum(dyn, axis=0, keepdims=True)
            dxh = dyn * lg_ref[...]
            dy = rs * (dxh - jnp.mean(dxh, axis=-1, keepdims=True) - xhat * jnp.mean(dxh * xhat, axis=-1, keepdims=True))
            dcb_ref[...] += jnp.sum(dy, axis=0, keepdims=True)
            dy_ref[pl.ds(r0, CONV_CHUNK), :] = dy
            for j in range(CONV_K):
                a, b = divmod(FWD_SHIFTS[j], 8)
                prod = dy * ub_ref[b, pl.ds(r0 + 8 * a, CONV_CHUNK), :]
                accw_ref[j] += jnp.sum(prod.reshape(CONV_CHUNK // 8, 8, CONV_W), axis=0)
            return carry

        lax.fori_loop(0, tm // CONV_CHUNK, chunk, 0)

        @pl.when(i == n_steps - 1)
        def _():
            for j in range(32):
                dcw_ref[j:j + 1, :] = jnp.sum(accw_ref[j], axis=0, keepdims=True)

    vec = _const((1, CONV_W))
    return pl.pallas_call(
        body, name="conv_bwd_params", grid=(n_steps,),
        in_specs=[_row(tm, CONV_W), _row(tm, 2 * CONV_W), _halo_before(tm, 2 * CONV_W), _const((32, CONV_W)), vec, vec, vec],
        out_specs=[_row(tm, CONV_W), _const((32, CONV_W)), vec, vec, vec],
        out_shape=[jax.ShapeDtypeStruct((r, CONV_W), F32), jax.ShapeDtypeStruct((32, CONV_W), F32)]
        + [jax.ShapeDtypeStruct((1, CONV_W), F32)] * 3,
        scratch_shapes=[pltpu.VMEM((8, tm + HALO, CONV_W), F32), pltpu.VMEM((32, 8, CONV_W), F32)],
        compiler_params=_cparams(),
    )(doc, cacg, cacg, cw, cb, lg, lb)


def _conv_bwd_data(dy, cacg, cw, tm):
    r = cacg.shape[0]
    n_steps = r // tm

    def body(dy_ref, dyn_ref, c_ref, w_ref, dc_ref, ub_ref):
        last = (jnp.zeros((HALO, CONV_W), jnp.int32) + pl.program_id(0)) == n_steps - 1
        win = jnp.concatenate([dy_ref[...], jnp.where(last, 0.0, dyn_ref[...])], axis=0)
        _shifted_copies(ub_ref, win)

        def chunk(ci, carry):
            r0 = pl.multiple_of(ci * CONV_CHUNK, CONV_CHUNK)
            du = _conv_chunk(ub_ref, w_ref, r0, BWD_SHIFTS)
            ca = c_ref[pl.ds(r0, CONV_CHUNK), 0:CONV_W]
            sg = _sigmoid(c_ref[pl.ds(r0, CONV_CHUNK), CONV_W:2 * CONV_W])
            dc_ref[pl.ds(r0, CONV_CHUNK), 0:CONV_W] = (du * sg).astype(BF16)
            dc_ref[pl.ds(r0, CONV_CHUNK), CONV_W:2 * CONV_W] = (du * ca * sg * (1.0 - sg)).astype(BF16)
            return carry

        lax.fori_loop(0, tm // CONV_CHUNK, chunk, 0)

    halo_after = pl.BlockSpec((HALO, CONV_W), lambda i: (jnp.minimum((i + 1) * (tm // HALO), r // HALO - 1), 0))
    return pl.pallas_call(
        body, name="conv_bwd_data", grid=(n_steps,),
        in_specs=[_row(tm, CONV_W), halo_after, _row(tm, 2 * CONV_W), _const((32, CONV_W))],
        out_specs=_row(tm, 2 * CONV_W),
        out_shape=jax.ShapeDtypeStruct((r, 2 * CONV_W), BF16),
        scratch_shapes=[pltpu.VMEM((8, tm + HALO, CONV_W), F32)],
        compiler_params=_cparams(),
    )(dy, dy, cacg, cw)


def _attn_bwd(q, kv, o, do, lse, sinks):
    r = q.shape[0]
    nb = r // BLOCK

    def body(sink_ref, q_ref, kvc_ref, kvp_ref, kvm_ref, o_ref, do_ref, lse_ref, dq_ref, dkv_ref, dmeta_ref, dsink_ref, hold_ref):
        i = pl.program_id(0)

        @pl.when(i == 0)
        def _():
            dmeta_ref[...] = jnp.zeros_like(dmeta_ref)
            dsink_ref[...] = jnp.zeros_like(dsink_ref)
            hold_ref[...] = jnp.zeros_like(hold_ref)

        @pl.when(i < nb)
        def _():
            ok, dist = _attn_masks(i)
            lane = lax.broadcasted_iota(jnp.int32, (BLOCK, BLOCK), 1)
            lse_tile = lse_ref[...]
            for g in range(N_KV):
                kcat, vcat = _kv_cat(kvm_ref, kvp_ref, kvc_ref, g)
                dk = jnp.zeros((3 * BLOCK, HEAD_DIM), F32)
                dv = jnp.zeros((3 * BLOCK, HEAD_DIM), F32)
                for j in range(GROUP):
                    h = g * GROUP + j
                    hs = slice(h * HEAD_DIM, (h + 1) * HEAD_DIM)
                    qh = q_ref[:, hs]
                    doh = do_ref[:, hs]
                    dohb = doh.astype(BF16)
                    lse = jnp.sum(jnp.where(lane == h, lse_tile, 0.0), axis=-1, keepdims=True)
                    s = _dot_nt(qh, kcat) * SCALE - SLOPES[h] * dist
                    p = jnp.exp(jnp.where(ok, s, NEG) - lse)
                    delta = jnp.sum(doh * o_ref[:, hs], axis=-1, keepdims=True)
                    dsink_ref[h:h + 1, :] += -jnp.sum(jnp.exp(sink_ref[h] - lse) * delta)
                    ds = ((p * (_dot_nt(dohb, vcat) - delta)) * SCALE).astype(BF16)
                    dq_ref[:, hs] = _dot(ds, kcat).astype(BF16)
                    dk = dk + _dot_tn(ds, qh)
                    dv = dv + _dot_tn(p.astype(BF16), dohb)
                ks = slice(g * HEAD_DIM, (g + 1) * HEAD_DIM)
                vs = slice(KV_W + g * HEAD_DIM, KV_W + (g + 1) * HEAD_DIM)
                for sl, grad in ((ks, dk), (vs, dv)):
                    dmeta_ref[:, sl] += grad[0:BLOCK]
                    dkv_ref[:, sl] = hold_ref[:, sl] + grad[BLOCK:2 * BLOCK]
                    hold_ref[:, sl] = grad[2 * BLOCK:3 * BLOCK]

        @pl.when(i == nb)
        def _():
            dkv_ref[...] = hold_ref[...]

    def cur(i):
        return jnp.minimum(i, nb - 1)

    return pl.pallas_call(
        body, name="attn_bwd", grid=(nb + 1,),
        in_specs=[pl.BlockSpec(memory_space=pltpu.SMEM),
                  pl.BlockSpec((BLOCK, ATTN_W), lambda i: (cur(i), 0)),
                  pl.BlockSpec((BLOCK, 2 * KV_W), lambda i: (cur(i), 0)),
                  pl.BlockSpec((BLOCK, 2 * KV_W), lambda i: (jnp.maximum(cur(i) - 1, 0), 0)),
                  _const((BLOCK, 2 * KV_W)),
                  pl.BlockSpec((BLOCK, ATTN_W), lambda i: (cur(i), 0)),
                  pl.BlockSpec((BLOCK, ATTN_W), lambda i: (cur(i), 0)),
                  pl.BlockSpec((BLOCK, BLOCK), lambda i: (cur(i), 0))],
        out_specs=[pl.BlockSpec((BLOCK, ATTN_W), lambda i: (cur(i), 0)),
                   pl.BlockSpec((BLOCK, 2 * KV_W), lambda i: (jnp.maximum(i - 1, 0), 0)),
                   _const((BLOCK, 2 * KV_W)), _const((N_HEADS, BLOCK))],
        out_shape=[jax.ShapeDtypeStruct((r, ATTN_W), BF16), jax.ShapeDtypeStruct((r, 2 * KV_W), F32),
                   jax.ShapeDtypeStruct((BLOCK, 2 * KV_W), F32), jax.ShapeDtypeStruct((N_HEADS, BLOCK), F32)],
        scratch_shapes=[pltpu.VMEM((BLOCK, 2 * KV_W), F32)],
        compiler_params=_cparams(),
    )(sinks, q, kv, kv, kv, o, do, lse)


def _in_proj_bwd(dq, dkv, dkv_meta, dc, dh1, h0, g1, w_in_b, tm):
    r = h0.shape[0]

    def body(dq_ref, dkv_ref, dm_ref, dc_ref, dh1_ref, h_ref, g_ref, w_ref, dh0_ref, dw_ref, dg_ref):
        i = pl.program_id(0)

        @pl.when(i == 0)
        def _():
            dw_ref[...] = jnp.zeros_like(dw_ref)
            dg_ref[...] = jnp.zeros_like(dg_ref)

        meta = jnp.concatenate([dm_ref[...], jnp.zeros((tm - BLOCK, 2 * KV_W), F32)], axis=0) if tm > BLOCK else dm_ref[...]
        first = (jnp.zeros((tm, 2 * KV_W), jnp.int32) + i) == 0
        dkvb = (dkv_ref[...] + jnp.where(first, meta, 0.0)).astype(BF16)
        dqb = dq_ref[...]
        dcb = dc_ref[...]
        hn, xhat, rstd = _rms_fwd(h_ref[...], g_ref[...])
        hnb = hn.astype(BF16)
        dhn = _dot_nt(dqb, w_ref[:, Q0:KV0]) + _dot_nt(dkvb, w_ref[:, KV0:C0]) + _dot_nt(dcb, w_ref[:, C0:IN_COLS])
        dw_ref[:, Q0:KV0] += _dot_tn(hnb, dqb)
        dw_ref[:, KV0:C0] += _dot_tn(hnb, dkvb)
        dw_ref[:, C0:IN_COLS] += _dot_tn(hnb, dcb)
        dx, dg_rows = _rms_bwd(dhn, xhat, rstd, g_ref[...])
        dg_ref[...] += jnp.sum(dg_rows, axis=0, keepdims=True)
        dh0_ref[...] = dh1_ref[...] + dx

    return pl.pallas_call(
        body, name="in_proj_bwd", grid=(r // tm,),
        in_specs=[_row(tm, ATTN_W), _row(tm, 2 * KV_W), _const((BLOCK, 2 * KV_W)), _row(tm, 2 * CONV_W), _row(tm, D_MODEL),
                  _row(tm, D_MODEL), _const((1, D_MODEL)), _const((D_MODEL, IN_COLS))],
        out_specs=[_row(tm, D_MODEL), _const((D_MODEL, IN_COLS)), _const((1, D_MODEL))],
        out_shape=[jax.ShapeDtypeStruct((r, D_MODEL), F32), jax.ShapeDtypeStruct((D_MODEL, IN_COLS), F32),
                   jax.ShapeDtypeStruct((1, D_MODEL), F32)],
        compiler_params=_cparams(),
    )(dq, dkv, dkv_meta, dc, dh1, h0, g1, w_in_b)


def _adamw(w, g, m, v, name):
    rows, cols = w.shape
    tr = rows
    for cand in (256, 176, 128, 64, 32, 16, 8):
        if rows % cand == 0:
            tr = cand
            break

    def body(w_ref, g_ref, m_ref, v_ref, d_ref, nm_ref, nv_ref):
        g = g_ref[...]
        m = ADAM_B1 * m_ref[...] + (1.0 - ADAM_B1) * g
        v = ADAM_B2 * v_ref[...] + (1.0 - ADAM_B2) * (g * g)
        m_hat = m / (1.0 - ADAM_B1 ** ADAM_STEP)
        v_hat = v / (1.0 - ADAM_B2 ** ADAM_STEP)
        d_ref[...] = -ADAM_LR * (m_hat / (jnp.sqrt(v_hat) + ADAM_EPS) + ADAM_WD * w_ref[...])
        nm_ref[...] = m
        nv_ref[...] = v

    spec = _row(tr, cols)
    return pl.pallas_call(
        body, name=name, grid=(rows // tr,), in_specs=[spec] * 4, out_specs=[spec] * 3,
        out_shape=[jax.ShapeDtypeStruct((rows, cols), F32)] * 3, compiler_params=_cparams(),
    )(w, g, m, v)


MESH = pl.DeviceIdType.MESH
ANY = pl.BlockSpec(memory_space=pl.ANY)


def _place():
    x, y, c = lax.axis_index("x"), lax.axis_index("y"), lax.axis_index("c")
    chips = [(1 - x, y), (x, 1 - y), (1 - x, 1 - y)]
    return x, y, c, chips


def _gather_weights(shards):
    n = len(shards)

    def body(*refs):
        src, dst = refs[:n], refs[n:2 * n]
        send_sems, recv_sems, local_sems = refs[2 * n:]
        x, y, c, chips = _place()
        mine = 2 * x + y

        def half(k):
            rows = shards[k].shape[0] // 2
            return pl.ds(c * rows, rows)

        def other_half(k):
            rows = shards[k].shape[0] // 2
            return pl.ds((1 - c) * rows, rows)

        local = [pltpu.make_async_copy(src[k], dst[k].at[mine], local_sems.at[k]) for k in range(n)]
        for cp in local:
            cp.start()

        def ici(k, p, chip):
            return pltpu.make_async_remote_copy(
                src_ref=src[k].at[half(k)], dst_ref=dst[k].at[mine, half(k)], send_sem=send_sems.at[k * 3 + p],
                recv_sem=recv_sems.at[k * 3 + p], device_id=(chip[0], chip[1], c), device_id_type=MESH)

        def landed(k, p, chip):
            blk = dst[k].at[2 * chip[0] + chip[1], half(k)]
            return pltpu.make_async_remote_copy(
                src_ref=blk, dst_ref=blk, send_sem=send_sems.at[k * 3 + p], recv_sem=recv_sems.at[k * 3 + p],
                device_id=(chip[0], chip[1], c), device_id_type=MESH)

        def d2d(k, p, chip):
            blk = dst[k].at[2 * chip[0] + chip[1], half(k)]
            return pltpu.make_async_remote_copy(
                src_ref=blk, dst_ref=blk, send_sem=send_sems.at[3 * n + k * 3 + p], recv_sem=recv_sems.at[3 * n + k * 3 + p],
                device_id=(x, y, 1 - c), device_id_type=MESH)

        def from_sibling(k, p, chip):
            blk = dst[k].at[2 * chip[0] + chip[1], other_half(k)]
            return pltpu.make_async_remote_copy(
                src_ref=blk, dst_ref=blk, send_sem=send_sems.at[3 * n + k * 3 + p], recv_sem=recv_sems.at[3 * n + k * 3 + p],
                device_id=(x, y, 1 - c), device_id_type=MESH)

        sent = [ici(k, p, chip) for k in range(n) for p, chip in enumerate(chips)]
        for cp in sent:
            cp.start()
        for k in range(n):
            for p, chip in enumerate(chips):
                landed(k, p, chip).wait_recv()
                fwd = d2d(k, p, chip)
                fwd.start()
                sent.append(fwd)
        for k in range(n):
            for p, chip in enumerate(chips):
                from_sibling(k, p, chip).wait_recv()
        for cp in sent:
            cp.wait_send()
        for cp in local:
            cp.wait()

    return pl.pallas_call(
        body, name="gather_weights",
        in_specs=[ANY] * n, out_specs=[ANY] * n,
        out_shape=[jax.ShapeDtypeStruct((N_SHARD,) + s.shape, s.dtype) for s in shards],
        scratch_shapes=[pltpu.SemaphoreType.DMA((6 * n,)), pltpu.SemaphoreType.DMA((6 * n,)), pltpu.SemaphoreType.DMA((n,))],
    )(*shards)


def _allreduce_small(part):
    rows = part.shape[0]

    def body(x_ref, o_ref, slots, send_sems, recv_sems):
        x, y, c = lax.axis_index("x"), lax.axis_index("y"), lax.axis_index("c")
        me = 4 * x + 2 * y + c
        slots[me] = x_ref[...]

        def copy(k):
            peer = ((x + (k >> 2)) % 2, (y + ((k >> 1) & 1)) % 2, (c + (k & 1)) % 2)
            return pltpu.make_async_remote_copy(
                src_ref=x_ref, dst_ref=slots.at[me], send_sem=send_sems.at[k], recv_sem=recv_sems.at[k],
                device_id=peer, device_id_type=MESH)

        def arrival(k):
            peer = ((x + (k >> 2)) % 2, (y + ((k >> 1) & 1)) % 2, (c + (k & 1)) % 2)
            return pltpu.make_async_remote_copy(
                src_ref=x_ref, dst_ref=slots.at[4 * peer[0] + 2 * peer[1] + peer[2]], send_sem=send_sems.at[k],
                recv_sem=recv_sems.at[k], device_id=peer, device_id_type=MESH)

        for k in range(1, N_DEV):
            copy(k).start()
        for k in range(1, N_DEV):
            arrival(k).wait_recv()
        total = slots[0]
        for d in range(1, N_DEV):
            total = total + slots[d]
        o_ref[...] = total
        for k in range(1, N_DEV):
            copy(k).wait_send()

    vmem = pl.BlockSpec(memory_space=pltpu.VMEM)
    return pl.pallas_call(
        body, name="allreduce_small", in_specs=[vmem], out_specs=vmem,
        out_shape=jax.ShapeDtypeStruct((rows, 128), F32),
        scratch_shapes=[pltpu.VMEM((N_DEV, rows, 128), F32), pltpu.SemaphoreType.DMA((N_DEV,)), pltpu.SemaphoreType.DMA((N_DEV,))],
    )(part)


def _swap_halves(parts):
    n = len(parts)

    def body(*refs):
        src, dst = refs[:n], refs[n:2 * n]
        send_sems, recv_sems = refs[2 * n:]
        x, y, c, _ = _place()
        copies = []
        for k in range(n):
            rows = parts[k].shape[1] // 2
            copies.append(pltpu.make_async_remote_copy(
                src_ref=src[k].at[:, pl.ds((1 - c) * rows, rows)], dst_ref=dst[k], send_sem=send_sems.at[k],
                recv_sem=recv_sems.at[k], device_id=(x, y, 1 - c), device_id_type=MESH))
        for cp in copies:
            cp.start()
        for cp in copies:
            cp.wait()

    return pl.pallas_call(
        body, name="swap_halves", in_specs=[ANY] * n, out_specs=[ANY] * n,
        out_shape=[jax.ShapeDtypeStruct((N_SHARD, p.shape[1] // 2, p.shape[2]), F32) for p in parts],
        scratch_shapes=[pltpu.SemaphoreType.DMA((n,)), pltpu.SemaphoreType.DMA((n,))],
    )(*parts)


def _chip_sums(parts, recvs, c):
    n = len(parts)

    def body(c_ref, *refs):
        for k in range(n):
            refs[2 * n + k][...] = (refs[k][...] + refs[n + k][...]).astype(BF16)

    in_specs, out_specs, out_shape = [], [], []
    for p in parts:
        rows, cols = p.shape[1] // 2, p.shape[2]
        in_specs.append(pl.BlockSpec((None, rows, cols), lambda s, c_ref: (s, c_ref[0], 0)))
    for p in parts:
        rows, cols = p.shape[1] // 2, p.shape[2]
        in_specs.append(pl.BlockSpec((None, rows, cols), lambda s, c_ref: (s, 0, 0)))
        out_specs.append(pl.BlockSpec((None, rows, cols), lambda s, c_ref: (s, 0, 0)))
        out_shape.append(jax.ShapeDtypeStruct((N_SHARD, rows, cols), BF16))
    return pl.pallas_call(
        body, name="chip_sums",
        grid_spec=pltpu.PrefetchScalarGridSpec(num_scalar_prefetch=1, grid=(N_SHARD,), in_specs=in_specs, out_specs=out_specs),
        out_shape=out_shape, compiler_params=_cparams(),
    )(c, *parts, *recvs)


def _exchange_chip_sums(sums):
    n = len(sums)

    def body(*refs):
        src, dst = refs[:n], refs[n:2 * n]
        send_sems, recv_sems = refs[2 * n:]
        x, y, c, chips = _place()
        copies = []
        for k in range(n):
            for p, chip in enumerate(chips):
                copies.append(pltpu.make_async_remote_copy(
                    src_ref=src[k].at[2 * chip[0] + chip[1]], dst_ref=dst[k].at[p], send_sem=send_sems.at[k * 3 + p],
                    recv_sem=recv_sems.at[k * 3 + p], device_id=(chip[0], chip[1], c), device_id_type=MESH))
        for cp in copies:
            cp.start()
        for cp in copies:
            cp.wait()

    return pl.pallas_call(
        body, name="exchange_chip_sums", in_specs=[ANY] * n, out_specs=[ANY] * n,
        out_shape=[jax.ShapeDtypeStruct((3,) + s.shape[1:], BF16) for s in sums],
        scratch_shapes=[pltpu.SemaphoreType.DMA((3 * n,)), pltpu.SemaphoreType.DMA((3 * n,))],
    )(*sums)


def _total_halves(parts, recvs, others, sc):
    n = len(parts)

    def body(sc_ref, *refs):
        for k in range(n):
            o = refs[2 * n + k]
            refs[3 * n + k][...] = ((refs[k][...] + refs[n + k][...]) + o[0].astype(F32)) + (o[1].astype(F32) + o[2].astype(F32))

    in_specs, out_specs, out_shape = [], [], []
    for p in parts:
        rows, cols = p.shape[1] // 4, p.shape[2]
        in_specs.append(pl.BlockSpec((None, rows, cols), lambda i, sc_ref: (sc_ref[0], 2 * sc_ref[1] + i, 0)))
    for p in parts:
        rows, cols = p.shape[1] // 4, p.shape[2]
        in_specs.append(pl.BlockSpec((None, rows, cols), lambda i, sc_ref: (sc_ref[0], i, 0)))
    for p in parts:
        rows, cols = p.shape[1] // 4, p.shape[2]
        in_specs.append(pl.BlockSpec((3, rows, cols), lambda i, sc_ref: (0, i, 0)))
        out_specs.append(pl.BlockSpec((rows, cols), lambda i, sc_ref: (i, 0)))
        out_shape.append(jax.ShapeDtypeStruct((2 * rows, cols), F32))
    return pl.pallas_call(
        body, name="total_halves",
        grid_spec=pltpu.PrefetchScalarGridSpec(num_scalar_prefetch=1, grid=(2,), in_specs=in_specs, out_specs=out_specs),
        out_shape=out_shape, compiler_params=_cparams(),
    )(sc, *parts, *recvs, *others)


def _join_halves(halves):
    n = len(halves)

    def body(*refs):
        src, dst = refs[:n], refs[n:2 * n]
        send_sems, recv_sems, local_sems = refs[2 * n:]
        x, y, c, _ = _place()
        local, remote = [], []
        for k in range(n):
            rows = halves[k].shape[0]
            mine = dst[k].at[pl.ds(c * rows, rows)]
            local.append(pltpu.make_async_copy(src[k], mine, local_sems.at[k]))
            remote.append(pltpu.make_async_remote_copy(
                src_ref=src[k], dst_ref=mine, send_sem=send_sems.at[k], recv_sem=recv_sems.at[k],
                device_id=(x, y, 1 - c), device_id_type=MESH))
        for cp in local + remote:
            cp.start()
        for k in range(n):
            rows = halves[k].shape[0]
            theirs = dst[k].at[pl.ds((1 - c) * rows, rows)]
            pltpu.make_async_remote_copy(
                src_ref=src[k], dst_ref=theirs, send_sem=send_sems.at[k], recv_sem=recv_sems.at[k],
                device_id=(x, y, 1 - c), device_id_type=MESH).wait_recv()
        for cp in remote:
            cp.wait_send()
        for cp in local:
            cp.wait()

    return pl.pallas_call(
        body, name="join_halves", in_specs=[ANY] * n, out_specs=[ANY] * n,
        out_shape=[jax.ShapeDtypeStruct((2 * h.shape[0], h.shape[1]), F32) for h in halves],
        scratch_shapes=[pltpu.SemaphoreType.DMA((n,)), pltpu.SemaphoreType.DMA((n,)), pltpu.SemaphoreType.DMA((n,))],
    )(*halves)


def _reduce_scatter(parts):
    c = lax.axis_index("c").astype(jnp.int32)
    s = (2 * lax.axis_index("x") + lax.axis_index("y")).astype(jnp.int32)
    recvs = _swap_halves(parts)
    sums = _chip_sums(parts, recvs, jnp.stack([c]))
    others = _exchange_chip_sums(sums)
    halves = _total_halves(parts, recvs, others, jnp.stack([s, c]))
    return _join_halves(halves)


def _col_shards(w):
    rows, cols = w.shape
    return w.reshape(rows, N_SHARD, cols // N_SHARD).transpose(1, 0, 2)


def _from_col_shards(g):
    return g.transpose(1, 0, 2).reshape(g.shape[1], -1)


SMALL_LAYOUT = (
    ("loss", 1), ("final_norm_g", 8), ("attn_norm_g", 8), ("ffn_norm_g", 8), ("attn_out_g", 4), ("conv_out_g", 4),
    ("conv_b", 4), ("conv_ln_g", 4), ("conv_ln_b", 4), ("attn_sinks", 1), ("conv_w", 128), ("meta_tokens", 128))


def _pack_small(vals):
    rows = []
    for name, n in SMALL_LAYOUT:
        flat = vals[name].astype(F32).reshape(-1)
        rows.append(jnp.pad(flat, (0, n * 128 - flat.shape[0])).reshape(n, 128))
    total = sum(n for _, n in SMALL_LAYOUT)
    pad = (-total) % 8
    if pad:
        rows.append(jnp.zeros((pad, 128), F32))
    return jnp.concatenate(rows, axis=0)


def _unpack_small(buf, shapes):
    out, r0 = {}, 0
    for name, n in SMALL_LAYOUT:
        size = math.prod(shapes[name])
        out[name] = buf[r0:r0 + n].reshape(-1)[:size].reshape(shapes[name])
        r0 += n
    return out


def kernel(x, meta_tokens, attn_norm_g, w_in, attn_sinks, conv_w, conv_b, conv_ln_g, conv_ln_b, attn_out_g, conv_out_g, w_out, ffn_norm_g, w_gate, w_up, w_down, final_norm_g, loss_target, m_meta_tokens, m_attn_norm_g, m_w_in, m_attn_sinks, m_conv_w, m_conv_b, m_conv_ln_g, m_conv_ln_b, m_attn_out_g, m_conv_out_g, m_w_out, m_ffn_norm_g, m_w_gate, m_w_up, m_w_down, m_final_norm_g, v_meta_tokens, v_attn_norm_g, v_w_in, v_attn_sinks, v_conv_w, v_conv_b, v_conv_ln_g, v_conv_ln_b, v_attn_out_g, v_conv_out_g, v_w_out, v_ffn_norm_g, v_w_gate, v_w_up, v_w_down, v_final_norm_g):
    seq = x.shape[1]
    r = -(-(seq + BLOCK) // ROW_QUANTUM) * ROW_QUANTUM
    tail = r - BLOCK - seq
    shard = 2 * lax.axis_index("x") + lax.axis_index("y")

    conv_w32 = jnp.pad(conv_w[0], ((0, 1), (0, 0)))
    small_shard = jnp.concatenate([meta_tokens, conv_w32.reshape(16, 256)], axis=0)
    g_in, g_gate, g_up, g_out, g_down, g_small = _gather_weights(
        [w_in[0].astype(BF16), w_gate[0].astype(BF16), w_up[0].astype(BF16), w_out[0].astype(BF16), w_down[0].astype(BF16),
         small_shard])
    w_in_b = _from_col_shards(g_in)
    wg_b = _from_col_shards(g_gate)
    wu_b = _from_col_shards(g_up)
    w_out_b = g_out.reshape(D_MODEL, D_MODEL)
    wd_b = g_down.reshape(D_FF, D_MODEL)
    meta_full = _from_col_shards(g_small[:, 0:N_META])
    cw_full = _from_col_shards(g_small[:, N_META:].reshape(N_SHARD, 32, 128))

    g1, ga, gc, g2 = attn_norm_g, attn_out_g, conv_out_g, ffn_norm_g
    gf = final_norm_g.reshape(1, D_MODEL)
    sinks = attn_sinks[0]

    h0 = jnp.concatenate([jnp.zeros((LEAD, D_MODEL), F32), meta_full, x[0], jnp.zeros((tail, D_MODEL), F32)], axis=0)
    tgt = jnp.pad(loss_target[0], ((BLOCK, tail), (0, 0)))
    q, kv, cacg = _in_proj(h0, g1, w_in_b, 768)
    oa, lse = _attn_fwd(q, kv, sinks)
    oc = _conv_fwd(cacg, cw_full, conv_b, conv_ln_g, conv_ln_b, 384)
    h1, hn2 = _out_proj(oa, oc, h0, ga, gc, g2, w_out_b, 768)
    gate, up, dh2, dh2b, loss_p, dgf = _ffn_fwd(hn2, h1, tgt, gf, wg_b, wu_b, wd_b, seq, 384)

    dgate, dup, dh1, dg2 = _ffn_bwd(dh2, dh2b, gate, up, h1, g2, wg_b, wu_b, wd_b, 384)
    dwg, dwu = _ffn_wgrad_gu(hn2, dgate, dup, 768)
    dwd = _ffn_wgrad_d(gate, up, dh2b, 768)
    doa, doc, dwo, dga, dgc = _out_proj_bwd(dh1, oa, oc, ga, gc, w_out_b, 768)
    dy, dcw, dcb, dlg, dlb = _conv_bwd_params(doc, cacg, cw_full, conv_b, conv_ln_g, conv_ln_b, 384)
    dc = _conv_bwd_data(dy, cacg, cw_full, 384)
    dq, dkv, dkv_meta, dsink = _attn_bwd(q, kv, oa, doa, lse, sinks)
    dh0, dwi, dg1 = _in_proj_bwd(dq, dkv, dkv_meta, dc, dh1, h0, g1, w_in_b, 768)
    grad_x = dh0[BLOCK:BLOCK + seq][None]

    small = _allreduce_small(_pack_small({
        "loss": loss_p[0, 0:1], "final_norm_g": dgf, "attn_norm_g": dg1, "ffn_norm_g": dg2, "attn_out_g": dga, "conv_out_g": dgc,
        "conv_b": dcb, "conv_ln_g": dlg, "conv_ln_b": dlb, "attn_sinks": dsink[:, 0], "conv_w": dcw[0:CONV_K],
        "meta_tokens": dh0[LEAD:BLOCK]}))
    red = _unpack_small(small, {
        "loss": (), "final_norm_g": (D_MODEL,), "attn_norm_g": (1, D_MODEL), "ffn_norm_g": (1, D_MODEL), "attn_out_g": (1, ATTN_W),
        "conv_out_g": (1, CONV_W), "conv_b": (1, CONV_W), "conv_ln_g": (1, CONV_W), "conv_ln_b": (1, CONV_W),
        "attn_sinks": (1, N_HEADS), "conv_w": (CONV_K, CONV_W), "meta_tokens": (N_META, D_MODEL)})
    loss = red["loss"]
    g_meta = lax.dynamic_slice_in_dim(red["meta_tokens"], shard * (D_MODEL // N_SHARD), D_MODEL // N_SHARD, axis=1)
    g_convw = lax.dynamic_slice_in_dim(red["conv_w"], shard * (CONV_W // N_SHARD), CONV_W // N_SHARD, axis=1)[None]

    g_w_in, g_w_gate, g_w_up, g_w_out, g_w_down = _reduce_scatter(
        [_col_shards(dwi), _col_shards(dwg), _col_shards(dwu), dwo.reshape(N_SHARD, D_MODEL // N_SHARD, D_MODEL),
         dwd.reshape(N_SHARD, D_FF // N_SHARD, D_MODEL)])

    grads = {
        "meta_tokens": g_meta, "attn_norm_g": red["attn_norm_g"], "w_in": g_w_in[None], "attn_sinks": red["attn_sinks"],
        "conv_w": g_convw, "conv_b": red["conv_b"], "conv_ln_g": red["conv_ln_g"], "conv_ln_b": red["conv_ln_b"],
        "attn_out_g": red["attn_out_g"], "conv_out_g": red["conv_out_g"], "w_out": g_w_out[None], "ffn_norm_g": red["ffn_norm_g"],
        "w_gate": g_w_gate[None], "w_up": g_w_up[None], "w_down": g_w_down[None], "final_norm_g": red["final_norm_g"]}
    params = {
        "meta_tokens": (meta_tokens, m_meta_tokens, v_meta_tokens), "attn_norm_g": (attn_norm_g, m_attn_norm_g, v_attn_norm_g),
        "w_in": (w_in, m_w_in, v_w_in), "attn_sinks": (attn_sinks, m_attn_sinks, v_attn_sinks), "conv_w": (conv_w, m_conv_w, v_conv_w),
        "conv_b": (conv_b, m_conv_b, v_conv_b), "conv_ln_g": (conv_ln_g, m_conv_ln_g, v_conv_ln_g),
        "conv_ln_b": (conv_ln_b, m_conv_ln_b, v_conv_ln_b), "attn_out_g": (attn_out_g, m_attn_out_g, v_attn_out_g),
        "conv_out_g": (conv_out_g, m_conv_out_g, v_conv_out_g), "w_out": (w_out, m_w_out, v_w_out),
        "ffn_norm_g": (ffn_norm_g, m_ffn_norm_g, v_ffn_norm_g), "w_gate": (w_gate, m_w_gate, v_w_gate), "w_up": (w_up, m_w_up, v_w_up),
        "w_down": (w_down, m_w_down, v_w_down), "final_norm_g": (final_norm_g, m_final_norm_g, v_final_norm_g)}
    names = list(params)
    big = ("w_in", "w_out", "w_gate", "w_up", "w_down")
    delta, new_m, new_v = {}, {}, {}
    for name in big:
        w, m, v = params[name]
        d, nm, nv = _adamw(w[0], grads[name][0], m[0], v[0], "adamw_" + name)
        delta[name], new_m[name], new_v[name] = d[None], nm[None], nv[None]
    rest = [name for name in names if name not in big]

    def pack(arrs):
        flat = jnp.concatenate([a.reshape(-1) for a in arrs])
        return jnp.pad(flat, (0, (-flat.shape[0]) % 1024)).reshape(-1, 128)

    packed = _adamw(pack([params[n][0] for n in rest]), pack([grads[n] for n in rest]), pack([params[n][1] for n in rest]),
                    pack([params[n][2] for n in rest]), "adamw_small")
    off = 0
    for name in rest:
        shape = params[name][0].shape
        size = math.prod(shape)
        for dst, buf in zip((delta, new_m, new_v), packed):
            dst[name] = buf.reshape(-1)[off:off + size].reshape(shape)
        off += size

    return (loss, grad_x, *[grads[n] for n in names], *[delta[n] for n in names], *[new_m[n] for n in names],
            *[new_v[n] for n in names])
```

```python
import functools
import math

import jax
import jax.numpy as jnp
from jax import lax
from jax.experimental import pallas as pl
from jax.experimental.pallas import tpu as pltpu

F32 = jnp.float32
BF16 = jnp.bfloat16

D_MODEL = 1024
N_META = 16
ATTN_W = 512
CONV_W = 512
HEAD_DIM = 64
N_HEADS = 8
N_KV = 2
GROUP = N_HEADS // N_KV
KV_W = N_KV * HEAD_DIM
BLOCK = 128
LEAD = BLOCK - N_META
CONV_K = 31
D_FF = 2816
IN_COLS = ATTN_W + 2 * KV_W + 2 * CONV_W
Q0, KV0, C0 = 0, ATTN_W, ATTN_W + 2 * KV_W
NORM_EPS = 1e-5
SCALE = 1.0 / math.sqrt(HEAD_DIM)
SLOPES = tuple(2.0 ** (-(8.0 / N_HEADS) * (h + 1)) for h in range(N_HEADS))
NEG = -1e30

ADAM_LR, ADAM_B1, ADAM_B2, ADAM_EPS, ADAM_WD, ADAM_STEP = 0.001, 0.9, 0.999, 1e-08, 0.01, 10

N_SHARD = 4
N_DEV = 8
ROW_QUANTUM = 768
HALO = 32
CONV_CHUNK = 32
FF_CHUNK = 256
VMEM_LIMIT = 60 * 1024 * 1024


def _cparams(n_axes=1):
    return pltpu.CompilerParams(dimension_semantics=("arbitrary",) * n_axes, vmem_limit_bytes=VMEM_LIMIT)


def _dot(a, b):
    return jnp.dot(a, b, preferred_element_type=F32)


def _dot_nt(a, b):
    return lax.dot_general(a, b, (((1,), (1,)), ((), ())), preferred_element_type=F32)


def _dot_tn(a, b):
    return lax.dot_general(a, b, (((0,), (0,)), ((), ())), preferred_element_type=F32)


def _sigmoid(x):
    return 1.0 / (1.0 + jnp.exp(-x))


def _row(tm, n):
    return pl.BlockSpec((tm, n), lambda i: (i, 0))


def _const(shape):
    return pl.BlockSpec(shape, lambda i: (0,) * len(shape))


def _resident(shape):
    return pl.BlockSpec(shape, lambda i: (0,) * len(shape), pipeline_mode=pl.Buffered(1))


def _rms_fwd(x, g):
    rstd = lax.rsqrt(jnp.mean(x * x, axis=-1, keepdims=True) + NORM_EPS)
    xhat = x * rstd
    return xhat * g, xhat, rstd


def _rms_bwd(dy, xhat, rstd, g):
    dxh = dy * g
    dx = rstd * (dxh - xhat * jnp.mean(dxh * xhat, axis=-1, keepdims=True))
    return dx, dy * xhat


def _in_proj(h0, g1, w_in_b, tm):
    r = h0.shape[0]

    def body(h_ref, g_ref, w_ref, q_ref, kv_ref, c_ref):
        hn = _rms_fwd(h_ref[...], g_ref[...])[0].astype(BF16)
        q_ref[...] = _dot(hn, w_ref[:, Q0:KV0]).astype(BF16)
        kv_ref[...] = _dot(hn, w_ref[:, KV0:C0]).astype(BF16)
        c_ref[...] = _dot(hn, w_ref[:, C0:IN_COLS])

    return pl.pallas_call(
        body, name="in_proj", grid=(r // tm,),
        in_specs=[_row(tm, D_MODEL), _const((1, D_MODEL)), _const((D_MODEL, IN_COLS))],
        out_specs=[_row(tm, ATTN_W), _row(tm, 2 * KV_W), _row(tm, 2 * CONV_W)],
        out_shape=[jax.ShapeDtypeStruct((r, ATTN_W), BF16), jax.ShapeDtypeStruct((r, 2 * KV_W), BF16),
                   jax.ShapeDtypeStruct((r, 2 * CONV_W), F32)],
        compiler_params=_cparams(),
    )(h0, g1, w_in_b)


def _attn_masks(i):
    row = lax.broadcasted_iota(jnp.int32, (GROUP * BLOCK, BLOCK), 0) & (BLOCK - 1)
    col = lax.broadcasted_iota(jnp.int32, (GROUP * BLOCK, BLOCK), 1)
    blk = jnp.zeros((GROUP * BLOCK, BLOCK), jnp.int32) + i
    ok_meta = (col >= LEAD) & ((blk >= 1) | (col <= row))
    ok_prev = (col > row) & (blk >= 2)
    ok_cur = (col <= row) & (blk >= 1)
    ok = jnp.concatenate([ok_meta, ok_prev, ok_cur], axis=1)
    dist = jnp.concatenate([jnp.zeros((GROUP * BLOCK, BLOCK), F32), (row - col + BLOCK).astype(F32), (row - col).astype(F32)], axis=1)
    return ok, dist


def _stack_heads(ref, g):
    return jnp.concatenate([ref[:, (g * GROUP + j) * HEAD_DIM:(g * GROUP + j + 1) * HEAD_DIM] for j in range(GROUP)], axis=0)


def _per_head_col(vals):
    return jnp.concatenate([jnp.zeros((BLOCK, 1), F32) + v for v in vals], axis=0)


def _kv_cat(kvm_ref, kvp_ref, kvc_ref, g):
    ks = slice(g * HEAD_DIM, (g + 1) * HEAD_DIM)
    vs = slice(KV_W + g * HEAD_DIM, KV_W + (g + 1) * HEAD_DIM)
    kcat = jnp.concatenate([kvm_ref[:, ks], kvp_ref[:, ks], kvc_ref[:, ks]], axis=0)
    vcat = jnp.concatenate([kvm_ref[:, vs], kvp_ref[:, vs], kvc_ref[:, vs]], axis=0)
    return kcat, vcat


def _attn_fwd(q, kv, sinks, shards, gathered):
    r = q.shape[0]
    nb = r // BLOCK
    n = len(shards)

    def body(sink_ref, q_ref, kvc_ref, kvp_ref, kvm_ref, *rest):
        src = rest[:n]
        o_ref, lse_ref = rest[2 * n:2 * n + 2]
        dst = rest[2 * n + 2:3 * n + 2]
        send_sems, recv_sems = rest[3 * n + 2:]
        i = pl.program_id(0)

        @pl.when(i == 0)
        def _():
            for cp in _gather_ici(src, dst, send_sems, recv_sems)[0]:
                cp.start()

        ok, dist = _attn_masks(i)
        lane = lax.broadcasted_iota(jnp.int32, (BLOCK, BLOCK), 1)
        lse_tile = jnp.zeros((BLOCK, BLOCK), F32)
        for g in range(N_KV):
            kcat, vcat = _kv_cat(kvm_ref, kvp_ref, kvc_ref, g)
            heads = range(g * GROUP, (g + 1) * GROUP)
            s = _dot_nt(_stack_heads(q_ref, g), kcat) * SCALE - _per_head_col([SLOPES[h] for h in heads]) * dist
            s = jnp.where(ok, s, NEG)
            sink = _per_head_col([sink_ref[h] for h in heads])
            m = jnp.maximum(jnp.max(s, axis=-1, keepdims=True), sink)
            p = jnp.exp(s - m)
            l = jnp.sum(p, axis=-1, keepdims=True) + jnp.exp(sink - m)
            o = _dot((p * (1.0 / l)).astype(BF16), vcat)
            lse = m + jnp.log(l)
            for j, h in enumerate(heads):
                o_ref[:, h * HEAD_DIM:(h + 1) * HEAD_DIM] = o[j * BLOCK:(j + 1) * BLOCK]
                lse_tile = jnp.where(lane == h, lse[j * BLOCK:(j + 1) * BLOCK], lse_tile)
        lse_ref[...] = lse_tile

        @pl.when(i == nb - 1)
        def _():
            sends, arrivals = _gather_ici(src, dst, send_sems, recv_sems)
            for cp in arrivals:
                cp.wait_recv()
            for cp in sends:
                cp.wait_send()

    return pl.pallas_call(
        body, name="attn_fwd", grid=(nb,),
        in_specs=[pl.BlockSpec(memory_space=pltpu.SMEM), _row(BLOCK, ATTN_W), _row(BLOCK, 2 * KV_W),
                  pl.BlockSpec((BLOCK, 2 * KV_W), lambda i: (jnp.maximum(i - 1, 0), 0)), _const((BLOCK, 2 * KV_W))] + [ANY] * (2 * n),
        out_specs=[_row(BLOCK, ATTN_W), _row(BLOCK, BLOCK)] + [ANY] * n,
        out_shape=[jax.ShapeDtypeStruct((r, ATTN_W), F32), jax.ShapeDtypeStruct((r, BLOCK), F32)]
        + [jax.ShapeDtypeStruct(g.shape, g.dtype) for g in gathered],
        input_output_aliases={5 + n + k: 2 + k for k in range(n)},
        scratch_shapes=[pltpu.SemaphoreType.DMA((3 * n,)), pltpu.SemaphoreType.DMA((3 * n,))],
        compiler_params=_cparams(),
    )(sinks, q, kv, kv, kv, *shards, *gathered)


def _shifted_copies(ub_ref, win):
    w = win.shape[0]
    ub_ref[0] = win
    for b in range(1, 8):
        ub_ref[b] = pltpu.roll(win, shift=w - b, axis=0)


def _conv_chunk(ub_ref, w_ref, r0, shifts):
    acc = jnp.zeros((CONV_CHUNK, CONV_W), F32)
    for j in range(CONV_K):
        a, b = divmod(shifts[j], 8)
        acc = acc + w_ref[j:j + 1, :] * ub_ref[b, pl.ds(r0 + 8 * a, CONV_CHUNK), :]
    return acc


FWD_SHIFTS = tuple(HALO - (CONV_K - 1) + j for j in range(CONV_K))
BWD_SHIFTS = tuple(CONV_K - 1 - j for j in range(CONV_K))


def _glu_window(cp_ref, c_ref, i):
    tile = c_ref[:, 0:CONV_W] * _sigmoid(c_ref[:, CONV_W:2 * CONV_W])
    halo = cp_ref[:, 0:CONV_W] * _sigmoid(cp_ref[:, CONV_W:2 * CONV_W])
    first = (jnp.zeros((HALO, CONV_W), jnp.int32) + i) == 0
    return jnp.concatenate([jnp.where(first, 0.0, halo), tile], axis=0)


def _halo_before(tm, n):
    return pl.BlockSpec((HALO, n), lambda i: (jnp.maximum(i * (tm // HALO) - 1, 0), 0))


def _conv_fwd(cacg, cw, cb, lg, lb, tm, gathered):
    r = cacg.shape[0]
    n = len(gathered)

    def body(c_ref, cp_ref, w_ref, cb_ref, lg_ref, lb_ref, *rest):
        o_ref = rest[n]
        dst = rest[n + 1:2 * n + 1]
        ub_ref, send_sems, recv_sems = rest[2 * n + 1:]
        i = pl.program_id(0)

        @pl.when(i == 0)
        def _():
            for cp in _gather_d2d(dst, send_sems, recv_sems)[0]:
                cp.start()

        _shifted_copies(ub_ref, _glu_window(cp_ref, c_ref, i))

        def chunk(ci, carry):
            r0 = pl.multiple_of(ci * CONV_CHUNK, CONV_CHUNK)
            y = _conv_chunk(ub_ref, w_ref, r0, FWD_SHIFTS) + cb_ref[...]
            yc = y - jnp.mean(y, axis=-1, keepdims=True)
            rs = lax.rsqrt(jnp.mean(yc * yc, axis=-1, keepdims=True) + NORM_EPS)
            yn = yc * rs * lg_ref[...] + lb_ref[...]
            o_ref[pl.ds(r0, CONV_CHUNK), :] = yn * _sigmoid(yn)
            return carry

        lax.fori_loop(0, tm // CONV_CHUNK, chunk, 0)

        @pl.when(i == r // tm - 1)
        def _():
            sends, arrivals = _gather_d2d(dst, send_sems, recv_sems)
            for cp in arrivals:
                cp.wait_recv()
            for cp in sends:
                cp.wait_send()

    return pl.pallas_call(
        body, name="conv_fwd", grid=(r // tm,),
        in_specs=[_row(tm, 2 * CONV_W), _halo_before(tm, 2 * CONV_W), _const((32, CONV_W)), _const((1, CONV_W)),
                  _const((1, CONV_W)), _const((1, CONV_W))] + [ANY] * n,
        out_specs=[_row(tm, CONV_W)] + [ANY] * n,
        out_shape=[jax.ShapeDtypeStruct((r, CONV_W), F32)] + [jax.ShapeDtypeStruct(g.shape, g.dtype) for g in gathered],
        input_output_aliases={6 + k: 1 + k for k in range(n)},
        scratch_shapes=[pltpu.VMEM((8, tm + HALO, CONV_W), F32), pltpu.SemaphoreType.DMA((3 * n,)), pltpu.SemaphoreType.DMA((3 * n,))],
        compiler_params=_cparams(),
    )(cacg, cacg, cw, cb, lg, lb, *gathered)


def _out_proj(oa, oc, h0, ga, gc, g2, w_out_b, tm):
    r = h0.shape[0]

    def body(oa_ref, oc_ref, h_ref, ga_ref, gc_ref, g2_ref, w_ref, h1_ref, hn2_ref):
        ma = _rms_fwd(oa_ref[...], ga_ref[...])[0].astype(BF16)
        mc = _rms_fwd(oc_ref[...], gc_ref[...])[0].astype(BF16)
        h1 = h_ref[...] + _dot(ma, w_ref[0:ATTN_W, :]) + _dot(mc, w_ref[ATTN_W:ATTN_W + CONV_W, :])
        h1_ref[...] = h1
        hn2_ref[...] = _rms_fwd(h1, g2_ref[...])[0].astype(BF16)

    return pl.pallas_call(
        body, name="out_proj", grid=(r // tm,),
        in_specs=[_row(tm, ATTN_W), _row(tm, CONV_W), _row(tm, D_MODEL), _const((1, ATTN_W)), _const((1, CONV_W)),
                  _const((1, D_MODEL)), _const((D_MODEL, D_MODEL))],
        out_specs=[_row(tm, D_MODEL), _row(tm, D_MODEL)],
        out_shape=[jax.ShapeDtypeStruct((r, D_MODEL), F32), jax.ShapeDtypeStruct((r, D_MODEL), BF16)],
        compiler_params=_cparams(),
    )(oa, oc, h0, ga, gc, g2, w_out_b)


def _ffn_fwd(hn2, h1, tgt, gf, wg_b, wu_b, wd_b, seq, tm):
    r = h1.shape[0]

    def body(hn_ref, h1_ref, t_ref, gf_ref, wg_ref, wu_ref, wd_ref, gate_ref, up_ref, dh2_ref, dh2b_ref, loss_ref, dgf_ref, acc_ref):
        i = pl.program_id(0)

        @pl.when(i == 0)
        def _():
            loss_ref[...] = jnp.zeros_like(loss_ref)
            dgf_ref[...] = jnp.zeros_like(dgf_ref)

        hn = hn_ref[...]
        acc_ref[...] = h1_ref[...]
        for ch in range(D_FF // FF_CHUNK):
            cs = slice(ch * FF_CHUNK, (ch + 1) * FF_CHUNK)
            gate = _dot(hn, wg_ref[:, cs])
            up = _dot(hn, wu_ref[:, cs])
            gate_ref[:, cs] = gate.astype(BF16)
            up_ref[:, cs] = up.astype(BF16)
            act = (gate * _sigmoid(gate) * up).astype(BF16)
            acc_ref[...] += _dot(act, wd_ref[cs, :])
        y, xhat, rstd = _rms_fwd(acc_ref[...], gf_ref[...])
        rows = lax.broadcasted_iota(jnp.int32, (tm, D_MODEL), 0) + i * tm
        real = (rows >= BLOCK) & (rows < BLOCK + seq)
        err = jnp.where(real, y - t_ref[...], 0.0)
        loss_ref[...] += jnp.sum(err * err) * (0.5 / D_MODEL)
        dy = err * (1.0 / D_MODEL)
        dh2, dg_rows = _rms_bwd(dy, xhat, rstd, gf_ref[...])
        dgf_ref[...] += jnp.sum(dg_rows, axis=0, keepdims=True)
        dh2_ref[...] = dh2
        dh2b_ref[...] = dh2.astype(BF16)

    return pl.pallas_call(
        body, name="ffn_fwd", grid=(r // tm,),
        in_specs=[_row(tm, D_MODEL), _row(tm, D_MODEL), _row(tm, D_MODEL), _const((1, D_MODEL)),
                  _resident((D_MODEL, D_FF)), _resident((D_MODEL, D_FF)), _resident((D_FF, D_MODEL))],
        out_specs=[_row(tm, D_FF), _row(tm, D_FF), _row(tm, D_MODEL), _row(tm, D_MODEL), _const((1, BLOCK)), _const((1, D_MODEL))],
        out_shape=[jax.ShapeDtypeStruct((r, D_FF), BF16), jax.ShapeDtypeStruct((r, D_FF), BF16),
                   jax.ShapeDtypeStruct((r, D_MODEL), F32), jax.ShapeDtypeStruct((r, D_MODEL), BF16),
                   jax.ShapeDtypeStruct((1, BLOCK), F32), jax.ShapeDtypeStruct((1, D_MODEL), F32)],
        scratch_shapes=[pltpu.VMEM((tm, D_MODEL), F32)],
        compiler_params=_cparams(),
    )(hn2, h1, tgt, gf, wg_b, wu_b, wd_b)


def _ffn_bwd(dh2, dh2b, gate, up, h1, g2, wg_b, wu_b, wd_b, tm):
    r = h1.shape[0]

    def body(dh2_ref, dh2b_ref, gate_ref, up_ref, h1_ref, g2_ref, wg_ref, wu_ref, wd_ref, dgate_ref, dup_ref, dh1_ref, dg2_ref, acc_ref):
        @pl.when(pl.program_id(0) == 0)
        def _():
            dg2_ref[...] = jnp.zeros_like(dg2_ref)

        dyb = dh2b_ref[...]
        acc_ref[...] = jnp.zeros_like(acc_ref)
        for ch in range(D_FF // FF_CHUNK):
            cs = slice(ch * FF_CHUNK, (ch + 1) * FF_CHUNK)
            dact = _dot_nt(dyb, wd_ref[cs, :])
            gate = gate_ref[:, cs].astype(F32)
            up = up_ref[:, cs].astype(F32)
            sg = _sigmoid(gate)
            silu = gate * sg
            dgate = (dact * up * (sg * (1.0 + gate * (1.0 - sg)))).astype(BF16)
            dup = (dact * silu).astype(BF16)
            dgate_ref[:, cs] = dgate
            dup_ref[:, cs] = dup
            acc_ref[...] += _dot_nt(dgate, wg_ref[:, cs]) + _dot_nt(dup, wu_ref[:, cs])
        _, xhat, rstd = _rms_fwd(h1_ref[...], g2_ref[...])
        dx, dg_rows = _rms_bwd(acc_ref[...], xhat, rstd, g2_ref[...])
        dg2_ref[...] += jnp.sum(dg_rows, axis=0, keepdims=True)
        dh1_ref[...] = dh2_ref[...] + dx

    return pl.pallas_call(
        body, name="ffn_bwd", grid=(r // tm,),
        in_specs=[_row(tm, D_MODEL), _row(tm, D_MODEL), _row(tm, D_FF), _row(tm, D_FF), _row(tm, D_MODEL), _const((1, D_MODEL)),
                  _resident((D_MODEL, D_FF)), _resident((D_MODEL, D_FF)), _resident((D_FF, D_MODEL))],
        out_specs=[_row(tm, D_FF), _row(tm, D_FF), _row(tm, D_MODEL), _const((1, D_MODEL))],
        out_shape=[jax.ShapeDtypeStruct((r, D_FF), BF16), jax.ShapeDtypeStruct((r, D_FF), BF16),
                   jax.ShapeDtypeStruct((r, D_MODEL), F32), jax.ShapeDtypeStruct((1, D_MODEL), F32)],
        scratch_shapes=[pltpu.VMEM((tm, D_MODEL), F32)],
        compiler_params=_cparams(),
    )(dh2, dh2b, gate, up, h1, g2, wg_b, wu_b, wd_b)


FF_HALF = D_FF // 2


def _ffn_wgrad_gu(hn2, dgate, dup, tk):
    r = hn2.shape[0]

    def body(hn_ref, dg_ref, du_ref, wg_ref, wu_ref):
        @pl.when(pl.program_id(1) == 0)
        def _():
            wg_ref[...] = jnp.zeros_like(wg_ref)
            wu_ref[...] = jnp.zeros_like(wu_ref)

        hn = hn_ref[...]
        wg_ref[...] += _dot_tn(hn, dg_ref[...])
        wu_ref[...] += _dot_tn(hn, du_ref[...])

    col = pl.BlockSpec((tk, FF_HALF), lambda j, k: (k, j))
    out = pl.BlockSpec((D_MODEL, FF_HALF), lambda j, k: (0, j))
    return pl.pallas_call(
        body, name="ffn_wgrad_gu", grid=(2, r // tk),
        in_specs=[pl.BlockSpec((tk, D_MODEL), lambda j, k: (k, 0)), col, col],
        out_specs=[out, out],
        out_shape=[jax.ShapeDtypeStruct((D_MODEL, D_FF), F32)] * 2,
        compiler_params=_cparams(2),
    )(hn2, dgate, dup)


def _ffn_wgrad_d(gate, up, dh2b, tk):
    r = gate.shape[0]

    def body(g_ref, u_ref, dy_ref, wd_ref):
        @pl.when(pl.program_id(1) == 0)
        def _():
            wd_ref[...] = jnp.zeros_like(wd_ref)

        gate = g_ref[...].astype(F32)
        act = (gate * _sigmoid(gate) * u_ref[...].astype(F32)).astype(BF16)
        wd_ref[...] += _dot_tn(act, dy_ref[...])

    col = pl.BlockSpec((tk, FF_HALF), lambda j, k: (k, j))
    return pl.pallas_call(
        body, name="ffn_wgrad_d", grid=(2, r // tk),
        in_specs=[col, col, pl.BlockSpec((tk, D_MODEL), lambda j, k: (k, 0))],
        out_specs=pl.BlockSpec((FF_HALF, D_MODEL), lambda j, k: (j, 0)),
        out_shape=jax.ShapeDtypeStruct((D_FF, D_MODEL), F32),
        compiler_params=_cparams(2),
    )(gate, up, dh2b)


def _out_proj_bwd(dh1, oa, oc, ga, gc, w_out_b, tm):
    r = dh1.shape[0]

    def body(dh_ref, oa_ref, oc_ref, ga_ref, gc_ref, w_ref, doa_ref, doc_ref, dw_ref, dga_ref, dgc_ref):
        @pl.when(pl.program_id(0) == 0)
        def _():
            dw_ref[...] = jnp.zeros_like(dw_ref)
            dga_ref[...] = jnp.zeros_like(dga_ref)
            dgc_ref[...] = jnp.zeros_like(dgc_ref)

        dhb = dh_ref[...].astype(BF16)
        dmix = _dot_nt(dhb, w_ref[...])
        ma, xa, ra = _rms_fwd(oa_ref[...], ga_ref[...])
        mc, xc, rc = _rms_fwd(oc_ref[...], gc_ref[...])
        dw_ref[0:ATTN_W, :] += _dot_tn(ma.astype(BF16), dhb)
        dw_ref[ATTN_W:ATTN_W + CONV_W, :] += _dot_tn(mc.astype(BF16), dhb)
        doa, dga_rows = _rms_bwd(dmix[:, 0:ATTN_W], xa, ra, ga_ref[...])
        doc, dgc_rows = _rms_bwd(dmix[:, ATTN_W:ATTN_W + CONV_W], xc, rc, gc_ref[...])
        doa_ref[...] = doa
        doc_ref[...] = doc
        dga_ref[...] += jnp.sum(dga_rows, axis=0, keepdims=True)
        dgc_ref[...] += jnp.sum(dgc_rows, axis=0, keepdims=True)

    return pl.pallas_call(
        body, name="out_proj_bwd", grid=(r // tm,),
        in_specs=[_row(tm, D_MODEL), _row(tm, ATTN_W), _row(tm, CONV_W), _const((1, ATTN_W)), _const((1, CONV_W)),
                  _const((D_MODEL, D_MODEL))],
        out_specs=[_row(tm, ATTN_W), _row(tm, CONV_W), _const((D_MODEL, D_MODEL)), _const((1, ATTN_W)), _const((1, CONV_W))],
        out_shape=[jax.ShapeDtypeStruct((r, ATTN_W), F32), jax.ShapeDtypeStruct((r, CONV_W), F32),
                   jax.ShapeDtypeStruct((D_MODEL, D_MODEL), F32), jax.ShapeDtypeStruct((1, ATTN_W), F32),
                   jax.ShapeDtypeStruct((1, CONV_W), F32)],
        compiler_params=_cparams(),
    )(dh1, oa, oc, ga, gc, w_out_b)


def _conv_bwd_params(doc, cacg, cw, cb, lg, lb, tm, parts):
    r = cacg.shape[0]
    n_steps = r // tm
    n = len(parts)

    def body(do_ref, c_ref, cp_ref, w_ref, cb_ref, lg_ref, lb_ref, *rest):
        src = rest[:n]
        dy_ref, dcw_ref, dcb_ref, dlg_ref, dlb_ref = rest[n:n + 5]
        dst = rest[n + 5:2 * n + 5]
        ub_ref, accw_ref, send_sems, recv_sems = rest[2 * n + 5:]
        i = pl.program_id(0)

        @pl.when(i == 0)
        def _():
            for cp in _scatter(src, dst, send_sems, recv_sems):
                cp.start()
            accw_ref[...] = jnp.zeros_like(accw_ref)
            dcb_ref[...] = jnp.zeros_like(dcb_ref)
            dlg_ref[...] = jnp.zeros_like(dlg_ref)
            dlb_ref[...] = jnp.zeros_like(dlb_ref)

        _shifted_copies(ub_ref, _glu_window(cp_ref, c_ref, i))

        def chunk(ci, carry):
            r0 = pl.multiple_of(ci * CONV_CHUNK, CONV_CHUNK)
            y = _conv_chunk(ub_ref, w_ref, r0, FWD_SHIFTS) + cb_ref[...]
            yc = y - jnp.mean(y, axis=-1, keepdims=True)
            rs = lax.rsqrt(jnp.mean(yc * yc, axis=-1, keepdims=True) + NORM_EPS)
            xhat = yc * rs
            yn = xhat * lg_ref[...] + lb_ref[...]
            sg = _sigmoid(yn)
            dyn = do_ref[pl.ds(r0, CONV_CHUNK), :] * (sg * (1.0 + yn * (1.0 - sg)))
            dlg_ref[...] += jnp.sum(dyn * xhat, axis=0, keepdims=True)
            dlb_ref[...] += jnp.sum(dyn, axis=0, keepdims=True)
            dxh = dyn * lg_ref[...]
            dy = rs * (dxh - jnp.mean(dxh, axis=-1, keepdims=True) - xhat * jnp.mean(dxh * xhat, axis=-1, keepdims=True))
            dcb_ref[...] += jnp.sum(dy, axis=0, keepdims=True)
            dy_ref[pl.ds(r0, CONV_CHUNK), :] = dy
            for j in range(CONV_K):
                a, b = divmod(FWD_SHIFTS[j], 8)
                prod = dy * ub_ref[b, pl.ds(r0 + 8 * a, CONV_CHUNK), :]
                accw_ref[j] += jnp.sum(prod.reshape(CONV_CHUNK // 8, 8, CONV_W), axis=0)
            return carry

        lax.fori_loop(0, tm // CONV_CHUNK, chunk, 0)

        @pl.when(i == n_steps - 1)
        def _():
            for j in range(32):
                dcw_ref[j:j + 1, :] = jnp.sum(accw_ref[j], axis=0, keepdims=True)
            for cp in _scatter(src, dst, send_sems, recv_sems):
                cp.wait()

    vec = _const((1, CONV_W))
    return pl.pallas_call(
        body, name="conv_bwd_params", grid=(n_steps,),
        in_specs=[_row(tm, CONV_W), _row(tm, 2 * CONV_W), _halo_before(tm, 2 * CONV_W), _const((32, CONV_W)), vec, vec, vec]
        + [ANY] * n,
        out_specs=[_row(tm, CONV_W), _const((32, CONV_W)), vec, vec, vec] + [ANY] * n,
        out_shape=[jax.ShapeDtypeStruct((r, CONV_W), F32), jax.ShapeDtypeStruct((32, CONV_W), F32)]
        + [jax.ShapeDtypeStruct((1, CONV_W), F32)] * 3 + _scatter_landing(parts),
        scratch_shapes=[pltpu.VMEM((8, tm + HALO, CONV_W), F32), pltpu.VMEM((32, 8, CONV_W), F32),
                        pltpu.SemaphoreType.DMA((7 * n,)), pltpu.SemaphoreType.DMA((7 * n,))],
        compiler_params=_cparams(),
    )(doc, cacg, cacg, cw, cb, lg, lb, *parts)


def _conv_bwd_data(dy, cacg, cw, tm):
    r = cacg.shape[0]
    n_steps = r // tm

    def body(dy_ref, dyn_ref, c_ref, w_ref, dc_ref, ub_ref):
        last = (jnp.zeros((HALO, CONV_W), jnp.int32) + pl.program_id(0)) == n_steps - 1
        win = jnp.concatenate([dy_ref[...], jnp.where(last, 0.0, dyn_ref[...])], axis=0)
        _shifted_copies(ub_ref, win)

        def chunk(ci, carry):
            r0 = pl.multiple_of(ci * CONV_CHUNK, CONV_CHUNK)
            du = _conv_chunk(ub_ref, w_ref, r0, BWD_SHIFTS)
            ca = c_ref[pl.ds(r0, CONV_CHUNK), 0:CONV_W]
            sg = _sigmoid(c_ref[pl.ds(r0, CONV_CHUNK), CONV_W:2 * CONV_W])
            dc_ref[pl.ds(r0, CONV_CHUNK), 0:CONV_W] = (du * sg).astype(BF16)
            dc_ref[pl.ds(r0, CONV_CHUNK), CONV_W:2 * CONV_W] = (du * ca * sg * (1.0 - sg)).astype(BF16)
            return carry

        lax.fori_loop(0, tm // CONV_CHUNK, chunk, 0)

    halo_after = pl.BlockSpec((HALO, CONV_W), lambda i: (jnp.minimum((i + 1) * (tm // HALO), r // HALO - 1), 0))
    return pl.pallas_call(
        body, name="conv_bwd_data", grid=(n_steps,),
        in_specs=[_row(tm, CONV_W), halo_after, _row(tm, 2 * CONV_W), _const((32, CONV_W))],
        out_specs=_row(tm, 2 * CONV_W),
        out_shape=jax.ShapeDtypeStruct((r, 2 * CONV_W), BF16),
        scratch_shapes=[pltpu.VMEM((8, tm + HALO, CONV_W), F32)],
        compiler_params=_cparams(),
    )(dy, dy, cacg, cw)


def _attn_bwd(q, kv, o, do, lse, sinks, parts):
    r = q.shape[0]
    nb = r // BLOCK
    n = len(parts)

    def body(sink_ref, q_ref, kvc_ref, kvp_ref, kvm_ref, o_ref, do_ref, lse_ref, *rest):
        src = rest[:n]
        dq_ref, dkv_ref, dmeta_ref, dsink_ref = rest[n:n + 4]
        dst = rest[n + 4:2 * n + 4]
        hold_ref, send_sems, recv_sems = rest[2 * n + 4:]
        i = pl.program_id(0)

        @pl.when(i == 0)
        def _():
            for cp in _scatter(src, dst, send_sems, recv_sems):
                cp.start()
            dmeta_ref[...] = jnp.zeros_like(dmeta_ref)
            dsink_ref[...] = jnp.zeros_like(dsink_ref)
            hold_ref[...] = jnp.zeros_like(hold_ref)

        @pl.when(i < nb)
        def _():
            ok, dist = _attn_masks(i)
            lane = lax.broadcasted_iota(jnp.int32, (BLOCK, BLOCK), 1)
            lse_tile = lse_ref[...]
            for g in range(N_KV):
                kcat, vcat = _kv_cat(kvm_ref, kvp_ref, kvc_ref, g)
                heads = range(g * GROUP, (g + 1) * GROUP)
                qs = _stack_heads(q_ref, g)
                dos = _stack_heads(do_ref, g)
                dosb = dos.astype(BF16)
                lse = jnp.concatenate(
                    [jnp.sum(jnp.where(lane == h, lse_tile, 0.0), axis=-1, keepdims=True) for h in heads], axis=0)
                s = _dot_nt(qs, kcat) * SCALE - _per_head_col([SLOPES[h] for h in heads]) * dist
                p = jnp.exp(jnp.where(ok, s, NEG) - lse)
                delta = jnp.sum(dos * _stack_heads(o_ref, g), axis=-1, keepdims=True)
                sink_term = jnp.exp(_per_head_col([sink_ref[h] for h in heads]) - lse) * delta
                ds = ((p * (_dot_nt(dosb, vcat) - delta)) * SCALE).astype(BF16)
                dq = _dot(ds, kcat).astype(BF16)
                for j, h in enumerate(heads):
                    dsink_ref[h:h + 1, :] += -jnp.sum(sink_term[j * BLOCK:(j + 1) * BLOCK])
                    dq_ref[:, h * HEAD_DIM:(h + 1) * HEAD_DIM] = dq[j * BLOCK:(j + 1) * BLOCK]
                dk = _dot_tn(ds, qs)
                dv = _dot_tn(p.astype(BF16), dosb)
                ks = slice(g * HEAD_DIM, (g + 1) * HEAD_DIM)
                vs = slice(KV_W + g * HEAD_DIM, KV_W + (g + 1) * HEAD_DIM)
                for sl, grad in ((ks, dk), (vs, dv)):
                    dmeta_ref[:, sl] += grad[0:BLOCK]
                    dkv_ref[:, sl] = hold_ref[:, sl] + grad[BLOCK:2 * BLOCK]
                    hold_ref[:, sl] = grad[2 * BLOCK:3 * BLOCK]

        @pl.when(i == nb)
        def _():
            dkv_ref[...] = hold_ref[...]
            for cp in _scatter(src, dst, send_sems, recv_sems):
                cp.wait()

    def cur(i):
        return jnp.minimum(i, nb - 1)

    return pl.pallas_call(
        body, name="attn_bwd", grid=(nb + 1,),
        in_specs=[pl.BlockSpec(memory_space=pltpu.SMEM),
                  pl.BlockSpec((BLOCK, ATTN_W), lambda i: (cur(i), 0)),
                  pl.BlockSpec((BLOCK, 2 * KV_W), lambda i: (cur(i), 0)),
                  pl.BlockSpec((BLOCK, 2 * KV_W), lambda i: (jnp.maximum(cur(i) - 1, 0), 0)),
                  _const((BLOCK, 2 * KV_W)),
                  pl.BlockSpec((BLOCK, ATTN_W), lambda i: (cur(i), 0)),
                  pl.BlockSpec((BLOCK, ATTN_W), lambda i: (cur(i), 0)),
                  pl.BlockSpec((BLOCK, BLOCK), lambda i: (cur(i), 0))] + [ANY] * n,
        out_specs=[pl.BlockSpec((BLOCK, ATTN_W), lambda i: (cur(i), 0)),
                   pl.BlockSpec((BLOCK, 2 * KV_W), lambda i: (jnp.maximum(i - 1, 0), 0)),
                   _const((BLOCK, 2 * KV_W)), _const((N_HEADS, BLOCK))] + [ANY] * n,
        out_shape=[jax.ShapeDtypeStruct((r, ATTN_W), BF16), jax.ShapeDtypeStruct((r, 2 * KV_W), F32),
                   jax.ShapeDtypeStruct((BLOCK, 2 * KV_W), F32), jax.ShapeDtypeStruct((N_HEADS, BLOCK), F32)] + _scatter_landing(parts),
        scratch_shapes=[pltpu.VMEM((BLOCK, 2 * KV_W), F32), pltpu.SemaphoreType.DMA((7 * n,)), pltpu.SemaphoreType.DMA((7 * n,))],
        compiler_params=_cparams(),
    )(sinks, q, kv, kv, kv, o, do, lse, *parts)


def _in_proj_bwd(dq, dkv, dkv_meta, dc, dh1, h0, g1, w_in_b, tm):
    r = h0.shape[0]

    def body(dq_ref, dkv_ref, dm_ref, dc_ref, dh1_ref, h_ref, g_ref, w_ref, dh0_ref, dw_ref, dg_ref):
        i = pl.program_id(0)

        @pl.when(i == 0)
        def _():
            dw_ref[...] = jnp.zeros_like(dw_ref)
            dg_ref[...] = jnp.zeros_like(dg_ref)

        meta = jnp.concatenate([dm_ref[...], jnp.zeros((tm - BLOCK, 2 * KV_W), F32)], axis=0) if tm > BLOCK else dm_ref[...]
        first = (jnp.zeros((tm, 2 * KV_W), jnp.int32) + i) == 0
        dkvb = (dkv_ref[...] + jnp.where(first, meta, 0.0)).astype(BF16)
        dqb = dq_ref[...]
        dcb = dc_ref[...]
        hn, xhat, rstd = _rms_fwd(h_ref[...], g_ref[...])
        hnb = hn.astype(BF16)
        dhn = _dot_nt(dqb, w_ref[:, Q0:KV0]) + _dot_nt(dkvb, w_ref[:, KV0:C0]) + _dot_nt(dcb, w_ref[:, C0:IN_COLS])
        dw_ref[:, Q0:KV0] += _dot_tn(hnb, dqb)
        dw_ref[:, KV0:C0] += _dot_tn(hnb, dkvb)
        dw_ref[:, C0:IN_COLS] += _dot_tn(hnb, dcb)
        dx, dg_rows = _rms_bwd(dhn, xhat, rstd, g_ref[...])
        dg_ref[...] += jnp.sum(dg_rows, axis=0, keepdims=True)
        dh0_ref[...] = dh1_ref[...] + dx

    return pl.pallas_call(
        body, name="in_proj_bwd", grid=(r // tm,),
        in_specs=[_row(tm, ATTN_W), _row(tm, 2 * KV_W), _const((BLOCK, 2 * KV_W)), _row(tm, 2 * CONV_W), _row(tm, D_MODEL),
                  _row(tm, D_MODEL), _const((1, D_MODEL)), _const((D_MODEL, IN_COLS))],
        out_specs=[_row(tm, D_MODEL), _const((D_MODEL, IN_COLS)), _const((1, D_MODEL))],
        out_shape=[jax.ShapeDtypeStruct((r, D_MODEL), F32), jax.ShapeDtypeStruct((D_MODEL, IN_COLS), F32),
                   jax.ShapeDtypeStruct((1, D_MODEL), F32)],
        compiler_params=_cparams(),
    )(dq, dkv, dkv_meta, dc, dh1, h0, g1, w_in_b)


def _adamw(w, g, m, v, name):
    rows, cols = w.shape
    tr = rows
    for cand in (256, 176, 128, 64, 32, 16, 8):
        if rows % cand == 0:
            tr = cand
            break

    def body(w_ref, g_ref, m_ref, v_ref, d_ref, nm_ref, nv_ref):
        g = g_ref[...]
        m = ADAM_B1 * m_ref[...] + (1.0 - ADAM_B1) * g
        v = ADAM_B2 * v_ref[...] + (1.0 - ADAM_B2) * (g * g)
        m_hat = m / (1.0 - ADAM_B1 ** ADAM_STEP)
        v_hat = v / (1.0 - ADAM_B2 ** ADAM_STEP)
        d_ref[...] = -ADAM_LR * (m_hat / (jnp.sqrt(v_hat) + ADAM_EPS) + ADAM_WD * w_ref[...])
        nm_ref[...] = m
        nv_ref[...] = v

    spec = _row(tr, cols)
    return pl.pallas_call(
        body, name=name, grid=(rows // tr,), in_specs=[spec] * 4, out_specs=[spec] * 3,
        out_shape=[jax.ShapeDtypeStruct((rows, cols), F32)] * 3, compiler_params=_cparams(),
    )(w, g, m, v)


MESH = pl.DeviceIdType.MESH
ANY = pl.BlockSpec(memory_space=pl.ANY)


def _place():
    x, y, c = lax.axis_index("x"), lax.axis_index("y"), lax.axis_index("c")
    chips = [(1 - x, y), (x, 1 - y), (1 - x, 1 - y)]
    return x, y, c, chips


def _gather_ici(src, dst, send_sems, recv_sems):
    x, y, c, chips = _place()
    sends, arrivals = [], []
    for k in range(len(src)):
        rows = src[k].shape[0] // 2
        half = pl.ds(c * rows, rows)
        for p, chip in enumerate(chips):
            sems = dict(send_sem=send_sems.at[3 * k + p], recv_sem=recv_sems.at[3 * k + p], device_id=(chip[0], chip[1], c),
                        device_id_type=MESH)
            sends.append(pltpu.make_async_remote_copy(src_ref=src[k].at[half], dst_ref=dst[k].at[2 * x + y, half], **sems))
            theirs = dst[k].at[2 * chip[0] + chip[1], half]
            arrivals.append(pltpu.make_async_remote_copy(src_ref=theirs, dst_ref=theirs, **sems))
    return sends, arrivals


def _gather_d2d(dst, send_sems, recv_sems):
    x, y, c, chips = _place()
    sends, arrivals = [], []
    for k in range(len(dst)):
        rows = dst[k].shape[1] // 2
        for p, chip in enumerate(chips):
            sems = dict(send_sem=send_sems.at[3 * k + p], recv_sem=recv_sems.at[3 * k + p], device_id=(x, y, 1 - c),
                        device_id_type=MESH)
            mine = dst[k].at[2 * chip[0] + chip[1], pl.ds(c * rows, rows)]
            sends.append(pltpu.make_async_remote_copy(src_ref=mine, dst_ref=mine, **sems))
            theirs = dst[k].at[2 * chip[0] + chip[1], pl.ds((1 - c) * rows, rows)]
            arrivals.append(pltpu.make_async_remote_copy(src_ref=theirs, dst_ref=theirs, **sems))
    return sends, arrivals


def _own_slots(shard):
    return jnp.broadcast_to(shard[None], (N_SHARD,) + shard.shape)


def _gather_weights(shards):
    n = len(shards)

    def body(*refs):
        src, dst = refs[:n], refs[2 * n:3 * n]
        ici_send, ici_recv, d2d_send, d2d_recv = refs[3 * n:]
        sends, arrivals = _gather_ici(src, dst, ici_send, ici_recv)
        for cp in sends:
            cp.start()
        for cp in arrivals:
            cp.wait_recv()
        forwards, from_sibling = _gather_d2d(dst, d2d_send, d2d_recv)
        for cp in forwards:
            cp.start()
        for cp in from_sibling:
            cp.wait_recv()
        for cp in sends + forwards:
            cp.wait_send()

    return pl.pallas_call(
        body, name="gather_weights",
        in_specs=[ANY] * (2 * n), out_specs=[ANY] * n,
        out_shape=[jax.ShapeDtypeStruct((N_SHARD,) + s.shape, s.dtype) for s in shards],
        input_output_aliases={n + k: k for k in range(n)},
        scratch_shapes=[pltpu.SemaphoreType.DMA((3 * n,))] * 4,
    )(*shards, *[_own_slots(s) for s in shards])


def _allreduce_small(part):
    rows = part.shape[0]

    def body(x_ref, o_ref, slots, send_sems, recv_sems):
        x, y, c = lax.axis_index("x"), lax.axis_index("y"), lax.axis_index("c")
        me = 4 * x + 2 * y + c
        slots[me] = x_ref[...]

        def copy(k):
            peer = ((x + (k >> 2)) % 2, (y + ((k >> 1) & 1)) % 2, (c + (k & 1)) % 2)
            return pltpu.make_async_remote_copy(
                src_ref=x_ref, dst_ref=slots.at[me], send_sem=send_sems.at[k], recv_sem=recv_sems.at[k],
                device_id=peer, device_id_type=MESH)

        def arrival(k):
            peer = ((x + (k >> 2)) % 2, (y + ((k >> 1) & 1)) % 2, (c + (k & 1)) % 2)
            return pltpu.make_async_remote_copy(
                src_ref=x_ref, dst_ref=slots.at[4 * peer[0] + 2 * peer[1] + peer[2]], send_sem=send_sems.at[k],
                recv_sem=recv_sems.at[k], device_id=peer, device_id_type=MESH)

        for k in range(1, N_DEV):
            copy(k).start()
        for k in range(1, N_DEV):
            arrival(k).wait_recv()
        total = slots[0]
        for d in range(1, N_DEV):
            total = total + slots[d]
        o_ref[...] = total
        for k in range(1, N_DEV):
            copy(k).wait_send()

    vmem = pl.BlockSpec(memory_space=pltpu.VMEM)
    return pl.pallas_call(
        body, name="allreduce_small", in_specs=[vmem], out_specs=vmem,
        out_shape=jax.ShapeDtypeStruct((rows, 128), F32),
        scratch_shapes=[pltpu.VMEM((N_DEV, rows, 128), F32), pltpu.SemaphoreType.DMA((N_DEV,)), pltpu.SemaphoreType.DMA((N_DEV,))],
    )(part)


def _scatter(src, dst, send_sems, recv_sems):
    x, y, c = lax.axis_index("x"), lax.axis_index("y"), lax.axis_index("c")
    copies = []
    for k in range(len(src)):
        rows = src[k].shape[1] // 2
        for j in range(1, N_DEV):
            px, py, pc = (x + (j >> 2)) % 2, (y + ((j >> 1) & 1)) % 2, (c + (j & 1)) % 2
            copies.append(pltpu.make_async_remote_copy(
                src_ref=src[k].at[2 * px + py, pl.ds(pc * rows, rows)], dst_ref=dst[k].at[j - 1],
                send_sem=send_sems.at[7 * k + j - 1], recv_sem=recv_sems.at[7 * k + j - 1], device_id=(px, py, pc),
                device_id_type=MESH))
    return copies


def _scatter_landing(parts):
    return [jax.ShapeDtypeStruct((N_DEV - 1, p.shape[1] // 2, p.shape[2]), p.dtype) for p in parts]


def _scatter_call(parts):
    n = len(parts)

    def body(*refs):
        copies = _scatter(refs[:n], refs[n:2 * n], *refs[2 * n:])
        for cp in copies:
            cp.start()
        for cp in copies:
            cp.wait()

    return pl.pallas_call(
        body, name="scatter_grads", in_specs=[ANY] * n, out_specs=[ANY] * n, out_shape=_scatter_landing(parts),
        scratch_shapes=[pltpu.SemaphoreType.DMA((7 * n,)), pltpu.SemaphoreType.DMA((7 * n,))],
    )(*parts)


def _sum_pieces(own, landed):
    n = len(own)

    def body(*refs):
        for k in range(n):
            got = refs[n + k]
            total = refs[k][...]
            for j in range(N_DEV - 1):
                total = total + got[j].astype(F32)
            refs[2 * n + k][...] = total

    in_specs, out_specs = [], []
    for o in own:
        in_specs.append(_row(o.shape[0] // 2, o.shape[1]))
    for o in own:
        in_specs.append(pl.BlockSpec((N_DEV - 1, o.shape[0] // 2, o.shape[1]), lambda i: (0, i, 0)))
        out_specs.append(_row(o.shape[0] // 2, o.shape[1]))
    return pl.pallas_call(
        body, name="sum_pieces", grid=(2,), in_specs=in_specs, out_specs=out_specs,
        out_shape=[jax.ShapeDtypeStruct(o.shape, F32) for o in own], compiler_params=_cparams(),
    )(*own, *landed)


def _swap_with_sibling(halves):
    n = len(halves)

    def body(*refs):
        x, y, c = lax.axis_index("x"), lax.axis_index("y"), lax.axis_index("c")
        copies = [pltpu.make_async_remote_copy(
            src_ref=refs[k], dst_ref=refs[n + k], send_sem=refs[2 * n].at[k], recv_sem=refs[2 * n + 1].at[k],
            device_id=(x, y, 1 - c), device_id_type=MESH) for k in range(n)]
        for cp in copies:
            cp.start()
        for cp in copies:
            cp.wait()

    return pl.pallas_call(
        body, name="swap_with_sibling", in_specs=[ANY] * n, out_specs=[ANY] * n,
        out_shape=[jax.ShapeDtypeStruct(h.shape, h.dtype) for h in halves],
        scratch_shapes=[pltpu.SemaphoreType.DMA((n,)), pltpu.SemaphoreType.DMA((n,))],
    )(*halves)


def _own_piece(part):
    rows = part.shape[1] // 2
    s = 2 * lax.axis_index("x") + lax.axis_index("y")
    return lax.dynamic_slice(part, (s, lax.axis_index("c") * rows, 0), (1, rows, part.shape[2]))[0]


def _both_halves(mine, theirs):
    south = lax.axis_index("c") == 0
    return jnp.concatenate([jnp.where(south, mine, theirs), jnp.where(south, theirs, mine)], axis=0)


def _col_shards(w):
    rows, cols = w.shape
    return w.reshape(rows, N_SHARD, cols // N_SHARD).transpose(1, 0, 2)


def _from_col_shards(g):
    return g.transpose(1, 0, 2).reshape(g.shape[1], -1)


SMALL_LAYOUT = (
    ("loss", 1), ("final_norm_g", 8), ("attn_norm_g", 8), ("ffn_norm_g", 8), ("attn_out_g", 4), ("conv_out_g", 4),
    ("conv_b", 4), ("conv_ln_g", 4), ("conv_ln_b", 4), ("attn_sinks", 1), ("conv_w", 128), ("meta_tokens", 128))


def _pack_small(vals):
    rows = []
    for name, n in SMALL_LAYOUT:
        flat = vals[name].astype(F32).reshape(-1)
        rows.append(jnp.pad(flat, (0, n * 128 - flat.shape[0])).reshape(n, 128))
    total = sum(n for _, n in SMALL_LAYOUT)
    pad = (-total) % 8
    if pad:
        rows.append(jnp.zeros((pad, 128), F32))
    return jnp.concatenate(rows, axis=0)


def _unpack_small(buf, shapes):
    out, r0 = {}, 0
    for name, n in SMALL_LAYOUT:
        size = math.prod(shapes[name])
        out[name] = buf[r0:r0 + n].reshape(-1)[:size].reshape(shapes[name])
        r0 += n
    return out


def kernel(x, meta_tokens, attn_norm_g, w_in, attn_sinks, conv_w, conv_b, conv_ln_g, conv_ln_b, attn_out_g, conv_out_g, w_out, ffn_norm_g, w_gate, w_up, w_down, final_norm_g, loss_target, m_meta_tokens, m_attn_norm_g, m_w_in, m_attn_sinks, m_conv_w, m_conv_b, m_conv_ln_g, m_conv_ln_b, m_attn_out_g, m_conv_out_g, m_w_out, m_ffn_norm_g, m_w_gate, m_w_up, m_w_down, m_final_norm_g, v_meta_tokens, v_attn_norm_g, v_w_in, v_attn_sinks, v_conv_w, v_conv_b, v_conv_ln_g, v_conv_ln_b, v_attn_out_g, v_conv_out_g, v_w_out, v_ffn_norm_g, v_w_gate, v_w_up, v_w_down, v_final_norm_g):
    seq = x.shape[1]
    r = -(-(seq + BLOCK) // ROW_QUANTUM) * ROW_QUANTUM
    tail = r - BLOCK - seq
    shard = 2 * lax.axis_index("x") + lax.axis_index("y")

    conv_w32 = jnp.pad(conv_w[0], ((0, 1), (0, 0)))
    small_shard = jnp.concatenate([meta_tokens, conv_w32.reshape(16, 256)], axis=0)
    g_in, g_small = _gather_weights([w_in[0].astype(BF16), small_shard])
    later = [w_gate[0].astype(BF16), w_up[0].astype(BF16), w_out[0].astype(BF16), w_down[0].astype(BF16)]
    w_in_b = _from_col_shards(g_in)
    meta_full = _from_col_shards(g_small[:, 0:N_META])
    cw_full = _from_col_shards(g_small[:, N_META:].reshape(N_SHARD, 32, 128))

    g1, ga, gc, g2 = attn_norm_g, attn_out_g, conv_out_g, ffn_norm_g
    gf = final_norm_g.reshape(1, D_MODEL)
    sinks = attn_sinks[0]

    h0 = jnp.concatenate([jnp.zeros((LEAD, D_MODEL), F32), meta_full, x[0], jnp.zeros((tail, D_MODEL), F32)], axis=0)
    tgt = jnp.pad(loss_target[0], ((BLOCK, tail), (0, 0)))
    q, kv, cacg = _in_proj(h0, g1, w_in_b, 768)
    oa, lse, *gathered = _attn_fwd(q, kv, sinks, later, [_own_slots(s) for s in later])
    oc, g_gate, g_up, g_out, g_down = _conv_fwd(cacg, cw_full, conv_b, conv_ln_g, conv_ln_b, 384, gathered)
    wg_b = _from_col_shards(g_gate)
    wu_b = _from_col_shards(g_up)
    w_out_b = g_out.reshape(D_MODEL, D_MODEL)
    wd_b = g_down.reshape(D_FF, D_MODEL)
    h1, hn2 = _out_proj(oa, oc, h0, ga, gc, g2, w_out_b, 768)
    gate, up, dh2, dh2b, loss_p, dgf = _ffn_fwd(hn2, h1, tgt, gf, wg_b, wu_b, wd_b, seq, 384)

    dgate, dup, dh1, dg2 = _ffn_bwd(dh2, dh2b, gate, up, h1, g2, wg_b, wu_b, wd_b, 384)
    dwg, dwu = _ffn_wgrad_gu(hn2, dgate, dup, 768)
    dwd = _ffn_wgrad_d(gate, up, dh2b, 768)
    doa, doc, dwo, dga, dgc = _out_proj_bwd(dh1, oa, oc, ga, gc, w_out_b, 768)
    p_gate, p_up = _col_shards(dwg), _col_shards(dwu)
    dy, dcw, dcb, dlg, dlb, l_gate, l_up = _conv_bwd_params(
        doc, cacg, cw_full, conv_b, conv_ln_g, conv_ln_b, 384, [p_gate.astype(BF16), p_up.astype(BF16)])
    dc = _conv_bwd_data(dy, cacg, cw_full, 384)
    p_out, p_down = dwo.reshape(N_SHARD, D_MODEL // N_SHARD, D_MODEL), dwd.reshape(N_SHARD, D_FF // N_SHARD, D_MODEL)
    dq, dkv, dkv_meta, dsink, l_out, l_down = _attn_bwd(q, kv, oa, doa, lse, sinks, [p_out.astype(BF16), p_down.astype(BF16)])
    dh0, dwi, dg1 = _in_proj_bwd(dq, dkv, dkv_meta, dc, dh1, h0, g1, w_in_b, 768)
    grad_x = dh0[BLOCK:BLOCK + seq][None]
    p_in = _col_shards(dwi)
    l_in, = _scatter_call([p_in.astype(BF16)])

    small = _allreduce_small(_pack_small({
        "loss": loss_p[0, 0:1], "final_norm_g": dgf, "attn_norm_g": dg1, "ffn_norm_g": dg2, "attn_out_g": dga, "conv_out_g": dgc,
        "conv_b": dcb, "conv_ln_g": dlg, "conv_ln_b": dlb, "attn_sinks": dsink[:, 0], "conv_w": dcw[0:CONV_K],
        "meta_tokens": dh0[LEAD:BLOCK]}))
    red = _unpack_small(small, {
        "loss": (), "final_norm_g": (D_MODEL,), "attn_norm_g": (1, D_MODEL), "ffn_norm_g": (1, D_MODEL), "attn_out_g": (1, ATTN_W),
        "conv_out_g": (1, CONV_W), "conv_b": (1, CONV_W), "conv_ln_g": (1, CONV_W), "conv_ln_b": (1, CONV_W),
        "attn_sinks": (1, N_HEADS), "conv_w": (CONV_K, CONV_W), "meta_tokens": (N_META, D_MODEL)})
    loss = red["loss"]
    g_meta = lax.dynamic_slice_in_dim(red["meta_tokens"], shard * (D_MODEL // N_SHARD), D_MODEL // N_SHARD, axis=1)
    g_convw = lax.dynamic_slice_in_dim(red["conv_w"], shard * (CONV_W // N_SHARD), CONV_W // N_SHARD, axis=1)[None]

    halves = _sum_pieces([_own_piece(p) for p in (p_in, p_gate, p_up, p_out, p_down)], [l_in, l_gate, l_up, l_out, l_down])
    g_w_in, g_w_gate, g_w_up, g_w_out, g_w_down = [
        _both_halves(mine, theirs) for mine, theirs in zip(halves, _swap_with_sibling(halves))]

    grads = {
        "meta_tokens": g_meta, "attn_norm_g": red["attn_norm_g"], "w_in": g_w_in[None], "attn_sinks": red["attn_sinks"],
        "conv_w": g_convw, "conv_b": red["conv_b"], "conv_ln_g": red["conv_ln_g"], "conv_ln_b": red["conv_ln_b"],
        "attn_out_g": red["attn_out_g"], "conv_out_g": red["conv_out_g"], "w_out": g_w_out[None], "ffn_norm_g": red["ffn_norm_g"],
        "w_gate": g_w_gate[None], "w_up": g_w_up[None], "w_down": g_w_down[None], "final_norm_g": red["final_norm_g"]}
    params = {
        "meta_tokens": (meta_tokens, m_meta_tokens, v_meta_tokens), "attn_norm_g": (attn_norm_g, m_attn_norm_g, v_attn_norm_g),
        "w_in": (w_in, m_w_in, v_w_in), "attn_sinks": (attn_sinks, m_attn_sinks, v_attn_sinks), "conv_w": (conv_w, m_conv_w, v_conv_w),
        "conv_b": (conv_b, m_conv_b, v_conv_b), "conv_ln_g": (conv_ln_g, m_conv_ln_g, v_conv_ln_g),
        "conv_ln_b": (conv_ln_b, m_conv_ln_b, v_conv_ln_b), "attn_out_g": (attn_out_g, m_attn_out_g, v_attn_out_g),
        "conv_out_g": (conv_out_g, m_conv_out_g, v_conv_out_g), "w_out": (w_out, m_w_out, v_w_out),
        "ffn_norm_g": (ffn_norm_g, m_ffn_norm_g, v_ffn_norm_g), "w_gate": (w_gate, m_w_gate, v_w_gate), "w_up": (w_up, m_w_up, v_w_up),
        "w_down": (w_down, m_w_down, v_w_down), "final_norm_g": (final_norm_g, m_final_norm_g, v_final_norm_g)}
    names = list(params)
    big = ("w_in", "w_out", "w_gate", "w_up", "w_down")
    delta, new_m, new_v = {}, {}, {}
    for name in big:
        w, m, v = params[name]
        d, nm, nv = _adamw(w[0], grads[name][0], m[0], v[0], "adamw_" + name)
        delta[name], new_m[name], new_v[name] = d[None], nm[None], nv[None]
    rest = [name for name in names if name not in big]

    def pack(arrs):
        flat = jnp.concatenate([a.reshape(-1) for a in arrs])
        return jnp.pad(flat, (0, (-flat.shape[0]) % 1024)).reshape(-1, 128)

    packed = _adamw(pack([params[n][0] for n in rest]), pack([grads[n] for n in rest]), pack([params[n][1] for n in rest]),
                    pack([params[n][2] for n in rest]), "adamw_small")
    off = 0
    for name in rest:
        shape = params[name][0].shape
        size = math.prod(shape)
        for dst, buf in zip((delta, new_m, new_v), packed):
            dst[name] = buf.reshape(-1)[off:off + size].reshape(shape)
        off += size

    return (loss, grad_x, *[grads[n] for n in names], *[delta[n] for n in names], *[new_m[n] for n in names],
            *[new_v[n] for n in names])
```

```python
import functools
import math

import jax
import jax.numpy as jnp
from jax import lax
from jax.experimental import pallas as pl
from jax.experimental.pallas import tpu as pltpu

F32 = jnp.float32
BF16 = jnp.bfloat16

D_MODEL = 1024
N_META = 16
ATTN_W = 512
CONV_W = 512
HEAD_DIM = 64
N_HEADS = 8
N_KV = 2
GROUP = N_HEADS // N_KV
KV_W = N_KV * HEAD_DIM
BLOCK = 128
LEAD = BLOCK - N_META
CONV_K = 31
D_FF = 2816
IN_COLS = ATTN_W + 2 * KV_W + 2 * CONV_W
Q0, KV0, C0 = 0, ATTN_W, ATTN_W + 2 * KV_W
NORM_EPS = 1e-5
SCALE = 1.0 / math.sqrt(HEAD_DIM)
SLOPES = tuple(2.0 ** (-(8.0 / N_HEADS) * (h + 1)) for h in range(N_HEADS))
NEG = -1e30

ADAM_LR, ADAM_B1, ADAM_B2, ADAM_EPS, ADAM_WD, ADAM_STEP = 0.001, 0.9, 0.999, 1e-08, 0.01, 10

N_SHARD = 4
N_DEV = 8
ROW_QUANTUM = 768
HALO = 32
CONV_CHUNK = 32
FF_CHUNK = 256
VMEM_LIMIT = 60 * 1024 * 1024


def _cparams(n_axes=1):
    return pltpu.CompilerParams(dimension_semantics=("arbitrary",) * n_axes, vmem_limit_bytes=VMEM_LIMIT)


def _dot(a, b):
    return jnp.dot(a, b, preferred_element_type=F32)


def _dot_nt(a, b):
    return lax.dot_general(a, b, (((1,), (1,)), ((), ())), preferred_element_type=F32)


def _dot_tn(a, b):
    return lax.dot_general(a, b, (((0,), (0,)), ((), ())), preferred_element_type=F32)


def _sigmoid(x):
    return 1.0 / (1.0 + jnp.exp(-x))


def _row(tm, n):
    return pl.BlockSpec((tm, n), lambda i: (i, 0))


def _const(shape):
    return pl.BlockSpec(shape, lambda i: (0,) * len(shape))


def _resident(shape):
    return pl.BlockSpec(shape, lambda i: (0,) * len(shape), pipeline_mode=pl.Buffered(1))


def _rms_fwd(x, g):
    rstd = lax.rsqrt(jnp.mean(x * x, axis=-1, keepdims=True) + NORM_EPS)
    xhat = x * rstd
    return xhat * g, xhat, rstd


def _rms_bwd(dy, xhat, rstd, g):
    dxh = dy * g
    dx = rstd * (dxh - xhat * jnp.mean(dxh * xhat, axis=-1, keepdims=True))
    return dx, dy * xhat


def _in_proj(h0, g1, w_in_b, tm):
    r = h0.shape[0]

    def body(h_ref, g_ref, w_ref, q_ref, kv_ref, c_ref):
        hn = _rms_fwd(h_ref[...], g_ref[...])[0].astype(BF16)
        q_ref[...] = _dot(hn, w_ref[:, Q0:KV0]).astype(BF16)
        kv_ref[...] = _dot(hn, w_ref[:, KV0:C0]).astype(BF16)
        c_ref[...] = _dot(hn, w_ref[:, C0:IN_COLS])

    return pl.pallas_call(
        body, name="in_proj", grid=(r // tm,),
        in_specs=[_row(tm, D_MODEL), _const((1, D_MODEL)), _const((D_MODEL, IN_COLS))],
        out_specs=[_row(tm, ATTN_W), _row(tm, 2 * KV_W), _row(tm, 2 * CONV_W)],
        out_shape=[jax.ShapeDtypeStruct((r, ATTN_W), BF16), jax.ShapeDtypeStruct((r, 2 * KV_W), BF16),
                   jax.ShapeDtypeStruct((r, 2 * CONV_W), F32)],
        compiler_params=_cparams(),
    )(h0, g1, w_in_b)


def _attn_bias_init(bias_ref, late_ref):
    row = lax.broadcasted_iota(jnp.int32, (GROUP * BLOCK, BLOCK), 0) & (BLOCK - 1)
    col = lax.broadcasted_iota(jnp.int32, (GROUP * BLOCK, BLOCK), 1)
    late_ref[...] = jnp.where(col > row, 1.0, 0.0)
    for g in range(N_KV):
        slope = jnp.concatenate([jnp.zeros((BLOCK, BLOCK), F32) + SLOPES[g * GROUP + j] for j in range(GROUP)], axis=0)
        bias_ref[g, :, 0:BLOCK] = jnp.where(col >= LEAD, 0.0, NEG)
        bias_ref[g, :, BLOCK:2 * BLOCK] = jnp.where(col > row, -slope * (row - col + BLOCK).astype(F32), NEG)
        bias_ref[g, :, 2 * BLOCK:3 * BLOCK] = jnp.where(col <= row, -slope * (row - col).astype(F32), NEG)


def _attn_bias(bias_ref, late_ref, g, i):
    meta0 = jnp.where(i == 0, NEG, 0.0)
    no_prev = jnp.where(i >= 2, 0.0, NEG)
    no_cur = jnp.where(i >= 1, 0.0, NEG)
    return jnp.concatenate([bias_ref[g, :, 0:BLOCK] + late_ref[...] * meta0, bias_ref[g, :, BLOCK:2 * BLOCK] + no_prev,
                            bias_ref[g, :, 2 * BLOCK:3 * BLOCK] + no_cur], axis=1)


def _head_rows(vals):
    return jnp.concatenate([jnp.zeros((BLOCK, BLOCK), F32) + v for v in vals], axis=0)


def _stack_heads(ref, g):
    return jnp.concatenate([ref[:, (g * GROUP + j) * HEAD_DIM:(g * GROUP + j + 1) * HEAD_DIM] for j in range(GROUP)], axis=0)


def _kv_cat(kvm_ref, kvp_ref, kvc_ref, g):
    ks = slice(g * HEAD_DIM, (g + 1) * HEAD_DIM)
    vs = slice(KV_W + g * HEAD_DIM, KV_W + (g + 1) * HEAD_DIM)
    kcat = jnp.concatenate([kvm_ref[:, ks], kvp_ref[:, ks], kvc_ref[:, ks]], axis=0)
    vcat = jnp.concatenate([kvm_ref[:, vs], kvp_ref[:, vs], kvc_ref[:, vs]], axis=0)
    return kcat, vcat


def _attn_fwd(q, kv, sinks, shards, gathered):
    r = q.shape[0]
    nb = r // BLOCK
    n = len(shards)

    def body(sink_ref, q_ref, kvc_ref, kvp_ref, kvm_ref, *rest):
        src = rest[:n]
        o_ref, lse_ref = rest[2 * n:2 * n + 2]
        dst = rest[2 * n + 2:3 * n + 2]
        bias_ref, late_ref, send_sems, recv_sems = rest[3 * n + 2:]
        i = pl.program_id(0)

        @pl.when(i == 0)
        def _():
            for cp in _gather_ici(src, dst, send_sems, recv_sems)[0]:
                cp.start()
            _attn_bias_init(bias_ref, late_ref)

        lane = lax.broadcasted_iota(jnp.int32, (BLOCK, BLOCK), 1)
        lse_tile = jnp.zeros((BLOCK, BLOCK), F32)
        for g in range(N_KV):
            kcat, vcat = _kv_cat(kvm_ref, kvp_ref, kvc_ref, g)
            heads = range(g * GROUP, (g + 1) * GROUP)
            s = _dot_nt(_stack_heads(q_ref, g), kcat) * SCALE + _attn_bias(bias_ref, late_ref, g, i)
            segs = [s[:, k * BLOCK:(k + 1) * BLOCK] for k in range(3)]
            sink = _head_rows([sink_ref[h] for h in heads])
            m = jnp.maximum(jnp.max(jnp.maximum(jnp.maximum(segs[0], segs[1]), segs[2]), axis=-1, keepdims=True), sink)
            ps = [jnp.exp(x - m) for x in segs]
            l = jnp.sum(ps[0] + ps[1] + ps[2], axis=-1, keepdims=True) + jnp.exp(sink - m)
            o = _dot(jnp.concatenate(ps, axis=1).astype(BF16), vcat) * (1.0 / l)[:, 0:HEAD_DIM]
            lse = m + jnp.log(l)
            for j, h in enumerate(heads):
                o_ref[:, h * HEAD_DIM:(h + 1) * HEAD_DIM] = o[j * BLOCK:(j + 1) * BLOCK]
                lse_tile = jnp.where(lane == h, lse[j * BLOCK:(j + 1) * BLOCK], lse_tile)
        lse_ref[...] = lse_tile

        @pl.when(i == nb - 1)
        def _():
            sends, arrivals = _gather_ici(src, dst, send_sems, recv_sems)
            for cp in arrivals:
                cp.wait_recv()
            for cp in sends:
                cp.wait_send()

    return pl.pallas_call(
        body, name="attn_fwd", grid=(nb,),
        in_specs=[pl.BlockSpec(memory_space=pltpu.SMEM), _row(BLOCK, ATTN_W), _row(BLOCK, 2 * KV_W),
                  pl.BlockSpec((BLOCK, 2 * KV_W), lambda i: (jnp.maximum(i - 1, 0), 0)), _const((BLOCK, 2 * KV_W))] + [ANY] * (2 * n),
        out_specs=[_row(BLOCK, ATTN_W), _row(BLOCK, BLOCK)] + [ANY] * n,
        out_shape=[jax.ShapeDtypeStruct((r, ATTN_W), F32), jax.ShapeDtypeStruct((r, BLOCK), F32)]
        + [jax.ShapeDtypeStruct(g.shape, g.dtype) for g in gathered],
        input_output_aliases={5 + n + k: 2 + k for k in range(n)},
        scratch_shapes=[pltpu.VMEM((N_KV, GROUP * BLOCK, 3 * BLOCK), F32), pltpu.VMEM((GROUP * BLOCK, BLOCK), F32),
                        pltpu.SemaphoreType.DMA((3 * n,)), pltpu.SemaphoreType.DMA((3 * n,))],
        compiler_params=_cparams(),
    )(sinks, q, kv, kv, kv, *shards, *gathered)


def _shifted_copies(ub_ref, win):
    w = win.shape[0]
    ub_ref[0] = win
    for b in range(1, 8):
        ub_ref[b] = pltpu.roll(win, shift=w - b, axis=0)


def _conv_chunk(ub_ref, w_ref, r0, shifts):
    acc = jnp.zeros((CONV_CHUNK, CONV_W), F32)
    for j in range(CONV_K):
        a, b = divmod(shifts[j], 8)
        acc = acc + w_ref[j:j + 1, :] * ub_ref[b, pl.ds(r0 + 8 * a, CONV_CHUNK), :]
    return acc


FWD_SHIFTS = tuple(HALO - (CONV_K - 1) + j for j in range(CONV_K))
BWD_SHIFTS = tuple(CONV_K - 1 - j for j in range(CONV_K))


def _glu_window(cp_ref, c_ref, i):
    tile = c_ref[:, 0:CONV_W] * _sigmoid(c_ref[:, CONV_W:2 * CONV_W])
    halo = cp_ref[:, 0:CONV_W] * _sigmoid(cp_ref[:, CONV_W:2 * CONV_W])
    first = (jnp.zeros((HALO, CONV_W), jnp.int32) + i) == 0
    return jnp.concatenate([jnp.where(first, 0.0, halo), tile], axis=0)


def _halo_before(tm, n):
    return pl.BlockSpec((HALO, n), lambda i: (jnp.maximum(i * (tm // HALO) - 1, 0), 0))


def _conv_fwd(cacg, cw, cb, lg, lb, tm, gathered):
    r = cacg.shape[0]
    n = len(gathered)

    def body(c_ref, cp_ref, w_ref, cb_ref, lg_ref, lb_ref, *rest):
        o_ref = rest[n]
        dst = rest[n + 1:2 * n + 1]
        ub_ref, send_sems, recv_sems = rest[2 * n + 1:]
        i = pl.program_id(0)

        @pl.when(i == 0)
        def _():
            for cp in _gather_d2d(dst, send_sems, recv_sems)[0]:
                cp.start()

        _shifted_copies(ub_ref, _glu_window(cp_ref, c_ref, i))

        def chunk(ci, carry):
            r0 = pl.multiple_of(ci * CONV_CHUNK, CONV_CHUNK)
            y = _conv_chunk(ub_ref, w_ref, r0, FWD_SHIFTS) + cb_ref[...]
            yc = y - jnp.mean(y, axis=-1, keepdims=True)
            rs = lax.rsqrt(jnp.mean(yc * yc, axis=-1, keepdims=True) + NORM_EPS)
            yn = yc * rs * lg_ref[...] + lb_ref[...]
            o_ref[pl.ds(r0, CONV_CHUNK), :] = yn * _sigmoid(yn)
            return carry

        lax.fori_loop(0, tm // CONV_CHUNK, chunk, 0)

        @pl.when(i == r // tm - 1)
        def _():
            sends, arrivals = _gather_d2d(dst, send_sems, recv_sems)
            for cp in arrivals:
                cp.wait_recv()
            for cp in sends:
                cp.wait_send()

    return pl.pallas_call(
        body, name="conv_fwd", grid=(r // tm,),
        in_specs=[_row(tm, 2 * CONV_W), _halo_before(tm, 2 * CONV_W), _const((32, CONV_W)), _const((1, CONV_W)),
                  _const((1, CONV_W)), _const((1, CONV_W))] + [ANY] * n,
        out_specs=[_row(tm, CONV_W)] + [ANY] * n,
        out_shape=[jax.ShapeDtypeStruct((r, CONV_W), F32)] + [jax.ShapeDtypeStruct(g.shape, g.dtype) for g in gathered],
        input_output_aliases={6 + k: 1 + k for k in range(n)},
        scratch_shapes=[pltpu.VMEM((8, tm + HALO, CONV_W), F32), pltpu.SemaphoreType.DMA((3 * n,)), pltpu.SemaphoreType.DMA((3 * n,))],
        compiler_params=_cparams(),
    )(cacg, cacg, cw, cb, lg, lb, *gathered)


def _out_proj(oa, oc, h0, ga, gc, g2, w_out_b, tm):
    r = h0.shape[0]

    def body(oa_ref, oc_ref, h_ref, ga_ref, gc_ref, g2_ref, w_ref, h1_ref, hn2_ref):
        ma = _rms_fwd(oa_ref[...], ga_ref[...])[0].astype(BF16)
        mc = _rms_fwd(oc_ref[...], gc_ref[...])[0].astype(BF16)
        h1 = h_ref[...] + _dot(ma, w_ref[0:ATTN_W, :]) + _dot(mc, w_ref[ATTN_W:ATTN_W + CONV_W, :])
        h1_ref[...] = h1
        hn2_ref[...] = _rms_fwd(h1, g2_ref[...])[0].astype(BF16)

    return pl.pallas_call(
        body, name="out_proj", grid=(r // tm,),
        in_specs=[_row(tm, ATTN_W), _row(tm, CONV_W), _row(tm, D_MODEL), _const((1, ATTN_W)), _const((1, CONV_W)),
                  _const((1, D_MODEL)), _const((D_MODEL, D_MODEL))],
        out_specs=[_row(tm, D_MODEL), _row(tm, D_MODEL)],
        out_shape=[jax.ShapeDtypeStruct((r, D_MODEL), F32), jax.ShapeDtypeStruct((r, D_MODEL), BF16)],
        compiler_params=_cparams(),
    )(oa, oc, h0, ga, gc, g2, w_out_b)


def _ffn_fwd(hn2, h1, tgt, gf, wg_b, wu_b, wd_b, seq, tm):
    r = h1.shape[0]

    def body(hn_ref, h1_ref, t_ref, gf_ref, wg_ref, wu_ref, wd_ref, gate_ref, up_ref, dh2_ref, dh2b_ref, loss_ref, dgf_ref, acc_ref):
        i = pl.program_id(0)

        @pl.when(i == 0)
        def _():
            loss_ref[...] = jnp.zeros_like(loss_ref)
            dgf_ref[...] = jnp.zeros_like(dgf_ref)

        hn = hn_ref[...]
        acc_ref[...] = h1_ref[...]
        for ch in range(D_FF // FF_CHUNK):
            cs = slice(ch * FF_CHUNK, (ch + 1) * FF_CHUNK)
            gate = _dot(hn, wg_ref[:, cs])
            up = _dot(hn, wu_ref[:, cs])
            gate_ref[:, cs] = gate.astype(BF16)
            up_ref[:, cs] = up.astype(BF16)
            act = (gate * _sigmoid(gate) * up).astype(BF16)
            acc_ref[...] += _dot(act, wd_ref[cs, :])
        y, xhat, rstd = _rms_fwd(acc_ref[...], gf_ref[...])
        rows = lax.broadcasted_iota(jnp.int32, (tm, D_MODEL), 0) + i * tm
        real = (rows >= BLOCK) & (rows < BLOCK + seq)
        err = jnp.where(real, y - t_ref[...], 0.0)
        loss_ref[...] += jnp.sum(err * err) * (0.5 / D_MODEL)
        dy = err * (1.0 / D_MODEL)
        dh2, dg_rows = _rms_bwd(dy, xhat, rstd, gf_ref[...])
        dgf_ref[...] += jnp.sum(dg_rows, axis=0, keepdims=True)
        dh2_ref[...] = dh2
        dh2b_ref[...] = dh2.astype(BF16)

    return pl.pallas_call(
        body, name="ffn_fwd", grid=(r // tm,),
        in_specs=[_row(tm, D_MODEL), _row(tm, D_MODEL), _row(tm, D_MODEL), _const((1, D_MODEL)),
                  _resident((D_MODEL, D_FF)), _resident((D_MODEL, D_FF)), _resident((D_FF, D_MODEL))],
        out_specs=[_row(tm, D_FF), _row(tm, D_FF), _row(tm, D_MODEL), _row(tm, D_MODEL), _const((1, BLOCK)), _const((1, D_MODEL))],
        out_shape=[jax.ShapeDtypeStruct((r, D_FF), BF16), jax.ShapeDtypeStruct((r, D_FF), BF16),
                   jax.ShapeDtypeStruct((r, D_MODEL), F32), jax.ShapeDtypeStruct((r, D_MODEL), BF16),
                   jax.ShapeDtypeStruct((1, BLOCK), F32), jax.ShapeDtypeStruct((1, D_MODEL), F32)],
        scratch_shapes=[pltpu.VMEM((tm, D_MODEL), F32)],
        compiler_params=_cparams(),
    )(hn2, h1, tgt, gf, wg_b, wu_b, wd_b)


def _ffn_bwd(dh2, dh2b, gate, up, h1, g2, wg_b, wu_b, wd_b, tm):
    r = h1.shape[0]

    def body(dh2_ref, dh2b_ref, gate_ref, up_ref, h1_ref, g2_ref, wg_ref, wu_ref, wd_ref, dgate_ref, dup_ref, dh1_ref, dg2_ref, acc_ref):
        @pl.when(pl.program_id(0) == 0)
        def _():
            dg2_ref[...] = jnp.zeros_like(dg2_ref)

        dyb = dh2b_ref[...]
        acc_ref[...] = jnp.zeros_like(acc_ref)
        for ch in range(D_FF // FF_CHUNK):
            cs = slice(ch * FF_CHUNK, (ch + 1) * FF_CHUNK)
            dact = _dot_nt(dyb, wd_ref[cs, :])
            gate = gate_ref[:, cs].astype(F32)
            up = up_ref[:, cs].astype(F32)
            sg = _sigmoid(gate)
            silu = gate * sg
            dgate = (dact * up * (sg * (1.0 + gate * (1.0 - sg)))).astype(BF16)
            dup = (dact * silu).astype(BF16)
            dgate_ref[:, cs] = dgate
            dup_ref[:, cs] = dup
            acc_ref[...] += _dot_nt(dgate, wg_ref[:, cs]) + _dot_nt(dup, wu_ref[:, cs])
        _, xhat, rstd = _rms_fwd(h1_ref[...], g2_ref[...])
        dx, dg_rows = _rms_bwd(acc_ref[...], xhat, rstd, g2_ref[...])
        dg2_ref[...] += jnp.sum(dg_rows, axis=0, keepdims=True)
        dh1_ref[...] = dh2_ref[...] + dx

    return pl.pallas_call(
        body, name="ffn_bwd", grid=(r // tm,),
        in_specs=[_row(tm, D_MODEL), _row(tm, D_MODEL), _row(tm, D_FF), _row(tm, D_FF), _row(tm, D_MODEL), _const((1, D_MODEL)),
                  _resident((D_MODEL, D_FF)), _resident((D_MODEL, D_FF)), _resident((D_FF, D_MODEL))],
        out_specs=[_row(tm, D_FF), _row(tm, D_FF), _row(tm, D_MODEL), _const((1, D_MODEL))],
        out_shape=[jax.ShapeDtypeStruct((r, D_FF), BF16), jax.ShapeDtypeStruct((r, D_FF), BF16),
                   jax.ShapeDtypeStruct((r, D_MODEL), F32), jax.ShapeDtypeStruct((1, D_MODEL), F32)],
        scratch_shapes=[pltpu.VMEM((tm, D_MODEL), F32)],
        compiler_params=_cparams(),
    )(dh2, dh2b, gate, up, h1, g2, wg_b, wu_b, wd_b)


FF_HALF = D_FF // 2


def _ffn_wgrad_gu(hn2, dgate, dup, tk):
    r = hn2.shape[0]

    def body(hn_ref, dg_ref, du_ref, wg_ref, wu_ref):
        @pl.when(pl.program_id(1) == 0)
        def _():
            wg_ref[...] = jnp.zeros_like(wg_ref)
            wu_ref[...] = jnp.zeros_like(wu_ref)

        hn = hn_ref[...]
        wg_ref[...] += _dot_tn(hn, dg_ref[...])
        wu_ref[...] += _dot_tn(hn, du_ref[...])

    col = pl.BlockSpec((tk, FF_HALF), lambda j, k: (k, j))
    out = pl.BlockSpec((D_MODEL, FF_HALF), lambda j, k: (0, j))
    return pl.pallas_call(
        body, name="ffn_wgrad_gu", grid=(2, r // tk),
        in_specs=[pl.BlockSpec((tk, D_MODEL), lambda j, k: (k, 0)), col, col],
        out_specs=[out, out],
        out_shape=[jax.ShapeDtypeStruct((D_MODEL, D_FF), F32)] * 2,
        compiler_params=_cparams(2),
    )(hn2, dgate, dup)


def _ffn_wgrad_d(gate, up, dh2b, tk):
    r = gate.shape[0]

    def body(g_ref, u_ref, dy_ref, wd_ref):
        @pl.when(pl.program_id(1) == 0)
        def _():
            wd_ref[...] = jnp.zeros_like(wd_ref)

        gate = g_ref[...].astype(F32)
        act = (gate * _sigmoid(gate) * u_ref[...].astype(F32)).astype(BF16)
        wd_ref[...] += _dot_tn(act, dy_ref[...])

    col = pl.BlockSpec((tk, FF_HALF), lambda j, k: (k, j))
    return pl.pallas_call(
        body, name="ffn_wgrad_d", grid=(2, r // tk),
        in_specs=[col, col, pl.BlockSpec((tk, D_MODEL), lambda j, k: (k, 0))],
        out_specs=pl.BlockSpec((FF_HALF, D_MODEL), lambda j, k: (j, 0)),
        out_shape=jax.ShapeDtypeStruct((D_FF, D_MODEL), F32),
        compiler_params=_cparams(2),
    )(gate, up, dh2b)


def _out_proj_bwd(dh1, oa, oc, ga, gc, w_out_b, tm):
    r = dh1.shape[0]

    def body(dh_ref, oa_ref, oc_ref, ga_ref, gc_ref, w_ref, doa_ref, doc_ref, dw_ref, dga_ref, dgc_ref):
        @pl.when(pl.program_id(0) == 0)
        def _():
            dw_ref[...] = jnp.zeros_like(dw_ref)
            dga_ref[...] = jnp.zeros_like(dga_ref)
            dgc_ref[...] = jnp.zeros_like(dgc_ref)

        dhb = dh_ref[...].astype(BF16)
        dmix = _dot_nt(dhb, w_ref[...])
        ma, xa, ra = _rms_fwd(oa_ref[...], ga_ref[...])
        mc, xc, rc = _rms_fwd(oc_ref[...], gc_ref[...])
        dw_ref[0:ATTN_W, :] += _dot_tn(ma.astype(BF16), dhb)
        dw_ref[ATTN_W:ATTN_W + CONV_W, :] += _dot_tn(mc.astype(BF16), dhb)
        doa, dga_rows = _rms_bwd(dmix[:, 0:ATTN_W], xa, ra, ga_ref[...])
        doc, dgc_rows = _rms_bwd(dmix[:, ATTN_W:ATTN_W + CONV_W], xc, rc, gc_ref[...])
        doa_ref[...] = doa
        doc_ref[...] = doc
        dga_ref[...] += jnp.sum(dga_rows, axis=0, keepdims=True)
        dgc_ref[...] += jnp.sum(dgc_rows, axis=0, keepdims=True)

    return pl.pallas_call(
        body, name="out_proj_bwd", grid=(r // tm,),
        in_specs=[_row(tm, D_MODEL), _row(tm, ATTN_W), _row(tm, CONV_W), _const((1, ATTN_W)), _const((1, CONV_W)),
                  _const((D_MODEL, D_MODEL))],
        out_specs=[_row(tm, ATTN_W), _row(tm, CONV_W), _const((D_MODEL, D_MODEL)), _const((1, ATTN_W)), _const((1, CONV_W))],
        out_shape=[jax.ShapeDtypeStruct((r, ATTN_W), F32), jax.ShapeDtypeStruct((r, CONV_W), F32),
                   jax.ShapeDtypeStruct((D_MODEL, D_MODEL), F32), jax.ShapeDtypeStruct((1, ATTN_W), F32),
                   jax.ShapeDtypeStruct((1, CONV_W), F32)],
        compiler_params=_cparams(),
    )(dh1, oa, oc, ga, gc, w_out_b)


def _conv_bwd_params(doc, cacg, cw, cb, lg, lb, tm, parts):
    r = cacg.shape[0]
    n_steps = r // tm
    n = len(parts)

    def body(do_ref, c_ref, cp_ref, w_ref, cb_ref, lg_ref, lb_ref, *rest):
        src = rest[:n]
        dy_ref, dcw_ref, dcb_ref, dlg_ref, dlb_ref = rest[n:n + 5]
        dst = rest[n + 5:2 * n + 5]
        ub_ref, accw_ref, send_sems, recv_sems = rest[2 * n + 5:]
        i = pl.program_id(0)

        @pl.when(i == 0)
        def _():
            for cp in _scatter(src, dst, send_sems, recv_sems):
                cp.start()
            accw_ref[...] = jnp.zeros_like(accw_ref)
            dcb_ref[...] = jnp.zeros_like(dcb_ref)
            dlg_ref[...] = jnp.zeros_like(dlg_ref)
            dlb_ref[...] = jnp.zeros_like(dlb_ref)

        _shifted_copies(ub_ref, _glu_window(cp_ref, c_ref, i))

        def chunk(ci, carry):
            r0 = pl.multiple_of(ci * CONV_CHUNK, CONV_CHUNK)
            y = _conv_chunk(ub_ref, w_ref, r0, FWD_SHIFTS) + cb_ref[...]
            yc = y - jnp.mean(y, axis=-1, keepdims=True)
            rs = lax.rsqrt(jnp.mean(yc * yc, axis=-1, keepdims=True) + NORM_EPS)
            xhat = yc * rs
            yn = xhat * lg_ref[...] + lb_ref[...]
            sg = _sigmoid(yn)
            dyn = do_ref[pl.ds(r0, CONV_CHUNK), :] * (sg * (1.0 + yn * (1.0 - sg)))
            dlg_ref[...] += jnp.sum(dyn * xhat, axis=0, keepdims=True)
            dlb_ref[...] += jnp.sum(dyn, axis=0, keepdims=True)
            dxh = dyn * lg_ref[...]
            dy = rs * (dxh - jnp.mean(dxh, axis=-1, keepdims=True) - xhat * jnp.mean(dxh * xhat, axis=-1, keepdims=True))
            dcb_ref[...] += jnp.sum(dy, axis=0, keepdims=True)
            dy_ref[pl.ds(r0, CONV_CHUNK), :] = dy
            for j in range(CONV_K):
                a, b = divmod(FWD_SHIFTS[j], 8)
                prod = dy * ub_ref[b, pl.ds(r0 + 8 * a, CONV_CHUNK), :]
                accw_ref[j] += jnp.sum(prod.reshape(CONV_CHUNK // 8, 8, CONV_W), axis=0)
            return carry

        lax.fori_loop(0, tm // CONV_CHUNK, chunk, 0)

        @pl.when(i == n_steps - 1)
        def _():
            for j in range(32):
                dcw_ref[j:j + 1, :] = jnp.sum(accw_ref[j], axis=0, keepdims=True)
            for cp in _scatter(src, dst, send_sems, recv_sems):
                cp.wait()

    vec = _const((1, CONV_W))
    return pl.pallas_call(
        body, name="conv_bwd_params", grid=(n_steps,),
        in_specs=[_row(tm, CONV_W), _row(tm, 2 * CONV_W), _halo_before(tm, 2 * CONV_W), _const((32, CONV_W)), vec, vec, vec]
        + [ANY] * n,
        out_specs=[_row(tm, CONV_W), _const((32, CONV_W)), vec, vec, vec] + [ANY] * n,
        out_shape=[jax.ShapeDtypeStruct((r, CONV_W), F32), jax.ShapeDtypeStruct((32, CONV_W), F32)]
        + [jax.ShapeDtypeStruct((1, CONV_W), F32)] * 3 + _scatter_landing(parts),
        scratch_shapes=[pltpu.VMEM((8, tm + HALO, CONV_W), F32), pltpu.VMEM((32, 8, CONV_W), F32),
                        pltpu.SemaphoreType.DMA((7 * n,)), pltpu.SemaphoreType.DMA((7 * n,))],
        compiler_params=_cparams(),
    )(doc, cacg, cacg, cw, cb, lg, lb, *parts)


def _conv_bwd_data(dy, cacg, cw, tm):
    r = cacg.shape[0]
    n_steps = r // tm

    def body(dy_ref, dyn_ref, c_ref, w_ref, dc_ref, ub_ref):
        last = (jnp.zeros((HALO, CONV_W), jnp.int32) + pl.program_id(0)) == n_steps - 1
        win = jnp.concatenate([dy_ref[...], jnp.where(last, 0.0, dyn_ref[...])], axis=0)
        _shifted_copies(ub_ref, win)

        def chunk(ci, carry):
            r0 = pl.multiple_of(ci * CONV_CHUNK, CONV_CHUNK)
            du = _conv_chunk(ub_ref, w_ref, r0, BWD_SHIFTS)
            ca = c_ref[pl.ds(r0, CONV_CHUNK), 0:CONV_W]
            sg = _sigmoid(c_ref[pl.ds(r0, CONV_CHUNK), CONV_W:2 * CONV_W])
            dc_ref[pl.ds(r0, CONV_CHUNK), 0:CONV_W] = (du * sg).astype(BF16)
            dc_ref[pl.ds(r0, CONV_CHUNK), CONV_W:2 * CONV_W] = (du * ca * sg * (1.0 - sg)).astype(BF16)
            return carry

        lax.fori_loop(0, tm // CONV_CHUNK, chunk, 0)

    halo_after = pl.BlockSpec((HALO, CONV_W), lambda i: (jnp.minimum((i + 1) * (tm // HALO), r // HALO - 1), 0))
    return pl.pallas_call(
        body, name="conv_bwd_data", grid=(n_steps,),
        in_specs=[_row(tm, CONV_W), halo_after, _row(tm, 2 * CONV_W), _const((32, CONV_W))],
        out_specs=_row(tm, 2 * CONV_W),
        out_shape=jax.ShapeDtypeStruct((r, 2 * CONV_W), BF16),
        scratch_shapes=[pltpu.VMEM((8, tm + HALO, CONV_W), F32)],
        compiler_params=_cparams(),
    )(dy, dy, cacg, cw)


def _attn_bwd(q, kv, o, do, lse, sinks, parts):
    r = q.shape[0]
    nb = r // BLOCK
    n = len(parts)

    def body(sink_ref, q_ref, kvc_ref, kvp_ref, kvm_ref, o_ref, do_ref, lse_ref, *rest):
        src = rest[:n]
        dq_ref, dkv_ref, dmeta_ref, dsink_ref = rest[n:n + 4]
        dst = rest[n + 4:2 * n + 4]
        hold_ref, bias_ref, late_ref, send_sems, recv_sems = rest[2 * n + 4:]
        i = pl.program_id(0)

        @pl.when(i == 0)
        def _():
            for cp in _scatter(src, dst, send_sems, recv_sems):
                cp.start()
            _attn_bias_init(bias_ref, late_ref)
            dmeta_ref[...] = jnp.zeros_like(dmeta_ref)
            dsink_ref[...] = jnp.zeros_like(dsink_ref)
            hold_ref[...] = jnp.zeros_like(hold_ref)

        @pl.when(i < nb)
        def _():
            lane = lax.broadcasted_iota(jnp.int32, (BLOCK, BLOCK), 1)
            lse_tile = lse_ref[...]
            zero = jnp.zeros((BLOCK, BLOCK), F32)
            for g in range(N_KV):
                kcat, vcat = _kv_cat(kvm_ref, kvp_ref, kvc_ref, g)
                heads = range(g * GROUP, (g + 1) * GROUP)
                qs = _stack_heads(q_ref, g)
                dos = _stack_heads(do_ref, g)
                dosb = dos.astype(BF16)
                lse = jnp.concatenate(
                    [jnp.sum(jnp.where(lane == h, lse_tile, 0.0), axis=-1, keepdims=True) + zero for h in heads], axis=0)
                delta = jnp.sum(dos * _stack_heads(o_ref, g), axis=-1, keepdims=True) + jnp.zeros((GROUP * BLOCK, BLOCK), F32)
                s = _dot_nt(qs, kcat) * SCALE + _attn_bias(bias_ref, late_ref, g, i)
                dp = _dot_nt(dosb, vcat)
                ps = [jnp.exp(s[:, k * BLOCK:(k + 1) * BLOCK] - lse) for k in range(3)]
                p = jnp.concatenate(ps, axis=1)
                ds = jnp.concatenate(
                    [(ps[k] * (dp[:, k * BLOCK:(k + 1) * BLOCK] - delta)) * SCALE for k in range(3)], axis=1).astype(BF16)
                sink_term = jnp.exp(_head_rows([sink_ref[h] for h in heads]) - lse)[:, 0:1] * delta[:, 0:1]
                dq = _dot(ds, kcat).astype(BF16)
                for j, h in enumerate(heads):
                    dsink_ref[h:h + 1, :] += -jnp.sum(sink_term[j * BLOCK:(j + 1) * BLOCK])
                    dq_ref[:, h * HEAD_DIM:(h + 1) * HEAD_DIM] = dq[j * BLOCK:(j + 1) * BLOCK]
                dk_t = _dot_tn(qs, ds)
                dv_t = _dot_tn(dosb, p.astype(BF16))
                ks = slice(g * HEAD_DIM, (g + 1) * HEAD_DIM)
                vs = slice(KV_W + g * HEAD_DIM, KV_W + (g + 1) * HEAD_DIM)
                for sl, grad_t in ((ks, dk_t), (vs, dv_t)):
                    dmeta_ref[:, sl] += grad_t[:, 0:BLOCK].T
                    dkv_ref[:, sl] = hold_ref[:, sl] + grad_t[:, BLOCK:2 * BLOCK].T
                    hold_ref[:, sl] = grad_t[:, 2 * BLOCK:3 * BLOCK].T

        @pl.when(i == nb)
        def _():
            dkv_ref[...] = hold_ref[...]
            for cp in _scatter(src, dst, send_sems, recv_sems):
                cp.wait()

    def cur(i):
        return jnp.minimum(i, nb - 1)

    return pl.pallas_call(
        body, name="attn_bwd", grid=(nb + 1,),
        in_specs=[pl.BlockSpec(memory_space=pltpu.SMEM),
                  pl.BlockSpec((BLOCK, ATTN_W), lambda i: (cur(i), 0)),
                  pl.BlockSpec((BLOCK, 2 * KV_W), lambda i: (cur(i), 0)),
                  pl.BlockSpec((BLOCK, 2 * KV_W), lambda i: (jnp.maximum(cur(i) - 1, 0), 0)),
                  _const((BLOCK, 2 * KV_W)),
                  pl.BlockSpec((BLOCK, ATTN_W), lambda i: (cur(i), 0)),
                  pl.BlockSpec((BLOCK, ATTN_W), lambda i: (cur(i), 0)),
                  pl.BlockSpec((BLOCK, BLOCK), lambda i: (cur(i), 0))] + [ANY] * n,
        out_specs=[pl.BlockSpec((BLOCK, ATTN_W), lambda i: (cur(i), 0)),
                   pl.BlockSpec((BLOCK, 2 * KV_W), lambda i: (jnp.maximum(i - 1, 0), 0)),
                   _const((BLOCK, 2 * KV_W)), _const((N_HEADS, BLOCK))] + [ANY] * n,
        out_shape=[jax.ShapeDtypeStruct((r, ATTN_W), BF16), jax.ShapeDtypeStruct((r, 2 * KV_W), F32),
                   jax.ShapeDtypeStruct((BLOCK, 2 * KV_W), F32), jax.ShapeDtypeStruct((N_HEADS, BLOCK), F32)] + _scatter_landing(parts),
        scratch_shapes=[pltpu.VMEM((BLOCK, 2 * KV_W), F32), pltpu.VMEM((N_KV, GROUP * BLOCK, 3 * BLOCK), F32),
                        pltpu.VMEM((GROUP * BLOCK, BLOCK), F32), pltpu.SemaphoreType.DMA((7 * n,)), pltpu.SemaphoreType.DMA((7 * n,))],
        compiler_params=_cparams(),
    )(sinks, q, kv, kv, kv, o, do, lse, *parts)


def _in_proj_bwd(dq, dkv, dkv_meta, dc, dh1, h0, g1, w_in_b, tm):
    r = h0.shape[0]

    def body(dq_ref, dkv_ref, dm_ref, dc_ref, dh1_ref, h_ref, g_ref, w_ref, dh0_ref, dw_ref, dg_ref):
        i = pl.program_id(0)

        @pl.when(i == 0)
        def _():
            dw_ref[...] = jnp.zeros_like(dw_ref)
            dg_ref[...] = jnp.zeros_like(dg_ref)

        meta = jnp.concatenate([dm_ref[...], jnp.zeros((tm - BLOCK, 2 * KV_W), F32)], axis=0) if tm > BLOCK else dm_ref[...]
        first = (jnp.zeros((tm, 2 * KV_W), jnp.int32) + i) == 0
        dkvb = (dkv_ref[...] + jnp.where(first, meta, 0.0)).astype(BF16)
        dqb = dq_ref[...]
        dcb = dc_ref[...]
        hn, xhat, rstd = _rms_fwd(h_ref[...], g_ref[...])
        hnb = hn.astype(BF16)
        dhn = _dot_nt(dqb, w_ref[:, Q0:KV0]) + _dot_nt(dkvb, w_ref[:, KV0:C0]) + _dot_nt(dcb, w_ref[:, C0:IN_COLS])
        dw_ref[:, Q0:KV0] += _dot_tn(hnb, dqb)
        dw_ref[:, KV0:C0] += _dot_tn(hnb, dkvb)
        dw_ref[:, C0:IN_COLS] += _dot_tn(hnb, dcb)
        dx, dg_rows = _rms_bwd(dhn, xhat, rstd, g_ref[...])
        dg_ref[...] += jnp.sum(dg_rows, axis=0, keepdims=True)
        dh0_ref[...] = dh1_ref[...] + dx

    return pl.pallas_call(
        body, name="in_proj_bwd", grid=(r // tm,),
        in_specs=[_row(tm, ATTN_W), _row(tm, 2 * KV_W), _const((BLOCK, 2 * KV_W)), _row(tm, 2 * CONV_W), _row(tm, D_MODEL),
                  _row(tm, D_MODEL), _const((1, D_MODEL)), _const((D_MODEL, IN_COLS))],
        out_specs=[_row(tm, D_MODEL), _const((D_MODEL, IN_COLS)), _const((1, D_MODEL))],
        out_shape=[jax.ShapeDtypeStruct((r, D_MODEL), F32), jax.ShapeDtypeStruct((D_MODEL, IN_COLS), F32),
                   jax.ShapeDtypeStruct((1, D_MODEL), F32)],
        compiler_params=_cparams(),
    )(dq, dkv, dkv_meta, dc, dh1, h0, g1, w_in_b)


def _adamw_update(w_ref, g_ref, m_ref, v_ref, d_ref, nm_ref, nv_ref):
    g = g_ref[...]
    m = ADAM_B1 * m_ref[...] + (1.0 - ADAM_B1) * g
    v = ADAM_B2 * v_ref[...] + (1.0 - ADAM_B2) * (g * g)
    m_hat = m / (1.0 - ADAM_B1 ** ADAM_STEP)
    v_hat = v / (1.0 - ADAM_B2 ** ADAM_STEP)
    d_ref[...] = -ADAM_LR * (m_hat / (jnp.sqrt(v_hat) + ADAM_EPS) + ADAM_WD * w_ref[...])
    nm_ref[...] = m
    nv_ref[...] = v


def _adamw(w, g, m, v, name):
    rows, cols = w.shape
    tr = rows
    for cand in (256, 176, 128, 64, 32, 16, 8):
        if rows % cand == 0:
            tr = cand
            break

    def body(*refs):
        _adamw_update(*refs)

    spec = _row(tr, cols)
    return pl.pallas_call(
        body, name=name, grid=(rows // tr,), in_specs=[spec] * 4, out_specs=[spec] * 3,
        out_shape=[jax.ShapeDtypeStruct((rows, cols), F32)] * 3, compiler_params=_cparams(),
    )(w, g, m, v)


MESH = pl.DeviceIdType.MESH
ANY = pl.BlockSpec(memory_space=pl.ANY)


def _place():
    x, y, c = lax.axis_index("x"), lax.axis_index("y"), lax.axis_index("c")
    chips = [(1 - x, y), (x, 1 - y), (1 - x, 1 - y)]
    return x, y, c, chips


def _gather_ici(src, dst, send_sems, recv_sems):
    x, y, c, chips = _place()
    sends, arrivals = [], []
    for k in range(len(src)):
        rows = src[k].shape[0] // 2
        half = pl.ds(c * rows, rows)
        for p, chip in enumerate(chips):
            sems = dict(send_sem=send_sems.at[3 * k + p], recv_sem=recv_sems.at[3 * k + p], device_id=(chip[0], chip[1], c),
                        device_id_type=MESH)
            sends.append(pltpu.make_async_remote_copy(src_ref=src[k].at[half], dst_ref=dst[k].at[2 * x + y, half], **sems))
            theirs = dst[k].at[2 * chip[0] + chip[1], half]
            arrivals.append(pltpu.make_async_remote_copy(src_ref=theirs, dst_ref=theirs, **sems))
    return sends, arrivals


def _gather_d2d(dst, send_sems, recv_sems):
    x, y, c, chips = _place()
    sends, arrivals = [], []
    for k in range(len(dst)):
        rows = dst[k].shape[1] // 2
        for p, chip in enumerate(chips):
            sems = dict(send_sem=send_sems.at[3 * k + p], recv_sem=recv_sems.at[3 * k + p], device_id=(x, y, 1 - c),
                        device_id_type=MESH)
            mine = dst[k].at[2 * chip[0] + chip[1], pl.ds(c * rows, rows)]
            sends.append(pltpu.make_async_remote_copy(src_ref=mine, dst_ref=mine, **sems))
            theirs = dst[k].at[2 * chip[0] + chip[1], pl.ds((1 - c) * rows, rows)]
            arrivals.append(pltpu.make_async_remote_copy(src_ref=theirs, dst_ref=theirs, **sems))
    return sends, arrivals


def _own_slots(shard):
    return jnp.broadcast_to(shard[None], (N_SHARD,) + shard.shape)


def _gather_weights(shards):
    n = len(shards)

    def body(*refs):
        src, dst = refs[:n], refs[2 * n:3 * n]
        ici_send, ici_recv, d2d_send, d2d_recv = refs[3 * n:]
        sends, arrivals = _gather_ici(src, dst, ici_send, ici_recv)
        for cp in sends:
            cp.start()
        for cp in arrivals:
            cp.wait_recv()
        forwards, from_sibling = _gather_d2d(dst, d2d_send, d2d_recv)
        for cp in forwards:
            cp.start()
        for cp in from_sibling:
            cp.wait_recv()
        for cp in sends + forwards:
            cp.wait_send()

    return pl.pallas_call(
        body, name="gather_weights",
        in_specs=[ANY] * (2 * n), out_specs=[ANY] * n,
        out_shape=[jax.ShapeDtypeStruct((N_SHARD,) + s.shape, s.dtype) for s in shards],
        input_output_aliases={n + k: k for k in range(n)},
        scratch_shapes=[pltpu.SemaphoreType.DMA((3 * n,))] * 4,
    )(*shards, *[_own_slots(s) for s in shards])


VMEM_WHOLE = pl.BlockSpec(memory_space=pltpu.VMEM)


def _allreduce_small(parts):
    widths = sorted({p.shape[1] for p in parts})
    place, heights = [], [0] * len(widths)
    for p in parts:
        gi = widths.index(p.shape[1])
        place.append((gi, heights[gi]))
        heights[gi] += -(-p.shape[0] // 8) * 8
    n, ng = len(parts), len(widths)

    def body(*refs):
        ins, outs, slots = refs[:n], refs[n:2 * n], refs[2 * n:2 * n + ng]
        send_sems, recv_sems = refs[2 * n + ng:]
        x, y, c = lax.axis_index("x"), lax.axis_index("y"), lax.axis_index("c")
        me = 4 * x + 2 * y + c
        for gi in range(ng):
            slots[gi][me] = jnp.zeros((heights[gi], widths[gi]), F32)
        for k, (gi, r0) in enumerate(place):
            slots[gi][me, r0:r0 + parts[k].shape[0], :] = ins[k][...]

        def copy(gi, j, arriving):
            peer = ((x + (j >> 2)) % 2, (y + ((j >> 1) & 1)) % 2, (c + (j & 1)) % 2)
            slot = 4 * peer[0] + 2 * peer[1] + peer[2] if arriving else me
            return pltpu.make_async_remote_copy(
                src_ref=slots[gi].at[me], dst_ref=slots[gi].at[slot], send_sem=send_sems.at[7 * gi + j - 1],
                recv_sem=recv_sems.at[7 * gi + j - 1], device_id=peer, device_id_type=MESH)

        pairs = [(gi, j) for gi in range(ng) for j in range(1, N_DEV)]
        for gi, j in pairs:
            copy(gi, j, False).start()
        for gi, j in pairs:
            copy(gi, j, True).wait_recv()
        totals = []
        for gi in range(ng):
            total = slots[gi][0]
            for d in range(1, N_DEV):
                total = total + slots[gi][d]
            totals.append(total)
        for k, (gi, r0) in enumerate(place):
            outs[k][...] = totals[gi][r0:r0 + parts[k].shape[0], :]
        for gi, j in pairs:
            copy(gi, j, False).wait_send()

    return pl.pallas_call(
        body, name="allreduce_small", in_specs=[VMEM_WHOLE] * n, out_specs=[VMEM_WHOLE] * n,
        out_shape=[jax.ShapeDtypeStruct(p.shape, F32) for p in parts],
        scratch_shapes=[pltpu.VMEM((N_DEV, heights[gi], widths[gi]), F32) for gi in range(ng)]
        + [pltpu.SemaphoreType.DMA((7 * ng,)), pltpu.SemaphoreType.DMA((7 * ng,))],
    )(*parts)


def _adamw_small(ws, gs, ms, vs):
    n = len(ws)

    def body(*refs):
        for k in range(n):
            w_ref, g_ref, m_ref, v_ref = (refs[j * n + k] for j in range(4))
            _adamw_update(w_ref, g_ref, m_ref, v_ref, *(refs[(4 + j) * n + k] for j in range(3)))

    shapes = [jax.ShapeDtypeStruct(w.shape, F32) for w in ws]
    out = pl.pallas_call(
        body, name="adamw_small", in_specs=[VMEM_WHOLE] * (4 * n), out_specs=[VMEM_WHOLE] * (3 * n), out_shape=shapes * 3,
    )(*ws, *gs, *ms, *vs)
    return out[:n], out[n:2 * n], out[2 * n:]


def _scatter(src, dst, send_sems, recv_sems):
    x, y, c = lax.axis_index("x"), lax.axis_index("y"), lax.axis_index("c")
    copies = []
    for k in range(len(src)):
        rows = src[k].shape[1] // 2
        for j in range(1, N_DEV):
            px, py, pc = (x + (j >> 2)) % 2, (y + ((j >> 1) & 1)) % 2, (c + (j & 1)) % 2
            copies.append(pltpu.make_async_remote_copy(
                src_ref=src[k].at[2 * px + py, pl.ds(pc * rows, rows)], dst_ref=dst[k].at[j - 1],
                send_sem=send_sems.at[7 * k + j - 1], recv_sem=recv_sems.at[7 * k + j - 1], device_id=(px, py, pc),
                device_id_type=MESH))
    return copies


def _scatter_landing(parts):
    return [jax.ShapeDtypeStruct((N_DEV - 1, p.shape[1] // 2, p.shape[2]), p.dtype) for p in parts]


def _scatter_call(parts):
    n = len(parts)

    def body(*refs):
        copies = _scatter(refs[:n], refs[n:2 * n], *refs[2 * n:])
        for cp in copies:
            cp.start()
        for cp in copies:
            cp.wait()

    return pl.pallas_call(
        body, name="scatter_grads", in_specs=[ANY] * n, out_specs=[ANY] * n, out_shape=_scatter_landing(parts),
        scratch_shapes=[pltpu.SemaphoreType.DMA((7 * n,)), pltpu.SemaphoreType.DMA((7 * n,))],
    )(*parts)


def _sum_pieces(own, landed):
    n = len(own)

    def body(*refs):
        for k in range(n):
            got = refs[n + k]
            total = refs[k][...]
            for j in range(N_DEV - 1):
                total = total + got[j].astype(F32)
            refs[2 * n + k][...] = total

    in_specs, out_specs = [], []
    for o in own:
        in_specs.append(_row(o.shape[0] // 2, o.shape[1]))
    for o in own:
        in_specs.append(pl.BlockSpec((N_DEV - 1, o.shape[0] // 2, o.shape[1]), lambda i: (0, i, 0)))
        out_specs.append(_row(o.shape[0] // 2, o.shape[1]))
    return pl.pallas_call(
        body, name="sum_pieces", grid=(2,), in_specs=in_specs, out_specs=out_specs,
        out_shape=[jax.ShapeDtypeStruct(o.shape, F32) for o in own], compiler_params=_cparams(),
    )(*own, *landed)


def _swap_with_sibling(halves):
    n = len(halves)

    def body(*refs):
        x, y, c = lax.axis_index("x"), lax.axis_index("y"), lax.axis_index("c")
        copies = [pltpu.make_async_remote_copy(
            src_ref=refs[k], dst_ref=refs[n + k], send_sem=refs[2 * n].at[k], recv_sem=refs[2 * n + 1].at[k],
            device_id=(x, y, 1 - c), device_id_type=MESH) for k in range(n)]
        for cp in copies:
            cp.start()
        for cp in copies:
            cp.wait()

    return pl.pallas_call(
        body, name="swap_with_sibling", in_specs=[ANY] * n, out_specs=[ANY] * n,
        out_shape=[jax.ShapeDtypeStruct(h.shape, h.dtype) for h in halves],
        scratch_shapes=[pltpu.SemaphoreType.DMA((n,)), pltpu.SemaphoreType.DMA((n,))],
    )(*halves)


def _own_piece(part):
    rows = part.shape[1] // 2
    s = 2 * lax.axis_index("x") + lax.axis_index("y")
    return lax.dynamic_slice(part, (s, lax.axis_index("c") * rows, 0), (1, rows, part.shape[2]))[0]


def _both_halves(mine, theirs):
    south = lax.axis_index("c") == 0
    return jnp.concatenate([jnp.where(south, mine, theirs), jnp.where(south, theirs, mine)], axis=0)


def _col_shards(w):
    rows, cols = w.shape
    return w.reshape(rows, N_SHARD, cols // N_SHARD).transpose(1, 0, 2)


def _from_col_shards(g):
    return g.transpose(1, 0, 2).reshape(g.shape[1], -1)


def kernel(x, meta_tokens, attn_norm_g, w_in, attn_sinks, conv_w, conv_b, conv_ln_g, conv_ln_b, attn_out_g, conv_out_g, w_out, ffn_norm_g, w_gate, w_up, w_down, final_norm_g, loss_target, m_meta_tokens, m_attn_norm_g, m_w_in, m_attn_sinks, m_conv_w, m_conv_b, m_conv_ln_g, m_conv_ln_b, m_attn_out_g, m_conv_out_g, m_w_out, m_ffn_norm_g, m_w_gate, m_w_up, m_w_down, m_final_norm_g, v_meta_tokens, v_attn_norm_g, v_w_in, v_attn_sinks, v_conv_w, v_conv_b, v_conv_ln_g, v_conv_ln_b, v_attn_out_g, v_conv_out_g, v_w_out, v_ffn_norm_g, v_w_gate, v_w_up, v_w_down, v_final_norm_g):
    seq = x.shape[1]
    r = -(-(seq + BLOCK) // ROW_QUANTUM) * ROW_QUANTUM
    tail = r - BLOCK - seq
    shard = 2 * lax.axis_index("x") + lax.axis_index("y")

    conv_w32 = jnp.pad(conv_w[0], ((0, 1), (0, 0)))
    small_shard = jnp.concatenate([meta_tokens, conv_w32.reshape(16, 256)], axis=0)
    g_in, g_small = _gather_weights([w_in[0].astype(BF16), small_shard])
    later = [w_gate[0].astype(BF16), w_up[0].astype(BF16), w_out[0].astype(BF16), w_down[0].astype(BF16)]
    w_in_b = _from_col_shards(g_in)
    meta_full = _from_col_shards(g_small[:, 0:N_META])
    cw_full = _from_col_shards(g_small[:, N_META:].reshape(N_SHARD, 32, 128))

    g1, ga, gc, g2 = attn_norm_g, attn_out_g, conv_out_g, ffn_norm_g
    gf = final_norm_g.reshape(1, D_MODEL)
    sinks = attn_sinks[0]

    h0 = jnp.concatenate([jnp.zeros((LEAD, D_MODEL), F32), meta_full, x[0], jnp.zeros((tail, D_MODEL), F32)], axis=0)
    tgt = jnp.pad(loss_target[0], ((BLOCK, tail), (0, 0)))
    q, kv, cacg = _in_proj(h0, g1, w_in_b, 768)
    oa, lse, *gathered = _attn_fwd(q, kv, sinks, later, [_own_slots(s) for s in later])
    oc, g_gate, g_up, g_out, g_down = _conv_fwd(cacg, cw_full, conv_b, conv_ln_g, conv_ln_b, 384, gathered)
    wg_b = _from_col_shards(g_gate)
    wu_b = _from_col_shards(g_up)
    w_out_b = g_out.reshape(D_MODEL, D_MODEL)
    wd_b = g_down.reshape(D_FF, D_MODEL)
    h1, hn2 = _out_proj(oa, oc, h0, ga, gc, g2, w_out_b, 768)
    gate, up, dh2, dh2b, loss_p, dgf = _ffn_fwd(hn2, h1, tgt, gf, wg_b, wu_b, wd_b, seq, 384)

    dgate, dup, dh1, dg2 = _ffn_bwd(dh2, dh2b, gate, up, h1, g2, wg_b, wu_b, wd_b, 384)
    dwg, dwu = _ffn_wgrad_gu(hn2, dgate, dup, 768)
    dwd = _ffn_wgrad_d(gate, up, dh2b, 768)
    doa, doc, dwo, dga, dgc = _out_proj_bwd(dh1, oa, oc, ga, gc, w_out_b, 768)
    p_gate, p_up = _col_shards(dwg), _col_shards(dwu)
    dy, dcw, dcb, dlg, dlb, l_gate, l_up = _conv_bwd_params(
        doc, cacg, cw_full, conv_b, conv_ln_g, conv_ln_b, 384, [p_gate.astype(BF16), p_up.astype(BF16)])
    dc = _conv_bwd_data(dy, cacg, cw_full, 384)
    p_out, p_down = dwo.reshape(N_SHARD, D_MODEL // N_SHARD, D_MODEL), dwd.reshape(N_SHARD, D_FF // N_SHARD, D_MODEL)
    dq, dkv, dkv_meta, dsink, l_out, l_down = _attn_bwd(q, kv, oa, doa, lse, sinks, [p_out.astype(BF16), p_down.astype(BF16)])
    dh0, dwi, dg1 = _in_proj_bwd(dq, dkv, dkv_meta, dc, dh1, h0, g1, w_in_b, 768)
    grad_x = dh0[BLOCK:BLOCK + seq][None]
    p_in = _col_shards(dwi)
    l_in, = _scatter_call([p_in.astype(BF16)])

    red_names = ("final_norm_g", "attn_norm_g", "ffn_norm_g", "meta_tokens", "attn_out_g", "conv_out_g", "conv_b", "conv_ln_g",
                 "conv_ln_b", "conv_w", "loss", "attn_sinks")
    red = dict(zip(red_names, _allreduce_small([dgf, dg1, dg2, dh0[LEAD:BLOCK], dga, dgc, dcb, dlg, dlb, dcw, loss_p, dsink])))
    loss = red["loss"][0, 0]
    red["final_norm_g"] = red["final_norm_g"].reshape(D_MODEL)
    red["attn_sinks"] = red["attn_sinks"][:, 0].reshape(1, N_HEADS)
    g_meta = lax.dynamic_slice_in_dim(red["meta_tokens"], shard * (D_MODEL // N_SHARD), D_MODEL // N_SHARD, axis=1)
    g_convw = lax.dynamic_slice_in_dim(red["conv_w"][0:CONV_K], shard * (CONV_W // N_SHARD), CONV_W // N_SHARD, axis=1)[None]

    halves = _sum_pieces([_own_piece(p) for p in (p_in, p_gate, p_up, p_out, p_down)], [l_in, l_gate, l_up, l_out, l_down])
    g_w_in, g_w_gate, g_w_up, g_w_out, g_w_down = [
        _both_halves(mine, theirs) for mine, theirs in zip(halves, _swap_with_sibling(halves))]

    grads = {
        "meta_tokens": g_meta, "attn_norm_g": red["attn_norm_g"], "w_in": g_w_in[None], "attn_sinks": red["attn_sinks"],
        "conv_w": g_convw, "conv_b": red["conv_b"], "conv_ln_g": red["conv_ln_g"], "conv_ln_b": red["conv_ln_b"],
        "attn_out_g": red["attn_out_g"], "conv_out_g": red["conv_out_g"], "w_out": g_w_out[None], "ffn_norm_g": red["ffn_norm_g"],
        "w_gate": g_w_gate[None], "w_up": g_w_up[None], "w_down": g_w_down[None], "final_norm_g": red["final_norm_g"]}
    params = {
        "meta_tokens": (meta_tokens, m_meta_tokens, v_meta_tokens), "attn_norm_g": (attn_norm_g, m_attn_norm_g, v_attn_norm_g),
        "w_in": (w_in, m_w_in, v_w_in), "attn_sinks": (attn_sinks, m_attn_sinks, v_attn_sinks), "conv_w": (conv_w, m_conv_w, v_conv_w),
        "conv_b": (conv_b, m_conv_b, v_conv_b), "conv_ln_g": (conv_ln_g, m_conv_ln_g, v_conv_ln_g),
        "conv_ln_b": (conv_ln_b, m_conv_ln_b, v_conv_ln_b), "attn_out_g": (attn_out_g, m_attn_out_g, v_attn_out_g),
        "conv_out_g": (conv_out_g, m_conv_out_g, v_conv_out_g), "w_out": (w_out, m_w_out, v_w_out),
        "ffn_norm_g": (ffn_norm_g, m_ffn_norm_g, v_ffn_norm_g), "w_gate": (w_gate, m_w_gate, v_w_gate), "w_up": (w_up, m_w_up, v_w_up),
        "w_down": (w_down, m_w_down, v_w_down), "final_norm_g": (final_norm_g, m_final_norm_g, v_final_norm_g)}
    names = list(params)
    big = ("w_in", "w_out", "w_gate", "w_up", "w_down")
    delta, new_m, new_v = {}, {}, {}
    for name in big:
        w, m, v = params[name]
        d, nm, nv = _adamw(w[0], grads[name][0], m[0], v[0], "adamw_" + name)
        delta[name], new_m[name], new_v[name] = d[None], nm[None], nv[None]
    rest = [name for name in names if name not in big]

    def rows_of(a):
        return a.reshape(-1, a.shape[-1])

    small = _adamw_small([rows_of(params[n][0]) for n in rest], [rows_of(grads[n]) for n in rest],
                         [rows_of(params[n][1]) for n in rest], [rows_of(params[n][2]) for n in rest])
    for dst, outs in zip((delta, new_m, new_v), small):
        for name, out in zip(rest, outs):
            dst[name] = out.reshape(params[name][0].shape)

    return (loss, grad_x, *[grads[n] for n in names], *[delta[n] for n in names], *[new_m[n] for n in names],
            *[new_v[n] for n in names])
```

```python
import functools
import math

import jax
import jax.numpy as jnp
from jax import lax
from jax.experimental import pallas as pl
from jax.experimental.pallas import tpu as pltpu

F32 = jnp.float32
BF16 = jnp.bfloat16

D_MODEL = 1024
N_META = 16
ATTN_W = 512
CONV_W = 512
HEAD_DIM = 64
N_HEADS = 8
N_KV = 2
GROUP = N_HEADS // N_KV
KV_W = N_KV * HEAD_DIM
BLOCK = 128
LEAD = BLOCK - N_META
CONV_K = 31
D_FF = 2816
IN_COLS = ATTN_W + 2 * KV_W + 2 * CONV_W
Q0, KV0, C0 = 0, ATTN_W, ATTN_W + 2 * KV_W
NORM_EPS = 1e-5
SCALE = 1.0 / math.sqrt(HEAD_DIM)
SLOPES = tuple(2.0 ** (-(8.0 / N_HEADS) * (h + 1)) for h in range(N_HEADS))
NEG = -1e30

ADAM_LR, ADAM_B1, ADAM_B2, ADAM_EPS, ADAM_WD, ADAM_STEP = 0.001, 0.9, 0.999, 1e-08, 0.01, 10

N_SHARD = 4
N_DEV = 8
ROW_QUANTUM = 768
HALO = 32
CONV_CHUNK = 32
FF_CHUNK = 256
VMEM_LIMIT = 60 * 1024 * 1024


def _cparams(n_axes=1):
    return pltpu.CompilerParams(dimension_semantics=("arbitrary",) * n_axes, vmem_limit_bytes=VMEM_LIMIT)


def _dot(a, b):
    return jnp.dot(a, b, preferred_element_type=F32)


def _dot_nt(a, b):
    return lax.dot_general(a, b, (((1,), (1,)), ((), ())), preferred_element_type=F32)


def _dot_tn(a, b):
    return lax.dot_general(a, b, (((0,), (0,)), ((), ())), preferred_element_type=F32)


def _sigmoid(x):
    return 1.0 / (1.0 + jnp.exp(-x))


def _row(tm, n):
    return pl.BlockSpec((tm, n), lambda i: (i, 0))


def _const(shape):
    return pl.BlockSpec(shape, lambda i: (0,) * len(shape))


def _resident(shape):
    return pl.BlockSpec(shape, lambda i: (0,) * len(shape), pipeline_mode=pl.Buffered(1))


def _rms_fwd(x, g):
    rstd = lax.rsqrt(jnp.mean(x * x, axis=-1, keepdims=True) + NORM_EPS)
    xhat = x * rstd
    return xhat * g, xhat, rstd


def _rms_bwd(dy, xhat, rstd, g):
    dxh = dy * g
    dx = rstd * (dxh - xhat * jnp.mean(dxh * xhat, axis=-1, keepdims=True))
    return dx, dy * xhat


def _in_proj(h0, g1, w_in_b, tm):
    r = h0.shape[0]

    def body(h_ref, g_ref, w_ref, q_ref, kv_ref, c_ref):
        hn = _rms_fwd(h_ref[...], g_ref[...])[0].astype(BF16)
        q_ref[...] = _dot(hn, w_ref[:, Q0:KV0]).astype(BF16)
        kv_ref[...] = _dot(hn, w_ref[:, KV0:C0]).astype(BF16)
        c_ref[...] = _dot(hn, w_ref[:, C0:IN_COLS])

    return pl.pallas_call(
        body, name="in_proj", grid=(r // tm,),
        in_specs=[_row(tm, D_MODEL), _const((1, D_MODEL)), _const((D_MODEL, IN_COLS))],
        out_specs=[_row(tm, ATTN_W), _row(tm, 2 * KV_W), _row(tm, 2 * CONV_W)],
        out_shape=[jax.ShapeDtypeStruct((r, ATTN_W), BF16), jax.ShapeDtypeStruct((r, 2 * KV_W), BF16),
                   jax.ShapeDtypeStruct((r, 2 * CONV_W), F32)],
        compiler_params=_cparams(),
    )(h0, g1, w_in_b)


def _attn_bias_init(bias_ref, late_ref):
    row = lax.broadcasted_iota(jnp.int32, (GROUP * BLOCK, BLOCK), 0) & (BLOCK - 1)
    col = lax.broadcasted_iota(jnp.int32, (GROUP * BLOCK, BLOCK), 1)
    late_ref[...] = jnp.where(col > row, 1.0, 0.0)
    for g in range(N_KV):
        slope = jnp.concatenate([jnp.zeros((BLOCK, BLOCK), F32) + SLOPES[g * GROUP + j] for j in range(GROUP)], axis=0)
        bias_ref[g, :, 0:BLOCK] = jnp.where(col >= LEAD, 0.0, NEG)
        bias_ref[g, :, BLOCK:2 * BLOCK] = jnp.where(col > row, -slope * (row - col + BLOCK).astype(F32), NEG)
        bias_ref[g, :, 2 * BLOCK:3 * BLOCK] = jnp.where(col <= row, -slope * (row - col).astype(F32), NEG)


def _attn_bias(bias_ref, late_ref, g, i):
    meta0 = jnp.where(i == 0, NEG, 0.0)
    no_prev = jnp.where(i >= 2, 0.0, NEG)
    no_cur = jnp.where(i >= 1, 0.0, NEG)
    return jnp.concatenate([bias_ref[g, :, 0:BLOCK] + late_ref[...] * meta0, bias_ref[g, :, BLOCK:2 * BLOCK] + no_prev,
                            bias_ref[g, :, 2 * BLOCK:3 * BLOCK] + no_cur], axis=1)


def _head_rows(vals):
    return jnp.concatenate([jnp.zeros((BLOCK, BLOCK), F32) + v for v in vals], axis=0)


def _stack_heads(ref, g):
    return jnp.concatenate([ref[:, (g * GROUP + j) * HEAD_DIM:(g * GROUP + j + 1) * HEAD_DIM] for j in range(GROUP)], axis=0)


def _kv_cat(kvm_ref, kvp_ref, kvc_ref, g):
    ks = slice(g * HEAD_DIM, (g + 1) * HEAD_DIM)
    vs = slice(KV_W + g * HEAD_DIM, KV_W + (g + 1) * HEAD_DIM)
    kcat = jnp.concatenate([kvm_ref[:, ks], kvp_ref[:, ks], kvc_ref[:, ks]], axis=0)
    vcat = jnp.concatenate([kvm_ref[:, vs], kvp_ref[:, vs], kvc_ref[:, vs]], axis=0)
    return kcat, vcat


def _attn_fwd(q, kv, sinks, shards, gathered):
    r = q.shape[0]
    nb = r // BLOCK
    n = len(shards)

    def body(sink_ref, q_ref, kvc_ref, kvp_ref, kvm_ref, *rest):
        src = rest[:n]
        o_ref, lse_ref = rest[2 * n:2 * n + 2]
        dst = rest[2 * n + 2:3 * n + 2]
        bias_ref, late_ref, send_sems, recv_sems = rest[3 * n + 2:]
        i = pl.program_id(0)

        @pl.when(i == 0)
        def _():
            for cp in _gather_ici(src, dst, send_sems, recv_sems)[0]:
                cp.start()
            _attn_bias_init(bias_ref, late_ref)

        lane = lax.broadcasted_iota(jnp.int32, (BLOCK, BLOCK), 1)
        lse_tile = jnp.zeros((BLOCK, BLOCK), F32)
        for g in range(N_KV):
            kcat, vcat = _kv_cat(kvm_ref, kvp_ref, kvc_ref, g)
            heads = range(g * GROUP, (g + 1) * GROUP)
            s = _dot_nt(_stack_heads(q_ref, g), kcat) * SCALE + _attn_bias(bias_ref, late_ref, g, i)
            segs = [s[:, k * BLOCK:(k + 1) * BLOCK] for k in range(3)]
            sink = _head_rows([sink_ref[h] for h in heads])
            m = jnp.maximum(jnp.max(jnp.maximum(jnp.maximum(segs[0], segs[1]), segs[2]), axis=-1, keepdims=True), sink)
            ps = [jnp.exp(x - m) for x in segs]
            l = jnp.sum(ps[0] + ps[1] + ps[2], axis=-1, keepdims=True) + jnp.exp(sink - m)
            o = _dot(jnp.concatenate(ps, axis=1).astype(BF16), vcat) * (1.0 / l)[:, 0:HEAD_DIM]
            lse = m + jnp.log(l)
            for j, h in enumerate(heads):
                o_ref[:, h * HEAD_DIM:(h + 1) * HEAD_DIM] = o[j * BLOCK:(j + 1) * BLOCK]
                lse_tile = jnp.where(lane == h, lse[j * BLOCK:(j + 1) * BLOCK], lse_tile)
        lse_ref[...] = lse_tile

        @pl.when(i == nb - 1)
        def _():
            sends, arrivals = _gather_ici(src, dst, send_sems, recv_sems)
            for cp in arrivals:
                cp.wait_recv()
            for cp in sends:
                cp.wait_send()

    return pl.pallas_call(
        body, name="attn_fwd", grid=(nb,),
        in_specs=[pl.BlockSpec(memory_space=pltpu.SMEM), _row(BLOCK, ATTN_W), _row(BLOCK, 2 * KV_W),
                  pl.BlockSpec((BLOCK, 2 * KV_W), lambda i: (jnp.maximum(i - 1, 0), 0)), _const((BLOCK, 2 * KV_W))] + [ANY] * (2 * n),
        out_specs=[_row(BLOCK, ATTN_W), _row(BLOCK, BLOCK)] + [ANY] * n,
        out_shape=[jax.ShapeDtypeStruct((r, ATTN_W), F32), jax.ShapeDtypeStruct((r, BLOCK), F32)]
        + [jax.ShapeDtypeStruct(g.shape, g.dtype) for g in gathered],
        input_output_aliases={5 + n + k: 2 + k for k in range(n)},
        scratch_shapes=[pltpu.VMEM((N_KV, GROUP * BLOCK, 3 * BLOCK), F32), pltpu.VMEM((GROUP * BLOCK, BLOCK), F32),
                        pltpu.SemaphoreType.DMA((3 * n,)), pltpu.SemaphoreType.DMA((3 * n,))],
        compiler_params=_cparams(),
    )(sinks, q, kv, kv, kv, *shards, *gathered)


def _shifted_copies(ub_ref, win):
    w = win.shape[0]
    ub_ref[0] = win
    for b in range(1, 8):
        ub_ref[b] = pltpu.roll(win, shift=w - b, axis=0)


def _conv_chunk(ub_ref, w_ref, r0, shifts):
    acc = jnp.zeros((CONV_CHUNK, CONV_W), F32)
    for j in range(CONV_K):
        a, b = divmod(shifts[j], 8)
        acc = acc + w_ref[j:j + 1, :] * ub_ref[b, pl.ds(r0 + 8 * a, CONV_CHUNK), :]
    return acc


FWD_SHIFTS = tuple(HALO - (CONV_K - 1) + j for j in range(CONV_K))
BWD_SHIFTS = tuple(CONV_K - 1 - j for j in range(CONV_K))


def _glu_window(cp_ref, c_ref, i):
    tile = c_ref[:, 0:CONV_W] * _sigmoid(c_ref[:, CONV_W:2 * CONV_W])
    halo = cp_ref[:, 0:CONV_W] * _sigmoid(cp_ref[:, CONV_W:2 * CONV_W])
    first = (jnp.zeros((HALO, CONV_W), jnp.int32) + i) == 0
    return jnp.concatenate([jnp.where(first, 0.0, halo), tile], axis=0)


def _halo_before(tm, n):
    return pl.BlockSpec((HALO, n), lambda i: (jnp.maximum(i * (tm // HALO) - 1, 0), 0))


def _conv_fwd(cacg, cw, cb, lg, lb, tm, gathered):
    r = cacg.shape[0]
    n = len(gathered)

    def body(c_ref, cp_ref, w_ref, cb_ref, lg_ref, lb_ref, *rest):
        o_ref, y_ref = rest[n:n + 2]
        dst = rest[n + 2:2 * n + 2]
        ub_ref, send_sems, recv_sems = rest[2 * n + 2:]
        i = pl.program_id(0)

        @pl.when(i == 0)
        def _():
            for cp in _gather_d2d(dst, send_sems, recv_sems)[0]:
                cp.start()

        _shifted_copies(ub_ref, _glu_window(cp_ref, c_ref, i))

        def chunk(ci, carry):
            r0 = pl.multiple_of(ci * CONV_CHUNK, CONV_CHUNK)
            y = _conv_chunk(ub_ref, w_ref, r0, FWD_SHIFTS) + cb_ref[...]
            yc = y - jnp.mean(y, axis=-1, keepdims=True)
            rs = lax.rsqrt(jnp.mean(yc * yc, axis=-1, keepdims=True) + NORM_EPS)
            yn = yc * rs * lg_ref[...] + lb_ref[...]
            o_ref[pl.ds(r0, CONV_CHUNK), :] = yn * _sigmoid(yn)
            y_ref[pl.ds(r0, CONV_CHUNK), :] = y
            return carry

        lax.fori_loop(0, tm // CONV_CHUNK, chunk, 0)

        @pl.when(i == r // tm - 1)
        def _():
            sends, arrivals = _gather_d2d(dst, send_sems, recv_sems)
            for cp in arrivals:
                cp.wait_recv()
            for cp in sends:
                cp.wait_send()

    return pl.pallas_call(
        body, name="conv_fwd", grid=(r // tm,),
        in_specs=[_row(tm, 2 * CONV_W), _halo_before(tm, 2 * CONV_W), _const((32, CONV_W)), _const((1, CONV_W)),
                  _const((1, CONV_W)), _const((1, CONV_W))] + [ANY] * n,
        out_specs=[_row(tm, CONV_W), _row(tm, CONV_W)] + [ANY] * n,
        out_shape=[jax.ShapeDtypeStruct((r, CONV_W), F32)] * 2 + [jax.ShapeDtypeStruct(g.shape, g.dtype) for g in gathered],
        input_output_aliases={6 + k: 2 + k for k in range(n)},
        scratch_shapes=[pltpu.VMEM((8, tm + HALO, CONV_W), F32), pltpu.SemaphoreType.DMA((3 * n,)), pltpu.SemaphoreType.DMA((3 * n,))],
        compiler_params=_cparams(),
    )(cacg, cacg, cw, cb, lg, lb, *gathered)


def _out_proj(oa, oc, h0, ga, gc, g2, w_out_b, tm):
    r = h0.shape[0]

    def body(oa_ref, oc_ref, h_ref, ga_ref, gc_ref, g2_ref, w_ref, h1_ref, hn2_ref):
        ma = _rms_fwd(oa_ref[...], ga_ref[...])[0].astype(BF16)
        mc = _rms_fwd(oc_ref[...], gc_ref[...])[0].astype(BF16)
        h1 = h_ref[...] + _dot(ma, w_ref[0:ATTN_W, :]) + _dot(mc, w_ref[ATTN_W:ATTN_W + CONV_W, :])
        h1_ref[...] = h1
        hn2_ref[...] = _rms_fwd(h1, g2_ref[...])[0].astype(BF16)

    return pl.pallas_call(
        body, name="out_proj", grid=(r // tm,),
        in_specs=[_row(tm, ATTN_W), _row(tm, CONV_W), _row(tm, D_MODEL), _const((1, ATTN_W)), _const((1, CONV_W)),
                  _const((1, D_MODEL)), _const((D_MODEL, D_MODEL))],
        out_specs=[_row(tm, D_MODEL), _row(tm, D_MODEL)],
        out_shape=[jax.ShapeDtypeStruct((r, D_MODEL), F32), jax.ShapeDtypeStruct((r, D_MODEL), BF16)],
        compiler_params=_cparams(),
    )(oa, oc, h0, ga, gc, g2, w_out_b)


def _ffn_fwd(hn2, h1, target, gf, wg_b, wu_b, wd_b, tm):
    r = h1.shape[0]
    seq = target.shape[0]
    n_sub = tm // BLOCK

    def body(hn_ref, h1_ref, *rest):
        t_refs = rest[:n_sub]
        gf_ref, wg_ref, wu_ref, wd_ref, gate_ref, up_ref, act_ref, dh2_ref, dh2b_ref, loss_ref, dgf_ref, acc_ref = rest[n_sub:]
        i = pl.program_id(0)

        @pl.when(i == 0)
        def _():
            loss_ref[...] = jnp.zeros_like(loss_ref)
            dgf_ref[...] = jnp.zeros_like(dgf_ref)

        hn = hn_ref[...]
        acc_ref[...] = h1_ref[...]
        for ch in range(D_FF // FF_CHUNK):
            cs = slice(ch * FF_CHUNK, (ch + 1) * FF_CHUNK)
            gate = _dot(hn, wg_ref[:, cs])
            up = _dot(hn, wu_ref[:, cs])
            gate_ref[:, cs] = gate.astype(BF16)
            up_ref[:, cs] = up.astype(BF16)
            act = (gate * _sigmoid(gate) * up).astype(BF16)
            act_ref[:, cs] = act
            acc_ref[...] += _dot(act, wd_ref[cs, :])
        y, xhat, rstd = _rms_fwd(acc_ref[...], gf_ref[...])
        rows = lax.broadcasted_iota(jnp.int32, (tm, D_MODEL), 0) + i * tm
        real = (rows >= BLOCK) & (rows < BLOCK + seq)
        err = jnp.where(real, y - jnp.concatenate([t[...] for t in t_refs], axis=0), 0.0)
        loss_ref[...] += jnp.sum(err * err) * (0.5 / D_MODEL)
        dy = err * (1.0 / D_MODEL)
        dh2, dg_rows = _rms_bwd(dy, xhat, rstd, gf_ref[...])
        dgf_ref[...] += jnp.sum(dg_rows, axis=0, keepdims=True)
        dh2_ref[...] = dh2
        dh2b_ref[...] = dh2.astype(BF16)

    def target_block(k):
        return pl.BlockSpec((BLOCK, D_MODEL), lambda i: (jnp.clip(n_sub * i - 1 + k, 0, seq // BLOCK - 1), 0))

    return pl.pallas_call(
        body, name="ffn_fwd", grid=(r // tm,),
        in_specs=[_row(tm, D_MODEL), _row(tm, D_MODEL)] + [target_block(k) for k in range(n_sub)]
        + [_const((1, D_MODEL)), _resident((D_MODEL, D_FF)), _resident((D_MODEL, D_FF)), _resident((D_FF, D_MODEL))],
        out_specs=[_row(tm, D_FF)] * 3 + [_row(tm, D_MODEL), _row(tm, D_MODEL), _const((1, BLOCK)), _const((1, D_MODEL))],
        out_shape=[jax.ShapeDtypeStruct((r, D_FF), BF16)] * 3
        + [jax.ShapeDtypeStruct((r, D_MODEL), F32), jax.ShapeDtypeStruct((r, D_MODEL), BF16),
           jax.ShapeDtypeStruct((1, BLOCK), F32), jax.ShapeDtypeStruct((1, D_MODEL), F32)],
        scratch_shapes=[pltpu.VMEM((tm, D_MODEL), F32)],
        compiler_params=_cparams(),
    )(hn2, h1, *[target] * n_sub, gf, wg_b, wu_b, wd_b)


def _ffn_bwd(dh2, dh2b, gate, up, h1, g2, wg_b, wu_b, wd_b, tm):
    r = h1.shape[0]

    def body(dh2_ref, dh2b_ref, gate_ref, up_ref, h1_ref, g2_ref, wg_ref, wu_ref, wd_ref, dgate_ref, dup_ref, dh1_ref, dg2_ref, acc_ref):
        @pl.when(pl.program_id(0) == 0)
        def _():
            dg2_ref[...] = jnp.zeros_like(dg2_ref)

        dyb = dh2b_ref[...]
        acc_ref[...] = jnp.zeros_like(acc_ref)
        for ch in range(D_FF // FF_CHUNK):
            cs = slice(ch * FF_CHUNK, (ch + 1) * FF_CHUNK)
            dact = _dot_nt(dyb, wd_ref[cs, :])
            gate = gate_ref[:, cs].astype(F32)
            up = up_ref[:, cs].astype(F32)
            sg = _sigmoid(gate)
            silu = gate * sg
            dgate = (dact * up * (sg * (1.0 + gate * (1.0 - sg)))).astype(BF16)
            dup = (dact * silu).astype(BF16)
            dgate_ref[:, cs] = dgate
            dup_ref[:, cs] = dup
            acc_ref[...] += _dot_nt(dgate, wg_ref[:, cs]) + _dot_nt(dup, wu_ref[:, cs])
        _, xhat, rstd = _rms_fwd(h1_ref[...], g2_ref[...])
        dx, dg_rows = _rms_bwd(acc_ref[...], xhat, rstd, g2_ref[...])
        dg2_ref[...] += jnp.sum(dg_rows, axis=0, keepdims=True)
        dh1_ref[...] = dh2_ref[...] + dx

    return pl.pallas_call(
        body, name="ffn_bwd", grid=(r // tm,),
        in_specs=[_row(tm, D_MODEL), _row(tm, D_MODEL), _row(tm, D_FF), _row(tm, D_FF), _row(tm, D_MODEL), _const((1, D_MODEL)),
                  _resident((D_MODEL, D_FF)), _resident((D_MODEL, D_FF)), _resident((D_FF, D_MODEL))],
        out_specs=[_row(tm, D_FF), _row(tm, D_FF), _row(tm, D_MODEL), _const((1, D_MODEL))],
        out_shape=[jax.ShapeDtypeStruct((r, D_FF), BF16), jax.ShapeDtypeStruct((r, D_FF), BF16),
                   jax.ShapeDtypeStruct((r, D_MODEL), F32), jax.ShapeDtypeStruct((1, D_MODEL), F32)],
        scratch_shapes=[pltpu.VMEM((tm, D_MODEL), F32)],
        compiler_params=_cparams(),
    )(dh2, dh2b, gate, up, h1, g2, wg_b, wu_b, wd_b)


FF_HALF = D_FF // 2


FF_SHARD = D_FF // N_SHARD


def _ffn_wgrad_gu(hn2, dgate, dup, tk):
    r = hn2.shape[0]
    n_k = r // tk

    def body(hn_ref, dg_ref, du_ref, wg_ref, wu_ref, accg_ref, accu_ref):
        k = pl.program_id(1)

        @pl.when(k == 0)
        def _():
            accg_ref[...] = jnp.zeros_like(accg_ref)
            accu_ref[...] = jnp.zeros_like(accu_ref)

        hn = hn_ref[...]
        accg_ref[...] += _dot_tn(hn, dg_ref[...])
        accu_ref[...] += _dot_tn(hn, du_ref[...])

        @pl.when(k == n_k - 1)
        def _():
            for t in range(2):
                wg_ref[t] = accg_ref[...][:, t * FF_SHARD:(t + 1) * FF_SHARD].astype(BF16)
                wu_ref[t] = accu_ref[...][:, t * FF_SHARD:(t + 1) * FF_SHARD].astype(BF16)

    col = pl.BlockSpec((tk, FF_HALF), lambda j, k: (k, j))
    out = pl.BlockSpec((2, D_MODEL, FF_SHARD), lambda j, k: (j, 0, 0))
    return pl.pallas_call(
        body, name="ffn_wgrad_gu", grid=(2, n_k),
        in_specs=[pl.BlockSpec((tk, D_MODEL), lambda j, k: (k, 0)), col, col],
        out_specs=[out, out],
        out_shape=[jax.ShapeDtypeStruct((N_SHARD, D_MODEL, FF_SHARD), BF16)] * 2,
        scratch_shapes=[pltpu.VMEM((D_MODEL, FF_HALF), F32)] * 2,
        compiler_params=_cparams(2),
    )(hn2, dgate, dup)


def _ffn_wgrad_d(act, dh2b, tk):
    r = act.shape[0]
    n_k = r // tk

    def body(a_ref, dy_ref, wd_ref, acc_ref):
        k = pl.program_id(1)

        @pl.when(k == 0)
        def _():
            acc_ref[...] = jnp.zeros_like(acc_ref)

        acc_ref[...] += _dot_tn(a_ref[...], dy_ref[...])

        @pl.when(k == n_k - 1)
        def _():
            wd_ref[...] = acc_ref[...].astype(BF16)

    return pl.pallas_call(
        body, name="ffn_wgrad_d", grid=(2, n_k),
        in_specs=[pl.BlockSpec((tk, FF_HALF), lambda j, k: (k, j)), pl.BlockSpec((tk, D_MODEL), lambda j, k: (k, 0))],
        out_specs=pl.BlockSpec((FF_HALF, D_MODEL), lambda j, k: (j, 0)),
        out_shape=jax.ShapeDtypeStruct((D_FF, D_MODEL), BF16),
        scratch_shapes=[pltpu.VMEM((FF_HALF, D_MODEL), F32)],
        compiler_params=_cparams(2),
    )(act, dh2b)


def _out_proj_bwd(dh1, oa, oc, ga, gc, w_out_b, tm):
    r = dh1.shape[0]

    def body(dh_ref, oa_ref, oc_ref, ga_ref, gc_ref, w_ref, doa_ref, doc_ref, dw_ref, dga_ref, dgc_ref, acc_ref):
        i = pl.program_id(0)

        @pl.when(i == 0)
        def _():
            acc_ref[...] = jnp.zeros_like(acc_ref)
            dga_ref[...] = jnp.zeros_like(dga_ref)
            dgc_ref[...] = jnp.zeros_like(dgc_ref)

        dhb = dh_ref[...].astype(BF16)
        dmix = _dot_nt(dhb, w_ref[...])
        ma, xa, ra = _rms_fwd(oa_ref[...], ga_ref[...])
        mc, xc, rc = _rms_fwd(oc_ref[...], gc_ref[...])
        acc_ref[0:ATTN_W, :] += _dot_tn(ma.astype(BF16), dhb)
        acc_ref[ATTN_W:ATTN_W + CONV_W, :] += _dot_tn(mc.astype(BF16), dhb)

        @pl.when(i == r // tm - 1)
        def _():
            dw_ref[...] = acc_ref[...].astype(BF16)

        doa, dga_rows = _rms_bwd(dmix[:, 0:ATTN_W], xa, ra, ga_ref[...])
        doc, dgc_rows = _rms_bwd(dmix[:, ATTN_W:ATTN_W + CONV_W], xc, rc, gc_ref[...])
        doa_ref[...] = doa
        doc_ref[...] = doc
        dga_ref[...] += jnp.sum(dga_rows, axis=0, keepdims=True)
        dgc_ref[...] += jnp.sum(dgc_rows, axis=0, keepdims=True)

    return pl.pallas_call(
        body, name="out_proj_bwd", grid=(r // tm,),
        in_specs=[_row(tm, D_MODEL), _row(tm, ATTN_W), _row(tm, CONV_W), _const((1, ATTN_W)), _const((1, CONV_W)),
                  _const((D_MODEL, D_MODEL))],
        out_specs=[_row(tm, ATTN_W), _row(tm, CONV_W), _const((D_MODEL, D_MODEL)), _const((1, ATTN_W)), _const((1, CONV_W))],
        out_shape=[jax.ShapeDtypeStruct((r, ATTN_W), F32), jax.ShapeDtypeStruct((r, CONV_W), F32),
                   jax.ShapeDtypeStruct((D_MODEL, D_MODEL), BF16), jax.ShapeDtypeStruct((1, ATTN_W), F32),
                   jax.ShapeDtypeStruct((1, CONV_W), F32)],
        scratch_shapes=[pltpu.VMEM((D_MODEL, D_MODEL), F32)],
        compiler_params=_cparams(),
    )(dh1, oa, oc, ga, gc, w_out_b)


def _conv_bwd_params(doc, y, cacg, lg, lb, tm, parts):
    r = cacg.shape[0]
    n_steps = r // tm
    n = len(parts)

    def body(do_ref, y_ref, c_ref, cp_ref, lg_ref, lb_ref, *rest):
        src = rest[:n]
        dy_ref, dcw_ref, dcb_ref, dlg_ref, dlb_ref = rest[n:n + 5]
        dst = rest[n + 5:2 * n + 5]
        ub_ref, accw_ref, send_sems, recv_sems = rest[2 * n + 5:]
        i = pl.program_id(0)

        @pl.when(i == 0)
        def _():
            for cp in _scatter(src, dst, send_sems, recv_sems):
                cp.start()
            accw_ref[...] = jnp.zeros_like(accw_ref)
            dcb_ref[...] = jnp.zeros_like(dcb_ref)
            dlg_ref[...] = jnp.zeros_like(dlg_ref)
            dlb_ref[...] = jnp.zeros_like(dlb_ref)

        _shifted_copies(ub_ref, _glu_window(cp_ref, c_ref, i))

        def chunk(ci, carry):
            r0 = pl.multiple_of(ci * CONV_CHUNK, CONV_CHUNK)
            y = y_ref[pl.ds(r0, CONV_CHUNK), :]
            yc = y - jnp.mean(y, axis=-1, keepdims=True)
            rs = lax.rsqrt(jnp.mean(yc * yc, axis=-1, keepdims=True) + NORM_EPS)
            xhat = yc * rs
            yn = xhat * lg_ref[...] + lb_ref[...]
            sg = _sigmoid(yn)
            dyn = do_ref[pl.ds(r0, CONV_CHUNK), :] * (sg * (1.0 + yn * (1.0 - sg)))
            dlg_ref[...] += jnp.sum(dyn * xhat, axis=0, keepdims=True)
            dlb_ref[...] += jnp.sum(dyn, axis=0, keepdims=True)
            dxh = dyn * lg_ref[...]
            dy = rs * (dxh - jnp.mean(dxh, axis=-1, keepdims=True) - xhat * jnp.mean(dxh * xhat, axis=-1, keepdims=True))
            dcb_ref[...] += jnp.sum(dy, axis=0, keepdims=True)
            dy_ref[pl.ds(r0, CONV_CHUNK), :] = dy
            for j in range(CONV_K):
                a, b = divmod(FWD_SHIFTS[j], 8)
                prod = dy * ub_ref[b, pl.ds(r0 + 8 * a, CONV_CHUNK), :]
                accw_ref[j] += jnp.sum(prod.reshape(CONV_CHUNK // 8, 8, CONV_W), axis=0)
            return carry

        lax.fori_loop(0, tm // CONV_CHUNK, chunk, 0)

        @pl.when(i == n_steps - 1)
        def _():
            for j in range(32):
                dcw_ref[j:j + 1, :] = jnp.sum(accw_ref[j], axis=0, keepdims=True)
            for cp in _scatter(src, dst, send_sems, recv_sems):
                cp.wait()

    vec = _const((1, CONV_W))
    return pl.pallas_call(
        body, name="conv_bwd_params", grid=(n_steps,),
        in_specs=[_row(tm, CONV_W), _row(tm, CONV_W), _row(tm, 2 * CONV_W), _halo_before(tm, 2 * CONV_W), vec, vec] + [ANY] * n,
        out_specs=[_row(tm, CONV_W), _const((32, CONV_W)), vec, vec, vec] + [ANY] * n,
        out_shape=[jax.ShapeDtypeStruct((r, CONV_W), F32), jax.ShapeDtypeStruct((32, CONV_W), F32)]
        + [jax.ShapeDtypeStruct((1, CONV_W), F32)] * 3 + _scatter_landing(parts),
        scratch_shapes=[pltpu.VMEM((8, tm + HALO, CONV_W), F32), pltpu.VMEM((32, 8, CONV_W), F32),
                        pltpu.SemaphoreType.DMA((7 * n,)), pltpu.SemaphoreType.DMA((7 * n,))],
        compiler_params=_cparams(),
    )(doc, y, cacg, cacg, lg, lb, *parts)


def _conv_bwd_data(dy, cacg, cw, tm):
    r = cacg.shape[0]
    n_steps = r // tm

    def body(dy_ref, dyn_ref, c_ref, w_ref, dc_ref, ub_ref):
        last = (jnp.zeros((HALO, CONV_W), jnp.int32) + pl.program_id(0)) == n_steps - 1
        win = jnp.concatenate([dy_ref[...], jnp.where(last, 0.0, dyn_ref[...])], axis=0)
        _shifted_copies(ub_ref, win)

        def chunk(ci, carry):
            r0 = pl.multiple_of(ci * CONV_CHUNK, CONV_CHUNK)
            du = _conv_chunk(ub_ref, w_ref, r0, BWD_SHIFTS)
            ca = c_ref[pl.ds(r0, CONV_CHUNK), 0:CONV_W]
            sg = _sigmoid(c_ref[pl.ds(r0, CONV_CHUNK), CONV_W:2 * CONV_W])
            dc_ref[pl.ds(r0, CONV_CHUNK), 0:CONV_W] = (du * sg).astype(BF16)
            dc_ref[pl.ds(r0, CONV_CHUNK), CONV_W:2 * CONV_W] = (du * ca * sg * (1.0 - sg)).astype(BF16)
            return carry

        lax.fori_loop(0, tm // CONV_CHUNK, chunk, 0)

    halo_after = pl.BlockSpec((HALO, CONV_W), lambda i: (jnp.minimum((i + 1) * (tm // HALO), r // HALO - 1), 0))
    return pl.pallas_call(
        body, name="conv_bwd_data", grid=(n_steps,),
        in_specs=[_row(tm, CONV_W), halo_after, _row(tm, 2 * CONV_W), _const((32, CONV_W))],
        out_specs=_row(tm, 2 * CONV_W),
        out_shape=jax.ShapeDtypeStruct((r, 2 * CONV_W), BF16),
        scratch_shapes=[pltpu.VMEM((8, tm + HALO, CONV_W), F32)],
        compiler_params=_cparams(),
    )(dy, dy, cacg, cw)


def _attn_bwd(q, kv, o, do, lse, sinks, parts):
    r = q.shape[0]
    nb = r // BLOCK
    n = len(parts)

    def body(sink_ref, q_ref, kvc_ref, kvp_ref, kvm_ref, o_ref, do_ref, lse_ref, *rest):
        src = rest[:n]
        dq_ref, dkv_ref, dmeta_ref, dsink_ref = rest[n:n + 4]
        dst = rest[n + 4:2 * n + 4]
        hold_ref, bias_ref, late_ref, send_sems, recv_sems = rest[2 * n + 4:]
        i = pl.program_id(0)

        @pl.when(i == 0)
        def _():
            for cp in _scatter(src, dst, send_sems, recv_sems):
                cp.start()
            _attn_bias_init(bias_ref, late_ref)
            dmeta_ref[...] = jnp.zeros_like(dmeta_ref)
            dsink_ref[...] = jnp.zeros_like(dsink_ref)
            hold_ref[...] = jnp.zeros_like(hold_ref)

        @pl.when(i < nb)
        def _():
            lane = lax.broadcasted_iota(jnp.int32, (BLOCK, BLOCK), 1)
            lse_tile = lse_ref[...]
            zero = jnp.zeros((BLOCK, BLOCK), F32)
            for g in range(N_KV):
                kcat, vcat = _kv_cat(kvm_ref, kvp_ref, kvc_ref, g)
                heads = range(g * GROUP, (g + 1) * GROUP)
                qs = _stack_heads(q_ref, g)
                dos = _stack_heads(do_ref, g)
                dosb = dos.astype(BF16)
                lse = jnp.concatenate(
                    [jnp.sum(jnp.where(lane == h, lse_tile, 0.0), axis=-1, keepdims=True) + zero for h in heads], axis=0)
                delta = jnp.sum(dos * _stack_heads(o_ref, g), axis=-1, keepdims=True) + jnp.zeros((GROUP * BLOCK, BLOCK), F32)
                s = _dot_nt(qs, kcat) * SCALE + _attn_bias(bias_ref, late_ref, g, i)
                dp = _dot_nt(dosb, vcat)
                ps = [jnp.exp(s[:, k * BLOCK:(k + 1) * BLOCK] - lse) for k in range(3)]
                p = jnp.concatenate(ps, axis=1)
                ds = jnp.concatenate(
                    [(ps[k] * (dp[:, k * BLOCK:(k + 1) * BLOCK] - delta)) * SCALE for k in range(3)], axis=1).astype(BF16)
                sink_term = jnp.exp(_head_rows([sink_ref[h] for h in heads]) - lse)[:, 0:1] * delta[:, 0:1]
                dq = _dot(ds, kcat).astype(BF16)
                for j, h in enumerate(heads):
                    dsink_ref[h:h + 1, :] += -jnp.sum(sink_term[j * BLOCK:(j + 1) * BLOCK])
                    dq_ref[:, h * HEAD_DIM:(h + 1) * HEAD_DIM] = dq[j * BLOCK:(j + 1) * BLOCK]
                dk_t = _dot_tn(qs, ds)
                dv_t = _dot_tn(dosb, p.astype(BF16))
                ks = slice(g * HEAD_DIM, (g + 1) * HEAD_DIM)
                vs = slice(KV_W + g * HEAD_DIM, KV_W + (g + 1) * HEAD_DIM)
                for sl, grad_t in ((ks, dk_t), (vs, dv_t)):
                    dmeta_ref[:, sl] += grad_t[:, 0:BLOCK].T
                    dkv_ref[:, sl] = hold_ref[:, sl] + grad_t[:, BLOCK:2 * BLOCK].T
                    hold_ref[:, sl] = grad_t[:, 2 * BLOCK:3 * BLOCK].T

        @pl.when(i == nb)
        def _():
            dkv_ref[...] = hold_ref[...]
            for cp in _scatter(src, dst, send_sems, recv_sems):
                cp.wait()

    def cur(i):
        return jnp.minimum(i, nb - 1)

    return pl.pallas_call(
        body, name="attn_bwd", grid=(nb + 1,),
        in_specs=[pl.BlockSpec(memory_space=pltpu.SMEM),
                  pl.BlockSpec((BLOCK, ATTN_W), lambda i: (cur(i), 0)),
                  pl.BlockSpec((BLOCK, 2 * KV_W), lambda i: (cur(i), 0)),
                  pl.BlockSpec((BLOCK, 2 * KV_W), lambda i: (jnp.maximum(cur(i) - 1, 0), 0)),
                  _const((BLOCK, 2 * KV_W)),
                  pl.BlockSpec((BLOCK, ATTN_W), lambda i: (cur(i), 0)),
                  pl.BlockSpec((BLOCK, ATTN_W), lambda i: (cur(i), 0)),
                  pl.BlockSpec((BLOCK, BLOCK), lambda i: (cur(i), 0))] + [ANY] * n,
        out_specs=[pl.BlockSpec((BLOCK, ATTN_W), lambda i: (cur(i), 0)),
                   pl.BlockSpec((BLOCK, 2 * KV_W), lambda i: (jnp.maximum(i - 1, 0), 0)),
                   _const((BLOCK, 2 * KV_W)), _const((N_HEADS, BLOCK))] + [ANY] * n,
        out_shape=[jax.ShapeDtypeStruct((r, ATTN_W), BF16), jax.ShapeDtypeStruct((r, 2 * KV_W), F32),
                   jax.ShapeDtypeStruct((BLOCK, 2 * KV_W), F32), jax.ShapeDtypeStruct((N_HEADS, BLOCK), F32)] + _scatter_landing(parts),
        scratch_shapes=[pltpu.VMEM((BLOCK, 2 * KV_W), F32), pltpu.VMEM((N_KV, GROUP * BLOCK, 3 * BLOCK), F32),
                        pltpu.VMEM((GROUP * BLOCK, BLOCK), F32), pltpu.SemaphoreType.DMA((7 * n,)), pltpu.SemaphoreType.DMA((7 * n,))],
        compiler_params=_cparams(),
    )(sinks, q, kv, kv, kv, o, do, lse, *parts)


def _in_proj_bwd(dq, dkv, dkv_meta, dc, dh1, h0, g1, w_in_b, tm):
    r = h0.shape[0]
    shard_cols = IN_COLS // N_SHARD

    def body(dq_ref, dkv_ref, dm_ref, dc_ref, dh1_ref, h_ref, g_ref, w_ref, dh0_ref, dws_ref, dg_ref, dw_ref):
        i = pl.program_id(0)

        @pl.when(i == 0)
        def _():
            dw_ref[...] = jnp.zeros_like(dw_ref)
            dg_ref[...] = jnp.zeros_like(dg_ref)


        meta = jnp.concatenate([dm_ref[...], jnp.zeros((tm - BLOCK, 2 * KV_W), F32)], axis=0) if tm > BLOCK else dm_ref[...]
        first = (jnp.zeros((tm, 2 * KV_W), jnp.int32) + i) == 0
        dkvb = (dkv_ref[...] + jnp.where(first, meta, 0.0)).astype(BF16)
        dqb = dq_ref[...]
        dcb = dc_ref[...]
        hn, xhat, rstd = _rms_fwd(h_ref[...], g_ref[...])
        hnb = hn.astype(BF16)
        dhn = _dot_nt(dqb, w_ref[:, Q0:KV0]) + _dot_nt(dkvb, w_ref[:, KV0:C0]) + _dot_nt(dcb, w_ref[:, C0:IN_COLS])
        dw_ref[:, Q0:KV0] += _dot_tn(hnb, dqb)
        dw_ref[:, KV0:C0] += _dot_tn(hnb, dkvb)
        dw_ref[:, C0:IN_COLS] += _dot_tn(hnb, dcb)
        dx, dg_rows = _rms_bwd(dhn, xhat, rstd, g_ref[...])
        dg_ref[...] += jnp.sum(dg_rows, axis=0, keepdims=True)
        dh0_ref[...] = dh1_ref[...] + dx

        @pl.when(i == r // tm - 1)
        def _():
            for s in range(N_SHARD):
                dws_ref[s] = dw_ref[...][:, s * shard_cols:(s + 1) * shard_cols].astype(BF16)

    return pl.pallas_call(
        body, name="in_proj_bwd", grid=(r // tm,),
        in_specs=[_row(tm, ATTN_W), _row(tm, 2 * KV_W), _const((BLOCK, 2 * KV_W)), _row(tm, 2 * CONV_W), _row(tm, D_MODEL),
                  _row(tm, D_MODEL), _const((1, D_MODEL)), _const((D_MODEL, IN_COLS))],
        out_specs=[_row(tm, D_MODEL), _const((N_SHARD, D_MODEL, shard_cols)), _const((1, D_MODEL))],
        out_shape=[jax.ShapeDtypeStruct((r, D_MODEL), F32), jax.ShapeDtypeStruct((N_SHARD, D_MODEL, shard_cols), BF16),
                   jax.ShapeDtypeStruct((1, D_MODEL), F32)],
        scratch_shapes=[pltpu.VMEM((D_MODEL, IN_COLS), F32)],
        compiler_params=_cparams(),
    )(dq, dkv, dkv_meta, dc, dh1, h0, g1, w_in_b)


def _adamw_update(w_ref, g_ref, m_ref, v_ref, d_ref, nm_ref, nv_ref):
    g = g_ref[...]
    m = ADAM_B1 * m_ref[...] + (1.0 - ADAM_B1) * g
    v = ADAM_B2 * v_ref[...] + (1.0 - ADAM_B2) * (g * g)
    m_hat = m / (1.0 - ADAM_B1 ** ADAM_STEP)
    v_hat = v / (1.0 - ADAM_B2 ** ADAM_STEP)
    d_ref[...] = -ADAM_LR * (m_hat / (jnp.sqrt(v_hat) + ADAM_EPS) + ADAM_WD * w_ref[...])
    nm_ref[...] = m
    nv_ref[...] = v


def _adamw(w, g, m, v, name):
    rows, cols = w.shape
    tr = rows
    for cand in (256, 176, 128, 64, 32, 16, 8):
        if rows % cand == 0:
            tr = cand
            break

    def body(*refs):
        _adamw_update(*refs)

    spec = _row(tr, cols)
    return pl.pallas_call(
        body, name=name, grid=(rows // tr,), in_specs=[spec] * 4, out_specs=[spec] * 3,
        out_shape=[jax.ShapeDtypeStruct((rows, cols), F32)] * 3, compiler_params=_cparams(),
    )(w, g, m, v)


MESH = pl.DeviceIdType.MESH
ANY = pl.BlockSpec(memory_space=pl.ANY)


def _place():
    x, y, c = lax.axis_index("x"), lax.axis_index("y"), lax.axis_index("c")
    chips = [(1 - x, y), (x, 1 - y), (1 - x, 1 - y)]
    return x, y, c, chips


def _gather_ici(src, dst, send_sems, recv_sems):
    x, y, c, chips = _place()
    sends, arrivals = [], []
    for k in range(len(src)):
        rows = src[k].shape[0] // 2
        half = pl.ds(c * rows, rows)
        for p, chip in enumerate(chips):
            sems = dict(send_sem=send_sems.at[3 * k + p], recv_sem=recv_sems.at[3 * k + p], device_id=(chip[0], chip[1], c),
                        device_id_type=MESH)
            sends.append(pltpu.make_async_remote_copy(src_ref=src[k].at[half], dst_ref=dst[k].at[2 * x + y, half], **sems))
            theirs = dst[k].at[2 * chip[0] + chip[1], half]
            arrivals.append(pltpu.make_async_remote_copy(src_ref=theirs, dst_ref=theirs, **sems))
    return sends, arrivals


def _gather_d2d(dst, send_sems, recv_sems):
    x, y, c, chips = _place()
    sends, arrivals = [], []
    for k in range(len(dst)):
        rows = dst[k].shape[1] // 2
        for p, chip in enumerate(chips):
            sems = dict(send_sem=send_sems.at[3 * k + p], recv_sem=recv_sems.at[3 * k + p], device_id=(x, y, 1 - c),
                        device_id_type=MESH)
            mine = dst[k].at[2 * chip[0] + chip[1], pl.ds(c * rows, rows)]
            sends.append(pltpu.make_async_remote_copy(src_ref=mine, dst_ref=mine, **sems))
            theirs = dst[k].at[2 * chip[0] + chip[1], pl.ds((1 - c) * rows, rows)]
            arrivals.append(pltpu.make_async_remote_copy(src_ref=theirs, dst_ref=theirs, **sems))
    return sends, arrivals


def _own_slots(shard):
    return jnp.broadcast_to(shard[None], (N_SHARD,) + shard.shape)


def _gather_weights(shards):
    n = len(shards)

    def body(*refs):
        src, dst = refs[:n], refs[2 * n:3 * n]
        ici_send, ici_recv, d2d_send, d2d_recv = refs[3 * n:]
        sends, arrivals = _gather_ici(src, dst, ici_send, ici_recv)
        for cp in sends:
            cp.start()
        for cp in arrivals:
            cp.wait_recv()
        forwards, from_sibling = _gather_d2d(dst, d2d_send, d2d_recv)
        for cp in forwards:
            cp.start()
        for cp in from_sibling:
            cp.wait_recv()
        for cp in sends + forwards:
            cp.wait_send()

    return pl.pallas_call(
        body, name="gather_weights",
        in_specs=[ANY] * (2 * n), out_specs=[ANY] * n,
        out_shape=[jax.ShapeDtypeStruct((N_SHARD,) + s.shape, s.dtype) for s in shards],
        input_output_aliases={n + k: k for k in range(n)},
        scratch_shapes=[pltpu.SemaphoreType.DMA((3 * n,))] * 4,
    )(*shards, *[_own_slots(s) for s in shards])


VMEM_WHOLE = pl.BlockSpec(memory_space=pltpu.VMEM)


def _allreduce_small(parts):
    widths = sorted({p.shape[1] for p in parts})
    place, heights = [], [0] * len(widths)
    for p in parts:
        gi = widths.index(p.shape[1])
        place.append((gi, heights[gi]))
        heights[gi] += -(-p.shape[0] // 8) * 8
    n, ng = len(parts), len(widths)

    def body(*refs):
        ins, outs, slots = refs[:n], refs[n:2 * n], refs[2 * n:2 * n + ng]
        send_sems, recv_sems = refs[2 * n + ng:]
        x, y, c = lax.axis_index("x"), lax.axis_index("y"), lax.axis_index("c")
        me = 4 * x + 2 * y + c
        for gi in range(ng):
            slots[gi][me] = jnp.zeros((heights[gi], widths[gi]), F32)
        for k, (gi, r0) in enumerate(place):
            slots[gi][me, r0:r0 + parts[k].shape[0], :] = ins[k][...]

        def copy(gi, j, arriving):
            peer = ((x + (j >> 2)) % 2, (y + ((j >> 1) & 1)) % 2, (c + (j & 1)) % 2)
            slot = 4 * peer[0] + 2 * peer[1] + peer[2] if arriving else me
            return pltpu.make_async_remote_copy(
                src_ref=slots[gi].at[me], dst_ref=slots[gi].at[slot], send_sem=send_sems.at[7 * gi + j - 1],
                recv_sem=recv_sems.at[7 * gi + j - 1], device_id=peer, device_id_type=MESH)

        pairs = [(gi, j) for gi in range(ng) for j in range(1, N_DEV)]
        for gi, j in pairs:
            copy(gi, j, False).start()
        for gi, j in pairs:
            copy(gi, j, True).wait_recv()
        totals = []
        for gi in range(ng):
            total = slots[gi][0]
            for d in range(1, N_DEV):
                total = total + slots[gi][d]
            totals.append(total)
        for k, (gi, r0) in enumerate(place):
            outs[k][...] = totals[gi][r0:r0 + parts[k].shape[0], :]
        for gi, j in pairs:
            copy(gi, j, False).wait_send()

    return pl.pallas_call(
        body, name="allreduce_small", in_specs=[VMEM_WHOLE] * n, out_specs=[VMEM_WHOLE] * n,
        out_shape=[jax.ShapeDtypeStruct(p.shape, F32) for p in parts],
        scratch_shapes=[pltpu.VMEM((N_DEV, heights[gi], widths[gi]), F32) for gi in range(ng)]
        + [pltpu.SemaphoreType.DMA((7 * ng,)), pltpu.SemaphoreType.DMA((7 * ng,))],
    )(*parts)


def _adamw_small(ws, gs, ms, vs):
    n = len(ws)

    def body(*refs):
        for k in range(n):
            w_ref, g_ref, m_ref, v_ref = (refs[j * n + k] for j in range(4))
            _adamw_update(w_ref, g_ref, m_ref, v_ref, *(refs[(4 + j) * n + k] for j in range(3)))

    shapes = [jax.ShapeDtypeStruct(w.shape, F32) for w in ws]
    out = pl.pallas_call(
        body, name="adamw_small", in_specs=[VMEM_WHOLE] * (4 * n), out_specs=[VMEM_WHOLE] * (3 * n), out_shape=shapes * 3,
    )(*ws, *gs, *ms, *vs)
    return out[:n], out[n:2 * n], out[2 * n:]


def _scatter(src, dst, send_sems, recv_sems):
    x, y, c = lax.axis_index("x"), lax.axis_index("y"), lax.axis_index("c")
    copies = []
    for k in range(len(src)):
        rows = src[k].shape[1] // 2
        for j in range(1, N_DEV):
            px, py, pc = (x + (j >> 2)) % 2, (y + ((j >> 1) & 1)) % 2, (c + (j & 1)) % 2
            copies.append(pltpu.make_async_remote_copy(
                src_ref=src[k].at[2 * px + py, pl.ds(pc * rows, rows)], dst_ref=dst[k].at[j - 1],
                send_sem=send_sems.at[7 * k + j - 1], recv_sem=recv_sems.at[7 * k + j - 1], device_id=(px, py, pc),
                device_id_type=MESH))
    return copies


def _scatter_landing(parts):
    return [jax.ShapeDtypeStruct((N_DEV - 1, p.shape[1] // 2, p.shape[2]), p.dtype) for p in parts]


def _scatter_call(parts):
    n = len(parts)

    def body(*refs):
        copies = _scatter(refs[:n], refs[n:2 * n], *refs[2 * n:])
        for cp in copies:
            cp.start()
        for cp in copies:
            cp.wait()

    return pl.pallas_call(
        body, name="scatter_grads", in_specs=[ANY] * n, out_specs=[ANY] * n, out_shape=_scatter_landing(parts),
        scratch_shapes=[pltpu.SemaphoreType.DMA((7 * n,)), pltpu.SemaphoreType.DMA((7 * n,))],
    )(*parts)


def _sum_pieces(own, landed):
    n = len(own)

    def body(*refs):
        for k in range(n):
            got = refs[n + k]
            total = refs[k][...].astype(F32)
            for j in range(N_DEV - 1):
                total = total + got[j].astype(F32)
            refs[2 * n + k][...] = total

    in_specs, out_specs = [], []
    for o in own:
        in_specs.append(_row(o.shape[0] // 2, o.shape[1]))
    for o in own:
        in_specs.append(pl.BlockSpec((N_DEV - 1, o.shape[0] // 2, o.shape[1]), lambda i: (0, i, 0)))
        out_specs.append(_row(o.shape[0] // 2, o.shape[1]))
    return pl.pallas_call(
        body, name="sum_pieces", grid=(2,), in_specs=in_specs, out_specs=out_specs,
        out_shape=[jax.ShapeDtypeStruct(o.shape, F32) for o in own], compiler_params=_cparams(),
    )(*own, *landed)


def _swap_with_sibling(halves):
    n = len(halves)

    def body(*refs):
        x, y, c = lax.axis_index("x"), lax.axis_index("y"), lax.axis_index("c")
        copies = [pltpu.make_async_remote_copy(
            src_ref=refs[k], dst_ref=refs[n + k], send_sem=refs[2 * n].at[k], recv_sem=refs[2 * n + 1].at[k],
            device_id=(x, y, 1 - c), device_id_type=MESH) for k in range(n)]
        for cp in copies:
            cp.start()
        for cp in copies:
            cp.wait()

    return pl.pallas_call(
        body, name="swap_with_sibling", in_specs=[ANY] * n, out_specs=[ANY] * n,
        out_shape=[jax.ShapeDtypeStruct(h.shape, h.dtype) for h in halves],
        scratch_shapes=[pltpu.SemaphoreType.DMA((n,)), pltpu.SemaphoreType.DMA((n,))],
    )(*halves)


def _own_piece(part):
    rows = part.shape[1] // 2
    s = 2 * lax.axis_index("x") + lax.axis_index("y")
    return lax.dynamic_slice(part, (s, lax.axis_index("c") * rows, 0), (1, rows, part.shape[2]))[0]


def _both_halves(mine, theirs):
    south = lax.axis_index("c") == 0
    return jnp.concatenate([jnp.where(south, mine, theirs), jnp.where(south, theirs, mine)], axis=0)


def _from_col_shards(g):
    return g.transpose(1, 0, 2).reshape(g.shape[1], -1)


def kernel(x, meta_tokens, attn_norm_g, w_in, attn_sinks, conv_w, conv_b, conv_ln_g, conv_ln_b, attn_out_g, conv_out_g, w_out, ffn_norm_g, w_gate, w_up, w_down, final_norm_g, loss_target, m_meta_tokens, m_attn_norm_g, m_w_in, m_attn_sinks, m_conv_w, m_conv_b, m_conv_ln_g, m_conv_ln_b, m_attn_out_g, m_conv_out_g, m_w_out, m_ffn_norm_g, m_w_gate, m_w_up, m_w_down, m_final_norm_g, v_meta_tokens, v_attn_norm_g, v_w_in, v_attn_sinks, v_conv_w, v_conv_b, v_conv_ln_g, v_conv_ln_b, v_attn_out_g, v_conv_out_g, v_w_out, v_ffn_norm_g, v_w_gate, v_w_up, v_w_down, v_final_norm_g):
    seq = x.shape[1]
    r = -(-(seq + BLOCK) // ROW_QUANTUM) * ROW_QUANTUM
    tail = r - BLOCK - seq
    shard = 2 * lax.axis_index("x") + lax.axis_index("y")

    conv_w32 = jnp.pad(conv_w[0], ((0, 1), (0, 0)))
    small_shard = jnp.concatenate([meta_tokens, conv_w32.reshape(16, 256)], axis=0)
    g_in, g_small = _gather_weights([w_in[0].astype(BF16), small_shard])
    later = [w_gate[0].astype(BF16), w_up[0].astype(BF16), w_out[0].astype(BF16), w_down[0].astype(BF16)]
    w_in_b = _from_col_shards(g_in)
    meta_full = _from_col_shards(g_small[:, 0:N_META])
    cw_full = _from_col_shards(g_small[:, N_META:].reshape(N_SHARD, 32, 128))

    g1, ga, gc, g2 = attn_norm_g, attn_out_g, conv_out_g, ffn_norm_g
    gf = final_norm_g.reshape(1, D_MODEL)
    sinks = attn_sinks[0]

    h0 = jnp.concatenate([jnp.zeros((LEAD, D_MODEL), F32), meta_full, x[0], jnp.zeros((tail, D_MODEL), F32)], axis=0)
    q, kv, cacg = _in_proj(h0, g1, w_in_b, 768)
    oa, lse, *gathered = _attn_fwd(q, kv, sinks, later, [_own_slots(s) for s in later])
    oc, yc, g_gate, g_up, g_out, g_down = _conv_fwd(cacg, cw_full, conv_b, conv_ln_g, conv_ln_b, 384, gathered)
    wg_b = _from_col_shards(g_gate)
    wu_b = _from_col_shards(g_up)
    w_out_b = g_out.reshape(D_MODEL, D_MODEL)
    wd_b = g_down.reshape(D_FF, D_MODEL)
    h1, hn2 = _out_proj(oa, oc, h0, ga, gc, g2, w_out_b, 768)
    gate, up, act, dh2, dh2b, loss_p, dgf = _ffn_fwd(hn2, h1, loss_target[0], gf, wg_b, wu_b, wd_b, 384)

    dgate, dup, dh1, dg2 = _ffn_bwd(dh2, dh2b, gate, up, h1, g2, wg_b, wu_b, wd_b, 384)
    p_gate, p_up = _ffn_wgrad_gu(hn2, dgate, dup, 768)
    p_down = _ffn_wgrad_d(act, dh2b, 768).reshape(N_SHARD, D_FF // N_SHARD, D_MODEL)
    doa, doc, dwo, dga, dgc = _out_proj_bwd(dh1, oa, oc, ga, gc, w_out_b, 768)
    p_out = dwo.reshape(N_SHARD, D_MODEL // N_SHARD, D_MODEL)
    dy, dcw, dcb, dlg, dlb, l_gate, l_up = _conv_bwd_params(doc, yc, cacg, conv_ln_g, conv_ln_b, 384, [p_gate, p_up])
    dc = _conv_bwd_data(dy, cacg, cw_full, 384)
    dq, dkv, dkv_meta, dsink, l_out, l_down = _attn_bwd(q, kv, oa, doa, lse, sinks, [p_out, p_down])
    dh0, p_in, dg1 = _in_proj_bwd(dq, dkv, dkv_meta, dc, dh1, h0, g1, w_in_b, 768)
    grad_x = dh0[BLOCK:BLOCK + seq][None]
    l_in, = _scatter_call([p_in])

    red_names = ("final_norm_g", "attn_norm_g", "ffn_norm_g", "meta_tokens", "attn_out_g", "conv_out_g", "conv_b", "conv_ln_g",
                 "conv_ln_b", "conv_w", "loss", "attn_sinks")
    red = dict(zip(red_names, _allreduce_small([dgf, dg1, dg2, dh0[LEAD:BLOCK], dga, dgc, dcb, dlg, dlb, dcw, loss_p, dsink])))
    loss = red["loss"][0, 0]
    red["final_norm_g"] = red["final_norm_g"].reshape(D_MODEL)
    red["attn_sinks"] = red["attn_sinks"][:, 0].reshape(1, N_HEADS)
    g_meta = lax.dynamic_slice_in_dim(red["meta_tokens"], shard * (D_MODEL // N_SHARD), D_MODEL // N_SHARD, axis=1)
    g_convw = lax.dynamic_slice_in_dim(red["conv_w"][0:CONV_K], shard * (CONV_W // N_SHARD), CONV_W // N_SHARD, axis=1)[None]

    halves = _sum_pieces([_own_piece(p) for p in (p_in, p_gate, p_up, p_out, p_down)], [l_in, l_gate, l_up, l_out, l_down])
    g_w_in, g_w_gate, g_w_up, g_w_out, g_w_down = [
        _both_halves(mine, theirs) for mine, theirs in zip(halves, _swap_with_sibling(halves))]

    grads = {
        "meta_tokens": g_meta, "attn_norm_g": red["attn_norm_g"], "w_in": g_w_in[None], "attn_sinks": red["attn_sinks"],
        "conv_w": g_convw, "conv_b": red["conv_b"], "conv_ln_g": red["conv_ln_g"], "conv_ln_b": red["conv_ln_b"],
        "attn_out_g": red["attn_out_g"], "conv_out_g": red["conv_out_g"], "w_out": g_w_out[None], "ffn_norm_g": red["ffn_norm_g"],
        "w_gate": g_w_gate[None], "w_up": g_w_up[None], "w_down": g_w_down[None], "final_norm_g": red["final_norm_g"]}
    params = {
        "meta_tokens": (meta_tokens, m_meta_tokens, v_meta_tokens), "attn_norm_g": (attn_norm_g, m_attn_norm_g, v_attn_norm_g),
        "w_in": (w_in, m_w_in, v_w_in), "attn_sinks": (attn_sinks, m_attn_sinks, v_attn_sinks), "conv_w": (conv_w, m_conv_w, v_conv_w),
        "conv_b": (conv_b, m_conv_b, v_conv_b), "conv_ln_g": (conv_ln_g, m_conv_ln_g, v_conv_ln_g),
        "conv_ln_b": (conv_ln_b, m_conv_ln_b, v_conv_ln_b), "attn_out_g": (attn_out_g, m_attn_out_g, v_attn_out_g),
        "conv_out_g": (conv_out_g, m_conv_out_g, v_conv_out_g), "w_out": (w_out, m_w_out, v_w_out),
        "ffn_norm_g": (ffn_norm_g, m_ffn_norm_g, v_ffn_norm_g), "w_gate": (w_gate, m_w_gate, v_w_gate), "w_up": (w_up, m_w_up, v_w_up),
        "w_down": (w_down, m_w_down, v_w_down), "final_norm_g": (final_norm_g, m_final_norm_g, v_final_norm_g)}
    names = list(params)
    big = ("w_in", "w_out", "w_gate", "w_up", "w_down")
    delta, new_m, new_v = {}, {}, {}
    for name in big:
        w, m, v = params[name]
        d, nm, nv = _adamw(w[0], grads[name][0], m[0], v[0], "adamw_" + name)
        delta[name], new_m[name], new_v[name] = d[None], nm[None], nv[None]
    rest = [name for name in names if name not in big]

    def rows_of(a):
        return a.reshape(-1, a.shape[-1])

    small = _adamw_small([rows_of(params[n][0]) for n in rest], [rows_of(grads[n]) for n in rest],
                         [rows_of(params[n][1]) for n in rest], [rows_of(params[n][2]) for n in rest])
    for dst, outs in zip((delta, new_m, new_v), small):
        for name, out in zip(rest, outs):
            dst[name] = out.reshape(params[name][0].shape)

    return (loss, grad_x, *[grads[n] for n in names], *[delta[n] for n in names], *[new_m[n] for n in names],
            *[new_v[n] for n in names])
```

```python
import functools
import math

import jax
import jax.numpy as jnp
from jax import lax
from jax.experimental import pallas as pl
from jax.experimental.pallas import tpu as pltpu

F32 = jnp.float32
BF16 = jnp.bfloat16

D_MODEL = 1024
N_META = 16
ATTN_W = 512
CONV_W = 512
HEAD_DIM = 64
N_HEADS = 8
N_KV = 2
GROUP = N_HEADS // N_KV
KV_W = N_KV * HEAD_DIM
BLOCK = 128
LEAD = BLOCK - N_META
CONV_K = 31
D_FF = 2816
IN_COLS = ATTN_W + 2 * KV_W + 2 * CONV_W
Q0, KV0, C0 = 0, ATTN_W, ATTN_W + 2 * KV_W
NORM_EPS = 1e-5
SCALE = 1.0 / math.sqrt(HEAD_DIM)
SLOPES = tuple(2.0 ** (-(8.0 / N_HEADS) * (h + 1)) for h in range(N_HEADS))
NEG = -1e30

ADAM_LR, ADAM_B1, ADAM_B2, ADAM_EPS, ADAM_WD, ADAM_STEP = 0.001, 0.9, 0.999, 1e-08, 0.01, 10

N_SHARD = 4
N_DEV = 8
ROW_QUANTUM = 768
HALO = 32
CONV_CHUNK = 32
FF_CHUNK = 256
VMEM_LIMIT = 60 * 1024 * 1024


def _cparams(n_axes=1):
    return pltpu.CompilerParams(dimension_semantics=("arbitrary",) * n_axes, vmem_limit_bytes=VMEM_LIMIT)


def _dot(a, b):
    return jnp.dot(a, b, preferred_element_type=F32)


def _dot_nt(a, b):
    return lax.dot_general(a, b, (((1,), (1,)), ((), ())), preferred_element_type=F32)


def _dot_tn(a, b):
    return lax.dot_general(a, b, (((0,), (0,)), ((), ())), preferred_element_type=F32)


def _sigmoid(x):
    return 1.0 / (1.0 + jnp.exp(-x))


def _row(tm, n):
    return pl.BlockSpec((tm, n), lambda i: (i, 0))


def _const(shape):
    return pl.BlockSpec(shape, lambda i: (0,) * len(shape))


def _resident(shape):
    return pl.BlockSpec(shape, lambda i: (0,) * len(shape), pipeline_mode=pl.Buffered(1))


def _rms_fwd(x, g):
    rstd = lax.rsqrt(jnp.mean(x * x, axis=-1, keepdims=True) + NORM_EPS)
    xhat = x * rstd
    return xhat * g, xhat, rstd


def _rms_bwd(dy, xhat, rstd, g):
    dxh = dy * g
    dx = rstd * (dxh - xhat * jnp.mean(dxh * xhat, axis=-1, keepdims=True))
    return dx, dy * xhat


def _in_proj(h0, g1, w_in_t, tm):
    r = h0.shape[0]

    def body(h_ref, g_ref, w_ref, q_ref, kv_ref, c_ref):
        hn = _rms_fwd(h_ref[...], g_ref[...])[0].astype(BF16)
        q_ref[...] = _dot_nt(hn, w_ref[Q0:KV0, :]).astype(BF16)
        kv_ref[...] = _dot_nt(hn, w_ref[KV0:C0, :]).astype(BF16)
        c_ref[...] = _dot_nt(hn, w_ref[C0:IN_COLS, :])

    return pl.pallas_call(
        body, name="in_proj", grid=(r // tm,),
        in_specs=[_row(tm, D_MODEL), _const((1, D_MODEL)), _const((IN_COLS, D_MODEL))],
        out_specs=[_row(tm, ATTN_W), _row(tm, 2 * KV_W), _row(tm, 2 * CONV_W)],
        out_shape=[jax.ShapeDtypeStruct((r, ATTN_W), BF16), jax.ShapeDtypeStruct((r, 2 * KV_W), BF16),
                   jax.ShapeDtypeStruct((r, 2 * CONV_W), F32)],
        compiler_params=_cparams(),
    )(h0, g1, w_in_t)


def _attn_bias_init(bias_ref, late_ref):
    row = lax.broadcasted_iota(jnp.int32, (GROUP * BLOCK, BLOCK), 0) & (BLOCK - 1)
    col = lax.broadcasted_iota(jnp.int32, (GROUP * BLOCK, BLOCK), 1)
    late_ref[...] = jnp.where(col > row, 1.0, 0.0)
    for g in range(N_KV):
        slope = jnp.concatenate([jnp.zeros((BLOCK, BLOCK), F32) + SLOPES[g * GROUP + j] for j in range(GROUP)], axis=0)
        bias_ref[g, :, 0:BLOCK] = jnp.where(col >= LEAD, 0.0, NEG)
        bias_ref[g, :, BLOCK:2 * BLOCK] = jnp.where(col > row, -slope * (row - col + BLOCK).astype(F32), NEG)
        bias_ref[g, :, 2 * BLOCK:3 * BLOCK] = jnp.where(col <= row, -slope * (row - col).astype(F32), NEG)


def _attn_bias(bias_ref, late_ref, g, i):
    meta0 = jnp.where(i == 0, NEG, 0.0)
    no_prev = jnp.where(i >= 2, 0.0, NEG)
    no_cur = jnp.where(i >= 1, 0.0, NEG)
    return jnp.concatenate([bias_ref[g, :, 0:BLOCK] + late_ref[...] * meta0, bias_ref[g, :, BLOCK:2 * BLOCK] + no_prev,
                            bias_ref[g, :, 2 * BLOCK:3 * BLOCK] + no_cur], axis=1)


def _head_rows(vals):
    return jnp.concatenate([jnp.zeros((BLOCK, BLOCK), F32) + v for v in vals], axis=0)


def _stack_heads(ref, g):
    return jnp.concatenate([ref[:, (g * GROUP + j) * HEAD_DIM:(g * GROUP + j + 1) * HEAD_DIM] for j in range(GROUP)], axis=0)


def _kv_cat(kvm_ref, kvp_ref, kvc_ref, g):
    ks = slice(g * HEAD_DIM, (g + 1) * HEAD_DIM)
    vs = slice(KV_W + g * HEAD_DIM, KV_W + (g + 1) * HEAD_DIM)
    kcat = jnp.concatenate([kvm_ref[:, ks], kvp_ref[:, ks], kvc_ref[:, ks]], axis=0)
    vcat = jnp.concatenate([kvm_ref[:, vs], kvp_ref[:, vs], kvc_ref[:, vs]], axis=0)
    return kcat, vcat


def _attn_fwd(q, kv, sinks, shards, gathered):
    r = q.shape[0]
    nb = r // BLOCK
    n = len(shards)

    def body(sink_ref, q_ref, kvc_ref, kvp_ref, kvm_ref, *rest):
        src = rest[:n]
        o_ref, lse_ref = rest[2 * n:2 * n + 2]
        dst = rest[2 * n + 2:3 * n + 2]
        bias_ref, late_ref, send_sems, recv_sems = rest[3 * n + 2:]
        i = pl.program_id(0)

        @pl.when(i == 0)
        def _():
            for cp in _gather_ici(src, dst, send_sems, recv_sems)[0]:
                cp.start()
            _attn_bias_init(bias_ref, late_ref)

        lane = lax.broadcasted_iota(jnp.int32, (BLOCK, BLOCK), 1)
        lse_tile = jnp.zeros((BLOCK, BLOCK), F32)
        for g in range(N_KV):
            kcat, vcat = _kv_cat(kvm_ref, kvp_ref, kvc_ref, g)
            heads = range(g * GROUP, (g + 1) * GROUP)
            s = _dot_nt(_stack_heads(q_ref, g), kcat) * SCALE + _attn_bias(bias_ref, late_ref, g, i)
            segs = [s[:, k * BLOCK:(k + 1) * BLOCK] for k in range(3)]
            sink = _head_rows([sink_ref[h] for h in heads])
            m = jnp.maximum(jnp.max(jnp.maximum(jnp.maximum(segs[0], segs[1]), segs[2]), axis=-1, keepdims=True), sink)
            ps = [jnp.exp(x - m) for x in segs]
            l = jnp.sum(ps[0] + ps[1] + ps[2], axis=-1, keepdims=True) + jnp.exp(sink - m)
            o = _dot(jnp.concatenate(ps, axis=1).astype(BF16), vcat) * (1.0 / l)[:, 0:HEAD_DIM]
            lse = m + jnp.log(l)
            for j, h in enumerate(heads):
                o_ref[:, h * HEAD_DIM:(h + 1) * HEAD_DIM] = o[j * BLOCK:(j + 1) * BLOCK]
                lse_tile = jnp.where(lane == h, lse[j * BLOCK:(j + 1) * BLOCK], lse_tile)
        lse_ref[...] = lse_tile

        @pl.when(i == nb - 1)
        def _():
            sends, arrivals = _gather_ici(src, dst, send_sems, recv_sems)
            for cp in arrivals:
                cp.wait_recv()
            for cp in sends:
                cp.wait_send()

    return pl.pallas_call(
        body, name="attn_fwd", grid=(nb,),
        in_specs=[pl.BlockSpec(memory_space=pltpu.SMEM), _row(BLOCK, ATTN_W), _row(BLOCK, 2 * KV_W),
                  pl.BlockSpec((BLOCK, 2 * KV_W), lambda i: (jnp.maximum(i - 1, 0), 0)), _const((BLOCK, 2 * KV_W))] + [ANY] * (2 * n),
        out_specs=[_row(BLOCK, ATTN_W), _row(BLOCK, BLOCK)] + [ANY] * n,
        out_shape=[jax.ShapeDtypeStruct((r, ATTN_W), F32), jax.ShapeDtypeStruct((r, BLOCK), F32)]
        + [jax.ShapeDtypeStruct(g.shape, g.dtype) for g in gathered],
        input_output_aliases={5 + n + k: 2 + k for k in range(n)},
        scratch_shapes=[pltpu.VMEM((N_KV, GROUP * BLOCK, 3 * BLOCK), F32), pltpu.VMEM((GROUP * BLOCK, BLOCK), F32),
                        pltpu.SemaphoreType.DMA((3 * n,)), pltpu.SemaphoreType.DMA((3 * n,))],
        compiler_params=_cparams(),
    )(sinks, q, kv, kv, kv, *shards, *gathered)


def _shifted_copies(ub_ref, win):
    w = win.shape[0]
    ub_ref[0] = win
    for b in range(1, 8):
        ub_ref[b] = pltpu.roll(win, shift=w - b, axis=0)


def _conv_chunk(ub_ref, w_ref, r0, shifts):
    acc = jnp.zeros((CONV_CHUNK, CONV_W), F32)
    for j in range(CONV_K):
        a, b = divmod(shifts[j], 8)
        acc = acc + w_ref[j:j + 1, :] * ub_ref[b, pl.ds(r0 + 8 * a, CONV_CHUNK), :]
    return acc


FWD_SHIFTS = tuple(HALO - (CONV_K - 1) + j for j in range(CONV_K))
BWD_SHIFTS = tuple(CONV_K - 1 - j for j in range(CONV_K))


def _glu_window(cp_ref, c_ref, i):
    tile = c_ref[:, 0:CONV_W] * _sigmoid(c_ref[:, CONV_W:2 * CONV_W])
    halo = cp_ref[:, 0:CONV_W] * _sigmoid(cp_ref[:, CONV_W:2 * CONV_W])
    first = (jnp.zeros((HALO, CONV_W), jnp.int32) + i) == 0
    return jnp.concatenate([jnp.where(first, 0.0, halo), tile], axis=0)


def _halo_before(tm, n):
    return pl.BlockSpec((HALO, n), lambda i: (jnp.maximum(i * (tm // HALO) - 1, 0), 0))


def _conv_fwd(cacg, cw, cb, lg, lb, tm, gathered):
    r = cacg.shape[0]
    n = len(gathered)

    def body(c_ref, cp_ref, w_ref, cb_ref, lg_ref, lb_ref, *rest):
        o_ref, y_ref = rest[n:n + 2]
        dst = rest[n + 2:2 * n + 2]
        ub_ref, send_sems, recv_sems = rest[2 * n + 2:]
        i = pl.program_id(0)

        @pl.when(i == 0)
        def _():
            for cp in _gather_d2d(dst, send_sems, recv_sems)[0]:
                cp.start()

        _shifted_copies(ub_ref, _glu_window(cp_ref, c_ref, i))

        def chunk(ci, carry):
            r0 = pl.multiple_of(ci * CONV_CHUNK, CONV_CHUNK)
            y = _conv_chunk(ub_ref, w_ref, r0, FWD_SHIFTS) + cb_ref[...]
            yc = y - jnp.mean(y, axis=-1, keepdims=True)
            rs = lax.rsqrt(jnp.mean(yc * yc, axis=-1, keepdims=True) + NORM_EPS)
            yn = yc * rs * lg_ref[...] + lb_ref[...]
            o_ref[pl.ds(r0, CONV_CHUNK), :] = yn * _sigmoid(yn)
            y_ref[pl.ds(r0, CONV_CHUNK), :] = y
            return carry

        lax.fori_loop(0, tm // CONV_CHUNK, chunk, 0)

        @pl.when(i == r // tm - 1)
        def _():
            sends, arrivals = _gather_d2d(dst, send_sems, recv_sems)
            for cp in arrivals:
                cp.wait_recv()
            for cp in sends:
                cp.wait_send()

    return pl.pallas_call(
        body, name="conv_fwd", grid=(r // tm,),
        in_specs=[_row(tm, 2 * CONV_W), _halo_before(tm, 2 * CONV_W), _const((32, CONV_W)), _const((1, CONV_W)),
                  _const((1, CONV_W)), _const((1, CONV_W))] + [ANY] * n,
        out_specs=[_row(tm, CONV_W), _row(tm, CONV_W)] + [ANY] * n,
        out_shape=[jax.ShapeDtypeStruct((r, CONV_W), F32)] * 2 + [jax.ShapeDtypeStruct(g.shape, g.dtype) for g in gathered],
        input_output_aliases={6 + k: 2 + k for k in range(n)},
        scratch_shapes=[pltpu.VMEM((8, tm + HALO, CONV_W), F32), pltpu.SemaphoreType.DMA((3 * n,)), pltpu.SemaphoreType.DMA((3 * n,))],
        compiler_params=_cparams(),
    )(cacg, cacg, cw, cb, lg, lb, *gathered)


def _out_proj(oa, oc, h0, ga, gc, g2, w_out_b, tm):
    r = h0.shape[0]

    def body(oa_ref, oc_ref, h_ref, ga_ref, gc_ref, g2_ref, w_ref, h1_ref, hn2_ref):
        ma = _rms_fwd(oa_ref[...], ga_ref[...])[0].astype(BF16)
        mc = _rms_fwd(oc_ref[...], gc_ref[...])[0].astype(BF16)
        h1 = h_ref[...] + _dot(jnp.concatenate([ma, mc], axis=1), w_ref[...])
        h1_ref[...] = h1
        hn2_ref[...] = _rms_fwd(h1, g2_ref[...])[0].astype(BF16)

    return pl.pallas_call(
        body, name="out_proj", grid=(r // tm,),
        in_specs=[_row(tm, ATTN_W), _row(tm, CONV_W), _row(tm, D_MODEL), _const((1, ATTN_W)), _const((1, CONV_W)),
                  _const((1, D_MODEL)), _const((D_MODEL, D_MODEL))],
        out_specs=[_row(tm, D_MODEL), _row(tm, D_MODEL)],
        out_shape=[jax.ShapeDtypeStruct((r, D_MODEL), F32), jax.ShapeDtypeStruct((r, D_MODEL), BF16)],
        compiler_params=_cparams(),
    )(oa, oc, h0, ga, gc, g2, w_out_b)


def _ffn_fwd(hn2, h1, target, gf, wg_t, wu_t, wd_b, tm):
    r = h1.shape[0]
    seq = target.shape[0]
    n_sub = tm // BLOCK

    def body(hn_ref, h1_ref, *rest):
        t_refs = rest[:n_sub]
        gf_ref, wg_ref, wu_ref, wd_ref, gate_ref, up_ref, act_ref, dh2_ref, dh2b_ref, loss_ref, dgf_ref = rest[n_sub:]
        i = pl.program_id(0)

        @pl.when(i == 0)
        def _():
            loss_ref[...] = jnp.zeros_like(loss_ref)
            dgf_ref[...] = jnp.zeros_like(dgf_ref)

        hn = hn_ref[...]
        for ch in range(D_FF // FF_CHUNK):
            cs = slice(ch * FF_CHUNK, (ch + 1) * FF_CHUNK)
            gate = _dot_nt(hn, wg_ref[cs, :])
            up = _dot_nt(hn, wu_ref[cs, :])
            gate_ref[:, cs] = gate.astype(BF16)
            up_ref[:, cs] = up.astype(BF16)
            act_ref[:, cs] = (gate * _sigmoid(gate) * up).astype(BF16)
        y, xhat, rstd = _rms_fwd(h1_ref[...] + _dot(act_ref[...], wd_ref[...]), gf_ref[...])
        rows = lax.broadcasted_iota(jnp.int32, (tm, D_MODEL), 0) + i * tm
        real = (rows >= BLOCK) & (rows < BLOCK + seq)
        err = jnp.where(real, y - jnp.concatenate([t[...] for t in t_refs], axis=0), 0.0)
        loss_ref[...] += jnp.sum(err * err) * (0.5 / D_MODEL)
        dy = err * (1.0 / D_MODEL)
        dh2, dg_rows = _rms_bwd(dy, xhat, rstd, gf_ref[...])
        dgf_ref[...] += jnp.sum(dg_rows, axis=0, keepdims=True)
        dh2_ref[...] = dh2
        dh2b_ref[...] = dh2.astype(BF16)

    def target_block(k):
        return pl.BlockSpec((BLOCK, D_MODEL), lambda i: (jnp.clip(n_sub * i - 1 + k, 0, seq // BLOCK - 1), 0))

    return pl.pallas_call(
        body, name="ffn_fwd", grid=(r // tm,),
        in_specs=[_row(tm, D_MODEL), _row(tm, D_MODEL)] + [target_block(k) for k in range(n_sub)]
        + [_const((1, D_MODEL))] + [_resident((D_FF, D_MODEL))] * 3,
        out_specs=[_row(tm, D_FF)] * 3 + [_row(tm, D_MODEL), _row(tm, D_MODEL), _const((1, BLOCK)), _const((1, D_MODEL))],
        out_shape=[jax.ShapeDtypeStruct((r, D_FF), BF16)] * 3
        + [jax.ShapeDtypeStruct((r, D_MODEL), F32), jax.ShapeDtypeStruct((r, D_MODEL), BF16),
           jax.ShapeDtypeStruct((1, BLOCK), F32), jax.ShapeDtypeStruct((1, D_MODEL), F32)],
        compiler_params=_cparams(),
    )(hn2, h1, *[target] * n_sub, gf, wg_t, wu_t, wd_b)


def _ffn_bwd(dh2, dh2b, gate, up, h1, g2, wg_t, wu_t, wd_b, tm):
    r = h1.shape[0]

    def body(dh2_ref, dh2b_ref, gate_ref, up_ref, h1_ref, g2_ref, wg_ref, wu_ref, wd_ref, dgate_ref, dup_ref, dh1_ref, dg2_ref):
        @pl.when(pl.program_id(0) == 0)
        def _():
            dg2_ref[...] = jnp.zeros_like(dg2_ref)

        dyb = dh2b_ref[...]
        for ch in range(D_FF // FF_CHUNK):
            cs = slice(ch * FF_CHUNK, (ch + 1) * FF_CHUNK)
            dact = _dot_nt(dyb, wd_ref[cs, :])
            gate = gate_ref[:, cs].astype(F32)
            up = up_ref[:, cs].astype(F32)
            sg = _sigmoid(gate)
            dgate_ref[:, cs] = (dact * up * (sg * (1.0 + gate * (1.0 - sg)))).astype(BF16)
            dup_ref[:, cs] = (dact * (gate * sg)).astype(BF16)
        dhn = _dot(dgate_ref[...], wg_ref[...]) + _dot(dup_ref[...], wu_ref[...])
        _, xhat, rstd = _rms_fwd(h1_ref[...], g2_ref[...])
        dx, dg_rows = _rms_bwd(dhn, xhat, rstd, g2_ref[...])
        dg2_ref[...] += jnp.sum(dg_rows, axis=0, keepdims=True)
        dh1_ref[...] = dh2_ref[...] + dx

    return pl.pallas_call(
        body, name="ffn_bwd", grid=(r // tm,),
        in_specs=[_row(tm, D_MODEL), _row(tm, D_MODEL), _row(tm, D_FF), _row(tm, D_FF), _row(tm, D_MODEL), _const((1, D_MODEL))]
        + [_resident((D_FF, D_MODEL))] * 3,
        out_specs=[_row(tm, D_FF), _row(tm, D_FF), _row(tm, D_MODEL), _const((1, D_MODEL))],
        out_shape=[jax.ShapeDtypeStruct((r, D_FF), BF16), jax.ShapeDtypeStruct((r, D_FF), BF16),
                   jax.ShapeDtypeStruct((r, D_MODEL), F32), jax.ShapeDtypeStruct((1, D_MODEL), F32)],
        compiler_params=_cparams(),
    )(dh2, dh2b, gate, up, h1, g2, wg_t, wu_t, wd_b)


FF_HALF = D_FF // 2


def _ffn_wgrad_gu(hn2, dgate, dup, tk):
    r = hn2.shape[0]
    n_k = r // tk

    def body(hn_ref, dg_ref, du_ref, wg_ref, wu_ref, accg_ref, accu_ref):
        k = pl.program_id(1)

        @pl.when(k == 0)
        def _():
            accg_ref[...] = jnp.zeros_like(accg_ref)
            accu_ref[...] = jnp.zeros_like(accu_ref)

        hn = hn_ref[...]
        accg_ref[...] += _dot_tn(dg_ref[...], hn)
        accu_ref[...] += _dot_tn(du_ref[...], hn)

        @pl.when(k == n_k - 1)
        def _():
            wg_ref[...] = accg_ref[...].astype(BF16)
            wu_ref[...] = accu_ref[...].astype(BF16)

    col = pl.BlockSpec((tk, FF_HALF), lambda j, k: (k, j))
    out = pl.BlockSpec((FF_HALF, D_MODEL), lambda j, k: (j, 0))
    return pl.pallas_call(
        body, name="ffn_wgrad_gu", grid=(2, n_k),
        in_specs=[pl.BlockSpec((tk, D_MODEL), lambda j, k: (k, 0)), col, col],
        out_specs=[out, out],
        out_shape=[jax.ShapeDtypeStruct((D_FF, D_MODEL), BF16)] * 2,
        scratch_shapes=[pltpu.VMEM((FF_HALF, D_MODEL), F32)] * 2,
        compiler_params=_cparams(2),
    )(hn2, dgate, dup)


def _ffn_wgrad_d(act, dh2b, tk):
    r = act.shape[0]
    n_k = r // tk

    def body(a_ref, dy_ref, wd_ref, acc_ref):
        k = pl.program_id(1)

        @pl.when(k == 0)
        def _():
            acc_ref[...] = jnp.zeros_like(acc_ref)

        acc_ref[...] += _dot_tn(a_ref[...], dy_ref[...])

        @pl.when(k == n_k - 1)
        def _():
            wd_ref[...] = acc_ref[...].astype(BF16)

    return pl.pallas_call(
        body, name="ffn_wgrad_d", grid=(2, n_k),
        in_specs=[pl.BlockSpec((tk, FF_HALF), lambda j, k: (k, j)), pl.BlockSpec((tk, D_MODEL), lambda j, k: (k, 0))],
        out_specs=pl.BlockSpec((FF_HALF, D_MODEL), lambda j, k: (j, 0)),
        out_shape=jax.ShapeDtypeStruct((D_FF, D_MODEL), BF16),
        scratch_shapes=[pltpu.VMEM((FF_HALF, D_MODEL), F32)],
        compiler_params=_cparams(2),
    )(act, dh2b)


def _out_proj_bwd(dh1, oa, oc, ga, gc, w_out_b, tm):
    r = dh1.shape[0]

    def body(dh_ref, oa_ref, oc_ref, ga_ref, gc_ref, w_ref, doa_ref, doc_ref, dw_ref, dga_ref, dgc_ref, acc_ref):
        i = pl.program_id(0)

        @pl.when(i == 0)
        def _():
            acc_ref[...] = jnp.zeros_like(acc_ref)
            dga_ref[...] = jnp.zeros_like(dga_ref)
            dgc_ref[...] = jnp.zeros_like(dgc_ref)

        dhb = dh_ref[...].astype(BF16)
        dmix = _dot_nt(dhb, w_ref[...])
        ma, xa, ra = _rms_fwd(oa_ref[...], ga_ref[...])
        mc, xc, rc = _rms_fwd(oc_ref[...], gc_ref[...])
        acc_ref[...] += _dot_tn(jnp.concatenate([ma.astype(BF16), mc.astype(BF16)], axis=1), dhb)

        @pl.when(i == r // tm - 1)
        def _():
            dw_ref[...] = acc_ref[...].astype(BF16)

        doa, dga_rows = _rms_bwd(dmix[:, 0:ATTN_W], xa, ra, ga_ref[...])
        doc, dgc_rows = _rms_bwd(dmix[:, ATTN_W:ATTN_W + CONV_W], xc, rc, gc_ref[...])
        doa_ref[...] = doa
        doc_ref[...] = doc
        dga_ref[...] += jnp.sum(dga_rows, axis=0, keepdims=True)
        dgc_ref[...] += jnp.sum(dgc_rows, axis=0, keepdims=True)

    return pl.pallas_call(
        body, name="out_proj_bwd", grid=(r // tm,),
        in_specs=[_row(tm, D_MODEL), _row(tm, ATTN_W), _row(tm, CONV_W), _const((1, ATTN_W)), _const((1, CONV_W)),
                  _const((D_MODEL, D_MODEL))],
        out_specs=[_row(tm, ATTN_W), _row(tm, CONV_W), _const((D_MODEL, D_MODEL)), _const((1, ATTN_W)), _const((1, CONV_W))],
        out_shape=[jax.ShapeDtypeStruct((r, ATTN_W), F32), jax.ShapeDtypeStruct((r, CONV_W), F32),
                   jax.ShapeDtypeStruct((D_MODEL, D_MODEL), BF16), jax.ShapeDtypeStruct((1, ATTN_W), F32),
                   jax.ShapeDtypeStruct((1, CONV_W), F32)],
        scratch_shapes=[pltpu.VMEM((D_MODEL, D_MODEL), F32)],
        compiler_params=_cparams(),
    )(dh1, oa, oc, ga, gc, w_out_b)


def _conv_bwd_params(doc, y, cacg, lg, lb, tm, parts):
    r = cacg.shape[0]
    n_steps = r // tm
    n = len(parts)

    def body(do_ref, y_ref, c_ref, cp_ref, lg_ref, lb_ref, *rest):
        src = rest[:n]
        dy_ref, dcw_ref, dcb_ref, dlg_ref, dlb_ref = rest[n:n + 5]
        dst = rest[n + 5:2 * n + 5]
        ub_ref, accw_ref, send_sems, recv_sems = rest[2 * n + 5:]
        i = pl.program_id(0)

        @pl.when(i == 0)
        def _():
            for cp in _scatter(src, dst, send_sems, recv_sems):
                cp.start()
            accw_ref[...] = jnp.zeros_like(accw_ref)
            dcb_ref[...] = jnp.zeros_like(dcb_ref)
            dlg_ref[...] = jnp.zeros_like(dlg_ref)
            dlb_ref[...] = jnp.zeros_like(dlb_ref)

        _shifted_copies(ub_ref, _glu_window(cp_ref, c_ref, i))

        def chunk(ci, carry):
            r0 = pl.multiple_of(ci * CONV_CHUNK, CONV_CHUNK)
            y = y_ref[pl.ds(r0, CONV_CHUNK), :]
            yc = y - jnp.mean(y, axis=-1, keepdims=True)
            rs = lax.rsqrt(jnp.mean(yc * yc, axis=-1, keepdims=True) + NORM_EPS)
            xhat = yc * rs
            yn = xhat * lg_ref[...] + lb_ref[...]
            sg = _sigmoid(yn)
            dyn = do_ref[pl.ds(r0, CONV_CHUNK), :] * (sg * (1.0 + yn * (1.0 - sg)))
            dlg_ref[...] += jnp.sum(dyn * xhat, axis=0, keepdims=True)
            dlb_ref[...] += jnp.sum(dyn, axis=0, keepdims=True)
            dxh = dyn * lg_ref[...]
            dy = rs * (dxh - jnp.mean(dxh, axis=-1, keepdims=True) - xhat * jnp.mean(dxh * xhat, axis=-1, keepdims=True))
            dcb_ref[...] += jnp.sum(dy, axis=0, keepdims=True)
            dy_ref[pl.ds(r0, CONV_CHUNK), :] = dy
            for j in range(CONV_K):
                a, b = divmod(FWD_SHIFTS[j], 8)
                prod = dy * ub_ref[b, pl.ds(r0 + 8 * a, CONV_CHUNK), :]
                accw_ref[j] += jnp.sum(prod.reshape(CONV_CHUNK // 8, 8, CONV_W), axis=0)
            return carry

        lax.fori_loop(0, tm // CONV_CHUNK, chunk, 0)

        @pl.when(i == n_steps - 1)
        def _():
            for j in range(32):
                dcw_ref[j:j + 1, :] = jnp.sum(accw_ref[j], axis=0, keepdims=True)
            for cp in _scatter(src, dst, send_sems, recv_sems):
                cp.wait()

    vec = _const((1, CONV_W))
    return pl.pallas_call(
        body, name="conv_bwd_params", grid=(n_steps,),
        in_specs=[_row(tm, CONV_W), _row(tm, CONV_W), _row(tm, 2 * CONV_W), _halo_before(tm, 2 * CONV_W), vec, vec] + [ANY] * n,
        out_specs=[_row(tm, CONV_W), _const((32, CONV_W)), vec, vec, vec] + [ANY] * n,
        out_shape=[jax.ShapeDtypeStruct((r, CONV_W), F32), jax.ShapeDtypeStruct((32, CONV_W), F32)]
        + [jax.ShapeDtypeStruct((1, CONV_W), F32)] * 3 + _scatter_landing(parts),
        scratch_shapes=[pltpu.VMEM((8, tm + HALO, CONV_W), F32), pltpu.VMEM((32, 8, CONV_W), F32),
                        pltpu.SemaphoreType.DMA((7 * n,)), pltpu.SemaphoreType.DMA((7 * n,))],
        compiler_params=_cparams(),
    )(doc, y, cacg, cacg, lg, lb, *parts)


def _conv_bwd_data(dy, cacg, cw, tm):
    r = cacg.shape[0]
    n_steps = r // tm

    def body(dy_ref, dyn_ref, c_ref, w_ref, dc_ref, ub_ref):
        last = (jnp.zeros((HALO, CONV_W), jnp.int32) + pl.program_id(0)) == n_steps - 1
        win = jnp.concatenate([dy_ref[...], jnp.where(last, 0.0, dyn_ref[...])], axis=0)
        _shifted_copies(ub_ref, win)

        def chunk(ci, carry):
            r0 = pl.multiple_of(ci * CONV_CHUNK, CONV_CHUNK)
            du = _conv_chunk(ub_ref, w_ref, r0, BWD_SHIFTS)
            ca = c_ref[pl.ds(r0, CONV_CHUNK), 0:CONV_W]
            sg = _sigmoid(c_ref[pl.ds(r0, CONV_CHUNK), CONV_W:2 * CONV_W])
            dc_ref[pl.ds(r0, CONV_CHUNK), 0:CONV_W] = (du * sg).astype(BF16)
            dc_ref[pl.ds(r0, CONV_CHUNK), CONV_W:2 * CONV_W] = (du * ca * sg * (1.0 - sg)).astype(BF16)
            return carry

        lax.fori_loop(0, tm // CONV_CHUNK, chunk, 0)

    halo_after = pl.BlockSpec((HALO, CONV_W), lambda i: (jnp.minimum((i + 1) * (tm // HALO), r // HALO - 1), 0))
    return pl.pallas_call(
        body, name="conv_bwd_data", grid=(n_steps,),
        in_specs=[_row(tm, CONV_W), halo_after, _row(tm, 2 * CONV_W), _const((32, CONV_W))],
        out_specs=_row(tm, 2 * CONV_W),
        out_shape=jax.ShapeDtypeStruct((r, 2 * CONV_W), BF16),
        scratch_shapes=[pltpu.VMEM((8, tm + HALO, CONV_W), F32)],
        compiler_params=_cparams(),
    )(dy, dy, cacg, cw)


def _attn_bwd(q, kv, o, do, lse, sinks, parts):
    r = q.shape[0]
    nb = r // BLOCK
    n = len(parts)

    def body(sink_ref, q_ref, kvc_ref, kvp_ref, kvm_ref, o_ref, do_ref, lse_ref, *rest):
        src = rest[:n]
        dq_ref, dkv_ref, dmeta_ref, dsink_ref = rest[n:n + 4]
        dst = rest[n + 4:2 * n + 4]
        hold_ref, bias_ref, late_ref, send_sems, recv_sems = rest[2 * n + 4:]
        i = pl.program_id(0)

        @pl.when(i == 0)
        def _():
            for cp in _scatter(src, dst, send_sems, recv_sems):
                cp.start()
            _attn_bias_init(bias_ref, late_ref)
            dmeta_ref[...] = jnp.zeros_like(dmeta_ref)
            dsink_ref[...] = jnp.zeros_like(dsink_ref)
            hold_ref[...] = jnp.zeros_like(hold_ref)

        @pl.when(i < nb)
        def _():
            lane = lax.broadcasted_iota(jnp.int32, (BLOCK, BLOCK), 1)
            lse_tile = lse_ref[...]
            zero = jnp.zeros((BLOCK, BLOCK), F32)
            for g in range(N_KV):
                kcat, vcat = _kv_cat(kvm_ref, kvp_ref, kvc_ref, g)
                heads = range(g * GROUP, (g + 1) * GROUP)
                qs = _stack_heads(q_ref, g)
                dos = _stack_heads(do_ref, g)
                dosb = dos.astype(BF16)
                lse = jnp.concatenate(
                    [jnp.sum(jnp.where(lane == h, lse_tile, 0.0), axis=-1, keepdims=True) + zero for h in heads], axis=0)
                delta = jnp.sum(dos * _stack_heads(o_ref, g), axis=-1, keepdims=True) + jnp.zeros((GROUP * BLOCK, BLOCK), F32)
                s = _dot_nt(qs, kcat) * SCALE + _attn_bias(bias_ref, late_ref, g, i)
                dp = _dot_nt(dosb, vcat)
                ps = [jnp.exp(s[:, k * BLOCK:(k + 1) * BLOCK] - lse) for k in range(3)]
                p = jnp.concatenate(ps, axis=1)
                ds = jnp.concatenate(
                    [(ps[k] * (dp[:, k * BLOCK:(k + 1) * BLOCK] - delta)) * SCALE for k in range(3)], axis=1).astype(BF16)
                sink_term = jnp.exp(_head_rows([sink_ref[h] for h in heads]) - lse)[:, 0:1] * delta[:, 0:1]
                dq = _dot(ds, kcat).astype(BF16)
                for j, h in enumerate(heads):
                    dsink_ref[h:h + 1, :] += -jnp.sum(sink_term[j * BLOCK:(j + 1) * BLOCK])
                    dq_ref[:, h * HEAD_DIM:(h + 1) * HEAD_DIM] = dq[j * BLOCK:(j + 1) * BLOCK]
                dk_t = _dot_tn(qs, ds)
                dv_t = _dot_tn(dosb, p.astype(BF16))
                ks = slice(g * HEAD_DIM, (g + 1) * HEAD_DIM)
                vs = slice(KV_W + g * HEAD_DIM, KV_W + (g + 1) * HEAD_DIM)
                for sl, grad_t in ((ks, dk_t), (vs, dv_t)):
                    dmeta_ref[:, sl] += grad_t[:, 0:BLOCK].T
                    dkv_ref[:, sl] = hold_ref[:, sl] + grad_t[:, BLOCK:2 * BLOCK].T
                    hold_ref[:, sl] = grad_t[:, 2 * BLOCK:3 * BLOCK].T

        @pl.when(i == nb)
        def _():
            dkv_ref[...] = hold_ref[...]
            for cp in _scatter(src, dst, send_sems, recv_sems):
                cp.wait()

    def cur(i):
        return jnp.minimum(i, nb - 1)

    return pl.pallas_call(
        body, name="attn_bwd", grid=(nb + 1,),
        in_specs=[pl.BlockSpec(memory_space=pltpu.SMEM),
                  pl.BlockSpec((BLOCK, ATTN_W), lambda i: (cur(i), 0)),
                  pl.BlockSpec((BLOCK, 2 * KV_W), lambda i: (cur(i), 0)),
                  pl.BlockSpec((BLOCK, 2 * KV_W), lambda i: (jnp.maximum(cur(i) - 1, 0), 0)),
                  _const((BLOCK, 2 * KV_W)),
                  pl.BlockSpec((BLOCK, ATTN_W), lambda i: (cur(i), 0)),
                  pl.BlockSpec((BLOCK, ATTN_W), lambda i: (cur(i), 0)),
                  pl.BlockSpec((BLOCK, BLOCK), lambda i: (cur(i), 0))] + [ANY] * n,
        out_specs=[pl.BlockSpec((BLOCK, ATTN_W), lambda i: (cur(i), 0)),
                   pl.BlockSpec((BLOCK, 2 * KV_W), lambda i: (jnp.maximum(i - 1, 0), 0)),
                   _const((BLOCK, 2 * KV_W)), _const((N_HEADS, BLOCK))] + [ANY] * n,
        out_shape=[jax.ShapeDtypeStruct((r, ATTN_W), BF16), jax.ShapeDtypeStruct((r, 2 * KV_W), F32),
                   jax.ShapeDtypeStruct((BLOCK, 2 * KV_W), F32), jax.ShapeDtypeStruct((N_HEADS, BLOCK), F32)] + _scatter_landing(parts),
        scratch_shapes=[pltpu.VMEM((BLOCK, 2 * KV_W), F32), pltpu.VMEM((N_KV, GROUP * BLOCK, 3 * BLOCK), F32),
                        pltpu.VMEM((GROUP * BLOCK, BLOCK), F32), pltpu.SemaphoreType.DMA((7 * n,)), pltpu.SemaphoreType.DMA((7 * n,))],
        compiler_params=_cparams(),
    )(sinks, q, kv, kv, kv, o, do, lse, *parts)


def _in_proj_bwd(dq, dkv, dkv_meta, dc, dh1, h0, g1, w_in_t, tm):
    r = h0.shape[0]

    def body(dq_ref, dkv_ref, dm_ref, dc_ref, dh1_ref, h_ref, g_ref, w_ref, dh0_ref, dwt_ref, dg_ref, dw_ref):
        i = pl.program_id(0)

        @pl.when(i == 0)
        def _():
            dw_ref[...] = jnp.zeros_like(dw_ref)
            dg_ref[...] = jnp.zeros_like(dg_ref)

        meta = jnp.concatenate([dm_ref[...], jnp.zeros((tm - BLOCK, 2 * KV_W), F32)], axis=0) if tm > BLOCK else dm_ref[...]
        first = (jnp.zeros((tm, 2 * KV_W), jnp.int32) + i) == 0
        dkvb = (dkv_ref[...] + jnp.where(first, meta, 0.0)).astype(BF16)
        dqb = dq_ref[...]
        dcb = dc_ref[...]
        hn, xhat, rstd = _rms_fwd(h_ref[...], g_ref[...])
        hnb = hn.astype(BF16)
        dproj = jnp.concatenate([dqb, dkvb, dcb], axis=1)
        dhn = _dot(dproj, w_ref[...])
        dw_ref[...] += _dot_tn(dproj, hnb)
        dx, dg_rows = _rms_bwd(dhn, xhat, rstd, g_ref[...])
        dg_ref[...] += jnp.sum(dg_rows, axis=0, keepdims=True)
        dh0_ref[...] = dh1_ref[...] + dx

        @pl.when(i == r // tm - 1)
        def _():
            dwt_ref[...] = dw_ref[...].astype(BF16)

    return pl.pallas_call(
        body, name="in_proj_bwd", grid=(r // tm,),
        in_specs=[_row(tm, ATTN_W), _row(tm, 2 * KV_W), _const((BLOCK, 2 * KV_W)), _row(tm, 2 * CONV_W), _row(tm, D_MODEL),
                  _row(tm, D_MODEL), _const((1, D_MODEL)), _const((IN_COLS, D_MODEL))],
        out_specs=[_row(tm, D_MODEL), _const((IN_COLS, D_MODEL)), _const((1, D_MODEL))],
        out_shape=[jax.ShapeDtypeStruct((r, D_MODEL), F32), jax.ShapeDtypeStruct((IN_COLS, D_MODEL), BF16),
                   jax.ShapeDtypeStruct((1, D_MODEL), F32)],
        scratch_shapes=[pltpu.VMEM((IN_COLS, D_MODEL), F32)],
        compiler_params=_cparams(),
    )(dq, dkv, dkv_meta, dc, dh1, h0, g1, w_in_t)


def _adamw_update(w_ref, g_ref, m_ref, v_ref, d_ref, nm_ref, nv_ref):
    g = g_ref[...]
    m = ADAM_B1 * m_ref[...] + (1.0 - ADAM_B1) * g
    v = ADAM_B2 * v_ref[...] + (1.0 - ADAM_B2) * (g * g)
    m_hat = m / (1.0 - ADAM_B1 ** ADAM_STEP)
    v_hat = v / (1.0 - ADAM_B2 ** ADAM_STEP)
    d_ref[...] = -ADAM_LR * (m_hat / (jnp.sqrt(v_hat) + ADAM_EPS) + ADAM_WD * w_ref[...])
    nm_ref[...] = m
    nv_ref[...] = v


def _adamw(w, g, m, v, name):
    rows, cols = w.shape
    tr = rows
    for cand in (256, 176, 128, 64, 32, 16, 8):
        if rows % cand == 0:
            tr = cand
            break

    def body(*refs):
        _adamw_update(*refs)

    spec = _row(tr, cols)
    return pl.pallas_call(
        body, name=name, grid=(rows // tr,), in_specs=[spec] * 4, out_specs=[spec] * 3,
        out_shape=[jax.ShapeDtypeStruct((rows, cols), F32)] * 3, compiler_params=_cparams(),
    )(w, g, m, v)


MESH = pl.DeviceIdType.MESH
ANY = pl.BlockSpec(memory_space=pl.ANY)


def _place():
    x, y, c = lax.axis_index("x"), lax.axis_index("y"), lax.axis_index("c")
    chips = [(1 - x, y), (x, 1 - y), (1 - x, 1 - y)]
    return x, y, c, chips


def _gather_ici(src, dst, send_sems, recv_sems):
    x, y, c, chips = _place()
    sends, arrivals = [], []
    for k in range(len(src)):
        rows = src[k].shape[0] // 2
        half = pl.ds(c * rows, rows)
        for p, chip in enumerate(chips):
            sems = dict(send_sem=send_sems.at[3 * k + p], recv_sem=recv_sems.at[3 * k + p], device_id=(chip[0], chip[1], c),
                        device_id_type=MESH)
            sends.append(pltpu.make_async_remote_copy(src_ref=src[k].at[half], dst_ref=dst[k].at[2 * x + y, half], **sems))
            theirs = dst[k].at[2 * chip[0] + chip[1], half]
            arrivals.append(pltpu.make_async_remote_copy(src_ref=theirs, dst_ref=theirs, **sems))
    return sends, arrivals


def _gather_d2d(dst, send_sems, recv_sems):
    x, y, c, chips = _place()
    sends, arrivals = [], []
    for k in range(len(dst)):
        rows = dst[k].shape[1] // 2
        for p, chip in enumerate(chips):
            sems = dict(send_sem=send_sems.at[3 * k + p], recv_sem=recv_sems.at[3 * k + p], device_id=(x, y, 1 - c),
                        device_id_type=MESH)
            mine = dst[k].at[2 * chip[0] + chip[1], pl.ds(c * rows, rows)]
            sends.append(pltpu.make_async_remote_copy(src_ref=mine, dst_ref=mine, **sems))
            theirs = dst[k].at[2 * chip[0] + chip[1], pl.ds((1 - c) * rows, rows)]
            arrivals.append(pltpu.make_async_remote_copy(src_ref=theirs, dst_ref=theirs, **sems))
    return sends, arrivals


def _own_slots(shard):
    return jnp.broadcast_to(shard[None], (N_SHARD,) + shard.shape)


def _gather_weights(shards):
    n = len(shards)

    def body(*refs):
        src, dst = refs[:n], refs[2 * n:3 * n]
        ici_send, ici_recv, d2d_send, d2d_recv = refs[3 * n:]
        sends, arrivals = _gather_ici(src, dst, ici_send, ici_recv)
        for cp in sends:
            cp.start()
        for cp in arrivals:
            cp.wait_recv()
        forwards, from_sibling = _gather_d2d(dst, d2d_send, d2d_recv)
        for cp in forwards:
            cp.start()
        for cp in from_sibling:
            cp.wait_recv()
        for cp in sends + forwards:
            cp.wait_send()

    return pl.pallas_call(
        body, name="gather_weights",
        in_specs=[ANY] * (2 * n), out_specs=[ANY] * n,
        out_shape=[jax.ShapeDtypeStruct((N_SHARD,) + s.shape, s.dtype) for s in shards],
        input_output_aliases={n + k: k for k in range(n)},
        scratch_shapes=[pltpu.SemaphoreType.DMA((3 * n,))] * 4,
    )(*shards, *[_own_slots(s) for s in shards])


VMEM_WHOLE = pl.BlockSpec(memory_space=pltpu.VMEM)


def _allreduce_small(parts):
    widths = sorted({p.shape[1] for p in parts})
    place, heights = [], [0] * len(widths)
    for p in parts:
        gi = widths.index(p.shape[1])
        place.append((gi, heights[gi]))
        heights[gi] += -(-p.shape[0] // 8) * 8
    n, ng = len(parts), len(widths)

    def body(*refs):
        ins, outs, slots = refs[:n], refs[n:2 * n], refs[2 * n:2 * n + ng]
        send_sems, recv_sems = refs[2 * n + ng:]
        x, y, c = lax.axis_index("x"), lax.axis_index("y"), lax.axis_index("c")
        me = 4 * x + 2 * y + c
        for gi in range(ng):
            slots[gi][me] = jnp.zeros((heights[gi], widths[gi]), F32)
        for k, (gi, r0) in enumerate(place):
            slots[gi][me, r0:r0 + parts[k].shape[0], :] = ins[k][...]

        def copy(gi, j, arriving):
            peer = ((x + (j >> 2)) % 2, (y + ((j >> 1) & 1)) % 2, (c + (j & 1)) % 2)
            slot = 4 * peer[0] + 2 * peer[1] + peer[2] if arriving else me
            return pltpu.make_async_remote_copy(
                src_ref=slots[gi].at[me], dst_ref=slots[gi].at[slot], send_sem=send_sems.at[7 * gi + j - 1],
                recv_sem=recv_sems.at[7 * gi + j - 1], device_id=peer, device_id_type=MESH)

        pairs = [(gi, j) for gi in range(ng) for j in range(1, N_DEV)]
        for gi, j in pairs:
            copy(gi, j, False).start()
        for gi, j in pairs:
            copy(gi, j, True).wait_recv()
        totals = []
        for gi in range(ng):
            total = slots[gi][0]
            for d in range(1, N_DEV):
                total = total + slots[gi][d]
            totals.append(total)
        for k, (gi, r0) in enumerate(place):
            outs[k][...] = totals[gi][r0:r0 + parts[k].shape[0], :]
        for gi, j in pairs:
            copy(gi, j, False).wait_send()

    return pl.pallas_call(
        body, name="allreduce_small", in_specs=[VMEM_WHOLE] * n, out_specs=[VMEM_WHOLE] * n,
        out_shape=[jax.ShapeDtypeStruct(p.shape, F32) for p in parts],
        scratch_shapes=[pltpu.VMEM((N_DEV, heights[gi], widths[gi]), F32) for gi in range(ng)]
        + [pltpu.SemaphoreType.DMA((7 * ng,)), pltpu.SemaphoreType.DMA((7 * ng,))],
    )(*parts)


def _adamw_small(ws, gs, ms, vs):
    n = len(ws)

    def body(*refs):
        for k in range(n):
            w_ref, g_ref, m_ref, v_ref = (refs[j * n + k] for j in range(4))
            _adamw_update(w_ref, g_ref, m_ref, v_ref, *(refs[(4 + j) * n + k] for j in range(3)))

    shapes = [jax.ShapeDtypeStruct(w.shape, F32) for w in ws]
    out = pl.pallas_call(
        body, name="adamw_small", in_specs=[VMEM_WHOLE] * (4 * n), out_specs=[VMEM_WHOLE] * (3 * n), out_shape=shapes * 3,
    )(*ws, *gs, *ms, *vs)
    return out[:n], out[n:2 * n], out[2 * n:]


def _scatter(src, dst, send_sems, recv_sems):
    x, y, c = lax.axis_index("x"), lax.axis_index("y"), lax.axis_index("c")
    copies = []
    for k in range(len(src)):
        rows = src[k].shape[1] // 2
        for j in range(1, N_DEV):
            px, py, pc = (x + (j >> 2)) % 2, (y + ((j >> 1) & 1)) % 2, (c + (j & 1)) % 2
            copies.append(pltpu.make_async_remote_copy(
                src_ref=src[k].at[2 * px + py, pl.ds(pc * rows, rows)], dst_ref=dst[k].at[j - 1],
                send_sem=send_sems.at[7 * k + j - 1], recv_sem=recv_sems.at[7 * k + j - 1], device_id=(px, py, pc),
                device_id_type=MESH))
    return copies


def _scatter_landing(parts):
    return [jax.ShapeDtypeStruct((N_DEV - 1, p.shape[1] // 2, p.shape[2]), p.dtype) for p in parts]


def _scatter_call(parts):
    n = len(parts)

    def body(*refs):
        copies = _scatter(refs[:n], refs[n:2 * n], *refs[2 * n:])
        for cp in copies:
            cp.start()
        for cp in copies:
            cp.wait()

    return pl.pallas_call(
        body, name="scatter_grads", in_specs=[ANY] * n, out_specs=[ANY] * n, out_shape=_scatter_landing(parts),
        scratch_shapes=[pltpu.SemaphoreType.DMA((7 * n,)), pltpu.SemaphoreType.DMA((7 * n,))],
    )(*parts)


def _sum_pieces(own, landed):
    n = len(own)

    def body(*refs):
        for k in range(n):
            got = refs[n + k]
            total = refs[k][...].astype(F32)
            for j in range(N_DEV - 1):
                total = total + got[j].astype(F32)
            refs[2 * n + k][...] = total

    in_specs, out_specs = [], []
    for o in own:
        in_specs.append(_row(o.shape[0] // 2, o.shape[1]))
    for o in own:
        in_specs.append(pl.BlockSpec((N_DEV - 1, o.shape[0] // 2, o.shape[1]), lambda i: (0, i, 0)))
        out_specs.append(_row(o.shape[0] // 2, o.shape[1]))
    return pl.pallas_call(
        body, name="sum_pieces", grid=(2,), in_specs=in_specs, out_specs=out_specs,
        out_shape=[jax.ShapeDtypeStruct(o.shape, F32) for o in own], compiler_params=_cparams(),
    )(*own, *landed)


def _swap_with_sibling(halves):
    n = len(halves)

    def body(*refs):
        x, y, c = lax.axis_index("x"), lax.axis_index("y"), lax.axis_index("c")
        copies = [pltpu.make_async_remote_copy(
            src_ref=refs[k], dst_ref=refs[n + k], send_sem=refs[2 * n].at[k], recv_sem=refs[2 * n + 1].at[k],
            device_id=(x, y, 1 - c), device_id_type=MESH) for k in range(n)]
        for cp in copies:
            cp.start()
        for cp in copies:
            cp.wait()

    return pl.pallas_call(
        body, name="swap_with_sibling", in_specs=[ANY] * n, out_specs=[ANY] * n,
        out_shape=[jax.ShapeDtypeStruct(h.shape, h.dtype) for h in halves],
        scratch_shapes=[pltpu.SemaphoreType.DMA((n,)), pltpu.SemaphoreType.DMA((n,))],
    )(*halves)


def _own_piece(part):
    rows = part.shape[1] // 2
    s = 2 * lax.axis_index("x") + lax.axis_index("y")
    return lax.dynamic_slice(part, (s, lax.axis_index("c") * rows, 0), (1, rows, part.shape[2]))[0]


def _both_halves(mine, theirs):
    south = lax.axis_index("c") == 0
    return jnp.concatenate([jnp.where(south, mine, theirs), jnp.where(south, theirs, mine)], axis=0)


def _from_col_shards(g):
    return g.transpose(1, 0, 2).reshape(g.shape[1], -1)


def kernel(x, meta_tokens, attn_norm_g, w_in, attn_sinks, conv_w, conv_b, conv_ln_g, conv_ln_b, attn_out_g, conv_out_g, w_out, ffn_norm_g, w_gate, w_up, w_down, final_norm_g, loss_target, m_meta_tokens, m_attn_norm_g, m_w_in, m_attn_sinks, m_conv_w, m_conv_b, m_conv_ln_g, m_conv_ln_b, m_attn_out_g, m_conv_out_g, m_w_out, m_ffn_norm_g, m_w_gate, m_w_up, m_w_down, m_final_norm_g, v_meta_tokens, v_attn_norm_g, v_w_in, v_attn_sinks, v_conv_w, v_conv_b, v_conv_ln_g, v_conv_ln_b, v_attn_out_g, v_conv_out_g, v_w_out, v_ffn_norm_g, v_w_gate, v_w_up, v_w_down, v_final_norm_g):
    seq = x.shape[1]
    r = -(-(seq + BLOCK) // ROW_QUANTUM) * ROW_QUANTUM
    tail = r - BLOCK - seq
    shard = 2 * lax.axis_index("x") + lax.axis_index("y")

    conv_w32 = jnp.pad(conv_w[0], ((0, 1), (0, 0)))
    small_shard = jnp.concatenate([meta_tokens, conv_w32.reshape(16, 256)], axis=0)
    g_in, g_small = _gather_weights([w_in[0].T.astype(BF16), small_shard])
    later = [w_gate[0].T.astype(BF16), w_up[0].T.astype(BF16), w_out[0].astype(BF16), w_down[0].astype(BF16)]
    w_in_t = g_in.reshape(IN_COLS, D_MODEL)
    meta_full = _from_col_shards(g_small[:, 0:N_META])
    cw_full = _from_col_shards(g_small[:, N_META:].reshape(N_SHARD, 32, 128))

    g1, ga, gc, g2 = attn_norm_g, attn_out_g, conv_out_g, ffn_norm_g
    gf = final_norm_g.reshape(1, D_MODEL)
    sinks = attn_sinks[0]

    h0 = jnp.concatenate([jnp.zeros((LEAD, D_MODEL), F32), meta_full, x[0], jnp.zeros((tail, D_MODEL), F32)], axis=0)
    q, kv, cacg = _in_proj(h0, g1, w_in_t, 768)
    oa, lse, *gathered = _attn_fwd(q, kv, sinks, later, [_own_slots(s) for s in later])
    oc, yc, g_gate, g_up, g_out, g_down = _conv_fwd(cacg, cw_full, conv_b, conv_ln_g, conv_ln_b, 384, gathered)
    wg_t, wu_t, wd_b = g_gate.reshape(D_FF, D_MODEL), g_up.reshape(D_FF, D_MODEL), g_down.reshape(D_FF, D_MODEL)
    w_out_b = g_out.reshape(D_MODEL, D_MODEL)
    h1, hn2 = _out_proj(oa, oc, h0, ga, gc, g2, w_out_b, 768)
    gate, up, act, dh2, dh2b, loss_p, dgf = _ffn_fwd(hn2, h1, loss_target[0], gf, wg_t, wu_t, wd_b, 384)

    def by_shard(dw):
        return dw.reshape(N_SHARD, dw.shape[0] // N_SHARD, D_MODEL)

    dgate, dup, dh1, dg2 = _ffn_bwd(dh2, dh2b, gate, up, h1, g2, wg_t, wu_t, wd_b, 384)
    p_gate, p_up = [by_shard(dw) for dw in _ffn_wgrad_gu(hn2, dgate, dup, 768)]
    p_down = by_shard(_ffn_wgrad_d(act, dh2b, 768))
    doa, doc, dwo, dga, dgc = _out_proj_bwd(dh1, oa, oc, ga, gc, w_out_b, 768)
    p_out = by_shard(dwo)
    dy, dcw, dcb, dlg, dlb, l_gate, l_up = _conv_bwd_params(doc, yc, cacg, conv_ln_g, conv_ln_b, 384, [p_gate, p_up])
    dc = _conv_bwd_data(dy, cacg, cw_full, 384)
    dq, dkv, dkv_meta, dsink, l_out, l_down = _attn_bwd(q, kv, oa, doa, lse, sinks, [p_out, p_down])
    dh0, dwi_t, dg1 = _in_proj_bwd(dq, dkv, dkv_meta, dc, dh1, h0, g1, w_in_t, 768)
    p_in = by_shard(dwi_t)
    grad_x = dh0[BLOCK:BLOCK + seq][None]
    l_in, = _scatter_call([p_in])

    red_names = ("final_norm_g", "attn_norm_g", "ffn_norm_g", "meta_tokens", "attn_out_g", "conv_out_g", "conv_b", "conv_ln_g",
                 "conv_ln_b", "conv_w", "loss", "attn_sinks")
    red = dict(zip(red_names, _allreduce_small([dgf, dg1, dg2, dh0[LEAD:BLOCK], dga, dgc, dcb, dlg, dlb, dcw, loss_p, dsink])))
    loss = red["loss"][0, 0]
    red["final_norm_g"] = red["final_norm_g"].reshape(D_MODEL)
    red["attn_sinks"] = red["attn_sinks"][:, 0].reshape(1, N_HEADS)
    g_meta = lax.dynamic_slice_in_dim(red["meta_tokens"], shard * (D_MODEL // N_SHARD), D_MODEL // N_SHARD, axis=1)
    g_convw = lax.dynamic_slice_in_dim(red["conv_w"][0:CONV_K], shard * (CONV_W // N_SHARD), CONV_W // N_SHARD, axis=1)[None]

    halves = _sum_pieces([_own_piece(p) for p in (p_in, p_gate, p_up, p_out, p_down)], [l_in, l_gate, l_up, l_out, l_down])
    big = ("w_in", "w_gate", "w_up", "w_out", "w_down")
    transposed = ("w_in", "w_gate", "w_up")
    g_big = {name: _both_halves(mine, theirs) for name, mine, theirs in zip(big, halves, _swap_with_sibling(halves))}

    grads = {
        "meta_tokens": g_meta, "attn_norm_g": red["attn_norm_g"], "attn_sinks": red["attn_sinks"],
        "conv_w": g_convw, "conv_b": red["conv_b"], "conv_ln_g": red["conv_ln_g"], "conv_ln_b": red["conv_ln_b"],
        "attn_out_g": red["attn_out_g"], "conv_out_g": red["conv_out_g"], "ffn_norm_g": red["ffn_norm_g"],
        "final_norm_g": red["final_norm_g"]}
    params = {
        "meta_tokens": (meta_tokens, m_meta_tokens, v_meta_tokens), "attn_norm_g": (attn_norm_g, m_attn_norm_g, v_attn_norm_g),
        "w_in": (w_in, m_w_in, v_w_in), "attn_sinks": (attn_sinks, m_attn_sinks, v_attn_sinks), "conv_w": (conv_w, m_conv_w, v_conv_w),
        "conv_b": (conv_b, m_conv_b, v_conv_b), "conv_ln_g": (conv_ln_g, m_conv_ln_g, v_conv_ln_g),
        "conv_ln_b": (conv_ln_b, m_conv_ln_b, v_conv_ln_b), "attn_out_g": (attn_out_g, m_attn_out_g, v_attn_out_g),
        "conv_out_g": (conv_out_g, m_conv_out_g, v_conv_out_g), "w_out": (w_out, m_w_out, v_w_out),
        "ffn_norm_g": (ffn_norm_g, m_ffn_norm_g, v_ffn_norm_g), "w_gate": (w_gate, m_w_gate, v_w_gate), "w_up": (w_up, m_w_up, v_w_up),
        "w_down": (w_down, m_w_down, v_w_down), "final_norm_g": (final_norm_g, m_final_norm_g, v_final_norm_g)}
    names = list(params)
    delta, new_m, new_v = {}, {}, {}
    for name in big:
        flip = (lambda a: a.T) if name in transposed else (lambda a: a)
        w, m, v = params[name]
        outs = _adamw(flip(w[0]), g_big[name], flip(m[0]), flip(v[0]), "adamw_" + name)
        grads[name], delta[name], new_m[name], new_v[name] = [flip(a)[None] for a in (g_big[name], *outs)]
    rest = [name for name in names if name not in big]

    def rows_of(a):
        return a.reshape(-1, a.shape[-1])

    small = _adamw_small([rows_of(params[n][0]) for n in rest], [rows_of(grads[n]) for n in rest],
                         [rows_of(params[n][1]) for n in rest], [rows_of(params[n][2]) for n in rest])
    for dst, outs in zip((delta, new_m, new_v), small):
        for name, out in zip(rest, outs):
            dst[name] = out.reshape(params[name][0].shape)

    return (loss, grad_x, *[grads[n] for n in names], *[delta[n] for n in names], *[new_m[n] for n in names],
            *[new_v[n] for n in names])
```

```python
import functools
import math

import jax
import jax.numpy as jnp
from jax import lax
from jax.experimental import pallas as pl
from jax.experimental.pallas import tpu as pltpu

F32 = jnp.float32
BF16 = jnp.bfloat16

D_MODEL = 1024
N_META = 16
ATTN_W = 512
CONV_W = 512
HEAD_DIM = 64
N_HEADS = 8
N_KV = 2
GROUP = N_HEADS // N_KV
KV_W = N_KV * HEAD_DIM
BLOCK = 128
LEAD = BLOCK - N_META
CONV_K = 31
D_FF = 2816
IN_COLS = ATTN_W + 2 * KV_W + 2 * CONV_W
Q0, KV0, C0 = 0, ATTN_W, ATTN_W + 2 * KV_W
NORM_EPS = 1e-5
SCALE = 1.0 / math.sqrt(HEAD_DIM)
SLOPES = tuple(2.0 ** (-(8.0 / N_HEADS) * (h + 1)) for h in range(N_HEADS))
NEG = -1e30

ADAM_LR, ADAM_B1, ADAM_B2, ADAM_EPS, ADAM_WD, ADAM_STEP = 0.001, 0.9, 0.999, 1e-08, 0.01, 10

N_SHARD = 4
N_DEV = 8
ROW_QUANTUM = 768
HALO = 32
CONV_CHUNK = 32
FF_CHUNK = 256
VMEM_LIMIT = 60 * 1024 * 1024


def _cparams(n_axes=1):
    return pltpu.CompilerParams(dimension_semantics=("arbitrary",) * n_axes, vmem_limit_bytes=VMEM_LIMIT)


def _dot(a, b):
    return jnp.dot(a, b, preferred_element_type=F32)


def _dot_nt(a, b):
    return lax.dot_general(a, b, (((1,), (1,)), ((), ())), preferred_element_type=F32)


def _dot_tn(a, b):
    return lax.dot_general(a, b, (((0,), (0,)), ((), ())), preferred_element_type=F32)


def _sigmoid(x):
    return 1.0 / (1.0 + jnp.exp(-x))


def _row(tm, n):
    return pl.BlockSpec((tm, n), lambda i: (i, 0))


def _const(shape):
    return pl.BlockSpec(shape, lambda i: (0,) * len(shape))


def _resident(shape):
    return pl.BlockSpec(shape, lambda i: (0,) * len(shape), pipeline_mode=pl.Buffered(1))


def _rms_fwd(x, g):
    rstd = lax.rsqrt(jnp.mean(x * x, axis=-1, keepdims=True) + NORM_EPS)
    xhat = x * rstd
    return xhat * g, xhat, rstd


def _rms_bwd(dy, xhat, rstd, g):
    dxh = dy * g
    dx = rstd * (dxh - xhat * jnp.mean(dxh * xhat, axis=-1, keepdims=True))
    return dx, dy * xhat


def _in_proj(h0, g1, w_in_t, tm):
    r = h0.shape[0]

    def body(h_ref, g_ref, w_ref, q_ref, kv_ref, c_ref):
        hn = _rms_fwd(h_ref[...], g_ref[...])[0].astype(BF16)
        q_ref[...] = _dot_nt(hn, w_ref[Q0:KV0, :]).astype(BF16)
        kv_ref[...] = _dot_nt(hn, w_ref[KV0:C0, :]).astype(BF16)
        c_ref[...] = _dot_nt(hn, w_ref[C0:IN_COLS, :])

    return pl.pallas_call(
        body, name="in_proj", grid=(r // tm,),
        in_specs=[_row(tm, D_MODEL), _const((1, D_MODEL)), _const((IN_COLS, D_MODEL))],
        out_specs=[_row(tm, ATTN_W), _row(tm, 2 * KV_W), _row(tm, 2 * CONV_W)],
        out_shape=[jax.ShapeDtypeStruct((r, ATTN_W), BF16), jax.ShapeDtypeStruct((r, 2 * KV_W), BF16),
                   jax.ShapeDtypeStruct((r, 2 * CONV_W), F32)],
        compiler_params=_cparams(),
    )(h0, g1, w_in_t)


def _attn_bias_init(bias_ref, late_ref):
    row = lax.broadcasted_iota(jnp.int32, (GROUP * BLOCK, BLOCK), 0) & (BLOCK - 1)
    col = lax.broadcasted_iota(jnp.int32, (GROUP * BLOCK, BLOCK), 1)
    late_ref[...] = jnp.where(col > row, 1.0, 0.0)
    for g in range(N_KV):
        slope = jnp.concatenate([jnp.zeros((BLOCK, BLOCK), F32) + SLOPES[g * GROUP + j] for j in range(GROUP)], axis=0)
        bias_ref[g, :, 0:BLOCK] = jnp.where(col >= LEAD, 0.0, NEG)
        bias_ref[g, :, BLOCK:2 * BLOCK] = -slope * jnp.where(col > row, row - col + BLOCK, row - col).astype(F32)


def _attn_block_bias(late_ref, i):
    late = late_ref[...]
    meta0 = jnp.where(i == 0, NEG, 0.0)
    no_prev = jnp.where(i >= 2, 0.0, NEG)
    no_cur = jnp.where(i >= 1, 0.0, NEG)
    return late * meta0, late * no_prev + no_cur


def _attn_logits(s3, bias_ref, block_bias, prev_part, g):
    meta = s3[:, 0:BLOCK] * SCALE + (bias_ref[g, :, 0:BLOCK] + block_bias[0])
    band = jnp.where(prev_part, s3[:, BLOCK:2 * BLOCK], s3[:, 2 * BLOCK:3 * BLOCK]) * SCALE + (bias_ref[g, :, BLOCK:2 * BLOCK] + block_bias[1])
    return meta, band


def _split_band(meta, band, prev_part):
    return jnp.concatenate([meta, jnp.where(prev_part, band, 0.0), jnp.where(prev_part, 0.0, band)], axis=1)


def _head_rows(vals):
    return jnp.concatenate([jnp.zeros((BLOCK, BLOCK), F32) + v for v in vals], axis=0)


def _stack_heads(ref, g):
    return jnp.concatenate([ref[:, (g * GROUP + j) * HEAD_DIM:(g * GROUP + j + 1) * HEAD_DIM] for j in range(GROUP)], axis=0)


def _kv_cat(kvm_ref, kvp_ref, kvc_ref, g):
    ks = slice(g * HEAD_DIM, (g + 1) * HEAD_DIM)
    vs = slice(KV_W + g * HEAD_DIM, KV_W + (g + 1) * HEAD_DIM)
    kcat = jnp.concatenate([kvm_ref[:, ks], kvp_ref[:, ks], kvc_ref[:, ks]], axis=0)
    vcat = jnp.concatenate([kvm_ref[:, vs], kvp_ref[:, vs], kvc_ref[:, vs]], axis=0)
    return kcat, vcat


def _attn_fwd(q, kv, sinks, shards, gathered):
    r = q.shape[0]
    nb = r // BLOCK
    n = len(shards)

    def body(sink_ref, q_ref, kvc_ref, kvp_ref, kvm_ref, *rest):
        src = rest[:n]
        o_ref, lse_ref = rest[2 * n:2 * n + 2]
        dst = rest[2 * n + 2:3 * n + 2]
        bias_ref, late_ref, send_sems, recv_sems = rest[3 * n + 2:]
        i = pl.program_id(0)

        @pl.when(i == 0)
        def _():
            for cp in _gather_ici(src, dst, send_sems, recv_sems)[0]:
                cp.start()
            _attn_bias_init(bias_ref, late_ref)

        lane = lax.broadcasted_iota(jnp.int32, (BLOCK, BLOCK), 1)
        lse_tile = jnp.zeros((BLOCK, BLOCK), F32)
        block_bias = _attn_block_bias(late_ref, i)
        prev_part = late_ref[...] > 0.5
        for g in range(N_KV):
            kcat, vcat = _kv_cat(kvm_ref, kvp_ref, kvc_ref, g)
            heads = range(g * GROUP, (g + 1) * GROUP)
            meta, band = _attn_logits(_dot_nt(_stack_heads(q_ref, g), kcat), bias_ref, block_bias, prev_part, g)
            sink = _head_rows([sink_ref[h] for h in heads])
            m = jnp.maximum(jnp.max(jnp.maximum(meta, band), axis=-1, keepdims=True), sink)
            p_meta, p_band = jnp.exp(meta - m), jnp.exp(band - m)
            l = jnp.sum(p_meta + p_band, axis=-1, keepdims=True) + jnp.exp(sink - m)
            o = _dot(_split_band(p_meta, p_band, prev_part).astype(BF16), vcat) * (1.0 / l)[:, 0:HEAD_DIM]
            lse = m + jnp.log(l)
            for j, h in enumerate(heads):
                o_ref[:, h * HEAD_DIM:(h + 1) * HEAD_DIM] = o[j * BLOCK:(j + 1) * BLOCK]
                lse_tile = jnp.where(lane == h, lse[j * BLOCK:(j + 1) * BLOCK], lse_tile)
        lse_ref[...] = lse_tile

        @pl.when(i == nb - 1)
        def _():
            sends, arrivals = _gather_ici(src, dst, send_sems, recv_sems)
            for cp in arrivals:
                cp.wait_recv()
            for cp in sends:
                cp.wait_send()

    return pl.pallas_call(
        body, name="attn_fwd", grid=(nb,),
        in_specs=[pl.BlockSpec(memory_space=pltpu.SMEM), _row(BLOCK, ATTN_W), _row(BLOCK, 2 * KV_W),
                  pl.BlockSpec((BLOCK, 2 * KV_W), lambda i: (jnp.maximum(i - 1, 0), 0)), _const((BLOCK, 2 * KV_W))] + [ANY] * (2 * n),
        out_specs=[_row(BLOCK, ATTN_W), _row(BLOCK, BLOCK)] + [ANY] * n,
        out_shape=[jax.ShapeDtypeStruct((r, ATTN_W), F32), jax.ShapeDtypeStruct((r, BLOCK), F32)]
        + [jax.ShapeDtypeStruct(g.shape, g.dtype) for g in gathered],
        input_output_aliases={5 + n + k: 2 + k for k in range(n)},
        scratch_shapes=[pltpu.VMEM((N_KV, GROUP * BLOCK, 2 * BLOCK), F32), pltpu.VMEM((GROUP * BLOCK, BLOCK), F32),
                        pltpu.SemaphoreType.DMA((3 * n,)), pltpu.SemaphoreType.DMA((3 * n,))],
        compiler_params=_cparams(),
    )(sinks, q, kv, kv, kv, *shards, *gathered)


def _shifted_copies(ub_ref, win):
    w = win.shape[0]
    ub_ref[0] = win
    for b in range(1, 8):
        ub_ref[b] = pltpu.roll(win, shift=w - b, axis=0)


def _conv_chunk(ub_ref, w_ref, r0, shifts):
    acc = jnp.zeros((CONV_CHUNK, CONV_W), F32)
    for j in range(CONV_K):
        a, b = divmod(shifts[j], 8)
        acc = acc + w_ref[j:j + 1, :] * ub_ref[b, pl.ds(r0 + 8 * a, CONV_CHUNK), :]
    return acc


FWD_SHIFTS = tuple(HALO - (CONV_K - 1) + j for j in range(CONV_K))
BWD_SHIFTS = tuple(CONV_K - 1 - j for j in range(CONV_K))


def _glu_window(cp_ref, c_ref, i):
    tile = c_ref[:, 0:CONV_W] * _sigmoid(c_ref[:, CONV_W:2 * CONV_W])
    halo = cp_ref[:, 0:CONV_W] * _sigmoid(cp_ref[:, CONV_W:2 * CONV_W])
    first = (jnp.zeros((HALO, CONV_W), jnp.int32) + i) == 0
    return jnp.concatenate([jnp.where(first, 0.0, halo), tile], axis=0)


def _halo_before(tm, n):
    return pl.BlockSpec((HALO, n), lambda i: (jnp.maximum(i * (tm // HALO) - 1, 0), 0))


def _conv_fwd(cacg, cw, cb, lg, lb, tm, gathered):
    r = cacg.shape[0]
    n = len(gathered)

    def body(c_ref, cp_ref, w_ref, cb_ref, lg_ref, lb_ref, *rest):
        o_ref, y_ref = rest[n:n + 2]
        dst = rest[n + 2:2 * n + 2]
        ub_ref, send_sems, recv_sems = rest[2 * n + 2:]
        i = pl.program_id(0)

        @pl.when(i == 0)
        def _():
            for cp in _gather_d2d(dst, send_sems, recv_sems)[0]:
                cp.start()

        _shifted_copies(ub_ref, _glu_window(cp_ref, c_ref, i))

        def chunk(ci, carry):
            r0 = pl.multiple_of(ci * CONV_CHUNK, CONV_CHUNK)
            y = _conv_chunk(ub_ref, w_ref, r0, FWD_SHIFTS) + cb_ref[...]
            yc = y - jnp.mean(y, axis=-1, keepdims=True)
            rs = lax.rsqrt(jnp.mean(yc * yc, axis=-1, keepdims=True) + NORM_EPS)
            yn = yc * rs * lg_ref[...] + lb_ref[...]
            o_ref[pl.ds(r0, CONV_CHUNK), :] = yn * _sigmoid(yn)
            y_ref[pl.ds(r0, CONV_CHUNK), :] = y
            return carry

        lax.fori_loop(0, tm // CONV_CHUNK, chunk, 0, unroll=2)

        @pl.when(i == r // tm - 1)
        def _():
            sends, arrivals = _gather_d2d(dst, send_sems, recv_sems)
            for cp in arrivals:
                cp.wait_recv()
            for cp in sends:
                cp.wait_send()

    return pl.pallas_call(
        body, name="conv_fwd", grid=(r // tm,),
        in_specs=[_row(tm, 2 * CONV_W), _halo_before(tm, 2 * CONV_W), _const((32, CONV_W)), _const((1, CONV_W)),
                  _const((1, CONV_W)), _const((1, CONV_W))] + [ANY] * n,
        out_specs=[_row(tm, CONV_W), _row(tm, CONV_W)] + [ANY] * n,
        out_shape=[jax.ShapeDtypeStruct((r, CONV_W), F32)] * 2 + [jax.ShapeDtypeStruct(g.shape, g.dtype) for g in gathered],
        input_output_aliases={6 + k: 2 + k for k in range(n)},
        scratch_shapes=[pltpu.VMEM((8, tm + HALO, CONV_W), F32), pltpu.SemaphoreType.DMA((3 * n,)), pltpu.SemaphoreType.DMA((3 * n,))],
        compiler_params=_cparams(),
    )(cacg, cacg, cw, cb, lg, lb, *gathered)


def _out_proj(oa, oc, h0, ga, gc, g2, w_out_b, tm):
    r = h0.shape[0]

    def body(oa_ref, oc_ref, h_ref, ga_ref, gc_ref, g2_ref, w_ref, h1_ref, hn2_ref):
        ma = _rms_fwd(oa_ref[...], ga_ref[...])[0].astype(BF16)
        mc = _rms_fwd(oc_ref[...], gc_ref[...])[0].astype(BF16)
        h1 = h_ref[...] + _dot(jnp.concatenate([ma, mc], axis=1), w_ref[...])
        h1_ref[...] = h1
        hn2_ref[...] = _rms_fwd(h1, g2_ref[...])[0].astype(BF16)

    return pl.pallas_call(
        body, name="out_proj", grid=(r // tm,),
        in_specs=[_row(tm, ATTN_W), _row(tm, CONV_W), _row(tm, D_MODEL), _const((1, ATTN_W)), _const((1, CONV_W)),
                  _const((1, D_MODEL)), _const((D_MODEL, D_MODEL))],
        out_specs=[_row(tm, D_MODEL), _row(tm, D_MODEL)],
        out_shape=[jax.ShapeDtypeStruct((r, D_MODEL), F32), jax.ShapeDtypeStruct((r, D_MODEL), BF16)],
        compiler_params=_cparams(),
    )(oa, oc, h0, ga, gc, g2, w_out_b)


def _ffn_fwd(hn2, h1, target, gf, wg_t, wu_t, wd_b, tm):
    r = h1.shape[0]
    seq = target.shape[0]
    n_sub = tm // BLOCK

    def body(hn_ref, h1_ref, *rest):
        t_refs = rest[:n_sub]
        gf_ref, wg_ref, wu_ref, wd_ref, gate_ref, up_ref, act_ref, dh2_ref, dh2b_ref, loss_ref, dgf_ref = rest[n_sub:]
        i = pl.program_id(0)

        @pl.when(i == 0)
        def _():
            loss_ref[...] = jnp.zeros_like(loss_ref)
            dgf_ref[...] = jnp.zeros_like(dgf_ref)

        hn = hn_ref[...]
        for ch in range(D_FF // FF_CHUNK):
            cs = slice(ch * FF_CHUNK, (ch + 1) * FF_CHUNK)
            gate = _dot_nt(hn, wg_ref[cs, :])
            up = _dot_nt(hn, wu_ref[cs, :])
            gate_ref[:, cs] = gate.astype(BF16)
            up_ref[:, cs] = up.astype(BF16)
            act_ref[:, cs] = (gate * _sigmoid(gate) * up).astype(BF16)
        y, xhat, rstd = _rms_fwd(h1_ref[...] + _dot(act_ref[...], wd_ref[...]), gf_ref[...])
        rows = lax.broadcasted_iota(jnp.int32, (tm, D_MODEL), 0) + i * tm
        real = (rows >= BLOCK) & (rows < BLOCK + seq)
        err = jnp.where(real, y - jnp.concatenate([t[...] for t in t_refs], axis=0), 0.0)
        loss_ref[...] += jnp.sum(err * err) * (0.5 / D_MODEL)
        dy = err * (1.0 / D_MODEL)
        dh2, dg_rows = _rms_bwd(dy, xhat, rstd, gf_ref[...])
        dgf_ref[...] += jnp.sum(dg_rows, axis=0, keepdims=True)
        dh2_ref[...] = dh2
        dh2b_ref[...] = dh2.astype(BF16)

    def target_block(k):
        return pl.BlockSpec((BLOCK, D_MODEL), lambda i: (jnp.clip(n_sub * i - 1 + k, 0, seq // BLOCK - 1), 0))

    return pl.pallas_call(
        body, name="ffn_fwd", grid=(r // tm,),
        in_specs=[_row(tm, D_MODEL), _row(tm, D_MODEL)] + [target_block(k) for k in range(n_sub)]
        + [_const((1, D_MODEL))] + [_resident((D_FF, D_MODEL))] * 3,
        out_specs=[_row(tm, D_FF)] * 3 + [_row(tm, D_MODEL), _row(tm, D_MODEL), _const((1, BLOCK)), _const((1, D_MODEL))],
        out_shape=[jax.ShapeDtypeStruct((r, D_FF), BF16)] * 3
        + [jax.ShapeDtypeStruct((r, D_MODEL), F32), jax.ShapeDtypeStruct((r, D_MODEL), BF16),
           jax.ShapeDtypeStruct((1, BLOCK), F32), jax.ShapeDtypeStruct((1, D_MODEL), F32)],
        compiler_params=_cparams(),
    )(hn2, h1, *[target] * n_sub, gf, wg_t, wu_t, wd_b)


def _ffn_bwd(dh2, dh2b, gate, up, h1, g2, wg_t, wu_t, wd_b, tm):
    r = h1.shape[0]

    def body(dh2_ref, dh2b_ref, gate_ref, up_ref, h1_ref, g2_ref, wg_ref, wu_ref, wd_ref, dgate_ref, dup_ref, dh1_ref, dg2_ref):
        @pl.when(pl.program_id(0) == 0)
        def _():
            dg2_ref[...] = jnp.zeros_like(dg2_ref)

        dyb = dh2b_ref[...]
        for ch in range(D_FF // FF_CHUNK):
            cs = slice(ch * FF_CHUNK, (ch + 1) * FF_CHUNK)
            dact = _dot_nt(dyb, wd_ref[cs, :])
            gate = gate_ref[:, cs].astype(F32)
            up = up_ref[:, cs].astype(F32)
            sg = _sigmoid(gate)
            dgate_ref[:, cs] = (dact * up * (sg * (1.0 + gate * (1.0 - sg)))).astype(BF16)
            dup_ref[:, cs] = (dact * (gate * sg)).astype(BF16)
        dhn = _dot(dgate_ref[...], wg_ref[...]) + _dot(dup_ref[...], wu_ref[...])
        _, xhat, rstd = _rms_fwd(h1_ref[...], g2_ref[...])
        dx, dg_rows = _rms_bwd(dhn, xhat, rstd, g2_ref[...])
        dg2_ref[...] += jnp.sum(dg_rows, axis=0, keepdims=True)
        dh1_ref[...] = dh2_ref[...] + dx

    return pl.pallas_call(
        body, name="ffn_bwd", grid=(r // tm,),
        in_specs=[_row(tm, D_MODEL), _row(tm, D_MODEL), _row(tm, D_FF), _row(tm, D_FF), _row(tm, D_MODEL), _const((1, D_MODEL))]
        + [_resident((D_FF, D_MODEL))] * 3,
        out_specs=[_row(tm, D_FF), _row(tm, D_FF), _row(tm, D_MODEL), _const((1, D_MODEL))],
        out_shape=[jax.ShapeDtypeStruct((r, D_FF), BF16), jax.ShapeDtypeStruct((r, D_FF), BF16),
                   jax.ShapeDtypeStruct((r, D_MODEL), F32), jax.ShapeDtypeStruct((1, D_MODEL), F32)],
        compiler_params=_cparams(),
    )(dh2, dh2b, gate, up, h1, g2, wg_t, wu_t, wd_b)


FF_HALF = D_FF // 2


def _ffn_wgrad_gu(hn2, dgate, dup, tk):
    r = hn2.shape[0]
    n_k = r // tk

    def body(hn_ref, dg_ref, du_ref, wg_ref, wu_ref, accg_ref, accu_ref):
        k = pl.program_id(1)

        @pl.when(k == 0)
        def _():
            accg_ref[...] = jnp.zeros_like(accg_ref)
            accu_ref[...] = jnp.zeros_like(accu_ref)

        hn = hn_ref[...]
        accg_ref[...] += _dot_tn(dg_ref[...], hn)
        accu_ref[...] += _dot_tn(du_ref[...], hn)

        @pl.when(k == n_k - 1)
        def _():
            wg_ref[...] = accg_ref[...].astype(BF16)
            wu_ref[...] = accu_ref[...].astype(BF16)

    col = pl.BlockSpec((tk, FF_HALF), lambda j, k: (k, j))
    out = pl.BlockSpec((FF_HALF, D_MODEL), lambda j, k: (j, 0))
    return pl.pallas_call(
        body, name="ffn_wgrad_gu", grid=(2, n_k),
        in_specs=[pl.BlockSpec((tk, D_MODEL), lambda j, k: (k, 0)), col, col],
        out_specs=[out, out],
        out_shape=[jax.ShapeDtypeStruct((D_FF, D_MODEL), BF16)] * 2,
        scratch_shapes=[pltpu.VMEM((FF_HALF, D_MODEL), F32)] * 2,
        compiler_params=_cparams(2),
    )(hn2, dgate, dup)


def _ffn_wgrad_d(act, dh2b, tk):
    r = act.shape[0]
    n_k = r // tk

    def body(a_ref, dy_ref, wd_ref, acc_ref):
        k = pl.program_id(1)

        @pl.when(k == 0)
        def _():
            acc_ref[...] = jnp.zeros_like(acc_ref)

        acc_ref[...] += _dot_tn(a_ref[...], dy_ref[...])

        @pl.when(k == n_k - 1)
        def _():
            wd_ref[...] = acc_ref[...].astype(BF16)

    return pl.pallas_call(
        body, name="ffn_wgrad_d", grid=(2, n_k),
        in_specs=[pl.BlockSpec((tk, FF_HALF), lambda j, k: (k, j)), pl.BlockSpec((tk, D_MODEL), lambda j, k: (k, 0))],
        out_specs=pl.BlockSpec((FF_HALF, D_MODEL), lambda j, k: (j, 0)),
        out_shape=jax.ShapeDtypeStruct((D_FF, D_MODEL), BF16),
        scratch_shapes=[pltpu.VMEM((FF_HALF, D_MODEL), F32)],
        compiler_params=_cparams(2),
    )(act, dh2b)


def _out_proj_bwd(dh1, oa, oc, ga, gc, w_out_b, tm):
    r = dh1.shape[0]

    def body(dh_ref, oa_ref, oc_ref, ga_ref, gc_ref, w_ref, doa_ref, doc_ref, dw_ref, dga_ref, dgc_ref, acc_ref):
        i = pl.program_id(0)

        @pl.when(i == 0)
        def _():
            acc_ref[...] = jnp.zeros_like(acc_ref)
            dga_ref[...] = jnp.zeros_like(dga_ref)
            dgc_ref[...] = jnp.zeros_like(dgc_ref)

        dhb = dh_ref[...].astype(BF16)
        dmix = _dot_nt(dhb, w_ref[...])
        ma, xa, ra = _rms_fwd(oa_ref[...], ga_ref[...])
        mc, xc, rc = _rms_fwd(oc_ref[...], gc_ref[...])
        acc_ref[...] += _dot_tn(jnp.concatenate([ma.astype(BF16), mc.astype(BF16)], axis=1), dhb)

        @pl.when(i == r // tm - 1)
        def _():
            dw_ref[...] = acc_ref[...].astype(BF16)

        doa, dga_rows = _rms_bwd(dmix[:, 0:ATTN_W], xa, ra, ga_ref[...])
        doc, dgc_rows = _rms_bwd(dmix[:, ATTN_W:ATTN_W + CONV_W], xc, rc, gc_ref[...])
        doa_ref[...] = doa
        doc_ref[...] = doc
        dga_ref[...] += jnp.sum(dga_rows, axis=0, keepdims=True)
        dgc_ref[...] += jnp.sum(dgc_rows, axis=0, keepdims=True)

    return pl.pallas_call(
        body, name="out_proj_bwd", grid=(r // tm,),
        in_specs=[_row(tm, D_MODEL), _row(tm, ATTN_W), _row(tm, CONV_W), _const((1, ATTN_W)), _const((1, CONV_W)),
                  _const((D_MODEL, D_MODEL))],
        out_specs=[_row(tm, ATTN_W), _row(tm, CONV_W), _const((D_MODEL, D_MODEL)), _const((1, ATTN_W)), _const((1, CONV_W))],
        out_shape=[jax.ShapeDtypeStruct((r, ATTN_W), F32), jax.ShapeDtypeStruct((r, CONV_W), F32),
                   jax.ShapeDtypeStruct((D_MODEL, D_MODEL), BF16), jax.ShapeDtypeStruct((1, ATTN_W), F32),
                   jax.ShapeDtypeStruct((1, CONV_W), F32)],
        scratch_shapes=[pltpu.VMEM((D_MODEL, D_MODEL), F32)],
        compiler_params=_cparams(),
    )(dh1, oa, oc, ga, gc, w_out_b)


def _conv_bwd_params(doc, y, cacg, lg, lb, tm, parts):
    r = cacg.shape[0]
    n_steps = r // tm
    n = len(parts)

    def body(do_ref, y_ref, c_ref, cp_ref, lg_ref, lb_ref, *rest):
        src = rest[:n]
        dy_ref, dcw_ref, dcb_ref, dlg_ref, dlb_ref = rest[n:n + 5]
        dst = rest[n + 5:2 * n + 5]
        ub_ref, accw_ref, send_sems, recv_sems = rest[2 * n + 5:]
        i = pl.program_id(0)

        @pl.when(i == 0)
        def _():
            for cp in _scatter(src, dst, send_sems, recv_sems):
                cp.start()
            accw_ref[...] = jnp.zeros_like(accw_ref)
            dcb_ref[...] = jnp.zeros_like(dcb_ref)
            dlg_ref[...] = jnp.zeros_like(dlg_ref)
            dlb_ref[...] = jnp.zeros_like(dlb_ref)

        _shifted_copies(ub_ref, _glu_window(cp_ref, c_ref, i))

        def chunk(ci, carry):
            r0 = pl.multiple_of(ci * CONV_CHUNK, CONV_CHUNK)
            y = y_ref[pl.ds(r0, CONV_CHUNK), :]
            yc = y - jnp.mean(y, axis=-1, keepdims=True)
            rs = lax.rsqrt(jnp.mean(yc * yc, axis=-1, keepdims=True) + NORM_EPS)
            xhat = yc * rs
            yn = xhat * lg_ref[...] + lb_ref[...]
            sg = _sigmoid(yn)
            dyn = do_ref[pl.ds(r0, CONV_CHUNK), :] * (sg * (1.0 + yn * (1.0 - sg)))
            dlg_ref[...] += jnp.sum(dyn * xhat, axis=0, keepdims=True)
            dlb_ref[...] += jnp.sum(dyn, axis=0, keepdims=True)
            dxh = dyn * lg_ref[...]
            dy = rs * (dxh - jnp.mean(dxh, axis=-1, keepdims=True) - xhat * jnp.mean(dxh * xhat, axis=-1, keepdims=True))
            dcb_ref[...] += jnp.sum(dy, axis=0, keepdims=True)
            dy_ref[pl.ds(r0, CONV_CHUNK), :] = dy
            for j in range(CONV_K):
                a, b = divmod(FWD_SHIFTS[j], 8)
                prod = dy * ub_ref[b, pl.ds(r0 + 8 * a, CONV_CHUNK), :]
                accw_ref[j] += jnp.sum(prod.reshape(CONV_CHUNK // 8, 8, CONV_W), axis=0)
            return carry

        lax.fori_loop(0, tm // CONV_CHUNK, chunk, 0, unroll=2)

        @pl.when(i == n_steps - 1)
        def _():
            for j in range(32):
                dcw_ref[j:j + 1, :] = jnp.sum(accw_ref[j], axis=0, keepdims=True)
            for cp in _scatter(src, dst, send_sems, recv_sems):
                cp.wait()

    vec = _const((1, CONV_W))
    return pl.pallas_call(
        body, name="conv_bwd_params", grid=(n_steps,),
        in_specs=[_row(tm, CONV_W), _row(tm, CONV_W), _row(tm, 2 * CONV_W), _halo_before(tm, 2 * CONV_W), vec, vec] + [ANY] * n,
        out_specs=[_row(tm, CONV_W), _const((32, CONV_W)), vec, vec, vec] + [ANY] * n,
        out_shape=[jax.ShapeDtypeStruct((r, CONV_W), F32), jax.ShapeDtypeStruct((32, CONV_W), F32)]
        + [jax.ShapeDtypeStruct((1, CONV_W), F32)] * 3 + _scatter_landing(parts),
        scratch_shapes=[pltpu.VMEM((8, tm + HALO, CONV_W), F32), pltpu.VMEM((32, 8, CONV_W), F32),
                        pltpu.SemaphoreType.DMA((7 * n,)), pltpu.SemaphoreType.DMA((7 * n,))],
        compiler_params=_cparams(),
    )(doc, y, cacg, cacg, lg, lb, *parts)


def _conv_bwd_data(dy, cacg, cw, tm):
    r = cacg.shape[0]
    n_steps = r // tm

    def body(dy_ref, dyn_ref, c_ref, w_ref, dc_ref, ub_ref):
        last = (jnp.zeros((HALO, CONV_W), jnp.int32) + pl.program_id(0)) == n_steps - 1
        win = jnp.concatenate([dy_ref[...], jnp.where(last, 0.0, dyn_ref[...])], axis=0)
        _shifted_copies(ub_ref, win)

        def chunk(ci, carry):
            r0 = pl.multiple_of(ci * CONV_CHUNK, CONV_CHUNK)
            du = _conv_chunk(ub_ref, w_ref, r0, BWD_SHIFTS)
            ca = c_ref[pl.ds(r0, CONV_CHUNK), 0:CONV_W]
            sg = _sigmoid(c_ref[pl.ds(r0, CONV_CHUNK), CONV_W:2 * CONV_W])
            dc_ref[pl.ds(r0, CONV_CHUNK), 0:CONV_W] = (du * sg).astype(BF16)
            dc_ref[pl.ds(r0, CONV_CHUNK), CONV_W:2 * CONV_W] = (du * ca * sg * (1.0 - sg)).astype(BF16)
            return carry

        lax.fori_loop(0, tm // CONV_CHUNK, chunk, 0, unroll=2)

    halo_after = pl.BlockSpec((HALO, CONV_W), lambda i: (jnp.minimum((i + 1) * (tm // HALO), r // HALO - 1), 0))
    return pl.pallas_call(
        body, name="conv_bwd_data", grid=(n_steps,),
        in_specs=[_row(tm, CONV_W), halo_after, _row(tm, 2 * CONV_W), _const((32, CONV_W))],
        out_specs=_row(tm, 2 * CONV_W),
        out_shape=jax.ShapeDtypeStruct((r, 2 * CONV_W), BF16),
        scratch_shapes=[pltpu.VMEM((8, tm + HALO, CONV_W), F32)],
        compiler_params=_cparams(),
    )(dy, dy, cacg, cw)


def _attn_bwd(q, kv, o, do, lse, sinks, parts):
    r = q.shape[0]
    nb = r // BLOCK
    n = len(parts)

    def body(sink_ref, q_ref, kvc_ref, kvp_ref, kvm_ref, o_ref, do_ref, lse_ref, *rest):
        src = rest[:n]
        dq_ref, dkv_ref, dmeta_ref, dsink_ref = rest[n:n + 4]
        dst = rest[n + 4:2 * n + 4]
        hold_ref, bias_ref, late_ref, send_sems, recv_sems = rest[2 * n + 4:]
        i = pl.program_id(0)

        @pl.when(i == 0)
        def _():
            for cp in _scatter(src, dst, send_sems, recv_sems):
                cp.start()
            _attn_bias_init(bias_ref, late_ref)
            dmeta_ref[...] = jnp.zeros_like(dmeta_ref)
            dsink_ref[...] = jnp.zeros_like(dsink_ref)
            hold_ref[...] = jnp.zeros_like(hold_ref)

        @pl.when(i < nb)
        def _():
            lane = lax.broadcasted_iota(jnp.int32, (BLOCK, BLOCK), 1)
            lse_tile = lse_ref[...]
            zero = jnp.zeros((BLOCK, BLOCK), F32)
            block_bias = _attn_block_bias(late_ref, i)
            prev_part = late_ref[...] > 0.5
            for g in range(N_KV):
                kcat, vcat = _kv_cat(kvm_ref, kvp_ref, kvc_ref, g)
                heads = range(g * GROUP, (g + 1) * GROUP)
                qs = _stack_heads(q_ref, g)
                dos = _stack_heads(do_ref, g)
                dosb = dos.astype(BF16)
                lse = jnp.concatenate(
                    [jnp.sum(jnp.where(lane == h, lse_tile, 0.0), axis=-1, keepdims=True) + zero for h in heads], axis=0)
                delta = jnp.sum(dos * _stack_heads(o_ref, g), axis=-1, keepdims=True) + jnp.zeros((GROUP * BLOCK, BLOCK), F32)
                meta, band = _attn_logits(_dot_nt(qs, kcat), bias_ref, block_bias, prev_part, g)
                p_meta, p_band = jnp.exp(meta - lse), jnp.exp(band - lse)
                dp = _dot_nt(dosb, vcat)
                dp_band = jnp.where(prev_part, dp[:, BLOCK:2 * BLOCK], dp[:, 2 * BLOCK:3 * BLOCK])
                p = _split_band(p_meta, p_band, prev_part)
                ds = _split_band(p_meta * (dp[:, 0:BLOCK] - delta) * SCALE, p_band * (dp_band - delta) * SCALE, prev_part).astype(BF16)
                sink_term = jnp.exp(_head_rows([sink_ref[h] for h in heads]) - lse)[:, 0:1] * delta[:, 0:1]
                dq = _dot(ds, kcat).astype(BF16)
                for j, h in enumerate(heads):
                    dsink_ref[h:h + 1, :] += -jnp.sum(sink_term[j * BLOCK:(j + 1) * BLOCK])
                    dq_ref[:, h * HEAD_DIM:(h + 1) * HEAD_DIM] = dq[j * BLOCK:(j + 1) * BLOCK]
                dk_t = _dot_tn(qs, ds)
                dv_t = _dot_tn(dosb, p.astype(BF16))
                ks = slice(g * HEAD_DIM, (g + 1) * HEAD_DIM)
                vs = slice(KV_W + g * HEAD_DIM, KV_W + (g + 1) * HEAD_DIM)
                for sl, grad_t in ((ks, dk_t), (vs, dv_t)):
                    dmeta_ref[:, sl] += grad_t[:, 0:BLOCK].T
                    dkv_ref[:, sl] = hold_ref[:, sl] + grad_t[:, BLOCK:2 * BLOCK].T
                    hold_ref[:, sl] = grad_t[:, 2 * BLOCK:3 * BLOCK].T

        @pl.when(i == nb)
        def _():
            dkv_ref[...] = hold_ref[...]
            for cp in _scatter(src, dst, send_sems, recv_sems):
                cp.wait()

    def cur(i):
        return jnp.minimum(i, nb - 1)

    return pl.pallas_call(
        body, name="attn_bwd", grid=(nb + 1,),
        in_specs=[pl.BlockSpec(memory_space=pltpu.SMEM),
                  pl.BlockSpec((BLOCK, ATTN_W), lambda i: (cur(i), 0)),
                  pl.BlockSpec((BLOCK, 2 * KV_W), lambda i: (cur(i), 0)),
                  pl.BlockSpec((BLOCK, 2 * KV_W), lambda i: (jnp.maximum(cur(i) - 1, 0), 0)),
                  _const((BLOCK, 2 * KV_W)),
                  pl.BlockSpec((BLOCK, ATTN_W), lambda i: (cur(i), 0)),
                  pl.BlockSpec((BLOCK, ATTN_W), lambda i: (cur(i), 0)),
                  pl.BlockSpec((BLOCK, BLOCK), lambda i: (cur(i), 0))] + [ANY] * n,
        out_specs=[pl.BlockSpec((BLOCK, ATTN_W), lambda i: (cur(i), 0)),
                   pl.BlockSpec((BLOCK, 2 * KV_W), lambda i: (jnp.maximum(i - 1, 0), 0)),
                   _const((BLOCK, 2 * KV_W)), _const((N_HEADS, BLOCK))] + [ANY] * n,
        out_shape=[jax.ShapeDtypeStruct((r, ATTN_W), BF16), jax.ShapeDtypeStruct((r, 2 * KV_W), F32),
                   jax.ShapeDtypeStruct((BLOCK, 2 * KV_W), F32), jax.ShapeDtypeStruct((N_HEADS, BLOCK), F32)] + _scatter_landing(parts),
        scratch_shapes=[pltpu.VMEM((BLOCK, 2 * KV_W), F32), pltpu.VMEM((N_KV, GROUP * BLOCK, 2 * BLOCK), F32),
                        pltpu.VMEM((GROUP * BLOCK, BLOCK), F32), pltpu.SemaphoreType.DMA((7 * n,)), pltpu.SemaphoreType.DMA((7 * n,))],
        compiler_params=_cparams(),
    )(sinks, q, kv, kv, kv, o, do, lse, *parts)


def _in_proj_bwd(dq, dkv, dkv_meta, dc, dh1, h0, g1, w_in_t, tm):
    r = h0.shape[0]

    def body(dq_ref, dkv_ref, dm_ref, dc_ref, dh1_ref, h_ref, g_ref, w_ref, dh0_ref, dwt_ref, dg_ref, dw_ref):
        i = pl.program_id(0)

        @pl.when(i == 0)
        def _():
            dw_ref[...] = jnp.zeros_like(dw_ref)
            dg_ref[...] = jnp.zeros_like(dg_ref)

        meta = jnp.concatenate([dm_ref[...], jnp.zeros((tm - BLOCK, 2 * KV_W), F32)], axis=0) if tm > BLOCK else dm_ref[...]
        first = (jnp.zeros((tm, 2 * KV_W), jnp.int32) + i) == 0
        dkvb = (dkv_ref[...] + jnp.where(first, meta, 0.0)).astype(BF16)
        dqb = dq_ref[...]
        dcb = dc_ref[...]
        hn, xhat, rstd = _rms_fwd(h_ref[...], g_ref[...])
        hnb = hn.astype(BF16)
        dproj = jnp.concatenate([dqb, dkvb, dcb], axis=1)
        dhn = _dot(dproj, w_ref[...])
        dw_ref[...] += _dot_tn(dproj, hnb)
        dx, dg_rows = _rms_bwd(dhn, xhat, rstd, g_ref[...])
        dg_ref[...] += jnp.sum(dg_rows, axis=0, keepdims=True)
        dh0_ref[...] = dh1_ref[...] + dx

        @pl.when(i == r // tm - 1)
        def _():
            dwt_ref[...] = dw_ref[...].astype(BF16)

    return pl.pallas_call(
        body, name="in_proj_bwd", grid=(r // tm,),
        in_specs=[_row(tm, ATTN_W), _row(tm, 2 * KV_W), _const((BLOCK, 2 * KV_W)), _row(tm, 2 * CONV_W), _row(tm, D_MODEL),
                  _row(tm, D_MODEL), _const((1, D_MODEL)), _const((IN_COLS, D_MODEL))],
        out_specs=[_row(tm, D_MODEL), _const((IN_COLS, D_MODEL)), _const((1, D_MODEL))],
        out_shape=[jax.ShapeDtypeStruct((r, D_MODEL), F32), jax.ShapeDtypeStruct((IN_COLS, D_MODEL), BF16),
                   jax.ShapeDtypeStruct((1, D_MODEL), F32)],
        scratch_shapes=[pltpu.VMEM((IN_COLS, D_MODEL), F32)],
        compiler_params=_cparams(),
    )(dq, dkv, dkv_meta, dc, dh1, h0, g1, w_in_t)


def _adamw_update(w_ref, g_ref, m_ref, v_ref, d_ref, nm_ref, nv_ref):
    g = g_ref[...]
    m = ADAM_B1 * m_ref[...] + (1.0 - ADAM_B1) * g
    v = ADAM_B2 * v_ref[...] + (1.0 - ADAM_B2) * (g * g)
    m_hat = m / (1.0 - ADAM_B1 ** ADAM_STEP)
    v_hat = v / (1.0 - ADAM_B2 ** ADAM_STEP)
    d_ref[...] = -ADAM_LR * (m_hat / (jnp.sqrt(v_hat) + ADAM_EPS) + ADAM_WD * w_ref[...])
    nm_ref[...] = m
    nv_ref[...] = v


def _adamw(w, g, m, v, name):
    rows, cols = w.shape
    tr = rows
    for cand in (256, 176, 128, 64, 32, 16, 8):
        if rows % cand == 0:
            tr = cand
            break

    def body(*refs):
        _adamw_update(*refs)

    spec = _row(tr, cols)
    return pl.pallas_call(
        body, name=name, grid=(rows // tr,), in_specs=[spec] * 4, out_specs=[spec] * 3,
        out_shape=[jax.ShapeDtypeStruct((rows, cols), F32)] * 3, compiler_params=_cparams(),
    )(w, g, m, v)


MESH = pl.DeviceIdType.MESH
ANY = pl.BlockSpec(memory_space=pl.ANY)


def _place():
    x, y, c = lax.axis_index("x"), lax.axis_index("y"), lax.axis_index("c")
    chips = [(1 - x, y), (x, 1 - y), (1 - x, 1 - y)]
    return x, y, c, chips


def _gather_ici(src, dst, send_sems, recv_sems):
    x, y, c, chips = _place()
    sends, arrivals = [], []
    for k in range(len(src)):
        rows = src[k].shape[0] // 2
        half = pl.ds(c * rows, rows)
        for p, chip in enumerate(chips):
            sems = dict(send_sem=send_sems.at[3 * k + p], recv_sem=recv_sems.at[3 * k + p], device_id=(chip[0], chip[1], c),
                        device_id_type=MESH)
            sends.append(pltpu.make_async_remote_copy(src_ref=src[k].at[half], dst_ref=dst[k].at[2 * x + y, half], **sems))
            theirs = dst[k].at[2 * chip[0] + chip[1], half]
            arrivals.append(pltpu.make_async_remote_copy(src_ref=theirs, dst_ref=theirs, **sems))
    return sends, arrivals


def _gather_d2d(dst, send_sems, recv_sems):
    x, y, c, chips = _place()
    sends, arrivals = [], []
    for k in range(len(dst)):
        rows = dst[k].shape[1] // 2
        for p, chip in enumerate(chips):
            sems = dict(send_sem=send_sems.at[3 * k + p], recv_sem=recv_sems.at[3 * k + p], device_id=(x, y, 1 - c),
                        device_id_type=MESH)
            mine = dst[k].at[2 * chip[0] + chip[1], pl.ds(c * rows, rows)]
            sends.append(pltpu.make_async_remote_copy(src_ref=mine, dst_ref=mine, **sems))
            theirs = dst[k].at[2 * chip[0] + chip[1], pl.ds((1 - c) * rows, rows)]
            arrivals.append(pltpu.make_async_remote_copy(src_ref=theirs, dst_ref=theirs, **sems))
    return sends, arrivals


def _own_slots(shard):
    return jnp.broadcast_to(shard[None], (N_SHARD,) + shard.shape)


def _gather_weights(shards):
    n = len(shards)

    def body(*refs):
        src, dst = refs[:n], refs[2 * n:3 * n]
        ici_send, ici_recv, d2d_send, d2d_recv = refs[3 * n:]
        sends, arrivals = _gather_ici(src, dst, ici_send, ici_recv)
        for cp in sends:
            cp.start()
        for cp in arrivals:
            cp.wait_recv()
        forwards, from_sibling = _gather_d2d(dst, d2d_send, d2d_recv)
        for cp in forwards:
            cp.start()
        for cp in from_sibling:
            cp.wait_recv()
        for cp in sends + forwards:
            cp.wait_send()

    return pl.pallas_call(
        body, name="gather_weights",
        in_specs=[ANY] * (2 * n), out_specs=[ANY] * n,
        out_shape=[jax.ShapeDtypeStruct((N_SHARD,) + s.shape, s.dtype) for s in shards],
        input_output_aliases={n + k: k for k in range(n)},
        scratch_shapes=[pltpu.SemaphoreType.DMA((3 * n,))] * 4,
    )(*shards, *[_own_slots(s) for s in shards])


VMEM_WHOLE = pl.BlockSpec(memory_space=pltpu.VMEM)


def _allreduce_small(parts, grads):
    widths = sorted({p.shape[1] for p in parts})
    place, heights = [], [0] * len(widths)
    for p in parts:
        gi = widths.index(p.shape[1])
        place.append((gi, heights[gi]))
        heights[gi] += -(-p.shape[0] // 8) * 8
    n, ng, nb = len(parts), len(widths), len(grads)

    def body(*refs):
        ins, big_src = refs[:n], refs[n:n + nb]
        outs, big_dst = refs[n + nb:2 * n + nb], refs[2 * n + nb:2 * (n + nb)]
        slots = refs[2 * (n + nb):2 * (n + nb) + ng]
        send_sems, recv_sems, big_send, big_recv = refs[2 * (n + nb) + ng:]
        scattered = _scatter(big_src, big_dst, big_send, big_recv)
        for cp in scattered:
            cp.start()
        x, y, c = lax.axis_index("x"), lax.axis_index("y"), lax.axis_index("c")
        me = 4 * x + 2 * y + c
        for gi in range(ng):
            slots[gi][me] = jnp.zeros((heights[gi], widths[gi]), F32)
        for k, (gi, r0) in enumerate(place):
            slots[gi][me, r0:r0 + parts[k].shape[0], :] = ins[k][...]

        def copy(gi, j, arriving):
            peer = ((x + (j >> 2)) % 2, (y + ((j >> 1) & 1)) % 2, (c + (j & 1)) % 2)
            slot = 4 * peer[0] + 2 * peer[1] + peer[2] if arriving else me
            return pltpu.make_async_remote_copy(
                src_ref=slots[gi].at[me], dst_ref=slots[gi].at[slot], send_sem=send_sems.at[7 * gi + j - 1],
                recv_sem=recv_sems.at[7 * gi + j - 1], device_id=peer, device_id_type=MESH)

        pairs = [(gi, j) for gi in range(ng) for j in range(1, N_DEV)]
        for gi, j in pairs:
            copy(gi, j, False).start()
        for gi, j in pairs:
            copy(gi, j, True).wait_recv()
        totals = []
        for gi in range(ng):
            total = slots[gi][0]
            for d in range(1, N_DEV):
                total = total + slots[gi][d]
            totals.append(total)
        for k, (gi, r0) in enumerate(place):
            outs[k][...] = totals[gi][r0:r0 + parts[k].shape[0], :]
        for gi, j in pairs:
            copy(gi, j, False).wait_send()
        for cp in scattered:
            cp.wait()

    out = pl.pallas_call(
        body, name="allreduce_small", in_specs=[VMEM_WHOLE] * n + [ANY] * nb, out_specs=[VMEM_WHOLE] * n + [ANY] * nb,
        out_shape=[jax.ShapeDtypeStruct(p.shape, F32) for p in parts] + _scatter_landing(grads),
        scratch_shapes=[pltpu.VMEM((N_DEV, heights[gi], widths[gi]), F32) for gi in range(ng)]
        + [pltpu.SemaphoreType.DMA((7 * ng,)), pltpu.SemaphoreType.DMA((7 * ng,)),
           pltpu.SemaphoreType.DMA((7 * nb,)), pltpu.SemaphoreType.DMA((7 * nb,))],
    )(*parts, *grads)
    return out[:n], out[n:]


def _adamw_small(ws, gs, ms, vs):
    n = len(ws)

    def body(*refs):
        for k in range(n):
            w_ref, g_ref, m_ref, v_ref = (refs[j * n + k] for j in range(4))
            _adamw_update(w_ref, g_ref, m_ref, v_ref, *(refs[(4 + j) * n + k] for j in range(3)))

    shapes = [jax.ShapeDtypeStruct(w.shape, F32) for w in ws]
    out = pl.pallas_call(
        body, name="adamw_small", in_specs=[VMEM_WHOLE] * (4 * n), out_specs=[VMEM_WHOLE] * (3 * n), out_shape=shapes * 3,
    )(*ws, *gs, *ms, *vs)
    return out[:n], out[n:2 * n], out[2 * n:]


def _scatter(src, dst, send_sems, recv_sems):
    x, y, c = lax.axis_index("x"), lax.axis_index("y"), lax.axis_index("c")
    copies = []
    for k in range(len(src)):
        rows = src[k].shape[1] // 2
        for j in range(1, N_DEV):
            px, py, pc = (x + (j >> 2)) % 2, (y + ((j >> 1) & 1)) % 2, (c + (j & 1)) % 2
            copies.append(pltpu.make_async_remote_copy(
                src_ref=src[k].at[2 * px + py, pl.ds(pc * rows, rows)], dst_ref=dst[k].at[j - 1],
                send_sem=send_sems.at[7 * k + j - 1], recv_sem=recv_sems.at[7 * k + j - 1], device_id=(px, py, pc),
                device_id_type=MESH))
    return copies


def _scatter_landing(parts):
    return [jax.ShapeDtypeStruct((N_DEV - 1, p.shape[1] // 2, p.shape[2]), p.dtype) for p in parts]


def _sum_pieces(own, landed):
    n = len(own)

    def body(*refs):
        for k in range(n):
            got = refs[n + k]
            total = refs[k][...].astype(F32)
            for j in range(N_DEV - 1):
                total = total + got[j].astype(F32)
            refs[2 * n + k][...] = total

    in_specs, out_specs = [], []
    for o in own:
        in_specs.append(_row(o.shape[0] // 2, o.shape[1]))
    for o in own:
        in_specs.append(pl.BlockSpec((N_DEV - 1, o.shape[0] // 2, o.shape[1]), lambda i: (0, i, 0)))
        out_specs.append(_row(o.shape[0] // 2, o.shape[1]))
    return pl.pallas_call(
        body, name="sum_pieces", grid=(2,), in_specs=in_specs, out_specs=out_specs,
        out_shape=[jax.ShapeDtypeStruct(o.shape, F32) for o in own], compiler_params=_cparams(),
    )(*own, *landed)


def _swap_with_sibling(halves):
    n = len(halves)

    def body(*refs):
        x, y, c = lax.axis_index("x"), lax.axis_index("y"), lax.axis_index("c")
        copies = [pltpu.make_async_remote_copy(
            src_ref=refs[k], dst_ref=refs[n + k], send_sem=refs[2 * n].at[k], recv_sem=refs[2 * n + 1].at[k],
            device_id=(x, y, 1 - c), device_id_type=MESH) for k in range(n)]
        for cp in copies:
            cp.start()
        for cp in copies:
            cp.wait()

    return pl.pallas_call(
        body, name="swap_with_sibling", in_specs=[ANY] * n, out_specs=[ANY] * n,
        out_shape=[jax.ShapeDtypeStruct(h.shape, h.dtype) for h in halves],
        scratch_shapes=[pltpu.SemaphoreType.DMA((n,)), pltpu.SemaphoreType.DMA((n,))],
    )(*halves)


def _own_piece(part):
    rows = part.shape[1] // 2
    s = 2 * lax.axis_index("x") + lax.axis_index("y")
    return lax.dynamic_slice(part, (s, lax.axis_index("c") * rows, 0), (1, rows, part.shape[2]))[0]


def _both_halves(mine, theirs):
    south = lax.axis_index("c") == 0
    return jnp.concatenate([jnp.where(south, mine, theirs), jnp.where(south, theirs, mine)], axis=0)


def _from_col_shards(g):
    return g.transpose(1, 0, 2).reshape(g.shape[1], -1)


def kernel(x, meta_tokens, attn_norm_g, w_in, attn_sinks, conv_w, conv_b, conv_ln_g, conv_ln_b, attn_out_g, conv_out_g, w_out, ffn_norm_g, w_gate, w_up, w_down, final_norm_g, loss_target, m_meta_tokens, m_attn_norm_g, m_w_in, m_attn_sinks, m_conv_w, m_conv_b, m_conv_ln_g, m_conv_ln_b, m_attn_out_g, m_conv_out_g, m_w_out, m_ffn_norm_g, m_w_gate, m_w_up, m_w_down, m_final_norm_g, v_meta_tokens, v_attn_norm_g, v_w_in, v_attn_sinks, v_conv_w, v_conv_b, v_conv_ln_g, v_conv_ln_b, v_attn_out_g, v_conv_out_g, v_w_out, v_ffn_norm_g, v_w_gate, v_w_up, v_w_down, v_final_norm_g):
    seq = x.shape[1]
    r = -(-(seq + BLOCK) // ROW_QUANTUM) * ROW_QUANTUM
    tail = r - BLOCK - seq
    shard = 2 * lax.axis_index("x") + lax.axis_index("y")

    conv_w32 = jnp.pad(conv_w[0], ((0, 1), (0, 0)))
    small_shard = jnp.concatenate([meta_tokens, conv_w32.reshape(16, 256)], axis=0)
    g_in, g_small = _gather_weights([w_in[0].T.astype(BF16), small_shard])
    later = [w_gate[0].T.astype(BF16), w_up[0].T.astype(BF16), w_out[0].astype(BF16), w_down[0].astype(BF16)]
    w_in_t = g_in.reshape(IN_COLS, D_MODEL)
    meta_full = _from_col_shards(g_small[:, 0:N_META])
    cw_full = _from_col_shards(g_small[:, N_META:].reshape(N_SHARD, 32, 128))

    g1, ga, gc, g2 = attn_norm_g, attn_out_g, conv_out_g, ffn_norm_g
    gf = final_norm_g.reshape(1, D_MODEL)
    sinks = attn_sinks[0]

    h0 = jnp.concatenate([jnp.zeros((LEAD, D_MODEL), F32), meta_full, x[0], jnp.zeros((tail, D_MODEL), F32)], axis=0)
    q, kv, cacg = _in_proj(h0, g1, w_in_t, 768)
    oa, lse, *gathered = _attn_fwd(q, kv, sinks, later, [_own_slots(s) for s in later])
    oc, yc, g_gate, g_up, g_out, g_down = _conv_fwd(cacg, cw_full, conv_b, conv_ln_g, conv_ln_b, 384, gathered)
    wg_t, wu_t, wd_b = g_gate.reshape(D_FF, D_MODEL), g_up.reshape(D_FF, D_MODEL), g_down.reshape(D_FF, D_MODEL)
    w_out_b = g_out.reshape(D_MODEL, D_MODEL)
    h1, hn2 = _out_proj(oa, oc, h0, ga, gc, g2, w_out_b, 768)
    gate, up, act, dh2, dh2b, loss_p, dgf = _ffn_fwd(hn2, h1, loss_target[0], gf, wg_t, wu_t, wd_b, 384)

    def by_shard(dw):
        return dw.reshape(N_SHARD, dw.shape[0] // N_SHARD, D_MODEL)

    dgate, dup, dh1, dg2 = _ffn_bwd(dh2, dh2b, gate, up, h1, g2, wg_t, wu_t, wd_b, 384)
    p_gate, p_up = [by_shard(dw) for dw in _ffn_wgrad_gu(hn2, dgate, dup, 768)]
    p_down = by_shard(_ffn_wgrad_d(act, dh2b, 768))
    doa, doc, dwo, dga, dgc = _out_proj_bwd(dh1, oa, oc, ga, gc, w_out_b, 768)
    p_out = by_shard(dwo)
    dy, dcw, dcb, dlg, dlb, l_gate, l_up = _conv_bwd_params(doc, yc, cacg, conv_ln_g, conv_ln_b, 384, [p_gate, p_up])
    dc = _conv_bwd_data(dy, cacg, cw_full, 384)
    dq, dkv, dkv_meta, dsink, l_out, l_down = _attn_bwd(q, kv, oa, doa, lse, sinks, [p_out, p_down])
    dh0, dwi_t, dg1 = _in_proj_bwd(dq, dkv, dkv_meta, dc, dh1, h0, g1, w_in_t, 768)
    p_in = by_shard(dwi_t)
    grad_x = dh0[BLOCK:BLOCK + seq][None]

    red_names = ("final_norm_g", "attn_norm_g", "ffn_norm_g", "meta_tokens", "attn_out_g", "conv_out_g", "conv_b", "conv_ln_g",
                 "conv_ln_b", "conv_w", "loss", "attn_sinks")
    sums, (l_in,) = _allreduce_small([dgf, dg1, dg2, dh0[LEAD:BLOCK], dga, dgc, dcb, dlg, dlb, dcw, loss_p, dsink], [p_in])
    red = dict(zip(red_names, sums))
    loss = red["loss"][0, 0]
    red["final_norm_g"] = red["final_norm_g"].reshape(D_MODEL)
    red["attn_sinks"] = red["attn_sinks"][:, 0].reshape(1, N_HEADS)
    g_meta = lax.dynamic_slice_in_dim(red["meta_tokens"], shard * (D_MODEL // N_SHARD), D_MODEL // N_SHARD, axis=1)
    g_convw = lax.dynamic_slice_in_dim(red["conv_w"][0:CONV_K], shard * (CONV_W // N_SHARD), CONV_W // N_SHARD, axis=1)[None]

    halves = _sum_pieces([_own_piece(p) for p in (p_in, p_gate, p_up, p_out, p_down)], [l_in, l_gate, l_up, l_out, l_down])
    big = ("w_in", "w_gate", "w_up", "w_out", "w_down")
    transposed = ("w_in", "w_gate", "w_up")
    g_big = {name: _both_halves(mine, theirs) for name, mine, theirs in zip(big, halves, _swap_with_sibling(halves))}

    grads = {
        "meta_tokens": g_meta, "attn_norm_g": red["attn_norm_g"], "attn_sinks": red["attn_sinks"],
        "conv_w": g_convw, "conv_b": red["conv_b"], "conv_ln_g": red["conv_ln_g"], "conv_ln_b": red["conv_ln_b"],
        "attn_out_g": red["attn_out_g"], "conv_out_g": red["conv_out_g"], "ffn_norm_g": red["ffn_norm_g"],
        "final_norm_g": red["final_norm_g"]}
    params = {
        "meta_tokens": (meta_tokens, m_meta_tokens, v_meta_tokens), "attn_norm_g": (attn_norm_g, m_attn_norm_g, v_attn_norm_g),
        "w_in": (w_in, m_w_in, v_w_in), "attn_sinks": (attn_sinks, m_attn_sinks, v_attn_sinks), "conv_w": (conv_w, m_conv_w, v_conv_w),
        "conv_b": (conv_b, m_conv_b, v_conv_b), "conv_ln_g": (conv_ln_g, m_conv_ln_g, v_conv_ln_g),
        "conv_ln_b": (conv_ln_b, m_conv_ln_b, v_conv_ln_b), "attn_out_g": (attn_out_g, m_attn_out_g, v_attn_out_g),
        "conv_out_g": (conv_out_g, m_conv_out_g, v_conv_out_g), "w_out": (w_out, m_w_out, v_w_out),
        "ffn_norm_g": (ffn_norm_g, m_ffn_norm_g, v_ffn_norm_g), "w_gate": (w_gate, m_w_gate, v_w_gate), "w_up": (w_up, m_w_up, v_w_up),
        "w_down": (w_down, m_w_down, v_w_down), "final_norm_g": (final_norm_g, m_final_norm_g, v_final_norm_g)}
    names = list(params)
    delta, new_m, new_v = {}, {}, {}
    for name in big:
        flip = (lambda a: a.T) if name in transposed else (lambda a: a)
        w, m, v = params[name]
        outs = _adamw(flip(w[0]), g_big[name], flip(m[0]), flip(v[0]), "adamw_" + name)
        grads[name], delta[name], new_m[name], new_v[name] = [flip(a)[None] for a in (g_big[name], *outs)]
    rest = [name for name in names if name not in big]

    def rows_of(a):
        return a.reshape(-1, a.shape[-1])

    small = _adamw_small([rows_of(params[n][0]) for n in rest], [rows_of(grads[n]) for n in rest],
                         [rows_of(params[n][1]) for n in rest], [rows_of(params[n][2]) for n in rest])
    for dst, outs in zip((delta, new_m, new_v), small):
        for name, out in zip(rest, outs):
            dst[name] = out.reshape(params[name][0].shape)

    return (loss, grad_x, *[grads[n] for n in names], *[delta[n] for n in names], *[new_m[n] for n in names],
            *[new_v[n] for n in names])
```

```python
import functools
import math

import jax
import jax.numpy as jnp
from jax import lax
from jax.experimental import pallas as pl
from jax.experimental.pallas import tpu as pltpu

F32 = jnp.float32
BF16 = jnp.bfloat16

D_MODEL = 1024
N_META = 16
ATTN_W = 512
CONV_W = 512
HEAD_DIM = 64
N_HEADS = 8
N_KV = 2
GROUP = N_HEADS // N_KV
KV_W = N_KV * HEAD_DIM
BLOCK = 128
LEAD = BLOCK - N_META
CONV_K = 31
D_FF = 2816
IN_COLS = ATTN_W + 2 * KV_W + 2 * CONV_W
Q0, KV0, C0 = 0, ATTN_W, ATTN_W + 2 * KV_W
NORM_EPS = 1e-5
SCALE = 1.0 / math.sqrt(HEAD_DIM)
SLOPES = tuple(2.0 ** (-(8.0 / N_HEADS) * (h + 1)) for h in range(N_HEADS))
NEG = -1e30

ADAM_LR, ADAM_B1, ADAM_B2, ADAM_EPS, ADAM_WD, ADAM_STEP = 0.001, 0.9, 0.999, 1e-08, 0.01, 10

N_SHARD = 4
N_DEV = 8
ROW_QUANTUM = 768
HALO = 32
CONV_CHUNK = 32
FF_CHUNK = 256
VMEM_LIMIT = 60 * 1024 * 1024


def _cparams(n_axes=1):
    return pltpu.CompilerParams(dimension_semantics=("arbitrary",) * n_axes, vmem_limit_bytes=VMEM_LIMIT)


def _dot(a, b):
    return jnp.dot(a, b, preferred_element_type=F32)


def _dot_nt(a, b):
    return lax.dot_general(a, b, (((1,), (1,)), ((), ())), preferred_element_type=F32)


def _dot_tn(a, b):
    return lax.dot_general(a, b, (((0,), (0,)), ((), ())), preferred_element_type=F32)


def _sigmoid(x):
    return 1.0 / (1.0 + jnp.exp(-x))


def _row(tm, n):
    return pl.BlockSpec((tm, n), lambda i: (i, 0))


def _const(shape):
    return pl.BlockSpec(shape, lambda i: (0,) * len(shape))


def _resident(shape):
    return pl.BlockSpec(shape, lambda i: (0,) * len(shape), pipeline_mode=pl.Buffered(1))


def _rms_fwd(x, g):
    rstd = lax.rsqrt(jnp.mean(x * x, axis=-1, keepdims=True) + NORM_EPS)
    xhat = x * rstd
    return xhat * g, xhat, rstd


def _rms_bwd(dy, xhat, rstd, g):
    dxh = dy * g
    dx = rstd * (dxh - xhat * jnp.mean(dxh * xhat, axis=-1, keepdims=True))
    return dx, dy * xhat


def _in_proj(h0, g1, w_in_t, tm):
    r = h0.shape[0]

    def body(h_ref, g_ref, w_ref, q_ref, kv_ref, c_ref):
        hn = _rms_fwd(h_ref[...], g_ref[...])[0].astype(BF16)
        q_ref[...] = _dot_nt(hn, w_ref[Q0:KV0, :]).astype(BF16)
        kv_ref[...] = _dot_nt(hn, w_ref[KV0:C0, :]).astype(BF16)
        c_ref[...] = _dot_nt(hn, w_ref[C0:IN_COLS, :])

    return pl.pallas_call(
        body, name="in_proj", grid=(r // tm,),
        in_specs=[_row(tm, D_MODEL), _const((1, D_MODEL)), _const((IN_COLS, D_MODEL))],
        out_specs=[_row(tm, ATTN_W), _row(tm, 2 * KV_W), _row(tm, 2 * CONV_W)],
        out_shape=[jax.ShapeDtypeStruct((r, ATTN_W), BF16), jax.ShapeDtypeStruct((r, 2 * KV_W), BF16),
                   jax.ShapeDtypeStruct((r, 2 * CONV_W), F32)],
        compiler_params=_cparams(),
    )(h0, g1, w_in_t)


def _attn_bias_init(bias_ref, late_ref):
    row = lax.broadcasted_iota(jnp.int32, (GROUP * BLOCK, BLOCK), 0) & (BLOCK - 1)
    col = lax.broadcasted_iota(jnp.int32, (GROUP * BLOCK, BLOCK), 1)
    late_ref[...] = jnp.where(col > row, 1.0, 0.0)
    for g in range(N_KV):
        slope = jnp.concatenate([jnp.zeros((BLOCK, BLOCK), F32) + SLOPES[g * GROUP + j] for j in range(GROUP)], axis=0)
        bias_ref[g, :, 0:BLOCK] = jnp.where(col >= LEAD, 0.0, NEG)
        bias_ref[g, :, BLOCK:2 * BLOCK] = -slope * jnp.where(col > row, row - col + BLOCK, row - col).astype(F32)


def _attn_block_bias(late_ref, i):
    late = late_ref[...]
    meta0 = jnp.where(i == 0, NEG, 0.0)
    no_prev = jnp.where(i >= 2, 0.0, NEG)
    no_cur = jnp.where(i >= 1, 0.0, NEG)
    return late * meta0, late * no_prev + no_cur


def _attn_logits(s3, bias_ref, block_bias, prev_part, g):
    meta = s3[:, 0:BLOCK] * SCALE + (bias_ref[g, :, 0:BLOCK] + block_bias[0])
    band = jnp.where(prev_part, s3[:, BLOCK:2 * BLOCK], s3[:, 2 * BLOCK:3 * BLOCK]) * SCALE + (bias_ref[g, :, BLOCK:2 * BLOCK] + block_bias[1])
    return meta, band


def _split_band(meta, band, prev_part):
    return jnp.concatenate([meta, jnp.where(prev_part, band, 0.0), jnp.where(prev_part, 0.0, band)], axis=1)


def _head_rows(vals):
    return jnp.concatenate([jnp.zeros((BLOCK, BLOCK), F32) + v for v in vals], axis=0)


def _stack_heads(ref, g):
    return jnp.concatenate([ref[:, (g * GROUP + j) * HEAD_DIM:(g * GROUP + j + 1) * HEAD_DIM] for j in range(GROUP)], axis=0)


def _kv_cat(kvm_ref, kvp_ref, kvc_ref, g):
    ks = slice(g * HEAD_DIM, (g + 1) * HEAD_DIM)
    vs = slice(KV_W + g * HEAD_DIM, KV_W + (g + 1) * HEAD_DIM)
    kcat = jnp.concatenate([kvm_ref[:, ks], kvp_ref[:, ks], kvc_ref[:, ks]], axis=0)
    vcat = jnp.concatenate([kvm_ref[:, vs], kvp_ref[:, vs], kvc_ref[:, vs]], axis=0)
    return kcat, vcat


def _attn_fwd(q, kv, sinks, shards, gathered):
    r = q.shape[0]
    nb = r // BLOCK
    n = len(shards)

    def body(sink_ref, q_ref, kvc_ref, kvp_ref, kvm_ref, *rest):
        src = rest[:n]
        o_ref, lse_ref = rest[2 * n:2 * n + 2]
        dst = rest[2 * n + 2:3 * n + 2]
        bias_ref, late_ref, send_sems, recv_sems = rest[3 * n + 2:]
        i = pl.program_id(0)

        @pl.when(i == 0)
        def _():
            for cp in _gather_ici(src, dst, send_sems, recv_sems)[0]:
                cp.start()
            _attn_bias_init(bias_ref, late_ref)

        lane = lax.broadcasted_iota(jnp.int32, (BLOCK, BLOCK), 1)
        lse_tile = jnp.zeros((BLOCK, BLOCK), F32)
        block_bias = _attn_block_bias(late_ref, i)
        prev_part = late_ref[...] > 0.5
        for g in range(N_KV):
            kcat, vcat = _kv_cat(kvm_ref, kvp_ref, kvc_ref, g)
            heads = range(g * GROUP, (g + 1) * GROUP)
            meta, band = _attn_logits(_dot_nt(_stack_heads(q_ref, g), kcat), bias_ref, block_bias, prev_part, g)
            sink = _head_rows([sink_ref[h] for h in heads])
            m = jnp.maximum(jnp.max(jnp.maximum(meta, band), axis=-1, keepdims=True), sink)
            p_meta, p_band = jnp.exp(meta - m), jnp.exp(band - m)
            l = jnp.sum(p_meta + p_band, axis=-1, keepdims=True) + jnp.exp(sink - m)
            o = _dot(_split_band(p_meta, p_band, prev_part).astype(BF16), vcat) * (1.0 / l)[:, 0:HEAD_DIM]
            lse = m + jnp.log(l)
            for j, h in enumerate(heads):
                o_ref[:, h * HEAD_DIM:(h + 1) * HEAD_DIM] = o[j * BLOCK:(j + 1) * BLOCK]
                lse_tile = jnp.where(lane == h, lse[j * BLOCK:(j + 1) * BLOCK], lse_tile)
        lse_ref[...] = lse_tile

        @pl.when(i == nb - 1)
        def _():
            sends, arrivals = _gather_ici(src, dst, send_sems, recv_sems)
            for cp in arrivals:
                cp.wait_recv()
            for cp in sends:
                cp.wait_send()

    return pl.pallas_call(
        body, name="attn_fwd", grid=(nb,),
        in_specs=[pl.BlockSpec(memory_space=pltpu.SMEM), _row(BLOCK, ATTN_W), _row(BLOCK, 2 * KV_W),
                  pl.BlockSpec((BLOCK, 2 * KV_W), lambda i: (jnp.maximum(i - 1, 0), 0)), _const((BLOCK, 2 * KV_W))] + [ANY] * (2 * n),
        out_specs=[_row(BLOCK, ATTN_W), _row(BLOCK, BLOCK)] + [ANY] * n,
        out_shape=[jax.ShapeDtypeStruct((r, ATTN_W), F32), jax.ShapeDtypeStruct((r, BLOCK), F32)]
        + [jax.ShapeDtypeStruct(g.shape, g.dtype) for g in gathered],
        input_output_aliases={5 + n + k: 2 + k for k in range(n)},
        scratch_shapes=[pltpu.VMEM((N_KV, GROUP * BLOCK, 2 * BLOCK), F32), pltpu.VMEM((GROUP * BLOCK, BLOCK), F32),
                        pltpu.SemaphoreType.DMA((3 * n,)), pltpu.SemaphoreType.DMA((3 * n,))],
        compiler_params=_cparams(),
    )(sinks, q, kv, kv, kv, *shards, *gathered)


def _shifted_copies(ub_ref, win):
    w = win.shape[0]
    ub_ref[0] = win
    for b in range(1, 8):
        ub_ref[b] = pltpu.roll(win, shift=w - b, axis=0)


def _conv_chunk(ub_ref, w_ref, r0, shifts):
    acc = jnp.zeros((CONV_CHUNK, CONV_W), F32)
    for j in range(CONV_K):
        a, b = divmod(shifts[j], 8)
        acc = acc + w_ref[j:j + 1, :] * ub_ref[b, pl.ds(r0 + 8 * a, CONV_CHUNK), :]
    return acc


FWD_SHIFTS = tuple(HALO - (CONV_K - 1) + j for j in range(CONV_K))
BWD_SHIFTS = tuple(CONV_K - 1 - j for j in range(CONV_K))


def _glu_window(cp_ref, c_ref, i):
    tile = c_ref[:, 0:CONV_W] * _sigmoid(c_ref[:, CONV_W:2 * CONV_W])
    halo = cp_ref[:, 0:CONV_W] * _sigmoid(cp_ref[:, CONV_W:2 * CONV_W])
    first = (jnp.zeros((HALO, CONV_W), jnp.int32) + i) == 0
    return jnp.concatenate([jnp.where(first, 0.0, halo), tile], axis=0)


def _halo_before(tm, n):
    return pl.BlockSpec((HALO, n), lambda i: (jnp.maximum(i * (tm // HALO) - 1, 0), 0))


def _conv_fwd(cacg, cw, cb, lg, lb, tm, gathered):
    r = cacg.shape[0]
    n = len(gathered)

    def body(c_ref, cp_ref, w_ref, cb_ref, lg_ref, lb_ref, *rest):
        o_ref, y_ref = rest[n:n + 2]
        dst = rest[n + 2:2 * n + 2]
        ub_ref, send_sems, recv_sems = rest[2 * n + 2:]
        i = pl.program_id(0)

        @pl.when(i == 0)
        def _():
            for cp in _gather_d2d(dst, send_sems, recv_sems)[0]:
                cp.start()

        _shifted_copies(ub_ref, _glu_window(cp_ref, c_ref, i))

        def chunk(ci, carry):
            r0 = pl.multiple_of(ci * CONV_CHUNK, CONV_CHUNK)
            y = _conv_chunk(ub_ref, w_ref, r0, FWD_SHIFTS) + cb_ref[...]
            yc = y - jnp.mean(y, axis=-1, keepdims=True)
            rs = lax.rsqrt(jnp.mean(yc * yc, axis=-1, keepdims=True) + NORM_EPS)
            yn = yc * rs * lg_ref[...] + lb_ref[...]
            o_ref[pl.ds(r0, CONV_CHUNK), :] = yn * _sigmoid(yn)
            y_ref[pl.ds(r0, CONV_CHUNK), :] = y
            return carry

        lax.fori_loop(0, tm // CONV_CHUNK, chunk, 0, unroll=4)

        @pl.when(i == r // tm - 1)
        def _():
            sends, arrivals = _gather_d2d(dst, send_sems, recv_sems)
            for cp in arrivals:
                cp.wait_recv()
            for cp in sends:
                cp.wait_send()

    return pl.pallas_call(
        body, name="conv_fwd", grid=(r // tm,),
        in_specs=[_row(tm, 2 * CONV_W), _halo_before(tm, 2 * CONV_W), _const((32, CONV_W)), _const((1, CONV_W)),
                  _const((1, CONV_W)), _const((1, CONV_W))] + [ANY] * n,
        out_specs=[_row(tm, CONV_W), _row(tm, CONV_W)] + [ANY] * n,
        out_shape=[jax.ShapeDtypeStruct((r, CONV_W), F32)] * 2 + [jax.ShapeDtypeStruct(g.shape, g.dtype) for g in gathered],
        input_output_aliases={6 + k: 2 + k for k in range(n)},
        scratch_shapes=[pltpu.VMEM((8, tm + HALO, CONV_W), F32), pltpu.SemaphoreType.DMA((3 * n,)), pltpu.SemaphoreType.DMA((3 * n,))],
        compiler_params=_cparams(),
    )(cacg, cacg, cw, cb, lg, lb, *gathered)


def _out_proj(oa, oc, h0, ga, gc, g2, w_out_b, tm):
    r = h0.shape[0]

    def body(oa_ref, oc_ref, h_ref, ga_ref, gc_ref, g2_ref, w_ref, h1_ref, hn2_ref):
        ma = _rms_fwd(oa_ref[...], ga_ref[...])[0].astype(BF16)
        mc = _rms_fwd(oc_ref[...], gc_ref[...])[0].astype(BF16)
        h1 = h_ref[...] + _dot(jnp.concatenate([ma, mc], axis=1), w_ref[...])
        h1_ref[...] = h1
        hn2_ref[...] = _rms_fwd(h1, g2_ref[...])[0].astype(BF16)

    return pl.pallas_call(
        body, name="out_proj", grid=(r // tm,),
        in_specs=[_row(tm, ATTN_W), _row(tm, CONV_W), _row(tm, D_MODEL), _const((1, ATTN_W)), _const((1, CONV_W)),
                  _const((1, D_MODEL)), _const((D_MODEL, D_MODEL))],
        out_specs=[_row(tm, D_MODEL), _row(tm, D_MODEL)],
        out_shape=[jax.ShapeDtypeStruct((r, D_MODEL), F32), jax.ShapeDtypeStruct((r, D_MODEL), BF16)],
        compiler_params=_cparams(),
    )(oa, oc, h0, ga, gc, g2, w_out_b)


def _ffn_fwd(hn2, h1, target, gf, wg_t, wu_t, wd_b, tm):
    r = h1.shape[0]
    seq = target.shape[0]
    n_sub = tm // BLOCK

    def body(hn_ref, h1_ref, *rest):
        t_refs = rest[:n_sub]
        gf_ref, wg_ref, wu_ref, wd_ref, gate_ref, up_ref, act_ref, dh2_ref, dh2b_ref, loss_ref, dgf_ref = rest[n_sub:]
        i = pl.program_id(0)

        @pl.when(i == 0)
        def _():
            loss_ref[...] = jnp.zeros_like(loss_ref)
            dgf_ref[...] = jnp.zeros_like(dgf_ref)

        hn = hn_ref[...]
        for ch in range(D_FF // FF_CHUNK):
            cs = slice(ch * FF_CHUNK, (ch + 1) * FF_CHUNK)
            gate = _dot_nt(hn, wg_ref[cs, :])
            up = _dot_nt(hn, wu_ref[cs, :])
            gate_ref[:, cs] = gate.astype(BF16)
            up_ref[:, cs] = up.astype(BF16)
            act_ref[:, cs] = (gate * _sigmoid(gate) * up).astype(BF16)
        y, xhat, rstd = _rms_fwd(h1_ref[...] + _dot(act_ref[...], wd_ref[...]), gf_ref[...])
        rows = lax.broadcasted_iota(jnp.int32, (tm, D_MODEL), 0) + i * tm
        real = (rows >= BLOCK) & (rows < BLOCK + seq)
        err = jnp.where(real, y - jnp.concatenate([t[...] for t in t_refs], axis=0), 0.0)
        loss_ref[...] += jnp.sum(err * err) * (0.5 / D_MODEL)
        dy = err * (1.0 / D_MODEL)
        dh2, dg_rows = _rms_bwd(dy, xhat, rstd, gf_ref[...])
        dgf_ref[...] += jnp.sum(dg_rows, axis=0, keepdims=True)
        dh2_ref[...] = dh2
        dh2b_ref[...] = dh2.astype(BF16)

    def target_block(k):
        return pl.BlockSpec((BLOCK, D_MODEL), lambda i: (jnp.clip(n_sub * i - 1 + k, 0, seq // BLOCK - 1), 0))

    return pl.pallas_call(
        body, name="ffn_fwd", grid=(r // tm,),
        in_specs=[_row(tm, D_MODEL), _row(tm, D_MODEL)] + [target_block(k) for k in range(n_sub)]
        + [_const((1, D_MODEL))] + [_resident((D_FF, D_MODEL))] * 3,
        out_specs=[_row(tm, D_FF)] * 3 + [_row(tm, D_MODEL), _row(tm, D_MODEL), _const((1, BLOCK)), _const((1, D_MODEL))],
        out_shape=[jax.ShapeDtypeStruct((r, D_FF), BF16)] * 3
        + [jax.ShapeDtypeStruct((r, D_MODEL), F32), jax.ShapeDtypeStruct((r, D_MODEL), BF16),
           jax.ShapeDtypeStruct((1, BLOCK), F32), jax.ShapeDtypeStruct((1, D_MODEL), F32)],
        compiler_params=_cparams(),
    )(hn2, h1, *[target] * n_sub, gf, wg_t, wu_t, wd_b)


def _ffn_bwd(dh2, dh2b, gate, up, h1, g2, wg_t, wu_t, wd_b, tm):
    r = h1.shape[0]

    def body(dh2_ref, dh2b_ref, gate_ref, up_ref, h1_ref, g2_ref, wg_ref, wu_ref, wd_ref, dgate_ref, dup_ref, dh1_ref, dg2_ref):
        @pl.when(pl.program_id(0) == 0)
        def _():
            dg2_ref[...] = jnp.zeros_like(dg2_ref)

        dyb = dh2b_ref[...]
        for ch in range(D_FF // FF_CHUNK):
            cs = slice(ch * FF_CHUNK, (ch + 1) * FF_CHUNK)
            dact = _dot_nt(dyb, wd_ref[cs, :])
            gate = gate_ref[:, cs].astype(F32)
            up = up_ref[:, cs].astype(F32)
            sg = _sigmoid(gate)
            dgate_ref[:, cs] = (dact * up * (sg * (1.0 + gate * (1.0 - sg)))).astype(BF16)
            dup_ref[:, cs] = (dact * (gate * sg)).astype(BF16)
        dhn = _dot(dgate_ref[...], wg_ref[...]) + _dot(dup_ref[...], wu_ref[...])
        _, xhat, rstd = _rms_fwd(h1_ref[...], g2_ref[...])
        dx, dg_rows = _rms_bwd(dhn, xhat, rstd, g2_ref[...])
        dg2_ref[...] += jnp.sum(dg_rows, axis=0, keepdims=True)
        dh1_ref[...] = dh2_ref[...] + dx

    return pl.pallas_call(
        body, name="ffn_bwd", grid=(r // tm,),
        in_specs=[_row(tm, D_MODEL), _row(tm, D_MODEL), _row(tm, D_FF), _row(tm, D_FF), _row(tm, D_MODEL), _const((1, D_MODEL))]
        + [_resident((D_FF, D_MODEL))] * 3,
        out_specs=[_row(tm, D_FF), _row(tm, D_FF), _row(tm, D_MODEL), _const((1, D_MODEL))],
        out_shape=[jax.ShapeDtypeStruct((r, D_FF), BF16), jax.ShapeDtypeStruct((r, D_FF), BF16),
                   jax.ShapeDtypeStruct((r, D_MODEL), F32), jax.ShapeDtypeStruct((1, D_MODEL), F32)],
        compiler_params=_cparams(),
    )(dh2, dh2b, gate, up, h1, g2, wg_t, wu_t, wd_b)


FF_HALF = D_FF // 2


def _ffn_wgrad_gu(hn2, dgate, dup, tk):
    r = hn2.shape[0]
    n_k = r // tk

    def body(hn_ref, dg_ref, du_ref, wg_ref, wu_ref, accg_ref, accu_ref):
        k = pl.program_id(1)

        @pl.when(k == 0)
        def _():
            accg_ref[...] = jnp.zeros_like(accg_ref)
            accu_ref[...] = jnp.zeros_like(accu_ref)

        hn = hn_ref[...]
        accg_ref[...] += _dot_tn(dg_ref[...], hn)
        accu_ref[...] += _dot_tn(du_ref[...], hn)

        @pl.when(k == n_k - 1)
        def _():
            wg_ref[...] = accg_ref[...].astype(BF16)
            wu_ref[...] = accu_ref[...].astype(BF16)

    col = pl.BlockSpec((tk, FF_HALF), lambda j, k: (k, j))
    out = pl.BlockSpec((FF_HALF, D_MODEL), lambda j, k: (j, 0))
    return pl.pallas_call(
        body, name="ffn_wgrad_gu", grid=(2, n_k),
        in_specs=[pl.BlockSpec((tk, D_MODEL), lambda j, k: (k, 0)), col, col],
        out_specs=[out, out],
        out_shape=[jax.ShapeDtypeStruct((D_FF, D_MODEL), BF16)] * 2,
        scratch_shapes=[pltpu.VMEM((FF_HALF, D_MODEL), F32)] * 2,
        compiler_params=_cparams(2),
    )(hn2, dgate, dup)


def _ffn_wgrad_d(act, dh2b, tk):
    r = act.shape[0]
    n_k = r // tk

    def body(a_ref, dy_ref, wd_ref, acc_ref):
        k = pl.program_id(1)

        @pl.when(k == 0)
        def _():
            acc_ref[...] = jnp.zeros_like(acc_ref)

        acc_ref[...] += _dot_tn(a_ref[...], dy_ref[...])

        @pl.when(k == n_k - 1)
        def _():
            wd_ref[...] = acc_ref[...].astype(BF16)

    return pl.pallas_call(
        body, name="ffn_wgrad_d", grid=(2, n_k),
        in_specs=[pl.BlockSpec((tk, FF_HALF), lambda j, k: (k, j)), pl.BlockSpec((tk, D_MODEL), lambda j, k: (k, 0))],
        out_specs=pl.BlockSpec((FF_HALF, D_MODEL), lambda j, k: (j, 0)),
        out_shape=jax.ShapeDtypeStruct((D_FF, D_MODEL), BF16),
        scratch_shapes=[pltpu.VMEM((FF_HALF, D_MODEL), F32)],
        compiler_params=_cparams(2),
    )(act, dh2b)


def _out_proj_bwd(dh1, oa, oc, ga, gc, w_out_b, tm):
    r = dh1.shape[0]

    def body(dh_ref, oa_ref, oc_ref, ga_ref, gc_ref, w_ref, doa_ref, doc_ref, dw_ref, dga_ref, dgc_ref, acc_ref):
        i = pl.program_id(0)

        @pl.when(i == 0)
        def _():
            acc_ref[...] = jnp.zeros_like(acc_ref)
            dga_ref[...] = jnp.zeros_like(dga_ref)
            dgc_ref[...] = jnp.zeros_like(dgc_ref)

        dhb = dh_ref[...].astype(BF16)
        dmix = _dot_nt(dhb, w_ref[...])
        ma, xa, ra = _rms_fwd(oa_ref[...], ga_ref[...])
        mc, xc, rc = _rms_fwd(oc_ref[...], gc_ref[...])
        acc_ref[...] += _dot_tn(jnp.concatenate([ma.astype(BF16), mc.astype(BF16)], axis=1), dhb)

        @pl.when(i == r // tm - 1)
        def _():
            dw_ref[...] = acc_ref[...].astype(BF16)

        doa, dga_rows = _rms_bwd(dmix[:, 0:ATTN_W], xa, ra, ga_ref[...])
        doc, dgc_rows = _rms_bwd(dmix[:, ATTN_W:ATTN_W + CONV_W], xc, rc, gc_ref[...])
        doa_ref[...] = doa
        doc_ref[...] = doc
        dga_ref[...] += jnp.sum(dga_rows, axis=0, keepdims=True)
        dgc_ref[...] += jnp.sum(dgc_rows, axis=0, keepdims=True)

    return pl.pallas_call(
        body, name="out_proj_bwd", grid=(r // tm,),
        in_specs=[_row(tm, D_MODEL), _row(tm, ATTN_W), _row(tm, CONV_W), _const((1, ATTN_W)), _const((1, CONV_W)),
                  _const((D_MODEL, D_MODEL))],
        out_specs=[_row(tm, ATTN_W), _row(tm, CONV_W), _const((D_MODEL, D_MODEL)), _const((1, ATTN_W)), _const((1, CONV_W))],
        out_shape=[jax.ShapeDtypeStruct((r, ATTN_W), F32), jax.ShapeDtypeStruct((r, CONV_W), F32),
                   jax.ShapeDtypeStruct((D_MODEL, D_MODEL), BF16), jax.ShapeDtypeStruct((1, ATTN_W), F32),
                   jax.ShapeDtypeStruct((1, CONV_W), F32)],
        scratch_shapes=[pltpu.VMEM((D_MODEL, D_MODEL), F32)],
        compiler_params=_cparams(),
    )(dh1, oa, oc, ga, gc, w_out_b)


def _conv_bwd_params(doc, y, cacg, lg, lb, tm, parts):
    r = cacg.shape[0]
    n_steps = r // tm
    n = len(parts)

    def body(do_ref, y_ref, c_ref, cp_ref, lg_ref, lb_ref, *rest):
        src = rest[:n]
        dy_ref, dcw_ref, dcb_ref, dlg_ref, dlb_ref = rest[n:n + 5]
        dst = rest[n + 5:2 * n + 5]
        ub_ref, accw_ref, send_sems, recv_sems = rest[2 * n + 5:]
        i = pl.program_id(0)

        @pl.when(i == 0)
        def _():
            for cp in _scatter(src, dst, send_sems, recv_sems):
                cp.start()
            accw_ref[...] = jnp.zeros_like(accw_ref)
            dcb_ref[...] = jnp.zeros_like(dcb_ref)
            dlg_ref[...] = jnp.zeros_like(dlg_ref)
            dlb_ref[...] = jnp.zeros_like(dlb_ref)

        _shifted_copies(ub_ref, _glu_window(cp_ref, c_ref, i))

        def chunk(ci, carry):
            r0 = pl.multiple_of(ci * CONV_CHUNK, CONV_CHUNK)
            y = y_ref[pl.ds(r0, CONV_CHUNK), :]
            yc = y - jnp.mean(y, axis=-1, keepdims=True)
            rs = lax.rsqrt(jnp.mean(yc * yc, axis=-1, keepdims=True) + NORM_EPS)
            xhat = yc * rs
            yn = xhat * lg_ref[...] + lb_ref[...]
            sg = _sigmoid(yn)
            dyn = do_ref[pl.ds(r0, CONV_CHUNK), :] * (sg * (1.0 + yn * (1.0 - sg)))
            dlg_ref[...] += jnp.sum(dyn * xhat, axis=0, keepdims=True)
            dlb_ref[...] += jnp.sum(dyn, axis=0, keepdims=True)
            dxh = dyn * lg_ref[...]
            dy = rs * (dxh - jnp.mean(dxh, axis=-1, keepdims=True) - xhat * jnp.mean(dxh * xhat, axis=-1, keepdims=True))
            dcb_ref[...] += jnp.sum(dy, axis=0, keepdims=True)
            dy_ref[pl.ds(r0, CONV_CHUNK), :] = dy
            for j in range(CONV_K):
                a, b = divmod(FWD_SHIFTS[j], 8)
                prod = dy * ub_ref[b, pl.ds(r0 + 8 * a, CONV_CHUNK), :]
                accw_ref[j] += jnp.sum(prod.reshape(CONV_CHUNK // 8, 8, CONV_W), axis=0)
            return carry

        lax.fori_loop(0, tm // CONV_CHUNK, chunk, 0, unroll=4)

        @pl.when(i == n_steps - 1)
        def _():
            for j in range(32):
                dcw_ref[j:j + 1, :] = jnp.sum(accw_ref[j], axis=0, keepdims=True)
            for cp in _scatter(src, dst, send_sems, recv_sems):
                cp.wait()

    vec = _const((1, CONV_W))
    return pl.pallas_call(
        body, name="conv_bwd_params", grid=(n_steps,),
        in_specs=[_row(tm, CONV_W), _row(tm, CONV_W), _row(tm, 2 * CONV_W), _halo_before(tm, 2 * CONV_W), vec, vec] + [ANY] * n,
        out_specs=[_row(tm, CONV_W), _const((32, CONV_W)), vec, vec, vec] + [ANY] * n,
        out_shape=[jax.ShapeDtypeStruct((r, CONV_W), F32), jax.ShapeDtypeStruct((32, CONV_W), F32)]
        + [jax.ShapeDtypeStruct((1, CONV_W), F32)] * 3 + _scatter_landing(parts),
        scratch_shapes=[pltpu.VMEM((8, tm + HALO, CONV_W), F32), pltpu.VMEM((32, 8, CONV_W), F32),
                        pltpu.SemaphoreType.DMA((7 * n,)), pltpu.SemaphoreType.DMA((7 * n,))],
        compiler_params=_cparams(),
    )(doc, y, cacg, cacg, lg, lb, *parts)


def _conv_bwd_data(dy, cacg, cw, tm):
    r = cacg.shape[0]
    n_steps = r // tm

    def body(dy_ref, dyn_ref, c_ref, w_ref, dc_ref, ub_ref):
        last = (jnp.zeros((HALO, CONV_W), jnp.int32) + pl.program_id(0)) == n_steps - 1
        win = jnp.concatenate([dy_ref[...], jnp.where(last, 0.0, dyn_ref[...])], axis=0)
        _shifted_copies(ub_ref, win)

        def chunk(ci, carry):
            r0 = pl.multiple_of(ci * CONV_CHUNK, CONV_CHUNK)
            du = _conv_chunk(ub_ref, w_ref, r0, BWD_SHIFTS)
            ca = c_ref[pl.ds(r0, CONV_CHUNK), 0:CONV_W]
            sg = _sigmoid(c_ref[pl.ds(r0, CONV_CHUNK), CONV_W:2 * CONV_W])
            dc_ref[pl.ds(r0, CONV_CHUNK), 0:CONV_W] = (du * sg).astype(BF16)
            dc_ref[pl.ds(r0, CONV_CHUNK), CONV_W:2 * CONV_W] = (du * ca * sg * (1.0 - sg)).astype(BF16)
            return carry

        lax.fori_loop(0, tm // CONV_CHUNK, chunk, 0, unroll=4)

    halo_after = pl.BlockSpec((HALO, CONV_W), lambda i: (jnp.minimum((i + 1) * (tm // HALO), r // HALO - 1), 0))
    return pl.pallas_call(
        body, name="conv_bwd_data", grid=(n_steps,),
        in_specs=[_row(tm, CONV_W), halo_after, _row(tm, 2 * CONV_W), _const((32, CONV_W))],
        out_specs=_row(tm, 2 * CONV_W),
        out_shape=jax.ShapeDtypeStruct((r, 2 * CONV_W), BF16),
        scratch_shapes=[pltpu.VMEM((8, tm + HALO, CONV_W), F32)],
        compiler_params=_cparams(),
    )(dy, dy, cacg, cw)


def _attn_bwd(q, kv, o, do, lse, sinks, parts):
    r = q.shape[0]
    nb = r // BLOCK
    n = len(parts)

    def body(sink_ref, q_ref, kvc_ref, kvp_ref, kvm_ref, o_ref, do_ref, lse_ref, *rest):
        src = rest[:n]
        dq_ref, dkv_ref, dmeta_ref, dsink_ref = rest[n:n + 4]
        dst = rest[n + 4:2 * n + 4]
        hold_ref, bias_ref, late_ref, send_sems, recv_sems = rest[2 * n + 4:]
        i = pl.program_id(0)

        @pl.when(i == 0)
        def _():
            for cp in _scatter(src, dst, send_sems, recv_sems):
                cp.start()
            _attn_bias_init(bias_ref, late_ref)
            dmeta_ref[...] = jnp.zeros_like(dmeta_ref)
            dsink_ref[...] = jnp.zeros_like(dsink_ref)
            hold_ref[...] = jnp.zeros_like(hold_ref)

        @pl.when(i < nb)
        def _():
            lane = lax.broadcasted_iota(jnp.int32, (BLOCK, BLOCK), 1)
            lse_tile = lse_ref[...]
            zero = jnp.zeros((BLOCK, BLOCK), F32)
            block_bias = _attn_block_bias(late_ref, i)
            prev_part = late_ref[...] > 0.5
            for g in range(N_KV):
                kcat, vcat = _kv_cat(kvm_ref, kvp_ref, kvc_ref, g)
                heads = range(g * GROUP, (g + 1) * GROUP)
                qs = _stack_heads(q_ref, g)
                dos = _stack_heads(do_ref, g)
                dosb = dos.astype(BF16)
                lse = jnp.concatenate(
                    [jnp.sum(jnp.where(lane == h, lse_tile, 0.0), axis=-1, keepdims=True) + zero for h in heads], axis=0)
                delta = jnp.sum(dos * _stack_heads(o_ref, g), axis=-1, keepdims=True) + jnp.zeros((GROUP * BLOCK, BLOCK), F32)
                band_bias = bias_ref[g, :, BLOCK:2 * BLOCK] + block_bias[1]
                bias = [bias_ref[g, :, 0:BLOCK] + block_bias[0], jnp.where(prev_part, band_bias, NEG), jnp.where(prev_part, NEG, band_bias)]
                s = _dot_nt(qs, kcat)
                dp = _dot_nt(dosb, vcat)
                ps = [jnp.exp(s[:, k * BLOCK:(k + 1) * BLOCK] * SCALE + bias[k] - lse) for k in range(3)]
                p = jnp.concatenate(ps, axis=1)
                ds = jnp.concatenate(
                    [(ps[k] * (dp[:, k * BLOCK:(k + 1) * BLOCK] - delta)) * SCALE for k in range(3)], axis=1).astype(BF16)
                sink_term = jnp.exp(_head_rows([sink_ref[h] for h in heads]) - lse)[:, 0:1] * delta[:, 0:1]
                dq = _dot(ds, kcat).astype(BF16)
                for j, h in enumerate(heads):
                    dsink_ref[h:h + 1, :] += -jnp.sum(sink_term[j * BLOCK:(j + 1) * BLOCK])
                    dq_ref[:, h * HEAD_DIM:(h + 1) * HEAD_DIM] = dq[j * BLOCK:(j + 1) * BLOCK]
                dk_t = _dot_tn(qs, ds)
                dv_t = _dot_tn(dosb, p.astype(BF16))
                ks = slice(g * HEAD_DIM, (g + 1) * HEAD_DIM)
                vs = slice(KV_W + g * HEAD_DIM, KV_W + (g + 1) * HEAD_DIM)
                for sl, grad_t in ((ks, dk_t), (vs, dv_t)):
                    dmeta_ref[:, sl] += grad_t[:, 0:BLOCK].T
                    dkv_ref[:, sl] = hold_ref[:, sl] + grad_t[:, BLOCK:2 * BLOCK].T
                    hold_ref[:, sl] = grad_t[:, 2 * BLOCK:3 * BLOCK].T

        @pl.when(i == nb)
        def _():
            dkv_ref[...] = hold_ref[...]
            for cp in _scatter(src, dst, send_sems, recv_sems):
                cp.wait()

    def cur(i):
        return jnp.minimum(i, nb - 1)

    return pl.pallas_call(
        body, name="attn_bwd", grid=(nb + 1,),
        in_specs=[pl.BlockSpec(memory_space=pltpu.SMEM),
                  pl.BlockSpec((BLOCK, ATTN_W), lambda i: (cur(i), 0)),
                  pl.BlockSpec((BLOCK, 2 * KV_W), lambda i: (cur(i), 0)),
                  pl.BlockSpec((BLOCK, 2 * KV_W), lambda i: (jnp.maximum(cur(i) - 1, 0), 0)),
                  _const((BLOCK, 2 * KV_W)),
                  pl.BlockSpec((BLOCK, ATTN_W), lambda i: (cur(i), 0)),
                  pl.BlockSpec((BLOCK, ATTN_W), lambda i: (cur(i), 0)),
                  pl.BlockSpec((BLOCK, BLOCK), lambda i: (cur(i), 0))] + [ANY] * n,
        out_specs=[pl.BlockSpec((BLOCK, ATTN_W), lambda i: (cur(i), 0)),
                   pl.BlockSpec((BLOCK, 2 * KV_W), lambda i: (jnp.maximum(i - 1, 0), 0)),
                   _const((BLOCK, 2 * KV_W)), _const((N_HEADS, BLOCK))] + [ANY] * n,
        out_shape=[jax.ShapeDtypeStruct((r, ATTN_W), BF16), jax.ShapeDtypeStruct((r, 2 * KV_W), F32),
                   jax.ShapeDtypeStruct((BLOCK, 2 * KV_W), F32), jax.ShapeDtypeStruct((N_HEADS, BLOCK), F32)] + _scatter_landing(parts),
        scratch_shapes=[pltpu.VMEM((BLOCK, 2 * KV_W), F32), pltpu.VMEM((N_KV, GROUP * BLOCK, 2 * BLOCK), F32),
                        pltpu.VMEM((GROUP * BLOCK, BLOCK), F32), pltpu.SemaphoreType.DMA((7 * n,)), pltpu.SemaphoreType.DMA((7 * n,))],
        compiler_params=_cparams(),
    )(sinks, q, kv, kv, kv, o, do, lse, *parts)


def _in_proj_bwd(dq, dkv, dkv_meta, dc, dh1, h0, g1, w_in_t, seq, tm):
    r = h0.shape[0]
    n_tiles = r // tm
    n_out = -(-seq // tm)

    def body(dq_ref, dkv_ref, dm_ref, dc_ref, dh1_ref, h_ref, g_ref, w_ref, gx_ref, lead_ref, dwt_ref, dg_ref, dw_ref, hold_ref):
        i = pl.program_id(0)

        @pl.when(i == 0)
        def _():
            dw_ref[...] = jnp.zeros_like(dw_ref)
            dg_ref[...] = jnp.zeros_like(dg_ref)

        @pl.when(i < n_tiles)
        def _():
            meta = jnp.concatenate([dm_ref[...], jnp.zeros((tm - BLOCK, 2 * KV_W), F32)], axis=0) if tm > BLOCK else dm_ref[...]
            first = (jnp.zeros((tm, 2 * KV_W), jnp.int32) + i) == 0
            dkvb = (dkv_ref[...] + jnp.where(first, meta, 0.0)).astype(BF16)
            hn, xhat, rstd = _rms_fwd(h_ref[...], g_ref[...])
            dproj = jnp.concatenate([dq_ref[...], dkvb, dc_ref[...]], axis=1)
            dhn = _dot(dproj, w_ref[...])
            dw_ref[...] += _dot_tn(dproj, hn.astype(BF16))
            dx, dg_rows = _rms_bwd(dhn, xhat, rstd, g_ref[...])
            dg_ref[...] += jnp.sum(dg_rows, axis=0, keepdims=True)
            dh0 = dh1_ref[...] + dx

            @pl.when(i == 0)
            def _():
                lead_ref[...] = dh0[0:BLOCK]

            @pl.when((i >= 1) & (i <= n_out))
            def _():
                gx_ref[0:tm - BLOCK, :] = hold_ref[...]
                gx_ref[tm - BLOCK:tm, :] = dh0[0:BLOCK]

            hold_ref[...] = dh0[BLOCK:tm]

        @pl.when((i == n_tiles) & (n_tiles <= n_out))
        def _():
            gx_ref[0:tm - BLOCK, :] = hold_ref[...]

        @pl.when(i == n_tiles - 1)
        def _():
            dwt_ref[...] = dw_ref[...].astype(BF16)

    def tile(n):
        return pl.BlockSpec((tm, n), lambda i: (jnp.minimum(i, n_tiles - 1), 0))

    return pl.pallas_call(
        body, name="in_proj_bwd", grid=(n_tiles + 1,),
        in_specs=[tile(ATTN_W), tile(2 * KV_W), _const((BLOCK, 2 * KV_W)), tile(2 * CONV_W), tile(D_MODEL), tile(D_MODEL),
                  _const((1, D_MODEL)), _const((IN_COLS, D_MODEL))],
        out_specs=[pl.BlockSpec((tm, D_MODEL), lambda i: (jnp.clip(i - 1, 0, n_out - 1), 0)), _const((BLOCK, D_MODEL)),
                   _const((IN_COLS, D_MODEL)), _const((1, D_MODEL))],
        out_shape=[jax.ShapeDtypeStruct((seq, D_MODEL), F32), jax.ShapeDtypeStruct((BLOCK, D_MODEL), F32),
                   jax.ShapeDtypeStruct((IN_COLS, D_MODEL), BF16), jax.ShapeDtypeStruct((1, D_MODEL), F32)],
        scratch_shapes=[pltpu.VMEM((IN_COLS, D_MODEL), F32), pltpu.VMEM((tm - BLOCK, D_MODEL), F32)],
        compiler_params=_cparams(),
    )(dq, dkv, dkv_meta, dc, dh1, h0, g1, w_in_t)


def _adamw_update(w_ref, g_ref, m_ref, v_ref, d_ref, nm_ref, nv_ref):
    g = g_ref[...]
    m = ADAM_B1 * m_ref[...] + (1.0 - ADAM_B1) * g
    v = ADAM_B2 * v_ref[...] + (1.0 - ADAM_B2) * (g * g)
    m_hat = m / (1.0 - ADAM_B1 ** ADAM_STEP)
    v_hat = v / (1.0 - ADAM_B2 ** ADAM_STEP)
    d_ref[...] = -ADAM_LR * (m_hat / (jnp.sqrt(v_hat) + ADAM_EPS) + ADAM_WD * w_ref[...])
    nm_ref[...] = m
    nv_ref[...] = v


def _adamw(w, g, m, v, name):
    rows, cols = w.shape
    tr = rows
    for cand in (256, 176, 128, 64, 32, 16, 8):
        if rows % cand == 0:
            tr = cand
            break

    def body(*refs):
        _adamw_update(*refs)

    spec = _row(tr, cols)
    return pl.pallas_call(
        body, name=name, grid=(rows // tr,), in_specs=[spec] * 4, out_specs=[spec] * 3,
        out_shape=[jax.ShapeDtypeStruct((rows, cols), F32)] * 3, compiler_params=_cparams(),
    )(w, g, m, v)


MESH = pl.DeviceIdType.MESH
ANY = pl.BlockSpec(memory_space=pl.ANY)


def _place():
    x, y, c = lax.axis_index("x"), lax.axis_index("y"), lax.axis_index("c")
    chips = [(1 - x, y), (x, 1 - y), (1 - x, 1 - y)]
    return x, y, c, chips


def _gather_ici(src, dst, send_sems, recv_sems):
    x, y, c, chips = _place()
    sends, arrivals = [], []
    for k in range(len(src)):
        rows = src[k].shape[0] // 2
        half = pl.ds(c * rows, rows)
        for p, chip in enumerate(chips):
            sems = dict(send_sem=send_sems.at[3 * k + p], recv_sem=recv_sems.at[3 * k + p], device_id=(chip[0], chip[1], c),
                        device_id_type=MESH)
            sends.append(pltpu.make_async_remote_copy(src_ref=src[k].at[half], dst_ref=dst[k].at[2 * x + y, half], **sems))
            theirs = dst[k].at[2 * chip[0] + chip[1], half]
            arrivals.append(pltpu.make_async_remote_copy(src_ref=theirs, dst_ref=theirs, **sems))
    return sends, arrivals


def _gather_d2d(dst, send_sems, recv_sems):
    x, y, c, chips = _place()
    sends, arrivals = [], []
    for k in range(len(dst)):
        rows = dst[k].shape[1] // 2
        for p, chip in enumerate(chips):
            sems = dict(send_sem=send_sems.at[3 * k + p], recv_sem=recv_sems.at[3 * k + p], device_id=(x, y, 1 - c),
                        device_id_type=MESH)
            mine = dst[k].at[2 * chip[0] + chip[1], pl.ds(c * rows, rows)]
            sends.append(pltpu.make_async_remote_copy(src_ref=mine, dst_ref=mine, **sems))
            theirs = dst[k].at[2 * chip[0] + chip[1], pl.ds((1 - c) * rows, rows)]
            arrivals.append(pltpu.make_async_remote_copy(src_ref=theirs, dst_ref=theirs, **sems))
    return sends, arrivals


def _own_slots(shard):
    return jnp.broadcast_to(shard[None], (N_SHARD,) + shard.shape)


def _gather_weights(shards):
    n = len(shards)

    def body(*refs):
        src, dst = refs[:n], refs[2 * n:3 * n]
        ici_send, ici_recv, d2d_send, d2d_recv = refs[3 * n:]
        sends, arrivals = _gather_ici(src, dst, ici_send, ici_recv)
        for cp in sends:
            cp.start()
        for cp in arrivals:
            cp.wait_recv()
        forwards, from_sibling = _gather_d2d(dst, d2d_send, d2d_recv)
        for cp in forwards:
            cp.start()
        for cp in from_sibling:
            cp.wait_recv()
        for cp in sends + forwards:
            cp.wait_send()

    return pl.pallas_call(
        body, name="gather_weights",
        in_specs=[ANY] * (2 * n), out_specs=[ANY] * n,
        out_shape=[jax.ShapeDtypeStruct((N_SHARD,) + s.shape, s.dtype) for s in shards],
        input_output_aliases={n + k: k for k in range(n)},
        scratch_shapes=[pltpu.SemaphoreType.DMA((3 * n,))] * 4,
    )(*shards, *[_own_slots(s) for s in shards])


VMEM_WHOLE = pl.BlockSpec(memory_space=pltpu.VMEM)


def _allreduce_small(parts, grads):
    widths = sorted({p.shape[1] for p in parts})
    place, heights = [], [0] * len(widths)
    for p in parts:
        gi = widths.index(p.shape[1])
        place.append((gi, heights[gi]))
        heights[gi] += -(-p.shape[0] // 8) * 8
    n, ng, nb = len(parts), len(widths), len(grads)

    def body(*refs):
        ins, big_src = refs[:n], refs[n:n + nb]
        outs, big_dst = refs[n + nb:2 * n + nb], refs[2 * n + nb:2 * (n + nb)]
        slots = refs[2 * (n + nb):2 * (n + nb) + ng]
        send_sems, recv_sems, big_send, big_recv = refs[2 * (n + nb) + ng:]
        scattered = _scatter(big_src, big_dst, big_send, big_recv)
        for cp in scattered:
            cp.start()
        x, y, c = lax.axis_index("x"), lax.axis_index("y"), lax.axis_index("c")
        me = 4 * x + 2 * y + c
        for gi in range(ng):
            slots[gi][me] = jnp.zeros((heights[gi], widths[gi]), F32)
        for k, (gi, r0) in enumerate(place):
            slots[gi][me, r0:r0 + parts[k].shape[0], :] = ins[k][...]

        def copy(gi, j, arriving):
            peer = ((x + (j >> 2)) % 2, (y + ((j >> 1) & 1)) % 2, (c + (j & 1)) % 2)
            slot = 4 * peer[0] + 2 * peer[1] + peer[2] if arriving else me
            return pltpu.make_async_remote_copy(
                src_ref=slots[gi].at[me], dst_ref=slots[gi].at[slot], send_sem=send_sems.at[7 * gi + j - 1],
                recv_sem=recv_sems.at[7 * gi + j - 1], device_id=peer, device_id_type=MESH)

        pairs = [(gi, j) for gi in range(ng) for j in range(1, N_DEV)]
        for gi, j in pairs:
            copy(gi, j, False).start()
        for gi, j in pairs:
            copy(gi, j, True).wait_recv()
        totals = []
        for gi in range(ng):
            total = slots[gi][0]
            for d in range(1, N_DEV):
                total = total + slots[gi][d]
            totals.append(total)
        for k, (gi, r0) in enumerate(place):
            outs[k][...] = totals[gi][r0:r0 + parts[k].shape[0], :]
        for gi, j in pairs:
            copy(gi, j, False).wait_send()
        for cp in scattered:
            cp.wait()

    out = pl.pallas_call(
        body, name="allreduce_small", in_specs=[VMEM_WHOLE] * n + [ANY] * nb, out_specs=[VMEM_WHOLE] * n + [ANY] * nb,
        out_shape=[jax.ShapeDtypeStruct(p.shape, F32) for p in parts] + _scatter_landing(grads),
        scratch_shapes=[pltpu.VMEM((N_DEV, heights[gi], widths[gi]), F32) for gi in range(ng)]
        + [pltpu.SemaphoreType.DMA((7 * ng,)), pltpu.SemaphoreType.DMA((7 * ng,)),
           pltpu.SemaphoreType.DMA((7 * nb,)), pltpu.SemaphoreType.DMA((7 * nb,))],
    )(*parts, *grads)
    return out[:n], out[n:]


def _adamw_small(ws, gs, ms, vs):
    n = len(ws)

    def body(*refs):
        for k in range(n):
            w_ref, g_ref, m_ref, v_ref = (refs[j * n + k] for j in range(4))
            _adamw_update(w_ref, g_ref, m_ref, v_ref, *(refs[(4 + j) * n + k] for j in range(3)))

    shapes = [jax.ShapeDtypeStruct(w.shape, F32) for w in ws]
    out = pl.pallas_call(
        body, name="adamw_small", in_specs=[VMEM_WHOLE] * (4 * n), out_specs=[VMEM_WHOLE] * (3 * n), out_shape=shapes * 3,
    )(*ws, *gs, *ms, *vs)
    return out[:n], out[n:2 * n], out[2 * n:]


def _scatter(src, dst, send_sems, recv_sems):
    x, y, c = lax.axis_index("x"), lax.axis_index("y"), lax.axis_index("c")
    copies = []
    for k in range(len(src)):
        rows = src[k].shape[1] // 2
        for j in range(1, N_DEV):
            px, py, pc = (x + (j >> 2)) % 2, (y + ((j >> 1) & 1)) % 2, (c + (j & 1)) % 2
            copies.append(pltpu.make_async_remote_copy(
                src_ref=src[k].at[2 * px + py, pl.ds(pc * rows, rows)], dst_ref=dst[k].at[j - 1],
                send_sem=send_sems.at[7 * k + j - 1], recv_sem=recv_sems.at[7 * k + j - 1], device_id=(px, py, pc),
                device_id_type=MESH))
    return copies


def _scatter_landing(parts):
    return [jax.ShapeDtypeStruct((N_DEV - 1, p.shape[1] // 2, p.shape[2]), p.dtype) for p in parts]


def _sum_pieces(own, landed):
    n = len(own)

    def body(*refs):
        for k in range(n):
            got = refs[n + k]
            total = refs[k][...].astype(F32)
            for j in range(N_DEV - 1):
                total = total + got[j].astype(F32)
            refs[2 * n + k][...] = total

    in_specs, out_specs = [], []
    for o in own:
        in_specs.append(_row(o.shape[0] // 2, o.shape[1]))
    for o in own:
        in_specs.append(pl.BlockSpec((N_DEV - 1, o.shape[0] // 2, o.shape[1]), lambda i: (0, i, 0)))
        out_specs.append(_row(o.shape[0] // 2, o.shape[1]))
    return pl.pallas_call(
        body, name="sum_pieces", grid=(2,), in_specs=in_specs, out_specs=out_specs,
        out_shape=[jax.ShapeDtypeStruct(o.shape, F32) for o in own], compiler_params=_cparams(),
    )(*own, *landed)


def _swap_with_sibling(halves):
    n = len(halves)

    def body(*refs):
        x, y, c = lax.axis_index("x"), lax.axis_index("y"), lax.axis_index("c")
        copies = [pltpu.make_async_remote_copy(
            src_ref=refs[k], dst_ref=refs[n + k], send_sem=refs[2 * n].at[k], recv_sem=refs[2 * n + 1].at[k],
            device_id=(x, y, 1 - c), device_id_type=MESH) for k in range(n)]
        for cp in copies:
            cp.start()
        for cp in copies:
            cp.wait()

    return pl.pallas_call(
        body, name="swap_with_sibling", in_specs=[ANY] * n, out_specs=[ANY] * n,
        out_shape=[jax.ShapeDtypeStruct(h.shape, h.dtype) for h in halves],
        scratch_shapes=[pltpu.SemaphoreType.DMA((n,)), pltpu.SemaphoreType.DMA((n,))],
    )(*halves)


def _own_piece(part):
    rows = part.shape[1] // 2
    s = 2 * lax.axis_index("x") + lax.axis_index("y")
    return lax.dynamic_slice(part, (s, lax.axis_index("c") * rows, 0), (1, rows, part.shape[2]))[0]


def _both_halves(mine, theirs):
    south = lax.axis_index("c") == 0
    return jnp.concatenate([jnp.where(south, mine, theirs), jnp.where(south, theirs, mine)], axis=0)


def _from_col_shards(g):
    return g.transpose(1, 0, 2).reshape(g.shape[1], -1)


def kernel(x, meta_tokens, attn_norm_g, w_in, attn_sinks, conv_w, conv_b, conv_ln_g, conv_ln_b, attn_out_g, conv_out_g, w_out, ffn_norm_g, w_gate, w_up, w_down, final_norm_g, loss_target, m_meta_tokens, m_attn_norm_g, m_w_in, m_attn_sinks, m_conv_w, m_conv_b, m_conv_ln_g, m_conv_ln_b, m_attn_out_g, m_conv_out_g, m_w_out, m_ffn_norm_g, m_w_gate, m_w_up, m_w_down, m_final_norm_g, v_meta_tokens, v_attn_norm_g, v_w_in, v_attn_sinks, v_conv_w, v_conv_b, v_conv_ln_g, v_conv_ln_b, v_attn_out_g, v_conv_out_g, v_w_out, v_ffn_norm_g, v_w_gate, v_w_up, v_w_down, v_final_norm_g):
    seq = x.shape[1]
    r = -(-(seq + BLOCK) // ROW_QUANTUM) * ROW_QUANTUM
    tail = r - BLOCK - seq
    tm_wide = 768 if seq >= 768 else 256
    shard = 2 * lax.axis_index("x") + lax.axis_index("y")

    conv_w32 = jnp.pad(conv_w[0], ((0, 1), (0, 0)))
    small_shard = jnp.concatenate([meta_tokens, conv_w32.reshape(16, 256)], axis=0)
    g_in, g_small = _gather_weights([w_in[0].T.astype(BF16), small_shard])
    later = [w_gate[0].T.astype(BF16), w_up[0].T.astype(BF16), w_out[0].astype(BF16), w_down[0].astype(BF16)]
    w_in_t = g_in.reshape(IN_COLS, D_MODEL)
    meta_full = _from_col_shards(g_small[:, 0:N_META])
    cw_full = _from_col_shards(g_small[:, N_META:].reshape(N_SHARD, 32, 128))

    g1, ga, gc, g2 = attn_norm_g, attn_out_g, conv_out_g, ffn_norm_g
    gf = final_norm_g.reshape(1, D_MODEL)
    sinks = attn_sinks[0]

    h0 = jnp.concatenate([jnp.zeros((LEAD, D_MODEL), F32), meta_full, x[0], jnp.zeros((tail, D_MODEL), F32)], axis=0)
    q, kv, cacg = _in_proj(h0, g1, w_in_t, 768)
    oa, lse, *gathered = _attn_fwd(q, kv, sinks, later, [_own_slots(s) for s in later])
    oc, yc, g_gate, g_up, g_out, g_down = _conv_fwd(cacg, cw_full, conv_b, conv_ln_g, conv_ln_b, 384, gathered)
    wg_t, wu_t, wd_b = g_gate.reshape(D_FF, D_MODEL), g_up.reshape(D_FF, D_MODEL), g_down.reshape(D_FF, D_MODEL)
    w_out_b = g_out.reshape(D_MODEL, D_MODEL)
    h1, hn2 = _out_proj(oa, oc, h0, ga, gc, g2, w_out_b, 768)
    gate, up, act, dh2, dh2b, loss_p, dgf = _ffn_fwd(hn2, h1, loss_target[0], gf, wg_t, wu_t, wd_b, 384)

    def by_shard(dw):
        return dw.reshape(N_SHARD, dw.shape[0] // N_SHARD, D_MODEL)

    dgate, dup, dh1, dg2 = _ffn_bwd(dh2, dh2b, gate, up, h1, g2, wg_t, wu_t, wd_b, 384)
    p_gate, p_up = [by_shard(dw) for dw in _ffn_wgrad_gu(hn2, dgate, dup, 768)]
    p_down = by_shard(_ffn_wgrad_d(act, dh2b, 768))
    doa, doc, dwo, dga, dgc = _out_proj_bwd(dh1, oa, oc, ga, gc, w_out_b, 768)
    p_out = by_shard(dwo)
    dy, dcw, dcb, dlg, dlb, l_gate, l_up = _conv_bwd_params(doc, yc, cacg, conv_ln_g, conv_ln_b, 384, [p_gate, p_up])
    dc = _conv_bwd_data(dy, cacg, cw_full, 384)
    dq, dkv, dkv_meta, dsink, l_out, l_down = _attn_bwd(q, kv, oa, doa, lse, sinks, [p_out, p_down])
    grad_x, dlead, dwi_t, dg1 = _in_proj_bwd(dq, dkv, dkv_meta, dc, dh1, h0, g1, w_in_t, seq, tm_wide)
    p_in = by_shard(dwi_t)

    red_names = ("final_norm_g", "attn_norm_g", "ffn_norm_g", "meta_tokens", "attn_out_g", "conv_out_g", "conv_b", "conv_ln_g",
                 "conv_ln_b", "conv_w", "loss", "attn_sinks")
    sums, (l_in,) = _allreduce_small([dgf, dg1, dg2, dlead[LEAD:BLOCK], dga, dgc, dcb, dlg, dlb, dcw, loss_p, dsink], [p_in])
    red = dict(zip(red_names, sums))
    loss = red["loss"][0, 0]
    red["final_norm_g"] = red["final_norm_g"].reshape(D_MODEL)
    red["attn_sinks"] = red["attn_sinks"][:, 0].reshape(1, N_HEADS)
    g_meta = lax.dynamic_slice_in_dim(red["meta_tokens"], shard * (D_MODEL // N_SHARD), D_MODEL // N_SHARD, axis=1)
    g_convw = lax.dynamic_slice_in_dim(red["conv_w"][0:CONV_K], shard * (CONV_W // N_SHARD), CONV_W // N_SHARD, axis=1)[None]

    halves = _sum_pieces([_own_piece(p) for p in (p_in, p_gate, p_up, p_out, p_down)], [l_in, l_gate, l_up, l_out, l_down])
    big = ("w_in", "w_gate", "w_up", "w_out", "w_down")
    transposed = ("w_in", "w_gate", "w_up")
    g_big = {name: _both_halves(mine, theirs) for name, mine, theirs in zip(big, halves, _swap_with_sibling(halves))}

    grads = {
        "meta_tokens": g_meta, "attn_norm_g": red["attn_norm_g"], "attn_sinks": red["attn_sinks"],
        "conv_w": g_convw, "conv_b": red["conv_b"], "conv_ln_g": red["conv_ln_g"], "conv_ln_b": red["conv_ln_b"],
        "attn_out_g": red["attn_out_g"], "conv_out_g": red["conv_out_g"], "ffn_norm_g": red["ffn_norm_g"],
        "final_norm_g": red["final_norm_g"]}
    params = {
        "meta_tokens": (meta_tokens, m_meta_tokens, v_meta_tokens), "attn_norm_g": (attn_norm_g, m_attn_norm_g, v_attn_norm_g),
        "w_in": (w_in, m_w_in, v_w_in), "attn_sinks": (attn_sinks, m_attn_sinks, v_attn_sinks), "conv_w": (conv_w, m_conv_w, v_conv_w),
        "conv_b": (conv_b, m_conv_b, v_conv_b), "conv_ln_g": (conv_ln_g, m_conv_ln_g, v_conv_ln_g),
        "conv_ln_b": (conv_ln_b, m_conv_ln_b, v_conv_ln_b), "attn_out_g": (attn_out_g, m_attn_out_g, v_attn_out_g),
        "conv_out_g": (conv_out_g, m_conv_out_g, v_conv_out_g), "w_out": (w_out, m_w_out, v_w_out),
        "ffn_norm_g": (ffn_norm_g, m_ffn_norm_g, v_ffn_norm_g), "w_gate": (w_gate, m_w_gate, v_w_gate), "w_up": (w_up, m_w_up, v_w_up),
        "w_down": (w_down, m_w_down, v_w_down), "final_norm_g": (final_norm_g, m_final_norm_g, v_final_norm_g)}
    names = list(params)
    delta, new_m, new_v = {}, {}, {}
    for name in big:
        flip = (lambda a: a.T) if name in transposed else (lambda a: a)
        w, m, v = params[name]
        outs = _adamw(flip(w[0]), g_big[name], flip(m[0]), flip(v[0]), "adamw_" + name)
        grads[name], delta[name], new_m[name], new_v[name] = [flip(a)[None] for a in (g_big[name], *outs)]
    rest = [name for name in names if name not in big]

    def rows_of(a):
        return a.reshape(-1, a.shape[-1])

    small = _adamw_small([rows_of(params[n][0]) for n in rest], [rows_of(grads[n]) for n in rest],
                         [rows_of(params[n][1]) for n in rest], [rows_of(params[n][2]) for n in rest])
    for dst, outs in zip((delta, new_m, new_v), small):
        for name, out in zip(rest, outs):
            dst[name] = out.reshape(params[name][0].shape)

    return (loss, grad_x[None], *[grads[n] for n in names], *[delta[n] for n in names], *[new_m[n] for n in names],
            *[new_v[n] for n in names])
```

```python
import functools
import math

import jax
import jax.numpy as jnp
from jax import lax
from jax.experimental import pallas as pl
from jax.experimental.pallas import tpu as pltpu

F32 = jnp.float32
BF16 = jnp.bfloat16

D_MODEL = 1024
N_META = 16
ATTN_W = 512
CONV_W = 512
HEAD_DIM = 64
N_HEADS = 8
N_KV = 2
GROUP = N_HEADS // N_KV
KV_W = N_KV * HEAD_DIM
BLOCK = 128
LEAD = BLOCK - N_META
CONV_K = 31
D_FF = 2816
IN_COLS = ATTN_W + 2 * KV_W + 2 * CONV_W
Q0, KV0, C0 = 0, ATTN_W, ATTN_W + 2 * KV_W
NORM_EPS = 1e-5
SCALE = 1.0 / math.sqrt(HEAD_DIM)
SLOPES = tuple(2.0 ** (-(8.0 / N_HEADS) * (h + 1)) for h in range(N_HEADS))
NEG = -1e30

ADAM_LR, ADAM_B1, ADAM_B2, ADAM_EPS, ADAM_WD, ADAM_STEP = 0.001, 0.9, 0.999, 1e-08, 0.01, 10

N_SHARD = 4
N_DEV = 8
ROW_QUANTUM = 768
HALO = 32
CONV_CHUNK = 32
FF_CHUNK = 256
FF_CHUNKS = tuple(slice(c, c + FF_CHUNK) for c in range(0, D_FF, FF_CHUNK))
VMEM_LIMIT = 60 * 1024 * 1024


def _cparams(n_axes=1):
    return pltpu.CompilerParams(dimension_semantics=("arbitrary",) * n_axes, vmem_limit_bytes=VMEM_LIMIT)


def _dot(a, b):
    return jnp.dot(a, b, preferred_element_type=F32)


def _dot_nt(a, b):
    return lax.dot_general(a, b, (((1,), (1,)), ((), ())), preferred_element_type=F32)


def _dot_tn(a, b):
    return lax.dot_general(a, b, (((0,), (0,)), ((), ())), preferred_element_type=F32)


def _sigmoid(x):
    return 1.0 / (1.0 + jnp.exp(-x))


def _row(tm, n):
    return pl.BlockSpec((tm, n), lambda i: (i, 0))


def _const(shape):
    return pl.BlockSpec(shape, lambda i: (0,) * len(shape))


def _resident(shape):
    return pl.BlockSpec(shape, lambda i: (0,) * len(shape), pipeline_mode=pl.Buffered(1))


def _rms_fwd(x, g):
    rstd = lax.rsqrt(jnp.mean(x * x, axis=-1, keepdims=True) + NORM_EPS)
    xhat = x * rstd
    return xhat * g, xhat, rstd


def _rms_bwd(dy, xhat, rstd, g):
    dxh = dy * g
    dx = rstd * (dxh - xhat * jnp.mean(dxh * xhat, axis=-1, keepdims=True))
    return dx, dy * xhat


def _in_proj(x, lead, g1, w_in_t, r, tm):
    seq = x.shape[0]
    n_sub = tm // BLOCK

    def body(*refs):
        x_refs = refs[:n_sub]
        lead_ref, g_ref, w_ref, h0_ref, q_ref, kv_ref, c_ref = refs[n_sub:]
        i = pl.program_id(0)
        pieces = []
        for k, x_ref in enumerate(x_refs):
            at = jnp.zeros((BLOCK, D_MODEL), jnp.int32) + (i * tm + (k - 1) * BLOCK)
            piece = jnp.where((at >= 0) & (at < seq), x_ref[...], 0.0)
            pieces.append(jnp.where(at < 0, lead_ref[...], piece) if k == 0 else piece)
        h = jnp.concatenate(pieces, axis=0)
        h0_ref[...] = h
        hn = _rms_fwd(h, g_ref[...])[0].astype(BF16)
        q_ref[...] = _dot_nt(hn, w_ref[Q0:KV0, :]).astype(BF16)
        kv_ref[...] = _dot_nt(hn, w_ref[KV0:C0, :]).astype(BF16)
        c_ref[...] = _dot_nt(hn, w_ref[C0:IN_COLS, :])

    def x_block(k):
        return pl.BlockSpec((BLOCK, D_MODEL), lambda i: (jnp.clip(n_sub * i - 1 + k, 0, seq // BLOCK - 1), 0))

    return pl.pallas_call(
        body, name="in_proj", grid=(r // tm,),
        in_specs=[x_block(k) for k in range(n_sub)] + [_const((BLOCK, D_MODEL)), _const((1, D_MODEL)), _const((IN_COLS, D_MODEL))],
        out_specs=[_row(tm, D_MODEL), _row(tm, ATTN_W), _row(tm, 2 * KV_W), _row(tm, 2 * CONV_W)],
        out_shape=[jax.ShapeDtypeStruct((r, D_MODEL), F32), jax.ShapeDtypeStruct((r, ATTN_W), BF16),
                   jax.ShapeDtypeStruct((r, 2 * KV_W), BF16), jax.ShapeDtypeStruct((r, 2 * CONV_W), F32)],
        compiler_params=_cparams(),
    )(*[x] * n_sub, lead, g1, w_in_t)


def _attn_bias_init(bias_ref, late_ref):
    row = lax.broadcasted_iota(jnp.int32, (GROUP * BLOCK, BLOCK), 0) & (BLOCK - 1)
    col = lax.broadcasted_iota(jnp.int32, (GROUP * BLOCK, BLOCK), 1)
    late_ref[...] = jnp.where(col > row, 1.0, 0.0)
    for g in range(N_KV):
        slope = jnp.concatenate([jnp.zeros((BLOCK, BLOCK), F32) + SLOPES[g * GROUP + j] for j in range(GROUP)], axis=0)
        bias_ref[g, :, 0:BLOCK] = jnp.where(col >= LEAD, 0.0, NEG)
        bias_ref[g, :, BLOCK:2 * BLOCK] = -slope * jnp.where(col > row, row - col + BLOCK, row - col).astype(F32)


def _attn_block_bias(late_ref, i):
    late = late_ref[...]
    meta0 = jnp.where(i == 0, NEG, 0.0)
    no_prev = jnp.where(i >= 2, 0.0, NEG)
    no_cur = jnp.where(i >= 1, 0.0, NEG)
    return late * meta0, late * no_prev + no_cur


def _attn_logits(s3, bias_ref, block_bias, prev_part, g):
    meta = s3[:, 0:BLOCK] * SCALE + (bias_ref[g, :, 0:BLOCK] + block_bias[0])
    band = jnp.where(prev_part, s3[:, BLOCK:2 * BLOCK], s3[:, 2 * BLOCK:3 * BLOCK]) * SCALE + (bias_ref[g, :, BLOCK:2 * BLOCK] + block_bias[1])
    return meta, band


def _split_band(meta, band, prev_part):
    return jnp.concatenate([meta, jnp.where(prev_part, band, 0.0), jnp.where(prev_part, 0.0, band)], axis=1)


def _head_rows(vals):
    return jnp.concatenate([jnp.zeros((BLOCK, BLOCK), F32) + v for v in vals], axis=0)


def _stack_heads(ref, g):
    return jnp.concatenate([ref[:, (g * GROUP + j) * HEAD_DIM:(g * GROUP + j + 1) * HEAD_DIM] for j in range(GROUP)], axis=0)


def _kv_cat(kvm_ref, kvp_ref, kvc_ref, g):
    ks = slice(g * HEAD_DIM, (g + 1) * HEAD_DIM)
    vs = slice(KV_W + g * HEAD_DIM, KV_W + (g + 1) * HEAD_DIM)
    kcat = jnp.concatenate([kvm_ref[:, ks], kvp_ref[:, ks], kvc_ref[:, ks]], axis=0)
    vcat = jnp.concatenate([kvm_ref[:, vs], kvp_ref[:, vs], kvc_ref[:, vs]], axis=0)
    return kcat, vcat


def _attn_fwd(q, kv, sinks, shards, gathered):
    r = q.shape[0]
    nb = r // BLOCK
    n = len(shards)

    def body(sink_ref, q_ref, kvc_ref, kvp_ref, kvm_ref, *rest):
        src = rest[:n]
        o_ref, lse_ref = rest[2 * n:2 * n + 2]
        dst = rest[2 * n + 2:3 * n + 2]
        bias_ref, late_ref, send_sems, recv_sems = rest[3 * n + 2:]
        i = pl.program_id(0)

        @pl.when(i == 0)
        def _():
            for cp in _gather_ici(src, dst, send_sems, recv_sems)[0]:
                cp.start()
            _attn_bias_init(bias_ref, late_ref)

        lane = lax.broadcasted_iota(jnp.int32, (BLOCK, BLOCK), 1)
        lse_tile = jnp.zeros((BLOCK, BLOCK), F32)
        block_bias = _attn_block_bias(late_ref, i)
        prev_part = late_ref[...] > 0.5
        for g in range(N_KV):
            kcat, vcat = _kv_cat(kvm_ref, kvp_ref, kvc_ref, g)
            heads = range(g * GROUP, (g + 1) * GROUP)
            meta, band = _attn_logits(_dot_nt(_stack_heads(q_ref, g), kcat), bias_ref, block_bias, prev_part, g)
            sink = _head_rows([sink_ref[h] for h in heads])
            m = jnp.maximum(jnp.max(jnp.maximum(meta, band), axis=-1, keepdims=True), sink)
            p_meta, p_band = jnp.exp(meta - m), jnp.exp(band - m)
            l = jnp.sum(p_meta + p_band, axis=-1, keepdims=True) + jnp.exp(sink - m)
            o = _dot(_split_band(p_meta, p_band, prev_part).astype(BF16), vcat) * (1.0 / l)[:, 0:HEAD_DIM]
            lse = m + jnp.log(l)
            for j, h in enumerate(heads):
                o_ref[:, h * HEAD_DIM:(h + 1) * HEAD_DIM] = o[j * BLOCK:(j + 1) * BLOCK]
                lse_tile = jnp.where(lane == h, lse[j * BLOCK:(j + 1) * BLOCK], lse_tile)
        lse_ref[...] = lse_tile

        @pl.when(i == nb - 1)
        def _():
            sends, arrivals = _gather_ici(src, dst, send_sems, recv_sems)
            for cp in arrivals:
                cp.wait_recv()
            for cp in sends:
                cp.wait_send()

    return pl.pallas_call(
        body, name="attn_fwd", grid=(nb,),
        in_specs=[pl.BlockSpec(memory_space=pltpu.SMEM), _row(BLOCK, ATTN_W), _row(BLOCK, 2 * KV_W),
                  pl.BlockSpec((BLOCK, 2 * KV_W), lambda i: (jnp.maximum(i - 1, 0), 0)), _const((BLOCK, 2 * KV_W))] + [ANY] * (2 * n),
        out_specs=[_row(BLOCK, ATTN_W), _row(BLOCK, BLOCK)] + [ANY] * n,
        out_shape=[jax.ShapeDtypeStruct((r, ATTN_W), F32), jax.ShapeDtypeStruct((r, BLOCK), F32)]
        + [jax.ShapeDtypeStruct(g.shape, g.dtype) for g in gathered],
        input_output_aliases={5 + n + k: 2 + k for k in range(n)},
        scratch_shapes=[pltpu.VMEM((N_KV, GROUP * BLOCK, 2 * BLOCK), F32), pltpu.VMEM((GROUP * BLOCK, BLOCK), F32),
                        pltpu.SemaphoreType.DMA((3 * n,)), pltpu.SemaphoreType.DMA((3 * n,))],
        compiler_params=_cparams(),
    )(sinks, q, kv, kv, kv, *shards, *gathered)


def _shifted_copies(ub_ref, win):
    w = win.shape[0]
    ub_ref[0] = win
    for b in range(1, 8):
        ub_ref[b] = pltpu.roll(win, shift=w - b, axis=0)


def _conv_chunk(ub_ref, w_ref, r0, shifts):
    acc = jnp.zeros((CONV_CHUNK, CONV_W), F32)
    for j in range(CONV_K):
        a, b = divmod(shifts[j], 8)
        acc = acc + w_ref[j:j + 1, :] * ub_ref[b, pl.ds(r0 + 8 * a, CONV_CHUNK), :]
    return acc


FWD_SHIFTS = tuple(HALO - (CONV_K - 1) + j for j in range(CONV_K))
BWD_SHIFTS = tuple(CONV_K - 1 - j for j in range(CONV_K))


def _glu_window(cp_ref, c_ref, i):
    tile = c_ref[:, 0:CONV_W] * _sigmoid(c_ref[:, CONV_W:2 * CONV_W])
    halo = cp_ref[:, 0:CONV_W] * _sigmoid(cp_ref[:, CONV_W:2 * CONV_W])
    first = (jnp.zeros((HALO, CONV_W), jnp.int32) + i) == 0
    return jnp.concatenate([jnp.where(first, 0.0, halo), tile], axis=0)


def _halo_before(tm, n):
    return pl.BlockSpec((HALO, n), lambda i: (jnp.maximum(i * (tm // HALO) - 1, 0), 0))


def _conv_fwd(cacg, cw, cb, lg, lb, tm, gathered):
    r = cacg.shape[0]
    n = len(gathered)

    def body(c_ref, cp_ref, w_ref, cb_ref, lg_ref, lb_ref, *rest):
        o_ref, y_ref = rest[n:n + 2]
        dst = rest[n + 2:2 * n + 2]
        ub_ref, send_sems, recv_sems = rest[2 * n + 2:]
        i = pl.program_id(0)

        @pl.when(i == 0)
        def _():
            for cp in _gather_d2d(dst, send_sems, recv_sems)[0]:
                cp.start()

        _shifted_copies(ub_ref, _glu_window(cp_ref, c_ref, i))

        def chunk(ci, carry):
            r0 = pl.multiple_of(ci * CONV_CHUNK, CONV_CHUNK)
            y = _conv_chunk(ub_ref, w_ref, r0, FWD_SHIFTS) + cb_ref[...]
            yc = y - jnp.mean(y, axis=-1, keepdims=True)
            rs = lax.rsqrt(jnp.mean(yc * yc, axis=-1, keepdims=True) + NORM_EPS)
            yn = yc * rs * lg_ref[...] + lb_ref[...]
            o_ref[pl.ds(r0, CONV_CHUNK), :] = yn * _sigmoid(yn)
            y_ref[pl.ds(r0, CONV_CHUNK), :] = y
            return carry

        lax.fori_loop(0, tm // CONV_CHUNK, chunk, 0, unroll=4)

        @pl.when(i == r // tm - 1)
        def _():
            sends, arrivals = _gather_d2d(dst, send_sems, recv_sems)
            for cp in arrivals:
                cp.wait_recv()
            for cp in sends:
                cp.wait_send()

    return pl.pallas_call(
        body, name="conv_fwd", grid=(r // tm,),
        in_specs=[_row(tm, 2 * CONV_W), _halo_before(tm, 2 * CONV_W), _const((32, CONV_W)), _const((1, CONV_W)),
                  _const((1, CONV_W)), _const((1, CONV_W))] + [ANY] * n,
        out_specs=[_row(tm, CONV_W), _row(tm, CONV_W)] + [ANY] * n,
        out_shape=[jax.ShapeDtypeStruct((r, CONV_W), F32)] * 2 + [jax.ShapeDtypeStruct(g.shape, g.dtype) for g in gathered],
        input_output_aliases={6 + k: 2 + k for k in range(n)},
        scratch_shapes=[pltpu.VMEM((8, tm + HALO, CONV_W), F32), pltpu.SemaphoreType.DMA((3 * n,)), pltpu.SemaphoreType.DMA((3 * n,))],
        compiler_params=_cparams(),
    )(cacg, cacg, cw, cb, lg, lb, *gathered)


def _out_proj(oa, oc, h0, ga, gc, g2, w_out_b, tm):
    r = h0.shape[0]

    def body(oa_ref, oc_ref, h_ref, ga_ref, gc_ref, g2_ref, w_ref, h1_ref, hn2_ref):
        ma = _rms_fwd(oa_ref[...], ga_ref[...])[0].astype(BF16)
        mc = _rms_fwd(oc_ref[...], gc_ref[...])[0].astype(BF16)
        h1 = h_ref[...] + _dot(jnp.concatenate([ma, mc], axis=1), w_ref[...])
        h1_ref[...] = h1
        hn2_ref[...] = _rms_fwd(h1, g2_ref[...])[0].astype(BF16)

    return pl.pallas_call(
        body, name="out_proj", grid=(r // tm,),
        in_specs=[_row(tm, ATTN_W), _row(tm, CONV_W), _row(tm, D_MODEL), _const((1, ATTN_W)), _const((1, CONV_W)),
                  _const((1, D_MODEL)), _const((D_MODEL, D_MODEL))],
        out_specs=[_row(tm, D_MODEL), _row(tm, D_MODEL)],
        out_shape=[jax.ShapeDtypeStruct((r, D_MODEL), F32), jax.ShapeDtypeStruct((r, D_MODEL), BF16)],
        compiler_params=_cparams(),
    )(oa, oc, h0, ga, gc, g2, w_out_b)


def _ffn_fwd(hn2, h1, target, gf, wg_t, wu_t, wd_b, tm):
    r = h1.shape[0]
    seq = target.shape[0]
    n_sub = tm // BLOCK

    def body(hn_ref, h1_ref, *rest):
        t_refs = rest[:n_sub]
        gf_ref, wg_ref, wu_ref, wd_ref, gate_ref, up_ref, act_ref, dh2_ref, dh2b_ref, loss_ref, dgf_ref = rest[n_sub:]
        i = pl.program_id(0)

        @pl.when(i == 0)
        def _():
            loss_ref[...] = jnp.zeros_like(loss_ref)
            dgf_ref[...] = jnp.zeros_like(dgf_ref)

        hn = hn_ref[...]
        for cs in FF_CHUNKS:
            gate = _dot_nt(hn, wg_ref[cs, :])
            up = _dot_nt(hn, wu_ref[cs, :])
            gate_ref[:, cs] = gate.astype(BF16)
            up_ref[:, cs] = up.astype(BF16)
            act_ref[:, cs] = (gate * _sigmoid(gate) * up).astype(BF16)
        y, xhat, rstd = _rms_fwd(h1_ref[...] + _dot(act_ref[...], wd_ref[...]), gf_ref[...])
        rows = lax.broadcasted_iota(jnp.int32, (tm, D_MODEL), 0) + i * tm
        real = (rows >= BLOCK) & (rows < BLOCK + seq)
        err = jnp.where(real, y - jnp.concatenate([t[...] for t in t_refs], axis=0), 0.0)
        loss_ref[...] += jnp.sum(err * err) * (0.5 / D_MODEL)
        dy = err * (1.0 / D_MODEL)
        dh2, dg_rows = _rms_bwd(dy, xhat, rstd, gf_ref[...])
        dgf_ref[...] += jnp.sum(dg_rows, axis=0, keepdims=True)
        dh2_ref[...] = dh2
        dh2b_ref[...] = dh2.astype(BF16)

    def target_block(k):
        return pl.BlockSpec((BLOCK, D_MODEL), lambda i: (jnp.clip(n_sub * i - 1 + k, 0, seq // BLOCK - 1), 0))

    return pl.pallas_call(
        body, name="ffn_fwd", grid=(r // tm,),
        in_specs=[_row(tm, D_MODEL), _row(tm, D_MODEL)] + [target_block(k) for k in range(n_sub)]
        + [_const((1, D_MODEL))] + [_resident((D_FF, D_MODEL))] * 3,
        out_specs=[_row(tm, D_FF)] * 3 + [_row(tm, D_MODEL), _row(tm, D_MODEL), _const((1, BLOCK)), _const((1, D_MODEL))],
        out_shape=[jax.ShapeDtypeStruct((r, D_FF), BF16)] * 3
        + [jax.ShapeDtypeStruct((r, D_MODEL), F32), jax.ShapeDtypeStruct((r, D_MODEL), BF16),
           jax.ShapeDtypeStruct((1, BLOCK), F32), jax.ShapeDtypeStruct((1, D_MODEL), F32)],
        compiler_params=_cparams(),
    )(hn2, h1, *[target] * n_sub, gf, wg_t, wu_t, wd_b)


def _ffn_bwd(dh2, dh2b, gate, up, h1, g2, wg_t, wu_t, wd_b, tm):
    r = h1.shape[0]

    def body(dh2_ref, dh2b_ref, gate_ref, up_ref, h1_ref, g2_ref, wg_ref, wu_ref, wd_ref, dgate_ref, dup_ref, dh1_ref, dg2_ref):
        @pl.when(pl.program_id(0) == 0)
        def _():
            dg2_ref[...] = jnp.zeros_like(dg2_ref)

        dyb = dh2b_ref[...]
        for cs in FF_CHUNKS:
            dact = _dot_nt(dyb, wd_ref[cs, :])
            gate = gate_ref[:, cs].astype(F32)
            up = up_ref[:, cs].astype(F32)
            sg = _sigmoid(gate)
            dgate_ref[:, cs] = (dact * up * (sg * (1.0 + gate * (1.0 - sg)))).astype(BF16)
            dup_ref[:, cs] = (dact * (gate * sg)).astype(BF16)
        dhn = _dot(dgate_ref[...], wg_ref[...]) + _dot(dup_ref[...], wu_ref[...])
        _, xhat, rstd = _rms_fwd(h1_ref[...], g2_ref[...])
        dx, dg_rows = _rms_bwd(dhn, xhat, rstd, g2_ref[...])
        dg2_ref[...] += jnp.sum(dg_rows, axis=0, keepdims=True)
        dh1_ref[...] = dh2_ref[...] + dx

    return pl.pallas_call(
        body, name="ffn_bwd", grid=(r // tm,),
        in_specs=[_row(tm, D_MODEL), _row(tm, D_MODEL), _row(tm, D_FF), _row(tm, D_FF), _row(tm, D_MODEL), _const((1, D_MODEL))]
        + [_resident((D_FF, D_MODEL))] * 3,
        out_specs=[_row(tm, D_FF), _row(tm, D_FF), _row(tm, D_MODEL), _const((1, D_MODEL))],
        out_shape=[jax.ShapeDtypeStruct((r, D_FF), BF16), jax.ShapeDtypeStruct((r, D_FF), BF16),
                   jax.ShapeDtypeStruct((r, D_MODEL), F32), jax.ShapeDtypeStruct((1, D_MODEL), F32)],
        compiler_params=_cparams(),
    )(dh2, dh2b, gate, up, h1, g2, wg_t, wu_t, wd_b)


FF_HALF = D_FF // 2


def _ffn_wgrad_gu(hn2, dgate, dup, tk):
    r = hn2.shape[0]
    n_k = r // tk

    def body(hn_ref, dg_ref, du_ref, wg_ref, wu_ref, accg_ref, accu_ref):
        k = pl.program_id(1)

        @pl.when(k == 0)
        def _():
            accg_ref[...] = jnp.zeros_like(accg_ref)
            accu_ref[...] = jnp.zeros_like(accu_ref)

        hn = hn_ref[...]
        accg_ref[...] += _dot_tn(dg_ref[...], hn)
        accu_ref[...] += _dot_tn(du_ref[...], hn)

        @pl.when(k == n_k - 1)
        def _():
            wg_ref[...] = accg_ref[...].astype(BF16)
            wu_ref[...] = accu_ref[...].astype(BF16)

    col = pl.BlockSpec((tk, FF_HALF), lambda j, k: (k, j))
    out = pl.BlockSpec((FF_HALF, D_MODEL), lambda j, k: (j, 0))
    return pl.pallas_call(
        body, name="ffn_wgrad_gu", grid=(2, n_k),
        in_specs=[pl.BlockSpec((tk, D_MODEL), lambda j, k: (k, 0)), col, col],
        out_specs=[out, out],
        out_shape=[jax.ShapeDtypeStruct((D_FF, D_MODEL), BF16)] * 2,
        scratch_shapes=[pltpu.VMEM((FF_HALF, D_MODEL), F32)] * 2,
        compiler_params=_cparams(2),
    )(hn2, dgate, dup)


def _ffn_wgrad_d(act, dh2b, tk):
    r = act.shape[0]
    n_k = r // tk

    def body(a_ref, dy_ref, wd_ref, acc_ref):
        k = pl.program_id(1)

        @pl.when(k == 0)
        def _():
            acc_ref[...] = jnp.zeros_like(acc_ref)

        acc_ref[...] += _dot_tn(a_ref[...], dy_ref[...])

        @pl.when(k == n_k - 1)
        def _():
            wd_ref[...] = acc_ref[...].astype(BF16)

    return pl.pallas_call(
        body, name="ffn_wgrad_d", grid=(2, n_k),
        in_specs=[pl.BlockSpec((tk, FF_HALF), lambda j, k: (k, j)), pl.BlockSpec((tk, D_MODEL), lambda j, k: (k, 0))],
        out_specs=pl.BlockSpec((FF_HALF, D_MODEL), lambda j, k: (j, 0)),
        out_shape=jax.ShapeDtypeStruct((D_FF, D_MODEL), BF16),
        scratch_shapes=[pltpu.VMEM((FF_HALF, D_MODEL), F32)],
        compiler_params=_cparams(2),
    )(act, dh2b)


def _out_proj_bwd(dh1, oa, oc, ga, gc, w_out_b, tm):
    r = dh1.shape[0]

    def body(dh_ref, oa_ref, oc_ref, ga_ref, gc_ref, w_ref, doa_ref, doc_ref, dw_ref, dga_ref, dgc_ref, acc_ref):
        i = pl.program_id(0)

        @pl.when(i == 0)
        def _():
            acc_ref[...] = jnp.zeros_like(acc_ref)
            dga_ref[...] = jnp.zeros_like(dga_ref)
            dgc_ref[...] = jnp.zeros_like(dgc_ref)

        dhb = dh_ref[...].astype(BF16)
        dmix = _dot_nt(dhb, w_ref[...])
        ma, xa, ra = _rms_fwd(oa_ref[...], ga_ref[...])
        mc, xc, rc = _rms_fwd(oc_ref[...], gc_ref[...])
        acc_ref[...] += _dot_tn(jnp.concatenate([ma.astype(BF16), mc.astype(BF16)], axis=1), dhb)

        @pl.when(i == r // tm - 1)
        def _():
            dw_ref[...] = acc_ref[...].astype(BF16)

        doa, dga_rows = _rms_bwd(dmix[:, 0:ATTN_W], xa, ra, ga_ref[...])
        doc, dgc_rows = _rms_bwd(dmix[:, ATTN_W:ATTN_W + CONV_W], xc, rc, gc_ref[...])
        doa_ref[...] = doa
        doc_ref[...] = doc
        dga_ref[...] += jnp.sum(dga_rows, axis=0, keepdims=True)
        dgc_ref[...] += jnp.sum(dgc_rows, axis=0, keepdims=True)

    return pl.pallas_call(
        body, name="out_proj_bwd", grid=(r // tm,),
        in_specs=[_row(tm, D_MODEL), _row(tm, ATTN_W), _row(tm, CONV_W), _const((1, ATTN_W)), _const((1, CONV_W)),
                  _const((D_MODEL, D_MODEL))],
        out_specs=[_row(tm, ATTN_W), _row(tm, CONV_W), _const((D_MODEL, D_MODEL)), _const((1, ATTN_W)), _const((1, CONV_W))],
        out_shape=[jax.ShapeDtypeStruct((r, ATTN_W), F32), jax.ShapeDtypeStruct((r, CONV_W), F32),
                   jax.ShapeDtypeStruct((D_MODEL, D_MODEL), BF16), jax.ShapeDtypeStruct((1, ATTN_W), F32),
                   jax.ShapeDtypeStruct((1, CONV_W), F32)],
        scratch_shapes=[pltpu.VMEM((D_MODEL, D_MODEL), F32)],
        compiler_params=_cparams(),
    )(dh1, oa, oc, ga, gc, w_out_b)


def _conv_bwd_params(doc, y, cacg, lg, lb, tm, parts):
    r = cacg.shape[0]
    n_steps = r // tm
    n = len(parts)

    def body(do_ref, y_ref, c_ref, cp_ref, lg_ref, lb_ref, *rest):
        src = rest[:n]
        dy_ref, dcw_ref, dcb_ref, dlg_ref, dlb_ref = rest[n:n + 5]
        dst = rest[n + 5:2 * n + 5]
        ub_ref, accw_ref, send_sems, recv_sems = rest[2 * n + 5:]
        i = pl.program_id(0)

        @pl.when(i == 0)
        def _():
            for cp in _scatter(src, dst, send_sems, recv_sems):
                cp.start()
            accw_ref[...] = jnp.zeros_like(accw_ref)
            dcb_ref[...] = jnp.zeros_like(dcb_ref)
            dlg_ref[...] = jnp.zeros_like(dlg_ref)
            dlb_ref[...] = jnp.zeros_like(dlb_ref)

        _shifted_copies(ub_ref, _glu_window(cp_ref, c_ref, i))

        def chunk(ci, carry):
            r0 = pl.multiple_of(ci * CONV_CHUNK, CONV_CHUNK)
            y = y_ref[pl.ds(r0, CONV_CHUNK), :]
            yc = y - jnp.mean(y, axis=-1, keepdims=True)
            rs = lax.rsqrt(jnp.mean(yc * yc, axis=-1, keepdims=True) + NORM_EPS)
            xhat = yc * rs
            yn = xhat * lg_ref[...] + lb_ref[...]
            sg = _sigmoid(yn)
            dyn = do_ref[pl.ds(r0, CONV_CHUNK), :] * (sg * (1.0 + yn * (1.0 - sg)))
            dlg_ref[...] += jnp.sum(dyn * xhat, axis=0, keepdims=True)
            dlb_ref[...] += jnp.sum(dyn, axis=0, keepdims=True)
            dxh = dyn * lg_ref[...]
            dy = rs * (dxh - jnp.mean(dxh, axis=-1, keepdims=True) - xhat * jnp.mean(dxh * xhat, axis=-1, keepdims=True))
            dcb_ref[...] += jnp.sum(dy, axis=0, keepdims=True)
            dy_ref[pl.ds(r0, CONV_CHUNK), :] = dy
            for j in range(CONV_K):
                a, b = divmod(FWD_SHIFTS[j], 8)
                prod = dy * ub_ref[b, pl.ds(r0 + 8 * a, CONV_CHUNK), :]
                accw_ref[j] += jnp.sum(prod.reshape(CONV_CHUNK // 8, 8, CONV_W), axis=0)
            return carry

        lax.fori_loop(0, tm // CONV_CHUNK, chunk, 0, unroll=4)

        @pl.when(i == n_steps - 1)
        def _():
            for j in range(32):
                dcw_ref[j:j + 1, :] = jnp.sum(accw_ref[j], axis=0, keepdims=True)
            for cp in _scatter(src, dst, send_sems, recv_sems):
                cp.wait()

    vec = _const((1, CONV_W))
    return pl.pallas_call(
        body, name="conv_bwd_params", grid=(n_steps,),
        in_specs=[_row(tm, CONV_W), _row(tm, CONV_W), _row(tm, 2 * CONV_W), _halo_before(tm, 2 * CONV_W), vec, vec] + [ANY] * n,
        out_specs=[_row(tm, CONV_W), _const((32, CONV_W)), vec, vec, vec] + [ANY] * n,
        out_shape=[jax.ShapeDtypeStruct((r, CONV_W), F32), jax.ShapeDtypeStruct((32, CONV_W), F32)]
        + [jax.ShapeDtypeStruct((1, CONV_W), F32)] * 3 + _scatter_landing(parts),
        scratch_shapes=[pltpu.VMEM((8, tm + HALO, CONV_W), F32), pltpu.VMEM((32, 8, CONV_W), F32),
                        pltpu.SemaphoreType.DMA((7 * n,)), pltpu.SemaphoreType.DMA((7 * n,))],
        compiler_params=_cparams(),
    )(doc, y, cacg, cacg, lg, lb, *parts)


def _conv_bwd_data(dy, cacg, cw, tm):
    r = cacg.shape[0]
    n_steps = r // tm

    def body(dy_ref, dyn_ref, c_ref, w_ref, dc_ref, ub_ref):
        last = (jnp.zeros((HALO, CONV_W), jnp.int32) + pl.program_id(0)) == n_steps - 1
        win = jnp.concatenate([dy_ref[...], jnp.where(last, 0.0, dyn_ref[...])], axis=0)
        _shifted_copies(ub_ref, win)

        def chunk(ci, carry):
            r0 = pl.multiple_of(ci * CONV_CHUNK, CONV_CHUNK)
            du = _conv_chunk(ub_ref, w_ref, r0, BWD_SHIFTS)
            ca = c_ref[pl.ds(r0, CONV_CHUNK), 0:CONV_W]
            sg = _sigmoid(c_ref[pl.ds(r0, CONV_CHUNK), CONV_W:2 * CONV_W])
            dc_ref[pl.ds(r0, CONV_CHUNK), 0:CONV_W] = (du * sg).astype(BF16)
            dc_ref[pl.ds(r0, CONV_CHUNK), CONV_W:2 * CONV_W] = (du * ca * sg * (1.0 - sg)).astype(BF16)
            return carry

        lax.fori_loop(0, tm // CONV_CHUNK, chunk, 0, unroll=4)

    halo_after = pl.BlockSpec((HALO, CONV_W), lambda i: (jnp.minimum((i + 1) * (tm // HALO), r // HALO - 1), 0))
    return pl.pallas_call(
        body, name="conv_bwd_data", grid=(n_steps,),
        in_specs=[_row(tm, CONV_W), halo_after, _row(tm, 2 * CONV_W), _const((32, CONV_W))],
        out_specs=_row(tm, 2 * CONV_W),
        out_shape=jax.ShapeDtypeStruct((r, 2 * CONV_W), BF16),
        scratch_shapes=[pltpu.VMEM((8, tm + HALO, CONV_W), F32)],
        compiler_params=_cparams(),
    )(dy, dy, cacg, cw)


def _attn_bwd(q, kv, o, do, lse, sinks, parts):
    r = q.shape[0]
    nb = r // BLOCK
    n = len(parts)

    def body(sink_ref, q_ref, kvc_ref, kvp_ref, kvm_ref, o_ref, do_ref, lse_ref, *rest):
        src = rest[:n]
        dq_ref, dkv_ref, dmeta_ref, dsink_ref = rest[n:n + 4]
        dst = rest[n + 4:2 * n + 4]
        hold_ref, bias_ref, late_ref, send_sems, recv_sems = rest[2 * n + 4:]
        i = pl.program_id(0)

        @pl.when(i == 0)
        def _():
            for cp in _scatter(src, dst, send_sems, recv_sems):
                cp.start()
            _attn_bias_init(bias_ref, late_ref)
            dmeta_ref[...] = jnp.zeros_like(dmeta_ref)
            dsink_ref[...] = jnp.zeros_like(dsink_ref)
            hold_ref[...] = jnp.zeros_like(hold_ref)

        @pl.when(i < nb)
        def _():
            lane = lax.broadcasted_iota(jnp.int32, (BLOCK, BLOCK), 1)
            lse_tile = lse_ref[...]
            zero = jnp.zeros((BLOCK, BLOCK), F32)
            block_bias = _attn_block_bias(late_ref, i)
            prev_part = late_ref[...] > 0.5
            for g in range(N_KV):
                kcat, vcat = _kv_cat(kvm_ref, kvp_ref, kvc_ref, g)
                heads = range(g * GROUP, (g + 1) * GROUP)
                qs = _stack_heads(q_ref, g)
                dos = _stack_heads(do_ref, g)
                dosb = dos.astype(BF16)
                lse = jnp.concatenate(
                    [jnp.sum(jnp.where(lane == h, lse_tile, 0.0), axis=-1, keepdims=True) + zero for h in heads], axis=0)
                delta = jnp.sum(dos * _stack_heads(o_ref, g), axis=-1, keepdims=True) + jnp.zeros((GROUP * BLOCK, BLOCK), F32)
                band_bias = bias_ref[g, :, BLOCK:2 * BLOCK] + block_bias[1]
                bias = [bias_ref[g, :, 0:BLOCK] + block_bias[0], jnp.where(prev_part, band_bias, NEG), jnp.where(prev_part, NEG, band_bias)]
                s = _dot_nt(qs, kcat)
                dp = _dot_nt(dosb, vcat)
                ps = [jnp.exp(s[:, k * BLOCK:(k + 1) * BLOCK] * SCALE + bias[k] - lse) for k in range(3)]
                p = jnp.concatenate(ps, axis=1)
                ds = jnp.concatenate(
                    [(ps[k] * (dp[:, k * BLOCK:(k + 1) * BLOCK] - delta)) * SCALE for k in range(3)], axis=1).astype(BF16)
                sink_term = jnp.exp(_head_rows([sink_ref[h] for h in heads]) - lse)[:, 0:1] * delta[:, 0:1]
                dq = _dot(ds, kcat).astype(BF16)
                for j, h in enumerate(heads):
                    dsink_ref[h:h + 1, :] += -jnp.sum(sink_term[j * BLOCK:(j + 1) * BLOCK])
                    dq_ref[:, h * HEAD_DIM:(h + 1) * HEAD_DIM] = dq[j * BLOCK:(j + 1) * BLOCK]
                dk_t = _dot_tn(qs, ds)
                dv_t = _dot_tn(dosb, p.astype(BF16))
                ks = slice(g * HEAD_DIM, (g + 1) * HEAD_DIM)
                vs = slice(KV_W + g * HEAD_DIM, KV_W + (g + 1) * HEAD_DIM)
                for sl, grad_t in ((ks, dk_t), (vs, dv_t)):
                    dmeta_ref[:, sl] += grad_t[:, 0:BLOCK].T
                    dkv_ref[:, sl] = hold_ref[:, sl] + grad_t[:, BLOCK:2 * BLOCK].T
                    hold_ref[:, sl] = grad_t[:, 2 * BLOCK:3 * BLOCK].T

        @pl.when(i == nb)
        def _():
            dkv_ref[...] = hold_ref[...]
            for cp in _scatter(src, dst, send_sems, recv_sems):
                cp.wait()

    def cur(i):
        return jnp.minimum(i, nb - 1)

    return pl.pallas_call(
        body, name="attn_bwd", grid=(nb + 1,),
        in_specs=[pl.BlockSpec(memory_space=pltpu.SMEM),
                  pl.BlockSpec((BLOCK, ATTN_W), lambda i: (cur(i), 0)),
                  pl.BlockSpec((BLOCK, 2 * KV_W), lambda i: (cur(i), 0)),
                  pl.BlockSpec((BLOCK, 2 * KV_W), lambda i: (jnp.maximum(cur(i) - 1, 0), 0)),
                  _const((BLOCK, 2 * KV_W)),
                  pl.BlockSpec((BLOCK, ATTN_W), lambda i: (cur(i), 0)),
                  pl.BlockSpec((BLOCK, ATTN_W), lambda i: (cur(i), 0)),
                  pl.BlockSpec((BLOCK, BLOCK), lambda i: (cur(i), 0))] + [ANY] * n,
        out_specs=[pl.BlockSpec((BLOCK, ATTN_W), lambda i: (cur(i), 0)),
                   pl.BlockSpec((BLOCK, 2 * KV_W), lambda i: (jnp.maximum(i - 1, 0), 0)),
                   _const((BLOCK, 2 * KV_W)), _const((N_HEADS, BLOCK))] + [ANY] * n,
        out_shape=[jax.ShapeDtypeStruct((r, ATTN_W), BF16), jax.ShapeDtypeStruct((r, 2 * KV_W), F32),
                   jax.ShapeDtypeStruct((BLOCK, 2 * KV_W), F32), jax.ShapeDtypeStruct((N_HEADS, BLOCK), F32)] + _scatter_landing(parts),
        scratch_shapes=[pltpu.VMEM((BLOCK, 2 * KV_W), F32), pltpu.VMEM((N_KV, GROUP * BLOCK, 2 * BLOCK), F32),
                        pltpu.VMEM((GROUP * BLOCK, BLOCK), F32), pltpu.SemaphoreType.DMA((7 * n,)), pltpu.SemaphoreType.DMA((7 * n,))],
        compiler_params=_cparams(),
    )(sinks, q, kv, kv, kv, o, do, lse, *parts)


def _in_proj_bwd(dq, dkv, dkv_meta, dc, dh1, h0, g1, w_in_t, seq, tm):
    r = h0.shape[0]
    n_tiles = r // tm
    n_out = -(-seq // tm)

    def body(dq_ref, dkv_ref, dm_ref, dc_ref, dh1_ref, h_ref, g_ref, w_ref, gx_ref, lead_ref, dwt_ref, dg_ref, dw_ref, hold_ref):
        i = pl.program_id(0)

        @pl.when(i == 0)
        def _():
            dw_ref[...] = jnp.zeros_like(dw_ref)
            dg_ref[...] = jnp.zeros_like(dg_ref)

        @pl.when(i < n_tiles)
        def _():
            meta = jnp.concatenate([dm_ref[...], jnp.zeros((tm - BLOCK, 2 * KV_W), F32)], axis=0) if tm > BLOCK else dm_ref[...]
            first = (jnp.zeros((tm, 2 * KV_W), jnp.int32) + i) == 0
            dkvb = (dkv_ref[...] + jnp.where(first, meta, 0.0)).astype(BF16)
            hn, xhat, rstd = _rms_fwd(h_ref[...], g_ref[...])
            dproj = jnp.concatenate([dq_ref[...], dkvb, dc_ref[...]], axis=1)
            dhn = _dot(dproj, w_ref[...])
            dw_ref[...] += _dot_tn(dproj, hn.astype(BF16))
            dx, dg_rows = _rms_bwd(dhn, xhat, rstd, g_ref[...])
            dg_ref[...] += jnp.sum(dg_rows, axis=0, keepdims=True)
            dh0 = dh1_ref[...] + dx

            @pl.when(i == 0)
            def _():
                lead_ref[...] = dh0[0:BLOCK]

            @pl.when((i >= 1) & (i <= n_out))
            def _():
                gx_ref[0:tm - BLOCK, :] = hold_ref[...]
                gx_ref[tm - BLOCK:tm, :] = dh0[0:BLOCK]

            hold_ref[...] = dh0[BLOCK:tm]

        @pl.when((i == n_tiles) & (n_tiles <= n_out))
        def _():
            gx_ref[0:tm - BLOCK, :] = hold_ref[...]

        @pl.when(i == n_tiles - 1)
        def _():
            dwt_ref[...] = dw_ref[...].astype(BF16)

    def tile(n):
        return pl.BlockSpec((tm, n), lambda i: (jnp.minimum(i, n_tiles - 1), 0))

    return pl.pallas_call(
        body, name="in_proj_bwd", grid=(n_tiles + 1,),
        in_specs=[tile(ATTN_W), tile(2 * KV_W), _const((BLOCK, 2 * KV_W)), tile(2 * CONV_W), tile(D_MODEL), tile(D_MODEL),
                  _const((1, D_MODEL)), _const((IN_COLS, D_MODEL))],
        out_specs=[pl.BlockSpec((tm, D_MODEL), lambda i: (jnp.clip(i - 1, 0, n_out - 1), 0)), _const((BLOCK, D_MODEL)),
                   _const((IN_COLS, D_MODEL)), _const((1, D_MODEL))],
        out_shape=[jax.ShapeDtypeStruct((seq, D_MODEL), F32), jax.ShapeDtypeStruct((BLOCK, D_MODEL), F32),
                   jax.ShapeDtypeStruct((IN_COLS, D_MODEL), BF16), jax.ShapeDtypeStruct((1, D_MODEL), F32)],
        scratch_shapes=[pltpu.VMEM((IN_COLS, D_MODEL), F32), pltpu.VMEM((tm - BLOCK, D_MODEL), F32)],
        compiler_params=_cparams(),
    )(dq, dkv, dkv_meta, dc, dh1, h0, g1, w_in_t)


def _adamw_update(w_ref, g_ref, m_ref, v_ref, d_ref, nm_ref, nv_ref):
    g = g_ref[...]
    m = ADAM_B1 * m_ref[...] + (1.0 - ADAM_B1) * g
    v = ADAM_B2 * v_ref[...] + (1.0 - ADAM_B2) * (g * g)
    m_hat = m / (1.0 - ADAM_B1 ** ADAM_STEP)
    v_hat = v / (1.0 - ADAM_B2 ** ADAM_STEP)
    d_ref[...] = -ADAM_LR * (m_hat / (jnp.sqrt(v_hat) + ADAM_EPS) + ADAM_WD * w_ref[...])
    nm_ref[...] = m
    nv_ref[...] = v


def _adamw(w, g, m, v, name):
    rows, cols = w.shape
    tr = rows
    for cand in (256, 176, 128, 64, 32, 16, 8):
        if rows % cand == 0:
            tr = cand
            break

    def body(*refs):
        _adamw_update(*refs)

    spec = _row(tr, cols)
    return pl.pallas_call(
        body, name=name, grid=(rows // tr,), in_specs=[spec] * 4, out_specs=[spec] * 3,
        out_shape=[jax.ShapeDtypeStruct((rows, cols), F32)] * 3, compiler_params=_cparams(),
    )(w, g, m, v)


MESH = pl.DeviceIdType.MESH
ANY = pl.BlockSpec(memory_space=pl.ANY)


def _place():
    x, y, c = lax.axis_index("x"), lax.axis_index("y"), lax.axis_index("c")
    chips = [(1 - x, y), (x, 1 - y), (1 - x, 1 - y)]
    return x, y, c, chips


def _gather_ici(src, dst, send_sems, recv_sems):
    x, y, c, chips = _place()
    sends, arrivals = [], []
    for k in range(len(src)):
        rows = src[k].shape[0] // 2
        half = pl.ds(c * rows, rows)
        for p, chip in enumerate(chips):
            sems = dict(send_sem=send_sems.at[3 * k + p], recv_sem=recv_sems.at[3 * k + p], device_id=(chip[0], chip[1], c),
                        device_id_type=MESH)
            sends.append(pltpu.make_async_remote_copy(src_ref=src[k].at[half], dst_ref=dst[k].at[2 * x + y, half], **sems))
            theirs = dst[k].at[2 * chip[0] + chip[1], half]
            arrivals.append(pltpu.make_async_remote_copy(src_ref=theirs, dst_ref=theirs, **sems))
    return sends, arrivals


def _gather_d2d(dst, send_sems, recv_sems):
    x, y, c, chips = _place()
    sends, arrivals = [], []
    for k in range(len(dst)):
        rows = dst[k].shape[1] // 2
        for p, chip in enumerate(chips):
            sems = dict(send_sem=send_sems.at[3 * k + p], recv_sem=recv_sems.at[3 * k + p], device_id=(x, y, 1 - c),
                        device_id_type=MESH)
            mine = dst[k].at[2 * chip[0] + chip[1], pl.ds(c * rows, rows)]
            sends.append(pltpu.make_async_remote_copy(src_ref=mine, dst_ref=mine, **sems))
            theirs = dst[k].at[2 * chip[0] + chip[1], pl.ds((1 - c) * rows, rows)]
            arrivals.append(pltpu.make_async_remote_copy(src_ref=theirs, dst_ref=theirs, **sems))
    return sends, arrivals


def _own_slots(shard):
    return jnp.broadcast_to(shard[None], (N_SHARD,) + shard.shape)


def _gather_weights(shards):
    n = len(shards)

    def body(*refs):
        src, dst = refs[:n], refs[2 * n:3 * n]
        ici_send, ici_recv, d2d_send, d2d_recv = refs[3 * n:]
        sends, arrivals = _gather_ici(src, dst, ici_send, ici_recv)
        for cp in sends:
            cp.start()
        for cp in arrivals:
            cp.wait_recv()
        forwards, from_sibling = _gather_d2d(dst, d2d_send, d2d_recv)
        for cp in forwards:
            cp.start()
        for cp in from_sibling:
            cp.wait_recv()
        for cp in sends + forwards:
            cp.wait_send()

    return pl.pallas_call(
        body, name="gather_weights",
        in_specs=[ANY] * (2 * n), out_specs=[ANY] * n,
        out_shape=[jax.ShapeDtypeStruct((N_SHARD,) + s.shape, s.dtype) for s in shards],
        input_output_aliases={n + k: k for k in range(n)},
        scratch_shapes=[pltpu.SemaphoreType.DMA((3 * n,))] * 4,
    )(*shards, *[_own_slots(s) for s in shards])


VMEM_WHOLE = pl.BlockSpec(memory_space=pltpu.VMEM)


def _allreduce_small(parts, grads):
    widths = sorted({p.shape[1] for p in parts})
    place, heights = [], [0] * len(widths)
    for p in parts:
        gi = widths.index(p.shape[1])
        place.append((gi, heights[gi]))
        heights[gi] += -(-p.shape[0] // 8) * 8
    n, ng, nb = len(parts), len(widths), len(grads)

    def body(*refs):
        ins, big_src = refs[:n], refs[n:n + nb]
        outs, big_dst = refs[n + nb:2 * n + nb], refs[2 * n + nb:2 * (n + nb)]
        slots = refs[2 * (n + nb):2 * (n + nb) + ng]
        send_sems, recv_sems, big_send, big_recv = refs[2 * (n + nb) + ng:]
        scattered = _scatter(big_src, big_dst, big_send, big_recv)
        for cp in scattered:
            cp.start()
        x, y, c = lax.axis_index("x"), lax.axis_index("y"), lax.axis_index("c")
        me = 4 * x + 2 * y + c
        for gi in range(ng):
            slots[gi][me] = jnp.zeros((heights[gi], widths[gi]), F32)
        for k, (gi, r0) in enumerate(place):
            slots[gi][me, r0:r0 + parts[k].shape[0], :] = ins[k][...]

        def copy(gi, j, arriving):
            peer = ((x + (j >> 2)) % 2, (y + ((j >> 1) & 1)) % 2, (c + (j & 1)) % 2)
            slot = 4 * peer[0] + 2 * peer[1] + peer[2] if arriving else me
            return pltpu.make_async_remote_copy(
                src_ref=slots[gi].at[me], dst_ref=slots[gi].at[slot], send_sem=send_sems.at[7 * gi + j - 1],
                recv_sem=recv_sems.at[7 * gi + j - 1], device_id=peer, device_id_type=MESH)

        pairs = [(gi, j) for gi in range(ng) for j in range(1, N_DEV)]
        for gi, j in pairs:
            copy(gi, j, False).start()
        for gi, j in pairs:
            copy(gi, j, True).wait_recv()
        totals = []
        for gi in range(ng):
            total = slots[gi][0]
            for d in range(1, N_DEV):
                total = total + slots[gi][d]
            totals.append(total)
        for k, (gi, r0) in enumerate(place):
            outs[k][...] = totals[gi][r0:r0 + parts[k].shape[0], :]
        for gi, j in pairs:
            copy(gi, j, False).wait_send()
        for cp in scattered:
            cp.wait()

    out = pl.pallas_call(
        body, name="allreduce_small", in_specs=[VMEM_WHOLE] * n + [ANY] * nb, out_specs=[VMEM_WHOLE] * n + [ANY] * nb,
        out_shape=[jax.ShapeDtypeStruct(p.shape, F32) for p in parts] + _scatter_landing(grads),
        scratch_shapes=[pltpu.VMEM((N_DEV, heights[gi], widths[gi]), F32) for gi in range(ng)]
        + [pltpu.SemaphoreType.DMA((7 * ng,)), pltpu.SemaphoreType.DMA((7 * ng,)),
           pltpu.SemaphoreType.DMA((7 * nb,)), pltpu.SemaphoreType.DMA((7 * nb,))],
    )(*parts, *grads)
    return out[:n], out[n:]


def _adamw_small(ws, gs, ms, vs):
    n = len(ws)

    def body(*refs):
        for k in range(n):
            w_ref, g_ref, m_ref, v_ref = (refs[j * n + k] for j in range(4))
            _adamw_update(w_ref, g_ref, m_ref, v_ref, *(refs[(4 + j) * n + k] for j in range(3)))

    shapes = [jax.ShapeDtypeStruct(w.shape, F32) for w in ws]
    out = pl.pallas_call(
        body, name="adamw_small", in_specs=[VMEM_WHOLE] * (4 * n), out_specs=[VMEM_WHOLE] * (3 * n), out_shape=shapes * 3,
    )(*ws, *gs, *ms, *vs)
    return out[:n], out[n:2 * n], out[2 * n:]


def _scatter(src, dst, send_sems, recv_sems):
    x, y, c = lax.axis_index("x"), lax.axis_index("y"), lax.axis_index("c")
    copies = []
    for k in range(len(src)):
        rows = src[k].shape[1] // 2
        for j in range(1, N_DEV):
            px, py, pc = (x + (j >> 2)) % 2, (y + ((j >> 1) & 1)) % 2, (c + (j & 1)) % 2
            copies.append(pltpu.make_async_remote_copy(
                src_ref=src[k].at[2 * px + py, pl.ds(pc * rows, rows)], dst_ref=dst[k].at[j - 1],
                send_sem=send_sems.at[7 * k + j - 1], recv_sem=recv_sems.at[7 * k + j - 1], device_id=(px, py, pc),
                device_id_type=MESH))
    return copies


def _scatter_landing(parts):
    return [jax.ShapeDtypeStruct((N_DEV - 1, p.shape[1] // 2, p.shape[2]), p.dtype) for p in parts]


def _sum_pieces(own, landed):
    n = len(own)

    def body(*refs):
        for k in range(n):
            got = refs[n + k]
            total = refs[k][...].astype(F32)
            for j in range(N_DEV - 1):
                total = total + got[j].astype(F32)
            refs[2 * n + k][...] = total

    in_specs, out_specs = [], []
    for o in own:
        in_specs.append(_row(o.shape[0] // 2, o.shape[1]))
    for o in own:
        in_specs.append(pl.BlockSpec((N_DEV - 1, o.shape[0] // 2, o.shape[1]), lambda i: (0, i, 0)))
        out_specs.append(_row(o.shape[0] // 2, o.shape[1]))
    return pl.pallas_call(
        body, name="sum_pieces", grid=(2,), in_specs=in_specs, out_specs=out_specs,
        out_shape=[jax.ShapeDtypeStruct(o.shape, F32) for o in own], compiler_params=_cparams(),
    )(*own, *landed)


def _swap_with_sibling(halves):
    n = len(halves)

    def body(*refs):
        x, y, c = lax.axis_index("x"), lax.axis_index("y"), lax.axis_index("c")
        copies = [pltpu.make_async_remote_copy(
            src_ref=refs[k], dst_ref=refs[n + k], send_sem=refs[2 * n].at[k], recv_sem=refs[2 * n + 1].at[k],
            device_id=(x, y, 1 - c), device_id_type=MESH) for k in range(n)]
        for cp in copies:
            cp.start()
        for cp in copies:
            cp.wait()

    return pl.pallas_call(
        body, name="swap_with_sibling", in_specs=[ANY] * n, out_specs=[ANY] * n,
        out_shape=[jax.ShapeDtypeStruct(h.shape, h.dtype) for h in halves],
        scratch_shapes=[pltpu.SemaphoreType.DMA((n,)), pltpu.SemaphoreType.DMA((n,))],
    )(*halves)


def _own_piece(part):
    rows = part.shape[1] // 2
    s = 2 * lax.axis_index("x") + lax.axis_index("y")
    return lax.dynamic_slice(part, (s, lax.axis_index("c") * rows, 0), (1, rows, part.shape[2]))[0]


def _both_halves(mine, theirs):
    south = lax.axis_index("c") == 0
    return jnp.concatenate([jnp.where(south, mine, theirs), jnp.where(south, theirs, mine)], axis=0)


def _from_col_shards(g):
    return g.transpose(1, 0, 2).reshape(g.shape[1], -1)


def kernel(x, meta_tokens, attn_norm_g, w_in, attn_sinks, conv_w, conv_b, conv_ln_g, conv_ln_b, attn_out_g, conv_out_g, w_out, ffn_norm_g, w_gate, w_up, w_down, final_norm_g, loss_target, m_meta_tokens, m_attn_norm_g, m_w_in, m_attn_sinks, m_conv_w, m_conv_b, m_conv_ln_g, m_conv_ln_b, m_attn_out_g, m_conv_out_g, m_w_out, m_ffn_norm_g, m_w_gate, m_w_up, m_w_down, m_final_norm_g, v_meta_tokens, v_attn_norm_g, v_w_in, v_attn_sinks, v_conv_w, v_conv_b, v_conv_ln_g, v_conv_ln_b, v_attn_out_g, v_conv_out_g, v_w_out, v_ffn_norm_g, v_w_gate, v_w_up, v_w_down, v_final_norm_g):
    seq = x.shape[1]
    r = -(-(seq + BLOCK) // ROW_QUANTUM) * ROW_QUANTUM
    tm_wide = 768 if seq >= 768 else 256
    shard = 2 * lax.axis_index("x") + lax.axis_index("y")

    conv_w32 = jnp.pad(conv_w[0], ((0, 1), (0, 0)))
    small_shard = jnp.concatenate([meta_tokens, conv_w32.reshape(16, 256)], axis=0)
    g_in, g_small = _gather_weights([w_in[0].T.astype(BF16), small_shard])
    later = [w_gate[0].T.astype(BF16), w_up[0].T.astype(BF16), w_out[0].astype(BF16), w_down[0].astype(BF16)]
    w_in_t = g_in.reshape(IN_COLS, D_MODEL)
    meta_full = _from_col_shards(g_small[:, 0:N_META])
    cw_full = _from_col_shards(g_small[:, N_META:].reshape(N_SHARD, 32, 128))

    g1, ga, gc, g2 = attn_norm_g, attn_out_g, conv_out_g, ffn_norm_g
    gf = final_norm_g.reshape(1, D_MODEL)
    sinks = attn_sinks[0]

    lead = jnp.concatenate([jnp.zeros((LEAD, D_MODEL), F32), meta_full], axis=0)
    h0, q, kv, cacg = _in_proj(x[0], lead, g1, w_in_t, r, 768)
    oa, lse, *gathered = _attn_fwd(q, kv, sinks, later, [_own_slots(s) for s in later])
    oc, yc, g_gate, g_up, g_out, g_down = _conv_fwd(cacg, cw_full, conv_b, conv_ln_g, conv_ln_b, 384, gathered)
    wg_t, wu_t, wd_b = g_gate.reshape(D_FF, D_MODEL), g_up.reshape(D_FF, D_MODEL), g_down.reshape(D_FF, D_MODEL)
    w_out_b = g_out.reshape(D_MODEL, D_MODEL)
    h1, hn2 = _out_proj(oa, oc, h0, ga, gc, g2, w_out_b, 768)
    gate, up, act, dh2, dh2b, loss_p, dgf = _ffn_fwd(hn2, h1, loss_target[0], gf, wg_t, wu_t, wd_b, 384)

    def by_shard(dw):
        return dw.reshape(N_SHARD, dw.shape[0] // N_SHARD, D_MODEL)

    dgate, dup, dh1, dg2 = _ffn_bwd(dh2, dh2b, gate, up, h1, g2, wg_t, wu_t, wd_b, 384)
    p_gate, p_up = [by_shard(dw) for dw in _ffn_wgrad_gu(hn2, dgate, dup, 768)]
    p_down = by_shard(_ffn_wgrad_d(act, dh2b, 768))
    doa, doc, dwo, dga, dgc = _out_proj_bwd(dh1, oa, oc, ga, gc, w_out_b, 768)
    p_out = by_shard(dwo)
    dy, dcw, dcb, dlg, dlb, l_gate, l_up = _conv_bwd_params(doc, yc, cacg, conv_ln_g, conv_ln_b, 384, [p_gate, p_up])
    dc = _conv_bwd_data(dy, cacg, cw_full, 384)
    dq, dkv, dkv_meta, dsink, l_out, l_down = _attn_bwd(q, kv, oa, doa, lse, sinks, [p_out, p_down])
    grad_x, dlead, dwi_t, dg1 = _in_proj_bwd(dq, dkv, dkv_meta, dc, dh1, h0, g1, w_in_t, seq, tm_wide)
    p_in = by_shard(dwi_t)

    red_names = ("final_norm_g", "attn_norm_g", "ffn_norm_g", "meta_tokens", "attn_out_g", "conv_out_g", "conv_b", "conv_ln_g",
                 "conv_ln_b", "conv_w", "loss", "attn_sinks")
    sums, (l_in,) = _allreduce_small([dgf, dg1, dg2, dlead[LEAD:BLOCK], dga, dgc, dcb, dlg, dlb, dcw, loss_p, dsink], [p_in])
    red = dict(zip(red_names, sums))
    loss = red["loss"][0, 0]
    red["final_norm_g"] = red["final_norm_g"].reshape(D_MODEL)
    red["attn_sinks"] = red["attn_sinks"][:, 0].reshape(1, N_HEADS)
    g_meta = lax.dynamic_slice_in_dim(red["meta_tokens"], shard * (D_MODEL // N_SHARD), D_MODEL // N_SHARD, axis=1)
    g_convw = lax.dynamic_slice_in_dim(red["conv_w"][0:CONV_K], shard * (CONV_W // N_SHARD), CONV_W // N_SHARD, axis=1)[None]

    halves = _sum_pieces([_own_piece(p) for p in (p_in, p_gate, p_up, p_out, p_down)], [l_in, l_gate, l_up, l_out, l_down])
    big = ("w_in", "w_gate", "w_up", "w_out", "w_down")
    transposed = ("w_in", "w_gate", "w_up")
    g_big = {name: _both_halves(mine, theirs) for name, mine, theirs in zip(big, halves, _swap_with_sibling(halves))}

    grads = {
        "meta_tokens": g_meta, "attn_norm_g": red["attn_norm_g"], "attn_sinks": red["attn_sinks"],
        "conv_w": g_convw, "conv_b": red["conv_b"], "conv_ln_g": red["conv_ln_g"], "conv_ln_b": red["conv_ln_b"],
        "attn_out_g": red["attn_out_g"], "conv_out_g": red["conv_out_g"], "ffn_norm_g": red["ffn_norm_g"],
        "final_norm_g": red["final_norm_g"]}
    params = {
        "meta_tokens": (meta_tokens, m_meta_tokens, v_meta_tokens), "attn_norm_g": (attn_norm_g, m_attn_norm_g, v_attn_norm_g),
        "w_in": (w_in, m_w_in, v_w_in), "attn_sinks": (attn_sinks, m_attn_sinks, v_attn_sinks), "conv_w": (conv_w, m_conv_w, v_conv_w),
        "conv_b": (conv_b, m_conv_b, v_conv_b), "conv_ln_g": (conv_ln_g, m_conv_ln_g, v_conv_ln_g),
        "conv_ln_b": (conv_ln_b, m_conv_ln_b, v_conv_ln_b), "attn_out_g": (attn_out_g, m_attn_out_g, v_attn_out_g),
        "conv_out_g": (conv_out_g, m_conv_out_g, v_conv_out_g), "w_out": (w_out, m_w_out, v_w_out),
        "ffn_norm_g": (ffn_norm_g, m_ffn_norm_g, v_ffn_norm_g), "w_gate": (w_gate, m_w_gate, v_w_gate), "w_up": (w_up, m_w_up, v_w_up),
        "w_down": (w_down, m_w_down, v_w_down), "final_norm_g": (final_norm_g, m_final_norm_g, v_final_norm_g)}
    names = list(params)
    delta, new_m, new_v = {}, {}, {}
    for name in big:
        flip = (lambda a: a.T) if name in transposed else (lambda a: a)
        w, m, v = params[name]
        outs = _adamw(flip(w[0]), g_big[name], flip(m[0]), flip(v[0]), "adamw_" + name)
        grads[name], delta[name], new_m[name], new_v[name] = [flip(a)[None] for a in (g_big[name], *outs)]
    rest = [name for name in names if name not in big]

    def rows_of(a):
        return a.reshape(-1, a.shape[-1])

    small = _adamw_small([rows_of(params[n][0]) for n in rest], [rows_of(grads[n]) for n in rest],
                         [rows_of(params[n][1]) for n in rest], [rows_of(params[n][2]) for n in rest])
    for dst, outs in zip((delta, new_m, new_v), small):
        for name, out in zip(rest, outs):
            dst[name] = out.reshape(params[name][0].shape)

    return (loss, grad_x[None], *[grads[n] for n in names], *[delta[n] for n in names], *[new_m[n] for n in names],
            *[new_v[n] for n in names])
```

```python
import functools
import math

import jax
import jax.numpy as jnp
from jax import lax
from jax.experimental import pallas as pl
from jax.experimental.pallas import tpu as pltpu

F32 = jnp.float32
BF16 = jnp.bfloat16

D_MODEL = 1024
N_META = 16
ATTN_W = 512
CONV_W = 512
HEAD_DIM = 64
N_HEADS = 8
N_KV = 2
GROUP = N_HEADS // N_KV
KV_W = N_KV * HEAD_DIM
BLOCK = 128
LEAD = BLOCK - N_META
CONV_K = 31
D_FF = 2816
IN_COLS = ATTN_W + 2 * KV_W + 2 * CONV_W
Q0, KV0, C0 = 0, ATTN_W, ATTN_W + 2 * KV_W
NORM_EPS = 1e-5
SCALE = 1.0 / math.sqrt(HEAD_DIM)
SLOPES = tuple(2.0 ** (-(8.0 / N_HEADS) * (h + 1)) for h in range(N_HEADS))
NEG = -1e30

ADAM_LR, ADAM_B1, ADAM_B2, ADAM_EPS, ADAM_WD, ADAM_STEP = 0.001, 0.9, 0.999, 1e-08, 0.01, 10

N_SHARD = 4
N_DEV = 8
ROW_QUANTUM = 768
HALO = 32
CONV_CHUNK = 32
FF_CHUNK = 256
FF_CHUNKS = tuple(slice(c, c + FF_CHUNK) for c in range(0, D_FF, FF_CHUNK))
VMEM_LIMIT = 60 * 1024 * 1024


def _cparams(n_axes=1):
    return pltpu.CompilerParams(dimension_semantics=("arbitrary",) * n_axes, vmem_limit_bytes=VMEM_LIMIT)


def _dot(a, b):
    return jnp.dot(a, b, preferred_element_type=F32)


def _dot_nt(a, b):
    return lax.dot_general(a, b, (((1,), (1,)), ((), ())), preferred_element_type=F32)


def _dot_tn(a, b):
    return lax.dot_general(a, b, (((0,), (0,)), ((), ())), preferred_element_type=F32)


def _sigmoid(x):
    return 1.0 / (1.0 + jnp.exp(-x))


def _row(tm, n):
    return pl.BlockSpec((tm, n), lambda i: (i, 0))


def _const(shape):
    return pl.BlockSpec(shape, lambda i: (0,) * len(shape))


def _resident(shape):
    return pl.BlockSpec(shape, lambda i: (0,) * len(shape), pipeline_mode=pl.Buffered(1))


def _rms_fwd(x, g):
    rstd = lax.rsqrt(jnp.mean(x * x, axis=-1, keepdims=True) + NORM_EPS)
    xhat = x * rstd
    return xhat * g, xhat, rstd


def _rms_bwd(dy, xhat, rstd, g):
    dxh = dy * g
    dx = rstd * (dxh - xhat * jnp.mean(dxh * xhat, axis=-1, keepdims=True))
    return dx, dy * xhat


def _in_proj(x, lead, g1, w_in_t, r, tm):
    seq = x.shape[0]
    n_sub = tm // BLOCK

    def body(*refs):
        x_refs = refs[:n_sub]
        lead_ref, g_ref, w_ref, h0_ref, q_ref, kv_ref, c_ref = refs[n_sub:]
        i = pl.program_id(0)
        pieces = []
        for k, x_ref in enumerate(x_refs):
            at = jnp.zeros((BLOCK, D_MODEL), jnp.int32) + (i * tm + (k - 1) * BLOCK)
            piece = jnp.where((at >= 0) & (at < seq), x_ref[...], 0.0)
            pieces.append(jnp.where(at < 0, lead_ref[...], piece) if k == 0 else piece)
        h = jnp.concatenate(pieces, axis=0)
        h0_ref[...] = h
        hn = _rms_fwd(h, g_ref[...])[0].astype(BF16)
        q_ref[...] = _dot_nt(hn, w_ref[Q0:KV0, :]).astype(BF16)
        kv_ref[...] = _dot_nt(hn, w_ref[KV0:C0, :]).astype(BF16)
        c_ref[...] = _dot_nt(hn, w_ref[C0:IN_COLS, :])

    def x_block(k):
        return pl.BlockSpec((BLOCK, D_MODEL), lambda i: (jnp.clip(n_sub * i - 1 + k, 0, seq // BLOCK - 1), 0))

    return pl.pallas_call(
        body, name="in_proj", grid=(r // tm,),
        in_specs=[x_block(k) for k in range(n_sub)] + [_const((BLOCK, D_MODEL)), _const((1, D_MODEL)), _const((IN_COLS, D_MODEL))],
        out_specs=[_row(tm, D_MODEL), _row(tm, ATTN_W), _row(tm, 2 * KV_W), _row(tm, 2 * CONV_W)],
        out_shape=[jax.ShapeDtypeStruct((r, D_MODEL), F32), jax.ShapeDtypeStruct((r, ATTN_W), BF16),
                   jax.ShapeDtypeStruct((r, 2 * KV_W), BF16), jax.ShapeDtypeStruct((r, 2 * CONV_W), F32)],
        compiler_params=_cparams(),
    )(*[x] * n_sub, lead, g1, w_in_t)


def _attn_bias_init(bias_ref, late_ref):
    row = lax.broadcasted_iota(jnp.int32, (GROUP * BLOCK, BLOCK), 0) & (BLOCK - 1)
    col = lax.broadcasted_iota(jnp.int32, (GROUP * BLOCK, BLOCK), 1)
    late_ref[...] = jnp.where(col > row, 1.0, 0.0)
    for g in range(N_KV):
        slope = jnp.concatenate([jnp.zeros((BLOCK, BLOCK), F32) + SLOPES[g * GROUP + j] for j in range(GROUP)], axis=0)
        bias_ref[g, :, 0:BLOCK] = jnp.where(col >= LEAD, 0.0, NEG)
        bias_ref[g, :, BLOCK:2 * BLOCK] = -slope * jnp.where(col > row, row - col + BLOCK, row - col).astype(F32)


def _attn_block_bias(late_ref, i):
    late = late_ref[...]
    meta0 = jnp.where(i == 0, NEG, 0.0)
    no_prev = jnp.where(i >= 2, 0.0, NEG)
    no_cur = jnp.where(i >= 1, 0.0, NEG)
    return late * meta0, late * no_prev + no_cur


def _attn_logits(s3, bias_ref, block_bias, prev_part, g):
    meta = s3[:, 0:BLOCK] * SCALE + (bias_ref[g, :, 0:BLOCK] + block_bias[0])
    band = jnp.where(prev_part, s3[:, BLOCK:2 * BLOCK], s3[:, 2 * BLOCK:3 * BLOCK]) * SCALE + (bias_ref[g, :, BLOCK:2 * BLOCK] + block_bias[1])
    return meta, band


def _split_band(meta, band, prev_part):
    return jnp.concatenate([meta, jnp.where(prev_part, band, 0.0), jnp.where(prev_part, 0.0, band)], axis=1)


def _head_rows(vals):
    return jnp.concatenate([jnp.zeros((BLOCK, BLOCK), F32) + v for v in vals], axis=0)


def _stack_heads(ref, g):
    return jnp.concatenate([ref[:, (g * GROUP + j) * HEAD_DIM:(g * GROUP + j + 1) * HEAD_DIM] for j in range(GROUP)], axis=0)


def _kv_cat(kvm_ref, kvp_ref, kvc_ref, g):
    ks = slice(g * HEAD_DIM, (g + 1) * HEAD_DIM)
    vs = slice(KV_W + g * HEAD_DIM, KV_W + (g + 1) * HEAD_DIM)
    kcat = jnp.concatenate([kvm_ref[:, ks], kvp_ref[:, ks], kvc_ref[:, ks]], axis=0)
    vcat = jnp.concatenate([kvm_ref[:, vs], kvp_ref[:, vs], kvc_ref[:, vs]], axis=0)
    return kcat, vcat


def _attn_fwd(q, kv, sinks, gathered):
    r = q.shape[0]
    nb = r // BLOCK
    n = len(gathered)

    def body(sink_ref, q_ref, kvc_ref, kvp_ref, kvm_ref, *rest):
        o_ref, lse_ref = rest[n:n + 2]
        dst = rest[n + 2:2 * n + 2]
        bias_ref, late_ref, send_sems, recv_sems = rest[2 * n + 2:]
        i = pl.program_id(0)

        @pl.when(i == 0)
        def _():
            for cp in _gather_ici(dst, send_sems, recv_sems)[0]:
                cp.start()
            _attn_bias_init(bias_ref, late_ref)

        lane = lax.broadcasted_iota(jnp.int32, (BLOCK, BLOCK), 1)
        lse_tile = jnp.zeros((BLOCK, BLOCK), F32)
        block_bias = _attn_block_bias(late_ref, i)
        prev_part = late_ref[...] > 0.5
        for g in range(N_KV):
            kcat, vcat = _kv_cat(kvm_ref, kvp_ref, kvc_ref, g)
            heads = range(g * GROUP, (g + 1) * GROUP)
            meta, band = _attn_logits(_dot_nt(_stack_heads(q_ref, g), kcat), bias_ref, block_bias, prev_part, g)
            sink = _head_rows([sink_ref[h] for h in heads])
            m = jnp.maximum(jnp.max(jnp.maximum(meta, band), axis=-1, keepdims=True), sink)
            p_meta, p_band = jnp.exp(meta - m), jnp.exp(band - m)
            l = jnp.sum(p_meta + p_band, axis=-1, keepdims=True) + jnp.exp(sink - m)
            o = _dot(_split_band(p_meta, p_band, prev_part).astype(BF16), vcat) * (1.0 / l)[:, 0:HEAD_DIM]
            lse = m + jnp.log(l)
            for j, h in enumerate(heads):
                o_ref[:, h * HEAD_DIM:(h + 1) * HEAD_DIM] = o[j * BLOCK:(j + 1) * BLOCK]
                lse_tile = jnp.where(lane == h, lse[j * BLOCK:(j + 1) * BLOCK], lse_tile)
        lse_ref[...] = lse_tile

        @pl.when(i == nb - 1)
        def _():
            sends, arrivals = _gather_ici(dst, send_sems, recv_sems)
            for cp in arrivals:
                cp.wait_recv()
            for cp in sends:
                cp.wait_send()

    return pl.pallas_call(
        body, name="attn_fwd", grid=(nb,),
        in_specs=[pl.BlockSpec(memory_space=pltpu.SMEM), _row(BLOCK, ATTN_W), _row(BLOCK, 2 * KV_W),
                  pl.BlockSpec((BLOCK, 2 * KV_W), lambda i: (jnp.maximum(i - 1, 0), 0)), _const((BLOCK, 2 * KV_W))] + [ANY] * n,
        out_specs=[_row(BLOCK, ATTN_W), _row(BLOCK, BLOCK)] + [ANY] * n,
        out_shape=[jax.ShapeDtypeStruct((r, ATTN_W), F32), jax.ShapeDtypeStruct((r, BLOCK), F32)]
        + [jax.ShapeDtypeStruct(g.shape, g.dtype) for g in gathered],
        input_output_aliases={5 + k: 2 + k for k in range(n)},
        scratch_shapes=[pltpu.VMEM((N_KV, GROUP * BLOCK, 2 * BLOCK), F32), pltpu.VMEM((GROUP * BLOCK, BLOCK), F32),
                        pltpu.SemaphoreType.DMA((3 * n,)), pltpu.SemaphoreType.DMA((3 * n,))],
        compiler_params=_cparams(),
    )(sinks, q, kv, kv, kv, *gathered)


def _shifted_copies(ub_ref, win):
    w = win.shape[0]
    ub_ref[0] = win
    for b in range(1, 8):
        ub_ref[b] = pltpu.roll(win, shift=w - b, axis=0)


def _conv_chunk(ub_ref, w_ref, r0, shifts):
    acc = jnp.zeros((CONV_CHUNK, CONV_W), F32)
    for j in range(CONV_K):
        a, b = divmod(shifts[j], 8)
        acc = acc + w_ref[j:j + 1, :] * ub_ref[b, pl.ds(r0 + 8 * a, CONV_CHUNK), :]
    return acc


FWD_SHIFTS = tuple(HALO - (CONV_K - 1) + j for j in range(CONV_K))
BWD_SHIFTS = tuple(CONV_K - 1 - j for j in range(CONV_K))


def _glu_window(cp_ref, c_ref, i):
    tile = c_ref[:, 0:CONV_W] * _sigmoid(c_ref[:, CONV_W:2 * CONV_W])
    halo = cp_ref[:, 0:CONV_W] * _sigmoid(cp_ref[:, CONV_W:2 * CONV_W])
    first = (jnp.zeros((HALO, CONV_W), jnp.int32) + i) == 0
    return jnp.concatenate([jnp.where(first, 0.0, halo), tile], axis=0)


def _halo_before(tm, n):
    return pl.BlockSpec((HALO, n), lambda i: (jnp.maximum(i * (tm // HALO) - 1, 0), 0))


def _conv_fwd(cacg, cw, cb, lg, lb, tm, gathered):
    r = cacg.shape[0]
    n = len(gathered)

    def body(c_ref, cp_ref, w_ref, cb_ref, lg_ref, lb_ref, *rest):
        o_ref, y_ref = rest[n:n + 2]
        dst = rest[n + 2:2 * n + 2]
        ub_ref, send_sems, recv_sems = rest[2 * n + 2:]
        i = pl.program_id(0)

        @pl.when(i == 0)
        def _():
            for cp in _gather_d2d(dst, send_sems, recv_sems)[0]:
                cp.start()

        _shifted_copies(ub_ref, _glu_window(cp_ref, c_ref, i))

        def chunk(ci, carry):
            r0 = pl.multiple_of(ci * CONV_CHUNK, CONV_CHUNK)
            y = _conv_chunk(ub_ref, w_ref, r0, FWD_SHIFTS) + cb_ref[...]
            yc = y - jnp.mean(y, axis=-1, keepdims=True)
            rs = lax.rsqrt(jnp.mean(yc * yc, axis=-1, keepdims=True) + NORM_EPS)
            yn = yc * rs * lg_ref[...] + lb_ref[...]
            o_ref[pl.ds(r0, CONV_CHUNK), :] = yn * _sigmoid(yn)
            y_ref[pl.ds(r0, CONV_CHUNK), :] = y
            return carry

        lax.fori_loop(0, tm // CONV_CHUNK, chunk, 0, unroll=4)

        @pl.when(i == r // tm - 1)
        def _():
            sends, arrivals = _gather_d2d(dst, send_sems, recv_sems)
            for cp in arrivals:
                cp.wait_recv()
            for cp in sends:
                cp.wait_send()

    return pl.pallas_call(
        body, name="conv_fwd", grid=(r // tm,),
        in_specs=[_row(tm, 2 * CONV_W), _halo_before(tm, 2 * CONV_W), _const((32, CONV_W)), _const((1, CONV_W)),
                  _const((1, CONV_W)), _const((1, CONV_W))] + [ANY] * n,
        out_specs=[_row(tm, CONV_W), _row(tm, CONV_W)] + [ANY] * n,
        out_shape=[jax.ShapeDtypeStruct((r, CONV_W), F32)] * 2 + [jax.ShapeDtypeStruct(g.shape, g.dtype) for g in gathered],
        input_output_aliases={6 + k: 2 + k for k in range(n)},
        scratch_shapes=[pltpu.VMEM((8, tm + HALO, CONV_W), F32), pltpu.SemaphoreType.DMA((3 * n,)), pltpu.SemaphoreType.DMA((3 * n,))],
        compiler_params=_cparams(),
    )(cacg, cacg, cw, cb, lg, lb, *gathered)


def _out_proj(oa, oc, h0, ga, gc, g2, w_out_b, tm):
    r = h0.shape[0]

    def body(oa_ref, oc_ref, h_ref, ga_ref, gc_ref, g2_ref, w_ref, h1_ref, hn2_ref):
        ma = _rms_fwd(oa_ref[...], ga_ref[...])[0].astype(BF16)
        mc = _rms_fwd(oc_ref[...], gc_ref[...])[0].astype(BF16)
        h1 = h_ref[...] + _dot(jnp.concatenate([ma, mc], axis=1), w_ref[...])
        h1_ref[...] = h1
        hn2_ref[...] = _rms_fwd(h1, g2_ref[...])[0].astype(BF16)

    return pl.pallas_call(
        body, name="out_proj", grid=(r // tm,),
        in_specs=[_row(tm, ATTN_W), _row(tm, CONV_W), _row(tm, D_MODEL), _const((1, ATTN_W)), _const((1, CONV_W)),
                  _const((1, D_MODEL)), _const((D_MODEL, D_MODEL))],
        out_specs=[_row(tm, D_MODEL), _row(tm, D_MODEL)],
        out_shape=[jax.ShapeDtypeStruct((r, D_MODEL), F32), jax.ShapeDtypeStruct((r, D_MODEL), BF16)],
        compiler_params=_cparams(),
    )(oa, oc, h0, ga, gc, g2, w_out_b)


def _ffn_fwd(hn2, h1, target, gf, wg_t, wu_t, wd_b, tm):
    r = h1.shape[0]
    seq = target.shape[0]
    n_sub = tm // BLOCK

    def body(hn_ref, h1_ref, *rest):
        t_refs = rest[:n_sub]
        gf_ref, wg_ref, wu_ref, wd_ref, gate_ref, up_ref, act_ref, dh2_ref, dh2b_ref, loss_ref, dgf_ref = rest[n_sub:]
        i = pl.program_id(0)

        @pl.when(i == 0)
        def _():
            loss_ref[...] = jnp.zeros_like(loss_ref)
            dgf_ref[...] = jnp.zeros_like(dgf_ref)

        hn = hn_ref[...]
        for cs in FF_CHUNKS:
            gate = _dot_nt(hn, wg_ref[cs, :])
            up = _dot_nt(hn, wu_ref[cs, :])
            gate_ref[:, cs] = gate.astype(BF16)
            up_ref[:, cs] = up.astype(BF16)
            act_ref[:, cs] = (gate * _sigmoid(gate) * up).astype(BF16)
        y, xhat, rstd = _rms_fwd(h1_ref[...] + _dot(act_ref[...], wd_ref[...]), gf_ref[...])
        rows = lax.broadcasted_iota(jnp.int32, (tm, D_MODEL), 0) + i * tm
        real = (rows >= BLOCK) & (rows < BLOCK + seq)
        err = jnp.where(real, y - jnp.concatenate([t[...] for t in t_refs], axis=0), 0.0)
        loss_ref[...] += jnp.sum(err * err) * (0.5 / D_MODEL)
        dy = err * (1.0 / D_MODEL)
        dh2, dg_rows = _rms_bwd(dy, xhat, rstd, gf_ref[...])
        dgf_ref[...] += jnp.sum(dg_rows, axis=0, keepdims=True)
        dh2_ref[...] = dh2
        dh2b_ref[...] = dh2.astype(BF16)

    def target_block(k):
        return pl.BlockSpec((BLOCK, D_MODEL), lambda i: (jnp.clip(n_sub * i - 1 + k, 0, seq // BLOCK - 1), 0))

    return pl.pallas_call(
        body, name="ffn_fwd", grid=(r // tm,),
        in_specs=[_row(tm, D_MODEL), _row(tm, D_MODEL)] + [target_block(k) for k in range(n_sub)]
        + [_const((1, D_MODEL))] + [_resident((D_FF, D_MODEL))] * 3,
        out_specs=[_row(tm, D_FF)] * 3 + [_row(tm, D_MODEL), _row(tm, D_MODEL), _const((1, BLOCK)), _const((1, D_MODEL))],
        out_shape=[jax.ShapeDtypeStruct((r, D_FF), BF16)] * 3
        + [jax.ShapeDtypeStruct((r, D_MODEL), F32), jax.ShapeDtypeStruct((r, D_MODEL), BF16),
           jax.ShapeDtypeStruct((1, BLOCK), F32), jax.ShapeDtypeStruct((1, D_MODEL), F32)],
        compiler_params=_cparams(),
    )(hn2, h1, *[target] * n_sub, gf, wg_t, wu_t, wd_b)


def _ffn_bwd(dh2, dh2b, gate, up, h1, g2, wg_t, wu_t, wd_b, tm):
    r = h1.shape[0]

    def body(dh2_ref, dh2b_ref, gate_ref, up_ref, h1_ref, g2_ref, wg_ref, wu_ref, wd_ref, dgate_ref, dup_ref, dh1_ref, dg2_ref):
        @pl.when(pl.program_id(0) == 0)
        def _():
            dg2_ref[...] = jnp.zeros_like(dg2_ref)

        dyb = dh2b_ref[...]
        for cs in FF_CHUNKS:
            dact = _dot_nt(dyb, wd_ref[cs, :])
            gate = gate_ref[:, cs].astype(F32)
            up = up_ref[:, cs].astype(F32)
            sg = _sigmoid(gate)
            dgate_ref[:, cs] = (dact * up * (sg * (1.0 + gate * (1.0 - sg)))).astype(BF16)
            dup_ref[:, cs] = (dact * (gate * sg)).astype(BF16)
        dhn = _dot(dgate_ref[...], wg_ref[...]) + _dot(dup_ref[...], wu_ref[...])
        _, xhat, rstd = _rms_fwd(h1_ref[...], g2_ref[...])
        dx, dg_rows = _rms_bwd(dhn, xhat, rstd, g2_ref[...])
        dg2_ref[...] += jnp.sum(dg_rows, axis=0, keepdims=True)
        dh1_ref[...] = dh2_ref[...] + dx

    return pl.pallas_call(
        body, name="ffn_bwd", grid=(r // tm,),
        in_specs=[_row(tm, D_MODEL), _row(tm, D_MODEL), _row(tm, D_FF), _row(tm, D_FF), _row(tm, D_MODEL), _const((1, D_MODEL))]
        + [_resident((D_FF, D_MODEL))] * 3,
        out_specs=[_row(tm, D_FF), _row(tm, D_FF), _row(tm, D_MODEL), _const((1, D_MODEL))],
        out_shape=[jax.ShapeDtypeStruct((r, D_FF), BF16), jax.ShapeDtypeStruct((r, D_FF), BF16),
                   jax.ShapeDtypeStruct((r, D_MODEL), F32), jax.ShapeDtypeStruct((1, D_MODEL), F32)],
        compiler_params=_cparams(),
    )(dh2, dh2b, gate, up, h1, g2, wg_t, wu_t, wd_b)


FF_HALF = D_FF // 2


def _ffn_wgrad_gu(hn2, dgate, dup, tk):
    r = hn2.shape[0]
    n_k = r // tk

    def body(hn_ref, dg_ref, du_ref, wg_ref, wu_ref, accg_ref, accu_ref):
        k = pl.program_id(1)

        @pl.when(k == 0)
        def _():
            accg_ref[...] = jnp.zeros_like(accg_ref)
            accu_ref[...] = jnp.zeros_like(accu_ref)

        hn = hn_ref[...]
        accg_ref[...] += _dot_tn(dg_ref[...], hn)
        accu_ref[...] += _dot_tn(du_ref[...], hn)

        @pl.when(k == n_k - 1)
        def _():
            wg_ref[...] = accg_ref[...].astype(BF16)
            wu_ref[...] = accu_ref[...].astype(BF16)

    col = pl.BlockSpec((tk, FF_HALF), lambda j, k: (k, j))
    out = pl.BlockSpec((FF_HALF, D_MODEL), lambda j, k: (j, 0))
    return pl.pallas_call(
        body, name="ffn_wgrad_gu", grid=(2, n_k),
        in_specs=[pl.BlockSpec((tk, D_MODEL), lambda j, k: (k, 0)), col, col],
        out_specs=[out, out],
        out_shape=[jax.ShapeDtypeStruct((D_FF, D_MODEL), BF16)] * 2,
        scratch_shapes=[pltpu.VMEM((FF_HALF, D_MODEL), F32)] * 2,
        compiler_params=_cparams(2),
    )(hn2, dgate, dup)


def _ffn_wgrad_d(act, dh2b, tk):
    r = act.shape[0]
    n_k = r // tk

    def body(a_ref, dy_ref, wd_ref, acc_ref):
        k = pl.program_id(1)

        @pl.when(k == 0)
        def _():
            acc_ref[...] = jnp.zeros_like(acc_ref)

        acc_ref[...] += _dot_tn(a_ref[...], dy_ref[...])

        @pl.when(k == n_k - 1)
        def _():
            wd_ref[...] = acc_ref[...].astype(BF16)

    return pl.pallas_call(
        body, name="ffn_wgrad_d", grid=(2, n_k),
        in_specs=[pl.BlockSpec((tk, FF_HALF), lambda j, k: (k, j)), pl.BlockSpec((tk, D_MODEL), lambda j, k: (k, 0))],
        out_specs=pl.BlockSpec((FF_HALF, D_MODEL), lambda j, k: (j, 0)),
        out_shape=jax.ShapeDtypeStruct((D_FF, D_MODEL), BF16),
        scratch_shapes=[pltpu.VMEM((FF_HALF, D_MODEL), F32)],
        compiler_params=_cparams(2),
    )(act, dh2b)


def _out_proj_bwd(dh1, oa, oc, ga, gc, w_out_b, tm):
    r = dh1.shape[0]

    def body(dh_ref, oa_ref, oc_ref, ga_ref, gc_ref, w_ref, doa_ref, doc_ref, dw_ref, dga_ref, dgc_ref, acc_ref):
        i = pl.program_id(0)

        @pl.when(i == 0)
        def _():
            acc_ref[...] = jnp.zeros_like(acc_ref)
            dga_ref[...] = jnp.zeros_like(dga_ref)
            dgc_ref[...] = jnp.zeros_like(dgc_ref)

        dhb = dh_ref[...].astype(BF16)
        dmix = _dot_nt(dhb, w_ref[...])
        ma, xa, ra = _rms_fwd(oa_ref[...], ga_ref[...])
        mc, xc, rc = _rms_fwd(oc_ref[...], gc_ref[...])
        acc_ref[...] += _dot_tn(jnp.concatenate([ma.astype(BF16), mc.astype(BF16)], axis=1), dhb)

        @pl.when(i == r // tm - 1)
        def _():
            dw_ref[...] = acc_ref[...].astype(BF16)

        doa, dga_rows = _rms_bwd(dmix[:, 0:ATTN_W], xa, ra, ga_ref[...])
        doc, dgc_rows = _rms_bwd(dmix[:, ATTN_W:ATTN_W + CONV_W], xc, rc, gc_ref[...])
        doa_ref[...] = doa
        doc_ref[...] = doc
        dga_ref[...] += jnp.sum(dga_rows, axis=0, keepdims=True)
        dgc_ref[...] += jnp.sum(dgc_rows, axis=0, keepdims=True)

    return pl.pallas_call(
        body, name="out_proj_bwd", grid=(r // tm,),
        in_specs=[_row(tm, D_MODEL), _row(tm, ATTN_W), _row(tm, CONV_W), _const((1, ATTN_W)), _const((1, CONV_W)),
                  _const((D_MODEL, D_MODEL))],
        out_specs=[_row(tm, ATTN_W), _row(tm, CONV_W), _const((D_MODEL, D_MODEL)), _const((1, ATTN_W)), _const((1, CONV_W))],
        out_shape=[jax.ShapeDtypeStruct((r, ATTN_W), F32), jax.ShapeDtypeStruct((r, CONV_W), F32),
                   jax.ShapeDtypeStruct((D_MODEL, D_MODEL), BF16), jax.ShapeDtypeStruct((1, ATTN_W), F32),
                   jax.ShapeDtypeStruct((1, CONV_W), F32)],
        scratch_shapes=[pltpu.VMEM((D_MODEL, D_MODEL), F32)],
        compiler_params=_cparams(),
    )(dh1, oa, oc, ga, gc, w_out_b)


def _conv_bwd_params(doc, y, cacg, lg, lb, tm, parts):
    r = cacg.shape[0]
    n_steps = r // tm
    n = len(parts)

    def body(do_ref, y_ref, c_ref, cp_ref, lg_ref, lb_ref, *rest):
        src = rest[:n]
        dy_ref, dcw_ref, dcb_ref, dlg_ref, dlb_ref = rest[n:n + 5]
        dst = rest[n + 5:2 * n + 5]
        ub_ref, accw_ref, send_sems, recv_sems = rest[2 * n + 5:]
        i = pl.program_id(0)

        @pl.when(i == 0)
        def _():
            for cp in _scatter(src, dst, send_sems, recv_sems):
                cp.start()
            accw_ref[...] = jnp.zeros_like(accw_ref)
            dcb_ref[...] = jnp.zeros_like(dcb_ref)
            dlg_ref[...] = jnp.zeros_like(dlg_ref)
            dlb_ref[...] = jnp.zeros_like(dlb_ref)

        _shifted_copies(ub_ref, _glu_window(cp_ref, c_ref, i))

        def chunk(ci, carry):
            r0 = pl.multiple_of(ci * CONV_CHUNK, CONV_CHUNK)
            y = y_ref[pl.ds(r0, CONV_CHUNK), :]
            yc = y - jnp.mean(y, axis=-1, keepdims=True)
            rs = lax.rsqrt(jnp.mean(yc * yc, axis=-1, keepdims=True) + NORM_EPS)
            xhat = yc * rs
            yn = xhat * lg_ref[...] + lb_ref[...]
            sg = _sigmoid(yn)
            dyn = do_ref[pl.ds(r0, CONV_CHUNK), :] * (sg * (1.0 + yn * (1.0 - sg)))
            dlg_ref[...] += jnp.sum(dyn * xhat, axis=0, keepdims=True)
            dlb_ref[...] += jnp.sum(dyn, axis=0, keepdims=True)
            dxh = dyn * lg_ref[...]
            dy = rs * (dxh - jnp.mean(dxh, axis=-1, keepdims=True) - xhat * jnp.mean(dxh * xhat, axis=-1, keepdims=True))
            dcb_ref[...] += jnp.sum(dy, axis=0, keepdims=True)
            dy_ref[pl.ds(r0, CONV_CHUNK), :] = dy
            for j in range(CONV_K):
                a, b = divmod(FWD_SHIFTS[j], 8)
                prod = dy * ub_ref[b, pl.ds(r0 + 8 * a, CONV_CHUNK), :]
                accw_ref[j] += jnp.sum(prod.reshape(CONV_CHUNK // 8, 8, CONV_W), axis=0)
            return carry

        lax.fori_loop(0, tm // CONV_CHUNK, chunk, 0, unroll=4)

        @pl.when(i == n_steps - 1)
        def _():
            for j in range(32):
                dcw_ref[j:j + 1, :] = jnp.sum(accw_ref[j], axis=0, keepdims=True)
            for cp in _scatter(src, dst, send_sems, recv_sems):
                cp.wait()

    vec = _const((1, CONV_W))
    return pl.pallas_call(
        body, name="conv_bwd_params", grid=(n_steps,),
        in_specs=[_row(tm, CONV_W), _row(tm, CONV_W), _row(tm, 2 * CONV_W), _halo_before(tm, 2 * CONV_W), vec, vec] + [ANY] * n,
        out_specs=[_row(tm, CONV_W), _const((32, CONV_W)), vec, vec, vec] + [ANY] * n,
        out_shape=[jax.ShapeDtypeStruct((r, CONV_W), F32), jax.ShapeDtypeStruct((32, CONV_W), F32)]
        + [jax.ShapeDtypeStruct((1, CONV_W), F32)] * 3 + _scatter_landing(parts),
        scratch_shapes=[pltpu.VMEM((8, tm + HALO, CONV_W), F32), pltpu.VMEM((32, 8, CONV_W), F32),
                        pltpu.SemaphoreType.DMA((7 * n,)), pltpu.SemaphoreType.DMA((7 * n,))],
        compiler_params=_cparams(),
    )(doc, y, cacg, cacg, lg, lb, *parts)


def _conv_bwd_data(dy, cacg, cw, tm):
    r = cacg.shape[0]
    n_steps = r // tm

    def body(dy_ref, dyn_ref, c_ref, w_ref, dc_ref, ub_ref):
        last = (jnp.zeros((HALO, CONV_W), jnp.int32) + pl.program_id(0)) == n_steps - 1
        win = jnp.concatenate([dy_ref[...], jnp.where(last, 0.0, dyn_ref[...])], axis=0)
        _shifted_copies(ub_ref, win)

        def chunk(ci, carry):
            r0 = pl.multiple_of(ci * CONV_CHUNK, CONV_CHUNK)
            du = _conv_chunk(ub_ref, w_ref, r0, BWD_SHIFTS)
            ca = c_ref[pl.ds(r0, CONV_CHUNK), 0:CONV_W]
            sg = _sigmoid(c_ref[pl.ds(r0, CONV_CHUNK), CONV_W:2 * CONV_W])
            dc_ref[pl.ds(r0, CONV_CHUNK), 0:CONV_W] = (du * sg).astype(BF16)
            dc_ref[pl.ds(r0, CONV_CHUNK), CONV_W:2 * CONV_W] = (du * ca * sg * (1.0 - sg)).astype(BF16)
            return carry

        lax.fori_loop(0, tm // CONV_CHUNK, chunk, 0, unroll=4)

    halo_after = pl.BlockSpec((HALO, CONV_W), lambda i: (jnp.minimum((i + 1) * (tm // HALO), r // HALO - 1), 0))
    return pl.pallas_call(
        body, name="conv_bwd_data", grid=(n_steps,),
        in_specs=[_row(tm, CONV_W), halo_after, _row(tm, 2 * CONV_W), _const((32, CONV_W))],
        out_specs=_row(tm, 2 * CONV_W),
        out_shape=jax.ShapeDtypeStruct((r, 2 * CONV_W), BF16),
        scratch_shapes=[pltpu.VMEM((8, tm + HALO, CONV_W), F32)],
        compiler_params=_cparams(),
    )(dy, dy, cacg, cw)


def _attn_bwd(q, kv, o, do, lse, sinks, parts):
    r = q.shape[0]
    nb = r // BLOCK
    n = len(parts)

    def body(sink_ref, q_ref, kvc_ref, kvp_ref, kvm_ref, o_ref, do_ref, lse_ref, *rest):
        src = rest[:n]
        dq_ref, dkv_ref, dmeta_ref, dsink_ref = rest[n:n + 4]
        dst = rest[n + 4:2 * n + 4]
        hold_ref, bias_ref, late_ref, send_sems, recv_sems = rest[2 * n + 4:]
        i = pl.program_id(0)

        @pl.when(i == 0)
        def _():
            for cp in _scatter(src, dst, send_sems, recv_sems):
                cp.start()
            _attn_bias_init(bias_ref, late_ref)
            dmeta_ref[...] = jnp.zeros_like(dmeta_ref)
            dsink_ref[...] = jnp.zeros_like(dsink_ref)
            hold_ref[...] = jnp.zeros_like(hold_ref)

        @pl.when(i < nb)
        def _():
            lane = lax.broadcasted_iota(jnp.int32, (BLOCK, BLOCK), 1)
            lse_tile = lse_ref[...]
            zero = jnp.zeros((BLOCK, BLOCK), F32)
            block_bias = _attn_block_bias(late_ref, i)
            prev_part = late_ref[...] > 0.5
            for g in range(N_KV):
                kcat, vcat = _kv_cat(kvm_ref, kvp_ref, kvc_ref, g)
                heads = range(g * GROUP, (g + 1) * GROUP)
                qs = _stack_heads(q_ref, g)
                dos = _stack_heads(do_ref, g)
                dosb = dos.astype(BF16)
                lse = jnp.concatenate(
                    [jnp.sum(jnp.where(lane == h, lse_tile, 0.0), axis=-1, keepdims=True) + zero for h in heads], axis=0)
                delta = jnp.sum(dos * _stack_heads(o_ref, g), axis=-1, keepdims=True) + jnp.zeros((GROUP * BLOCK, BLOCK), F32)
                band_bias = bias_ref[g, :, BLOCK:2 * BLOCK] + block_bias[1]
                bias = [bias_ref[g, :, 0:BLOCK] + block_bias[0], jnp.where(prev_part, band_bias, NEG), jnp.where(prev_part, NEG, band_bias)]
                s = _dot_nt(qs, kcat)
                dp = _dot_nt(dosb, vcat)
                ps = [jnp.exp(s[:, k * BLOCK:(k + 1) * BLOCK] * SCALE + bias[k] - lse) for k in range(3)]
                p = jnp.concatenate(ps, axis=1)
                ds = jnp.concatenate(
                    [(ps[k] * (dp[:, k * BLOCK:(k + 1) * BLOCK] - delta)) * SCALE for k in range(3)], axis=1).astype(BF16)
                sink_term = jnp.exp(_head_rows([sink_ref[h] for h in heads]) - lse)[:, 0:1] * delta[:, 0:1]
                dq = _dot(ds, kcat).astype(BF16)
                for j, h in enumerate(heads):
                    dsink_ref[h:h + 1, :] += -jnp.sum(sink_term[j * BLOCK:(j + 1) * BLOCK])
                    dq_ref[:, h * HEAD_DIM:(h + 1) * HEAD_DIM] = dq[j * BLOCK:(j + 1) * BLOCK]
                dk_t = _dot_tn(qs, ds)
                dv_t = _dot_tn(dosb, p.astype(BF16))
                ks = slice(g * HEAD_DIM, (g + 1) * HEAD_DIM)
                vs = slice(KV_W + g * HEAD_DIM, KV_W + (g + 1) * HEAD_DIM)
                for sl, grad_t in ((ks, dk_t), (vs, dv_t)):
                    dmeta_ref[:, sl] += grad_t[:, 0:BLOCK].T
                    dkv_ref[:, sl] = hold_ref[:, sl] + grad_t[:, BLOCK:2 * BLOCK].T
                    hold_ref[:, sl] = grad_t[:, 2 * BLOCK:3 * BLOCK].T

        @pl.when(i == nb)
        def _():
            dkv_ref[...] = hold_ref[...]
            for cp in _scatter(src, dst, send_sems, recv_sems):
                cp.wait()

    def cur(i):
        return jnp.minimum(i, nb - 1)

    return pl.pallas_call(
        body, name="attn_bwd", grid=(nb + 1,),
        in_specs=[pl.BlockSpec(memory_space=pltpu.SMEM),
                  pl.BlockSpec((BLOCK, ATTN_W), lambda i: (cur(i), 0)),
                  pl.BlockSpec((BLOCK, 2 * KV_W), lambda i: (cur(i), 0)),
                  pl.BlockSpec((BLOCK, 2 * KV_W), lambda i: (jnp.maximum(cur(i) - 1, 0), 0)),
                  _const((BLOCK, 2 * KV_W)),
                  pl.BlockSpec((BLOCK, ATTN_W), lambda i: (cur(i), 0)),
                  pl.BlockSpec((BLOCK, ATTN_W), lambda i: (cur(i), 0)),
                  pl.BlockSpec((BLOCK, BLOCK), lambda i: (cur(i), 0))] + [ANY] * n,
        out_specs=[pl.BlockSpec((BLOCK, ATTN_W), lambda i: (cur(i), 0)),
                   pl.BlockSpec((BLOCK, 2 * KV_W), lambda i: (jnp.maximum(i - 1, 0), 0)),
                   _const((BLOCK, 2 * KV_W)), _const((N_HEADS, BLOCK))] + [ANY] * n,
        out_shape=[jax.ShapeDtypeStruct((r, ATTN_W), BF16), jax.ShapeDtypeStruct((r, 2 * KV_W), F32),
                   jax.ShapeDtypeStruct((BLOCK, 2 * KV_W), F32), jax.ShapeDtypeStruct((N_HEADS, BLOCK), F32)] + _scatter_landing(parts),
        scratch_shapes=[pltpu.VMEM((BLOCK, 2 * KV_W), F32), pltpu.VMEM((N_KV, GROUP * BLOCK, 2 * BLOCK), F32),
                        pltpu.VMEM((GROUP * BLOCK, BLOCK), F32), pltpu.SemaphoreType.DMA((7 * n,)), pltpu.SemaphoreType.DMA((7 * n,))],
        compiler_params=_cparams(),
    )(sinks, q, kv, kv, kv, o, do, lse, *parts)


def _in_proj_bwd(dq, dkv, dkv_meta, dc, dh1, h0, g1, w_in_t, seq, tm):
    r = h0.shape[0]
    n_tiles = r // tm
    n_out = -(-seq // tm)

    def body(dq_ref, dkv_ref, dm_ref, dc_ref, dh1_ref, h_ref, g_ref, w_ref, gx_ref, lead_ref, dwt_ref, dg_ref, dw_ref, hold_ref):
        i = pl.program_id(0)

        @pl.when(i == 0)
        def _():
            dw_ref[...] = jnp.zeros_like(dw_ref)
            dg_ref[...] = jnp.zeros_like(dg_ref)

        @pl.when(i < n_tiles)
        def _():
            meta = jnp.concatenate([dm_ref[...], jnp.zeros((tm - BLOCK, 2 * KV_W), F32)], axis=0) if tm > BLOCK else dm_ref[...]
            first = (jnp.zeros((tm, 2 * KV_W), jnp.int32) + i) == 0
            dkvb = (dkv_ref[...] + jnp.where(first, meta, 0.0)).astype(BF16)
            hn, xhat, rstd = _rms_fwd(h_ref[...], g_ref[...])
            dproj = jnp.concatenate([dq_ref[...], dkvb, dc_ref[...]], axis=1)
            dhn = _dot(dproj, w_ref[...])
            dw_ref[...] += _dot_tn(dproj, hn.astype(BF16))
            dx, dg_rows = _rms_bwd(dhn, xhat, rstd, g_ref[...])
            dg_ref[...] += jnp.sum(dg_rows, axis=0, keepdims=True)
            dh0 = dh1_ref[...] + dx

            @pl.when(i == 0)
            def _():
                lead_ref[...] = dh0[0:BLOCK]

            @pl.when((i >= 1) & (i <= n_out))
            def _():
                gx_ref[0:tm - BLOCK, :] = hold_ref[...]
                gx_ref[tm - BLOCK:tm, :] = dh0[0:BLOCK]

            hold_ref[...] = dh0[BLOCK:tm]

        @pl.when((i == n_tiles) & (n_tiles <= n_out))
        def _():
            gx_ref[0:tm - BLOCK, :] = hold_ref[...]

        @pl.when(i == n_tiles - 1)
        def _():
            dwt_ref[...] = dw_ref[...].astype(BF16)

    def tile(n):
        return pl.BlockSpec((tm, n), lambda i: (jnp.minimum(i, n_tiles - 1), 0))

    return pl.pallas_call(
        body, name="in_proj_bwd", grid=(n_tiles + 1,),
        in_specs=[tile(ATTN_W), tile(2 * KV_W), _const((BLOCK, 2 * KV_W)), tile(2 * CONV_W), tile(D_MODEL), tile(D_MODEL),
                  _const((1, D_MODEL)), _const((IN_COLS, D_MODEL))],
        out_specs=[pl.BlockSpec((tm, D_MODEL), lambda i: (jnp.clip(i - 1, 0, n_out - 1), 0)), _const((BLOCK, D_MODEL)),
                   _const((IN_COLS, D_MODEL)), _const((1, D_MODEL))],
        out_shape=[jax.ShapeDtypeStruct((seq, D_MODEL), F32), jax.ShapeDtypeStruct((BLOCK, D_MODEL), F32),
                   jax.ShapeDtypeStruct((IN_COLS, D_MODEL), BF16), jax.ShapeDtypeStruct((1, D_MODEL), F32)],
        scratch_shapes=[pltpu.VMEM((IN_COLS, D_MODEL), F32), pltpu.VMEM((tm - BLOCK, D_MODEL), F32)],
        compiler_params=_cparams(),
    )(dq, dkv, dkv_meta, dc, dh1, h0, g1, w_in_t)


def _adamw_update(w_ref, g_ref, m_ref, v_ref, d_ref, nm_ref, nv_ref):
    g = g_ref[...]
    m = ADAM_B1 * m_ref[...] + (1.0 - ADAM_B1) * g
    v = ADAM_B2 * v_ref[...] + (1.0 - ADAM_B2) * (g * g)
    m_hat = m / (1.0 - ADAM_B1 ** ADAM_STEP)
    v_hat = v / (1.0 - ADAM_B2 ** ADAM_STEP)
    d_ref[...] = -ADAM_LR * (m_hat / (jnp.sqrt(v_hat) + ADAM_EPS) + ADAM_WD * w_ref[...])
    nm_ref[...] = m
    nv_ref[...] = v


def _adamw(w, g, m, v, name):
    rows, cols = w.shape
    tr = rows
    for cand in (256, 176, 128, 64, 32, 16, 8):
        if rows % cand == 0:
            tr = cand
            break

    def body(*refs):
        _adamw_update(*refs)

    spec = _row(tr, cols)
    return pl.pallas_call(
        body, name=name, grid=(rows // tr,), in_specs=[spec] * 4, out_specs=[spec] * 3,
        out_shape=[jax.ShapeDtypeStruct((rows, cols), F32)] * 3, compiler_params=_cparams(),
    )(w, g, m, v)


MESH = pl.DeviceIdType.MESH
ANY = pl.BlockSpec(memory_space=pl.ANY)


def _place():
    x, y, c = lax.axis_index("x"), lax.axis_index("y"), lax.axis_index("c")
    chips = [(1 - x, y), (x, 1 - y), (1 - x, 1 - y)]
    return x, y, c, chips


def _gather_ici(dst, send_sems, recv_sems):
    x, y, c, chips = _place()
    sends, arrivals = [], []
    for k in range(len(dst)):
        rows = dst[k].shape[1] // 2
        half = pl.ds(c * rows, rows)
        mine = dst[k].at[2 * x + y, half]
        for p, chip in enumerate(chips):
            sems = dict(send_sem=send_sems.at[3 * k + p], recv_sem=recv_sems.at[3 * k + p], device_id=(chip[0], chip[1], c),
                        device_id_type=MESH)
            sends.append(pltpu.make_async_remote_copy(src_ref=mine, dst_ref=mine, **sems))
            theirs = dst[k].at[2 * chip[0] + chip[1], half]
            arrivals.append(pltpu.make_async_remote_copy(src_ref=theirs, dst_ref=theirs, **sems))
    return sends, arrivals


def _gather_d2d(dst, send_sems, recv_sems):
    x, y, c, chips = _place()
    sends, arrivals = [], []
    for k in range(len(dst)):
        rows = dst[k].shape[1] // 2
        for p, chip in enumerate(chips):
            sems = dict(send_sem=send_sems.at[3 * k + p], recv_sem=recv_sems.at[3 * k + p], device_id=(x, y, 1 - c),
                        device_id_type=MESH)
            mine = dst[k].at[2 * chip[0] + chip[1], pl.ds(c * rows, rows)]
            sends.append(pltpu.make_async_remote_copy(src_ref=mine, dst_ref=mine, **sems))
            theirs = dst[k].at[2 * chip[0] + chip[1], pl.ds((1 - c) * rows, rows)]
            arrivals.append(pltpu.make_async_remote_copy(src_ref=theirs, dst_ref=theirs, **sems))
    return sends, arrivals


def _own_slots(shard, dtype):
    return jnp.broadcast_to(shard[None], (N_SHARD,) + shard.shape).astype(dtype)


def _gather_weights(slots):
    n = len(slots)

    def body(*refs):
        dst = refs[n:2 * n]
        ici_send, ici_recv, d2d_send, d2d_recv = refs[2 * n:]
        sends, arrivals = _gather_ici(dst, ici_send, ici_recv)
        for cp in sends:
            cp.start()
        for cp in arrivals:
            cp.wait_recv()
        forwards, from_sibling = _gather_d2d(dst, d2d_send, d2d_recv)
        for cp in forwards:
            cp.start()
        for cp in from_sibling:
            cp.wait_recv()
        for cp in sends + forwards:
            cp.wait_send()

    return pl.pallas_call(
        body, name="gather_weights",
        in_specs=[ANY] * n, out_specs=[ANY] * n,
        out_shape=[jax.ShapeDtypeStruct(s.shape, s.dtype) for s in slots],
        input_output_aliases={k: k for k in range(n)},
        scratch_shapes=[pltpu.SemaphoreType.DMA((3 * n,))] * 4,
    )(*slots)


VMEM_WHOLE = pl.BlockSpec(memory_space=pltpu.VMEM)


def _allreduce_small(parts, grads):
    widths = sorted({p.shape[1] for p in parts})
    place, heights = [], [0] * len(widths)
    for p in parts:
        gi = widths.index(p.shape[1])
        place.append((gi, heights[gi]))
        heights[gi] += -(-p.shape[0] // 8) * 8
    n, ng, nb = len(parts), len(widths), len(grads)

    def body(*refs):
        ins, big_src = refs[:n], refs[n:n + nb]
        outs, big_dst = refs[n + nb:2 * n + nb], refs[2 * n + nb:2 * (n + nb)]
        slots = refs[2 * (n + nb):2 * (n + nb) + ng]
        send_sems, recv_sems, big_send, big_recv = refs[2 * (n + nb) + ng:]
        scattered = _scatter(big_src, big_dst, big_send, big_recv)
        for cp in scattered:
            cp.start()
        x, y, c = lax.axis_index("x"), lax.axis_index("y"), lax.axis_index("c")
        me = 4 * x + 2 * y + c
        for gi in range(ng):
            slots[gi][me] = jnp.zeros((heights[gi], widths[gi]), F32)
        for k, (gi, r0) in enumerate(place):
            slots[gi][me, r0:r0 + parts[k].shape[0], :] = ins[k][...]

        def copy(gi, j, arriving):
            peer = ((x + (j >> 2)) % 2, (y + ((j >> 1) & 1)) % 2, (c + (j & 1)) % 2)
            slot = 4 * peer[0] + 2 * peer[1] + peer[2] if arriving else me
            return pltpu.make_async_remote_copy(
                src_ref=slots[gi].at[me], dst_ref=slots[gi].at[slot], send_sem=send_sems.at[7 * gi + j - 1],
                recv_sem=recv_sems.at[7 * gi + j - 1], device_id=peer, device_id_type=MESH)

        pairs = [(gi, j) for gi in range(ng) for j in range(1, N_DEV)]
        for gi, j in pairs:
            copy(gi, j, False).start()
        for gi, j in pairs:
            copy(gi, j, True).wait_recv()
        totals = []
        for gi in range(ng):
            total = slots[gi][0]
            for d in range(1, N_DEV):
                total = total + slots[gi][d]
            totals.append(total)
        for k, (gi, r0) in enumerate(place):
            outs[k][...] = totals[gi][r0:r0 + parts[k].shape[0], :]
        for gi, j in pairs:
            copy(gi, j, False).wait_send()
        for cp in scattered:
            cp.wait()

    out = pl.pallas_call(
        body, name="allreduce_small", in_specs=[VMEM_WHOLE] * n + [ANY] * nb, out_specs=[VMEM_WHOLE] * n + [ANY] * nb,
        out_shape=[jax.ShapeDtypeStruct(p.shape, F32) for p in parts] + _scatter_landing(grads),
        scratch_shapes=[pltpu.VMEM((N_DEV, heights[gi], widths[gi]), F32) for gi in range(ng)]
        + [pltpu.SemaphoreType.DMA((7 * ng,)), pltpu.SemaphoreType.DMA((7 * ng,)),
           pltpu.SemaphoreType.DMA((7 * nb,)), pltpu.SemaphoreType.DMA((7 * nb,))],
    )(*parts, *grads)
    return out[:n], out[n:]


def _adamw_small(ws, gs, ms, vs):
    n = len(ws)

    def body(*refs):
        for k in range(n):
            w_ref, g_ref, m_ref, v_ref = (refs[j * n + k] for j in range(4))
            _adamw_update(w_ref, g_ref, m_ref, v_ref, *(refs[(4 + j) * n + k] for j in range(3)))

    shapes = [jax.ShapeDtypeStruct(w.shape, F32) for w in ws]
    out = pl.pallas_call(
        body, name="adamw_small", in_specs=[VMEM_WHOLE] * (4 * n), out_specs=[VMEM_WHOLE] * (3 * n), out_shape=shapes * 3,
    )(*ws, *gs, *ms, *vs)
    return out[:n], out[n:2 * n], out[2 * n:]


def _scatter(src, dst, send_sems, recv_sems):
    x, y, c = lax.axis_index("x"), lax.axis_index("y"), lax.axis_index("c")
    copies = []
    for k in range(len(src)):
        rows = src[k].shape[1] // 2
        for j in range(1, N_DEV):
            px, py, pc = (x + (j >> 2)) % 2, (y + ((j >> 1) & 1)) % 2, (c + (j & 1)) % 2
            copies.append(pltpu.make_async_remote_copy(
                src_ref=src[k].at[2 * px + py, pl.ds(pc * rows, rows)], dst_ref=dst[k].at[j - 1],
                send_sem=send_sems.at[7 * k + j - 1], recv_sem=recv_sems.at[7 * k + j - 1], device_id=(px, py, pc),
                device_id_type=MESH))
    return copies


def _scatter_landing(parts):
    return [jax.ShapeDtypeStruct((N_DEV - 1, p.shape[1] // 2, p.shape[2]), p.dtype) for p in parts]


def _sum_pieces(own, landed):
    n = len(own)

    def body(*refs):
        for k in range(n):
            got = refs[n + k]
            total = refs[k][...].astype(F32)
            for j in range(N_DEV - 1):
                total = total + got[j].astype(F32)
            refs[2 * n + k][...] = total

    in_specs, out_specs = [], []
    for o in own:
        in_specs.append(_row(o.shape[0] // 2, o.shape[1]))
    for o in own:
        in_specs.append(pl.BlockSpec((N_DEV - 1, o.shape[0] // 2, o.shape[1]), lambda i: (0, i, 0)))
        out_specs.append(_row(o.shape[0] // 2, o.shape[1]))
    return pl.pallas_call(
        body, name="sum_pieces", grid=(2,), in_specs=in_specs, out_specs=out_specs,
        out_shape=[jax.ShapeDtypeStruct(o.shape, F32) for o in own], compiler_params=_cparams(),
    )(*own, *landed)


def _swap_with_sibling(halves):
    n = len(halves)

    def body(*refs):
        x, y, c = lax.axis_index("x"), lax.axis_index("y"), lax.axis_index("c")
        copies = [pltpu.make_async_remote_copy(
            src_ref=refs[k], dst_ref=refs[n + k], send_sem=refs[2 * n].at[k], recv_sem=refs[2 * n + 1].at[k],
            device_id=(x, y, 1 - c), device_id_type=MESH) for k in range(n)]
        for cp in copies:
            cp.start()
        for cp in copies:
            cp.wait()

    return pl.pallas_call(
        body, name="swap_with_sibling", in_specs=[ANY] * n, out_specs=[ANY] * n,
        out_shape=[jax.ShapeDtypeStruct(h.shape, h.dtype) for h in halves],
        scratch_shapes=[pltpu.SemaphoreType.DMA((n,)), pltpu.SemaphoreType.DMA((n,))],
    )(*halves)


def _own_piece(part):
    rows = part.shape[1] // 2
    s = 2 * lax.axis_index("x") + lax.axis_index("y")
    return lax.dynamic_slice(part, (s, lax.axis_index("c") * rows, 0), (1, rows, part.shape[2]))[0]


def _both_halves(mine, theirs):
    south = lax.axis_index("c") == 0
    return jnp.concatenate([jnp.where(south, mine, theirs), jnp.where(south, theirs, mine)], axis=0)


def _from_col_shards(g):
    return g.transpose(1, 0, 2).reshape(g.shape[1], -1)


def kernel(x, meta_tokens, attn_norm_g, w_in, attn_sinks, conv_w, conv_b, conv_ln_g, conv_ln_b, attn_out_g, conv_out_g, w_out, ffn_norm_g, w_gate, w_up, w_down, final_norm_g, loss_target, m_meta_tokens, m_attn_norm_g, m_w_in, m_attn_sinks, m_conv_w, m_conv_b, m_conv_ln_g, m_conv_ln_b, m_attn_out_g, m_conv_out_g, m_w_out, m_ffn_norm_g, m_w_gate, m_w_up, m_w_down, m_final_norm_g, v_meta_tokens, v_attn_norm_g, v_w_in, v_attn_sinks, v_conv_w, v_conv_b, v_conv_ln_g, v_conv_ln_b, v_attn_out_g, v_conv_out_g, v_w_out, v_ffn_norm_g, v_w_gate, v_w_up, v_w_down, v_final_norm_g):
    seq = x.shape[1]
    r = -(-(seq + BLOCK) // ROW_QUANTUM) * ROW_QUANTUM
    tm_wide = 768 if seq >= 768 else 256
    tk_ffn = 1408 if r % 1408 == 0 else 768
    shard = 2 * lax.axis_index("x") + lax.axis_index("y")

    conv_w32 = jnp.pad(conv_w[0], ((0, 1), (0, 0)))
    small_shard = jnp.concatenate([meta_tokens, conv_w32.reshape(16, 256)], axis=0)
    g_in, g_small = _gather_weights([_own_slots(w_in[0].T, BF16), _own_slots(small_shard, F32)])
    later = [_own_slots(w, BF16) for w in (w_gate[0].T, w_up[0].T, w_out[0], w_down[0])]
    w_in_t = g_in.reshape(IN_COLS, D_MODEL)
    meta_full = _from_col_shards(g_small[:, 0:N_META])
    cw_full = _from_col_shards(g_small[:, N_META:].reshape(N_SHARD, 32, 128))

    g1, ga, gc, g2 = attn_norm_g, attn_out_g, conv_out_g, ffn_norm_g
    gf = final_norm_g.reshape(1, D_MODEL)
    sinks = attn_sinks[0]

    lead = jnp.concatenate([jnp.zeros((LEAD, D_MODEL), F32), meta_full], axis=0)
    h0, q, kv, cacg = _in_proj(x[0], lead, g1, w_in_t, r, 768)
    oa, lse, *gathered = _attn_fwd(q, kv, sinks, later)
    oc, yc, g_gate, g_up, g_out, g_down = _conv_fwd(cacg, cw_full, conv_b, conv_ln_g, conv_ln_b, 768, gathered)
    wg_t, wu_t, wd_b = g_gate.reshape(D_FF, D_MODEL), g_up.reshape(D_FF, D_MODEL), g_down.reshape(D_FF, D_MODEL)
    w_out_b = g_out.reshape(D_MODEL, D_MODEL)
    h1, hn2 = _out_proj(oa, oc, h0, ga, gc, g2, w_out_b, 768)
    gate, up, act, dh2, dh2b, loss_p, dgf = _ffn_fwd(hn2, h1, loss_target[0], gf, wg_t, wu_t, wd_b, 384)

    def by_shard(dw):
        return dw.reshape(N_SHARD, dw.shape[0] // N_SHARD, D_MODEL)

    dgate, dup, dh1, dg2 = _ffn_bwd(dh2, dh2b, gate, up, h1, g2, wg_t, wu_t, wd_b, 384)
    p_gate, p_up = [by_shard(dw) for dw in _ffn_wgrad_gu(hn2, dgate, dup, tk_ffn)]
    p_down = by_shard(_ffn_wgrad_d(act, dh2b, tk_ffn))
    doa, doc, dwo, dga, dgc = _out_proj_bwd(dh1, oa, oc, ga, gc, w_out_b, 768)
    p_out = by_shard(dwo)
    dy, dcw, dcb, dlg, dlb, l_gate, l_up = _conv_bwd_params(doc, yc, cacg, conv_ln_g, conv_ln_b, 768, [p_gate, p_up])
    dc = _conv_bwd_data(dy, cacg, cw_full, 768)
    dq, dkv, dkv_meta, dsink, l_out, l_down = _attn_bwd(q, kv, oa, doa, lse, sinks, [p_out, p_down])
    grad_x, dlead, dwi_t, dg1 = _in_proj_bwd(dq, dkv, dkv_meta, dc, dh1, h0, g1, w_in_t, seq, tm_wide)
    p_in = by_shard(dwi_t)

    red_names = ("final_norm_g", "attn_norm_g", "ffn_norm_g", "meta_tokens", "attn_out_g", "conv_out_g", "conv_b", "conv_ln_g",
                 "conv_ln_b", "conv_w", "loss", "attn_sinks")
    sums, (l_in,) = _allreduce_small([dgf, dg1, dg2, dlead[LEAD:BLOCK], dga, dgc, dcb, dlg, dlb, dcw, loss_p, dsink], [p_in])
    red = dict(zip(red_names, sums))
    loss = red["loss"][0, 0]
    red["final_norm_g"] = red["final_norm_g"].reshape(D_MODEL)
    red["attn_sinks"] = red["attn_sinks"][:, 0].reshape(1, N_HEADS)
    g_meta = lax.dynamic_slice_in_dim(red["meta_tokens"], shard * (D_MODEL // N_SHARD), D_MODEL // N_SHARD, axis=1)
    g_convw = lax.dynamic_slice_in_dim(red["conv_w"][0:CONV_K], shard * (CONV_W // N_SHARD), CONV_W // N_SHARD, axis=1)[None]

    halves = _sum_pieces([_own_piece(p) for p in (p_in, p_gate, p_up, p_out, p_down)], [l_in, l_gate, l_up, l_out, l_down])
    big = ("w_in", "w_gate", "w_up", "w_out", "w_down")
    transposed = ("w_in", "w_gate", "w_up")
    g_big = {name: _both_halves(mine, theirs) for name, mine, theirs in zip(big, halves, _swap_with_sibling(halves))}

    grads = {
        "meta_tokens": g_meta, "attn_norm_g": red["attn_norm_g"], "attn_sinks": red["attn_sinks"],
        "conv_w": g_convw, "conv_b": red["conv_b"], "conv_ln_g": red["conv_ln_g"], "conv_ln_b": red["conv_ln_b"],
        "attn_out_g": red["attn_out_g"], "conv_out_g": red["conv_out_g"], "ffn_norm_g": red["ffn_norm_g"],
        "final_norm_g": red["final_norm_g"]}
    params = {
        "meta_tokens": (meta_tokens, m_meta_tokens, v_meta_tokens), "attn_norm_g": (attn_norm_g, m_attn_norm_g, v_attn_norm_g),
        "w_in": (w_in, m_w_in, v_w_in), "attn_sinks": (attn_sinks, m_attn_sinks, v_attn_sinks), "conv_w": (conv_w, m_conv_w, v_conv_w),
        "conv_b": (conv_b, m_conv_b, v_conv_b), "conv_ln_g": (conv_ln_g, m_conv_ln_g, v_conv_ln_g),
        "conv_ln_b": (conv_ln_b, m_conv_ln_b, v_conv_ln_b), "attn_out_g": (attn_out_g, m_attn_out_g, v_attn_out_g),
        "conv_out_g": (conv_out_g, m_conv_out_g, v_conv_out_g), "w_out": (w_out, m_w_out, v_w_out),
        "ffn_norm_g": (ffn_norm_g, m_ffn_norm_g, v_ffn_norm_g), "w_gate": (w_gate, m_w_gate, v_w_gate), "w_up": (w_up, m_w_up, v_w_up),
        "w_down": (w_down, m_w_down, v_w_down), "final_norm_g": (final_norm_g, m_final_norm_g, v_final_norm_g)}
    names = list(params)
    delta, new_m, new_v = {}, {}, {}
    for name in big:
        flip = (lambda a: a.T) if name in transposed else (lambda a: a)
        w, m, v = params[name]
        outs = _adamw(flip(w[0]), g_big[name], flip(m[0]), flip(v[0]), "adamw_" + name)
        grads[name], delta[name], new_m[name], new_v[name] = [flip(a)[None] for a in (g_big[name], *outs)]
    rest = [name for name in names if name not in big]

    def rows_of(a):
        return a.reshape(-1, a.shape[-1])

    small = _adamw_small([rows_of(params[n][0]) for n in rest], [rows_of(grads[n]) for n in rest],
                         [rows_of(params[n][1]) for n in rest], [rows_of(params[n][2]) for n in rest])
    for dst, outs in zip((delta, new_m, new_v), small):
        for name, out in zip(rest, outs):
            dst[name] = out.reshape(params[name][0].shape)

    return (loss, grad_x[None], *[grads[n] for n in names], *[delta[n] for n in names], *[new_m[n] for n in names],
            *[new_v[n] for n in names])
```

```python
import functools
import math

import jax
import jax.numpy as jnp
from jax import lax
from jax.experimental import pallas as pl
from jax.experimental.pallas import tpu as pltpu

F32 = jnp.float32
BF16 = jnp.bfloat16

D_MODEL = 1024
N_META = 16
ATTN_W = 512
CONV_W = 512
HEAD_DIM = 64
N_HEADS = 8
N_KV = 2
GROUP = N_HEADS // N_KV
KV_W = N_KV * HEAD_DIM
BLOCK = 128
LEAD = BLOCK - N_META
CONV_K = 31
D_FF = 2816
IN_COLS = ATTN_W + 2 * KV_W + 2 * CONV_W
Q0, KV0, C0 = 0, ATTN_W, ATTN_W + 2 * KV_W
NORM_EPS = 1e-5
SCALE = 1.0 / math.sqrt(HEAD_DIM)
SLOPES = tuple(2.0 ** (-(8.0 / N_HEADS) * (h + 1)) for h in range(N_HEADS))
NEG = -1e30

ADAM_LR, ADAM_B1, ADAM_B2, ADAM_EPS, ADAM_WD, ADAM_STEP = 0.001, 0.9, 0.999, 1e-08, 0.01, 10

N_SHARD = 4
N_DEV = 8
ROW_QUANTUM = 768
HALO = 32
CONV_CHUNK = 32
FF_CHUNK = 256
FF_CHUNKS = tuple(slice(c, c + FF_CHUNK) for c in range(0, D_FF, FF_CHUNK))
VMEM_LIMIT = 60 * 1024 * 1024


def _cparams(n_axes=1):
    return pltpu.CompilerParams(dimension_semantics=("arbitrary",) * n_axes, vmem_limit_bytes=VMEM_LIMIT)


def _dot(a, b):
    return jnp.dot(a, b, preferred_element_type=F32)


def _dot_nt(a, b):
    return lax.dot_general(a, b, (((1,), (1,)), ((), ())), preferred_element_type=F32)


def _dot_tn(a, b):
    return lax.dot_general(a, b, (((0,), (0,)), ((), ())), preferred_element_type=F32)


def _sigmoid(x):
    return 1.0 / (1.0 + jnp.exp(-x))


def _row(tm, n):
    return pl.BlockSpec((tm, n), lambda i: (i, 0))


def _const(shape):
    return pl.BlockSpec(shape, lambda i: (0,) * len(shape))


def _resident(shape):
    return pl.BlockSpec(shape, lambda i: (0,) * len(shape), pipeline_mode=pl.Buffered(1))


def _rms_fwd(x, g):
    rstd = lax.rsqrt(jnp.mean(x * x, axis=-1, keepdims=True) + NORM_EPS)
    xhat = x * rstd
    return xhat * g, xhat, rstd


def _rms_bwd(dy, xhat, rstd, g):
    dxh = dy * g
    dx = rstd * (dxh - xhat * jnp.mean(dxh * xhat, axis=-1, keepdims=True))
    return dx, dy * xhat


def _in_proj(x, lead, g1, w_in_t, r, tm):
    seq = x.shape[0]
    n_sub = tm // BLOCK

    def body(*refs):
        x_refs = refs[:n_sub]
        lead_ref, g_ref, w_ref, h0_ref, q_ref, kv_ref, c_ref = refs[n_sub:]
        i = pl.program_id(0)
        pieces = []
        for k, x_ref in enumerate(x_refs):
            at = jnp.zeros((BLOCK, D_MODEL), jnp.int32) + (i * tm + (k - 1) * BLOCK)
            piece = jnp.where((at >= 0) & (at < seq), x_ref[...], 0.0)
            pieces.append(jnp.where(at < 0, lead_ref[...], piece) if k == 0 else piece)
        h = jnp.concatenate(pieces, axis=0)
        h0_ref[...] = h
        hn = _rms_fwd(h, g_ref[...])[0].astype(BF16)
        q_ref[...] = _dot_nt(hn, w_ref[Q0:KV0, :]).astype(BF16)
        kv_ref[...] = _dot_nt(hn, w_ref[KV0:C0, :]).astype(BF16)
        c_ref[...] = _dot_nt(hn, w_ref[C0:IN_COLS, :])

    def x_block(k):
        return pl.BlockSpec((BLOCK, D_MODEL), lambda i: (jnp.clip(n_sub * i - 1 + k, 0, seq // BLOCK - 1), 0))

    return pl.pallas_call(
        body, name="in_proj", grid=(r // tm,),
        in_specs=[x_block(k) for k in range(n_sub)] + [_const((BLOCK, D_MODEL)), _const((1, D_MODEL)), _const((IN_COLS, D_MODEL))],
        out_specs=[_row(tm, D_MODEL), _row(tm, ATTN_W), _row(tm, 2 * KV_W), _row(tm, 2 * CONV_W)],
        out_shape=[jax.ShapeDtypeStruct((r, D_MODEL), F32), jax.ShapeDtypeStruct((r, ATTN_W), BF16),
                   jax.ShapeDtypeStruct((r, 2 * KV_W), BF16), jax.ShapeDtypeStruct((r, 2 * CONV_W), F32)],
        compiler_params=_cparams(),
    )(*[x] * n_sub, lead, g1, w_in_t)


def _attn_bias_init(bias_ref, late_ref):
    row = lax.broadcasted_iota(jnp.int32, (GROUP * BLOCK, BLOCK), 0) & (BLOCK - 1)
    col = lax.broadcasted_iota(jnp.int32, (GROUP * BLOCK, BLOCK), 1)
    late_ref[...] = jnp.where(col > row, 1.0, 0.0)
    for g in range(N_KV):
        slope = jnp.concatenate([jnp.zeros((BLOCK, BLOCK), F32) + SLOPES[g * GROUP + j] for j in range(GROUP)], axis=0)
        bias_ref[g, :, 0:BLOCK] = jnp.where(col >= LEAD, 0.0, NEG)
        bias_ref[g, :, BLOCK:2 * BLOCK] = -slope * jnp.where(col > row, row - col + BLOCK, row - col).astype(F32)


def _attn_block_bias(late_ref, i):
    late = late_ref[...]
    meta0 = jnp.where(i == 0, NEG, 0.0)
    no_prev = jnp.where(i >= 2, 0.0, NEG)
    no_cur = jnp.where(i >= 1, 0.0, NEG)
    return late * meta0, late * no_prev + no_cur


def _attn_logits(s3, bias_ref, block_bias, prev_part, g):
    meta = s3[:, 0:BLOCK] * SCALE + (bias_ref[g, :, 0:BLOCK] + block_bias[0])
    band = jnp.where(prev_part, s3[:, BLOCK:2 * BLOCK], s3[:, 2 * BLOCK:3 * BLOCK]) * SCALE + (bias_ref[g, :, BLOCK:2 * BLOCK] + block_bias[1])
    return meta, band


def _split_band(meta, band, prev_part):
    return jnp.concatenate([meta, jnp.where(prev_part, band, 0.0), jnp.where(prev_part, 0.0, band)], axis=1)


def _head_rows(vals):
    return jnp.concatenate([jnp.zeros((BLOCK, BLOCK), F32) + v for v in vals], axis=0)


def _stack_heads(ref, g):
    return jnp.concatenate([ref[:, (g * GROUP + j) * HEAD_DIM:(g * GROUP + j + 1) * HEAD_DIM] for j in range(GROUP)], axis=0)


def _kv_cat(kvm_ref, kvp_ref, kvc_ref, g):
    ks = slice(g * HEAD_DIM, (g + 1) * HEAD_DIM)
    vs = slice(KV_W + g * HEAD_DIM, KV_W + (g + 1) * HEAD_DIM)
    kcat = jnp.concatenate([kvm_ref[:, ks], kvp_ref[:, ks], kvc_ref[:, ks]], axis=0)
    vcat = jnp.concatenate([kvm_ref[:, vs], kvp_ref[:, vs], kvc_ref[:, vs]], axis=0)
    return kcat, vcat


def _attn_fwd(q, kv, sinks, gathered):
    r = q.shape[0]
    nb = r // BLOCK
    n = len(gathered)

    def body(sink_ref, q_ref, kvc_ref, kvp_ref, kvm_ref, *rest):
        o_ref, lse_ref = rest[n:n + 2]
        dst = rest[n + 2:2 * n + 2]
        bias_ref, late_ref, send_sems, recv_sems = rest[2 * n + 2:]
        i = pl.program_id(0)

        @pl.when(i == 0)
        def _():
            for cp in _gather_ici(dst, send_sems, recv_sems)[0]:
                cp.start()
            _attn_bias_init(bias_ref, late_ref)

        lane = lax.broadcasted_iota(jnp.int32, (BLOCK, BLOCK), 1)
        lse_tile = jnp.zeros((BLOCK, BLOCK), F32)
        block_bias = _attn_block_bias(late_ref, i)
        prev_part = late_ref[...] > 0.5
        for g in range(N_KV):
            kcat, vcat = _kv_cat(kvm_ref, kvp_ref, kvc_ref, g)
            heads = range(g * GROUP, (g + 1) * GROUP)
            meta, band = _attn_logits(_dot_nt(_stack_heads(q_ref, g), kcat), bias_ref, block_bias, prev_part, g)
            sink = _head_rows([sink_ref[h] for h in heads])
            m = jnp.maximum(jnp.max(jnp.maximum(meta, band), axis=-1, keepdims=True), sink)
            p_meta, p_band = jnp.exp(meta - m), jnp.exp(band - m)
            l = jnp.sum(p_meta + p_band, axis=-1, keepdims=True) + jnp.exp(sink - m)
            o = _dot(_split_band(p_meta, p_band, prev_part).astype(BF16), vcat) * (1.0 / l)[:, 0:HEAD_DIM]
            lse = m + jnp.log(l)
            for j, h in enumerate(heads):
                o_ref[:, h * HEAD_DIM:(h + 1) * HEAD_DIM] = o[j * BLOCK:(j + 1) * BLOCK]
                lse_tile = jnp.where(lane == h, lse[j * BLOCK:(j + 1) * BLOCK], lse_tile)
        lse_ref[...] = lse_tile

        @pl.when(i == nb - 1)
        def _():
            sends, arrivals = _gather_ici(dst, send_sems, recv_sems)
            for cp in arrivals:
                cp.wait_recv()
            for cp in sends:
                cp.wait_send()

    return pl.pallas_call(
        body, name="attn_fwd", grid=(nb,),
        in_specs=[pl.BlockSpec(memory_space=pltpu.SMEM), _row(BLOCK, ATTN_W), _row(BLOCK, 2 * KV_W),
                  pl.BlockSpec((BLOCK, 2 * KV_W), lambda i: (jnp.maximum(i - 1, 0), 0)), _const((BLOCK, 2 * KV_W))] + [ANY] * n,
        out_specs=[_row(BLOCK, ATTN_W), _row(BLOCK, BLOCK)] + [ANY] * n,
        out_shape=[jax.ShapeDtypeStruct((r, ATTN_W), F32), jax.ShapeDtypeStruct((r, BLOCK), F32)]
        + [jax.ShapeDtypeStruct(g.shape, g.dtype) for g in gathered],
        input_output_aliases={5 + k: 2 + k for k in range(n)},
        scratch_shapes=[pltpu.VMEM((N_KV, GROUP * BLOCK, 2 * BLOCK), F32), pltpu.VMEM((GROUP * BLOCK, BLOCK), F32),
                        pltpu.SemaphoreType.DMA((3 * n,)), pltpu.SemaphoreType.DMA((3 * n,))],
        compiler_params=_cparams(),
    )(sinks, q, kv, kv, kv, *gathered)


def _shifted_copies(ub_ref, win):
    w = win.shape[0]
    ub_ref[0] = win
    for b in range(1, 8):
        ub_ref[b] = pltpu.roll(win, shift=w - b, axis=0)


def _conv_chunk(ub_ref, w_ref, r0, shifts):
    acc = jnp.zeros((CONV_CHUNK, CONV_W), F32)
    for j in range(CONV_K):
        a, b = divmod(shifts[j], 8)
        acc = acc + w_ref[j:j + 1, :] * ub_ref[b, pl.ds(r0 + 8 * a, CONV_CHUNK), :]
    return acc


FWD_SHIFTS = tuple(HALO - (CONV_K - 1) + j for j in range(CONV_K))
BWD_SHIFTS = tuple(CONV_K - 1 - j for j in range(CONV_K))


def _glu_window(cp_ref, c_ref, i):
    tile = c_ref[:, 0:CONV_W] * _sigmoid(c_ref[:, CONV_W:2 * CONV_W])
    halo = cp_ref[:, 0:CONV_W] * _sigmoid(cp_ref[:, CONV_W:2 * CONV_W])
    first = (jnp.zeros((HALO, CONV_W), jnp.int32) + i) == 0
    return jnp.concatenate([jnp.where(first, 0.0, halo), tile], axis=0)


def _halo_before(tm, n):
    return pl.BlockSpec((HALO, n), lambda i: (jnp.maximum(i * (tm // HALO) - 1, 0), 0))


def _conv_fwd(cacg, cw, cb, lg, lb, tm, gathered):
    r = cacg.shape[0]
    n = len(gathered)

    def body(c_ref, cp_ref, w_ref, cb_ref, lg_ref, lb_ref, *rest):
        o_ref, y_ref = rest[n:n + 2]
        dst = rest[n + 2:2 * n + 2]
        ub_ref, send_sems, recv_sems = rest[2 * n + 2:]
        i = pl.program_id(0)

        @pl.when(i == 0)
        def _():
            for cp in _gather_d2d(dst, send_sems, recv_sems)[0]:
                cp.start()

        _shifted_copies(ub_ref, _glu_window(cp_ref, c_ref, i))

        def chunk(ci, carry):
            r0 = pl.multiple_of(ci * CONV_CHUNK, CONV_CHUNK)
            y = _conv_chunk(ub_ref, w_ref, r0, FWD_SHIFTS) + cb_ref[...]
            yc = y - jnp.mean(y, axis=-1, keepdims=True)
            rs = lax.rsqrt(jnp.mean(yc * yc, axis=-1, keepdims=True) + NORM_EPS)
            yn = yc * rs * lg_ref[...] + lb_ref[...]
            o_ref[pl.ds(r0, CONV_CHUNK), :] = yn * _sigmoid(yn)
            y_ref[pl.ds(r0, CONV_CHUNK), :] = y
            return carry

        lax.fori_loop(0, tm // CONV_CHUNK, chunk, 0, unroll=4)

        @pl.when(i == r // tm - 1)
        def _():
            sends, arrivals = _gather_d2d(dst, send_sems, recv_sems)
            for cp in arrivals:
                cp.wait_recv()
            for cp in sends:
                cp.wait_send()

    return pl.pallas_call(
        body, name="conv_fwd", grid=(r // tm,),
        in_specs=[_row(tm, 2 * CONV_W), _halo_before(tm, 2 * CONV_W), _const((32, CONV_W)), _const((1, CONV_W)),
                  _const((1, CONV_W)), _const((1, CONV_W))] + [ANY] * n,
        out_specs=[_row(tm, CONV_W), _row(tm, CONV_W)] + [ANY] * n,
        out_shape=[jax.ShapeDtypeStruct((r, CONV_W), F32)] * 2 + [jax.ShapeDtypeStruct(g.shape, g.dtype) for g in gathered],
        input_output_aliases={6 + k: 2 + k for k in range(n)},
        scratch_shapes=[pltpu.VMEM((8, tm + HALO, CONV_W), F32), pltpu.SemaphoreType.DMA((3 * n,)), pltpu.SemaphoreType.DMA((3 * n,))],
        compiler_params=_cparams(),
    )(cacg, cacg, cw, cb, lg, lb, *gathered)


def _out_proj(oa, oc, h0, ga, gc, g2, w_out_b, tm):
    r = h0.shape[0]

    def body(oa_ref, oc_ref, h_ref, ga_ref, gc_ref, g2_ref, w_ref, h1_ref, hn2_ref):
        ma = _rms_fwd(oa_ref[...], ga_ref[...])[0].astype(BF16)
        mc = _rms_fwd(oc_ref[...], gc_ref[...])[0].astype(BF16)
        h1 = h_ref[...] + _dot(jnp.concatenate([ma, mc], axis=1), w_ref[...])
        h1_ref[...] = h1
        hn2_ref[...] = _rms_fwd(h1, g2_ref[...])[0].astype(BF16)

    return pl.pallas_call(
        body, name="out_proj", grid=(r // tm,),
        in_specs=[_row(tm, ATTN_W), _row(tm, CONV_W), _row(tm, D_MODEL), _const((1, ATTN_W)), _const((1, CONV_W)),
                  _const((1, D_MODEL)), _const((D_MODEL, D_MODEL))],
        out_specs=[_row(tm, D_MODEL), _row(tm, D_MODEL)],
        out_shape=[jax.ShapeDtypeStruct((r, D_MODEL), F32), jax.ShapeDtypeStruct((r, D_MODEL), BF16)],
        compiler_params=_cparams(),
    )(oa, oc, h0, ga, gc, g2, w_out_b)


def _ffn_fwd(hn2, h1, target, gf, wg_t, wu_t, wd_b, tm):
    r = h1.shape[0]
    seq = target.shape[0]
    n_sub = tm // BLOCK

    def body(hn_ref, h1_ref, *rest):
        t_refs = rest[:n_sub]
        gf_ref, wg_ref, wu_ref, wd_ref, gate_ref, up_ref, act_ref, dh2_ref, dh2b_ref, loss_ref, dgf_ref = rest[n_sub:]
        i = pl.program_id(0)

        @pl.when(i == 0)
        def _():
            loss_ref[...] = jnp.zeros_like(loss_ref)
            dgf_ref[...] = jnp.zeros_like(dgf_ref)

        hn = hn_ref[...]
        for cs in FF_CHUNKS:
            gate = _dot_nt(hn, wg_ref[cs, :])
            up = _dot_nt(hn, wu_ref[cs, :])
            gate_ref[:, cs] = gate.astype(BF16)
            up_ref[:, cs] = up.astype(BF16)
            act_ref[:, cs] = (gate * _sigmoid(gate) * up).astype(BF16)
        y, xhat, rstd = _rms_fwd(h1_ref[...] + _dot(act_ref[...], wd_ref[...]), gf_ref[...])
        rows = lax.broadcasted_iota(jnp.int32, (tm, D_MODEL), 0) + i * tm
        real = (rows >= BLOCK) & (rows < BLOCK + seq)
        err = jnp.where(real, y - jnp.concatenate([t[...] for t in t_refs], axis=0), 0.0)
        loss_ref[...] += jnp.sum(err * err) * (0.5 / D_MODEL)
        dy = err * (1.0 / D_MODEL)
        dh2, dg_rows = _rms_bwd(dy, xhat, rstd, gf_ref[...])
        dgf_ref[...] += jnp.sum(dg_rows, axis=0, keepdims=True)
        dh2_ref[...] = dh2
        dh2b_ref[...] = dh2.astype(BF16)

    def target_block(k):
        return pl.BlockSpec((BLOCK, D_MODEL), lambda i: (jnp.clip(n_sub * i - 1 + k, 0, seq // BLOCK - 1), 0))

    return pl.pallas_call(
        body, name="ffn_fwd", grid=(r // tm,),
        in_specs=[_row(tm, D_MODEL), _row(tm, D_MODEL)] + [target_block(k) for k in range(n_sub)]
        + [_const((1, D_MODEL))] + [_resident((D_FF, D_MODEL))] * 3,
        out_specs=[_row(tm, D_FF)] * 3 + [_row(tm, D_MODEL), _row(tm, D_MODEL), _const((1, BLOCK)), _const((1, D_MODEL))],
        out_shape=[jax.ShapeDtypeStruct((r, D_FF), BF16)] * 3
        + [jax.ShapeDtypeStruct((r, D_MODEL), F32), jax.ShapeDtypeStruct((r, D_MODEL), BF16),
           jax.ShapeDtypeStruct((1, BLOCK), F32), jax.ShapeDtypeStruct((1, D_MODEL), F32)],
        compiler_params=_cparams(),
    )(hn2, h1, *[target] * n_sub, gf, wg_t, wu_t, wd_b)


def _ffn_bwd(dh2, dh2b, gate, up, h1, g2, wg_t, wu_t, wd_b, tm):
    r = h1.shape[0]

    def body(dh2_ref, dh2b_ref, gate_ref, up_ref, h1_ref, g2_ref, wg_ref, wu_ref, wd_ref, dgate_ref, dup_ref, dh1_ref, dg2_ref):
        @pl.when(pl.program_id(0) == 0)
        def _():
            dg2_ref[...] = jnp.zeros_like(dg2_ref)

        dyb = dh2b_ref[...]
        for cs in FF_CHUNKS:
            dact = _dot_nt(dyb, wd_ref[cs, :])
            gate = gate_ref[:, cs].astype(F32)
            up = up_ref[:, cs].astype(F32)
            sg = _sigmoid(gate)
            dgate_ref[:, cs] = (dact * up * (sg * (1.0 + gate * (1.0 - sg)))).astype(BF16)
            dup_ref[:, cs] = (dact * (gate * sg)).astype(BF16)
        dhn = _dot(dgate_ref[...], wg_ref[...]) + _dot(dup_ref[...], wu_ref[...])
        _, xhat, rstd = _rms_fwd(h1_ref[...], g2_ref[...])
        dx, dg_rows = _rms_bwd(dhn, xhat, rstd, g2_ref[...])
        dg2_ref[...] += jnp.sum(dg_rows, axis=0, keepdims=True)
        dh1_ref[...] = dh2_ref[...] + dx

    return pl.pallas_call(
        body, name="ffn_bwd", grid=(r // tm,),
        in_specs=[_row(tm, D_MODEL), _row(tm, D_MODEL), _row(tm, D_FF), _row(tm, D_FF), _row(tm, D_MODEL), _const((1, D_MODEL))]
        + [_resident((D_FF, D_MODEL))] * 3,
        out_specs=[_row(tm, D_FF), _row(tm, D_FF), _row(tm, D_MODEL), _const((1, D_MODEL))],
        out_shape=[jax.ShapeDtypeStruct((r, D_FF), BF16), jax.ShapeDtypeStruct((r, D_FF), BF16),
                   jax.ShapeDtypeStruct((r, D_MODEL), F32), jax.ShapeDtypeStruct((1, D_MODEL), F32)],
        compiler_params=_cparams(),
    )(dh2, dh2b, gate, up, h1, g2, wg_t, wu_t, wd_b)


FF_HALF = D_FF // 2


def _ffn_wgrad_gu(hn2, dgate, dup, tk):
    r = hn2.shape[0]
    n_k = r // tk

    def body(hn_ref, dg_ref, du_ref, wg_ref, wu_ref, accg_ref, accu_ref):
        k = pl.program_id(1)

        @pl.when(k == 0)
        def _():
            accg_ref[...] = jnp.zeros_like(accg_ref)
            accu_ref[...] = jnp.zeros_like(accu_ref)

        hn = hn_ref[...]
        accg_ref[...] += _dot_tn(dg_ref[...], hn)
        accu_ref[...] += _dot_tn(du_ref[...], hn)

        @pl.when(k == n_k - 1)
        def _():
            wg_ref[...] = accg_ref[...].astype(BF16)
            wu_ref[...] = accu_ref[...].astype(BF16)

    col = pl.BlockSpec((tk, FF_HALF), lambda j, k: (k, j))
    out = pl.BlockSpec((FF_HALF, D_MODEL), lambda j, k: (j, 0))
    return pl.pallas_call(
        body, name="ffn_wgrad_gu", grid=(2, n_k),
        in_specs=[pl.BlockSpec((tk, D_MODEL), lambda j, k: (k, 0)), col, col],
        out_specs=[out, out],
        out_shape=[jax.ShapeDtypeStruct((D_FF, D_MODEL), BF16)] * 2,
        scratch_shapes=[pltpu.VMEM((FF_HALF, D_MODEL), F32)] * 2,
        compiler_params=_cparams(2),
    )(hn2, dgate, dup)


def _ffn_wgrad_d(act, dh2b, tk):
    r = act.shape[0]
    n_k = r // tk

    def body(a_ref, dy_ref, wd_ref, acc_ref):
        k = pl.program_id(1)

        @pl.when(k == 0)
        def _():
            acc_ref[...] = jnp.zeros_like(acc_ref)

        acc_ref[...] += _dot_tn(a_ref[...], dy_ref[...])

        @pl.when(k == n_k - 1)
        def _():
            wd_ref[...] = acc_ref[...].astype(BF16)

    return pl.pallas_call(
        body, name="ffn_wgrad_d", grid=(2, n_k),
        in_specs=[pl.BlockSpec((tk, FF_HALF), lambda j, k: (k, j)), pl.BlockSpec((tk, D_MODEL), lambda j, k: (k, 0))],
        out_specs=pl.BlockSpec((FF_HALF, D_MODEL), lambda j, k: (j, 0)),
        out_shape=jax.ShapeDtypeStruct((D_FF, D_MODEL), BF16),
        scratch_shapes=[pltpu.VMEM((FF_HALF, D_MODEL), F32)],
        compiler_params=_cparams(2),
    )(act, dh2b)


def _out_proj_bwd(dh1, oa, oc, ga, gc, w_out_b, tm):
    r = dh1.shape[0]

    def body(dh_ref, oa_ref, oc_ref, ga_ref, gc_ref, w_ref, doa_ref, doc_ref, dw_ref, dga_ref, dgc_ref, acc_ref):
        i = pl.program_id(0)

        @pl.when(i == 0)
        def _():
            acc_ref[...] = jnp.zeros_like(acc_ref)
            dga_ref[...] = jnp.zeros_like(dga_ref)
            dgc_ref[...] = jnp.zeros_like(dgc_ref)

        dhb = dh_ref[...].astype(BF16)
        dmix = _dot_nt(dhb, w_ref[...])
        ma, xa, ra = _rms_fwd(oa_ref[...], ga_ref[...])
        mc, xc, rc = _rms_fwd(oc_ref[...], gc_ref[...])
        acc_ref[...] += _dot_tn(jnp.concatenate([ma.astype(BF16), mc.astype(BF16)], axis=1), dhb)

        @pl.when(i == r // tm - 1)
        def _():
            dw_ref[...] = acc_ref[...].astype(BF16)

        doa, dga_rows = _rms_bwd(dmix[:, 0:ATTN_W], xa, ra, ga_ref[...])
        doc, dgc_rows = _rms_bwd(dmix[:, ATTN_W:ATTN_W + CONV_W], xc, rc, gc_ref[...])
        doa_ref[...] = doa
        doc_ref[...] = doc
        dga_ref[...] += jnp.sum(dga_rows, axis=0, keepdims=True)
        dgc_ref[...] += jnp.sum(dgc_rows, axis=0, keepdims=True)

    return pl.pallas_call(
        body, name="out_proj_bwd", grid=(r // tm,),
        in_specs=[_row(tm, D_MODEL), _row(tm, ATTN_W), _row(tm, CONV_W), _const((1, ATTN_W)), _const((1, CONV_W)),
                  _const((D_MODEL, D_MODEL))],
        out_specs=[_row(tm, ATTN_W), _row(tm, CONV_W), _const((D_MODEL, D_MODEL)), _const((1, ATTN_W)), _const((1, CONV_W))],
        out_shape=[jax.ShapeDtypeStruct((r, ATTN_W), F32), jax.ShapeDtypeStruct((r, CONV_W), F32),
                   jax.ShapeDtypeStruct((D_MODEL, D_MODEL), BF16), jax.ShapeDtypeStruct((1, ATTN_W), F32),
                   jax.ShapeDtypeStruct((1, CONV_W), F32)],
        scratch_shapes=[pltpu.VMEM((D_MODEL, D_MODEL), F32)],
        compiler_params=_cparams(),
    )(dh1, oa, oc, ga, gc, w_out_b)


def _conv_bwd_params(doc, y, cacg, lg, lb, tm, parts):
    r = cacg.shape[0]
    n_steps = r // tm
    n = len(parts)

    def body(do_ref, y_ref, c_ref, cp_ref, lg_ref, lb_ref, *rest):
        src = rest[:n]
        dy_ref, dcw_ref, dcb_ref, dlg_ref, dlb_ref = rest[n:n + 5]
        dst = rest[n + 5:2 * n + 5]
        ub_ref, accw_ref, send_sems, recv_sems = rest[2 * n + 5:]
        i = pl.program_id(0)

        @pl.when(i == 0)
        def _():
            for cp in _scatter(src, dst, send_sems, recv_sems):
                cp.start()
            accw_ref[...] = jnp.zeros_like(accw_ref)
            dcb_ref[...] = jnp.zeros_like(dcb_ref)
            dlg_ref[...] = jnp.zeros_like(dlg_ref)
            dlb_ref[...] = jnp.zeros_like(dlb_ref)

        _shifted_copies(ub_ref, _glu_window(cp_ref, c_ref, i))

        def chunk(ci, carry):
            r0 = pl.multiple_of(ci * CONV_CHUNK, CONV_CHUNK)
            y = y_ref[pl.ds(r0, CONV_CHUNK), :]
            yc = y - jnp.mean(y, axis=-1, keepdims=True)
            rs = lax.rsqrt(jnp.mean(yc * yc, axis=-1, keepdims=True) + NORM_EPS)
            xhat = yc * rs
            yn = xhat * lg_ref[...] + lb_ref[...]
            sg = _sigmoid(yn)
            dyn = do_ref[pl.ds(r0, CONV_CHUNK), :] * (sg * (1.0 + yn * (1.0 - sg)))
            dlg_ref[...] += jnp.sum(dyn * xhat, axis=0, keepdims=True)
            dlb_ref[...] += jnp.sum(dyn, axis=0, keepdims=True)
            dxh = dyn * lg_ref[...]
            dy = rs * (dxh - jnp.mean(dxh, axis=-1, keepdims=True) - xhat * jnp.mean(dxh * xhat, axis=-1, keepdims=True))
            dcb_ref[...] += jnp.sum(dy, axis=0, keepdims=True)
            dy_ref[pl.ds(r0, CONV_CHUNK), :] = dy
            for j in range(CONV_K):
                a, b = divmod(FWD_SHIFTS[j], 8)
                prod = dy * ub_ref[b, pl.ds(r0 + 8 * a, CONV_CHUNK), :]
                accw_ref[j] += jnp.sum(prod.reshape(CONV_CHUNK // 8, 8, CONV_W), axis=0)
            return carry

        lax.fori_loop(0, tm // CONV_CHUNK, chunk, 0, unroll=4)

        @pl.when(i == n_steps - 1)
        def _():
            for j in range(32):
                dcw_ref[j:j + 1, :] = jnp.sum(accw_ref[j], axis=0, keepdims=True)
            for cp in _scatter(src, dst, send_sems, recv_sems):
                cp.wait()

    vec = _const((1, CONV_W))
    return pl.pallas_call(
        body, name="conv_bwd_params", grid=(n_steps,),
        in_specs=[_row(tm, CONV_W), _row(tm, CONV_W), _row(tm, 2 * CONV_W), _halo_before(tm, 2 * CONV_W), vec, vec] + [ANY] * n,
        out_specs=[_row(tm, CONV_W), _const((32, CONV_W)), vec, vec, vec] + [ANY] * n,
        out_shape=[jax.ShapeDtypeStruct((r, CONV_W), F32), jax.ShapeDtypeStruct((32, CONV_W), F32)]
        + [jax.ShapeDtypeStruct((1, CONV_W), F32)] * 3 + _scatter_landing(parts),
        scratch_shapes=[pltpu.VMEM((8, tm + HALO, CONV_W), F32), pltpu.VMEM((32, 8, CONV_W), F32),
                        pltpu.SemaphoreType.DMA((7 * n,)), pltpu.SemaphoreType.DMA((7 * n,))],
        compiler_params=_cparams(),
    )(doc, y, cacg, cacg, lg, lb, *parts)


def _conv_bwd_data(dy, cacg, cw, tm):
    r = cacg.shape[0]
    n_steps = r // tm

    def body(dy_ref, dyn_ref, c_ref, w_ref, dc_ref, ub_ref):
        last = (jnp.zeros((HALO, CONV_W), jnp.int32) + pl.program_id(0)) == n_steps - 1
        win = jnp.concatenate([dy_ref[...], jnp.where(last, 0.0, dyn_ref[...])], axis=0)
        _shifted_copies(ub_ref, win)

        def chunk(ci, carry):
            r0 = pl.multiple_of(ci * CONV_CHUNK, CONV_CHUNK)
            du = _conv_chunk(ub_ref, w_ref, r0, BWD_SHIFTS)
            ca = c_ref[pl.ds(r0, CONV_CHUNK), 0:CONV_W]
            sg = _sigmoid(c_ref[pl.ds(r0, CONV_CHUNK), CONV_W:2 * CONV_W])
            dc_ref[pl.ds(r0, CONV_CHUNK), 0:CONV_W] = (du * sg).astype(BF16)
            dc_ref[pl.ds(r0, CONV_CHUNK), CONV_W:2 * CONV_W] = (du * ca * sg * (1.0 - sg)).astype(BF16)
            return carry

        lax.fori_loop(0, tm // CONV_CHUNK, chunk, 0, unroll=4)

    halo_after = pl.BlockSpec((HALO, CONV_W), lambda i: (jnp.minimum((i + 1) * (tm // HALO), r // HALO - 1), 0))
    return pl.pallas_call(
        body, name="conv_bwd_data", grid=(n_steps,),
        in_specs=[_row(tm, CONV_W), halo_after, _row(tm, 2 * CONV_W), _const((32, CONV_W))],
        out_specs=_row(tm, 2 * CONV_W),
        out_shape=jax.ShapeDtypeStruct((r, 2 * CONV_W), BF16),
        scratch_shapes=[pltpu.VMEM((8, tm + HALO, CONV_W), F32)],
        compiler_params=_cparams(),
    )(dy, dy, cacg, cw)


def _attn_bwd(q, kv, o, do, lse, sinks, parts):
    r = q.shape[0]
    nb = r // BLOCK
    n = len(parts)

    def body(sink_ref, q_ref, kvc_ref, kvp_ref, kvm_ref, o_ref, do_ref, lse_ref, *rest):
        src = rest[:n]
        dq_ref, dkv_ref, dmeta_ref, dsink_ref = rest[n:n + 4]
        dst = rest[n + 4:2 * n + 4]
        hold_ref, bias_ref, late_ref, send_sems, recv_sems = rest[2 * n + 4:]
        i = pl.program_id(0)

        @pl.when(i == 0)
        def _():
            for cp in _scatter(src, dst, send_sems, recv_sems):
                cp.start()
            _attn_bias_init(bias_ref, late_ref)
            dmeta_ref[...] = jnp.zeros_like(dmeta_ref)
            dsink_ref[...] = jnp.zeros_like(dsink_ref)
            hold_ref[...] = jnp.zeros_like(hold_ref)

        @pl.when(i < nb)
        def _():
            lane = lax.broadcasted_iota(jnp.int32, (BLOCK, BLOCK), 1)
            lse_tile = lse_ref[...]
            zero = jnp.zeros((BLOCK, BLOCK), F32)
            block_bias = _attn_block_bias(late_ref, i)
            prev_part = late_ref[...] > 0.5
            for g in range(N_KV):
                kcat, vcat = _kv_cat(kvm_ref, kvp_ref, kvc_ref, g)
                heads = range(g * GROUP, (g + 1) * GROUP)
                qs = _stack_heads(q_ref, g)
                dos = _stack_heads(do_ref, g)
                dosb = dos.astype(BF16)
                lse = jnp.concatenate(
                    [jnp.sum(jnp.where(lane == h, lse_tile, 0.0), axis=-1, keepdims=True) + zero for h in heads], axis=0)
                delta = jnp.sum(dos * _stack_heads(o_ref, g), axis=-1, keepdims=True) + jnp.zeros((GROUP * BLOCK, BLOCK), F32)
                band_bias = bias_ref[g, :, BLOCK:2 * BLOCK] + block_bias[1]
                bias = [bias_ref[g, :, 0:BLOCK] + block_bias[0], jnp.where(prev_part, band_bias, NEG), jnp.where(prev_part, NEG, band_bias)]
                s = _dot_nt(qs, kcat)
                dp = _dot_nt(dosb, vcat)
                ps = [jnp.exp(s[:, k * BLOCK:(k + 1) * BLOCK] * SCALE + bias[k] - lse) for k in range(3)]
                p = jnp.concatenate(ps, axis=1)
                ds = jnp.concatenate(
                    [(ps[k] * (dp[:, k * BLOCK:(k + 1) * BLOCK] - delta)) * SCALE for k in range(3)], axis=1).astype(BF16)
                sink_term = jnp.exp(_head_rows([sink_ref[h] for h in heads]) - lse)[:, 0:1] * delta[:, 0:1]
                dq = _dot(ds, kcat).astype(BF16)
                for j, h in enumerate(heads):
                    dsink_ref[h:h + 1, :] += -jnp.sum(sink_term[j * BLOCK:(j + 1) * BLOCK])
                    dq_ref[:, h * HEAD_DIM:(h + 1) * HEAD_DIM] = dq[j * BLOCK:(j + 1) * BLOCK]
                dk_t = _dot_tn(qs, ds)
                dv_t = _dot_tn(dosb, p.astype(BF16))
                ks = slice(g * HEAD_DIM, (g + 1) * HEAD_DIM)
                vs = slice(KV_W + g * HEAD_DIM, KV_W + (g + 1) * HEAD_DIM)
                for sl, grad_t in ((ks, dk_t), (vs, dv_t)):
                    dmeta_ref[:, sl] += grad_t[:, 0:BLOCK].T
                    dkv_ref[:, sl] = hold_ref[:, sl] + grad_t[:, BLOCK:2 * BLOCK].T
                    hold_ref[:, sl] = grad_t[:, 2 * BLOCK:3 * BLOCK].T

        @pl.when(i == nb)
        def _():
            dkv_ref[...] = hold_ref[...]
            for cp in _scatter(src, dst, send_sems, recv_sems):
                cp.wait()

    def cur(i):
        return jnp.minimum(i, nb - 1)

    return pl.pallas_call(
        body, name="attn_bwd", grid=(nb + 1,),
        in_specs=[pl.BlockSpec(memory_space=pltpu.SMEM),
                  pl.BlockSpec((BLOCK, ATTN_W), lambda i: (cur(i), 0)),
                  pl.BlockSpec((BLOCK, 2 * KV_W), lambda i: (cur(i), 0)),
                  pl.BlockSpec((BLOCK, 2 * KV_W), lambda i: (jnp.maximum(cur(i) - 1, 0), 0)),
                  _const((BLOCK, 2 * KV_W)),
                  pl.BlockSpec((BLOCK, ATTN_W), lambda i: (cur(i), 0)),
                  pl.BlockSpec((BLOCK, ATTN_W), lambda i: (cur(i), 0)),
                  pl.BlockSpec((BLOCK, BLOCK), lambda i: (cur(i), 0))] + [ANY] * n,
        out_specs=[pl.BlockSpec((BLOCK, ATTN_W), lambda i: (cur(i), 0)),
                   pl.BlockSpec((BLOCK, 2 * KV_W), lambda i: (jnp.maximum(i - 1, 0), 0)),
                   _const((BLOCK, 2 * KV_W)), _const((N_HEADS, BLOCK))] + [ANY] * n,
        out_shape=[jax.ShapeDtypeStruct((r, ATTN_W), BF16), jax.ShapeDtypeStruct((r, 2 * KV_W), F32),
                   jax.ShapeDtypeStruct((BLOCK, 2 * KV_W), F32), jax.ShapeDtypeStruct((N_HEADS, BLOCK), F32)] + _scatter_landing(parts),
        scratch_shapes=[pltpu.VMEM((BLOCK, 2 * KV_W), F32), pltpu.VMEM((N_KV, GROUP * BLOCK, 2 * BLOCK), F32),
                        pltpu.VMEM((GROUP * BLOCK, BLOCK), F32), pltpu.SemaphoreType.DMA((7 * n,)), pltpu.SemaphoreType.DMA((7 * n,))],
        compiler_params=_cparams(),
    )(sinks, q, kv, kv, kv, o, do, lse, *parts)


def _in_proj_bwd(dq, dkv, dkv_meta, dc, dh1, h0, g1, w_in_t, seq, tm):
    r = h0.shape[0]
    n_tiles = r // tm
    n_out = -(-seq // tm)

    def body(dq_ref, dkv_ref, dm_ref, dc_ref, dh1_ref, h_ref, g_ref, w_ref, gx_ref, lead_ref, dwt_ref, dg_ref, dw_ref, hold_ref):
        i = pl.program_id(0)

        @pl.when(i == 0)
        def _():
            dw_ref[...] = jnp.zeros_like(dw_ref)
            dg_ref[...] = jnp.zeros_like(dg_ref)

        @pl.when(i < n_tiles)
        def _():
            meta = jnp.concatenate([dm_ref[...], jnp.zeros((tm - BLOCK, 2 * KV_W), F32)], axis=0) if tm > BLOCK else dm_ref[...]
            first = (jnp.zeros((tm, 2 * KV_W), jnp.int32) + i) == 0
            dkvb = (dkv_ref[...] + jnp.where(first, meta, 0.0)).astype(BF16)
            hn, xhat, rstd = _rms_fwd(h_ref[...], g_ref[...])
            dproj = jnp.concatenate([dq_ref[...], dkvb, dc_ref[...]], axis=1)
            dhn = _dot(dproj, w_ref[...])
            dw_ref[...] += _dot_tn(dproj, hn.astype(BF16))
            dx, dg_rows = _rms_bwd(dhn, xhat, rstd, g_ref[...])
            dg_ref[...] += jnp.sum(dg_rows, axis=0, keepdims=True)
            dh0 = dh1_ref[...] + dx

            @pl.when(i == 0)
            def _():
                lead_ref[...] = dh0[0:BLOCK]

            @pl.when((i >= 1) & (i <= n_out))
            def _():
                gx_ref[0:tm - BLOCK, :] = hold_ref[...]
                gx_ref[tm - BLOCK:tm, :] = dh0[0:BLOCK]

            hold_ref[...] = dh0[BLOCK:tm]

        @pl.when((i == n_tiles) & (n_tiles <= n_out))
        def _():
            gx_ref[0:tm - BLOCK, :] = hold_ref[...]

        @pl.when(i == n_tiles - 1)
        def _():
            dwt_ref[...] = dw_ref[...].astype(BF16)

    def tile(n):
        return pl.BlockSpec((tm, n), lambda i: (jnp.minimum(i, n_tiles - 1), 0))

    return pl.pallas_call(
        body, name="in_proj_bwd", grid=(n_tiles + 1,),
        in_specs=[tile(ATTN_W), tile(2 * KV_W), _const((BLOCK, 2 * KV_W)), tile(2 * CONV_W), tile(D_MODEL), tile(D_MODEL),
                  _const((1, D_MODEL)), _const((IN_COLS, D_MODEL))],
        out_specs=[pl.BlockSpec((tm, D_MODEL), lambda i: (jnp.clip(i - 1, 0, n_out - 1), 0)), _const((BLOCK, D_MODEL)),
                   _const((IN_COLS, D_MODEL)), _const((1, D_MODEL))],
        out_shape=[jax.ShapeDtypeStruct((seq, D_MODEL), F32), jax.ShapeDtypeStruct((BLOCK, D_MODEL), F32),
                   jax.ShapeDtypeStruct((IN_COLS, D_MODEL), BF16), jax.ShapeDtypeStruct((1, D_MODEL), F32)],
        scratch_shapes=[pltpu.VMEM((IN_COLS, D_MODEL), F32), pltpu.VMEM((tm - BLOCK, D_MODEL), F32)],
        compiler_params=_cparams(),
    )(dq, dkv, dkv_meta, dc, dh1, h0, g1, w_in_t)


def _adamw_update(w_ref, g_ref, m_ref, v_ref, d_ref, nm_ref, nv_ref):
    g = g_ref[...]
    m = ADAM_B1 * m_ref[...] + (1.0 - ADAM_B1) * g
    v = ADAM_B2 * v_ref[...] + (1.0 - ADAM_B2) * (g * g)
    m_hat = m / (1.0 - ADAM_B1 ** ADAM_STEP)
    v_hat = v / (1.0 - ADAM_B2 ** ADAM_STEP)
    d_ref[...] = -ADAM_LR * (m_hat / (jnp.sqrt(v_hat) + ADAM_EPS) + ADAM_WD * w_ref[...])
    nm_ref[...] = m
    nv_ref[...] = v


def _adamw(w, g, m, v, name):
    rows, cols = w.shape
    tr = rows
    for cand in (256, 176, 128, 64, 32, 16, 8):
        if rows % cand == 0:
            tr = cand
            break

    def body(*refs):
        _adamw_update(*refs)

    spec = _row(tr, cols)
    return pl.pallas_call(
        body, name=name, grid=(rows // tr,), in_specs=[spec] * 4, out_specs=[spec] * 3,
        out_shape=[jax.ShapeDtypeStruct((rows, cols), F32)] * 3, compiler_params=_cparams(),
    )(w, g, m, v)


MESH = pl.DeviceIdType.MESH
ANY = pl.BlockSpec(memory_space=pl.ANY)


def _place():
    x, y, c = lax.axis_index("x"), lax.axis_index("y"), lax.axis_index("c")
    chips = [(1 - x, y), (x, 1 - y), (1 - x, 1 - y)]
    return x, y, c, chips


def _gather_ici(dst, send_sems, recv_sems):
    x, y, c, chips = _place()
    sends, arrivals = [], []
    for k in range(len(dst)):
        rows = dst[k].shape[1] // 2
        half = pl.ds(c * rows, rows)
        mine = dst[k].at[2 * x + y, half]
        for p, chip in enumerate(chips):
            sems = dict(send_sem=send_sems.at[3 * k + p], recv_sem=recv_sems.at[3 * k + p], device_id=(chip[0], chip[1], c),
                        device_id_type=MESH)
            sends.append(pltpu.make_async_remote_copy(src_ref=mine, dst_ref=mine, **sems))
            theirs = dst[k].at[2 * chip[0] + chip[1], half]
            arrivals.append(pltpu.make_async_remote_copy(src_ref=theirs, dst_ref=theirs, **sems))
    return sends, arrivals


def _gather_d2d(dst, send_sems, recv_sems):
    x, y, c, chips = _place()
    sends, arrivals = [], []
    for k in range(len(dst)):
        rows = dst[k].shape[1] // 2
        for p, chip in enumerate(chips):
            sems = dict(send_sem=send_sems.at[3 * k + p], recv_sem=recv_sems.at[3 * k + p], device_id=(x, y, 1 - c),
                        device_id_type=MESH)
            mine = dst[k].at[2 * chip[0] + chip[1], pl.ds(c * rows, rows)]
            sends.append(pltpu.make_async_remote_copy(src_ref=mine, dst_ref=mine, **sems))
            theirs = dst[k].at[2 * chip[0] + chip[1], pl.ds((1 - c) * rows, rows)]
            arrivals.append(pltpu.make_async_remote_copy(src_ref=theirs, dst_ref=theirs, **sems))
    return sends, arrivals


def _own_slots(shard, dtype):
    return jnp.broadcast_to(shard[None], (N_SHARD,) + shard.shape).astype(dtype)


def _gather_weights(slots):
    n = len(slots)

    def body(*refs):
        dst = refs[n:2 * n]
        ici_send, ici_recv, d2d_send, d2d_recv = refs[2 * n:]
        sends, arrivals = _gather_ici(dst, ici_send, ici_recv)
        for cp in sends:
            cp.start()
        for cp in arrivals:
            cp.wait_recv()
        forwards, from_sibling = _gather_d2d(dst, d2d_send, d2d_recv)
        for cp in forwards:
            cp.start()
        for cp in from_sibling:
            cp.wait_recv()
        for cp in sends + forwards:
            cp.wait_send()

    return pl.pallas_call(
        body, name="gather_weights",
        in_specs=[ANY] * n, out_specs=[ANY] * n,
        out_shape=[jax.ShapeDtypeStruct(s.shape, s.dtype) for s in slots],
        input_output_aliases={k: k for k in range(n)},
        scratch_shapes=[pltpu.SemaphoreType.DMA((3 * n,))] * 4,
    )(*slots)


VMEM_WHOLE = pl.BlockSpec(memory_space=pltpu.VMEM)


def _allreduce_small(parts, grads):
    widths = sorted({p.shape[1] for p in parts})
    place, heights = [], [0] * len(widths)
    for p in parts:
        gi = widths.index(p.shape[1])
        place.append((gi, heights[gi]))
        heights[gi] += -(-p.shape[0] // 8) * 8
    n, ng, nb = len(parts), len(widths), len(grads)

    def body(*refs):
        ins, big_src = refs[:n], refs[n:n + nb]
        outs, big_dst = refs[n + nb:2 * n + nb], refs[2 * n + nb:2 * (n + nb)]
        slots = refs[2 * (n + nb):2 * (n + nb) + ng]
        send_sems, recv_sems = refs[2 * (n + nb) + ng:2 * (n + nb) + ng + 2]
        scattered = _scatter(big_src, big_dst, *refs[2 * (n + nb) + ng + 2:]) if nb else []
        for cp in scattered:
            cp.start()
        x, y, c = lax.axis_index("x"), lax.axis_index("y"), lax.axis_index("c")
        me = 4 * x + 2 * y + c
        for gi in range(ng):
            slots[gi][me] = jnp.zeros((heights[gi], widths[gi]), F32)
        for k, (gi, r0) in enumerate(place):
            slots[gi][me, r0:r0 + parts[k].shape[0], :] = ins[k][...]

        def copy(gi, j, arriving):
            peer = ((x + (j >> 2)) % 2, (y + ((j >> 1) & 1)) % 2, (c + (j & 1)) % 2)
            slot = 4 * peer[0] + 2 * peer[1] + peer[2] if arriving else me
            return pltpu.make_async_remote_copy(
                src_ref=slots[gi].at[me], dst_ref=slots[gi].at[slot], send_sem=send_sems.at[7 * gi + j - 1],
                recv_sem=recv_sems.at[7 * gi + j - 1], device_id=peer, device_id_type=MESH)

        pairs = [(gi, j) for gi in range(ng) for j in range(1, N_DEV)]
        for gi, j in pairs:
            copy(gi, j, False).start()
        for gi, j in pairs:
            copy(gi, j, True).wait_recv()
        totals = []
        for gi in range(ng):
            total = slots[gi][0]
            for d in range(1, N_DEV):
                total = total + slots[gi][d]
            totals.append(total)
        for k, (gi, r0) in enumerate(place):
            outs[k][...] = totals[gi][r0:r0 + parts[k].shape[0], :]
        for gi, j in pairs:
            copy(gi, j, False).wait_send()
        for cp in scattered:
            cp.wait()

    out = pl.pallas_call(
        body, name="allreduce_small", in_specs=[VMEM_WHOLE] * n + [ANY] * nb, out_specs=[VMEM_WHOLE] * n + [ANY] * nb,
        out_shape=[jax.ShapeDtypeStruct(p.shape, F32) for p in parts] + _scatter_landing(grads),
        scratch_shapes=[pltpu.VMEM((N_DEV, heights[gi], widths[gi]), F32) for gi in range(ng)]
        + [pltpu.SemaphoreType.DMA((7 * ng,))] * 2 + [pltpu.SemaphoreType.DMA((7 * nb,))] * (2 if nb else 0),
    )(*parts, *grads)
    return out[:n], out[n:]


def _adamw_small(ws, gs, ms, vs):
    n = len(ws)

    def body(*refs):
        for k in range(n):
            w_ref, g_ref, m_ref, v_ref = (refs[j * n + k] for j in range(4))
            _adamw_update(w_ref, g_ref, m_ref, v_ref, *(refs[(4 + j) * n + k] for j in range(3)))

    shapes = [jax.ShapeDtypeStruct(w.shape, F32) for w in ws]
    out = pl.pallas_call(
        body, name="adamw_small", in_specs=[VMEM_WHOLE] * (4 * n), out_specs=[VMEM_WHOLE] * (3 * n), out_shape=shapes * 3,
    )(*ws, *gs, *ms, *vs)
    return out[:n], out[n:2 * n], out[2 * n:]


def _scatter(src, dst, send_sems, recv_sems):
    x, y, c = lax.axis_index("x"), lax.axis_index("y"), lax.axis_index("c")
    copies = []
    for k in range(len(src)):
        rows = src[k].shape[1] // 2
        for j in range(1, N_DEV):
            px, py, pc = (x + (j >> 2)) % 2, (y + ((j >> 1) & 1)) % 2, (c + (j & 1)) % 2
            copies.append(pltpu.make_async_remote_copy(
                src_ref=src[k].at[2 * px + py, pl.ds(pc * rows, rows)], dst_ref=dst[k].at[j - 1],
                send_sem=send_sems.at[7 * k + j - 1], recv_sem=recv_sems.at[7 * k + j - 1], device_id=(px, py, pc),
                device_id_type=MESH))
    return copies


def _scatter_landing(parts):
    return [jax.ShapeDtypeStruct((N_DEV - 1, p.shape[1] // 2, p.shape[2]), p.dtype) for p in parts]


HBM = pl.BlockSpec(memory_space=pltpu.HBM)
SEMAPHORES = pl.BlockSpec(memory_space=pltpu.SEMAPHORE)
SPLIT_COPY = pltpu.CompilerParams(has_side_effects=pltpu.SideEffectType.DATAFLOW_SIDE_EFFECTING)


def _scatter_start(part):
    landing, = _scatter_landing([part])

    def body(src_ref, land_ref, send_sems, recv_sems, src_thru, land_thru, token_ref):
        for cp in _scatter([src_ref], [land_ref], send_sems, recv_sems):
            cp.start()
        token_ref[...] = jnp.zeros_like(token_ref)

    return pl.pallas_call(
        body, name="scatter_start",
        out_shape=(pltpu.SemaphoreType.DMA((N_DEV - 1,)), pltpu.SemaphoreType.DMA((N_DEV - 1,)), pltpu.HBM(part.shape, part.dtype),
                   pltpu.HBM(landing.shape, landing.dtype), jax.ShapeDtypeStruct((8, 128), F32)),
        in_specs=(HBM, HBM), out_specs=(SEMAPHORES, SEMAPHORES, HBM, HBM, VMEM_WHOLE), input_output_aliases={0: 2, 1: 3},
        compiler_params=SPLIT_COPY,
    )(pltpu.with_memory_space_constraint(part, pltpu.HBM),
      pltpu.with_memory_space_constraint(lax.empty(landing.shape, landing.dtype), pltpu.HBM))


def _scatter_wait(send_sems, recv_sems, part_thru, land_thru, after):
    def body(src_ref, land_ref, send_ref, recv_ref, after_ref, src_dead, got_ref):
        for cp in _scatter([src_ref], [land_ref], send_ref, recv_ref):
            cp.wait_send()
            cp.wait_recv()

    return pl.pallas_call(
        body, name="scatter_wait",
        out_shape=(pltpu.HBM(part_thru.shape, part_thru.dtype), pltpu.HBM(land_thru.shape, land_thru.dtype)),
        in_specs=(HBM, HBM, SEMAPHORES, SEMAPHORES, ANY), out_specs=(HBM, HBM), input_output_aliases={0: 0, 1: 1},
        compiler_params=SPLIT_COPY,
    )(part_thru, land_thru, send_sems, recv_sems, after)


def _sum_pieces(own, landed, behind, name):
    n = len(own)

    def body(*refs):
        for k in range(n):
            got = refs[n + k]
            total = refs[k][...].astype(F32)
            for j in range(N_DEV - 1):
                total = total + got[j].astype(F32)
            refs[2 * n + 1 + k][...] = total

    in_specs, out_specs = [], []
    for o in own:
        in_specs.append(_row(o.shape[0] // 2, o.shape[1]))
    for o in own:
        in_specs.append(pl.BlockSpec((N_DEV - 1, o.shape[0] // 2, o.shape[1]), lambda i: (0, i, 0)))
        out_specs.append(_row(o.shape[0] // 2, o.shape[1]))
    return pl.pallas_call(
        body, name=name, grid=(2,), in_specs=in_specs + [ANY], out_specs=out_specs,
        out_shape=[jax.ShapeDtypeStruct(o.shape, F32) for o in own], compiler_params=_cparams(),
    )(*own, *landed, behind)


def _swap_with_sibling(halves, name):
    n = len(halves)

    def body(*refs):
        x, y, c = lax.axis_index("x"), lax.axis_index("y"), lax.axis_index("c")
        copies = [pltpu.make_async_remote_copy(
            src_ref=refs[k], dst_ref=refs[n + k], send_sem=refs[2 * n].at[k], recv_sem=refs[2 * n + 1].at[k],
            device_id=(x, y, 1 - c), device_id_type=MESH) for k in range(n)]
        for cp in copies:
            cp.start()
        for cp in copies:
            cp.wait()

    return pl.pallas_call(
        body, name=name, in_specs=[ANY] * n, out_specs=[ANY] * n,
        out_shape=[jax.ShapeDtypeStruct(h.shape, h.dtype) for h in halves],
        scratch_shapes=[pltpu.SemaphoreType.DMA((n,)), pltpu.SemaphoreType.DMA((n,))],
    )(*halves)


def _own_piece(part):
    rows = part.shape[1] // 2
    s = 2 * lax.axis_index("x") + lax.axis_index("y")
    return lax.dynamic_slice(part, (s, lax.axis_index("c") * rows, 0), (1, rows, part.shape[2]))[0]


def _both_halves(mine, theirs):
    south = lax.axis_index("c") == 0
    return jnp.concatenate([jnp.where(south, mine, theirs), jnp.where(south, theirs, mine)], axis=0)


def _from_col_shards(g):
    return g.transpose(1, 0, 2).reshape(g.shape[1], -1)


def kernel(x, meta_tokens, attn_norm_g, w_in, attn_sinks, conv_w, conv_b, conv_ln_g, conv_ln_b, attn_out_g, conv_out_g, w_out, ffn_norm_g, w_gate, w_up, w_down, final_norm_g, loss_target, m_meta_tokens, m_attn_norm_g, m_w_in, m_attn_sinks, m_conv_w, m_conv_b, m_conv_ln_g, m_conv_ln_b, m_attn_out_g, m_conv_out_g, m_w_out, m_ffn_norm_g, m_w_gate, m_w_up, m_w_down, m_final_norm_g, v_meta_tokens, v_attn_norm_g, v_w_in, v_attn_sinks, v_conv_w, v_conv_b, v_conv_ln_g, v_conv_ln_b, v_attn_out_g, v_conv_out_g, v_w_out, v_ffn_norm_g, v_w_gate, v_w_up, v_w_down, v_final_norm_g):
    seq = x.shape[1]
    r = -(-(seq + BLOCK) // ROW_QUANTUM) * ROW_QUANTUM
    tm_wide = 768 if seq >= 768 else 256
    shard = 2 * lax.axis_index("x") + lax.axis_index("y")

    conv_w32 = jnp.pad(conv_w[0], ((0, 1), (0, 0)))
    small_shard = jnp.concatenate([meta_tokens, conv_w32.reshape(16, 256)], axis=0)
    g_in, g_small = _gather_weights([_own_slots(w_in[0].T, BF16), _own_slots(small_shard, F32)])
    later = [_own_slots(w, BF16) for w in (w_gate[0].T, w_up[0].T, w_out[0], w_down[0])]
    w_in_t = g_in.reshape(IN_COLS, D_MODEL)
    meta_full = _from_col_shards(g_small[:, 0:N_META])
    cw_full = _from_col_shards(g_small[:, N_META:].reshape(N_SHARD, 32, 128))

    g1, ga, gc, g2 = attn_norm_g, attn_out_g, conv_out_g, ffn_norm_g
    gf = final_norm_g.reshape(1, D_MODEL)
    sinks = attn_sinks[0]

    lead = jnp.concatenate([jnp.zeros((LEAD, D_MODEL), F32), meta_full], axis=0)
    h0, q, kv, cacg = _in_proj(x[0], lead, g1, w_in_t, r, 768)
    oa, lse, *gathered = _attn_fwd(q, kv, sinks, later)
    oc, yc, g_gate, g_up, g_out, g_down = _conv_fwd(cacg, cw_full, conv_b, conv_ln_g, conv_ln_b, 384, gathered)
    wg_t, wu_t, wd_b = g_gate.reshape(D_FF, D_MODEL), g_up.reshape(D_FF, D_MODEL), g_down.reshape(D_FF, D_MODEL)
    w_out_b = g_out.reshape(D_MODEL, D_MODEL)
    h1, hn2 = _out_proj(oa, oc, h0, ga, gc, g2, w_out_b, 768)
    gate, up, act, dh2, dh2b, loss_p, dgf = _ffn_fwd(hn2, h1, loss_target[0], gf, wg_t, wu_t, wd_b, 384)

    def by_shard(dw):
        return dw.reshape(N_SHARD, dw.shape[0] // N_SHARD, D_MODEL)

    dgate, dup, dh1, dg2 = _ffn_bwd(dh2, dh2b, gate, up, h1, g2, wg_t, wu_t, wd_b, 384)
    p_gate, p_up = [by_shard(dw) for dw in _ffn_wgrad_gu(hn2, dgate, dup, 768)]
    p_down = by_shard(_ffn_wgrad_d(act, dh2b, 768))
    doa, doc, dwo, dga, dgc = _out_proj_bwd(dh1, oa, oc, ga, gc, w_out_b, 768)
    p_out = by_shard(dwo)
    dy, dcw, dcb, dlg, dlb, l_gate, l_up = _conv_bwd_params(doc, yc, cacg, conv_ln_g, conv_ln_b, 384, [p_gate, p_up])
    dc = _conv_bwd_data(dy, cacg, cw_full, 384)
    dq, dkv, dkv_meta, dsink, l_out, l_down = _attn_bwd(q, kv, oa, doa, lse, sinks, [p_out, p_down])
    grad_x, dlead, dwi_t, dg1 = _in_proj_bwd(dq, dkv, dkv_meta, dc, dh1, h0, g1, w_in_t, seq, tm_wide)
    in_send, in_recv, p_in_thru, l_in_thru, started = _scatter_start(by_shard(dwi_t))

    red_names = ("final_norm_g", "attn_norm_g", "ffn_norm_g", "meta_tokens", "attn_out_g", "conv_out_g", "conv_b", "conv_ln_g",
                 "conv_ln_b", "conv_w", "loss", "attn_sinks")
    sums, _ = _allreduce_small(
        [dgf, dg1, dg2, dlead[LEAD:BLOCK], dga, dgc, dcb, dlg, dlb, dcw, loss_p + started[0:1], dsink], [])
    red = dict(zip(red_names, sums))
    loss = red["loss"][0, 0]
    red["final_norm_g"] = red["final_norm_g"].reshape(D_MODEL)
    red["attn_sinks"] = red["attn_sinks"][:, 0].reshape(1, N_HEADS)
    g_meta = lax.dynamic_slice_in_dim(red["meta_tokens"], shard * (D_MODEL // N_SHARD), D_MODEL // N_SHARD, axis=1)
    g_convw = lax.dynamic_slice_in_dim(red["conv_w"][0:CONV_K], shard * (CONV_W // N_SHARD), CONV_W // N_SHARD, axis=1)[None]

    big = ("w_gate", "w_up", "w_out", "w_down", "w_in")
    transposed = ("w_in", "w_gate", "w_up")

    grads = {
        "meta_tokens": g_meta, "attn_norm_g": red["attn_norm_g"], "attn_sinks": red["attn_sinks"],
        "conv_w": g_convw, "conv_b": red["conv_b"], "conv_ln_g": red["conv_ln_g"], "conv_ln_b": red["conv_ln_b"],
        "attn_out_g": red["attn_out_g"], "conv_out_g": red["conv_out_g"], "ffn_norm_g": red["ffn_norm_g"],
        "final_norm_g": red["final_norm_g"]}
    params = {
        "meta_tokens": (meta_tokens, m_meta_tokens, v_meta_tokens), "attn_norm_g": (attn_norm_g, m_attn_norm_g, v_attn_norm_g),
        "w_in": (w_in, m_w_in, v_w_in), "attn_sinks": (attn_sinks, m_attn_sinks, v_attn_sinks), "conv_w": (conv_w, m_conv_w, v_conv_w),
        "conv_b": (conv_b, m_conv_b, v_conv_b), "conv_ln_g": (conv_ln_g, m_conv_ln_g, v_conv_ln_g),
        "conv_ln_b": (conv_ln_b, m_conv_ln_b, v_conv_ln_b), "attn_out_g": (attn_out_g, m_attn_out_g, v_attn_out_g),
        "conv_out_g": (conv_out_g, m_conv_out_g, v_conv_out_g), "w_out": (w_out, m_w_out, v_w_out),
        "ffn_norm_g": (ffn_norm_g, m_ffn_norm_g, v_ffn_norm_g), "w_gate": (w_gate, m_w_gate, v_w_gate), "w_up": (w_up, m_w_up, v_w_up),
        "w_down": (w_down, m_w_down, v_w_down), "final_norm_g": (final_norm_g, m_final_norm_g, v_final_norm_g)}
    names = list(params)
    delta, new_m, new_v = {}, {}, {}

    def finish(group, parts, landed, behind, tag):
        halves = _sum_pieces([_own_piece(p) for p in parts], landed, behind, "sum_pieces_" + tag)
        for name, mine, theirs in zip(group, halves, _swap_with_sibling(halves, "swap_with_sibling_" + tag)):
            flip = (lambda a: a.T) if name in transposed else (lambda a: a)
            g = _both_halves(mine, theirs)
            w, m, v = params[name]
            outs = _adamw(flip(w[0]), g, flip(m[0]), flip(v[0]), "adamw_" + name)
            grads[name], delta[name], new_m[name], new_v[name] = [flip(a)[None] for a in (g, *outs)]
        return outs[0]

    done = finish(big[:4], [p_gate, p_up, p_out, p_down], [l_gate, l_up, l_out, l_down], started, "ffn_out")
    p_in, l_in = _scatter_wait(in_send, in_recv, p_in_thru, l_in_thru, done)
    finish(big[4:], [p_in], [l_in], done, "in")
    rest = [name for name in names if name not in big]

    def rows_of(a):
        return a.reshape(-1, a.shape[-1])

    small = _adamw_small([rows_of(params[n][0]) for n in rest], [rows_of(grads[n]) for n in rest],
                         [rows_of(params[n][1]) for n in rest], [rows_of(params[n][2]) for n in rest])
    for dst, outs in zip((delta, new_m, new_v), small):
        for name, out in zip(rest, outs):
            dst[name] = out.reshape(params[name][0].shape)

    return (loss, grad_x[None], *[grads[n] for n in names], *[delta[n] for n in names], *[new_m[n] for n in names],
            *[new_v[n] for n in names])
```

```python
import functools
import math

import jax
import jax.numpy as jnp
from jax import lax
from jax.experimental import pallas as pl
from jax.experimental.pallas import tpu as pltpu

F32 = jnp.float32
BF16 = jnp.bfloat16

D_MODEL = 1024
N_META = 16
ATTN_W = 512
CONV_W = 512
HEAD_DIM = 64
N_HEADS = 8
N_KV = 2
GROUP = N_HEADS // N_KV
KV_W = N_KV * HEAD_DIM
BLOCK = 128
LEAD = BLOCK - N_META
CONV_K = 31
D_FF = 2816
IN_COLS = ATTN_W + 2 * KV_W + 2 * CONV_W
Q0, KV0, C0 = 0, ATTN_W, ATTN_W + 2 * KV_W
NORM_EPS = 1e-5
SCALE = 1.0 / math.sqrt(HEAD_DIM)
SLOPES = tuple(2.0 ** (-(8.0 / N_HEADS) * (h + 1)) for h in range(N_HEADS))
NEG = -1e30

ADAM_LR, ADAM_B1, ADAM_B2, ADAM_EPS, ADAM_WD, ADAM_STEP = 0.001, 0.9, 0.999, 1e-08, 0.01, 10

N_SHARD = 4
N_DEV = 8
ROW_QUANTUM = 768
HALO = 32
CONV_CHUNK = 32
FF_CHUNK = 256
FF_CHUNKS = tuple(slice(c, c + FF_CHUNK) for c in range(0, D_FF, FF_CHUNK))
VMEM_LIMIT = 60 * 1024 * 1024


def _cparams(n_axes=1):
    return pltpu.CompilerParams(dimension_semantics=("arbitrary",) * n_axes, vmem_limit_bytes=VMEM_LIMIT)


def _dot(a, b):
    return jnp.dot(a, b, preferred_element_type=F32)


def _dot_nt(a, b):
    return lax.dot_general(a, b, (((1,), (1,)), ((), ())), preferred_element_type=F32)


def _dot_tn(a, b):
    return lax.dot_general(a, b, (((0,), (0,)), ((), ())), preferred_element_type=F32)


def _sigmoid(x):
    return 1.0 / (1.0 + jnp.exp(-x))


def _row(tm, n):
    return pl.BlockSpec((tm, n), lambda i: (i, 0))


def _const(shape):
    return pl.BlockSpec(shape, lambda i: (0,) * len(shape))


def _resident(shape):
    return pl.BlockSpec(shape, lambda i: (0,) * len(shape), pipeline_mode=pl.Buffered(1))


def _rms_fwd(x, g):
    rstd = lax.rsqrt(jnp.mean(x * x, axis=-1, keepdims=True) + NORM_EPS)
    xhat = x * rstd
    return xhat * g, xhat, rstd


def _rms_bwd(dy, xhat, rstd, g):
    dxh = dy * g
    dx = rstd * (dxh - xhat * jnp.mean(dxh * xhat, axis=-1, keepdims=True))
    return dx, dy * xhat


def _in_proj(x, lead, g1, w_in_t, r, tm):
    seq = x.shape[0]
    n_sub = tm // BLOCK

    def body(*refs):
        x_refs = refs[:n_sub]
        lead_ref, g_ref, w_ref, h0_ref, q_ref, kv_ref, c_ref = refs[n_sub:]
        i = pl.program_id(0)
        pieces = []
        for k, x_ref in enumerate(x_refs):
            at = jnp.zeros((BLOCK, D_MODEL), jnp.int32) + (i * tm + (k - 1) * BLOCK)
            piece = jnp.where((at >= 0) & (at < seq), x_ref[...], 0.0)
            pieces.append(jnp.where(at < 0, lead_ref[...], piece) if k == 0 else piece)
        h = jnp.concatenate(pieces, axis=0)
        h0_ref[...] = h
        hn = _rms_fwd(h, g_ref[...])[0].astype(BF16)
        q_ref[...] = _dot_nt(hn, w_ref[Q0:KV0, :]).astype(BF16)
        kv_ref[...] = _dot_nt(hn, w_ref[KV0:C0, :]).astype(BF16)
        c_ref[...] = _dot_nt(hn, w_ref[C0:IN_COLS, :])

    def x_block(k):
        return pl.BlockSpec((BLOCK, D_MODEL), lambda i: (jnp.clip(n_sub * i - 1 + k, 0, seq // BLOCK - 1), 0))

    return pl.pallas_call(
        body, name="in_proj", grid=(r // tm,),
        in_specs=[x_block(k) for k in range(n_sub)] + [_const((BLOCK, D_MODEL)), _const((1, D_MODEL)), _const((IN_COLS, D_MODEL))],
        out_specs=[_row(tm, D_MODEL), _row(tm, ATTN_W), _row(tm, 2 * KV_W), _row(tm, 2 * CONV_W)],
        out_shape=[jax.ShapeDtypeStruct((r, D_MODEL), F32), jax.ShapeDtypeStruct((r, ATTN_W), BF16),
                   jax.ShapeDtypeStruct((r, 2 * KV_W), BF16), jax.ShapeDtypeStruct((r, 2 * CONV_W), F32)],
        compiler_params=_cparams(),
    )(*[x] * n_sub, lead, g1, w_in_t)


def _attn_bias_init(bias_ref, late_ref):
    row = lax.broadcasted_iota(jnp.int32, (GROUP * BLOCK, BLOCK), 0) & (BLOCK - 1)
    col = lax.broadcasted_iota(jnp.int32, (GROUP * BLOCK, BLOCK), 1)
    late_ref[...] = jnp.where(col > row, 1.0, 0.0)
    for g in range(N_KV):
        slope = jnp.concatenate([jnp.zeros((BLOCK, BLOCK), F32) + SLOPES[g * GROUP + j] for j in range(GROUP)], axis=0)
        bias_ref[g, :, 0:BLOCK] = jnp.where(col >= LEAD, 0.0, NEG)
        bias_ref[g, :, BLOCK:2 * BLOCK] = -slope * jnp.where(col > row, row - col + BLOCK, row - col).astype(F32)


def _attn_block_bias(late_ref, i):
    late = late_ref[...]
    meta0 = jnp.where(i == 0, NEG, 0.0)
    no_prev = jnp.where(i >= 2, 0.0, NEG)
    no_cur = jnp.where(i >= 1, 0.0, NEG)
    return late * meta0, late * no_prev + no_cur


def _attn_logits(s3, bias_ref, block_bias, prev_part, g):
    meta = s3[:, 0:BLOCK] * SCALE + (bias_ref[g, :, 0:BLOCK] + block_bias[0])
    band = jnp.where(prev_part, s3[:, BLOCK:2 * BLOCK], s3[:, 2 * BLOCK:3 * BLOCK]) * SCALE + (bias_ref[g, :, BLOCK:2 * BLOCK] + block_bias[1])
    return meta, band


def _split_band(meta, band, prev_part):
    return jnp.concatenate([meta, jnp.where(prev_part, band, 0.0), jnp.where(prev_part, 0.0, band)], axis=1)


def _head_rows(vals):
    return jnp.concatenate([jnp.zeros((BLOCK, BLOCK), F32) + v for v in vals], axis=0)


def _stack_heads(ref, g):
    return jnp.concatenate([ref[:, (g * GROUP + j) * HEAD_DIM:(g * GROUP + j + 1) * HEAD_DIM] for j in range(GROUP)], axis=0)


def _kv_cat(kvm_ref, kvp_ref, kvc_ref, g):
    ks = slice(g * HEAD_DIM, (g + 1) * HEAD_DIM)
    vs = slice(KV_W + g * HEAD_DIM, KV_W + (g + 1) * HEAD_DIM)
    kcat = jnp.concatenate([kvm_ref[:, ks], kvp_ref[:, ks], kvc_ref[:, ks]], axis=0)
    vcat = jnp.concatenate([kvm_ref[:, vs], kvp_ref[:, vs], kvc_ref[:, vs]], axis=0)
    return kcat, vcat


def _attn_fwd(q, kv, sinks, gathered):
    r = q.shape[0]
    nb = r // BLOCK
    n = len(gathered)

    def body(sink_ref, q_ref, kvc_ref, kvp_ref, kvm_ref, *rest):
        o_ref, lse_ref = rest[n:n + 2]
        dst = rest[n + 2:2 * n + 2]
        bias_ref, late_ref, send_sems, recv_sems = rest[2 * n + 2:]
        i = pl.program_id(0)

        @pl.when(i == 0)
        def _():
            for cp in _gather_ici(dst, send_sems, recv_sems)[0]:
                cp.start()
            _attn_bias_init(bias_ref, late_ref)

        lane = lax.broadcasted_iota(jnp.int32, (BLOCK, BLOCK), 1)
        lse_tile = jnp.zeros((BLOCK, BLOCK), F32)
        block_bias = _attn_block_bias(late_ref, i)
        prev_part = late_ref[...] > 0.5
        for g in range(N_KV):
            kcat, vcat = _kv_cat(kvm_ref, kvp_ref, kvc_ref, g)
            heads = range(g * GROUP, (g + 1) * GROUP)
            meta, band = _attn_logits(_dot_nt(_stack_heads(q_ref, g), kcat), bias_ref, block_bias, prev_part, g)
            sink = _head_rows([sink_ref[h] for h in heads])
            m = jnp.maximum(jnp.max(jnp.maximum(meta, band), axis=-1, keepdims=True), sink)
            p_meta, p_band = jnp.exp(meta - m), jnp.exp(band - m)
            l = jnp.sum(p_meta + p_band, axis=-1, keepdims=True) + jnp.exp(sink - m)
            o = _dot(_split_band(p_meta, p_band, prev_part).astype(BF16), vcat) * (1.0 / l)[:, 0:HEAD_DIM]
            lse = m + jnp.log(l)
            for j, h in enumerate(heads):
                o_ref[:, h * HEAD_DIM:(h + 1) * HEAD_DIM] = o[j * BLOCK:(j + 1) * BLOCK]
                lse_tile = jnp.where(lane == h, lse[j * BLOCK:(j + 1) * BLOCK], lse_tile)
        lse_ref[...] = lse_tile

        @pl.when(i == nb - 1)
        def _():
            sends, arrivals = _gather_ici(dst, send_sems, recv_sems)
            for cp in arrivals:
                cp.wait_recv()
            for cp in sends:
                cp.wait_send()

    return pl.pallas_call(
        body, name="attn_fwd", grid=(nb,),
        in_specs=[pl.BlockSpec(memory_space=pltpu.SMEM), _row(BLOCK, ATTN_W), _row(BLOCK, 2 * KV_W),
                  pl.BlockSpec((BLOCK, 2 * KV_W), lambda i: (jnp.maximum(i - 1, 0), 0)), _const((BLOCK, 2 * KV_W))] + [ANY] * n,
        out_specs=[_row(BLOCK, ATTN_W), _row(BLOCK, BLOCK)] + [ANY] * n,
        out_shape=[jax.ShapeDtypeStruct((r, ATTN_W), F32), jax.ShapeDtypeStruct((r, BLOCK), F32)]
        + [jax.ShapeDtypeStruct(g.shape, g.dtype) for g in gathered],
        input_output_aliases={5 + k: 2 + k for k in range(n)},
        scratch_shapes=[pltpu.VMEM((N_KV, GROUP * BLOCK, 2 * BLOCK), F32), pltpu.VMEM((GROUP * BLOCK, BLOCK), F32),
                        pltpu.SemaphoreType.DMA((3 * n,)), pltpu.SemaphoreType.DMA((3 * n,))],
        compiler_params=_cparams(),
    )(sinks, q, kv, kv, kv, *gathered)


def _shifted_copies(ub_ref, win):
    w = win.shape[0]
    ub_ref[0] = win
    for b in range(1, 8):
        ub_ref[b] = pltpu.roll(win, shift=w - b, axis=0)


def _conv_chunk(ub_ref, w_ref, r0, shifts):
    acc = jnp.zeros((CONV_CHUNK, CONV_W), F32)
    for j in range(CONV_K):
        a, b = divmod(shifts[j], 8)
        acc = acc + w_ref[j:j + 1, :] * ub_ref[b, pl.ds(r0 + 8 * a, CONV_CHUNK), :]
    return acc


FWD_SHIFTS = tuple(HALO - (CONV_K - 1) + j for j in range(CONV_K))
BWD_SHIFTS = tuple(CONV_K - 1 - j for j in range(CONV_K))


def _glu_window(cp_ref, c_ref, i):
    tile = c_ref[:, 0:CONV_W] * _sigmoid(c_ref[:, CONV_W:2 * CONV_W])
    halo = cp_ref[:, 0:CONV_W] * _sigmoid(cp_ref[:, CONV_W:2 * CONV_W])
    first = (jnp.zeros((HALO, CONV_W), jnp.int32) + i) == 0
    return jnp.concatenate([jnp.where(first, 0.0, halo), tile], axis=0)


def _halo_before(tm, n):
    return pl.BlockSpec((HALO, n), lambda i: (jnp.maximum(i * (tm // HALO) - 1, 0), 0))


def _conv_fwd(cacg, cw, cb, lg, lb, tm, gathered):
    r = cacg.shape[0]
    n = len(gathered)

    def body(c_ref, cp_ref, w_ref, cb_ref, lg_ref, lb_ref, *rest):
        o_ref, y_ref = rest[n:n + 2]
        dst = rest[n + 2:2 * n + 2]
        ub_ref, send_sems, recv_sems = rest[2 * n + 2:]
        i = pl.program_id(0)

        @pl.when(i == 0)
        def _():
            for cp in _gather_d2d(dst, send_sems, recv_sems)[0]:
                cp.start()

        _shifted_copies(ub_ref, _glu_window(cp_ref, c_ref, i))

        def chunk(ci, carry):
            r0 = pl.multiple_of(ci * CONV_CHUNK, CONV_CHUNK)
            y = _conv_chunk(ub_ref, w_ref, r0, FWD_SHIFTS) + cb_ref[...]
            yc = y - jnp.mean(y, axis=-1, keepdims=True)
            rs = lax.rsqrt(jnp.mean(yc * yc, axis=-1, keepdims=True) + NORM_EPS)
            yn = yc * rs * lg_ref[...] + lb_ref[...]
            o_ref[pl.ds(r0, CONV_CHUNK), :] = yn * _sigmoid(yn)
            y_ref[pl.ds(r0, CONV_CHUNK), :] = y
            return carry

        lax.fori_loop(0, tm // CONV_CHUNK, chunk, 0, unroll=4)

        @pl.when(i == r // tm - 1)
        def _():
            sends, arrivals = _gather_d2d(dst, send_sems, recv_sems)
            for cp in arrivals:
                cp.wait_recv()
            for cp in sends:
                cp.wait_send()

    return pl.pallas_call(
        body, name="conv_fwd", grid=(r // tm,),
        in_specs=[_row(tm, 2 * CONV_W), _halo_before(tm, 2 * CONV_W), _const((32, CONV_W)), _const((1, CONV_W)),
                  _const((1, CONV_W)), _const((1, CONV_W))] + [ANY] * n,
        out_specs=[_row(tm, CONV_W), _row(tm, CONV_W)] + [ANY] * n,
        out_shape=[jax.ShapeDtypeStruct((r, CONV_W), F32)] * 2 + [jax.ShapeDtypeStruct(g.shape, g.dtype) for g in gathered],
        input_output_aliases={6 + k: 2 + k for k in range(n)},
        scratch_shapes=[pltpu.VMEM((8, tm + HALO, CONV_W), F32), pltpu.SemaphoreType.DMA((3 * n,)), pltpu.SemaphoreType.DMA((3 * n,))],
        compiler_params=_cparams(),
    )(cacg, cacg, cw, cb, lg, lb, *gathered)


def _out_proj(oa, oc, h0, ga, gc, g2, w_out_b, tm):
    r = h0.shape[0]

    def body(oa_ref, oc_ref, h_ref, ga_ref, gc_ref, g2_ref, w_ref, h1_ref, hn2_ref):
        ma = _rms_fwd(oa_ref[...], ga_ref[...])[0].astype(BF16)
        mc = _rms_fwd(oc_ref[...], gc_ref[...])[0].astype(BF16)
        h1 = h_ref[...] + _dot(jnp.concatenate([ma, mc], axis=1), w_ref[...])
        h1_ref[...] = h1
        hn2_ref[...] = _rms_fwd(h1, g2_ref[...])[0].astype(BF16)

    return pl.pallas_call(
        body, name="out_proj", grid=(r // tm,),
        in_specs=[_row(tm, ATTN_W), _row(tm, CONV_W), _row(tm, D_MODEL), _const((1, ATTN_W)), _const((1, CONV_W)),
                  _const((1, D_MODEL)), _const((D_MODEL, D_MODEL))],
        out_specs=[_row(tm, D_MODEL), _row(tm, D_MODEL)],
        out_shape=[jax.ShapeDtypeStruct((r, D_MODEL), F32), jax.ShapeDtypeStruct((r, D_MODEL), BF16)],
        compiler_params=_cparams(),
    )(oa, oc, h0, ga, gc, g2, w_out_b)


def _ffn_fwd(hn2, h1, target, gf, wg_t, wu_t, wd_b, tm):
    r = h1.shape[0]
    seq = target.shape[0]
    n_sub = tm // BLOCK

    def body(hn_ref, h1_ref, *rest):
        t_refs = rest[:n_sub]
        gf_ref, wg_ref, wu_ref, wd_ref, gate_ref, up_ref, act_ref, dh2_ref, dh2b_ref, loss_ref, dgf_ref = rest[n_sub:]
        i = pl.program_id(0)

        @pl.when(i == 0)
        def _():
            loss_ref[...] = jnp.zeros_like(loss_ref)
            dgf_ref[...] = jnp.zeros_like(dgf_ref)

        hn = hn_ref[...]
        for cs in FF_CHUNKS:
            gate = _dot_nt(hn, wg_ref[cs, :])
            up = _dot_nt(hn, wu_ref[cs, :])
            gate_ref[:, cs] = gate.astype(BF16)
            up_ref[:, cs] = up.astype(BF16)
            act_ref[:, cs] = (gate * _sigmoid(gate) * up).astype(BF16)
        y, xhat, rstd = _rms_fwd(h1_ref[...] + _dot(act_ref[...], wd_ref[...]), gf_ref[...])
        rows = lax.broadcasted_iota(jnp.int32, (tm, D_MODEL), 0) + i * tm
        real = (rows >= BLOCK) & (rows < BLOCK + seq)
        err = jnp.where(real, y - jnp.concatenate([t[...] for t in t_refs], axis=0), 0.0)
        loss_ref[...] += jnp.sum(err * err) * (0.5 / D_MODEL)
        dy = err * (1.0 / D_MODEL)
        dh2, dg_rows = _rms_bwd(dy, xhat, rstd, gf_ref[...])
        dgf_ref[...] += jnp.sum(dg_rows, axis=0, keepdims=True)
        dh2_ref[...] = dh2
        dh2b_ref[...] = dh2.astype(BF16)

    def target_block(k):
        return pl.BlockSpec((BLOCK, D_MODEL), lambda i: (jnp.clip(n_sub * i - 1 + k, 0, seq // BLOCK - 1), 0))

    return pl.pallas_call(
        body, name="ffn_fwd", grid=(r // tm,),
        in_specs=[_row(tm, D_MODEL), _row(tm, D_MODEL)] + [target_block(k) for k in range(n_sub)]
        + [_const((1, D_MODEL))] + [_resident((D_FF, D_MODEL))] * 3,
        out_specs=[_row(tm, D_FF)] * 3 + [_row(tm, D_MODEL), _row(tm, D_MODEL), _const((1, BLOCK)), _const((1, D_MODEL))],
        out_shape=[jax.ShapeDtypeStruct((r, D_FF), BF16)] * 3
        + [jax.ShapeDtypeStruct((r, D_MODEL), F32), jax.ShapeDtypeStruct((r, D_MODEL), BF16),
           jax.ShapeDtypeStruct((1, BLOCK), F32), jax.ShapeDtypeStruct((1, D_MODEL), F32)],
        compiler_params=_cparams(),
    )(hn2, h1, *[target] * n_sub, gf, wg_t, wu_t, wd_b)


def _ffn_bwd(dh2, dh2b, gate, up, h1, g2, wg_t, wu_t, wd_b, tm):
    r = h1.shape[0]

    def body(dh2_ref, dh2b_ref, gate_ref, up_ref, h1_ref, g2_ref, wg_ref, wu_ref, wd_ref, dgate_ref, dup_ref, dh1_ref, dg2_ref):
        @pl.when(pl.program_id(0) == 0)
        def _():
            dg2_ref[...] = jnp.zeros_like(dg2_ref)

        dyb = dh2b_ref[...]
        for cs in FF_CHUNKS:
            dact = _dot_nt(dyb, wd_ref[cs, :])
            gate = gate_ref[:, cs].astype(F32)
            up = up_ref[:, cs].astype(F32)
            sg = _sigmoid(gate)
            dgate_ref[:, cs] = (dact * up * (sg * (1.0 + gate * (1.0 - sg)))).astype(BF16)
            dup_ref[:, cs] = (dact * (gate * sg)).astype(BF16)
        dhn = _dot(dgate_ref[...], wg_ref[...]) + _dot(dup_ref[...], wu_ref[...])
        _, xhat, rstd = _rms_fwd(h1_ref[...], g2_ref[...])
        dx, dg_rows = _rms_bwd(dhn, xhat, rstd, g2_ref[...])
        dg2_ref[...] += jnp.sum(dg_rows, axis=0, keepdims=True)
        dh1_ref[...] = dh2_ref[...] + dx

    return pl.pallas_call(
        body, name="ffn_bwd", grid=(r // tm,),
        in_specs=[_row(tm, D_MODEL), _row(tm, D_MODEL), _row(tm, D_FF), _row(tm, D_FF), _row(tm, D_MODEL), _const((1, D_MODEL))]
        + [_resident((D_FF, D_MODEL))] * 3,
        out_specs=[_row(tm, D_FF), _row(tm, D_FF), _row(tm, D_MODEL), _const((1, D_MODEL))],
        out_shape=[jax.ShapeDtypeStruct((r, D_FF), BF16), jax.ShapeDtypeStruct((r, D_FF), BF16),
                   jax.ShapeDtypeStruct((r, D_MODEL), F32), jax.ShapeDtypeStruct((1, D_MODEL), F32)],
        compiler_params=_cparams(),
    )(dh2, dh2b, gate, up, h1, g2, wg_t, wu_t, wd_b)


FF_HALF = D_FF // 2


def _ffn_wgrad_gu(hn2, dgate, dup, tk):
    r = hn2.shape[0]
    n_k = r // tk

    def body(hn_ref, dg_ref, du_ref, wg_ref, wu_ref, accg_ref, accu_ref):
        k = pl.program_id(1)

        @pl.when(k == 0)
        def _():
            accg_ref[...] = jnp.zeros_like(accg_ref)
            accu_ref[...] = jnp.zeros_like(accu_ref)

        hn = hn_ref[...]
        accg_ref[...] += _dot_tn(dg_ref[...], hn)
        accu_ref[...] += _dot_tn(du_ref[...], hn)

        @pl.when(k == n_k - 1)
        def _():
            wg_ref[...] = accg_ref[...].astype(BF16)
            wu_ref[...] = accu_ref[...].astype(BF16)

    col = pl.BlockSpec((tk, FF_HALF), lambda j, k: (k, j))
    out = pl.BlockSpec((FF_HALF, D_MODEL), lambda j, k: (j, 0))
    return pl.pallas_call(
        body, name="ffn_wgrad_gu", grid=(2, n_k),
        in_specs=[pl.BlockSpec((tk, D_MODEL), lambda j, k: (k, 0)), col, col],
        out_specs=[out, out],
        out_shape=[jax.ShapeDtypeStruct((D_FF, D_MODEL), BF16)] * 2,
        scratch_shapes=[pltpu.VMEM((FF_HALF, D_MODEL), F32)] * 2,
        compiler_params=_cparams(2),
    )(hn2, dgate, dup)


def _ffn_wgrad_d(act, dh2b, tk):
    r = act.shape[0]
    n_k = r // tk

    def body(a_ref, dy_ref, wd_ref, acc_ref):
        k = pl.program_id(1)

        @pl.when(k == 0)
        def _():
            acc_ref[...] = jnp.zeros_like(acc_ref)

        acc_ref[...] += _dot_tn(a_ref[...], dy_ref[...])

        @pl.when(k == n_k - 1)
        def _():
            wd_ref[...] = acc_ref[...].astype(BF16)

    return pl.pallas_call(
        body, name="ffn_wgrad_d", grid=(2, n_k),
        in_specs=[pl.BlockSpec((tk, FF_HALF), lambda j, k: (k, j)), pl.BlockSpec((tk, D_MODEL), lambda j, k: (k, 0))],
        out_specs=pl.BlockSpec((FF_HALF, D_MODEL), lambda j, k: (j, 0)),
        out_shape=jax.ShapeDtypeStruct((D_FF, D_MODEL), BF16),
        scratch_shapes=[pltpu.VMEM((FF_HALF, D_MODEL), F32)],
        compiler_params=_cparams(2),
    )(act, dh2b)


def _out_proj_bwd(dh1, oa, oc, ga, gc, w_out_b, tm):
    r = dh1.shape[0]

    def body(dh_ref, oa_ref, oc_ref, ga_ref, gc_ref, w_ref, doa_ref, doc_ref, dw_ref, dga_ref, dgc_ref, acc_ref):
        i = pl.program_id(0)

        @pl.when(i == 0)
        def _():
            acc_ref[...] = jnp.zeros_like(acc_ref)
            dga_ref[...] = jnp.zeros_like(dga_ref)
            dgc_ref[...] = jnp.zeros_like(dgc_ref)

        dhb = dh_ref[...].astype(BF16)
        dmix = _dot_nt(dhb, w_ref[...])
        ma, xa, ra = _rms_fwd(oa_ref[...], ga_ref[...])
        mc, xc, rc = _rms_fwd(oc_ref[...], gc_ref[...])
        acc_ref[...] += _dot_tn(jnp.concatenate([ma.astype(BF16), mc.astype(BF16)], axis=1), dhb)

        @pl.when(i == r // tm - 1)
        def _():
            dw_ref[...] = acc_ref[...].astype(BF16)

        doa, dga_rows = _rms_bwd(dmix[:, 0:ATTN_W], xa, ra, ga_ref[...])
        doc, dgc_rows = _rms_bwd(dmix[:, ATTN_W:ATTN_W + CONV_W], xc, rc, gc_ref[...])
        doa_ref[...] = doa
        doc_ref[...] = doc
        dga_ref[...] += jnp.sum(dga_rows, axis=0, keepdims=True)
        dgc_ref[...] += jnp.sum(dgc_rows, axis=0, keepdims=True)

    return pl.pallas_call(
        body, name="out_proj_bwd", grid=(r // tm,),
        in_specs=[_row(tm, D_MODEL), _row(tm, ATTN_W), _row(tm, CONV_W), _const((1, ATTN_W)), _const((1, CONV_W)),
                  _const((D_MODEL, D_MODEL))],
        out_specs=[_row(tm, ATTN_W), _row(tm, CONV_W), _const((D_MODEL, D_MODEL)), _const((1, ATTN_W)), _const((1, CONV_W))],
        out_shape=[jax.ShapeDtypeStruct((r, ATTN_W), F32), jax.ShapeDtypeStruct((r, CONV_W), F32),
                   jax.ShapeDtypeStruct((D_MODEL, D_MODEL), BF16), jax.ShapeDtypeStruct((1, ATTN_W), F32),
                   jax.ShapeDtypeStruct((1, CONV_W), F32)],
        scratch_shapes=[pltpu.VMEM((D_MODEL, D_MODEL), F32)],
        compiler_params=_cparams(),
    )(dh1, oa, oc, ga, gc, w_out_b)


def _conv_bwd_params(doc, y, cacg, lg, lb, tm, parts):
    r = cacg.shape[0]
    n_steps = r // tm
    n = len(parts)

    def body(do_ref, y_ref, c_ref, cp_ref, lg_ref, lb_ref, *rest):
        src = rest[:n]
        dy_ref, dcw_ref, dcb_ref, dlg_ref, dlb_ref = rest[n:n + 5]
        dst = rest[n + 5:2 * n + 5]
        ub_ref, accw_ref, send_sems, recv_sems = rest[2 * n + 5:]
        i = pl.program_id(0)

        @pl.when(i == 0)
        def _():
            for cp in _scatter(src, dst, send_sems, recv_sems):
                cp.start()
            accw_ref[...] = jnp.zeros_like(accw_ref)
            dcb_ref[...] = jnp.zeros_like(dcb_ref)
            dlg_ref[...] = jnp.zeros_like(dlg_ref)
            dlb_ref[...] = jnp.zeros_like(dlb_ref)

        _shifted_copies(ub_ref, _glu_window(cp_ref, c_ref, i))

        def chunk(ci, carry):
            r0 = pl.multiple_of(ci * CONV_CHUNK, CONV_CHUNK)
            y = y_ref[pl.ds(r0, CONV_CHUNK), :]
            yc = y - jnp.mean(y, axis=-1, keepdims=True)
            rs = lax.rsqrt(jnp.mean(yc * yc, axis=-1, keepdims=True) + NORM_EPS)
            xhat = yc * rs
            yn = xhat * lg_ref[...] + lb_ref[...]
            sg = _sigmoid(yn)
            dyn = do_ref[pl.ds(r0, CONV_CHUNK), :] * (sg * (1.0 + yn * (1.0 - sg)))
            dlg_ref[...] += jnp.sum(dyn * xhat, axis=0, keepdims=True)
            dlb_ref[...] += jnp.sum(dyn, axis=0, keepdims=True)
            dxh = dyn * lg_ref[...]
            dy = rs * (dxh - jnp.mean(dxh, axis=-1, keepdims=True) - xhat * jnp.mean(dxh * xhat, axis=-1, keepdims=True))
            dcb_ref[...] += jnp.sum(dy, axis=0, keepdims=True)
            dy_ref[pl.ds(r0, CONV_CHUNK), :] = dy
            for j in range(CONV_K):
                a, b = divmod(FWD_SHIFTS[j], 8)
                prod = dy * ub_ref[b, pl.ds(r0 + 8 * a, CONV_CHUNK), :]
                accw_ref[j] += jnp.sum(prod.reshape(CONV_CHUNK // 8, 8, CONV_W), axis=0)
            return carry

        lax.fori_loop(0, tm // CONV_CHUNK, chunk, 0, unroll=4)

        @pl.when(i == n_steps - 1)
        def _():
            for j in range(32):
                dcw_ref[j:j + 1, :] = jnp.sum(accw_ref[j], axis=0, keepdims=True)
            for cp in _scatter(src, dst, send_sems, recv_sems):
                cp.wait()

    vec = _const((1, CONV_W))
    return pl.pallas_call(
        body, name="conv_bwd_params", grid=(n_steps,),
        in_specs=[_row(tm, CONV_W), _row(tm, CONV_W), _row(tm, 2 * CONV_W), _halo_before(tm, 2 * CONV_W), vec, vec] + [ANY] * n,
        out_specs=[_row(tm, CONV_W), _const((32, CONV_W)), vec, vec, vec] + [ANY] * n,
        out_shape=[jax.ShapeDtypeStruct((r, CONV_W), F32), jax.ShapeDtypeStruct((32, CONV_W), F32)]
        + [jax.ShapeDtypeStruct((1, CONV_W), F32)] * 3 + _scatter_landing(parts),
        scratch_shapes=[pltpu.VMEM((8, tm + HALO, CONV_W), F32), pltpu.VMEM((32, 8, CONV_W), F32),
                        pltpu.SemaphoreType.DMA((7 * n,)), pltpu.SemaphoreType.DMA((7 * n,))],
        compiler_params=_cparams(),
    )(doc, y, cacg, cacg, lg, lb, *parts)


def _conv_bwd_data(dy, cacg, cw, tm):
    r = cacg.shape[0]
    n_steps = r // tm

    def body(dy_ref, dyn_ref, c_ref, w_ref, dc_ref, ub_ref):
        last = (jnp.zeros((HALO, CONV_W), jnp.int32) + pl.program_id(0)) == n_steps - 1
        win = jnp.concatenate([dy_ref[...], jnp.where(last, 0.0, dyn_ref[...])], axis=0)
        _shifted_copies(ub_ref, win)

        def chunk(ci, carry):
            r0 = pl.multiple_of(ci * CONV_CHUNK, CONV_CHUNK)
            du = _conv_chunk(ub_ref, w_ref, r0, BWD_SHIFTS)
            ca = c_ref[pl.ds(r0, CONV_CHUNK), 0:CONV_W]
            sg = _sigmoid(c_ref[pl.ds(r0, CONV_CHUNK), CONV_W:2 * CONV_W])
            dc_ref[pl.ds(r0, CONV_CHUNK), 0:CONV_W] = (du * sg).astype(BF16)
            dc_ref[pl.ds(r0, CONV_CHUNK), CONV_W:2 * CONV_W] = (du * ca * sg * (1.0 - sg)).astype(BF16)
            return carry

        lax.fori_loop(0, tm // CONV_CHUNK, chunk, 0, unroll=4)

    halo_after = pl.BlockSpec((HALO, CONV_W), lambda i: (jnp.minimum((i + 1) * (tm // HALO), r // HALO - 1), 0))
    return pl.pallas_call(
        body, name="conv_bwd_data", grid=(n_steps,),
        in_specs=[_row(tm, CONV_W), halo_after, _row(tm, 2 * CONV_W), _const((32, CONV_W))],
        out_specs=_row(tm, 2 * CONV_W),
        out_shape=jax.ShapeDtypeStruct((r, 2 * CONV_W), BF16),
        scratch_shapes=[pltpu.VMEM((8, tm + HALO, CONV_W), F32)],
        compiler_params=_cparams(),
    )(dy, dy, cacg, cw)


def _attn_bwd(q, kv, o, do, lse, sinks, parts):
    r = q.shape[0]
    nb = r // BLOCK
    n = len(parts)

    def body(sink_ref, q_ref, kvc_ref, kvp_ref, kvm_ref, o_ref, do_ref, lse_ref, *rest):
        src = rest[:n]
        dq_ref, dkv_ref, dmeta_ref, dsink_ref = rest[n:n + 4]
        dst = rest[n + 4:2 * n + 4]
        hold_ref, bias_ref, late_ref, send_sems, recv_sems = rest[2 * n + 4:]
        i = pl.program_id(0)

        @pl.when(i == 0)
        def _():
            for cp in _scatter(src, dst, send_sems, recv_sems):
                cp.start()
            _attn_bias_init(bias_ref, late_ref)
            dmeta_ref[...] = jnp.zeros_like(dmeta_ref)
            dsink_ref[...] = jnp.zeros_like(dsink_ref)
            hold_ref[...] = jnp.zeros_like(hold_ref)

        @pl.when(i < nb)
        def _():
            lane = lax.broadcasted_iota(jnp.int32, (BLOCK, BLOCK), 1)
            lse_tile = lse_ref[...]
            zero = jnp.zeros((BLOCK, BLOCK), F32)
            block_bias = _attn_block_bias(late_ref, i)
            prev_part = late_ref[...] > 0.5
            for g in range(N_KV):
                kcat, vcat = _kv_cat(kvm_ref, kvp_ref, kvc_ref, g)
                heads = range(g * GROUP, (g + 1) * GROUP)
                qs = _stack_heads(q_ref, g)
                dos = _stack_heads(do_ref, g)
                dosb = dos.astype(BF16)
                lse = jnp.concatenate(
                    [jnp.sum(jnp.where(lane == h, lse_tile, 0.0), axis=-1, keepdims=True) + zero for h in heads], axis=0)
                delta = jnp.sum(dos * _stack_heads(o_ref, g), axis=-1, keepdims=True) + jnp.zeros((GROUP * BLOCK, BLOCK), F32)
                band_bias = bias_ref[g, :, BLOCK:2 * BLOCK] + block_bias[1]
                bias = [bias_ref[g, :, 0:BLOCK] + block_bias[0], jnp.where(prev_part, band_bias, NEG), jnp.where(prev_part, NEG, band_bias)]
                s = _dot_nt(qs, kcat)
                dp = _dot_nt(dosb, vcat)
                ps = [jnp.exp(s[:, k * BLOCK:(k + 1) * BLOCK] * SCALE + bias[k] - lse) for k in range(3)]
                p = jnp.concatenate(ps, axis=1)
                ds = jnp.concatenate(
                    [(ps[k] * (dp[:, k * BLOCK:(k + 1) * BLOCK] - delta)) * SCALE for k in range(3)], axis=1).astype(BF16)
                sink_term = jnp.exp(_head_rows([sink_ref[h] for h in heads]) - lse)[:, 0:1] * delta[:, 0:1]
                dq = _dot(ds, kcat).astype(BF16)
                for j, h in enumerate(heads):
                    dsink_ref[h:h + 1, :] += -jnp.sum(sink_term[j * BLOCK:(j + 1) * BLOCK])
                    dq_ref[:, h * HEAD_DIM:(h + 1) * HEAD_DIM] = dq[j * BLOCK:(j + 1) * BLOCK]
                dk_t = _dot_tn(qs, ds)
                dv_t = _dot_tn(dosb, p.astype(BF16))
                ks = slice(g * HEAD_DIM, (g + 1) * HEAD_DIM)
                vs = slice(KV_W + g * HEAD_DIM, KV_W + (g + 1) * HEAD_DIM)
                for sl, grad_t in ((ks, dk_t), (vs, dv_t)):
                    dmeta_ref[:, sl] += grad_t[:, 0:BLOCK].T
                    dkv_ref[:, sl] = hold_ref[:, sl] + grad_t[:, BLOCK:2 * BLOCK].T
                    hold_ref[:, sl] = grad_t[:, 2 * BLOCK:3 * BLOCK].T

        @pl.when(i == nb)
        def _():
            dkv_ref[...] = hold_ref[...]
            for cp in _scatter(src, dst, send_sems, recv_sems):
                cp.wait()

    def cur(i):
        return jnp.minimum(i, nb - 1)

    return pl.pallas_call(
        body, name="attn_bwd", grid=(nb + 1,),
        in_specs=[pl.BlockSpec(memory_space=pltpu.SMEM),
                  pl.BlockSpec((BLOCK, ATTN_W), lambda i: (cur(i), 0)),
                  pl.BlockSpec((BLOCK, 2 * KV_W), lambda i: (cur(i), 0)),
                  pl.BlockSpec((BLOCK, 2 * KV_W), lambda i: (jnp.maximum(cur(i) - 1, 0), 0)),
                  _const((BLOCK, 2 * KV_W)),
                  pl.BlockSpec((BLOCK, ATTN_W), lambda i: (cur(i), 0)),
                  pl.BlockSpec((BLOCK, ATTN_W), lambda i: (cur(i), 0)),
                  pl.BlockSpec((BLOCK, BLOCK), lambda i: (cur(i), 0))] + [ANY] * n,
        out_specs=[pl.BlockSpec((BLOCK, ATTN_W), lambda i: (cur(i), 0)),
                   pl.BlockSpec((BLOCK, 2 * KV_W), lambda i: (jnp.maximum(i - 1, 0), 0)),
                   _const((BLOCK, 2 * KV_W)), _const((N_HEADS, BLOCK))] + [ANY] * n,
        out_shape=[jax.ShapeDtypeStruct((r, ATTN_W), BF16), jax.ShapeDtypeStruct((r, 2 * KV_W), F32),
                   jax.ShapeDtypeStruct((BLOCK, 2 * KV_W), F32), jax.ShapeDtypeStruct((N_HEADS, BLOCK), F32)] + _scatter_landing(parts),
        scratch_shapes=[pltpu.VMEM((BLOCK, 2 * KV_W), F32), pltpu.VMEM((N_KV, GROUP * BLOCK, 2 * BLOCK), F32),
                        pltpu.VMEM((GROUP * BLOCK, BLOCK), F32), pltpu.SemaphoreType.DMA((7 * n,)), pltpu.SemaphoreType.DMA((7 * n,))],
        compiler_params=_cparams(),
    )(sinks, q, kv, kv, kv, o, do, lse, *parts)


def _in_proj_bwd(dq, dkv, dkv_meta, dc, dh1, h0, g1, w_in_t, seq, tm):
    r = h0.shape[0]
    n_tiles = r // tm
    n_out = -(-seq // tm)

    def body(dq_ref, dkv_ref, dm_ref, dc_ref, dh1_ref, h_ref, g_ref, w_ref, gx_ref, lead_ref, dwt_ref, dg_ref, dw_ref, hold_ref):
        i = pl.program_id(0)

        @pl.when(i == 0)
        def _():
            dw_ref[...] = jnp.zeros_like(dw_ref)
            dg_ref[...] = jnp.zeros_like(dg_ref)

        @pl.when(i < n_tiles)
        def _():
            meta = jnp.concatenate([dm_ref[...], jnp.zeros((tm - BLOCK, 2 * KV_W), F32)], axis=0) if tm > BLOCK else dm_ref[...]
            first = (jnp.zeros((tm, 2 * KV_W), jnp.int32) + i) == 0
            dkvb = (dkv_ref[...] + jnp.where(first, meta, 0.0)).astype(BF16)
            hn, xhat, rstd = _rms_fwd(h_ref[...], g_ref[...])
            dproj = jnp.concatenate([dq_ref[...], dkvb, dc_ref[...]], axis=1)
            dhn = _dot(dproj, w_ref[...])
            dw_ref[...] += _dot_tn(dproj, hn.astype(BF16))
            dx, dg_rows = _rms_bwd(dhn, xhat, rstd, g_ref[...])
            dg_ref[...] += jnp.sum(dg_rows, axis=0, keepdims=True)
            dh0 = dh1_ref[...] + dx

            @pl.when(i == 0)
            def _():
                lead_ref[...] = dh0[0:BLOCK]

            @pl.when((i >= 1) & (i <= n_out))
            def _():
                gx_ref[0:tm - BLOCK, :] = hold_ref[...]
                gx_ref[tm - BLOCK:tm, :] = dh0[0:BLOCK]

            hold_ref[...] = dh0[BLOCK:tm]

        @pl.when((i == n_tiles) & (n_tiles <= n_out))
        def _():
            gx_ref[0:tm - BLOCK, :] = hold_ref[...]

        @pl.when(i == n_tiles - 1)
        def _():
            dwt_ref[...] = dw_ref[...].astype(BF16)

    def tile(n):
        return pl.BlockSpec((tm, n), lambda i: (jnp.minimum(i, n_tiles - 1), 0))

    return pl.pallas_call(
        body, name="in_proj_bwd", grid=(n_tiles + 1,),
        in_specs=[tile(ATTN_W), tile(2 * KV_W), _const((BLOCK, 2 * KV_W)), tile(2 * CONV_W), tile(D_MODEL), tile(D_MODEL),
                  _const((1, D_MODEL)), _const((IN_COLS, D_MODEL))],
        out_specs=[pl.BlockSpec((tm, D_MODEL), lambda i: (jnp.clip(i - 1, 0, n_out - 1), 0)), _const((BLOCK, D_MODEL)),
                   _const((IN_COLS, D_MODEL)), _const((1, D_MODEL))],
        out_shape=[jax.ShapeDtypeStruct((seq, D_MODEL), F32), jax.ShapeDtypeStruct((BLOCK, D_MODEL), F32),
                   jax.ShapeDtypeStruct((IN_COLS, D_MODEL), BF16), jax.ShapeDtypeStruct((1, D_MODEL), F32)],
        scratch_shapes=[pltpu.VMEM((IN_COLS, D_MODEL), F32), pltpu.VMEM((tm - BLOCK, D_MODEL), F32)],
        compiler_params=_cparams(),
    )(dq, dkv, dkv_meta, dc, dh1, h0, g1, w_in_t)


def _adamw_update(w_ref, g_ref, m_ref, v_ref, d_ref, nm_ref, nv_ref):
    g = g_ref[...]
    m = ADAM_B1 * m_ref[...] + (1.0 - ADAM_B1) * g
    v = ADAM_B2 * v_ref[...] + (1.0 - ADAM_B2) * (g * g)
    m_hat = m / (1.0 - ADAM_B1 ** ADAM_STEP)
    v_hat = v / (1.0 - ADAM_B2 ** ADAM_STEP)
    d_ref[...] = -ADAM_LR * (m_hat / (jnp.sqrt(v_hat) + ADAM_EPS) + ADAM_WD * w_ref[...])
    nm_ref[...] = m
    nv_ref[...] = v


def _adamw(w, g, m, v, name):
    rows, cols = w.shape
    tr = rows
    for cand in (256, 176, 128, 64, 32, 16, 8):
        if rows % cand == 0:
            tr = cand
            break

    def body(*refs):
        _adamw_update(*refs)

    spec = _row(tr, cols)
    return pl.pallas_call(
        body, name=name, grid=(rows // tr,), in_specs=[spec] * 4, out_specs=[spec] * 3,
        out_shape=[jax.ShapeDtypeStruct((rows, cols), F32)] * 3, compiler_params=_cparams(),
    )(w, g, m, v)


MESH = pl.DeviceIdType.MESH
ANY = pl.BlockSpec(memory_space=pl.ANY)


def _place():
    x, y, c = lax.axis_index("x"), lax.axis_index("y"), lax.axis_index("c")
    chips = [(1 - x, y), (x, 1 - y), (1 - x, 1 - y)]
    return x, y, c, chips


def _gather_ici(dst, send_sems, recv_sems):
    x, y, c, chips = _place()
    sends, arrivals = [], []
    for k in range(len(dst)):
        rows = dst[k].shape[1] // 2
        half = pl.ds(c * rows, rows)
        mine = dst[k].at[2 * x + y, half]
        for p, chip in enumerate(chips):
            sems = dict(send_sem=send_sems.at[3 * k + p], recv_sem=recv_sems.at[3 * k + p], device_id=(chip[0], chip[1], c),
                        device_id_type=MESH)
            sends.append(pltpu.make_async_remote_copy(src_ref=mine, dst_ref=mine, **sems))
            theirs = dst[k].at[2 * chip[0] + chip[1], half]
            arrivals.append(pltpu.make_async_remote_copy(src_ref=theirs, dst_ref=theirs, **sems))
    return sends, arrivals


def _gather_d2d(dst, send_sems, recv_sems):
    x, y, c, chips = _place()
    sends, arrivals = [], []
    for k in range(len(dst)):
        rows = dst[k].shape[1] // 2
        for p, chip in enumerate(chips):
            sems = dict(send_sem=send_sems.at[3 * k + p], recv_sem=recv_sems.at[3 * k + p], device_id=(x, y, 1 - c),
                        device_id_type=MESH)
            mine = dst[k].at[2 * chip[0] + chip[1], pl.ds(c * rows, rows)]
            sends.append(pltpu.make_async_remote_copy(src_ref=mine, dst_ref=mine, **sems))
            theirs = dst[k].at[2 * chip[0] + chip[1], pl.ds((1 - c) * rows, rows)]
            arrivals.append(pltpu.make_async_remote_copy(src_ref=theirs, dst_ref=theirs, **sems))
    return sends, arrivals


def _own_slots(shard, dtype):
    return jnp.broadcast_to(shard[None], (N_SHARD,) + shard.shape).astype(dtype)


def _gather_weights(slots):
    n = len(slots)

    def body(*refs):
        dst = refs[n:2 * n]
        ici_send, ici_recv, d2d_send, d2d_recv = refs[2 * n:]
        sends, arrivals = _gather_ici(dst, ici_send, ici_recv)
        for cp in sends:
            cp.start()
        for cp in arrivals:
            cp.wait_recv()
        forwards, from_sibling = _gather_d2d(dst, d2d_send, d2d_recv)
        for cp in forwards:
            cp.start()
        for cp in from_sibling:
            cp.wait_recv()
        for cp in sends + forwards:
            cp.wait_send()

    return pl.pallas_call(
        body, name="gather_weights",
        in_specs=[ANY] * n, out_specs=[ANY] * n,
        out_shape=[jax.ShapeDtypeStruct(s.shape, s.dtype) for s in slots],
        input_output_aliases={k: k for k in range(n)},
        scratch_shapes=[pltpu.SemaphoreType.DMA((3 * n,))] * 4,
    )(*slots)


VMEM_WHOLE = pl.BlockSpec(memory_space=pltpu.VMEM)


def _allreduce_small(parts, behind):
    widths = sorted({p.shape[1] for p in parts})
    place, heights = [], [0] * len(widths)
    for p in parts:
        gi = widths.index(p.shape[1])
        place.append((gi, heights[gi]))
        heights[gi] += -(-p.shape[0] // 8) * 8
    n, ng = len(parts), len(widths)

    def body(*refs):
        ins, outs, slots = refs[:n], refs[n + 1:2 * n + 1], refs[2 * n + 1:2 * n + 1 + ng]
        send_sems, recv_sems = refs[2 * n + 1 + ng:]
        x, y, c = lax.axis_index("x"), lax.axis_index("y"), lax.axis_index("c")
        me = 4 * x + 2 * y + c
        for gi in range(ng):
            slots[gi][me] = jnp.zeros((heights[gi], widths[gi]), F32)
        for k, (gi, r0) in enumerate(place):
            slots[gi][me, r0:r0 + parts[k].shape[0], :] = ins[k][...]

        def copy(gi, j, arriving):
            peer = ((x + (j >> 2)) % 2, (y + ((j >> 1) & 1)) % 2, (c + (j & 1)) % 2)
            slot = 4 * peer[0] + 2 * peer[1] + peer[2] if arriving else me
            return pltpu.make_async_remote_copy(
                src_ref=slots[gi].at[me], dst_ref=slots[gi].at[slot], send_sem=send_sems.at[7 * gi + j - 1],
                recv_sem=recv_sems.at[7 * gi + j - 1], device_id=peer, device_id_type=MESH)

        pairs = [(gi, j) for gi in range(ng) for j in range(1, N_DEV)]
        for gi, j in pairs:
            copy(gi, j, False).start()
        for gi, j in pairs:
            copy(gi, j, True).wait_recv()
        totals = []
        for gi in range(ng):
            total = slots[gi][0]
            for d in range(1, N_DEV):
                total = total + slots[gi][d]
            totals.append(total)
        for k, (gi, r0) in enumerate(place):
            outs[k][...] = totals[gi][r0:r0 + parts[k].shape[0], :]
        for gi, j in pairs:
            copy(gi, j, False).wait_send()

    return pl.pallas_call(
        body, name="allreduce_small", in_specs=[VMEM_WHOLE] * n + [ANY], out_specs=[VMEM_WHOLE] * n,
        out_shape=[jax.ShapeDtypeStruct(p.shape, F32) for p in parts],
        scratch_shapes=[pltpu.VMEM((N_DEV, heights[gi], widths[gi]), F32) for gi in range(ng)]
        + [pltpu.SemaphoreType.DMA((7 * ng,))] * 2,
    )(*parts, behind)


def _adamw_small(ws, gs, ms, vs):
    n = len(ws)

    def body(*refs):
        for k in range(n):
            w_ref, g_ref, m_ref, v_ref = (refs[j * n + k] for j in range(4))
            _adamw_update(w_ref, g_ref, m_ref, v_ref, *(refs[(4 + j) * n + k] for j in range(3)))

    shapes = [jax.ShapeDtypeStruct(w.shape, F32) for w in ws]
    out = pl.pallas_call(
        body, name="adamw_small", in_specs=[VMEM_WHOLE] * (4 * n), out_specs=[VMEM_WHOLE] * (3 * n), out_shape=shapes * 3,
    )(*ws, *gs, *ms, *vs)
    return out[:n], out[n:2 * n], out[2 * n:]


def _scatter(src, dst, send_sems, recv_sems):
    x, y, c = lax.axis_index("x"), lax.axis_index("y"), lax.axis_index("c")
    copies = []
    for k in range(len(src)):
        rows = src[k].shape[1] // 2
        for j in range(1, N_DEV):
            px, py, pc = (x + (j >> 2)) % 2, (y + ((j >> 1) & 1)) % 2, (c + (j & 1)) % 2
            copies.append(pltpu.make_async_remote_copy(
                src_ref=src[k].at[2 * px + py, pl.ds(pc * rows, rows)], dst_ref=dst[k].at[j - 1],
                send_sem=send_sems.at[7 * k + j - 1], recv_sem=recv_sems.at[7 * k + j - 1], device_id=(px, py, pc),
                device_id_type=MESH))
    return copies


def _scatter_landing(parts):
    return [jax.ShapeDtypeStruct((N_DEV - 1, p.shape[1] // 2, p.shape[2]), p.dtype) for p in parts]


HBM = pl.BlockSpec(memory_space=pltpu.HBM)
SEMAPHORES = pl.BlockSpec(memory_space=pltpu.SEMAPHORE)
SPLIT_COPY = pltpu.CompilerParams(has_side_effects=pltpu.SideEffectType.DATAFLOW_SIDE_EFFECTING)


def _scatter_start(part):
    landing, = _scatter_landing([part])

    def body(src_ref, land_ref, send_sems, recv_sems, src_thru, land_thru, token_ref):
        for cp in _scatter([src_ref], [land_ref], send_sems, recv_sems):
            cp.start()
        token_ref[...] = jnp.zeros_like(token_ref)

    return pl.pallas_call(
        body, name="scatter_start",
        out_shape=(pltpu.SemaphoreType.DMA((N_DEV - 1,)), pltpu.SemaphoreType.DMA((N_DEV - 1,)), pltpu.HBM(part.shape, part.dtype),
                   pltpu.HBM(landing.shape, landing.dtype), jax.ShapeDtypeStruct((8, 128), F32)),
        in_specs=(HBM, HBM), out_specs=(SEMAPHORES, SEMAPHORES, HBM, HBM, VMEM_WHOLE), input_output_aliases={0: 2, 1: 3},
        compiler_params=SPLIT_COPY,
    )(pltpu.with_memory_space_constraint(part, pltpu.HBM),
      pltpu.with_memory_space_constraint(lax.empty(landing.shape, landing.dtype), pltpu.HBM))


def _scatter_wait(send_sems, recv_sems, part_thru, land_thru, after):
    def body(src_ref, land_ref, send_ref, recv_ref, after_ref, src_dead, got_ref):
        for cp in _scatter([src_ref], [land_ref], send_ref, recv_ref):
            cp.wait_send()
            cp.wait_recv()

    return pl.pallas_call(
        body, name="scatter_wait",
        out_shape=(pltpu.HBM(part_thru.shape, part_thru.dtype), pltpu.HBM(land_thru.shape, land_thru.dtype)),
        in_specs=(HBM, HBM, SEMAPHORES, SEMAPHORES, ANY), out_specs=(HBM, HBM), input_output_aliases={0: 0, 1: 1},
        compiler_params=SPLIT_COPY,
    )(part_thru, land_thru, send_sems, recv_sems, after)


def _sum_pieces(own, landed, behind, name):
    n = len(own)

    def body(*refs):
        for k in range(n):
            got = refs[n + k]
            total = refs[k][...].astype(F32)
            for j in range(N_DEV - 1):
                total = total + got[j].astype(F32)
            refs[2 * n + 1 + k][...] = total

    in_specs, out_specs = [], []
    for o in own:
        in_specs.append(_row(o.shape[0] // 2, o.shape[1]))
    for o in own:
        in_specs.append(pl.BlockSpec((N_DEV - 1, o.shape[0] // 2, o.shape[1]), lambda i: (0, i, 0)))
        out_specs.append(_row(o.shape[0] // 2, o.shape[1]))
    return pl.pallas_call(
        body, name=name, grid=(2,), in_specs=in_specs + [ANY], out_specs=out_specs,
        out_shape=[jax.ShapeDtypeStruct(o.shape, F32) for o in own], compiler_params=_cparams(),
    )(*own, *landed, behind)


def _swap_with_sibling(halves, name):
    n = len(halves)

    def body(*refs):
        x, y, c = lax.axis_index("x"), lax.axis_index("y"), lax.axis_index("c")
        copies = [pltpu.make_async_remote_copy(
            src_ref=refs[k], dst_ref=refs[n + k], send_sem=refs[2 * n].at[k], recv_sem=refs[2 * n + 1].at[k],
            device_id=(x, y, 1 - c), device_id_type=MESH) for k in range(n)]
        for cp in copies:
            cp.start()
        for cp in copies:
            cp.wait()

    return pl.pallas_call(
        body, name=name, in_specs=[ANY] * n, out_specs=[ANY] * n,
        out_shape=[jax.ShapeDtypeStruct(h.shape, h.dtype) for h in halves],
        scratch_shapes=[pltpu.SemaphoreType.DMA((n,)), pltpu.SemaphoreType.DMA((n,))],
    )(*halves)


def _own_piece(part):
    rows = part.shape[1] // 2
    s = 2 * lax.axis_index("x") + lax.axis_index("y")
    return lax.dynamic_slice(part, (s, lax.axis_index("c") * rows, 0), (1, rows, part.shape[2]))[0]


def _both_halves(mine, theirs):
    south = lax.axis_index("c") == 0
    return jnp.concatenate([jnp.where(south, mine, theirs), jnp.where(south, theirs, mine)], axis=0)


def _from_col_shards(g):
    return g.transpose(1, 0, 2).reshape(g.shape[1], -1)


def kernel(x, meta_tokens, attn_norm_g, w_in, attn_sinks, conv_w, conv_b, conv_ln_g, conv_ln_b, attn_out_g, conv_out_g, w_out, ffn_norm_g, w_gate, w_up, w_down, final_norm_g, loss_target, m_meta_tokens, m_attn_norm_g, m_w_in, m_attn_sinks, m_conv_w, m_conv_b, m_conv_ln_g, m_conv_ln_b, m_attn_out_g, m_conv_out_g, m_w_out, m_ffn_norm_g, m_w_gate, m_w_up, m_w_down, m_final_norm_g, v_meta_tokens, v_attn_norm_g, v_w_in, v_attn_sinks, v_conv_w, v_conv_b, v_conv_ln_g, v_conv_ln_b, v_attn_out_g, v_conv_out_g, v_w_out, v_ffn_norm_g, v_w_gate, v_w_up, v_w_down, v_final_norm_g):
    seq = x.shape[1]
    r = -(-(seq + BLOCK) // ROW_QUANTUM) * ROW_QUANTUM
    tm_wide = 768 if seq >= 768 else 256
    shard = 2 * lax.axis_index("x") + lax.axis_index("y")

    conv_w32 = jnp.pad(conv_w[0], ((0, 1), (0, 0)))
    small_shard = jnp.concatenate([meta_tokens, conv_w32.reshape(16, 256)], axis=0)
    g_in, g_small = _gather_weights([_own_slots(w_in[0].T, BF16), _own_slots(small_shard, F32)])
    later = [_own_slots(w, BF16) for w in (w_gate[0].T, w_up[0].T, w_out[0], w_down[0])]
    w_in_t = g_in.reshape(IN_COLS, D_MODEL)
    meta_full = _from_col_shards(g_small[:, 0:N_META])
    cw_full = _from_col_shards(g_small[:, N_META:].reshape(N_SHARD, 32, 128))

    g1, ga, gc, g2 = attn_norm_g, attn_out_g, conv_out_g, ffn_norm_g
    gf = final_norm_g.reshape(1, D_MODEL)
    sinks = attn_sinks[0]

    lead = jnp.concatenate([jnp.zeros((LEAD, D_MODEL), F32), meta_full], axis=0)
    h0, q, kv, cacg = _in_proj(x[0], lead, g1, w_in_t, r, 768)
    oa, lse, *gathered = _attn_fwd(q, kv, sinks, later)
    oc, yc, g_gate, g_up, g_out, g_down = _conv_fwd(cacg, cw_full, conv_b, conv_ln_g, conv_ln_b, 384, gathered)
    wg_t, wu_t, wd_b = g_gate.reshape(D_FF, D_MODEL), g_up.reshape(D_FF, D_MODEL), g_down.reshape(D_FF, D_MODEL)
    w_out_b = g_out.reshape(D_MODEL, D_MODEL)
    h1, hn2 = _out_proj(oa, oc, h0, ga, gc, g2, w_out_b, 768)
    gate, up, act, dh2, dh2b, loss_p, dgf = _ffn_fwd(hn2, h1, loss_target[0], gf, wg_t, wu_t, wd_b, 384)

    def by_shard(dw):
        return dw.reshape(N_SHARD, dw.shape[0] // N_SHARD, D_MODEL)

    dgate, dup, dh1, dg2 = _ffn_bwd(dh2, dh2b, gate, up, h1, g2, wg_t, wu_t, wd_b, 384)
    p_gate, p_up = [by_shard(dw) for dw in _ffn_wgrad_gu(hn2, dgate, dup, 768)]
    p_down = by_shard(_ffn_wgrad_d(act, dh2b, 768))
    doa, doc, dwo, dga, dgc = _out_proj_bwd(dh1, oa, oc, ga, gc, w_out_b, 768)
    p_out = by_shard(dwo)
    dy, dcw, dcb, dlg, dlb, l_gate, l_up = _conv_bwd_params(doc, yc, cacg, conv_ln_g, conv_ln_b, 384, [p_gate, p_up])
    dc = _conv_bwd_data(dy, cacg, cw_full, 384)
    dq, dkv, dkv_meta, dsink, l_out, l_down = _attn_bwd(q, kv, oa, doa, lse, sinks, [p_out, p_down])
    grad_x, dlead, dwi_t, dg1 = _in_proj_bwd(dq, dkv, dkv_meta, dc, dh1, h0, g1, w_in_t, seq, tm_wide)
    in_send, in_recv, p_in_thru, l_in_thru, started = _scatter_start(by_shard(dwi_t))

    big = ("w_gate", "w_up", "w_out", "w_down", "w_in")
    transposed = ("w_in", "w_gate", "w_up")
    grads = {}
    params = {
        "meta_tokens": (meta_tokens, m_meta_tokens, v_meta_tokens), "attn_norm_g": (attn_norm_g, m_attn_norm_g, v_attn_norm_g),
        "w_in": (w_in, m_w_in, v_w_in), "attn_sinks": (attn_sinks, m_attn_sinks, v_attn_sinks), "conv_w": (conv_w, m_conv_w, v_conv_w),
        "conv_b": (conv_b, m_conv_b, v_conv_b), "conv_ln_g": (conv_ln_g, m_conv_ln_g, v_conv_ln_g),
        "conv_ln_b": (conv_ln_b, m_conv_ln_b, v_conv_ln_b), "attn_out_g": (attn_out_g, m_attn_out_g, v_attn_out_g),
        "conv_out_g": (conv_out_g, m_conv_out_g, v_conv_out_g), "w_out": (w_out, m_w_out, v_w_out),
        "ffn_norm_g": (ffn_norm_g, m_ffn_norm_g, v_ffn_norm_g), "w_gate": (w_gate, m_w_gate, v_w_gate), "w_up": (w_up, m_w_up, v_w_up),
        "w_down": (w_down, m_w_down, v_w_down), "final_norm_g": (final_norm_g, m_final_norm_g, v_final_norm_g)}
    names = list(params)
    delta, new_m, new_v = {}, {}, {}

    def finish(group, parts, landed, behind, tag):
        halves = _sum_pieces([_own_piece(p) for p in parts], landed, behind, "sum_pieces_" + tag)
        for name, mine, theirs in zip(group, halves, _swap_with_sibling(halves, "swap_with_sibling_" + tag)):
            flip = (lambda a: a.T) if name in transposed else (lambda a: a)
            g = _both_halves(mine, theirs)
            w, m, v = params[name]
            outs = _adamw(flip(w[0]), g, flip(m[0]), flip(v[0]), "adamw_" + name)
            grads[name], delta[name], new_m[name], new_v[name] = [flip(a)[None] for a in (g, *outs)]
        return outs[0]

    done = finish(big[:4], [p_gate, p_up, p_out, p_down], [l_gate, l_up, l_out, l_down], started, "ffn_out")

    red_names = ("final_norm_g", "attn_norm_g", "ffn_norm_g", "meta_tokens", "attn_out_g", "conv_out_g", "conv_b", "conv_ln_g",
                 "conv_ln_b", "conv_w", "loss", "attn_sinks")
    red = dict(zip(red_names, _allreduce_small([dgf, dg1, dg2, dlead[LEAD:BLOCK], dga, dgc, dcb, dlg, dlb, dcw, loss_p, dsink], done)))
    loss = red["loss"][0, 0]
    for name in ("attn_norm_g", "conv_b", "conv_ln_g", "conv_ln_b", "attn_out_g", "conv_out_g", "ffn_norm_g"):
        grads[name] = red[name]
    grads["final_norm_g"] = red["final_norm_g"].reshape(D_MODEL)
    grads["attn_sinks"] = red["attn_sinks"][:, 0].reshape(1, N_HEADS)
    grads["meta_tokens"] = lax.dynamic_slice_in_dim(red["meta_tokens"], shard * (D_MODEL // N_SHARD), D_MODEL // N_SHARD, axis=1)
    grads["conv_w"] = lax.dynamic_slice_in_dim(
        red["conv_w"][0:CONV_K], shard * (CONV_W // N_SHARD), CONV_W // N_SHARD, axis=1)[None]

    p_in, l_in = _scatter_wait(in_send, in_recv, p_in_thru, l_in_thru, red["loss"])
    finish(big[4:], [p_in], [l_in], l_in, "in")
    rest = [name for name in names if name not in big]

    def rows_of(a):
        return a.reshape(-1, a.shape[-1])

    small = _adamw_small([rows_of(params[n][0]) for n in rest], [rows_of(grads[n]) for n in rest],
                         [rows_of(params[n][1]) for n in rest], [rows_of(params[n][2]) for n in rest])
    for dst, outs in zip((delta, new_m, new_v), small):
        for name, out in zip(rest, outs):
            dst[name] = out.reshape(params[name][0].shape)

    return (loss, grad_x[None], *[grads[n] for n in names], *[delta[n] for n in names], *[new_m[n] for n in names],
            *[new_v[n] for n in names])
```

```python
import functools
import math

import jax
import jax.numpy as jnp
from jax import lax
from jax.experimental import pallas as pl
from jax.experimental.pallas import tpu as pltpu

F32 = jnp.float32
BF16 = jnp.bfloat16

D_MODEL = 1024
N_META = 16
ATTN_W = 512
CONV_W = 512
HEAD_DIM = 64
N_HEADS = 8
N_KV = 2
GROUP = N_HEADS // N_KV
KV_W = N_KV * HEAD_DIM
BLOCK = 128
LEAD = BLOCK - N_META
CONV_K = 31
D_FF = 2816
IN_COLS = ATTN_W + 2 * KV_W + 2 * CONV_W
Q0, KV0, C0 = 0, ATTN_W, ATTN_W + 2 * KV_W
NORM_EPS = 1e-5
SCALE = 1.0 / math.sqrt(HEAD_DIM)
SLOPES = tuple(2.0 ** (-(8.0 / N_HEADS) * (h + 1)) for h in range(N_HEADS))
NEG = -1e30

ADAM_LR, ADAM_B1, ADAM_B2, ADAM_EPS, ADAM_WD, ADAM_STEP = 0.001, 0.9, 0.999, 1e-08, 0.01, 10

N_SHARD = 4
N_DEV = 8
ROW_QUANTUM = 768
HALO = 32
CONV_CHUNK = 32
FF_CHUNK = 256
FF_CHUNKS = tuple(slice(c, c + FF_CHUNK) for c in range(0, D_FF, FF_CHUNK))
VMEM_LIMIT = 60 * 1024 * 1024


def _cparams(n_axes=1):
    return pltpu.CompilerParams(dimension_semantics=("arbitrary",) * n_axes, vmem_limit_bytes=VMEM_LIMIT)


def _dot(a, b):
    return jnp.dot(a, b, preferred_element_type=F32)


def _dot_nt(a, b):
    return lax.dot_general(a, b, (((1,), (1,)), ((), ())), preferred_element_type=F32)


def _dot_tn(a, b):
    return lax.dot_general(a, b, (((0,), (0,)), ((), ())), preferred_element_type=F32)


def _sigmoid(x):
    return 1.0 / (1.0 + jnp.exp(-x))


def _row(tm, n):
    return pl.BlockSpec((tm, n), lambda i: (i, 0))


def _const(shape):
    return pl.BlockSpec(shape, lambda i: (0,) * len(shape))


def _resident(shape):
    return pl.BlockSpec(shape, lambda i: (0,) * len(shape), pipeline_mode=pl.Buffered(1))


def _rms_fwd(x, g):
    rstd = lax.rsqrt(jnp.mean(x * x, axis=-1, keepdims=True) + NORM_EPS)
    xhat = x * rstd
    return xhat * g, xhat, rstd


def _rms_bwd(dy, xhat, rstd, g):
    dxh = dy * g
    dx = rstd * (dxh - xhat * jnp.mean(dxh * xhat, axis=-1, keepdims=True))
    return dx, dy * xhat


def _in_proj(x, lead, g1, w_in_t, r, tm):
    seq = x.shape[0]
    n_sub = tm // BLOCK

    def body(*refs):
        x_refs = refs[:n_sub]
        lead_ref, g_ref, w_ref, h0_ref, q_ref, kv_ref, c_ref = refs[n_sub:]
        i = pl.program_id(0)
        pieces = []
        for k, x_ref in enumerate(x_refs):
            at = jnp.zeros((BLOCK, D_MODEL), jnp.int32) + (i * tm + (k - 1) * BLOCK)
            piece = jnp.where((at >= 0) & (at < seq), x_ref[...], 0.0)
            pieces.append(jnp.where(at < 0, lead_ref[...], piece) if k == 0 else piece)
        h = jnp.concatenate(pieces, axis=0)
        h0_ref[...] = h
        hn = _rms_fwd(h, g_ref[...])[0].astype(BF16)
        q_ref[...] = _dot_nt(hn, w_ref[Q0:KV0, :]).astype(BF16)
        kv_ref[...] = _dot_nt(hn, w_ref[KV0:C0, :]).astype(BF16)
        c_ref[...] = _dot_nt(hn, w_ref[C0:IN_COLS, :])

    def x_block(k):
        return pl.BlockSpec((BLOCK, D_MODEL), lambda i: (jnp.clip(n_sub * i - 1 + k, 0, seq // BLOCK - 1), 0))

    return pl.pallas_call(
        body, name="in_proj", grid=(r // tm,),
        in_specs=[x_block(k) for k in range(n_sub)] + [_const((BLOCK, D_MODEL)), _const((1, D_MODEL)), _const((IN_COLS, D_MODEL))],
        out_specs=[_row(tm, D_MODEL), _row(tm, ATTN_W), _row(tm, 2 * KV_W), _row(tm, 2 * CONV_W)],
        out_shape=[jax.ShapeDtypeStruct((r, D_MODEL), F32), jax.ShapeDtypeStruct((r, ATTN_W), BF16),
                   jax.ShapeDtypeStruct((r, 2 * KV_W), BF16), jax.ShapeDtypeStruct((r, 2 * CONV_W), F32)],
        compiler_params=_cparams(),
    )(*[x] * n_sub, lead, g1, w_in_t)


def _attn_bias_init(bias_ref, late_ref):
    row = lax.broadcasted_iota(jnp.int32, (GROUP * BLOCK, BLOCK), 0) & (BLOCK - 1)
    col = lax.broadcasted_iota(jnp.int32, (GROUP * BLOCK, BLOCK), 1)
    late_ref[...] = jnp.where(col > row, 1.0, 0.0)
    for g in range(N_KV):
        slope = jnp.concatenate([jnp.zeros((BLOCK, BLOCK), F32) + SLOPES[g * GROUP + j] for j in range(GROUP)], axis=0)
        bias_ref[g, :, 0:BLOCK] = jnp.where(col >= LEAD, 0.0, NEG)
        bias_ref[g, :, BLOCK:2 * BLOCK] = -slope * jnp.where(col > row, row - col + BLOCK, row - col).astype(F32)


def _attn_block_bias(late_ref, i):
    late = late_ref[...]
    meta0 = jnp.where(i == 0, NEG, 0.0)
    no_prev = jnp.where(i >= 2, 0.0, NEG)
    no_cur = jnp.where(i >= 1, 0.0, NEG)
    return late * meta0, late * no_prev + no_cur


def _attn_logits(s3, bias_ref, block_bias, prev_part, g):
    meta = s3[:, 0:BLOCK] * SCALE + (bias_ref[g, :, 0:BLOCK] + block_bias[0])
    band = jnp.where(prev_part, s3[:, BLOCK:2 * BLOCK], s3[:, 2 * BLOCK:3 * BLOCK]) * SCALE + (bias_ref[g, :, BLOCK:2 * BLOCK] + block_bias[1])
    return meta, band


def _split_band(meta, band, prev_part):
    return jnp.concatenate([meta, jnp.where(prev_part, band, 0.0), jnp.where(prev_part, 0.0, band)], axis=1)


def _head_rows(vals):
    return jnp.concatenate([jnp.zeros((BLOCK, BLOCK), F32) + v for v in vals], axis=0)


def _stack_heads(ref, g):
    return jnp.concatenate([ref[:, (g * GROUP + j) * HEAD_DIM:(g * GROUP + j + 1) * HEAD_DIM] for j in range(GROUP)], axis=0)


def _kv_cat(kvm_ref, kvp_ref, kvc_ref, g):
    ks = slice(g * HEAD_DIM, (g + 1) * HEAD_DIM)
    vs = slice(KV_W + g * HEAD_DIM, KV_W + (g + 1) * HEAD_DIM)
    kcat = jnp.concatenate([kvm_ref[:, ks], kvp_ref[:, ks], kvc_ref[:, ks]], axis=0)
    vcat = jnp.concatenate([kvm_ref[:, vs], kvp_ref[:, vs], kvc_ref[:, vs]], axis=0)
    return kcat, vcat


def _attn_fwd(q, kv, sinks, gathered):
    r = q.shape[0]
    nb = r // BLOCK
    n = len(gathered)

    def body(sink_ref, q_ref, kvc_ref, kvp_ref, kvm_ref, *rest):
        o_ref, lse_ref = rest[n:n + 2]
        dst = rest[n + 2:2 * n + 2]
        bias_ref, late_ref, send_sems, recv_sems = rest[2 * n + 2:]
        i = pl.program_id(0)

        @pl.when(i == 0)
        def _():
            for cp in _gather_ici(dst, send_sems, recv_sems)[0]:
                cp.start()
            _attn_bias_init(bias_ref, late_ref)

        lane = lax.broadcasted_iota(jnp.int32, (BLOCK, BLOCK), 1)
        lse_tile = jnp.zeros((BLOCK, BLOCK), F32)
        block_bias = _attn_block_bias(late_ref, i)
        prev_part = late_ref[...] > 0.5
        for g in range(N_KV):
            kcat, vcat = _kv_cat(kvm_ref, kvp_ref, kvc_ref, g)
            heads = range(g * GROUP, (g + 1) * GROUP)
            meta, band = _attn_logits(_dot_nt(_stack_heads(q_ref, g), kcat), bias_ref, block_bias, prev_part, g)
            sink = _head_rows([sink_ref[h] for h in heads])
            m = jnp.maximum(jnp.max(jnp.maximum(meta, band), axis=-1, keepdims=True), sink)
            p_meta, p_band = jnp.exp(meta - m), jnp.exp(band - m)
            l = jnp.sum(p_meta + p_band, axis=-1, keepdims=True) + jnp.exp(sink - m)
            o = _dot(_split_band(p_meta, p_band, prev_part).astype(BF16), vcat) * (1.0 / l)[:, 0:HEAD_DIM]
            lse = m + jnp.log(l)
            for j, h in enumerate(heads):
                o_ref[:, h * HEAD_DIM:(h + 1) * HEAD_DIM] = o[j * BLOCK:(j + 1) * BLOCK]
                lse_tile = jnp.where(lane == h, lse[j * BLOCK:(j + 1) * BLOCK], lse_tile)
        lse_ref[...] = lse_tile

        @pl.when(i == nb - 1)
        def _():
            sends, arrivals = _gather_ici(dst, send_sems, recv_sems)
            for cp in arrivals:
                cp.wait_recv()
            for cp in sends:
                cp.wait_send()

    return pl.pallas_call(
        body, name="attn_fwd", grid=(nb,),
        in_specs=[pl.BlockSpec(memory_space=pltpu.SMEM), _row(BLOCK, ATTN_W), _row(BLOCK, 2 * KV_W),
                  pl.BlockSpec((BLOCK, 2 * KV_W), lambda i: (jnp.maximum(i - 1, 0), 0)), _const((BLOCK, 2 * KV_W))] + [ANY] * n,
        out_specs=[_row(BLOCK, ATTN_W), _row(BLOCK, BLOCK)] + [ANY] * n,
        out_shape=[jax.ShapeDtypeStruct((r, ATTN_W), F32), jax.ShapeDtypeStruct((r, BLOCK), F32)]
        + [jax.ShapeDtypeStruct(g.shape, g.dtype) for g in gathered],
        input_output_aliases={5 + k: 2 + k for k in range(n)},
        scratch_shapes=[pltpu.VMEM((N_KV, GROUP * BLOCK, 2 * BLOCK), F32), pltpu.VMEM((GROUP * BLOCK, BLOCK), F32),
                        pltpu.SemaphoreType.DMA((3 * n,)), pltpu.SemaphoreType.DMA((3 * n,))],
        compiler_params=_cparams(),
    )(sinks, q, kv, kv, kv, *gathered)


def _shifted_copies(ub_ref, win):
    w = win.shape[0]
    ub_ref[0] = win
    for b in range(1, 8):
        ub_ref[b] = pltpu.roll(win, shift=w - b, axis=0)


def _conv_chunk(ub_ref, w_ref, r0, shifts):
    acc = jnp.zeros((CONV_CHUNK, CONV_W), F32)
    for j in range(CONV_K):
        a, b = divmod(shifts[j], 8)
        acc = acc + w_ref[j:j + 1, :] * ub_ref[b, pl.ds(r0 + 8 * a, CONV_CHUNK), :]
    return acc


FWD_SHIFTS = tuple(HALO - (CONV_K - 1) + j for j in range(CONV_K))
BWD_SHIFTS = tuple(CONV_K - 1 - j for j in range(CONV_K))


def _glu_window(cp_ref, c_ref, i):
    tile = c_ref[:, 0:CONV_W] * _sigmoid(c_ref[:, CONV_W:2 * CONV_W])
    halo = cp_ref[:, 0:CONV_W] * _sigmoid(cp_ref[:, CONV_W:2 * CONV_W])
    first = (jnp.zeros((HALO, CONV_W), jnp.int32) + i) == 0
    return jnp.concatenate([jnp.where(first, 0.0, halo), tile], axis=0)


def _halo_before(tm, n):
    return pl.BlockSpec((HALO, n), lambda i: (jnp.maximum(i * (tm // HALO) - 1, 0), 0))


def _conv_fwd(cacg, cw, cb, lg, lb, tm, gathered):
    r = cacg.shape[0]
    n = len(gathered)

    def body(c_ref, cp_ref, w_ref, cb_ref, lg_ref, lb_ref, *rest):
        o_ref, y_ref = rest[n:n + 2]
        dst = rest[n + 2:2 * n + 2]
        ub_ref, send_sems, recv_sems = rest[2 * n + 2:]
        i = pl.program_id(0)

        @pl.when(i == 0)
        def _():
            for cp in _gather_d2d(dst, send_sems, recv_sems)[0]:
                cp.start()

        _shifted_copies(ub_ref, _glu_window(cp_ref, c_ref, i))

        def chunk(ci, carry):
            r0 = pl.multiple_of(ci * CONV_CHUNK, CONV_CHUNK)
            y = _conv_chunk(ub_ref, w_ref, r0, FWD_SHIFTS) + cb_ref[...]
            yc = y - jnp.mean(y, axis=-1, keepdims=True)
            rs = lax.rsqrt(jnp.mean(yc * yc, axis=-1, keepdims=True) + NORM_EPS)
            yn = yc * rs * lg_ref[...] + lb_ref[...]
            o_ref[pl.ds(r0, CONV_CHUNK), :] = yn * _sigmoid(yn)
            y_ref[pl.ds(r0, CONV_CHUNK), :] = y
            return carry

        lax.fori_loop(0, tm // CONV_CHUNK, chunk, 0, unroll=4)

        @pl.when(i == r // tm - 1)
        def _():
            sends, arrivals = _gather_d2d(dst, send_sems, recv_sems)
            for cp in arrivals:
                cp.wait_recv()
            for cp in sends:
                cp.wait_send()

    return pl.pallas_call(
        body, name="conv_fwd", grid=(r // tm,),
        in_specs=[_row(tm, 2 * CONV_W), _halo_before(tm, 2 * CONV_W), _const((32, CONV_W)), _const((1, CONV_W)),
                  _const((1, CONV_W)), _const((1, CONV_W))] + [ANY] * n,
        out_specs=[_row(tm, CONV_W), _row(tm, CONV_W)] + [ANY] * n,
        out_shape=[jax.ShapeDtypeStruct((r, CONV_W), F32)] * 2 + [jax.ShapeDtypeStruct(g.shape, g.dtype) for g in gathered],
        input_output_aliases={6 + k: 2 + k for k in range(n)},
        scratch_shapes=[pltpu.VMEM((8, tm + HALO, CONV_W), F32), pltpu.SemaphoreType.DMA((3 * n,)), pltpu.SemaphoreType.DMA((3 * n,))],
        compiler_params=_cparams(),
    )(cacg, cacg, cw, cb, lg, lb, *gathered)


def _out_proj(oa, oc, h0, ga, gc, g2, w_out_b, tm):
    r = h0.shape[0]

    def body(oa_ref, oc_ref, h_ref, ga_ref, gc_ref, g2_ref, w_ref, h1_ref, hn2_ref):
        ma = _rms_fwd(oa_ref[...], ga_ref[...])[0].astype(BF16)
        mc = _rms_fwd(oc_ref[...], gc_ref[...])[0].astype(BF16)
        h1 = h_ref[...] + _dot(jnp.concatenate([ma, mc], axis=1), w_ref[...])
        h1_ref[...] = h1
        hn2_ref[...] = _rms_fwd(h1, g2_ref[...])[0].astype(BF16)

    return pl.pallas_call(
        body, name="out_proj", grid=(r // tm,),
        in_specs=[_row(tm, ATTN_W), _row(tm, CONV_W), _row(tm, D_MODEL), _const((1, ATTN_W)), _const((1, CONV_W)),
                  _const((1, D_MODEL)), _const((D_MODEL, D_MODEL))],
        out_specs=[_row(tm, D_MODEL), _row(tm, D_MODEL)],
        out_shape=[jax.ShapeDtypeStruct((r, D_MODEL), F32), jax.ShapeDtypeStruct((r, D_MODEL), BF16)],
        compiler_params=_cparams(),
    )(oa, oc, h0, ga, gc, g2, w_out_b)


def _ffn_fwd(hn2, h1, target, gf, wg_t, wu_t, wd_b, tm):
    r = h1.shape[0]
    seq = target.shape[0]
    n_sub = tm // BLOCK

    def body(hn_ref, h1_ref, *rest):
        t_refs = rest[:n_sub]
        gf_ref, wg_ref, wu_ref, wd_ref, gate_ref, up_ref, act_ref, dh2_ref, dh2b_ref, loss_ref, dgf_ref = rest[n_sub:]
        i = pl.program_id(0)

        @pl.when(i == 0)
        def _():
            loss_ref[...] = jnp.zeros_like(loss_ref)
            dgf_ref[...] = jnp.zeros_like(dgf_ref)

        hn = hn_ref[...]
        for cs in FF_CHUNKS:
            gate = _dot_nt(hn, wg_ref[cs, :])
            up = _dot_nt(hn, wu_ref[cs, :])
            gate_ref[:, cs] = gate.astype(BF16)
            up_ref[:, cs] = up.astype(BF16)
            act_ref[:, cs] = (gate * _sigmoid(gate) * up).astype(BF16)
        y, xhat, rstd = _rms_fwd(h1_ref[...] + _dot(act_ref[...], wd_ref[...]), gf_ref[...])
        rows = lax.broadcasted_iota(jnp.int32, (tm, D_MODEL), 0) + i * tm
        real = (rows >= BLOCK) & (rows < BLOCK + seq)
        err = jnp.where(real, y - jnp.concatenate([t[...] for t in t_refs], axis=0), 0.0)
        loss_ref[...] += jnp.sum(err * err) * (0.5 / D_MODEL)
        dy = err * (1.0 / D_MODEL)
        dh2, dg_rows = _rms_bwd(dy, xhat, rstd, gf_ref[...])
        dgf_ref[...] += jnp.sum(dg_rows, axis=0, keepdims=True)
        dh2_ref[...] = dh2
        dh2b_ref[...] = dh2.astype(BF16)

    def target_block(k):
        return pl.BlockSpec((BLOCK, D_MODEL), lambda i: (jnp.clip(n_sub * i - 1 + k, 0, seq // BLOCK - 1), 0))

    return pl.pallas_call(
        body, name="ffn_fwd", grid=(r // tm,),
        in_specs=[_row(tm, D_MODEL), _row(tm, D_MODEL)] + [target_block(k) for k in range(n_sub)]
        + [_const((1, D_MODEL))] + [_resident((D_FF, D_MODEL))] * 3,
        out_specs=[_row(tm, D_FF)] * 3 + [_row(tm, D_MODEL), _row(tm, D_MODEL), _const((1, BLOCK)), _const((1, D_MODEL))],
        out_shape=[jax.ShapeDtypeStruct((r, D_FF), BF16)] * 3
        + [jax.ShapeDtypeStruct((r, D_MODEL), F32), jax.ShapeDtypeStruct((r, D_MODEL), BF16),
           jax.ShapeDtypeStruct((1, BLOCK), F32), jax.ShapeDtypeStruct((1, D_MODEL), F32)],
        compiler_params=_cparams(),
    )(hn2, h1, *[target] * n_sub, gf, wg_t, wu_t, wd_b)


def _ffn_bwd(dh2, dh2b, gate, up, h1, g2, wg_t, wu_t, wd_b, tm):
    r = h1.shape[0]

    def body(dh2_ref, dh2b_ref, gate_ref, up_ref, h1_ref, g2_ref, wg_ref, wu_ref, wd_ref, dgate_ref, dup_ref, dh1_ref, dg2_ref):
        @pl.when(pl.program_id(0) == 0)
        def _():
            dg2_ref[...] = jnp.zeros_like(dg2_ref)

        dyb = dh2b_ref[...]
        for cs in FF_CHUNKS:
            dact = _dot_nt(dyb, wd_ref[cs, :])
            gate = gate_ref[:, cs].astype(F32)
            up = up_ref[:, cs].astype(F32)
            sg = _sigmoid(gate)
            dgate_ref[:, cs] = (dact * up * (sg * (1.0 + gate * (1.0 - sg)))).astype(BF16)
            dup_ref[:, cs] = (dact * (gate * sg)).astype(BF16)
        dhn = _dot(dgate_ref[...], wg_ref[...]) + _dot(dup_ref[...], wu_ref[...])
        _, xhat, rstd = _rms_fwd(h1_ref[...], g2_ref[...])
        dx, dg_rows = _rms_bwd(dhn, xhat, rstd, g2_ref[...])
        dg2_ref[...] += jnp.sum(dg_rows, axis=0, keepdims=True)
        dh1_ref[...] = dh2_ref[...] + dx

    return pl.pallas_call(
        body, name="ffn_bwd", grid=(r // tm,),
        in_specs=[_row(tm, D_MODEL), _row(tm, D_MODEL), _row(tm, D_FF), _row(tm, D_FF), _row(tm, D_MODEL), _const((1, D_MODEL))]
        + [_resident((D_FF, D_MODEL))] * 3,
        out_specs=[_row(tm, D_FF), _row(tm, D_FF), _row(tm, D_MODEL), _const((1, D_MODEL))],
        out_shape=[jax.ShapeDtypeStruct((r, D_FF), BF16), jax.ShapeDtypeStruct((r, D_FF), BF16),
                   jax.ShapeDtypeStruct((r, D_MODEL), F32), jax.ShapeDtypeStruct((1, D_MODEL), F32)],
        compiler_params=_cparams(),
    )(dh2, dh2b, gate, up, h1, g2, wg_t, wu_t, wd_b)


FF_HALF = D_FF // 2


def _ffn_wgrad_gu(hn2, dgate, dup, tk):
    r = hn2.shape[0]
    n_k = r // tk

    def body(hn_ref, dg_ref, du_ref, wg_ref, wu_ref, accg_ref, accu_ref):
        k = pl.program_id(1)

        @pl.when(k == 0)
        def _():
            accg_ref[...] = jnp.zeros_like(accg_ref)
            accu_ref[...] = jnp.zeros_like(accu_ref)

        hn = hn_ref[...]
        accg_ref[...] += _dot_tn(dg_ref[...], hn)
        accu_ref[...] += _dot_tn(du_ref[...], hn)

        @pl.when(k == n_k - 1)
        def _():
            wg_ref[...] = accg_ref[...].astype(BF16)
            wu_ref[...] = accu_ref[...].astype(BF16)

    col = pl.BlockSpec((tk, FF_HALF), lambda j, k: (k, j))
    out = pl.BlockSpec((FF_HALF, D_MODEL), lambda j, k: (j, 0))
    return pl.pallas_call(
        body, name="ffn_wgrad_gu", grid=(2, n_k),
        in_specs=[pl.BlockSpec((tk, D_MODEL), lambda j, k: (k, 0)), col, col],
        out_specs=[out, out],
        out_shape=[jax.ShapeDtypeStruct((D_FF, D_MODEL), BF16)] * 2,
        scratch_shapes=[pltpu.VMEM((FF_HALF, D_MODEL), F32)] * 2,
        compiler_params=_cparams(2),
    )(hn2, dgate, dup)


def _ffn_wgrad_d(act, dh2b, tk):
    r = act.shape[0]
    n_k = r // tk

    def body(a_ref, dy_ref, wd_ref, acc_ref):
        k = pl.program_id(1)

        @pl.when(k == 0)
        def _():
            acc_ref[...] = jnp.zeros_like(acc_ref)

        acc_ref[...] += _dot_tn(a_ref[...], dy_ref[...])

        @pl.when(k == n_k - 1)
        def _():
            wd_ref[...] = acc_ref[...].astype(BF16)

    return pl.pallas_call(
        body, name="ffn_wgrad_d", grid=(2, n_k),
        in_specs=[pl.BlockSpec((tk, FF_HALF), lambda j, k: (k, j)), pl.BlockSpec((tk, D_MODEL), lambda j, k: (k, 0))],
        out_specs=pl.BlockSpec((FF_HALF, D_MODEL), lambda j, k: (j, 0)),
        out_shape=jax.ShapeDtypeStruct((D_FF, D_MODEL), BF16),
        scratch_shapes=[pltpu.VMEM((FF_HALF, D_MODEL), F32)],
        compiler_params=_cparams(2),
    )(act, dh2b)


def _out_proj_bwd(dh1, oa, oc, ga, gc, w_out_b, tm):
    r = dh1.shape[0]

    def body(dh_ref, oa_ref, oc_ref, ga_ref, gc_ref, w_ref, doa_ref, doc_ref, dw_ref, dga_ref, dgc_ref, acc_ref):
        i = pl.program_id(0)

        @pl.when(i == 0)
        def _():
            acc_ref[...] = jnp.zeros_like(acc_ref)
            dga_ref[...] = jnp.zeros_like(dga_ref)
            dgc_ref[...] = jnp.zeros_like(dgc_ref)

        dhb = dh_ref[...].astype(BF16)
        dmix = _dot_nt(dhb, w_ref[...])
        ma, xa, ra = _rms_fwd(oa_ref[...], ga_ref[...])
        mc, xc, rc = _rms_fwd(oc_ref[...], gc_ref[...])
        acc_ref[...] += _dot_tn(jnp.concatenate([ma.astype(BF16), mc.astype(BF16)], axis=1), dhb)

        @pl.when(i == r // tm - 1)
        def _():
            dw_ref[...] = acc_ref[...].astype(BF16)

        doa, dga_rows = _rms_bwd(dmix[:, 0:ATTN_W], xa, ra, ga_ref[...])
        doc, dgc_rows = _rms_bwd(dmix[:, ATTN_W:ATTN_W + CONV_W], xc, rc, gc_ref[...])
        doa_ref[...] = doa
        doc_ref[...] = doc
        dga_ref[...] += jnp.sum(dga_rows, axis=0, keepdims=True)
        dgc_ref[...] += jnp.sum(dgc_rows, axis=0, keepdims=True)

    return pl.pallas_call(
        body, name="out_proj_bwd", grid=(r // tm,),
        in_specs=[_row(tm, D_MODEL), _row(tm, ATTN_W), _row(tm, CONV_W), _const((1, ATTN_W)), _const((1, CONV_W)),
                  _const((D_MODEL, D_MODEL))],
        out_specs=[_row(tm, ATTN_W), _row(tm, CONV_W), _const((D_MODEL, D_MODEL)), _const((1, ATTN_W)), _const((1, CONV_W))],
        out_shape=[jax.ShapeDtypeStruct((r, ATTN_W), F32), jax.ShapeDtypeStruct((r, CONV_W), F32),
                   jax.ShapeDtypeStruct((D_MODEL, D_MODEL), BF16), jax.ShapeDtypeStruct((1, ATTN_W), F32),
                   jax.ShapeDtypeStruct((1, CONV_W), F32)],
        scratch_shapes=[pltpu.VMEM((D_MODEL, D_MODEL), F32)],
        compiler_params=_cparams(),
    )(dh1, oa, oc, ga, gc, w_out_b)


def _conv_bwd_params(doc, y, cacg, lg, lb, tm, parts):
    r = cacg.shape[0]
    n_steps = r // tm
    n = len(parts)

    def body(do_ref, y_ref, c_ref, cp_ref, lg_ref, lb_ref, *rest):
        src = rest[:n]
        dy_ref, dcw_ref, dcb_ref, dlg_ref, dlb_ref = rest[n:n + 5]
        dst = rest[n + 5:2 * n + 5]
        ub_ref, accw_ref, send_sems, recv_sems = rest[2 * n + 5:]
        i = pl.program_id(0)

        @pl.when(i == 0)
        def _():
            for cp in _scatter(src, dst, send_sems, recv_sems):
                cp.start()
            accw_ref[...] = jnp.zeros_like(accw_ref)
            dcb_ref[...] = jnp.zeros_like(dcb_ref)
            dlg_ref[...] = jnp.zeros_like(dlg_ref)
            dlb_ref[...] = jnp.zeros_like(dlb_ref)

        _shifted_copies(ub_ref, _glu_window(cp_ref, c_ref, i))

        def chunk(ci, carry):
            r0 = pl.multiple_of(ci * CONV_CHUNK, CONV_CHUNK)
            y = y_ref[pl.ds(r0, CONV_CHUNK), :]
            yc = y - jnp.mean(y, axis=-1, keepdims=True)
            rs = lax.rsqrt(jnp.mean(yc * yc, axis=-1, keepdims=True) + NORM_EPS)
            xhat = yc * rs
            yn = xhat * lg_ref[...] + lb_ref[...]
            sg = _sigmoid(yn)
            dyn = do_ref[pl.ds(r0, CONV_CHUNK), :] * (sg * (1.0 + yn * (1.0 - sg)))
            dlg_ref[...] += jnp.sum(dyn * xhat, axis=0, keepdims=True)
            dlb_ref[...] += jnp.sum(dyn, axis=0, keepdims=True)
            dxh = dyn * lg_ref[...]
            dy = rs * (dxh - jnp.mean(dxh, axis=-1, keepdims=True) - xhat * jnp.mean(dxh * xhat, axis=-1, keepdims=True))
            dcb_ref[...] += jnp.sum(dy, axis=0, keepdims=True)
            dy_ref[pl.ds(r0, CONV_CHUNK), :] = dy
            for j in range(CONV_K):
                a, b = divmod(FWD_SHIFTS[j], 8)
                prod = dy * ub_ref[b, pl.ds(r0 + 8 * a, CONV_CHUNK), :]
                accw_ref[j] += jnp.sum(prod.reshape(CONV_CHUNK // 8, 8, CONV_W), axis=0)
            return carry

        lax.fori_loop(0, tm // CONV_CHUNK, chunk, 0, unroll=4)

        @pl.when(i == n_steps - 1)
        def _():
            for j in range(32):
                dcw_ref[j:j + 1, :] = jnp.sum(accw_ref[j], axis=0, keepdims=True)
            for cp in _scatter(src, dst, send_sems, recv_sems):
                cp.wait()

    vec = _const((1, CONV_W))
    return pl.pallas_call(
        body, name="conv_bwd_params", grid=(n_steps,),
        in_specs=[_row(tm, CONV_W), _row(tm, CONV_W), _row(tm, 2 * CONV_W), _halo_before(tm, 2 * CONV_W), vec, vec] + [ANY] * n,
        out_specs=[_row(tm, CONV_W), _const((32, CONV_W)), vec, vec, vec] + [ANY] * n,
        out_shape=[jax.ShapeDtypeStruct((r, CONV_W), F32), jax.ShapeDtypeStruct((32, CONV_W), F32)]
        + [jax.ShapeDtypeStruct((1, CONV_W), F32)] * 3 + _scatter_landing(parts),
        scratch_shapes=[pltpu.VMEM((8, tm + HALO, CONV_W), F32), pltpu.VMEM((32, 8, CONV_W), F32),
                        pltpu.SemaphoreType.DMA((7 * n,)), pltpu.SemaphoreType.DMA((7 * n,))],
        compiler_params=_cparams(),
    )(doc, y, cacg, cacg, lg, lb, *parts)


def _conv_bwd_data(dy, cacg, cw, tm):
    r = cacg.shape[0]
    n_steps = r // tm

    def body(dy_ref, dyn_ref, c_ref, w_ref, dc_ref, ub_ref):
        last = (jnp.zeros((HALO, CONV_W), jnp.int32) + pl.program_id(0)) == n_steps - 1
        win = jnp.concatenate([dy_ref[...], jnp.where(last, 0.0, dyn_ref[...])], axis=0)
        _shifted_copies(ub_ref, win)

        def chunk(ci, carry):
            r0 = pl.multiple_of(ci * CONV_CHUNK, CONV_CHUNK)
            du = _conv_chunk(ub_ref, w_ref, r0, BWD_SHIFTS)
            ca = c_ref[pl.ds(r0, CONV_CHUNK), 0:CONV_W]
            sg = _sigmoid(c_ref[pl.ds(r0, CONV_CHUNK), CONV_W:2 * CONV_W])
            dc_ref[pl.ds(r0, CONV_CHUNK), 0:CONV_W] = (du * sg).astype(BF16)
            dc_ref[pl.ds(r0, CONV_CHUNK), CONV_W:2 * CONV_W] = (du * ca * sg * (1.0 - sg)).astype(BF16)
            return carry

        lax.fori_loop(0, tm // CONV_CHUNK, chunk, 0, unroll=4)

    halo_after = pl.BlockSpec((HALO, CONV_W), lambda i: (jnp.minimum((i + 1) * (tm // HALO), r // HALO - 1), 0))
    return pl.pallas_call(
        body, name="conv_bwd_data", grid=(n_steps,),
        in_specs=[_row(tm, CONV_W), halo_after, _row(tm, 2 * CONV_W), _const((32, CONV_W))],
        out_specs=_row(tm, 2 * CONV_W),
        out_shape=jax.ShapeDtypeStruct((r, 2 * CONV_W), BF16),
        scratch_shapes=[pltpu.VMEM((8, tm + HALO, CONV_W), F32)],
        compiler_params=_cparams(),
    )(dy, dy, cacg, cw)


def _attn_bwd(q, kv, o, do, lse, sinks, parts):
    r = q.shape[0]
    nb = r // BLOCK
    n = len(parts)

    def body(sink_ref, q_ref, kvc_ref, kvp_ref, kvm_ref, o_ref, do_ref, lse_ref, *rest):
        src = rest[:n]
        dq_ref, dkv_ref, dmeta_ref, dsink_ref = rest[n:n + 4]
        dst = rest[n + 4:2 * n + 4]
        hold_ref, bias_ref, late_ref, send_sems, recv_sems = rest[2 * n + 4:]
        i = pl.program_id(0)

        @pl.when(i == 0)
        def _():
            for cp in _scatter(src, dst, send_sems, recv_sems):
                cp.start()
            _attn_bias_init(bias_ref, late_ref)
            dmeta_ref[...] = jnp.zeros_like(dmeta_ref)
            dsink_ref[...] = jnp.zeros_like(dsink_ref)
            hold_ref[...] = jnp.zeros_like(hold_ref)

        @pl.when(i < nb)
        def _():
            lane = lax.broadcasted_iota(jnp.int32, (BLOCK, BLOCK), 1)
            lse_tile = lse_ref[...]
            zero = jnp.zeros((BLOCK, BLOCK), F32)
            block_bias = _attn_block_bias(late_ref, i)
            prev_part = late_ref[...] > 0.5
            for g in range(N_KV):
                kcat, vcat = _kv_cat(kvm_ref, kvp_ref, kvc_ref, g)
                heads = range(g * GROUP, (g + 1) * GROUP)
                qs = _stack_heads(q_ref, g)
                dos = _stack_heads(do_ref, g)
                dosb = dos.astype(BF16)
                lse = jnp.concatenate(
                    [jnp.sum(jnp.where(lane == h, lse_tile, 0.0), axis=-1, keepdims=True) + zero for h in heads], axis=0)
                delta = jnp.sum(dos * _stack_heads(o_ref, g), axis=-1, keepdims=True) + jnp.zeros((GROUP * BLOCK, BLOCK), F32)
                band_bias = bias_ref[g, :, BLOCK:2 * BLOCK] + block_bias[1]
                bias = [bias_ref[g, :, 0:BLOCK] + block_bias[0], jnp.where(prev_part, band_bias, NEG), jnp.where(prev_part, NEG, band_bias)]
                s = _dot_nt(qs, kcat)
                dp = _dot_nt(dosb, vcat)
                ps = [jnp.exp(s[:, k * BLOCK:(k + 1) * BLOCK] * SCALE + bias[k] - lse) for k in range(3)]
                p = jnp.concatenate(ps, axis=1)
                ds = jnp.concatenate(
                    [(ps[k] * (dp[:, k * BLOCK:(k + 1) * BLOCK] - delta)) * SCALE for k in range(3)], axis=1).astype(BF16)
                sink_term = jnp.exp(_head_rows([sink_ref[h] for h in heads]) - lse)[:, 0:1] * delta[:, 0:1]
                dq = _dot(ds, kcat).astype(BF16)
                for j, h in enumerate(heads):
                    dsink_ref[h:h + 1, :] += -jnp.sum(sink_term[j * BLOCK:(j + 1) * BLOCK])
                    dq_ref[:, h * HEAD_DIM:(h + 1) * HEAD_DIM] = dq[j * BLOCK:(j + 1) * BLOCK]
                dk_t = _dot_tn(qs, ds)
                dv_t = _dot_tn(dosb, p.astype(BF16))
                ks = slice(g * HEAD_DIM, (g + 1) * HEAD_DIM)
                vs = slice(KV_W + g * HEAD_DIM, KV_W + (g + 1) * HEAD_DIM)
                for sl, grad_t in ((ks, dk_t), (vs, dv_t)):
                    dmeta_ref[:, sl] += grad_t[:, 0:BLOCK].T
                    dkv_ref[:, sl] = hold_ref[:, sl] + grad_t[:, BLOCK:2 * BLOCK].T
                    hold_ref[:, sl] = grad_t[:, 2 * BLOCK:3 * BLOCK].T

        @pl.when(i == nb)
        def _():
            dkv_ref[...] = hold_ref[...]
            for cp in _scatter(src, dst, send_sems, recv_sems):
                cp.wait()

    def cur(i):
        return jnp.minimum(i, nb - 1)

    return pl.pallas_call(
        body, name="attn_bwd", grid=(nb + 1,),
        in_specs=[pl.BlockSpec(memory_space=pltpu.SMEM),
                  pl.BlockSpec((BLOCK, ATTN_W), lambda i: (cur(i), 0)),
                  pl.BlockSpec((BLOCK, 2 * KV_W), lambda i: (cur(i), 0)),
                  pl.BlockSpec((BLOCK, 2 * KV_W), lambda i: (jnp.maximum(cur(i) - 1, 0), 0)),
                  _const((BLOCK, 2 * KV_W)),
                  pl.BlockSpec((BLOCK, ATTN_W), lambda i: (cur(i), 0)),
                  pl.BlockSpec((BLOCK, ATTN_W), lambda i: (cur(i), 0)),
                  pl.BlockSpec((BLOCK, BLOCK), lambda i: (cur(i), 0))] + [ANY] * n,
        out_specs=[pl.BlockSpec((BLOCK, ATTN_W), lambda i: (cur(i), 0)),
                   pl.BlockSpec((BLOCK, 2 * KV_W), lambda i: (jnp.maximum(i - 1, 0), 0)),
                   _const((BLOCK, 2 * KV_W)), _const((N_HEADS, BLOCK))] + [ANY] * n,
        out_shape=[jax.ShapeDtypeStruct((r, ATTN_W), BF16), jax.ShapeDtypeStruct((r, 2 * KV_W), F32),
                   jax.ShapeDtypeStruct((BLOCK, 2 * KV_W), F32), jax.ShapeDtypeStruct((N_HEADS, BLOCK), F32)] + _scatter_landing(parts),
        scratch_shapes=[pltpu.VMEM((BLOCK, 2 * KV_W), F32), pltpu.VMEM((N_KV, GROUP * BLOCK, 2 * BLOCK), F32),
                        pltpu.VMEM((GROUP * BLOCK, BLOCK), F32), pltpu.SemaphoreType.DMA((7 * n,)), pltpu.SemaphoreType.DMA((7 * n,))],
        compiler_params=_cparams(),
    )(sinks, q, kv, kv, kv, o, do, lse, *parts)


def _in_proj_bwd(dq, dkv, dkv_meta, dc, dh1, h0, g1, w_in_t, seq, tm):
    r = h0.shape[0]
    n_tiles = r // tm
    n_out = -(-seq // tm)

    def body(dq_ref, dkv_ref, dm_ref, dc_ref, dh1_ref, h_ref, g_ref, w_ref, gx_ref, lead_ref, dwt_ref, dg_ref, dw_ref, hold_ref):
        i = pl.program_id(0)

        @pl.when(i == 0)
        def _():
            dw_ref[...] = jnp.zeros_like(dw_ref)
            dg_ref[...] = jnp.zeros_like(dg_ref)

        @pl.when(i < n_tiles)
        def _():
            meta = jnp.concatenate([dm_ref[...], jnp.zeros((tm - BLOCK, 2 * KV_W), F32)], axis=0) if tm > BLOCK else dm_ref[...]
            first = (jnp.zeros((tm, 2 * KV_W), jnp.int32) + i) == 0
            dkvb = (dkv_ref[...] + jnp.where(first, meta, 0.0)).astype(BF16)
            hn, xhat, rstd = _rms_fwd(h_ref[...], g_ref[...])
            dproj = jnp.concatenate([dq_ref[...], dkvb, dc_ref[...]], axis=1)
            dhn = _dot(dproj, w_ref[...])
            dw_ref[...] += _dot_tn(dproj, hn.astype(BF16))
            dx, dg_rows = _rms_bwd(dhn, xhat, rstd, g_ref[...])
            dg_ref[...] += jnp.sum(dg_rows, axis=0, keepdims=True)
            dh0 = dh1_ref[...] + dx

            @pl.when(i == 0)
            def _():
                lead_ref[...] = dh0[0:BLOCK]

            @pl.when((i >= 1) & (i <= n_out))
            def _():
                gx_ref[0:tm - BLOCK, :] = hold_ref[...]
                gx_ref[tm - BLOCK:tm, :] = dh0[0:BLOCK]

            hold_ref[...] = dh0[BLOCK:tm]

        @pl.when((i == n_tiles) & (n_tiles <= n_out))
        def _():
            gx_ref[0:tm - BLOCK, :] = hold_ref[...]

        @pl.when(i == n_tiles - 1)
        def _():
            dwt_ref[...] = dw_ref[...].astype(BF16)

    def tile(n):
        return pl.BlockSpec((tm, n), lambda i: (jnp.minimum(i, n_tiles - 1), 0))

    return pl.pallas_call(
        body, name="in_proj_bwd", grid=(n_tiles + 1,),
        in_specs=[tile(ATTN_W), tile(2 * KV_W), _const((BLOCK, 2 * KV_W)), tile(2 * CONV_W), tile(D_MODEL), tile(D_MODEL),
                  _const((1, D_MODEL)), _const((IN_COLS, D_MODEL))],
        out_specs=[pl.BlockSpec((tm, D_MODEL), lambda i: (jnp.clip(i - 1, 0, n_out - 1), 0)), _const((BLOCK, D_MODEL)),
                   _const((IN_COLS, D_MODEL)), _const((1, D_MODEL))],
        out_shape=[jax.ShapeDtypeStruct((seq, D_MODEL), F32), jax.ShapeDtypeStruct((BLOCK, D_MODEL), F32),
                   jax.ShapeDtypeStruct((IN_COLS, D_MODEL), BF16), jax.ShapeDtypeStruct((1, D_MODEL), F32)],
        scratch_shapes=[pltpu.VMEM((IN_COLS, D_MODEL), F32), pltpu.VMEM((tm - BLOCK, D_MODEL), F32)],
        compiler_params=_cparams(),
    )(dq, dkv, dkv_meta, dc, dh1, h0, g1, w_in_t)


def _adamw_update(w_ref, g_ref, m_ref, v_ref, d_ref, nm_ref, nv_ref):
    g = g_ref[...]
    m = ADAM_B1 * m_ref[...] + (1.0 - ADAM_B1) * g
    v = ADAM_B2 * v_ref[...] + (1.0 - ADAM_B2) * (g * g)
    m_hat = m / (1.0 - ADAM_B1 ** ADAM_STEP)
    v_hat = v / (1.0 - ADAM_B2 ** ADAM_STEP)
    d_ref[...] = -ADAM_LR * (m_hat / (jnp.sqrt(v_hat) + ADAM_EPS) + ADAM_WD * w_ref[...])
    nm_ref[...] = m
    nv_ref[...] = v


def _adamw(w, g, m, v, name):
    rows, cols = w.shape
    tr = rows
    for cand in (256, 176, 128, 64, 32, 16, 8):
        if rows % cand == 0:
            tr = cand
            break

    def body(*refs):
        _adamw_update(*refs)

    spec = _row(tr, cols)
    return pl.pallas_call(
        body, name=name, grid=(rows // tr,), in_specs=[spec] * 4, out_specs=[spec] * 3,
        out_shape=[jax.ShapeDtypeStruct((rows, cols), F32)] * 3, compiler_params=_cparams(),
    )(w, g, m, v)


MESH = pl.DeviceIdType.MESH
ANY = pl.BlockSpec(memory_space=pl.ANY)


def _place():
    x, y, c = lax.axis_index("x"), lax.axis_index("y"), lax.axis_index("c")
    chips = [(1 - x, y), (x, 1 - y), (1 - x, 1 - y)]
    return x, y, c, chips


def _gather_ici(dst, send_sems, recv_sems):
    x, y, c, chips = _place()
    sends, arrivals = [], []
    for k in range(len(dst)):
        rows = dst[k].shape[1] // 2
        half = pl.ds(c * rows, rows)
        mine = dst[k].at[2 * x + y, half]
        for p, chip in enumerate(chips):
            sems = dict(send_sem=send_sems.at[3 * k + p], recv_sem=recv_sems.at[3 * k + p], device_id=(chip[0], chip[1], c),
                        device_id_type=MESH)
            sends.append(pltpu.make_async_remote_copy(src_ref=mine, dst_ref=mine, **sems))
            theirs = dst[k].at[2 * chip[0] + chip[1], half]
            arrivals.append(pltpu.make_async_remote_copy(src_ref=theirs, dst_ref=theirs, **sems))
    return sends, arrivals


def _gather_d2d(dst, send_sems, recv_sems):
    x, y, c, chips = _place()
    sends, arrivals = [], []
    for k in range(len(dst)):
        rows = dst[k].shape[1] // 2
        for p, chip in enumerate(chips):
            sems = dict(send_sem=send_sems.at[3 * k + p], recv_sem=recv_sems.at[3 * k + p], device_id=(x, y, 1 - c),
                        device_id_type=MESH)
            mine = dst[k].at[2 * chip[0] + chip[1], pl.ds(c * rows, rows)]
            sends.append(pltpu.make_async_remote_copy(src_ref=mine, dst_ref=mine, **sems))
            theirs = dst[k].at[2 * chip[0] + chip[1], pl.ds((1 - c) * rows, rows)]
            arrivals.append(pltpu.make_async_remote_copy(src_ref=theirs, dst_ref=theirs, **sems))
    return sends, arrivals


def _own_slots(shard, dtype):
    return jnp.broadcast_to(shard[None], (N_SHARD,) + shard.shape).astype(dtype)


def _gather_weights(slots):
    n = len(slots)

    def body(*refs):
        dst = refs[n:2 * n]
        ici_send, ici_recv, d2d_send, d2d_recv = refs[2 * n:]
        sends, arrivals = _gather_ici(dst, ici_send, ici_recv)
        for cp in sends:
            cp.start()
        for cp in arrivals:
            cp.wait_recv()
        forwards, from_sibling = _gather_d2d(dst, d2d_send, d2d_recv)
        for cp in forwards:
            cp.start()
        for cp in from_sibling:
            cp.wait_recv()
        for cp in sends + forwards:
            cp.wait_send()

    return pl.pallas_call(
        body, name="gather_weights",
        in_specs=[ANY] * n, out_specs=[ANY] * n,
        out_shape=[jax.ShapeDtypeStruct(s.shape, s.dtype) for s in slots],
        input_output_aliases={k: k for k in range(n)},
        scratch_shapes=[pltpu.SemaphoreType.DMA((3 * n,))] * 4,
    )(*slots)


VMEM_WHOLE = pl.BlockSpec(memory_space=pltpu.VMEM)


SMALL_ROWS = 48
SMALL_PLACES = ((0, (0, 1), 0, 0), (1, (0, 1), 1, 0), (2, (0, 1), 2, 0), (3, (0, 16), 8, 0), (4, (0, 1), 3, 0), (5, (0, 1), 3, 512),
                (6, (0, 1), 4, 0), (7, (0, 1), 4, 512), (8, (0, 1), 5, 0), (9, (0, 16), 24, 0), (9, (16, 32), 24, 512),
                (10, (0, 1), 5, 512), (11, (0, 8), 40, 0))


def _pack_small(parts):
    n = len(parts)

    def body(*refs):
        out = refs[n]
        out[...] = jnp.zeros_like(out)
        for k, (lo, hi), r0, c0 in SMALL_PLACES:
            out[r0:r0 + hi - lo, c0:c0 + parts[k].shape[1]] = refs[k][lo:hi, :]

    return pl.pallas_call(body, name="pack_small", in_specs=[VMEM_WHOLE] * n, out_specs=VMEM_WHOLE,
                          out_shape=jax.ShapeDtypeStruct((SMALL_ROWS, D_MODEL), F32))(*parts)


def _sum_small(own, landed, parts):
    n = len(parts)

    def body(own_ref, landed_ref, *outs):
        x, y, c = lax.axis_index("x"), lax.axis_index("y"), lax.axis_index("c")
        total = jnp.zeros((SMALL_ROWS, D_MODEL), F32)
        for d in range(N_DEV):
            j = 4 * ((x + (d >> 2)) % 2) + 2 * ((y + ((d >> 1) & 1)) % 2) + (c + (d & 1)) % 2
            mine = (jnp.zeros((SMALL_ROWS, D_MODEL), jnp.int32) + j) == 0
            total = total + jnp.where(mine, own_ref[...], landed_ref[jnp.maximum(j - 1, 0)])
        for k, (lo, hi), r0, c0 in SMALL_PLACES:
            outs[k][lo:hi, :] = total[r0:r0 + hi - lo, c0:c0 + parts[k].shape[1]]

    return pl.pallas_call(body, name="sum_small", in_specs=[VMEM_WHOLE] * 2, out_specs=[VMEM_WHOLE] * n,
                          out_shape=[jax.ShapeDtypeStruct(p.shape, F32) for p in parts])(own, landed)


def _adamw_small(ws, gs, ms, vs):
    n = len(ws)

    def body(*refs):
        for k in range(n):
            w_ref, g_ref, m_ref, v_ref = (refs[j * n + k] for j in range(4))
            _adamw_update(w_ref, g_ref, m_ref, v_ref, *(refs[(4 + j) * n + k] for j in range(3)))

    shapes = [jax.ShapeDtypeStruct(w.shape, F32) for w in ws]
    out = pl.pallas_call(
        body, name="adamw_small", in_specs=[VMEM_WHOLE] * (4 * n), out_specs=[VMEM_WHOLE] * (3 * n), out_shape=shapes * 3,
    )(*ws, *gs, *ms, *vs)
    return out[:n], out[n:2 * n], out[2 * n:]


def _scatter(src, dst, send_sems, recv_sems, whole=False):
    x, y, c = lax.axis_index("x"), lax.axis_index("y"), lax.axis_index("c")
    copies = []
    for k in range(len(src)):
        for j in range(1, N_DEV):
            px, py, pc = (x + (j >> 2)) % 2, (y + ((j >> 1) & 1)) % 2, (c + (j & 1)) % 2
            rows = src[k].shape[1] // 2
            piece = src[k] if whole else src[k].at[2 * px + py, pl.ds(pc * rows, rows)]
            copies.append(pltpu.make_async_remote_copy(
                src_ref=piece, dst_ref=dst[k].at[j - 1],
                send_sem=send_sems.at[7 * k + j - 1], recv_sem=recv_sems.at[7 * k + j - 1], device_id=(px, py, pc),
                device_id_type=MESH))
    return copies


def _scatter_landing(parts, whole=False):
    return [jax.ShapeDtypeStruct((N_DEV - 1,) + (p.shape if whole else (p.shape[1] // 2, p.shape[2])), p.dtype) for p in parts]


HBM = pl.BlockSpec(memory_space=pltpu.HBM)
SEMAPHORES = pl.BlockSpec(memory_space=pltpu.SEMAPHORE)
SPLIT_COPY = pltpu.CompilerParams(has_side_effects=pltpu.SideEffectType.DATAFLOW_SIDE_EFFECTING)


def _scatter_start(part, whole, name):
    landing, = _scatter_landing([part], whole)

    def body(src_ref, land_ref, send_sems, recv_sems, src_thru, land_thru, token_ref):
        for cp in _scatter([src_ref], [land_ref], send_sems, recv_sems, whole):
            cp.start()
        token_ref[...] = jnp.zeros_like(token_ref)

    return pl.pallas_call(
        body, name=name,
        out_shape=(pltpu.SemaphoreType.DMA((N_DEV - 1,)), pltpu.SemaphoreType.DMA((N_DEV - 1,)), pltpu.HBM(part.shape, part.dtype),
                   pltpu.HBM(landing.shape, landing.dtype), jax.ShapeDtypeStruct((8, 128), F32)),
        in_specs=(HBM, HBM), out_specs=(SEMAPHORES, SEMAPHORES, HBM, HBM, VMEM_WHOLE), input_output_aliases={0: 2, 1: 3},
        compiler_params=SPLIT_COPY,
    )(pltpu.with_memory_space_constraint(part, pltpu.HBM),
      pltpu.with_memory_space_constraint(lax.empty(landing.shape, landing.dtype), pltpu.HBM))


def _scatter_wait(send_sems, recv_sems, part_thru, land_thru, after, whole, name):
    def body(src_ref, land_ref, send_ref, recv_ref, after_ref, src_dead, got_ref):
        for cp in _scatter([src_ref], [land_ref], send_ref, recv_ref, whole):
            cp.wait_send()
            cp.wait_recv()

    return pl.pallas_call(
        body, name=name,
        out_shape=(pltpu.HBM(part_thru.shape, part_thru.dtype), pltpu.HBM(land_thru.shape, land_thru.dtype)),
        in_specs=(HBM, HBM, SEMAPHORES, SEMAPHORES, ANY), out_specs=(HBM, HBM), input_output_aliases={0: 0, 1: 1},
        compiler_params=SPLIT_COPY,
    )(part_thru, land_thru, send_sems, recv_sems, after)


def _sum_pieces(own, landed, behind, name):
    n = len(own)

    def body(*refs):
        for k in range(n):
            got = refs[n + k]
            total = refs[k][...].astype(F32)
            for j in range(N_DEV - 1):
                total = total + got[j].astype(F32)
            refs[2 * n + 1 + k][...] = total

    in_specs, out_specs = [], []
    for o in own:
        in_specs.append(_row(o.shape[0] // 2, o.shape[1]))
    for o in own:
        in_specs.append(pl.BlockSpec((N_DEV - 1, o.shape[0] // 2, o.shape[1]), lambda i: (0, i, 0)))
        out_specs.append(_row(o.shape[0] // 2, o.shape[1]))
    return pl.pallas_call(
        body, name=name, grid=(2,), in_specs=in_specs + [ANY], out_specs=out_specs,
        out_shape=[jax.ShapeDtypeStruct(o.shape, F32) for o in own], compiler_params=_cparams(),
    )(*own, *landed, behind)


def _swap_with_sibling(halves, name):
    n = len(halves)

    def body(*refs):
        x, y, c = lax.axis_index("x"), lax.axis_index("y"), lax.axis_index("c")
        copies = [pltpu.make_async_remote_copy(
            src_ref=refs[k], dst_ref=refs[n + k], send_sem=refs[2 * n].at[k], recv_sem=refs[2 * n + 1].at[k],
            device_id=(x, y, 1 - c), device_id_type=MESH) for k in range(n)]
        for cp in copies:
            cp.start()
        for cp in copies:
            cp.wait()

    return pl.pallas_call(
        body, name=name, in_specs=[ANY] * n, out_specs=[ANY] * n,
        out_shape=[jax.ShapeDtypeStruct(h.shape, h.dtype) for h in halves],
        scratch_shapes=[pltpu.SemaphoreType.DMA((n,)), pltpu.SemaphoreType.DMA((n,))],
    )(*halves)


def _own_piece(part):
    rows = part.shape[1] // 2
    s = 2 * lax.axis_index("x") + lax.axis_index("y")
    return lax.dynamic_slice(part, (s, lax.axis_index("c") * rows, 0), (1, rows, part.shape[2]))[0]


def _both_halves(mine, theirs):
    south = lax.axis_index("c") == 0
    return jnp.concatenate([jnp.where(south, mine, theirs), jnp.where(south, theirs, mine)], axis=0)


def _from_col_shards(g):
    return g.transpose(1, 0, 2).reshape(g.shape[1], -1)


def kernel(x, meta_tokens, attn_norm_g, w_in, attn_sinks, conv_w, conv_b, conv_ln_g, conv_ln_b, attn_out_g, conv_out_g, w_out, ffn_norm_g, w_gate, w_up, w_down, final_norm_g, loss_target, m_meta_tokens, m_attn_norm_g, m_w_in, m_attn_sinks, m_conv_w, m_conv_b, m_conv_ln_g, m_conv_ln_b, m_attn_out_g, m_conv_out_g, m_w_out, m_ffn_norm_g, m_w_gate, m_w_up, m_w_down, m_final_norm_g, v_meta_tokens, v_attn_norm_g, v_w_in, v_attn_sinks, v_conv_w, v_conv_b, v_conv_ln_g, v_conv_ln_b, v_attn_out_g, v_conv_out_g, v_w_out, v_ffn_norm_g, v_w_gate, v_w_up, v_w_down, v_final_norm_g):
    seq = x.shape[1]
    r = -(-(seq + BLOCK) // ROW_QUANTUM) * ROW_QUANTUM
    tm_wide = 768 if seq >= 768 else 256
    shard = 2 * lax.axis_index("x") + lax.axis_index("y")

    conv_w32 = jnp.pad(conv_w[0], ((0, 1), (0, 0)))
    small_shard = jnp.concatenate([meta_tokens, conv_w32.reshape(16, 256)], axis=0)
    g_in, g_small = _gather_weights([_own_slots(w_in[0].T, BF16), _own_slots(small_shard, F32)])
    later = [_own_slots(w, BF16) for w in (w_gate[0].T, w_up[0].T, w_out[0], w_down[0])]
    w_in_t = g_in.reshape(IN_COLS, D_MODEL)
    meta_full = _from_col_shards(g_small[:, 0:N_META])
    cw_full = _from_col_shards(g_small[:, N_META:].reshape(N_SHARD, 32, 128))

    g1, ga, gc, g2 = attn_norm_g, attn_out_g, conv_out_g, ffn_norm_g
    gf = final_norm_g.reshape(1, D_MODEL)
    sinks = attn_sinks[0]

    lead = jnp.concatenate([jnp.zeros((LEAD, D_MODEL), F32), meta_full], axis=0)
    h0, q, kv, cacg = _in_proj(x[0], lead, g1, w_in_t, r, 768)
    oa, lse, *gathered = _attn_fwd(q, kv, sinks, later)
    oc, yc, g_gate, g_up, g_out, g_down = _conv_fwd(cacg, cw_full, conv_b, conv_ln_g, conv_ln_b, 384, gathered)
    wg_t, wu_t, wd_b = g_gate.reshape(D_FF, D_MODEL), g_up.reshape(D_FF, D_MODEL), g_down.reshape(D_FF, D_MODEL)
    w_out_b = g_out.reshape(D_MODEL, D_MODEL)
    h1, hn2 = _out_proj(oa, oc, h0, ga, gc, g2, w_out_b, 768)
    gate, up, act, dh2, dh2b, loss_p, dgf = _ffn_fwd(hn2, h1, loss_target[0], gf, wg_t, wu_t, wd_b, 384)

    def by_shard(dw):
        return dw.reshape(N_SHARD, dw.shape[0] // N_SHARD, D_MODEL)

    dgate, dup, dh1, dg2 = _ffn_bwd(dh2, dh2b, gate, up, h1, g2, wg_t, wu_t, wd_b, 384)
    p_gate, p_up = [by_shard(dw) for dw in _ffn_wgrad_gu(hn2, dgate, dup, 768)]
    p_down = by_shard(_ffn_wgrad_d(act, dh2b, 768))
    doa, doc, dwo, dga, dgc = _out_proj_bwd(dh1, oa, oc, ga, gc, w_out_b, 768)
    p_out = by_shard(dwo)
    dy, dcw, dcb, dlg, dlb, l_gate, l_up = _conv_bwd_params(doc, yc, cacg, conv_ln_g, conv_ln_b, 384, [p_gate, p_up])
    dc = _conv_bwd_data(dy, cacg, cw_full, 384)
    dq, dkv, dkv_meta, dsink, l_out, l_down = _attn_bwd(q, kv, oa, doa, lse, sinks, [p_out, p_down])
    grad_x, dlead, dwi_t, dg1 = _in_proj_bwd(dq, dkv, dkv_meta, dc, dh1, h0, g1, w_in_t, seq, tm_wide)
    in_send, in_recv, p_in_thru, l_in_thru, started = _scatter_start(by_shard(dwi_t), False, "scatter_start")
    small_parts = [dgf, dg1, dg2, dlead[LEAD:BLOCK], dga, dgc, dcb, dlg, dlb, dcw, loss_p, dsink]
    sm_send, sm_recv, pack_thru, packs_thru, _ = _scatter_start(_pack_small(small_parts), True, "gather_small_start")

    big = ("w_gate", "w_up", "w_out", "w_down", "w_in")
    transposed = ("w_in", "w_gate", "w_up")
    grads = {}
    params = {
        "meta_tokens": (meta_tokens, m_meta_tokens, v_meta_tokens), "attn_norm_g": (attn_norm_g, m_attn_norm_g, v_attn_norm_g),
        "w_in": (w_in, m_w_in, v_w_in), "attn_sinks": (attn_sinks, m_attn_sinks, v_attn_sinks), "conv_w": (conv_w, m_conv_w, v_conv_w),
        "conv_b": (conv_b, m_conv_b, v_conv_b), "conv_ln_g": (conv_ln_g, m_conv_ln_g, v_conv_ln_g),
        "conv_ln_b": (conv_ln_b, m_conv_ln_b, v_conv_ln_b), "attn_out_g": (attn_out_g, m_attn_out_g, v_attn_out_g),
        "conv_out_g": (conv_out_g, m_conv_out_g, v_conv_out_g), "w_out": (w_out, m_w_out, v_w_out),
        "ffn_norm_g": (ffn_norm_g, m_ffn_norm_g, v_ffn_norm_g), "w_gate": (w_gate, m_w_gate, v_w_gate), "w_up": (w_up, m_w_up, v_w_up),
        "w_down": (w_down, m_w_down, v_w_down), "final_norm_g": (final_norm_g, m_final_norm_g, v_final_norm_g)}
    names = list(params)
    delta, new_m, new_v = {}, {}, {}

    def finish(group, parts, landed, behind, tag):
        halves = _sum_pieces([_own_piece(p) for p in parts], landed, behind, "sum_pieces_" + tag)
        for name, mine, theirs in zip(group, halves, _swap_with_sibling(halves, "swap_with_sibling_" + tag)):
            flip = (lambda a: a.T) if name in transposed else (lambda a: a)
            g = _both_halves(mine, theirs)
            w, m, v = params[name]
            outs = _adamw(flip(w[0]), g, flip(m[0]), flip(v[0]), "adamw_" + name)
            grads[name], delta[name], new_m[name], new_v[name] = [flip(a)[None] for a in (g, *outs)]
        return outs[0]

    done = finish(big[:4], [p_gate, p_up, p_out, p_down], [l_gate, l_up, l_out, l_down], started, "ffn_out")

    red_names = ("final_norm_g", "attn_norm_g", "ffn_norm_g", "meta_tokens", "attn_out_g", "conv_out_g", "conv_b", "conv_ln_g",
                 "conv_ln_b", "conv_w", "loss", "attn_sinks")
    pack, packs = _scatter_wait(sm_send, sm_recv, pack_thru, packs_thru, done, True, "gather_small_wait")
    red = dict(zip(red_names, _sum_small(pack, packs, small_parts)))
    loss = red["loss"][0, 0]
    for name in ("attn_norm_g", "conv_b", "conv_ln_g", "conv_ln_b", "attn_out_g", "conv_out_g", "ffn_norm_g"):
        grads[name] = red[name]
    grads["final_norm_g"] = red["final_norm_g"].reshape(D_MODEL)
    grads["attn_sinks"] = red["attn_sinks"][:, 0].reshape(1, N_HEADS)
    grads["meta_tokens"] = lax.dynamic_slice_in_dim(red["meta_tokens"], shard * (D_MODEL // N_SHARD), D_MODEL // N_SHARD, axis=1)
    grads["conv_w"] = lax.dynamic_slice_in_dim(
        red["conv_w"][0:CONV_K], shard * (CONV_W // N_SHARD), CONV_W // N_SHARD, axis=1)[None]

    p_in, l_in = _scatter_wait(in_send, in_recv, p_in_thru, l_in_thru, red["loss"], False, "scatter_wait")
    finish(big[4:], [p_in], [l_in], l_in, "in")
    rest = [name for name in names if name not in big]

    def rows_of(a):
        return a.reshape(-1, a.shape[-1])

    small = _adamw_small([rows_of(params[n][0]) for n in rest], [rows_of(grads[n]) for n in rest],
                         [rows_of(params[n][1]) for n in rest], [rows_of(params[n][2]) for n in rest])
    for dst, outs in zip((delta, new_m, new_v), small):
        for name, out in zip(rest, outs):
            dst[name] = out.reshape(params[name][0].shape)

    return (loss, grad_x[None], *[grads[n] for n in names], *[delta[n] for n in names], *[new_m[n] for n in names],
            *[new_v[n] for n in names])
```

```python
import functools
import math

import jax
import jax.numpy as jnp
from jax import lax
from jax.experimental import pallas as pl
from jax.experimental.pallas import tpu as pltpu

F32 = jnp.float32
BF16 = jnp.bfloat16

D_MODEL = 1024
N_META = 16
ATTN_W = 512
CONV_W = 512
HEAD_DIM = 64
N_HEADS = 8
N_KV = 2
GROUP = N_HEADS // N_KV
KV_W = N_KV * HEAD_DIM
BLOCK = 128
LEAD = BLOCK - N_META
CONV_K = 31
D_FF = 2816
IN_COLS = ATTN_W + 2 * KV_W + 2 * CONV_W
Q0, KV0, C0 = 0, ATTN_W, ATTN_W + 2 * KV_W
NORM_EPS = 1e-5
SCALE = 1.0 / math.sqrt(HEAD_DIM)
SLOPES = tuple(2.0 ** (-(8.0 / N_HEADS) * (h + 1)) for h in range(N_HEADS))
NEG = -1e30

ADAM_LR, ADAM_B1, ADAM_B2, ADAM_EPS, ADAM_WD, ADAM_STEP = 0.001, 0.9, 0.999, 1e-08, 0.01, 10

N_SHARD = 4
N_DEV = 8
ROW_QUANTUM = 768
HALO = 32
CONV_CHUNK = 32
FF_CHUNK = 256
FF_CHUNKS = tuple(slice(c, c + FF_CHUNK) for c in range(0, D_FF, FF_CHUNK))
VMEM_LIMIT = 60 * 1024 * 1024


def _cparams(n_axes=1):
    return pltpu.CompilerParams(dimension_semantics=("arbitrary",) * n_axes, vmem_limit_bytes=VMEM_LIMIT)


def _dot(a, b):
    return jnp.dot(a, b, preferred_element_type=F32)


def _dot_nt(a, b):
    return lax.dot_general(a, b, (((1,), (1,)), ((), ())), preferred_element_type=F32)


def _dot_tn(a, b):
    return lax.dot_general(a, b, (((0,), (0,)), ((), ())), preferred_element_type=F32)


def _sigmoid(x):
    return 1.0 / (1.0 + jnp.exp(-x))


def _row(tm, n):
    return pl.BlockSpec((tm, n), lambda i: (i, 0))


def _const(shape):
    return pl.BlockSpec(shape, lambda i: (0,) * len(shape))


def _resident(shape):
    return pl.BlockSpec(shape, lambda i: (0,) * len(shape), pipeline_mode=pl.Buffered(1))


def _rms_fwd(x, g):
    rstd = lax.rsqrt(jnp.mean(x * x, axis=-1, keepdims=True) + NORM_EPS)
    xhat = x * rstd
    return xhat * g, xhat, rstd


def _rms_bwd(dy, xhat, rstd, g):
    dxh = dy * g
    dx = rstd * (dxh - xhat * jnp.mean(dxh * xhat, axis=-1, keepdims=True))
    return dx, dy * xhat


def _in_proj(x, lead, g1, w_in_t, r, tm):
    seq = x.shape[0]
    n_sub = tm // BLOCK

    def body(*refs):
        x_refs = refs[:n_sub]
        lead_ref, g_ref, w_ref, h0_ref, q_ref, kv_ref, c_ref = refs[n_sub:]
        i = pl.program_id(0)
        pieces = []
        for k, x_ref in enumerate(x_refs):
            at = jnp.zeros((BLOCK, D_MODEL), jnp.int32) + (i * tm + (k - 1) * BLOCK)
            piece = jnp.where((at >= 0) & (at < seq), x_ref[...], 0.0)
            pieces.append(jnp.where(at < 0, lead_ref[...], piece) if k == 0 else piece)
        h = jnp.concatenate(pieces, axis=0)
        h0_ref[...] = h
        hn = _rms_fwd(h, g_ref[...])[0].astype(BF16)
        q_ref[...] = _dot_nt(hn, w_ref[Q0:KV0, :]).astype(BF16)
        kv_ref[...] = _dot_nt(hn, w_ref[KV0:C0, :]).astype(BF16)
        c_ref[...] = _dot_nt(hn, w_ref[C0:IN_COLS, :])

    def x_block(k):
        return pl.BlockSpec((BLOCK, D_MODEL), lambda i: (jnp.clip(n_sub * i - 1 + k, 0, seq // BLOCK - 1), 0))

    return pl.pallas_call(
        body, name="in_proj", grid=(r // tm,),
        in_specs=[x_block(k) for k in range(n_sub)] + [_const((BLOCK, D_MODEL)), _const((1, D_MODEL)), _const((IN_COLS, D_MODEL))],
        out_specs=[_row(tm, D_MODEL), _row(tm, ATTN_W), _row(tm, 2 * KV_W), _row(tm, 2 * CONV_W)],
        out_shape=[jax.ShapeDtypeStruct((r, D_MODEL), F32), jax.ShapeDtypeStruct((r, ATTN_W), BF16),
                   jax.ShapeDtypeStruct((r, 2 * KV_W), BF16), jax.ShapeDtypeStruct((r, 2 * CONV_W), F32)],
        compiler_params=_cparams(),
    )(*[x] * n_sub, lead, g1, w_in_t)


def _attn_bias_init(bias_ref, late_ref):
    row = lax.broadcasted_iota(jnp.int32, (GROUP * BLOCK, BLOCK), 0) & (BLOCK - 1)
    col = lax.broadcasted_iota(jnp.int32, (GROUP * BLOCK, BLOCK), 1)
    late_ref[...] = jnp.where(col > row, 1.0, 0.0)
    for g in range(N_KV):
        slope = jnp.concatenate([jnp.zeros((BLOCK, BLOCK), F32) + SLOPES[g * GROUP + j] for j in range(GROUP)], axis=0)
        bias_ref[g, :, 0:BLOCK] = jnp.where(col >= LEAD, 0.0, NEG)
        bias_ref[g, :, BLOCK:2 * BLOCK] = -slope * jnp.where(col > row, row - col + BLOCK, row - col).astype(F32)


def _attn_block_bias(late_ref, i):
    late = late_ref[...]
    meta0 = jnp.where(i == 0, NEG, 0.0)
    no_prev = jnp.where(i >= 2, 0.0, NEG)
    no_cur = jnp.where(i >= 1, 0.0, NEG)
    return late * meta0, late * no_prev + no_cur


def _attn_logits(s3, bias_ref, block_bias, prev_part, g):
    meta = s3[:, 0:BLOCK] * SCALE + (bias_ref[g, :, 0:BLOCK] + block_bias[0])
    band = jnp.where(prev_part, s3[:, BLOCK:2 * BLOCK], s3[:, 2 * BLOCK:3 * BLOCK]) * SCALE + (bias_ref[g, :, BLOCK:2 * BLOCK] + block_bias[1])
    return meta, band


def _split_band(meta, band, prev_part):
    return jnp.concatenate([meta, jnp.where(prev_part, band, 0.0), jnp.where(prev_part, 0.0, band)], axis=1)


def _head_rows(vals):
    return jnp.concatenate([jnp.zeros((BLOCK, BLOCK), F32) + v for v in vals], axis=0)


def _stack_heads(ref, g):
    return jnp.concatenate([ref[:, (g * GROUP + j) * HEAD_DIM:(g * GROUP + j + 1) * HEAD_DIM] for j in range(GROUP)], axis=0)


def _kv_cat(kvm_ref, kvp_ref, kvc_ref, g):
    ks = slice(g * HEAD_DIM, (g + 1) * HEAD_DIM)
    vs = slice(KV_W + g * HEAD_DIM, KV_W + (g + 1) * HEAD_DIM)
    kcat = jnp.concatenate([kvm_ref[:, ks], kvp_ref[:, ks], kvc_ref[:, ks]], axis=0)
    vcat = jnp.concatenate([kvm_ref[:, vs], kvp_ref[:, vs], kvc_ref[:, vs]], axis=0)
    return kcat, vcat


def _attn_fwd(q, kv, sinks, gathered):
    r = q.shape[0]
    nb = r // BLOCK
    n = len(gathered)

    def body(sink_ref, q_ref, kvc_ref, kvp_ref, kvm_ref, *rest):
        o_ref, lse_ref = rest[n:n + 2]
        dst = rest[n + 2:2 * n + 2]
        bias_ref, late_ref, send_sems, recv_sems = rest[2 * n + 2:]
        i = pl.program_id(0)

        @pl.when(i == 0)
        def _():
            for cp in _gather_ici(dst, send_sems, recv_sems)[0]:
                cp.start()
            _attn_bias_init(bias_ref, late_ref)

        lane = lax.broadcasted_iota(jnp.int32, (BLOCK, BLOCK), 1)
        lse_tile = jnp.zeros((BLOCK, BLOCK), F32)
        block_bias = _attn_block_bias(late_ref, i)
        prev_part = late_ref[...] > 0.5
        for g in range(N_KV):
            kcat, vcat = _kv_cat(kvm_ref, kvp_ref, kvc_ref, g)
            heads = range(g * GROUP, (g + 1) * GROUP)
            meta, band = _attn_logits(_dot_nt(_stack_heads(q_ref, g), kcat), bias_ref, block_bias, prev_part, g)
            sink = _head_rows([sink_ref[h] for h in heads])
            m = jnp.maximum(jnp.max(jnp.maximum(meta, band), axis=-1, keepdims=True), sink)
            p_meta, p_band = jnp.exp(meta - m), jnp.exp(band - m)
            l = jnp.sum(p_meta + p_band, axis=-1, keepdims=True) + jnp.exp(sink - m)
            o = _dot(_split_band(p_meta, p_band, prev_part).astype(BF16), vcat) * (1.0 / l)[:, 0:HEAD_DIM]
            lse = m + jnp.log(l)
            for j, h in enumerate(heads):
                o_ref[:, h * HEAD_DIM:(h + 1) * HEAD_DIM] = o[j * BLOCK:(j + 1) * BLOCK]
                lse_tile = jnp.where(lane == h, lse[j * BLOCK:(j + 1) * BLOCK], lse_tile)
        lse_ref[...] = lse_tile

        @pl.when(i == nb - 1)
        def _():
            sends, arrivals = _gather_ici(dst, send_sems, recv_sems)
            for cp in arrivals:
                cp.wait_recv()
            for cp in sends:
                cp.wait_send()

    return pl.pallas_call(
        body, name="attn_fwd", grid=(nb,),
        in_specs=[pl.BlockSpec(memory_space=pltpu.SMEM), _row(BLOCK, ATTN_W), _row(BLOCK, 2 * KV_W),
                  pl.BlockSpec((BLOCK, 2 * KV_W), lambda i: (jnp.maximum(i - 1, 0), 0)), _const((BLOCK, 2 * KV_W))] + [ANY] * n,
        out_specs=[_row(BLOCK, ATTN_W), _row(BLOCK, BLOCK)] + [ANY] * n,
        out_shape=[jax.ShapeDtypeStruct((r, ATTN_W), F32), jax.ShapeDtypeStruct((r, BLOCK), F32)]
        + [jax.ShapeDtypeStruct(g.shape, g.dtype) for g in gathered],
        input_output_aliases={5 + k: 2 + k for k in range(n)},
        scratch_shapes=[pltpu.VMEM((N_KV, GROUP * BLOCK, 2 * BLOCK), F32), pltpu.VMEM((GROUP * BLOCK, BLOCK), F32),
                        pltpu.SemaphoreType.DMA((3 * n,)), pltpu.SemaphoreType.DMA((3 * n,))],
        compiler_params=_cparams(),
    )(sinks, q, kv, kv, kv, *gathered)


def _shifted_copies(ub_ref, win):
    w = win.shape[0]
    ub_ref[0] = win
    for b in range(1, 8):
        ub_ref[b] = pltpu.roll(win, shift=w - b, axis=0)


def _conv_chunk(ub_ref, w_ref, r0, shifts):
    acc = jnp.zeros((CONV_CHUNK, CONV_W), F32)
    for j in range(CONV_K):
        a, b = divmod(shifts[j], 8)
        acc = acc + w_ref[j:j + 1, :] * ub_ref[b, pl.ds(r0 + 8 * a, CONV_CHUNK), :]
    return acc


FWD_SHIFTS = tuple(HALO - (CONV_K - 1) + j for j in range(CONV_K))
BWD_SHIFTS = tuple(CONV_K - 1 - j for j in range(CONV_K))


def _glu_window(cp_ref, c_ref, i):
    tile = c_ref[:, 0:CONV_W] * _sigmoid(c_ref[:, CONV_W:2 * CONV_W])
    halo = cp_ref[:, 0:CONV_W] * _sigmoid(cp_ref[:, CONV_W:2 * CONV_W])
    first = (jnp.zeros((HALO, CONV_W), jnp.int32) + i) == 0
    return jnp.concatenate([jnp.where(first, 0.0, halo), tile], axis=0)


def _halo_before(tm, n):
    return pl.BlockSpec((HALO, n), lambda i: (jnp.maximum(i * (tm // HALO) - 1, 0), 0))


def _conv_fwd(cacg, cw, cb, lg, lb, tm, gathered):
    r = cacg.shape[0]
    n = len(gathered)

    def body(c_ref, cp_ref, w_ref, cb_ref, lg_ref, lb_ref, *rest):
        o_ref, y_ref = rest[n:n + 2]
        dst = rest[n + 2:2 * n + 2]
        ub_ref, send_sems, recv_sems = rest[2 * n + 2:]
        i = pl.program_id(0)

        @pl.when(i == 0)
        def _():
            for cp in _gather_d2d(dst, send_sems, recv_sems)[0]:
                cp.start()

        _shifted_copies(ub_ref, _glu_window(cp_ref, c_ref, i))

        def chunk(ci, carry):
            r0 = pl.multiple_of(ci * CONV_CHUNK, CONV_CHUNK)
            y = _conv_chunk(ub_ref, w_ref, r0, FWD_SHIFTS) + cb_ref[...]
            yc = y - jnp.mean(y, axis=-1, keepdims=True)
            rs = lax.rsqrt(jnp.mean(yc * yc, axis=-1, keepdims=True) + NORM_EPS)
            yn = yc * rs * lg_ref[...] + lb_ref[...]
            o_ref[pl.ds(r0, CONV_CHUNK), :] = yn * _sigmoid(yn)
            y_ref[pl.ds(r0, CONV_CHUNK), :] = y
            return carry

        lax.fori_loop(0, tm // CONV_CHUNK, chunk, 0, unroll=6)

        @pl.when(i == r // tm - 1)
        def _():
            sends, arrivals = _gather_d2d(dst, send_sems, recv_sems)
            for cp in arrivals:
                cp.wait_recv()
            for cp in sends:
                cp.wait_send()

    return pl.pallas_call(
        body, name="conv_fwd", grid=(r // tm,),
        in_specs=[_row(tm, 2 * CONV_W), _halo_before(tm, 2 * CONV_W), _const((32, CONV_W)), _const((1, CONV_W)),
                  _const((1, CONV_W)), _const((1, CONV_W))] + [ANY] * n,
        out_specs=[_row(tm, CONV_W), _row(tm, CONV_W)] + [ANY] * n,
        out_shape=[jax.ShapeDtypeStruct((r, CONV_W), F32)] * 2 + [jax.ShapeDtypeStruct(g.shape, g.dtype) for g in gathered],
        input_output_aliases={6 + k: 2 + k for k in range(n)},
        scratch_shapes=[pltpu.VMEM((8, tm + HALO, CONV_W), F32), pltpu.SemaphoreType.DMA((3 * n,)), pltpu.SemaphoreType.DMA((3 * n,))],
        compiler_params=_cparams(),
    )(cacg, cacg, cw, cb, lg, lb, *gathered)


def _out_proj(oa, oc, h0, ga, gc, g2, w_out_b, tm):
    r = h0.shape[0]

    def body(oa_ref, oc_ref, h_ref, ga_ref, gc_ref, g2_ref, w_ref, h1_ref, hn2_ref):
        ma = _rms_fwd(oa_ref[...], ga_ref[...])[0].astype(BF16)
        mc = _rms_fwd(oc_ref[...], gc_ref[...])[0].astype(BF16)
        h1 = h_ref[...] + _dot(jnp.concatenate([ma, mc], axis=1), w_ref[...])
        h1_ref[...] = h1
        hn2_ref[...] = _rms_fwd(h1, g2_ref[...])[0].astype(BF16)

    return pl.pallas_call(
        body, name="out_proj", grid=(r // tm,),
        in_specs=[_row(tm, ATTN_W), _row(tm, CONV_W), _row(tm, D_MODEL), _const((1, ATTN_W)), _const((1, CONV_W)),
                  _const((1, D_MODEL)), _const((D_MODEL, D_MODEL))],
        out_specs=[_row(tm, D_MODEL), _row(tm, D_MODEL)],
        out_shape=[jax.ShapeDtypeStruct((r, D_MODEL), F32), jax.ShapeDtypeStruct((r, D_MODEL), BF16)],
        compiler_params=_cparams(),
    )(oa, oc, h0, ga, gc, g2, w_out_b)


def _ffn_fwd(hn2, h1, target, gf, wg_t, wu_t, wd_b, tm):
    r = h1.shape[0]
    seq = target.shape[0]
    n_sub = tm // BLOCK

    def body(hn_ref, h1_ref, *rest):
        t_refs = rest[:n_sub]
        gf_ref, wg_ref, wu_ref, wd_ref, gate_ref, up_ref, act_ref, dh2_ref, dh2b_ref, loss_ref, dgf_ref = rest[n_sub:]
        i = pl.program_id(0)

        @pl.when(i == 0)
        def _():
            loss_ref[...] = jnp.zeros_like(loss_ref)
            dgf_ref[...] = jnp.zeros_like(dgf_ref)

        hn = hn_ref[...]
        for cs in FF_CHUNKS:
            gate = _dot_nt(hn, wg_ref[cs, :])
            up = _dot_nt(hn, wu_ref[cs, :])
            gate_ref[:, cs] = gate.astype(BF16)
            up_ref[:, cs] = up.astype(BF16)
            act_ref[:, cs] = (gate * _sigmoid(gate) * up).astype(BF16)
        y, xhat, rstd = _rms_fwd(h1_ref[...] + _dot(act_ref[...], wd_ref[...]), gf_ref[...])
        rows = lax.broadcasted_iota(jnp.int32, (tm, D_MODEL), 0) + i * tm
        real = (rows >= BLOCK) & (rows < BLOCK + seq)
        err = jnp.where(real, y - jnp.concatenate([t[...] for t in t_refs], axis=0), 0.0)
        loss_ref[...] += jnp.sum(err * err) * (0.5 / D_MODEL)
        dy = err * (1.0 / D_MODEL)
        dh2, dg_rows = _rms_bwd(dy, xhat, rstd, gf_ref[...])
        dgf_ref[...] += jnp.sum(dg_rows, axis=0, keepdims=True)
        dh2_ref[...] = dh2
        dh2b_ref[...] = dh2.astype(BF16)

    def target_block(k):
        return pl.BlockSpec((BLOCK, D_MODEL), lambda i: (jnp.clip(n_sub * i - 1 + k, 0, seq // BLOCK - 1), 0))

    return pl.pallas_call(
        body, name="ffn_fwd", grid=(r // tm,),
        in_specs=[_row(tm, D_MODEL), _row(tm, D_MODEL)] + [target_block(k) for k in range(n_sub)]
        + [_const((1, D_MODEL))] + [_resident((D_FF, D_MODEL))] * 3,
        out_specs=[_row(tm, D_FF)] * 3 + [_row(tm, D_MODEL), _row(tm, D_MODEL), _const((1, BLOCK)), _const((1, D_MODEL))],
        out_shape=[jax.ShapeDtypeStruct((r, D_FF), BF16)] * 3
        + [jax.ShapeDtypeStruct((r, D_MODEL), F32), jax.ShapeDtypeStruct((r, D_MODEL), BF16),
           jax.ShapeDtypeStruct((1, BLOCK), F32), jax.ShapeDtypeStruct((1, D_MODEL), F32)],
        compiler_params=_cparams(),
    )(hn2, h1, *[target] * n_sub, gf, wg_t, wu_t, wd_b)


def _ffn_bwd(dh2, dh2b, gate, up, h1, g2, wg_t, wu_t, wd_b, tm):
    r = h1.shape[0]

    def body(dh2_ref, dh2b_ref, gate_ref, up_ref, h1_ref, g2_ref, wg_ref, wu_ref, wd_ref, dgate_ref, dup_ref, dh1_ref, dg2_ref):
        @pl.when(pl.program_id(0) == 0)
        def _():
            dg2_ref[...] = jnp.zeros_like(dg2_ref)

        dyb = dh2b_ref[...]
        for cs in FF_CHUNKS:
            dact = _dot_nt(dyb, wd_ref[cs, :])
            gate = gate_ref[:, cs].astype(F32)
            up = up_ref[:, cs].astype(F32)
            sg = _sigmoid(gate)
            dgate_ref[:, cs] = (dact * up * (sg * (1.0 + gate * (1.0 - sg)))).astype(BF16)
            dup_ref[:, cs] = (dact * (gate * sg)).astype(BF16)
        dhn = _dot(dgate_ref[...], wg_ref[...]) + _dot(dup_ref[...], wu_ref[...])
        _, xhat, rstd = _rms_fwd(h1_ref[...], g2_ref[...])
        dx, dg_rows = _rms_bwd(dhn, xhat, rstd, g2_ref[...])
        dg2_ref[...] += jnp.sum(dg_rows, axis=0, keepdims=True)
        dh1_ref[...] = dh2_ref[...] + dx

    return pl.pallas_call(
        body, name="ffn_bwd", grid=(r // tm,),
        in_specs=[_row(tm, D_MODEL), _row(tm, D_MODEL), _row(tm, D_FF), _row(tm, D_FF), _row(tm, D_MODEL), _const((1, D_MODEL))]
        + [_resident((D_FF, D_MODEL))] * 3,
        out_specs=[_row(tm, D_FF), _row(tm, D_FF), _row(tm, D_MODEL), _const((1, D_MODEL))],
        out_shape=[jax.ShapeDtypeStruct((r, D_FF), BF16), jax.ShapeDtypeStruct((r, D_FF), BF16),
                   jax.ShapeDtypeStruct((r, D_MODEL), F32), jax.ShapeDtypeStruct((1, D_MODEL), F32)],
        compiler_params=_cparams(),
    )(dh2, dh2b, gate, up, h1, g2, wg_t, wu_t, wd_b)


FF_HALF = D_FF // 2


def _ffn_wgrad_gu(hn2, dgate, dup, tk):
    r = hn2.shape[0]
    n_k = r // tk

    def body(hn_ref, dg_ref, du_ref, wg_ref, wu_ref, accg_ref, accu_ref):
        k = pl.program_id(1)

        @pl.when(k == 0)
        def _():
            accg_ref[...] = jnp.zeros_like(accg_ref)
            accu_ref[...] = jnp.zeros_like(accu_ref)

        hn = hn_ref[...]
        accg_ref[...] += _dot_tn(dg_ref[...], hn)
        accu_ref[...] += _dot_tn(du_ref[...], hn)

        @pl.when(k == n_k - 1)
        def _():
            wg_ref[...] = accg_ref[...].astype(BF16)
            wu_ref[...] = accu_ref[...].astype(BF16)

    col = pl.BlockSpec((tk, FF_HALF), lambda j, k: (k, j))
    out = pl.BlockSpec((FF_HALF, D_MODEL), lambda j, k: (j, 0))
    return pl.pallas_call(
        body, name="ffn_wgrad_gu", grid=(2, n_k),
        in_specs=[pl.BlockSpec((tk, D_MODEL), lambda j, k: (k, 0)), col, col],
        out_specs=[out, out],
        out_shape=[jax.ShapeDtypeStruct((D_FF, D_MODEL), BF16)] * 2,
        scratch_shapes=[pltpu.VMEM((FF_HALF, D_MODEL), F32)] * 2,
        compiler_params=_cparams(2),
    )(hn2, dgate, dup)


def _ffn_wgrad_d(act, dh2b, tk):
    r = act.shape[0]
    n_k = r // tk

    def body(a_ref, dy_ref, wd_ref, acc_ref):
        k = pl.program_id(1)

        @pl.when(k == 0)
        def _():
            acc_ref[...] = jnp.zeros_like(acc_ref)

        acc_ref[...] += _dot_tn(a_ref[...], dy_ref[...])

        @pl.when(k == n_k - 1)
        def _():
            wd_ref[...] = acc_ref[...].astype(BF16)

    return pl.pallas_call(
        body, name="ffn_wgrad_d", grid=(2, n_k),
        in_specs=[pl.BlockSpec((tk, FF_HALF), lambda j, k: (k, j)), pl.BlockSpec((tk, D_MODEL), lambda j, k: (k, 0))],
        out_specs=pl.BlockSpec((FF_HALF, D_MODEL), lambda j, k: (j, 0)),
        out_shape=jax.ShapeDtypeStruct((D_FF, D_MODEL), BF16),
        scratch_shapes=[pltpu.VMEM((FF_HALF, D_MODEL), F32)],
        compiler_params=_cparams(2),
    )(act, dh2b)


def _out_proj_bwd(dh1, oa, oc, ga, gc, w_out_b, tm):
    r = dh1.shape[0]

    def body(dh_ref, oa_ref, oc_ref, ga_ref, gc_ref, w_ref, doa_ref, doc_ref, dw_ref, dga_ref, dgc_ref, acc_ref):
        i = pl.program_id(0)

        @pl.when(i == 0)
        def _():
            acc_ref[...] = jnp.zeros_like(acc_ref)
            dga_ref[...] = jnp.zeros_like(dga_ref)
            dgc_ref[...] = jnp.zeros_like(dgc_ref)

        dhb = dh_ref[...].astype(BF16)
        dmix = _dot_nt(dhb, w_ref[...])
        ma, xa, ra = _rms_fwd(oa_ref[...], ga_ref[...])
        mc, xc, rc = _rms_fwd(oc_ref[...], gc_ref[...])
        acc_ref[...] += _dot_tn(jnp.concatenate([ma.astype(BF16), mc.astype(BF16)], axis=1), dhb)

        @pl.when(i == r // tm - 1)
        def _():
            dw_ref[...] = acc_ref[...].astype(BF16)

        doa, dga_rows = _rms_bwd(dmix[:, 0:ATTN_W], xa, ra, ga_ref[...])
        doc, dgc_rows = _rms_bwd(dmix[:, ATTN_W:ATTN_W + CONV_W], xc, rc, gc_ref[...])
        doa_ref[...] = doa
        doc_ref[...] = doc
        dga_ref[...] += jnp.sum(dga_rows, axis=0, keepdims=True)
        dgc_ref[...] += jnp.sum(dgc_rows, axis=0, keepdims=True)

    return pl.pallas_call(
        body, name="out_proj_bwd", grid=(r // tm,),
        in_specs=[_row(tm, D_MODEL), _row(tm, ATTN_W), _row(tm, CONV_W), _const((1, ATTN_W)), _const((1, CONV_W)),
                  _const((D_MODEL, D_MODEL))],
        out_specs=[_row(tm, ATTN_W), _row(tm, CONV_W), _const((D_MODEL, D_MODEL)), _const((1, ATTN_W)), _const((1, CONV_W))],
        out_shape=[jax.ShapeDtypeStruct((r, ATTN_W), F32), jax.ShapeDtypeStruct((r, CONV_W), F32),
                   jax.ShapeDtypeStruct((D_MODEL, D_MODEL), BF16), jax.ShapeDtypeStruct((1, ATTN_W), F32),
                   jax.ShapeDtypeStruct((1, CONV_W), F32)],
        scratch_shapes=[pltpu.VMEM((D_MODEL, D_MODEL), F32)],
        compiler_params=_cparams(),
    )(dh1, oa, oc, ga, gc, w_out_b)


def _conv_bwd_params(doc, y, cacg, lg, lb, tm, parts):
    r = cacg.shape[0]
    n_steps = r // tm
    n = len(parts)

    def body(do_ref, y_ref, c_ref, cp_ref, lg_ref, lb_ref, *rest):
        src = rest[:n]
        dy_ref, dcw_ref, dcb_ref, dlg_ref, dlb_ref = rest[n:n + 5]
        dst = rest[n + 5:2 * n + 5]
        ub_ref, accw_ref, send_sems, recv_sems = rest[2 * n + 5:]
        i = pl.program_id(0)

        @pl.when(i == 0)
        def _():
            for cp in _scatter(src, dst, send_sems, recv_sems):
                cp.start()
            accw_ref[...] = jnp.zeros_like(accw_ref)
            dcb_ref[...] = jnp.zeros_like(dcb_ref)
            dlg_ref[...] = jnp.zeros_like(dlg_ref)
            dlb_ref[...] = jnp.zeros_like(dlb_ref)

        _shifted_copies(ub_ref, _glu_window(cp_ref, c_ref, i))

        def chunk(ci, carry):
            r0 = pl.multiple_of(ci * CONV_CHUNK, CONV_CHUNK)
            y = y_ref[pl.ds(r0, CONV_CHUNK), :]
            yc = y - jnp.mean(y, axis=-1, keepdims=True)
            rs = lax.rsqrt(jnp.mean(yc * yc, axis=-1, keepdims=True) + NORM_EPS)
            xhat = yc * rs
            yn = xhat * lg_ref[...] + lb_ref[...]
            sg = _sigmoid(yn)
            dyn = do_ref[pl.ds(r0, CONV_CHUNK), :] * (sg * (1.0 + yn * (1.0 - sg)))
            dlg_ref[...] += jnp.sum(dyn * xhat, axis=0, keepdims=True)
            dlb_ref[...] += jnp.sum(dyn, axis=0, keepdims=True)
            dxh = dyn * lg_ref[...]
            dy = rs * (dxh - jnp.mean(dxh, axis=-1, keepdims=True) - xhat * jnp.mean(dxh * xhat, axis=-1, keepdims=True))
            dcb_ref[...] += jnp.sum(dy, axis=0, keepdims=True)
            dy_ref[pl.ds(r0, CONV_CHUNK), :] = dy
            for j in range(CONV_K):
                a, b = divmod(FWD_SHIFTS[j], 8)
                prod = dy * ub_ref[b, pl.ds(r0 + 8 * a, CONV_CHUNK), :]
                accw_ref[j] += jnp.sum(prod.reshape(CONV_CHUNK // 8, 8, CONV_W), axis=0)
            return carry

        lax.fori_loop(0, tm // CONV_CHUNK, chunk, 0, unroll=6)

        @pl.when(i == n_steps - 1)
        def _():
            for j in range(32):
                dcw_ref[j:j + 1, :] = jnp.sum(accw_ref[j], axis=0, keepdims=True)
            for cp in _scatter(src, dst, send_sems, recv_sems):
                cp.wait()

    vec = _const((1, CONV_W))
    return pl.pallas_call(
        body, name="conv_bwd_params", grid=(n_steps,),
        in_specs=[_row(tm, CONV_W), _row(tm, CONV_W), _row(tm, 2 * CONV_W), _halo_before(tm, 2 * CONV_W), vec, vec] + [ANY] * n,
        out_specs=[_row(tm, CONV_W), _const((32, CONV_W)), vec, vec, vec] + [ANY] * n,
        out_shape=[jax.ShapeDtypeStruct((r, CONV_W), F32), jax.ShapeDtypeStruct((32, CONV_W), F32)]
        + [jax.ShapeDtypeStruct((1, CONV_W), F32)] * 3 + _scatter_landing(parts),
        scratch_shapes=[pltpu.VMEM((8, tm + HALO, CONV_W), F32), pltpu.VMEM((32, 8, CONV_W), F32),
                        pltpu.SemaphoreType.DMA((7 * n,)), pltpu.SemaphoreType.DMA((7 * n,))],
        compiler_params=_cparams(),
    )(doc, y, cacg, cacg, lg, lb, *parts)


def _conv_bwd_data(dy, cacg, cw, tm):
    r = cacg.shape[0]
    n_steps = r // tm

    def body(dy_ref, dyn_ref, c_ref, w_ref, dc_ref, ub_ref):
        last = (jnp.zeros((HALO, CONV_W), jnp.int32) + pl.program_id(0)) == n_steps - 1
        win = jnp.concatenate([dy_ref[...], jnp.where(last, 0.0, dyn_ref[...])], axis=0)
        _shifted_copies(ub_ref, win)

        def chunk(ci, carry):
            r0 = pl.multiple_of(ci * CONV_CHUNK, CONV_CHUNK)
            du = _conv_chunk(ub_ref, w_ref, r0, BWD_SHIFTS)
            ca = c_ref[pl.ds(r0, CONV_CHUNK), 0:CONV_W]
            sg = _sigmoid(c_ref[pl.ds(r0, CONV_CHUNK), CONV_W:2 * CONV_W])
            dc_ref[pl.ds(r0, CONV_CHUNK), 0:CONV_W] = (du * sg).astype(BF16)
            dc_ref[pl.ds(r0, CONV_CHUNK), CONV_W:2 * CONV_W] = (du * ca * sg * (1.0 - sg)).astype(BF16)
            return carry

        lax.fori_loop(0, tm // CONV_CHUNK, chunk, 0, unroll=6)

    halo_after = pl.BlockSpec((HALO, CONV_W), lambda i: (jnp.minimum((i + 1) * (tm // HALO), r // HALO - 1), 0))
    return pl.pallas_call(
        body, name="conv_bwd_data", grid=(n_steps,),
        in_specs=[_row(tm, CONV_W), halo_after, _row(tm, 2 * CONV_W), _const((32, CONV_W))],
        out_specs=_row(tm, 2 * CONV_W),
        out_shape=jax.ShapeDtypeStruct((r, 2 * CONV_W), BF16),
        scratch_shapes=[pltpu.VMEM((8, tm + HALO, CONV_W), F32)],
        compiler_params=_cparams(),
    )(dy, dy, cacg, cw)


def _attn_bwd(q, kv, o, do, lse, sinks, parts):
    r = q.shape[0]
    nb = r // BLOCK
    n = len(parts)

    def body(sink_ref, q_ref, kvc_ref, kvp_ref, kvm_ref, o_ref, do_ref, lse_ref, *rest):
        src = rest[:n]
        dq_ref, dkv_ref, dmeta_ref, dsink_ref = rest[n:n + 4]
        dst = rest[n + 4:2 * n + 4]
        hold_ref, bias_ref, late_ref, send_sems, recv_sems = rest[2 * n + 4:]
        i = pl.program_id(0)

        @pl.when(i == 0)
        def _():
            for cp in _scatter(src, dst, send_sems, recv_sems):
                cp.start()
            _attn_bias_init(bias_ref, late_ref)
            dmeta_ref[...] = jnp.zeros_like(dmeta_ref)
            dsink_ref[...] = jnp.zeros_like(dsink_ref)
            hold_ref[...] = jnp.zeros_like(hold_ref)

        @pl.when(i < nb)
        def _():
            lane = lax.broadcasted_iota(jnp.int32, (BLOCK, BLOCK), 1)
            lse_tile = lse_ref[...]
            zero = jnp.zeros((BLOCK, BLOCK), F32)
            block_bias = _attn_block_bias(late_ref, i)
            prev_part = late_ref[...] > 0.5
            for g in range(N_KV):
                kcat, vcat = _kv_cat(kvm_ref, kvp_ref, kvc_ref, g)
                heads = range(g * GROUP, (g + 1) * GROUP)
                qs = _stack_heads(q_ref, g)
                dos = _stack_heads(do_ref, g)
                dosb = dos.astype(BF16)
                lse = jnp.concatenate(
                    [jnp.sum(jnp.where(lane == h, lse_tile, 0.0), axis=-1, keepdims=True) + zero for h in heads], axis=0)
                delta = jnp.sum(dos * _stack_heads(o_ref, g), axis=-1, keepdims=True) + jnp.zeros((GROUP * BLOCK, BLOCK), F32)
                band_bias = bias_ref[g, :, BLOCK:2 * BLOCK] + block_bias[1]
                bias = [bias_ref[g, :, 0:BLOCK] + block_bias[0], jnp.where(prev_part, band_bias, NEG), jnp.where(prev_part, NEG, band_bias)]
                s = _dot_nt(qs, kcat)
                dp = _dot_nt(dosb, vcat)
                ps = [jnp.exp(s[:, k * BLOCK:(k + 1) * BLOCK] * SCALE + bias[k] - lse) for k in range(3)]
                p = jnp.concatenate(ps, axis=1)
                ds = jnp.concatenate(
                    [(ps[k] * (dp[:, k * BLOCK:(k + 1) * BLOCK] - delta)) * SCALE for k in range(3)], axis=1).astype(BF16)
                sink_term = jnp.exp(_head_rows([sink_ref[h] for h in heads]) - lse)[:, 0:1] * delta[:, 0:1]
                dq = _dot(ds, kcat).astype(BF16)
                for j, h in enumerate(heads):
                    dsink_ref[h:h + 1, :] += -jnp.sum(sink_term[j * BLOCK:(j + 1) * BLOCK])
                    dq_ref[:, h * HEAD_DIM:(h + 1) * HEAD_DIM] = dq[j * BLOCK:(j + 1) * BLOCK]
                dk_t = _dot_tn(qs, ds)
                dv_t = _dot_tn(dosb, p.astype(BF16))
                ks = slice(g * HEAD_DIM, (g + 1) * HEAD_DIM)
                vs = slice(KV_W + g * HEAD_DIM, KV_W + (g + 1) * HEAD_DIM)
                for sl, grad_t in ((ks, dk_t), (vs, dv_t)):
                    dmeta_ref[:, sl] += grad_t[:, 0:BLOCK].T
                    dkv_ref[:, sl] = hold_ref[:, sl] + grad_t[:, BLOCK:2 * BLOCK].T
                    hold_ref[:, sl] = grad_t[:, 2 * BLOCK:3 * BLOCK].T

        @pl.when(i == nb)
        def _():
            dkv_ref[...] = hold_ref[...]
            for cp in _scatter(src, dst, send_sems, recv_sems):
                cp.wait()

    def cur(i):
        return jnp.minimum(i, nb - 1)

    return pl.pallas_call(
        body, name="attn_bwd", grid=(nb + 1,),
        in_specs=[pl.BlockSpec(memory_space=pltpu.SMEM),
                  pl.BlockSpec((BLOCK, ATTN_W), lambda i: (cur(i), 0)),
                  pl.BlockSpec((BLOCK, 2 * KV_W), lambda i: (cur(i), 0)),
                  pl.BlockSpec((BLOCK, 2 * KV_W), lambda i: (jnp.maximum(cur(i) - 1, 0), 0)),
                  _const((BLOCK, 2 * KV_W)),
                  pl.BlockSpec((BLOCK, ATTN_W), lambda i: (cur(i), 0)),
                  pl.BlockSpec((BLOCK, ATTN_W), lambda i: (cur(i), 0)),
                  pl.BlockSpec((BLOCK, BLOCK), lambda i: (cur(i), 0))] + [ANY] * n,
        out_specs=[pl.BlockSpec((BLOCK, ATTN_W), lambda i: (cur(i), 0)),
                   pl.BlockSpec((BLOCK, 2 * KV_W), lambda i: (jnp.maximum(i - 1, 0), 0)),
                   _const((BLOCK, 2 * KV_W)), _const((N_HEADS, BLOCK))] + [ANY] * n,
        out_shape=[jax.ShapeDtypeStruct((r, ATTN_W), BF16), jax.ShapeDtypeStruct((r, 2 * KV_W), F32),
                   jax.ShapeDtypeStruct((BLOCK, 2 * KV_W), F32), jax.ShapeDtypeStruct((N_HEADS, BLOCK), F32)] + _scatter_landing(parts),
        scratch_shapes=[pltpu.VMEM((BLOCK, 2 * KV_W), F32), pltpu.VMEM((N_KV, GROUP * BLOCK, 2 * BLOCK), F32),
                        pltpu.VMEM((GROUP * BLOCK, BLOCK), F32), pltpu.SemaphoreType.DMA((7 * n,)), pltpu.SemaphoreType.DMA((7 * n,))],
        compiler_params=_cparams(),
    )(sinks, q, kv, kv, kv, o, do, lse, *parts)


def _in_proj_bwd(dq, dkv, dkv_meta, dc, dh1, h0, g1, w_in_t, seq, tm):
    r = h0.shape[0]
    n_tiles = r // tm
    n_out = -(-seq // tm)

    def body(dq_ref, dkv_ref, dm_ref, dc_ref, dh1_ref, h_ref, g_ref, w_ref, gx_ref, lead_ref, dwt_ref, dg_ref, dw_ref, hold_ref):
        i = pl.program_id(0)

        @pl.when(i == 0)
        def _():
            dw_ref[...] = jnp.zeros_like(dw_ref)
            dg_ref[...] = jnp.zeros_like(dg_ref)

        @pl.when(i < n_tiles)
        def _():
            meta = jnp.concatenate([dm_ref[...], jnp.zeros((tm - BLOCK, 2 * KV_W), F32)], axis=0) if tm > BLOCK else dm_ref[...]
            first = (jnp.zeros((tm, 2 * KV_W), jnp.int32) + i) == 0
            dkvb = (dkv_ref[...] + jnp.where(first, meta, 0.0)).astype(BF16)
            hn, xhat, rstd = _rms_fwd(h_ref[...], g_ref[...])
            dproj = jnp.concatenate([dq_ref[...], dkvb, dc_ref[...]], axis=1)
            dhn = _dot(dproj, w_ref[...])
            dw_ref[...] += _dot_tn(dproj, hn.astype(BF16))
            dx, dg_rows = _rms_bwd(dhn, xhat, rstd, g_ref[...])
            dg_ref[...] += jnp.sum(dg_rows, axis=0, keepdims=True)
            dh0 = dh1_ref[...] + dx

            @pl.when(i == 0)
            def _():
                lead_ref[...] = dh0[0:BLOCK]

            @pl.when((i >= 1) & (i <= n_out))
            def _():
                gx_ref[0:tm - BLOCK, :] = hold_ref[...]
                gx_ref[tm - BLOCK:tm, :] = dh0[0:BLOCK]

            hold_ref[...] = dh0[BLOCK:tm]

        @pl.when((i == n_tiles) & (n_tiles <= n_out))
        def _():
            gx_ref[0:tm - BLOCK, :] = hold_ref[...]

        @pl.when(i == n_tiles - 1)
        def _():
            dwt_ref[...] = dw_ref[...].astype(BF16)

    def tile(n):
        return pl.BlockSpec((tm, n), lambda i: (jnp.minimum(i, n_tiles - 1), 0))

    return pl.pallas_call(
        body, name="in_proj_bwd", grid=(n_tiles + 1,),
        in_specs=[tile(ATTN_W), tile(2 * KV_W), _const((BLOCK, 2 * KV_W)), tile(2 * CONV_W), tile(D_MODEL), tile(D_MODEL),
                  _const((1, D_MODEL)), _const((IN_COLS, D_MODEL))],
        out_specs=[pl.BlockSpec((tm, D_MODEL), lambda i: (jnp.clip(i - 1, 0, n_out - 1), 0)), _const((BLOCK, D_MODEL)),
                   _const((IN_COLS, D_MODEL)), _const((1, D_MODEL))],
        out_shape=[jax.ShapeDtypeStruct((seq, D_MODEL), F32), jax.ShapeDtypeStruct((BLOCK, D_MODEL), F32),
                   jax.ShapeDtypeStruct((IN_COLS, D_MODEL), BF16), jax.ShapeDtypeStruct((1, D_MODEL), F32)],
        scratch_shapes=[pltpu.VMEM((IN_COLS, D_MODEL), F32), pltpu.VMEM((tm - BLOCK, D_MODEL), F32)],
        compiler_params=_cparams(),
    )(dq, dkv, dkv_meta, dc, dh1, h0, g1, w_in_t)


def _adamw_update(w_ref, g_ref, m_ref, v_ref, d_ref, nm_ref, nv_ref):
    g = g_ref[...]
    m = ADAM_B1 * m_ref[...] + (1.0 - ADAM_B1) * g
    v = ADAM_B2 * v_ref[...] + (1.0 - ADAM_B2) * (g * g)
    m_hat = m / (1.0 - ADAM_B1 ** ADAM_STEP)
    v_hat = v / (1.0 - ADAM_B2 ** ADAM_STEP)
    d_ref[...] = -ADAM_LR * (m_hat / (jnp.sqrt(v_hat) + ADAM_EPS) + ADAM_WD * w_ref[...])
    nm_ref[...] = m
    nv_ref[...] = v


def _adamw(w, g, m, v, name):
    rows, cols = w.shape
    tr = rows
    for cand in (256, 176, 128, 64, 32, 16, 8):
        if rows % cand == 0:
            tr = cand
            break

    def body(*refs):
        _adamw_update(*refs)

    spec = _row(tr, cols)
    return pl.pallas_call(
        body, name=name, grid=(rows // tr,), in_specs=[spec] * 4, out_specs=[spec] * 3,
        out_shape=[jax.ShapeDtypeStruct((rows, cols), F32)] * 3, compiler_params=_cparams(),
    )(w, g, m, v)


MESH = pl.DeviceIdType.MESH
ANY = pl.BlockSpec(memory_space=pl.ANY)


def _place():
    x, y, c = lax.axis_index("x"), lax.axis_index("y"), lax.axis_index("c")
    chips = [(1 - x, y), (x, 1 - y), (1 - x, 1 - y)]
    return x, y, c, chips


def _gather_ici(dst, send_sems, recv_sems):
    x, y, c, chips = _place()
    sends, arrivals = [], []
    for k in range(len(dst)):
        rows = dst[k].shape[1] // 2
        half = pl.ds(c * rows, rows)
        mine = dst[k].at[2 * x + y, half]
        for p, chip in enumerate(chips):
            sems = dict(send_sem=send_sems.at[3 * k + p], recv_sem=recv_sems.at[3 * k + p], device_id=(chip[0], chip[1], c),
                        device_id_type=MESH)
            sends.append(pltpu.make_async_remote_copy(src_ref=mine, dst_ref=mine, **sems))
            theirs = dst[k].at[2 * chip[0] + chip[1], half]
            arrivals.append(pltpu.make_async_remote_copy(src_ref=theirs, dst_ref=theirs, **sems))
    return sends, arrivals


def _gather_d2d(dst, send_sems, recv_sems):
    x, y, c, chips = _place()
    sends, arrivals = [], []
    for k in range(len(dst)):
        rows = dst[k].shape[1] // 2
        for p, chip in enumerate(chips):
            sems = dict(send_sem=send_sems.at[3 * k + p], recv_sem=recv_sems.at[3 * k + p], device_id=(x, y, 1 - c),
                        device_id_type=MESH)
            mine = dst[k].at[2 * chip[0] + chip[1], pl.ds(c * rows, rows)]
            sends.append(pltpu.make_async_remote_copy(src_ref=mine, dst_ref=mine, **sems))
            theirs = dst[k].at[2 * chip[0] + chip[1], pl.ds((1 - c) * rows, rows)]
            arrivals.append(pltpu.make_async_remote_copy(src_ref=theirs, dst_ref=theirs, **sems))
    return sends, arrivals


def _own_slots(shard, dtype):
    return jnp.broadcast_to(shard[None], (N_SHARD,) + shard.shape).astype(dtype)


def _gather_weights(slots):
    n = len(slots)

    def body(*refs):
        dst = refs[n:2 * n]
        ici_send, ici_recv, d2d_send, d2d_recv = refs[2 * n:]
        sends, arrivals = _gather_ici(dst, ici_send, ici_recv)
        for cp in sends:
            cp.start()
        for cp in arrivals:
            cp.wait_recv()
        forwards, from_sibling = _gather_d2d(dst, d2d_send, d2d_recv)
        for cp in forwards:
            cp.start()
        for cp in from_sibling:
            cp.wait_recv()
        for cp in sends + forwards:
            cp.wait_send()

    return pl.pallas_call(
        body, name="gather_weights",
        in_specs=[ANY] * n, out_specs=[ANY] * n,
        out_shape=[jax.ShapeDtypeStruct(s.shape, s.dtype) for s in slots],
        input_output_aliases={k: k for k in range(n)},
        scratch_shapes=[pltpu.SemaphoreType.DMA((3 * n,))] * 4,
    )(*slots)


VMEM_WHOLE = pl.BlockSpec(memory_space=pltpu.VMEM)


SMALL_ROWS = 48
SMALL_PLACES = ((0, (0, 1), 0, 0), (1, (0, 1), 1, 0), (2, (0, 1), 2, 0), (3, (0, 16), 8, 0), (4, (0, 1), 3, 0), (5, (0, 1), 3, 512),
                (6, (0, 1), 4, 0), (7, (0, 1), 4, 512), (8, (0, 1), 5, 0), (9, (0, 16), 24, 0), (9, (16, 32), 24, 512),
                (10, (0, 1), 5, 512), (11, (0, 8), 40, 0))


def _pack_small(parts):
    n = len(parts)

    def body(*refs):
        out = refs[n]
        out[...] = jnp.zeros_like(out)
        for k, (lo, hi), r0, c0 in SMALL_PLACES:
            out[r0:r0 + hi - lo, c0:c0 + parts[k].shape[1]] = refs[k][lo:hi, :]

    return pl.pallas_call(body, name="pack_small", in_specs=[VMEM_WHOLE] * n, out_specs=VMEM_WHOLE,
                          out_shape=jax.ShapeDtypeStruct((SMALL_ROWS, D_MODEL), F32))(*parts)


def _sum_small(own, landed, parts):
    n = len(parts)

    def body(own_ref, landed_ref, *outs):
        x, y, c = lax.axis_index("x"), lax.axis_index("y"), lax.axis_index("c")
        total = jnp.zeros((SMALL_ROWS, D_MODEL), F32)
        for d in range(N_DEV):
            j = 4 * ((x + (d >> 2)) % 2) + 2 * ((y + ((d >> 1) & 1)) % 2) + (c + (d & 1)) % 2
            mine = (jnp.zeros((SMALL_ROWS, D_MODEL), jnp.int32) + j) == 0
            total = total + jnp.where(mine, own_ref[...], landed_ref[jnp.maximum(j - 1, 0)])
        for k, (lo, hi), r0, c0 in SMALL_PLACES:
            outs[k][lo:hi, :] = total[r0:r0 + hi - lo, c0:c0 + parts[k].shape[1]]

    return pl.pallas_call(body, name="sum_small", in_specs=[VMEM_WHOLE] * 2, out_specs=[VMEM_WHOLE] * n,
                          out_shape=[jax.ShapeDtypeStruct(p.shape, F32) for p in parts])(own, landed)


def _adamw_small(ws, gs, ms, vs):
    n = len(ws)

    def body(*refs):
        for k in range(n):
            w_ref, g_ref, m_ref, v_ref = (refs[j * n + k] for j in range(4))
            _adamw_update(w_ref, g_ref, m_ref, v_ref, *(refs[(4 + j) * n + k] for j in range(3)))

    shapes = [jax.ShapeDtypeStruct(w.shape, F32) for w in ws]
    out = pl.pallas_call(
        body, name="adamw_small", in_specs=[VMEM_WHOLE] * (4 * n), out_specs=[VMEM_WHOLE] * (3 * n), out_shape=shapes * 3,
    )(*ws, *gs, *ms, *vs)
    return out[:n], out[n:2 * n], out[2 * n:]


def _scatter(src, dst, send_sems, recv_sems, whole=False):
    x, y, c = lax.axis_index("x"), lax.axis_index("y"), lax.axis_index("c")
    copies = []
    for k in range(len(src)):
        for j in range(1, N_DEV):
            px, py, pc = (x + (j >> 2)) % 2, (y + ((j >> 1) & 1)) % 2, (c + (j & 1)) % 2
            rows = src[k].shape[1] // 2
            piece = src[k] if whole else src[k].at[2 * px + py, pl.ds(pc * rows, rows)]
            copies.append(pltpu.make_async_remote_copy(
                src_ref=piece, dst_ref=dst[k].at[j - 1],
                send_sem=send_sems.at[7 * k + j - 1], recv_sem=recv_sems.at[7 * k + j - 1], device_id=(px, py, pc),
                device_id_type=MESH))
    return copies


def _scatter_landing(parts, whole=False):
    return [jax.ShapeDtypeStruct((N_DEV - 1,) + (p.shape if whole else (p.shape[1] // 2, p.shape[2])), p.dtype) for p in parts]


HBM = pl.BlockSpec(memory_space=pltpu.HBM)
SEMAPHORES = pl.BlockSpec(memory_space=pltpu.SEMAPHORE)
SPLIT_COPY = pltpu.CompilerParams(has_side_effects=pltpu.SideEffectType.DATAFLOW_SIDE_EFFECTING)


def _scatter_start(part, whole, name):
    landing, = _scatter_landing([part], whole)

    def body(src_ref, land_ref, send_sems, recv_sems, src_thru, land_thru, token_ref):
        for cp in _scatter([src_ref], [land_ref], send_sems, recv_sems, whole):
            cp.start()
        token_ref[...] = jnp.zeros_like(token_ref)

    return pl.pallas_call(
        body, name=name,
        out_shape=(pltpu.SemaphoreType.DMA((N_DEV - 1,)), pltpu.SemaphoreType.DMA((N_DEV - 1,)), pltpu.HBM(part.shape, part.dtype),
                   pltpu.HBM(landing.shape, landing.dtype), jax.ShapeDtypeStruct((8, 128), F32)),
        in_specs=(HBM, HBM), out_specs=(SEMAPHORES, SEMAPHORES, HBM, HBM, VMEM_WHOLE), input_output_aliases={0: 2, 1: 3},
        compiler_params=SPLIT_COPY,
    )(pltpu.with_memory_space_constraint(part, pltpu.HBM),
      pltpu.with_memory_space_constraint(lax.empty(landing.shape, landing.dtype), pltpu.HBM))


def _scatter_wait(send_sems, recv_sems, part_thru, land_thru, after, whole, name):
    def body(src_ref, land_ref, send_ref, recv_ref, after_ref, src_dead, got_ref):
        for cp in _scatter([src_ref], [land_ref], send_ref, recv_ref, whole):
            cp.wait_send()
            cp.wait_recv()

    return pl.pallas_call(
        body, name=name,
        out_shape=(pltpu.HBM(part_thru.shape, part_thru.dtype), pltpu.HBM(land_thru.shape, land_thru.dtype)),
        in_specs=(HBM, HBM, SEMAPHORES, SEMAPHORES, ANY), out_specs=(HBM, HBM), input_output_aliases={0: 0, 1: 1},
        compiler_params=SPLIT_COPY,
    )(part_thru, land_thru, send_sems, recv_sems, after)


def _sum_pieces(own, landed, behind, name):
    n = len(own)

    def body(*refs):
        for k in range(n):
            got = refs[n + k]
            total = refs[k][...].astype(F32)
            for j in range(N_DEV - 1):
                total = total + got[j].astype(F32)
            refs[2 * n + 1 + k][...] = total

    in_specs, out_specs = [], []
    for o in own:
        in_specs.append(_row(o.shape[0] // 2, o.shape[1]))
    for o in own:
        in_specs.append(pl.BlockSpec((N_DEV - 1, o.shape[0] // 2, o.shape[1]), lambda i: (0, i, 0)))
        out_specs.append(_row(o.shape[0] // 2, o.shape[1]))
    return pl.pallas_call(
        body, name=name, grid=(2,), in_specs=in_specs + [ANY], out_specs=out_specs,
        out_shape=[jax.ShapeDtypeStruct(o.shape, F32) for o in own], compiler_params=_cparams(),
    )(*own, *landed, behind)


def _swap_with_sibling(halves, name):
    n = len(halves)

    def body(*refs):
        x, y, c = lax.axis_index("x"), lax.axis_index("y"), lax.axis_index("c")
        copies = [pltpu.make_async_remote_copy(
            src_ref=refs[k], dst_ref=refs[n + k], send_sem=refs[2 * n].at[k], recv_sem=refs[2 * n + 1].at[k],
            device_id=(x, y, 1 - c), device_id_type=MESH) for k in range(n)]
        for cp in copies:
            cp.start()
        for cp in copies:
            cp.wait()

    return pl.pallas_call(
        body, name=name, in_specs=[ANY] * n, out_specs=[ANY] * n,
        out_shape=[jax.ShapeDtypeStruct(h.shape, h.dtype) for h in halves],
        scratch_shapes=[pltpu.SemaphoreType.DMA((n,)), pltpu.SemaphoreType.DMA((n,))],
    )(*halves)


def _own_piece(part):
    rows = part.shape[1] // 2
    s = 2 * lax.axis_index("x") + lax.axis_index("y")
    return lax.dynamic_slice(part, (s, lax.axis_index("c") * rows, 0), (1, rows, part.shape[2]))[0]


def _both_halves(mine, theirs):
    south = lax.axis_index("c") == 0
    return jnp.concatenate([jnp.where(south, mine, theirs), jnp.where(south, theirs, mine)], axis=0)


def _from_col_shards(g):
    return g.transpose(1, 0, 2).reshape(g.shape[1], -1)


def kernel(x, meta_tokens, attn_norm_g, w_in, attn_sinks, conv_w, conv_b, conv_ln_g, conv_ln_b, attn_out_g, conv_out_g, w_out, ffn_norm_g, w_gate, w_up, w_down, final_norm_g, loss_target, m_meta_tokens, m_attn_norm_g, m_w_in, m_attn_sinks, m_conv_w, m_conv_b, m_conv_ln_g, m_conv_ln_b, m_attn_out_g, m_conv_out_g, m_w_out, m_ffn_norm_g, m_w_gate, m_w_up, m_w_down, m_final_norm_g, v_meta_tokens, v_attn_norm_g, v_w_in, v_attn_sinks, v_conv_w, v_conv_b, v_conv_ln_g, v_conv_ln_b, v_attn_out_g, v_conv_out_g, v_w_out, v_ffn_norm_g, v_w_gate, v_w_up, v_w_down, v_final_norm_g):
    seq = x.shape[1]
    r = -(-(seq + BLOCK) // ROW_QUANTUM) * ROW_QUANTUM
    tm_wide = 768 if seq >= 768 else 256
    shard = 2 * lax.axis_index("x") + lax.axis_index("y")

    conv_w32 = jnp.pad(conv_w[0], ((0, 1), (0, 0)))
    small_shard = jnp.concatenate([meta_tokens, conv_w32.reshape(16, 256)], axis=0)
    g_in, g_small = _gather_weights([_own_slots(w_in[0].T, BF16), _own_slots(small_shard, F32)])
    later = [_own_slots(w, BF16) for w in (w_gate[0].T, w_up[0].T, w_out[0], w_down[0])]
    w_in_t = g_in.reshape(IN_COLS, D_MODEL)
    meta_full = _from_col_shards(g_small[:, 0:N_META])
    cw_full = _from_col_shards(g_small[:, N_META:].reshape(N_SHARD, 32, 128))

    g1, ga, gc, g2 = attn_norm_g, attn_out_g, conv_out_g, ffn_norm_g
    gf = final_norm_g.reshape(1, D_MODEL)
    sinks = attn_sinks[0]

    lead = jnp.concatenate([jnp.zeros((LEAD, D_MODEL), F32), meta_full], axis=0)
    h0, q, kv, cacg = _in_proj(x[0], lead, g1, w_in_t, r, 768)
    oa, lse, *gathered = _attn_fwd(q, kv, sinks, later)
    oc, yc, g_gate, g_up, g_out, g_down = _conv_fwd(cacg, cw_full, conv_b, conv_ln_g, conv_ln_b, 384, gathered)
    wg_t, wu_t, wd_b = g_gate.reshape(D_FF, D_MODEL), g_up.reshape(D_FF, D_MODEL), g_down.reshape(D_FF, D_MODEL)
    w_out_b = g_out.reshape(D_MODEL, D_MODEL)
    h1, hn2 = _out_proj(oa, oc, h0, ga, gc, g2, w_out_b, 768)
    gate, up, act, dh2, dh2b, loss_p, dgf = _ffn_fwd(hn2, h1, loss_target[0], gf, wg_t, wu_t, wd_b, 384)

    def by_shard(dw):
        return dw.reshape(N_SHARD, dw.shape[0] // N_SHARD, D_MODEL)

    dgate, dup, dh1, dg2 = _ffn_bwd(dh2, dh2b, gate, up, h1, g2, wg_t, wu_t, wd_b, 384)
    p_gate, p_up = [by_shard(dw) for dw in _ffn_wgrad_gu(hn2, dgate, dup, 768)]
    p_down = by_shard(_ffn_wgrad_d(act, dh2b, 768))
    doa, doc, dwo, dga, dgc = _out_proj_bwd(dh1, oa, oc, ga, gc, w_out_b, 768)
    p_out = by_shard(dwo)
    dy, dcw, dcb, dlg, dlb, l_gate, l_up = _conv_bwd_params(doc, yc, cacg, conv_ln_g, conv_ln_b, 384, [p_gate, p_up])
    dc = _conv_bwd_data(dy, cacg, cw_full, 384)
    dq, dkv, dkv_meta, dsink, l_out, l_down = _attn_bwd(q, kv, oa, doa, lse, sinks, [p_out, p_down])
    grad_x, dlead, dwi_t, dg1 = _in_proj_bwd(dq, dkv, dkv_meta, dc, dh1, h0, g1, w_in_t, seq, tm_wide)
    in_send, in_recv, p_in_thru, l_in_thru, started = _scatter_start(by_shard(dwi_t), False, "scatter_start")
    small_parts = [dgf, dg1, dg2, dlead[LEAD:BLOCK], dga, dgc, dcb, dlg, dlb, dcw, loss_p, dsink]
    sm_send, sm_recv, pack_thru, packs_thru, _ = _scatter_start(_pack_small(small_parts), True, "gather_small_start")

    big = ("w_gate", "w_up", "w_out", "w_down", "w_in")
    transposed = ("w_in", "w_gate", "w_up")
    grads = {}
    params = {
        "meta_tokens": (meta_tokens, m_meta_tokens, v_meta_tokens), "attn_norm_g": (attn_norm_g, m_attn_norm_g, v_attn_norm_g),
        "w_in": (w_in, m_w_in, v_w_in), "attn_sinks": (attn_sinks, m_attn_sinks, v_attn_sinks), "conv_w": (conv_w, m_conv_w, v_conv_w),
        "conv_b": (conv_b, m_conv_b, v_conv_b), "conv_ln_g": (conv_ln_g, m_conv_ln_g, v_conv_ln_g),
        "conv_ln_b": (conv_ln_b, m_conv_ln_b, v_conv_ln_b), "attn_out_g": (attn_out_g, m_attn_out_g, v_attn_out_g),
        "conv_out_g": (conv_out_g, m_conv_out_g, v_conv_out_g), "w_out": (w_out, m_w_out, v_w_out),
        "ffn_norm_g": (ffn_norm_g, m_ffn_norm_g, v_ffn_norm_g), "w_gate": (w_gate, m_w_gate, v_w_gate), "w_up": (w_up, m_w_up, v_w_up),
        "w_down": (w_down, m_w_down, v_w_down), "final_norm_g": (final_norm_g, m_final_norm_g, v_final_norm_g)}
    names = list(params)
    delta, new_m, new_v = {}, {}, {}

    def finish(group, parts, landed, behind, tag):
        halves = _sum_pieces([_own_piece(p) for p in parts], landed, behind, "sum_pieces_" + tag)
        for name, mine, theirs in zip(group, halves, _swap_with_sibling(halves, "swap_with_sibling_" + tag)):
            flip = (lambda a: a.T) if name in transposed else (lambda a: a)
            g = _both_halves(mine, theirs)
            w, m, v = params[name]
            outs = _adamw(flip(w[0]), g, flip(m[0]), flip(v[0]), "adamw_" + name)
            grads[name], delta[name], new_m[name], new_v[name] = [flip(a)[None] for a in (g, *outs)]
        return outs[0]

    done = finish(big[:4], [p_gate, p_up, p_out, p_down], [l_gate, l_up, l_out, l_down], started, "ffn_out")

    red_names = ("final_norm_g", "attn_norm_g", "ffn_norm_g", "meta_tokens", "attn_out_g", "conv_out_g", "conv_b", "conv_ln_g",
                 "conv_ln_b", "conv_w", "loss", "attn_sinks")
    pack, packs = _scatter_wait(sm_send, sm_recv, pack_thru, packs_thru, done, True, "gather_small_wait")
    red = dict(zip(red_names, _sum_small(pack, packs, small_parts)))
    loss = red["loss"][0, 0]
    for name in ("attn_norm_g", "conv_b", "conv_ln_g", "conv_ln_b", "attn_out_g", "conv_out_g", "ffn_norm_g"):
        grads[name] = red[name]
    grads["final_norm_g"] = red["final_norm_g"].reshape(D_MODEL)
    grads["attn_sinks"] = red["attn_sinks"][:, 0].reshape(1, N_HEADS)
    grads["meta_tokens"] = lax.dynamic_slice_in_dim(red["meta_tokens"], shard * (D_MODEL // N_SHARD), D_MODEL // N_SHARD, axis=1)
    grads["conv_w"] = lax.dynamic_slice_in_dim(
        red["conv_w"][0:CONV_K], shard * (CONV_W // N_SHARD), CONV_W // N_SHARD, axis=1)[None]

    p_in, l_in = _scatter_wait(in_send, in_recv, p_in_thru, l_in_thru, red["loss"], False, "scatter_wait")
    finish(big[4:], [p_in], [l_in], l_in, "in")
    rest = [name for name in names if name not in big]

    def rows_of(a):
        return a.reshape(-1, a.shape[-1])

    small = _adamw_small([rows_of(params[n][0]) for n in rest], [rows_of(grads[n]) for n in rest],
                         [rows_of(params[n][1]) for n in rest], [rows_of(params[n][2]) for n in rest])
    for dst, outs in zip((delta, new_m, new_v), small):
        for name, out in zip(rest, outs):
            dst[name] = out.reshape(params[name][0].shape)

    return (loss, grad_x[None], *[grads[n] for n in names], *[delta[n] for n in names], *[new_m[n] for n in names],
            *[new_v[n] for n in names])
```

```python
import functools
import math

import jax
import jax.numpy as jnp
from jax import lax
from jax.experimental import pallas as pl
from jax.experimental.pallas import tpu as pltpu

F32 = jnp.float32
BF16 = jnp.bfloat16

D_MODEL = 1024
N_META = 16
ATTN_W = 512
CONV_W = 512
HEAD_DIM = 64
N_HEADS = 8
N_KV = 2
GROUP = N_HEADS // N_KV
KV_W = N_KV * HEAD_DIM
BLOCK = 128
LEAD = BLOCK - N_META
CONV_K = 31
D_FF = 2816
IN_COLS = ATTN_W + 2 * KV_W + 2 * CONV_W
Q0, KV0, C0 = 0, ATTN_W, ATTN_W + 2 * KV_W
NORM_EPS = 1e-5
SCALE = 1.0 / math.sqrt(HEAD_DIM)
SLOPES = tuple(2.0 ** (-(8.0 / N_HEADS) * (h + 1)) for h in range(N_HEADS))
NEG = -1e30

ADAM_LR, ADAM_B1, ADAM_B2, ADAM_EPS, ADAM_WD, ADAM_STEP = 0.001, 0.9, 0.999, 1e-08, 0.01, 10

N_SHARD = 4
N_DEV = 8
ROW_QUANTUM = 768
HALO = 32
CONV_CHUNK = 32
FF_CHUNK = 256
FF_CHUNKS = tuple(slice(c, c + FF_CHUNK) for c in range(0, D_FF, FF_CHUNK))
VMEM_LIMIT = 60 * 1024 * 1024


def _cparams(n_axes=1):
    return pltpu.CompilerParams(dimension_semantics=("arbitrary",) * n_axes, vmem_limit_bytes=VMEM_LIMIT)


def _dot(a, b):
    return jnp.dot(a, b, preferred_element_type=F32)


def _dot_nt(a, b):
    return lax.dot_general(a, b, (((1,), (1,)), ((), ())), preferred_element_type=F32)


def _dot_tn(a, b):
    return lax.dot_general(a, b, (((0,), (0,)), ((), ())), preferred_element_type=F32)


def _sigmoid(x):
    return 1.0 / (1.0 + jnp.exp(-x))


def _row(tm, n):
    return pl.BlockSpec((tm, n), lambda i: (i, 0))


def _const(shape):
    return pl.BlockSpec(shape, lambda i: (0,) * len(shape))


def _resident(shape):
    return pl.BlockSpec(shape, lambda i: (0,) * len(shape), pipeline_mode=pl.Buffered(1))


def _rms_fwd(x, g):
    rstd = lax.rsqrt(jnp.mean(x * x, axis=-1, keepdims=True) + NORM_EPS)
    xhat = x * rstd
    return xhat * g, xhat, rstd


def _rms_bwd(dy, xhat, rstd, g):
    dxh = dy * g
    dx = rstd * (dxh - xhat * jnp.mean(dxh * xhat, axis=-1, keepdims=True))
    return dx, dy * xhat


def _in_proj(x, lead, g1, w_in_t, r, tm):
    seq = x.shape[0]
    n_sub = tm // BLOCK

    def body(*refs):
        x_refs = refs[:n_sub]
        lead_ref, g_ref, w_ref, h0_ref, q_ref, kv_ref, c_ref = refs[n_sub:]
        i = pl.program_id(0)
        pieces = []
        for k, x_ref in enumerate(x_refs):
            at = jnp.zeros((BLOCK, D_MODEL), jnp.int32) + (i * tm + (k - 1) * BLOCK)
            piece = jnp.where((at >= 0) & (at < seq), x_ref[...], 0.0)
            pieces.append(jnp.where(at < 0, lead_ref[...], piece) if k == 0 else piece)
        h = jnp.concatenate(pieces, axis=0)
        h0_ref[...] = h
        hn = _rms_fwd(h, g_ref[...])[0].astype(BF16)
        q_ref[...] = _dot_nt(hn, w_ref[Q0:KV0, :]).astype(BF16)
        kv_ref[...] = _dot_nt(hn, w_ref[KV0:C0, :]).astype(BF16)
        c_ref[...] = _dot_nt(hn, w_ref[C0:IN_COLS, :])

    def x_block(k):
        return pl.BlockSpec((BLOCK, D_MODEL), lambda i: (jnp.clip(n_sub * i - 1 + k, 0, seq // BLOCK - 1), 0))

    return pl.pallas_call(
        body, name="in_proj", grid=(r // tm,),
        in_specs=[x_block(k) for k in range(n_sub)] + [_const((BLOCK, D_MODEL)), _const((1, D_MODEL)), _const((IN_COLS, D_MODEL))],
        out_specs=[_row(tm, D_MODEL), _row(tm, ATTN_W), _row(tm, 2 * KV_W), _row(tm, 2 * CONV_W)],
        out_shape=[jax.ShapeDtypeStruct((r, D_MODEL), F32), jax.ShapeDtypeStruct((r, ATTN_W), BF16),
                   jax.ShapeDtypeStruct((r, 2 * KV_W), BF16), jax.ShapeDtypeStruct((r, 2 * CONV_W), F32)],
        compiler_params=_cparams(),
    )(*[x] * n_sub, lead, g1, w_in_t)


def _attn_bias_init(bias_ref, late_ref):
    row = lax.broadcasted_iota(jnp.int32, (GROUP * BLOCK, BLOCK), 0) & (BLOCK - 1)
    col = lax.broadcasted_iota(jnp.int32, (GROUP * BLOCK, BLOCK), 1)
    late_ref[...] = jnp.where(col > row, 1.0, 0.0)
    for g in range(N_KV):
        slope = jnp.concatenate([jnp.zeros((BLOCK, BLOCK), F32) + SLOPES[g * GROUP + j] for j in range(GROUP)], axis=0)
        bias_ref[g, :, 0:BLOCK] = jnp.where(col >= LEAD, 0.0, NEG)
        bias_ref[g, :, BLOCK:2 * BLOCK] = -slope * jnp.where(col > row, row - col + BLOCK, row - col).astype(F32)


def _attn_block_bias(late_ref, i):
    late = late_ref[...]
    meta0 = jnp.where(i == 0, NEG, 0.0)
    no_prev = jnp.where(i >= 2, 0.0, NEG)
    no_cur = jnp.where(i >= 1, 0.0, NEG)
    return late * meta0, late * no_prev + no_cur


def _attn_logits(s3, bias_ref, block_bias, prev_part, g):
    meta = s3[:, 0:BLOCK] * SCALE + (bias_ref[g, :, 0:BLOCK] + block_bias[0])
    band = jnp.where(prev_part, s3[:, BLOCK:2 * BLOCK], s3[:, 2 * BLOCK:3 * BLOCK]) * SCALE + (bias_ref[g, :, BLOCK:2 * BLOCK] + block_bias[1])
    return meta, band


def _split_band(meta, band, prev_part):
    return jnp.concatenate([meta, jnp.where(prev_part, band, 0.0), jnp.where(prev_part, 0.0, band)], axis=1)


def _head_rows(vals):
    return jnp.concatenate([jnp.zeros((BLOCK, BLOCK), F32) + v for v in vals], axis=0)


def _stack_heads(ref, g):
    return jnp.concatenate([ref[:, (g * GROUP + j) * HEAD_DIM:(g * GROUP + j + 1) * HEAD_DIM] for j in range(GROUP)], axis=0)


def _kv_cat(kvm_ref, kvp_ref, kvc_ref, g):
    ks = slice(g * HEAD_DIM, (g + 1) * HEAD_DIM)
    vs = slice(KV_W + g * HEAD_DIM, KV_W + (g + 1) * HEAD_DIM)
    kcat = jnp.concatenate([kvm_ref[:, ks], kvp_ref[:, ks], kvc_ref[:, ks]], axis=0)
    vcat = jnp.concatenate([kvm_ref[:, vs], kvp_ref[:, vs], kvc_ref[:, vs]], axis=0)
    return kcat, vcat


def _attn_fwd(q, kv, sinks, gathered):
    r = q.shape[0]
    nb = r // BLOCK
    n = len(gathered)

    def body(sink_ref, q_ref, kvc_ref, kvp_ref, kvm_ref, *rest):
        o_ref, lse_ref = rest[n:n + 2]
        dst = rest[n + 2:2 * n + 2]
        bias_ref, late_ref, send_sems, recv_sems = rest[2 * n + 2:]
        i = pl.program_id(0)

        @pl.when(i == 0)
        def _():
            for cp in _gather_ici(dst, send_sems, recv_sems)[0]:
                cp.start()
            _attn_bias_init(bias_ref, late_ref)

        lane = lax.broadcasted_iota(jnp.int32, (BLOCK, BLOCK), 1)
        lse_tile = jnp.zeros((BLOCK, BLOCK), F32)
        block_bias = _attn_block_bias(late_ref, i)
        prev_part = late_ref[...] > 0.5
        for g in range(N_KV):
            kcat, vcat = _kv_cat(kvm_ref, kvp_ref, kvc_ref, g)
            heads = range(g * GROUP, (g + 1) * GROUP)
            meta, band = _attn_logits(_dot_nt(_stack_heads(q_ref, g), kcat), bias_ref, block_bias, prev_part, g)
            sink = _head_rows([sink_ref[h] for h in heads])
            m = jnp.maximum(jnp.max(jnp.maximum(meta, band), axis=-1, keepdims=True), sink)
            p_meta, p_band = jnp.exp(meta - m), jnp.exp(band - m)
            l = jnp.sum(p_meta + p_band, axis=-1, keepdims=True) + jnp.exp(sink - m)
            o = _dot(_split_band(p_meta, p_band, prev_part).astype(BF16), vcat) * (1.0 / l)[:, 0:HEAD_DIM]
            lse = m + jnp.log(l)
            for j, h in enumerate(heads):
                o_ref[:, h * HEAD_DIM:(h + 1) * HEAD_DIM] = o[j * BLOCK:(j + 1) * BLOCK].astype(BF16)
                lse_tile = jnp.where(lane == h, lse[j * BLOCK:(j + 1) * BLOCK], lse_tile)
        lse_ref[...] = lse_tile

        @pl.when(i == nb - 1)
        def _():
            sends, arrivals = _gather_ici(dst, send_sems, recv_sems)
            for cp in arrivals:
                cp.wait_recv()
            for cp in sends:
                cp.wait_send()

    return pl.pallas_call(
        body, name="attn_fwd", grid=(nb,),
        in_specs=[pl.BlockSpec(memory_space=pltpu.SMEM), _row(BLOCK, ATTN_W), _row(BLOCK, 2 * KV_W),
                  pl.BlockSpec((BLOCK, 2 * KV_W), lambda i: (jnp.maximum(i - 1, 0), 0)), _const((BLOCK, 2 * KV_W))] + [ANY] * n,
        out_specs=[_row(BLOCK, ATTN_W), _row(BLOCK, BLOCK)] + [ANY] * n,
        out_shape=[jax.ShapeDtypeStruct((r, ATTN_W), BF16), jax.ShapeDtypeStruct((r, BLOCK), F32)]
        + [jax.ShapeDtypeStruct(g.shape, g.dtype) for g in gathered],
        input_output_aliases={5 + k: 2 + k for k in range(n)},
        scratch_shapes=[pltpu.VMEM((N_KV, GROUP * BLOCK, 2 * BLOCK), F32), pltpu.VMEM((GROUP * BLOCK, BLOCK), F32),
                        pltpu.SemaphoreType.DMA((3 * n,)), pltpu.SemaphoreType.DMA((3 * n,))],
        compiler_params=_cparams(),
    )(sinks, q, kv, kv, kv, *gathered)


def _shifted_copies(ub_ref, win):
    w = win.shape[0]
    ub_ref[0] = win
    for b in range(1, 8):
        ub_ref[b] = pltpu.roll(win, shift=w - b, axis=0)


def _conv_chunk(ub_ref, w_ref, r0, shifts):
    acc = jnp.zeros((CONV_CHUNK, CONV_W), F32)
    for j in range(CONV_K):
        a, b = divmod(shifts[j], 8)
        acc = acc + w_ref[j:j + 1, :] * ub_ref[b, pl.ds(r0 + 8 * a, CONV_CHUNK), :]
    return acc


FWD_SHIFTS = tuple(HALO - (CONV_K - 1) + j for j in range(CONV_K))
BWD_SHIFTS = tuple(CONV_K - 1 - j for j in range(CONV_K))


def _glu_window(cp_ref, c_ref, i):
    tile = c_ref[:, 0:CONV_W] * _sigmoid(c_ref[:, CONV_W:2 * CONV_W])
    halo = cp_ref[:, 0:CONV_W] * _sigmoid(cp_ref[:, CONV_W:2 * CONV_W])
    first = (jnp.zeros((HALO, CONV_W), jnp.int32) + i) == 0
    return jnp.concatenate([jnp.where(first, 0.0, halo), tile], axis=0)


def _halo_before(tm, n):
    return pl.BlockSpec((HALO, n), lambda i: (jnp.maximum(i * (tm // HALO) - 1, 0), 0))


def _conv_fwd(cacg, cw, cb, lg, lb, tm, gathered):
    r = cacg.shape[0]
    n = len(gathered)

    def body(c_ref, cp_ref, w_ref, cb_ref, lg_ref, lb_ref, *rest):
        o_ref, y_ref = rest[n:n + 2]
        dst = rest[n + 2:2 * n + 2]
        ub_ref, send_sems, recv_sems = rest[2 * n + 2:]
        i = pl.program_id(0)

        @pl.when(i == 0)
        def _():
            for cp in _gather_d2d(dst, send_sems, recv_sems)[0]:
                cp.start()

        _shifted_copies(ub_ref, _glu_window(cp_ref, c_ref, i))

        def chunk(ci, carry):
            r0 = pl.multiple_of(ci * CONV_CHUNK, CONV_CHUNK)
            y = _conv_chunk(ub_ref, w_ref, r0, FWD_SHIFTS) + cb_ref[...]
            yc = y - jnp.mean(y, axis=-1, keepdims=True)
            rs = lax.rsqrt(jnp.mean(yc * yc, axis=-1, keepdims=True) + NORM_EPS)
            yn = yc * rs * lg_ref[...] + lb_ref[...]
            o_ref[pl.ds(r0, CONV_CHUNK), :] = (yn * _sigmoid(yn)).astype(BF16)
            y_ref[pl.ds(r0, CONV_CHUNK), :] = y
            return carry

        lax.fori_loop(0, tm // CONV_CHUNK, chunk, 0, unroll=6)

        @pl.when(i == r // tm - 1)
        def _():
            sends, arrivals = _gather_d2d(dst, send_sems, recv_sems)
            for cp in arrivals:
                cp.wait_recv()
            for cp in sends:
                cp.wait_send()

    return pl.pallas_call(
        body, name="conv_fwd", grid=(r // tm,),
        in_specs=[_row(tm, 2 * CONV_W), _halo_before(tm, 2 * CONV_W), _const((32, CONV_W)), _const((1, CONV_W)),
                  _const((1, CONV_W)), _const((1, CONV_W))] + [ANY] * n,
        out_specs=[_row(tm, CONV_W), _row(tm, CONV_W)] + [ANY] * n,
        out_shape=[jax.ShapeDtypeStruct((r, CONV_W), BF16), jax.ShapeDtypeStruct((r, CONV_W), F32)]
        + [jax.ShapeDtypeStruct(g.shape, g.dtype) for g in gathered],
        input_output_aliases={6 + k: 2 + k for k in range(n)},
        scratch_shapes=[pltpu.VMEM((8, tm + HALO, CONV_W), F32), pltpu.SemaphoreType.DMA((3 * n,)), pltpu.SemaphoreType.DMA((3 * n,))],
        compiler_params=_cparams(),
    )(cacg, cacg, cw, cb, lg, lb, *gathered)


def _out_proj(oa, oc, h0, ga, gc, g2, w_out_b, tm):
    r = h0.shape[0]

    def body(oa_ref, oc_ref, h_ref, ga_ref, gc_ref, g2_ref, w_ref, h1_ref, hn2_ref):
        ma = _rms_fwd(oa_ref[...].astype(F32), ga_ref[...])[0].astype(BF16)
        mc = _rms_fwd(oc_ref[...].astype(F32), gc_ref[...])[0].astype(BF16)
        h1 = h_ref[...] + _dot(jnp.concatenate([ma, mc], axis=1), w_ref[...])
        h1_ref[...] = h1
        hn2_ref[...] = _rms_fwd(h1, g2_ref[...])[0].astype(BF16)

    return pl.pallas_call(
        body, name="out_proj", grid=(r // tm,),
        in_specs=[_row(tm, ATTN_W), _row(tm, CONV_W), _row(tm, D_MODEL), _const((1, ATTN_W)), _const((1, CONV_W)),
                  _const((1, D_MODEL)), _const((D_MODEL, D_MODEL))],
        out_specs=[_row(tm, D_MODEL), _row(tm, D_MODEL)],
        out_shape=[jax.ShapeDtypeStruct((r, D_MODEL), F32), jax.ShapeDtypeStruct((r, D_MODEL), BF16)],
        compiler_params=_cparams(),
    )(oa, oc, h0, ga, gc, g2, w_out_b)


def _ffn_fwd(hn2, h1, target, gf, wg_t, wu_t, wd_b, tm):
    r = h1.shape[0]
    seq = target.shape[0]
    n_sub = tm // BLOCK

    def body(hn_ref, h1_ref, *rest):
        t_refs = rest[:n_sub]
        gf_ref, wg_ref, wu_ref, wd_ref, gate_ref, up_ref, act_ref, dh2_ref, dh2b_ref, loss_ref, dgf_ref = rest[n_sub:]
        i = pl.program_id(0)

        @pl.when(i == 0)
        def _():
            loss_ref[...] = jnp.zeros_like(loss_ref)
            dgf_ref[...] = jnp.zeros_like(dgf_ref)

        hn = hn_ref[...]
        for cs in FF_CHUNKS:
            gate = _dot_nt(hn, wg_ref[cs, :])
            up = _dot_nt(hn, wu_ref[cs, :])
            gate_ref[:, cs] = gate.astype(BF16)
            up_ref[:, cs] = up.astype(BF16)
            act_ref[:, cs] = (gate * _sigmoid(gate) * up).astype(BF16)
        y, xhat, rstd = _rms_fwd(h1_ref[...] + _dot(act_ref[...], wd_ref[...]), gf_ref[...])
        rows = lax.broadcasted_iota(jnp.int32, (tm, D_MODEL), 0) + i * tm
        real = (rows >= BLOCK) & (rows < BLOCK + seq)
        err = jnp.where(real, y - jnp.concatenate([t[...] for t in t_refs], axis=0), 0.0)
        loss_ref[...] += jnp.sum(err * err) * (0.5 / D_MODEL)
        dy = err * (1.0 / D_MODEL)
        dh2, dg_rows = _rms_bwd(dy, xhat, rstd, gf_ref[...])
        dgf_ref[...] += jnp.sum(dg_rows, axis=0, keepdims=True)
        dh2_ref[...] = dh2
        dh2b_ref[...] = dh2.astype(BF16)

    def target_block(k):
        return pl.BlockSpec((BLOCK, D_MODEL), lambda i: (jnp.clip(n_sub * i - 1 + k, 0, seq // BLOCK - 1), 0))

    return pl.pallas_call(
        body, name="ffn_fwd", grid=(r // tm,),
        in_specs=[_row(tm, D_MODEL), _row(tm, D_MODEL)] + [target_block(k) for k in range(n_sub)]
        + [_const((1, D_MODEL))] + [_resident((D_FF, D_MODEL))] * 3,
        out_specs=[_row(tm, D_FF)] * 3 + [_row(tm, D_MODEL), _row(tm, D_MODEL), _const((1, BLOCK)), _const((1, D_MODEL))],
        out_shape=[jax.ShapeDtypeStruct((r, D_FF), BF16)] * 3
        + [jax.ShapeDtypeStruct((r, D_MODEL), F32), jax.ShapeDtypeStruct((r, D_MODEL), BF16),
           jax.ShapeDtypeStruct((1, BLOCK), F32), jax.ShapeDtypeStruct((1, D_MODEL), F32)],
        compiler_params=_cparams(),
    )(hn2, h1, *[target] * n_sub, gf, wg_t, wu_t, wd_b)


def _ffn_bwd(dh2, dh2b, gate, up, h1, g2, wg_t, wu_t, wd_b, tm):
    r = h1.shape[0]

    def body(dh2_ref, dh2b_ref, gate_ref, up_ref, h1_ref, g2_ref, wg_ref, wu_ref, wd_ref, dgate_ref, dup_ref, dh1_ref, dg2_ref):
        @pl.when(pl.program_id(0) == 0)
        def _():
            dg2_ref[...] = jnp.zeros_like(dg2_ref)

        dyb = dh2b_ref[...]
        for cs in FF_CHUNKS:
            dact = _dot_nt(dyb, wd_ref[cs, :])
            gate = gate_ref[:, cs].astype(F32)
            up = up_ref[:, cs].astype(F32)
            sg = _sigmoid(gate)
            dgate_ref[:, cs] = (dact * up * (sg * (1.0 + gate * (1.0 - sg)))).astype(BF16)
            dup_ref[:, cs] = (dact * (gate * sg)).astype(BF16)
        dhn = _dot(dgate_ref[...], wg_ref[...]) + _dot(dup_ref[...], wu_ref[...])
        _, xhat, rstd = _rms_fwd(h1_ref[...], g2_ref[...])
        dx, dg_rows = _rms_bwd(dhn, xhat, rstd, g2_ref[...])
        dg2_ref[...] += jnp.sum(dg_rows, axis=0, keepdims=True)
        dh1_ref[...] = dh2_ref[...] + dx

    return pl.pallas_call(
        body, name="ffn_bwd", grid=(r // tm,),
        in_specs=[_row(tm, D_MODEL), _row(tm, D_MODEL), _row(tm, D_FF), _row(tm, D_FF), _row(tm, D_MODEL), _const((1, D_MODEL))]
        + [_resident((D_FF, D_MODEL))] * 3,
        out_specs=[_row(tm, D_FF), _row(tm, D_FF), _row(tm, D_MODEL), _const((1, D_MODEL))],
        out_shape=[jax.ShapeDtypeStruct((r, D_FF), BF16), jax.ShapeDtypeStruct((r, D_FF), BF16),
                   jax.ShapeDtypeStruct((r, D_MODEL), F32), jax.ShapeDtypeStruct((1, D_MODEL), F32)],
        compiler_params=_cparams(),
    )(dh2, dh2b, gate, up, h1, g2, wg_t, wu_t, wd_b)


FF_HALF = D_FF // 2


def _ffn_wgrad_gu(hn2, dgate, dup, tk):
    r = hn2.shape[0]
    n_k = r // tk

    def body(hn_ref, dg_ref, du_ref, wg_ref, wu_ref, accg_ref, accu_ref):
        k = pl.program_id(1)

        @pl.when(k == 0)
        def _():
            accg_ref[...] = jnp.zeros_like(accg_ref)
            accu_ref[...] = jnp.zeros_like(accu_ref)

        hn = hn_ref[...]
        accg_ref[...] += _dot_tn(dg_ref[...], hn)
        accu_ref[...] += _dot_tn(du_ref[...], hn)

        @pl.when(k == n_k - 1)
        def _():
            wg_ref[...] = accg_ref[...].astype(BF16)
            wu_ref[...] = accu_ref[...].astype(BF16)

    col = pl.BlockSpec((tk, FF_HALF), lambda j, k: (k, j))
    out = pl.BlockSpec((FF_HALF, D_MODEL), lambda j, k: (j, 0))
    return pl.pallas_call(
        body, name="ffn_wgrad_gu", grid=(2, n_k),
        in_specs=[pl.BlockSpec((tk, D_MODEL), lambda j, k: (k, 0)), col, col],
        out_specs=[out, out],
        out_shape=[jax.ShapeDtypeStruct((D_FF, D_MODEL), BF16)] * 2,
        scratch_shapes=[pltpu.VMEM((FF_HALF, D_MODEL), F32)] * 2,
        compiler_params=_cparams(2),
    )(hn2, dgate, dup)


def _ffn_wgrad_d(act, dh2b, tk):
    r = act.shape[0]
    n_k = r // tk

    def body(a_ref, dy_ref, wd_ref, acc_ref):
        k = pl.program_id(1)

        @pl.when(k == 0)
        def _():
            acc_ref[...] = jnp.zeros_like(acc_ref)

        acc_ref[...] += _dot_tn(a_ref[...], dy_ref[...])

        @pl.when(k == n_k - 1)
        def _():
            wd_ref[...] = acc_ref[...].astype(BF16)

    return pl.pallas_call(
        body, name="ffn_wgrad_d", grid=(2, n_k),
        in_specs=[pl.BlockSpec((tk, FF_HALF), lambda j, k: (k, j)), pl.BlockSpec((tk, D_MODEL), lambda j, k: (k, 0))],
        out_specs=pl.BlockSpec((FF_HALF, D_MODEL), lambda j, k: (j, 0)),
        out_shape=jax.ShapeDtypeStruct((D_FF, D_MODEL), BF16),
        scratch_shapes=[pltpu.VMEM((FF_HALF, D_MODEL), F32)],
        compiler_params=_cparams(2),
    )(act, dh2b)


def _out_proj_bwd(dh1, oa, oc, ga, gc, w_out_b, tm):
    r = dh1.shape[0]

    def body(dh_ref, oa_ref, oc_ref, ga_ref, gc_ref, w_ref, doa_ref, doc_ref, dw_ref, dga_ref, dgc_ref, acc_ref):
        i = pl.program_id(0)

        @pl.when(i == 0)
        def _():
            acc_ref[...] = jnp.zeros_like(acc_ref)
            dga_ref[...] = jnp.zeros_like(dga_ref)
            dgc_ref[...] = jnp.zeros_like(dgc_ref)

        dhb = dh_ref[...].astype(BF16)
        dmix = _dot_nt(dhb, w_ref[...])
        ma, xa, ra = _rms_fwd(oa_ref[...].astype(F32), ga_ref[...])
        mc, xc, rc = _rms_fwd(oc_ref[...].astype(F32), gc_ref[...])
        acc_ref[...] += _dot_tn(jnp.concatenate([ma.astype(BF16), mc.astype(BF16)], axis=1), dhb)

        @pl.when(i == r // tm - 1)
        def _():
            dw_ref[...] = acc_ref[...].astype(BF16)

        doa, dga_rows = _rms_bwd(dmix[:, 0:ATTN_W], xa, ra, ga_ref[...])
        doc, dgc_rows = _rms_bwd(dmix[:, ATTN_W:ATTN_W + CONV_W], xc, rc, gc_ref[...])
        doa_ref[...] = doa
        doc_ref[...] = doc
        dga_ref[...] += jnp.sum(dga_rows, axis=0, keepdims=True)
        dgc_ref[...] += jnp.sum(dgc_rows, axis=0, keepdims=True)

    return pl.pallas_call(
        body, name="out_proj_bwd", grid=(r // tm,),
        in_specs=[_row(tm, D_MODEL), _row(tm, ATTN_W), _row(tm, CONV_W), _const((1, ATTN_W)), _const((1, CONV_W)),
                  _const((D_MODEL, D_MODEL))],
        out_specs=[_row(tm, ATTN_W), _row(tm, CONV_W), _const((D_MODEL, D_MODEL)), _const((1, ATTN_W)), _const((1, CONV_W))],
        out_shape=[jax.ShapeDtypeStruct((r, ATTN_W), F32), jax.ShapeDtypeStruct((r, CONV_W), F32),
                   jax.ShapeDtypeStruct((D_MODEL, D_MODEL), BF16), jax.ShapeDtypeStruct((1, ATTN_W), F32),
                   jax.ShapeDtypeStruct((1, CONV_W), F32)],
        scratch_shapes=[pltpu.VMEM((D_MODEL, D_MODEL), F32)],
        compiler_params=_cparams(),
    )(dh1, oa, oc, ga, gc, w_out_b)


def _conv_bwd_params(doc, y, cacg, lg, lb, tm, parts):
    r = cacg.shape[0]
    n_steps = r // tm
    n = len(parts)

    def body(do_ref, y_ref, c_ref, cp_ref, lg_ref, lb_ref, *rest):
        src = rest[:n]
        dy_ref, dcw_ref, dcb_ref, dlg_ref, dlb_ref = rest[n:n + 5]
        dst = rest[n + 5:2 * n + 5]
        ub_ref, accw_ref, send_sems, recv_sems = rest[2 * n + 5:]
        i = pl.program_id(0)

        @pl.when(i == 0)
        def _():
            for cp in _scatter(src, dst, send_sems, recv_sems):
                cp.start()
            accw_ref[...] = jnp.zeros_like(accw_ref)
            dcb_ref[...] = jnp.zeros_like(dcb_ref)
            dlg_ref[...] = jnp.zeros_like(dlg_ref)
            dlb_ref[...] = jnp.zeros_like(dlb_ref)

        _shifted_copies(ub_ref, _glu_window(cp_ref, c_ref, i))

        def chunk(ci, carry):
            r0 = pl.multiple_of(ci * CONV_CHUNK, CONV_CHUNK)
            y = y_ref[pl.ds(r0, CONV_CHUNK), :]
            yc = y - jnp.mean(y, axis=-1, keepdims=True)
            rs = lax.rsqrt(jnp.mean(yc * yc, axis=-1, keepdims=True) + NORM_EPS)
            xhat = yc * rs
            yn = xhat * lg_ref[...] + lb_ref[...]
            sg = _sigmoid(yn)
            dyn = do_ref[pl.ds(r0, CONV_CHUNK), :] * (sg * (1.0 + yn * (1.0 - sg)))
            dlg_ref[...] += jnp.sum(dyn * xhat, axis=0, keepdims=True)
            dlb_ref[...] += jnp.sum(dyn, axis=0, keepdims=True)
            dxh = dyn * lg_ref[...]
            dy = rs * (dxh - jnp.mean(dxh, axis=-1, keepdims=True) - xhat * jnp.mean(dxh * xhat, axis=-1, keepdims=True))
            dcb_ref[...] += jnp.sum(dy, axis=0, keepdims=True)
            dy_ref[pl.ds(r0, CONV_CHUNK), :] = dy
            for j in range(CONV_K):
                a, b = divmod(FWD_SHIFTS[j], 8)
                prod = dy * ub_ref[b, pl.ds(r0 + 8 * a, CONV_CHUNK), :]
                accw_ref[j] += jnp.sum(prod.reshape(CONV_CHUNK // 8, 8, CONV_W), axis=0)
            return carry

        lax.fori_loop(0, tm // CONV_CHUNK, chunk, 0, unroll=6)

        @pl.when(i == n_steps - 1)
        def _():
            for j in range(32):
                dcw_ref[j:j + 1, :] = jnp.sum(accw_ref[j], axis=0, keepdims=True)
            for cp in _scatter(src, dst, send_sems, recv_sems):
                cp.wait()

    vec = _const((1, CONV_W))
    return pl.pallas_call(
        body, name="conv_bwd_params", grid=(n_steps,),
        in_specs=[_row(tm, CONV_W), _row(tm, CONV_W), _row(tm, 2 * CONV_W), _halo_before(tm, 2 * CONV_W), vec, vec] + [ANY] * n,
        out_specs=[_row(tm, CONV_W), _const((32, CONV_W)), vec, vec, vec] + [ANY] * n,
        out_shape=[jax.ShapeDtypeStruct((r, CONV_W), F32), jax.ShapeDtypeStruct((32, CONV_W), F32)]
        + [jax.ShapeDtypeStruct((1, CONV_W), F32)] * 3 + _scatter_landing(parts),
        scratch_shapes=[pltpu.VMEM((8, tm + HALO, CONV_W), F32), pltpu.VMEM((32, 8, CONV_W), F32),
                        pltpu.SemaphoreType.DMA((7 * n,)), pltpu.SemaphoreType.DMA((7 * n,))],
        compiler_params=_cparams(),
    )(doc, y, cacg, cacg, lg, lb, *parts)


def _conv_bwd_data(dy, cacg, cw, tm):
    r = cacg.shape[0]
    n_steps = r // tm

    def body(dy_ref, dyn_ref, c_ref, w_ref, dc_ref, ub_ref):
        last = (jnp.zeros((HALO, CONV_W), jnp.int32) + pl.program_id(0)) == n_steps - 1
        win = jnp.concatenate([dy_ref[...], jnp.where(last, 0.0, dyn_ref[...])], axis=0)
        _shifted_copies(ub_ref, win)

        def chunk(ci, carry):
            r0 = pl.multiple_of(ci * CONV_CHUNK, CONV_CHUNK)
            du = _conv_chunk(ub_ref, w_ref, r0, BWD_SHIFTS)
            ca = c_ref[pl.ds(r0, CONV_CHUNK), 0:CONV_W]
            sg = _sigmoid(c_ref[pl.ds(r0, CONV_CHUNK), CONV_W:2 * CONV_W])
            dc_ref[pl.ds(r0, CONV_CHUNK), 0:CONV_W] = (du * sg).astype(BF16)
            dc_ref[pl.ds(r0, CONV_CHUNK), CONV_W:2 * CONV_W] = (du * ca * sg * (1.0 - sg)).astype(BF16)
            return carry

        lax.fori_loop(0, tm // CONV_CHUNK, chunk, 0, unroll=6)

    halo_after = pl.BlockSpec((HALO, CONV_W), lambda i: (jnp.minimum((i + 1) * (tm // HALO), r // HALO - 1), 0))
    return pl.pallas_call(
        body, name="conv_bwd_data", grid=(n_steps,),
        in_specs=[_row(tm, CONV_W), halo_after, _row(tm, 2 * CONV_W), _const((32, CONV_W))],
        out_specs=_row(tm, 2 * CONV_W),
        out_shape=jax.ShapeDtypeStruct((r, 2 * CONV_W), BF16),
        scratch_shapes=[pltpu.VMEM((8, tm + HALO, CONV_W), F32)],
        compiler_params=_cparams(),
    )(dy, dy, cacg, cw)


def _attn_bwd(q, kv, o, do, lse, sinks, parts):
    r = q.shape[0]
    nb = r // BLOCK
    n = len(parts)

    def body(sink_ref, q_ref, kvc_ref, kvp_ref, kvm_ref, o_ref, do_ref, lse_ref, *rest):
        src = rest[:n]
        dq_ref, dkv_ref, dmeta_ref, dsink_ref = rest[n:n + 4]
        dst = rest[n + 4:2 * n + 4]
        hold_ref, bias_ref, late_ref, send_sems, recv_sems = rest[2 * n + 4:]
        i = pl.program_id(0)

        @pl.when(i == 0)
        def _():
            for cp in _scatter(src, dst, send_sems, recv_sems):
                cp.start()
            _attn_bias_init(bias_ref, late_ref)
            dmeta_ref[...] = jnp.zeros_like(dmeta_ref)
            dsink_ref[...] = jnp.zeros_like(dsink_ref)
            hold_ref[...] = jnp.zeros_like(hold_ref)

        @pl.when(i < nb)
        def _():
            lane = lax.broadcasted_iota(jnp.int32, (BLOCK, BLOCK), 1)
            lse_tile = lse_ref[...]
            zero = jnp.zeros((BLOCK, BLOCK), F32)
            block_bias = _attn_block_bias(late_ref, i)
            prev_part = late_ref[...] > 0.5
            for g in range(N_KV):
                kcat, vcat = _kv_cat(kvm_ref, kvp_ref, kvc_ref, g)
                heads = range(g * GROUP, (g + 1) * GROUP)
                qs = _stack_heads(q_ref, g)
                dos = _stack_heads(do_ref, g)
                dosb = dos.astype(BF16)
                lse = jnp.concatenate(
                    [jnp.sum(jnp.where(lane == h, lse_tile, 0.0), axis=-1, keepdims=True) + zero for h in heads], axis=0)
                delta = jnp.sum(dos * _stack_heads(o_ref, g).astype(F32), axis=-1, keepdims=True) + jnp.zeros((GROUP * BLOCK, BLOCK), F32)
                band_bias = bias_ref[g, :, BLOCK:2 * BLOCK] + block_bias[1]
                bias = [bias_ref[g, :, 0:BLOCK] + block_bias[0], jnp.where(prev_part, band_bias, NEG), jnp.where(prev_part, NEG, band_bias)]
                s = _dot_nt(qs, kcat)
                dp = _dot_nt(dosb, vcat)
                ps = [jnp.exp(s[:, k * BLOCK:(k + 1) * BLOCK] * SCALE + bias[k] - lse) for k in range(3)]
                p = jnp.concatenate(ps, axis=1)
                ds = jnp.concatenate(
                    [(ps[k] * (dp[:, k * BLOCK:(k + 1) * BLOCK] - delta)) * SCALE for k in range(3)], axis=1).astype(BF16)
                sink_term = jnp.exp(_head_rows([sink_ref[h] for h in heads]) - lse)[:, 0:1] * delta[:, 0:1]
                dq = _dot(ds, kcat).astype(BF16)
                for j, h in enumerate(heads):
                    dsink_ref[h:h + 1, :] += -jnp.sum(sink_term[j * BLOCK:(j + 1) * BLOCK])
                    dq_ref[:, h * HEAD_DIM:(h + 1) * HEAD_DIM] = dq[j * BLOCK:(j + 1) * BLOCK]
                dk_t = _dot_tn(qs, ds)
                dv_t = _dot_tn(dosb, p.astype(BF16))
                ks = slice(g * HEAD_DIM, (g + 1) * HEAD_DIM)
                vs = slice(KV_W + g * HEAD_DIM, KV_W + (g + 1) * HEAD_DIM)
                for sl, grad_t in ((ks, dk_t), (vs, dv_t)):
                    dmeta_ref[:, sl] += grad_t[:, 0:BLOCK].T
                    dkv_ref[:, sl] = hold_ref[:, sl] + grad_t[:, BLOCK:2 * BLOCK].T
                    hold_ref[:, sl] = grad_t[:, 2 * BLOCK:3 * BLOCK].T

        @pl.when(i == nb)
        def _():
            dkv_ref[...] = hold_ref[...]
            for cp in _scatter(src, dst, send_sems, recv_sems):
                cp.wait()

    def cur(i):
        return jnp.minimum(i, nb - 1)

    return pl.pallas_call(
        body, name="attn_bwd", grid=(nb + 1,),
        in_specs=[pl.BlockSpec(memory_space=pltpu.SMEM),
                  pl.BlockSpec((BLOCK, ATTN_W), lambda i: (cur(i), 0)),
                  pl.BlockSpec((BLOCK, 2 * KV_W), lambda i: (cur(i), 0)),
                  pl.BlockSpec((BLOCK, 2 * KV_W), lambda i: (jnp.maximum(cur(i) - 1, 0), 0)),
                  _const((BLOCK, 2 * KV_W)),
                  pl.BlockSpec((BLOCK, ATTN_W), lambda i: (cur(i), 0)),
                  pl.BlockSpec((BLOCK, ATTN_W), lambda i: (cur(i), 0)),
                  pl.BlockSpec((BLOCK, BLOCK), lambda i: (cur(i), 0))] + [ANY] * n,
        out_specs=[pl.BlockSpec((BLOCK, ATTN_W), lambda i: (cur(i), 0)),
                   pl.BlockSpec((BLOCK, 2 * KV_W), lambda i: (jnp.maximum(i - 1, 0), 0)),
                   _const((BLOCK, 2 * KV_W)), _const((N_HEADS, BLOCK))] + [ANY] * n,
        out_shape=[jax.ShapeDtypeStruct((r, ATTN_W), BF16), jax.ShapeDtypeStruct((r, 2 * KV_W), F32),
                   jax.ShapeDtypeStruct((BLOCK, 2 * KV_W), F32), jax.ShapeDtypeStruct((N_HEADS, BLOCK), F32)] + _scatter_landing(parts),
        scratch_shapes=[pltpu.VMEM((BLOCK, 2 * KV_W), F32), pltpu.VMEM((N_KV, GROUP * BLOCK, 2 * BLOCK), F32),
                        pltpu.VMEM((GROUP * BLOCK, BLOCK), F32), pltpu.SemaphoreType.DMA((7 * n,)), pltpu.SemaphoreType.DMA((7 * n,))],
        compiler_params=_cparams(),
    )(sinks, q, kv, kv, kv, o, do, lse, *parts)


def _in_proj_bwd(dq, dkv, dkv_meta, dc, dh1, h0, g1, w_in_t, seq, tm):
    r = h0.shape[0]
    n_tiles = r // tm
    n_out = -(-seq // tm)

    def body(dq_ref, dkv_ref, dm_ref, dc_ref, dh1_ref, h_ref, g_ref, w_ref, gx_ref, lead_ref, dwt_ref, dg_ref, dw_ref, hold_ref):
        i = pl.program_id(0)

        @pl.when(i == 0)
        def _():
            dw_ref[...] = jnp.zeros_like(dw_ref)
            dg_ref[...] = jnp.zeros_like(dg_ref)

        @pl.when(i < n_tiles)
        def _():
            meta = jnp.concatenate([dm_ref[...], jnp.zeros((tm - BLOCK, 2 * KV_W), F32)], axis=0) if tm > BLOCK else dm_ref[...]
            first = (jnp.zeros((tm, 2 * KV_W), jnp.int32) + i) == 0
            dkvb = (dkv_ref[...] + jnp.where(first, meta, 0.0)).astype(BF16)
            hn, xhat, rstd = _rms_fwd(h_ref[...], g_ref[...])
            dproj = jnp.concatenate([dq_ref[...], dkvb, dc_ref[...]], axis=1)
            dhn = _dot(dproj, w_ref[...])
            dw_ref[...] += _dot_tn(dproj, hn.astype(BF16))
            dx, dg_rows = _rms_bwd(dhn, xhat, rstd, g_ref[...])
            dg_ref[...] += jnp.sum(dg_rows, axis=0, keepdims=True)
            dh0 = dh1_ref[...] + dx

            @pl.when(i == 0)
            def _():
                lead_ref[...] = dh0[0:BLOCK]

            @pl.when((i >= 1) & (i <= n_out))
            def _():
                gx_ref[0:tm - BLOCK, :] = hold_ref[...]
                gx_ref[tm - BLOCK:tm, :] = dh0[0:BLOCK]

            hold_ref[...] = dh0[BLOCK:tm]

        @pl.when((i == n_tiles) & (n_tiles <= n_out))
        def _():
            gx_ref[0:tm - BLOCK, :] = hold_ref[...]

        @pl.when(i == n_tiles - 1)
        def _():
            dwt_ref[...] = dw_ref[...].astype(BF16)

    def tile(n):
        return pl.BlockSpec((tm, n), lambda i: (jnp.minimum(i, n_tiles - 1), 0))

    return pl.pallas_call(
        body, name="in_proj_bwd", grid=(n_tiles + 1,),
        in_specs=[tile(ATTN_W), tile(2 * KV_W), _const((BLOCK, 2 * KV_W)), tile(2 * CONV_W), tile(D_MODEL), tile(D_MODEL),
                  _const((1, D_MODEL)), _const((IN_COLS, D_MODEL))],
        out_specs=[pl.BlockSpec((tm, D_MODEL), lambda i: (jnp.clip(i - 1, 0, n_out - 1), 0)), _const((BLOCK, D_MODEL)),
                   _const((IN_COLS, D_MODEL)), _const((1, D_MODEL))],
        out_shape=[jax.ShapeDtypeStruct((seq, D_MODEL), F32), jax.ShapeDtypeStruct((BLOCK, D_MODEL), F32),
                   jax.ShapeDtypeStruct((IN_COLS, D_MODEL), BF16), jax.ShapeDtypeStruct((1, D_MODEL), F32)],
        scratch_shapes=[pltpu.VMEM((IN_COLS, D_MODEL), F32), pltpu.VMEM((tm - BLOCK, D_MODEL), F32)],
        compiler_params=_cparams(),
    )(dq, dkv, dkv_meta, dc, dh1, h0, g1, w_in_t)


def _adamw_update(w_ref, g_ref, m_ref, v_ref, d_ref, nm_ref, nv_ref):
    g = g_ref[...]
    m = ADAM_B1 * m_ref[...] + (1.0 - ADAM_B1) * g
    v = ADAM_B2 * v_ref[...] + (1.0 - ADAM_B2) * (g * g)
    m_hat = m / (1.0 - ADAM_B1 ** ADAM_STEP)
    v_hat = v / (1.0 - ADAM_B2 ** ADAM_STEP)
    d_ref[...] = -ADAM_LR * (m_hat / (jnp.sqrt(v_hat) + ADAM_EPS) + ADAM_WD * w_ref[...])
    nm_ref[...] = m
    nv_ref[...] = v


def _adamw(w, g, m, v, name):
    rows, cols = w.shape
    tr = rows
    for cand in (256, 176, 128, 64, 32, 16, 8):
        if rows % cand == 0:
            tr = cand
            break

    def body(*refs):
        _adamw_update(*refs)

    spec = _row(tr, cols)
    return pl.pallas_call(
        body, name=name, grid=(rows // tr,), in_specs=[spec] * 4, out_specs=[spec] * 3,
        out_shape=[jax.ShapeDtypeStruct((rows, cols), F32)] * 3, compiler_params=_cparams(),
    )(w, g, m, v)


MESH = pl.DeviceIdType.MESH
ANY = pl.BlockSpec(memory_space=pl.ANY)


def _place():
    x, y, c = lax.axis_index("x"), lax.axis_index("y"), lax.axis_index("c")
    chips = [(1 - x, y), (x, 1 - y), (1 - x, 1 - y)]
    return x, y, c, chips


def _gather_ici(dst, send_sems, recv_sems):
    x, y, c, chips = _place()
    sends, arrivals = [], []
    for k in range(len(dst)):
        rows = dst[k].shape[1] // 2
        half = pl.ds(c * rows, rows)
        mine = dst[k].at[2 * x + y, half]
        for p, chip in enumerate(chips):
            sems = dict(send_sem=send_sems.at[3 * k + p], recv_sem=recv_sems.at[3 * k + p], device_id=(chip[0], chip[1], c),
                        device_id_type=MESH)
            sends.append(pltpu.make_async_remote_copy(src_ref=mine, dst_ref=mine, **sems))
            theirs = dst[k].at[2 * chip[0] + chip[1], half]
            arrivals.append(pltpu.make_async_remote_copy(src_ref=theirs, dst_ref=theirs, **sems))
    return sends, arrivals


def _gather_d2d(dst, send_sems, recv_sems):
    x, y, c, chips = _place()
    sends, arrivals = [], []
    for k in range(len(dst)):
        rows = dst[k].shape[1] // 2
        for p, chip in enumerate(chips):
            sems = dict(send_sem=send_sems.at[3 * k + p], recv_sem=recv_sems.at[3 * k + p], device_id=(x, y, 1 - c),
                        device_id_type=MESH)
            mine = dst[k].at[2 * chip[0] + chip[1], pl.ds(c * rows, rows)]
            sends.append(pltpu.make_async_remote_copy(src_ref=mine, dst_ref=mine, **sems))
            theirs = dst[k].at[2 * chip[0] + chip[1], pl.ds((1 - c) * rows, rows)]
            arrivals.append(pltpu.make_async_remote_copy(src_ref=theirs, dst_ref=theirs, **sems))
    return sends, arrivals


def _own_slots(shard, dtype):
    return jnp.broadcast_to(shard[None], (N_SHARD,) + shard.shape).astype(dtype)


def _gather_weights(slots):
    n = len(slots)

    def body(*refs):
        dst = refs[n:2 * n]
        ici_send, ici_recv, d2d_send, d2d_recv = refs[2 * n:]
        sends, arrivals = _gather_ici(dst, ici_send, ici_recv)
        for cp in sends:
            cp.start()
        for cp in arrivals:
            cp.wait_recv()
        forwards, from_sibling = _gather_d2d(dst, d2d_send, d2d_recv)
        for cp in forwards:
            cp.start()
        for cp in from_sibling:
            cp.wait_recv()
        for cp in sends + forwards:
            cp.wait_send()

    return pl.pallas_call(
        body, name="gather_weights",
        in_specs=[ANY] * n, out_specs=[ANY] * n,
        out_shape=[jax.ShapeDtypeStruct(s.shape, s.dtype) for s in slots],
        input_output_aliases={k: k for k in range(n)},
        scratch_shapes=[pltpu.SemaphoreType.DMA((3 * n,))] * 4,
    )(*slots)


VMEM_WHOLE = pl.BlockSpec(memory_space=pltpu.VMEM)


SMALL_ROWS = 48
SMALL_PLACES = ((0, (0, 1), 0, 0), (1, (0, 1), 1, 0), (2, (0, 1), 2, 0), (3, (0, 16), 8, 0), (4, (0, 1), 3, 0), (5, (0, 1), 3, 512),
                (6, (0, 1), 4, 0), (7, (0, 1), 4, 512), (8, (0, 1), 5, 0), (9, (0, 16), 24, 0), (9, (16, 32), 24, 512),
                (10, (0, 1), 5, 512), (11, (0, 8), 40, 0))


def _pack_small(parts):
    n = len(parts)

    def body(*refs):
        out = refs[n]
        out[...] = jnp.zeros_like(out)
        for k, (lo, hi), r0, c0 in SMALL_PLACES:
            out[r0:r0 + hi - lo, c0:c0 + parts[k].shape[1]] = refs[k][lo:hi, :]

    return pl.pallas_call(body, name="pack_small", in_specs=[VMEM_WHOLE] * n, out_specs=VMEM_WHOLE,
                          out_shape=jax.ShapeDtypeStruct((SMALL_ROWS, D_MODEL), F32))(*parts)


def _sum_small(own, landed, parts):
    n = len(parts)

    def body(own_ref, landed_ref, *outs):
        x, y, c = lax.axis_index("x"), lax.axis_index("y"), lax.axis_index("c")
        total = jnp.zeros((SMALL_ROWS, D_MODEL), F32)
        for d in range(N_DEV):
            j = 4 * ((x + (d >> 2)) % 2) + 2 * ((y + ((d >> 1) & 1)) % 2) + (c + (d & 1)) % 2
            mine = (jnp.zeros((SMALL_ROWS, D_MODEL), jnp.int32) + j) == 0
            total = total + jnp.where(mine, own_ref[...], landed_ref[jnp.maximum(j - 1, 0)])
        for k, (lo, hi), r0, c0 in SMALL_PLACES:
            outs[k][lo:hi, :] = total[r0:r0 + hi - lo, c0:c0 + parts[k].shape[1]]

    return pl.pallas_call(body, name="sum_small", in_specs=[VMEM_WHOLE] * 2, out_specs=[VMEM_WHOLE] * n,
                          out_shape=[jax.ShapeDtypeStruct(p.shape, F32) for p in parts])(own, landed)


def _adamw_small(ws, gs, ms, vs):
    n = len(ws)

    def body(*refs):
        for k in range(n):
            w_ref, g_ref, m_ref, v_ref = (refs[j * n + k] for j in range(4))
            _adamw_update(w_ref, g_ref, m_ref, v_ref, *(refs[(4 + j) * n + k] for j in range(3)))

    shapes = [jax.ShapeDtypeStruct(w.shape, F32) for w in ws]
    out = pl.pallas_call(
        body, name="adamw_small", in_specs=[VMEM_WHOLE] * (4 * n), out_specs=[VMEM_WHOLE] * (3 * n), out_shape=shapes * 3,
    )(*ws, *gs, *ms, *vs)
    return out[:n], out[n:2 * n], out[2 * n:]


def _scatter(src, dst, send_sems, recv_sems, whole=False):
    x, y, c = lax.axis_index("x"), lax.axis_index("y"), lax.axis_index("c")
    copies = []
    for k in range(len(src)):
        for j in range(1, N_DEV):
            px, py, pc = (x + (j >> 2)) % 2, (y + ((j >> 1) & 1)) % 2, (c + (j & 1)) % 2
            rows = src[k].shape[1] // 2
            piece = src[k] if whole else src[k].at[2 * px + py, pl.ds(pc * rows, rows)]
            copies.append(pltpu.make_async_remote_copy(
                src_ref=piece, dst_ref=dst[k].at[j - 1],
                send_sem=send_sems.at[7 * k + j - 1], recv_sem=recv_sems.at[7 * k + j - 1], device_id=(px, py, pc),
                device_id_type=MESH))
    return copies


def _scatter_landing(parts, whole=False):
    return [jax.ShapeDtypeStruct((N_DEV - 1,) + (p.shape if whole else (p.shape[1] // 2, p.shape[2])), p.dtype) for p in parts]


HBM = pl.BlockSpec(memory_space=pltpu.HBM)
SEMAPHORES = pl.BlockSpec(memory_space=pltpu.SEMAPHORE)
SPLIT_COPY = pltpu.CompilerParams(has_side_effects=pltpu.SideEffectType.DATAFLOW_SIDE_EFFECTING)


def _scatter_start(part, whole, name):
    landing, = _scatter_landing([part], whole)

    def body(src_ref, land_ref, send_sems, recv_sems, src_thru, land_thru, token_ref):
        for cp in _scatter([src_ref], [land_ref], send_sems, recv_sems, whole):
            cp.start()
        token_ref[...] = jnp.zeros_like(token_ref)

    return pl.pallas_call(
        body, name=name,
        out_shape=(pltpu.SemaphoreType.DMA((N_DEV - 1,)), pltpu.SemaphoreType.DMA((N_DEV - 1,)), pltpu.HBM(part.shape, part.dtype),
                   pltpu.HBM(landing.shape, landing.dtype), jax.ShapeDtypeStruct((8, 128), F32)),
        in_specs=(HBM, HBM), out_specs=(SEMAPHORES, SEMAPHORES, HBM, HBM, VMEM_WHOLE), input_output_aliases={0: 2, 1: 3},
        compiler_params=SPLIT_COPY,
    )(pltpu.with_memory_space_constraint(part, pltpu.HBM),
      pltpu.with_memory_space_constraint(lax.empty(landing.shape, landing.dtype), pltpu.HBM))


def _scatter_wait(send_sems, recv_sems, part_thru, land_thru, after, whole, name):
    def body(src_ref, land_ref, send_ref, recv_ref, after_ref, src_dead, got_ref):
        for cp in _scatter([src_ref], [land_ref], send_ref, recv_ref, whole):
            cp.wait_send()
            cp.wait_recv()

    return pl.pallas_call(
        body, name=name,
        out_shape=(pltpu.HBM(part_thru.shape, part_thru.dtype), pltpu.HBM(land_thru.shape, land_thru.dtype)),
        in_specs=(HBM, HBM, SEMAPHORES, SEMAPHORES, ANY), out_specs=(HBM, HBM), input_output_aliases={0: 0, 1: 1},
        compiler_params=SPLIT_COPY,
    )(part_thru, land_thru, send_sems, recv_sems, after)


def _sum_pieces(own, landed, behind, name):
    n = len(own)

    def body(*refs):
        for k in range(n):
            got = refs[n + k]
            total = refs[k][...].astype(F32)
            for j in range(N_DEV - 1):
                total = total + got[j].astype(F32)
            refs[2 * n + 1 + k][...] = total

    in_specs, out_specs = [], []
    for o in own:
        in_specs.append(_row(o.shape[0] // 2, o.shape[1]))
    for o in own:
        in_specs.append(pl.BlockSpec((N_DEV - 1, o.shape[0] // 2, o.shape[1]), lambda i: (0, i, 0)))
        out_specs.append(_row(o.shape[0] // 2, o.shape[1]))
    return pl.pallas_call(
        body, name=name, grid=(2,), in_specs=in_specs + [ANY], out_specs=out_specs,
        out_shape=[jax.ShapeDtypeStruct(o.shape, F32) for o in own], compiler_params=_cparams(),
    )(*own, *landed, behind)


def _swap_with_sibling(halves, name):
    n = len(halves)

    def body(*refs):
        x, y, c = lax.axis_index("x"), lax.axis_index("y"), lax.axis_index("c")
        copies = [pltpu.make_async_remote_copy(
            src_ref=refs[k], dst_ref=refs[n + k], send_sem=refs[2 * n].at[k], recv_sem=refs[2 * n + 1].at[k],
            device_id=(x, y, 1 - c), device_id_type=MESH) for k in range(n)]
        for cp in copies:
            cp.start()
        for cp in copies:
            cp.wait()

    return pl.pallas_call(
        body, name=name, in_specs=[ANY] * n, out_specs=[ANY] * n,
        out_shape=[jax.ShapeDtypeStruct(h.shape, h.dtype) for h in halves],
        scratch_shapes=[pltpu.SemaphoreType.DMA((n,)), pltpu.SemaphoreType.DMA((n,))],
    )(*halves)


def _own_piece(part):
    rows = part.shape[1] // 2
    s = 2 * lax.axis_index("x") + lax.axis_index("y")
    return lax.dynamic_slice(part, (s, lax.axis_index("c") * rows, 0), (1, rows, part.shape[2]))[0]


def _both_halves(mine, theirs):
    south = lax.axis_index("c") == 0
    return jnp.concatenate([jnp.where(south, mine, theirs), jnp.where(south, theirs, mine)], axis=0)


def _from_col_shards(g):
    return g.transpose(1, 0, 2).reshape(g.shape[1], -1)


def kernel(x, meta_tokens, attn_norm_g, w_in, attn_sinks, conv_w, conv_b, conv_ln_g, conv_ln_b, attn_out_g, conv_out_g, w_out, ffn_norm_g, w_gate, w_up, w_down, final_norm_g, loss_target, m_meta_tokens, m_attn_norm_g, m_w_in, m_attn_sinks, m_conv_w, m_conv_b, m_conv_ln_g, m_conv_ln_b, m_attn_out_g, m_conv_out_g, m_w_out, m_ffn_norm_g, m_w_gate, m_w_up, m_w_down, m_final_norm_g, v_meta_tokens, v_attn_norm_g, v_w_in, v_attn_sinks, v_conv_w, v_conv_b, v_conv_ln_g, v_conv_ln_b, v_attn_out_g, v_conv_out_g, v_w_out, v_ffn_norm_g, v_w_gate, v_w_up, v_w_down, v_final_norm_g):
    seq = x.shape[1]
    r = -(-(seq + BLOCK) // ROW_QUANTUM) * ROW_QUANTUM
    tm_wide = 768 if seq >= 768 else 256
    shard = 2 * lax.axis_index("x") + lax.axis_index("y")

    conv_w32 = jnp.pad(conv_w[0], ((0, 1), (0, 0)))
    small_shard = jnp.concatenate([meta_tokens, conv_w32.reshape(16, 256)], axis=0)
    g_in, g_small = _gather_weights([_own_slots(w_in[0].T, BF16), _own_slots(small_shard, F32)])
    later = [_own_slots(w, BF16) for w in (w_gate[0].T, w_up[0].T, w_out[0], w_down[0])]
    w_in_t = g_in.reshape(IN_COLS, D_MODEL)
    meta_full = _from_col_shards(g_small[:, 0:N_META])
    cw_full = _from_col_shards(g_small[:, N_META:].reshape(N_SHARD, 32, 128))

    g1, ga, gc, g2 = attn_norm_g, attn_out_g, conv_out_g, ffn_norm_g
    gf = final_norm_g.reshape(1, D_MODEL)
    sinks = attn_sinks[0]

    lead = jnp.concatenate([jnp.zeros((LEAD, D_MODEL), F32), meta_full], axis=0)
    h0, q, kv, cacg = _in_proj(x[0], lead, g1, w_in_t, r, 768)
    oa, lse, *gathered = _attn_fwd(q, kv, sinks, later)
    oc, yc, g_gate, g_up, g_out, g_down = _conv_fwd(cacg, cw_full, conv_b, conv_ln_g, conv_ln_b, 384, gathered)
    wg_t, wu_t, wd_b = g_gate.reshape(D_FF, D_MODEL), g_up.reshape(D_FF, D_MODEL), g_down.reshape(D_FF, D_MODEL)
    w_out_b = g_out.reshape(D_MODEL, D_MODEL)
    h1, hn2 = _out_proj(oa, oc, h0, ga, gc, g2, w_out_b, 768)
    gate, up, act, dh2, dh2b, loss_p, dgf = _ffn_fwd(hn2, h1, loss_target[0], gf, wg_t, wu_t, wd_b, 384)

    def by_shard(dw):
        return dw.reshape(N_SHARD, dw.shape[0] // N_SHARD, D_MODEL)

    dgate, dup, dh1, dg2 = _ffn_bwd(dh2, dh2b, gate, up, h1, g2, wg_t, wu_t, wd_b, 384)
    p_gate, p_up = [by_shard(dw) for dw in _ffn_wgrad_gu(hn2, dgate, dup, 768)]
    p_down = by_shard(_ffn_wgrad_d(act, dh2b, 768))
    doa, doc, dwo, dga, dgc = _out_proj_bwd(dh1, oa, oc, ga, gc, w_out_b, 768)
    p_out = by_shard(dwo)
    dy, dcw, dcb, dlg, dlb, l_gate, l_up = _conv_bwd_params(doc, yc, cacg, conv_ln_g, conv_ln_b, 384, [p_gate, p_up])
    dc = _conv_bwd_data(dy, cacg, cw_full, 384)
    dq, dkv, dkv_meta, dsink, l_out, l_down = _attn_bwd(q, kv, oa, doa, lse, sinks, [p_out, p_down])
    grad_x, dlead, dwi_t, dg1 = _in_proj_bwd(dq, dkv, dkv_meta, dc, dh1, h0, g1, w_in_t, seq, tm_wide)
    in_send, in_recv, p_in_thru, l_in_thru, started = _scatter_start(by_shard(dwi_t), False, "scatter_start")
    small_parts = [dgf, dg1, dg2, dlead[LEAD:BLOCK], dga, dgc, dcb, dlg, dlb, dcw, loss_p, dsink]
    sm_send, sm_recv, pack_thru, packs_thru, _ = _scatter_start(_pack_small(small_parts), True, "gather_small_start")

    big = ("w_gate", "w_up", "w_out", "w_down", "w_in")
    transposed = ("w_in", "w_gate", "w_up")
    grads = {}
    params = {
        "meta_tokens": (meta_tokens, m_meta_tokens, v_meta_tokens), "attn_norm_g": (attn_norm_g, m_attn_norm_g, v_attn_norm_g),
        "w_in": (w_in, m_w_in, v_w_in), "attn_sinks": (attn_sinks, m_attn_sinks, v_attn_sinks), "conv_w": (conv_w, m_conv_w, v_conv_w),
        "conv_b": (conv_b, m_conv_b, v_conv_b), "conv_ln_g": (conv_ln_g, m_conv_ln_g, v_conv_ln_g),
        "conv_ln_b": (conv_ln_b, m_conv_ln_b, v_conv_ln_b), "attn_out_g": (attn_out_g, m_attn_out_g, v_attn_out_g),
        "conv_out_g": (conv_out_g, m_conv_out_g, v_conv_out_g), "w_out": (w_out, m_w_out, v_w_out),
        "ffn_norm_g": (ffn_norm_g, m_ffn_norm_g, v_ffn_norm_g), "w_gate": (w_gate, m_w_gate, v_w_gate), "w_up": (w_up, m_w_up, v_w_up),
        "w_down": (w_down, m_w_down, v_w_down), "final_norm_g": (final_norm_g, m_final_norm_g, v_final_norm_g)}
    names = list(params)
    delta, new_m, new_v = {}, {}, {}

    def finish(group, parts, landed, behind, tag):
        halves = _sum_pieces([_own_piece(p) for p in parts], landed, behind, "sum_pieces_" + tag)
        for name, mine, theirs in zip(group, halves, _swap_with_sibling(halves, "swap_with_sibling_" + tag)):
            flip = (lambda a: a.T) if name in transposed else (lambda a: a)
            g = _both_halves(mine, theirs)
            w, m, v = params[name]
            outs = _adamw(flip(w[0]), g, flip(m[0]), flip(v[0]), "adamw_" + name)
            grads[name], delta[name], new_m[name], new_v[name] = [flip(a)[None] for a in (g, *outs)]
        return outs[0]

    done = finish(big[:4], [p_gate, p_up, p_out, p_down], [l_gate, l_up, l_out, l_down], started, "ffn_out")

    red_names = ("final_norm_g", "attn_norm_g", "ffn_norm_g", "meta_tokens", "attn_out_g", "conv_out_g", "conv_b", "conv_ln_g",
                 "conv_ln_b", "conv_w", "loss", "attn_sinks")
    pack, packs = _scatter_wait(sm_send, sm_recv, pack_thru, packs_thru, done, True, "gather_small_wait")
    red = dict(zip(red_names, _sum_small(pack, packs, small_parts)))
    loss = red["loss"][0, 0]
    for name in ("attn_norm_g", "conv_b", "conv_ln_g", "conv_ln_b", "attn_out_g", "conv_out_g", "ffn_norm_g"):
        grads[name] = red[name]
    grads["final_norm_g"] = red["final_norm_g"].reshape(D_MODEL)
    grads["attn_sinks"] = red["attn_sinks"][:, 0].reshape(1, N_HEADS)
    grads["meta_tokens"] = lax.dynamic_slice_in_dim(red["meta_tokens"], shard * (D_MODEL // N_SHARD), D_MODEL // N_SHARD, axis=1)
    grads["conv_w"] = lax.dynamic_slice_in_dim(
        red["conv_w"][0:CONV_K], shard * (CONV_W // N_SHARD), CONV_W // N_SHARD, axis=1)[None]

    p_in, l_in = _scatter_wait(in_send, in_recv, p_in_thru, l_in_thru, red["loss"], False, "scatter_wait")
    finish(big[4:], [p_in], [l_in], l_in, "in")
    rest = [name for name in names if name not in big]

    def rows_of(a):
        return a.reshape(-1, a.shape[-1])

    small = _adamw_small([rows_of(params[n][0]) for n in rest], [rows_of(grads[n]) for n in rest],
                         [rows_of(params[n][1]) for n in rest], [rows_of(params[n][2]) for n in rest])
    for dst, outs in zip((delta, new_m, new_v), small):
        for name, out in zip(rest, outs):
            dst[name] = out.reshape(params[name][0].shape)

    return (loss, grad_x[None], *[grads[n] for n in names], *[delta[n] for n in names], *[new_m[n] for n in names],
            *[new_v[n] for n in names])
```

```python
import functools
import math

import jax
import jax.numpy as jnp
from jax import lax
from jax.experimental import pallas as pl
from jax.experimental.pallas import tpu as pltpu

F32 = jnp.float32
BF16 = jnp.bfloat16

D_MODEL = 1024
N_META = 16
ATTN_W = 512
CONV_W = 512
HEAD_DIM = 64
N_HEADS = 8
N_KV = 2
GROUP = N_HEADS // N_KV
KV_W = N_KV * HEAD_DIM
BLOCK = 128
LEAD = BLOCK - N_META
CONV_K = 31
D_FF = 2816
IN_COLS = ATTN_W + 2 * KV_W + 2 * CONV_W
Q0, KV0, C0 = 0, ATTN_W, ATTN_W + 2 * KV_W
NORM_EPS = 1e-5
SCALE = 1.0 / math.sqrt(HEAD_DIM)
SLOPES = tuple(2.0 ** (-(8.0 / N_HEADS) * (h + 1)) for h in range(N_HEADS))
NEG = -1e30

ADAM_LR, ADAM_B1, ADAM_B2, ADAM_EPS, ADAM_WD, ADAM_STEP = 0.001, 0.9, 0.999, 1e-08, 0.01, 10

N_SHARD = 4
N_DEV = 8
ROW_QUANTUM = 768
HALO = 32
CONV_CHUNK = 32
FF_CHUNK = 256
FF_CHUNKS = tuple(slice(c, c + FF_CHUNK) for c in range(0, D_FF, FF_CHUNK))
VMEM_LIMIT = 60 * 1024 * 1024


def _cparams(n_axes=1):
    return pltpu.CompilerParams(dimension_semantics=("arbitrary",) * n_axes, vmem_limit_bytes=VMEM_LIMIT)


def _dot(a, b):
    return jnp.dot(a, b, preferred_element_type=F32)


def _dot_nt(a, b):
    return lax.dot_general(a, b, (((1,), (1,)), ((), ())), preferred_element_type=F32)


def _dot_tn(a, b):
    return lax.dot_general(a, b, (((0,), (0,)), ((), ())), preferred_element_type=F32)


def _sigmoid(x):
    return 1.0 / (1.0 + jnp.exp(-x))


def _row(tm, n):
    return pl.BlockSpec((tm, n), lambda i: (i, 0))


def _const(shape):
    return pl.BlockSpec(shape, lambda i: (0,) * len(shape))


def _resident(shape):
    return pl.BlockSpec(shape, lambda i: (0,) * len(shape), pipeline_mode=pl.Buffered(1))


def _rms_fwd(x, g):
    rstd = lax.rsqrt(jnp.mean(x * x, axis=-1, keepdims=True) + NORM_EPS)
    xhat = x * rstd
    return xhat * g, xhat, rstd


def _rms_bwd(dy, xhat, rstd, g):
    dxh = dy * g
    dx = rstd * (dxh - xhat * jnp.mean(dxh * xhat, axis=-1, keepdims=True))
    return dx, dy * xhat


def _in_proj(x, lead, g1, w_in_t, r, tm):
    seq = x.shape[0]
    n_sub = tm // BLOCK

    def body(*refs):
        x_refs = refs[:n_sub]
        lead_ref, g_ref, w_ref, h0_ref, q_ref, kv_ref, c_ref = refs[n_sub:]
        i = pl.program_id(0)
        pieces = []
        for k, x_ref in enumerate(x_refs):
            at = jnp.zeros((BLOCK, D_MODEL), jnp.int32) + (i * tm + (k - 1) * BLOCK)
            piece = jnp.where((at >= 0) & (at < seq), x_ref[...], 0.0)
            pieces.append(jnp.where(at < 0, lead_ref[...], piece) if k == 0 else piece)
        h = jnp.concatenate(pieces, axis=0)
        h0_ref[...] = h
        hn = _rms_fwd(h, g_ref[...])[0].astype(BF16)
        q_ref[...] = _dot_nt(hn, w_ref[Q0:KV0, :]).astype(BF16)
        kv_ref[...] = _dot_nt(hn, w_ref[KV0:C0, :]).astype(BF16)
        c_ref[...] = _dot_nt(hn, w_ref[C0:IN_COLS, :])

    def x_block(k):
        return pl.BlockSpec((BLOCK, D_MODEL), lambda i: (jnp.clip(n_sub * i - 1 + k, 0, seq // BLOCK - 1), 0))

    return pl.pallas_call(
        body, name="in_proj", grid=(r // tm,),
        in_specs=[x_block(k) for k in range(n_sub)] + [_const((BLOCK, D_MODEL)), _const((1, D_MODEL)), _const((IN_COLS, D_MODEL))],
        out_specs=[_row(tm, D_MODEL), _row(tm, ATTN_W), _row(tm, 2 * KV_W), _row(tm, 2 * CONV_W)],
        out_shape=[jax.ShapeDtypeStruct((r, D_MODEL), F32), jax.ShapeDtypeStruct((r, ATTN_W), BF16),
                   jax.ShapeDtypeStruct((r, 2 * KV_W), BF16), jax.ShapeDtypeStruct((r, 2 * CONV_W), F32)],
        compiler_params=_cparams(),
    )(*[x] * n_sub, lead, g1, w_in_t)


def _attn_bias_init(bias_ref, late_ref):
    row = lax.broadcasted_iota(jnp.int32, (GROUP * BLOCK, BLOCK), 0) & (BLOCK - 1)
    col = lax.broadcasted_iota(jnp.int32, (GROUP * BLOCK, BLOCK), 1)
    late_ref[...] = jnp.where(col > row, 1.0, 0.0)
    for g in range(N_KV):
        slope = jnp.concatenate([jnp.zeros((BLOCK, BLOCK), F32) + SLOPES[g * GROUP + j] for j in range(GROUP)], axis=0)
        bias_ref[g, :, 0:BLOCK] = jnp.where(col >= LEAD, 0.0, NEG)
        bias_ref[g, :, BLOCK:2 * BLOCK] = -slope * jnp.where(col > row, row - col + BLOCK, row - col).astype(F32)


def _attn_block_bias(late_ref, i):
    late = late_ref[...]
    meta0 = jnp.where(i == 0, NEG, 0.0)
    no_prev = jnp.where(i >= 2, 0.0, NEG)
    no_cur = jnp.where(i >= 1, 0.0, NEG)
    return late * meta0, late * no_prev + no_cur


def _attn_logits(s3, bias_ref, block_bias, prev_part, g):
    meta = s3[:, 0:BLOCK] * SCALE + (bias_ref[g, :, 0:BLOCK] + block_bias[0])
    band = jnp.where(prev_part, s3[:, BLOCK:2 * BLOCK], s3[:, 2 * BLOCK:3 * BLOCK]) * SCALE + (bias_ref[g, :, BLOCK:2 * BLOCK] + block_bias[1])
    return meta, band


def _split_band(meta, band, prev_part):
    return jnp.concatenate([meta, jnp.where(prev_part, band, 0.0), jnp.where(prev_part, 0.0, band)], axis=1)


def _head_rows(vals):
    return jnp.concatenate([jnp.zeros((BLOCK, BLOCK), F32) + v for v in vals], axis=0)


def _stack_heads(ref, g):
    return jnp.concatenate([ref[:, (g * GROUP + j) * HEAD_DIM:(g * GROUP + j + 1) * HEAD_DIM] for j in range(GROUP)], axis=0)


def _kv_cat(kvm_ref, kvp_ref, kvc_ref, g):
    ks = slice(g * HEAD_DIM, (g + 1) * HEAD_DIM)
    vs = slice(KV_W + g * HEAD_DIM, KV_W + (g + 1) * HEAD_DIM)
    kcat = jnp.concatenate([kvm_ref[:, ks], kvp_ref[:, ks], kvc_ref[:, ks]], axis=0)
    vcat = jnp.concatenate([kvm_ref[:, vs], kvp_ref[:, vs], kvc_ref[:, vs]], axis=0)
    return kcat, vcat


def _attn_fwd(q, kv, sinks, gathered):
    r = q.shape[0]
    nb = r // BLOCK
    n = len(gathered)

    def body(sink_ref, q_ref, kvc_ref, kvp_ref, kvm_ref, *rest):
        o_ref, lse_ref = rest[n:n + 2]
        dst = rest[n + 2:2 * n + 2]
        bias_ref, late_ref, send_sems, recv_sems = rest[2 * n + 2:]
        i = pl.program_id(0)

        @pl.when(i == 0)
        def _():
            for cp in _gather_ici(dst, send_sems, recv_sems)[0]:
                cp.start()
            _attn_bias_init(bias_ref, late_ref)

        lane = lax.broadcasted_iota(jnp.int32, (BLOCK, BLOCK), 1)
        lse_tile = jnp.zeros((BLOCK, BLOCK), F32)
        block_bias = _attn_block_bias(late_ref, i)
        prev_part = late_ref[...] > 0.5
        for g in range(N_KV):
            kcat, vcat = _kv_cat(kvm_ref, kvp_ref, kvc_ref, g)
            heads = range(g * GROUP, (g + 1) * GROUP)
            meta, band = _attn_logits(_dot_nt(_stack_heads(q_ref, g), kcat), bias_ref, block_bias, prev_part, g)
            sink = _head_rows([sink_ref[h] for h in heads])
            m = jnp.maximum(jnp.max(jnp.maximum(meta, band), axis=-1, keepdims=True), sink)
            p_meta, p_band = jnp.exp(meta - m), jnp.exp(band - m)
            l = jnp.sum(p_meta + p_band, axis=-1, keepdims=True) + jnp.exp(sink - m)
            o = _dot(_split_band(p_meta, p_band, prev_part).astype(BF16), vcat) * (1.0 / l)[:, 0:HEAD_DIM]
            lse = m + jnp.log(l)
            for j, h in enumerate(heads):
                o_ref[:, h * HEAD_DIM:(h + 1) * HEAD_DIM] = o[j * BLOCK:(j + 1) * BLOCK].astype(BF16)
                lse_tile = jnp.where(lane == h, lse[j * BLOCK:(j + 1) * BLOCK], lse_tile)
        lse_ref[...] = lse_tile

        @pl.when(i == nb - 1)
        def _():
            sends, arrivals = _gather_ici(dst, send_sems, recv_sems)
            for cp in arrivals:
                cp.wait_recv()
            for cp in sends:
                cp.wait_send()

    return pl.pallas_call(
        body, name="attn_fwd", grid=(nb,),
        in_specs=[pl.BlockSpec(memory_space=pltpu.SMEM), _row(BLOCK, ATTN_W), _row(BLOCK, 2 * KV_W),
                  pl.BlockSpec((BLOCK, 2 * KV_W), lambda i: (jnp.maximum(i - 1, 0), 0)), _const((BLOCK, 2 * KV_W))] + [ANY] * n,
        out_specs=[_row(BLOCK, ATTN_W), _row(BLOCK, BLOCK)] + [ANY] * n,
        out_shape=[jax.ShapeDtypeStruct((r, ATTN_W), BF16), jax.ShapeDtypeStruct((r, BLOCK), F32)]
        + [jax.ShapeDtypeStruct(g.shape, g.dtype) for g in gathered],
        input_output_aliases={5 + k: 2 + k for k in range(n)},
        scratch_shapes=[pltpu.VMEM((N_KV, GROUP * BLOCK, 2 * BLOCK), F32), pltpu.VMEM((GROUP * BLOCK, BLOCK), F32),
                        pltpu.SemaphoreType.DMA((3 * n,)), pltpu.SemaphoreType.DMA((3 * n,))],
        compiler_params=_cparams(),
    )(sinks, q, kv, kv, kv, *gathered)


def _shifted_copies(ub_ref, win):
    w = win.shape[0]
    ub_ref[0] = win
    for b in range(1, 8):
        ub_ref[b] = pltpu.roll(win, shift=w - b, axis=0)


def _conv_chunk(ub_ref, w_ref, r0, shifts):
    acc = jnp.zeros((CONV_CHUNK, CONV_W), F32)
    for j in range(CONV_K):
        a, b = divmod(shifts[j], 8)
        acc = acc + w_ref[j:j + 1, :] * ub_ref[b, pl.ds(r0 + 8 * a, CONV_CHUNK), :]
    return acc


FWD_SHIFTS = tuple(HALO - (CONV_K - 1) + j for j in range(CONV_K))
BWD_SHIFTS = tuple(CONV_K - 1 - j for j in range(CONV_K))


def _glu_window(cp_ref, c_ref, i):
    tile = c_ref[:, 0:CONV_W] * _sigmoid(c_ref[:, CONV_W:2 * CONV_W])
    halo = cp_ref[:, 0:CONV_W] * _sigmoid(cp_ref[:, CONV_W:2 * CONV_W])
    first = (jnp.zeros((HALO, CONV_W), jnp.int32) + i) == 0
    return jnp.concatenate([jnp.where(first, 0.0, halo), tile], axis=0)


def _halo_before(tm, n):
    return pl.BlockSpec((HALO, n), lambda i: (jnp.maximum(i * (tm // HALO) - 1, 0), 0))


def _conv_fwd(cacg, cw, cb, lg, lb, tm, gathered):
    r = cacg.shape[0]
    n = len(gathered)

    def body(c_ref, cp_ref, w_ref, cb_ref, lg_ref, lb_ref, *rest):
        o_ref, y_ref = rest[n:n + 2]
        dst = rest[n + 2:2 * n + 2]
        ub_ref, send_sems, recv_sems = rest[2 * n + 2:]
        i = pl.program_id(0)

        @pl.when(i == 0)
        def _():
            for cp in _gather_d2d(dst, send_sems, recv_sems)[0]:
                cp.start()

        _shifted_copies(ub_ref, _glu_window(cp_ref, c_ref, i))

        def chunk(ci, carry):
            r0 = pl.multiple_of(ci * CONV_CHUNK, CONV_CHUNK)
            y = _conv_chunk(ub_ref, w_ref, r0, FWD_SHIFTS) + cb_ref[...]
            yc = y - jnp.mean(y, axis=-1, keepdims=True)
            rs = lax.rsqrt(jnp.mean(yc * yc, axis=-1, keepdims=True) + NORM_EPS)
            yn = yc * rs * lg_ref[...] + lb_ref[...]
            o_ref[pl.ds(r0, CONV_CHUNK), :] = (yn * _sigmoid(yn)).astype(BF16)
            y_ref[pl.ds(r0, CONV_CHUNK), :] = y
            return carry

        lax.fori_loop(0, tm // CONV_CHUNK, chunk, 0, unroll=6)

        @pl.when(i == r // tm - 1)
        def _():
            sends, arrivals = _gather_d2d(dst, send_sems, recv_sems)
            for cp in arrivals:
                cp.wait_recv()
            for cp in sends:
                cp.wait_send()

    return pl.pallas_call(
        body, name="conv_fwd", grid=(r // tm,),
        in_specs=[_row(tm, 2 * CONV_W), _halo_before(tm, 2 * CONV_W), _const((32, CONV_W)), _const((1, CONV_W)),
                  _const((1, CONV_W)), _const((1, CONV_W))] + [ANY] * n,
        out_specs=[_row(tm, CONV_W), _row(tm, CONV_W)] + [ANY] * n,
        out_shape=[jax.ShapeDtypeStruct((r, CONV_W), BF16), jax.ShapeDtypeStruct((r, CONV_W), F32)]
        + [jax.ShapeDtypeStruct(g.shape, g.dtype) for g in gathered],
        input_output_aliases={6 + k: 2 + k for k in range(n)},
        scratch_shapes=[pltpu.VMEM((8, tm + HALO, CONV_W), F32), pltpu.SemaphoreType.DMA((3 * n,)), pltpu.SemaphoreType.DMA((3 * n,))],
        compiler_params=_cparams(),
    )(cacg, cacg, cw, cb, lg, lb, *gathered)


def _out_proj(oa, oc, h0, ga, gc, g2, w_out_b, tm):
    r = h0.shape[0]

    def body(oa_ref, oc_ref, h_ref, ga_ref, gc_ref, g2_ref, w_ref, h1_ref, hn2_ref):
        ma = _rms_fwd(oa_ref[...].astype(F32), ga_ref[...])[0].astype(BF16)
        mc = _rms_fwd(oc_ref[...].astype(F32), gc_ref[...])[0].astype(BF16)
        h1 = h_ref[...] + _dot(jnp.concatenate([ma, mc], axis=1), w_ref[...])
        h1_ref[...] = h1
        hn2_ref[...] = _rms_fwd(h1, g2_ref[...])[0].astype(BF16)

    return pl.pallas_call(
        body, name="out_proj", grid=(r // tm,),
        in_specs=[_row(tm, ATTN_W), _row(tm, CONV_W), _row(tm, D_MODEL), _const((1, ATTN_W)), _const((1, CONV_W)),
                  _const((1, D_MODEL)), _const((D_MODEL, D_MODEL))],
        out_specs=[_row(tm, D_MODEL), _row(tm, D_MODEL)],
        out_shape=[jax.ShapeDtypeStruct((r, D_MODEL), F32), jax.ShapeDtypeStruct((r, D_MODEL), BF16)],
        compiler_params=_cparams(),
    )(oa, oc, h0, ga, gc, g2, w_out_b)


def _ffn_fwd(hn2, h1, target, gf, wg_t, wu_t, wd_b, tm):
    r = h1.shape[0]
    seq = target.shape[0]
    n_sub = tm // BLOCK

    def body(hn_ref, h1_ref, *rest):
        t_refs = rest[:n_sub]
        gf_ref, wg_ref, wu_ref, wd_ref, gate_ref, up_ref, act_ref, dh2_ref, dh2b_ref, loss_ref, dgf_ref = rest[n_sub:]
        i = pl.program_id(0)

        @pl.when(i == 0)
        def _():
            loss_ref[...] = jnp.zeros_like(loss_ref)
            dgf_ref[...] = jnp.zeros_like(dgf_ref)

        hn = hn_ref[...]
        for cs in FF_CHUNKS:
            gate = _dot_nt(hn, wg_ref[cs, :])
            up = _dot_nt(hn, wu_ref[cs, :])
            gate_ref[:, cs] = gate.astype(BF16)
            up_ref[:, cs] = up.astype(BF16)
            act_ref[:, cs] = (gate * _sigmoid(gate) * up).astype(BF16)
        y, xhat, rstd = _rms_fwd(h1_ref[...] + _dot(act_ref[...], wd_ref[...]), gf_ref[...])
        rows = lax.broadcasted_iota(jnp.int32, (tm, D_MODEL), 0) + i * tm
        real = (rows >= BLOCK) & (rows < BLOCK + seq)
        err = jnp.where(real, y - jnp.concatenate([t[...] for t in t_refs], axis=0), 0.0)
        loss_ref[...] += jnp.sum(err * err) * (0.5 / D_MODEL)
        dy = err * (1.0 / D_MODEL)
        dh2, dg_rows = _rms_bwd(dy, xhat, rstd, gf_ref[...])
        dgf_ref[...] += jnp.sum(dg_rows, axis=0, keepdims=True)
        dh2_ref[...] = dh2
        dh2b_ref[...] = dh2.astype(BF16)

    def target_block(k):
        return pl.BlockSpec((BLOCK, D_MODEL), lambda i: (jnp.clip(n_sub * i - 1 + k, 0, seq // BLOCK - 1), 0))

    return pl.pallas_call(
        body, name="ffn_fwd", grid=(r // tm,),
        in_specs=[_row(tm, D_MODEL), _row(tm, D_MODEL)] + [target_block(k) for k in range(n_sub)]
        + [_const((1, D_MODEL))] + [_resident((D_FF, D_MODEL))] * 3,
        out_specs=[_row(tm, D_FF)] * 3 + [_row(tm, D_MODEL), _row(tm, D_MODEL), _const((1, BLOCK)), _const((1, D_MODEL))],
        out_shape=[jax.ShapeDtypeStruct((r, D_FF), BF16)] * 3
        + [jax.ShapeDtypeStruct((r, D_MODEL), F32), jax.ShapeDtypeStruct((r, D_MODEL), BF16),
           jax.ShapeDtypeStruct((1, BLOCK), F32), jax.ShapeDtypeStruct((1, D_MODEL), F32)],
        compiler_params=_cparams(),
    )(hn2, h1, *[target] * n_sub, gf, wg_t, wu_t, wd_b)


def _ffn_bwd(dh2, dh2b, gate, up, h1, g2, wg_t, wu_t, wd_b, tm):
    r = h1.shape[0]

    def body(dh2_ref, dh2b_ref, gate_ref, up_ref, h1_ref, g2_ref, wg_ref, wu_ref, wd_ref, dgate_ref, dup_ref, dh1_ref, dg2_ref):
        @pl.when(pl.program_id(0) == 0)
        def _():
            dg2_ref[...] = jnp.zeros_like(dg2_ref)

        dyb = dh2b_ref[...]
        for cs in FF_CHUNKS:
            dact = _dot_nt(dyb, wd_ref[cs, :])
            gate = gate_ref[:, cs].astype(F32)
            up = up_ref[:, cs].astype(F32)
            sg = _sigmoid(gate)
            dgate_ref[:, cs] = (dact * up * (sg * (1.0 + gate * (1.0 - sg)))).astype(BF16)
            dup_ref[:, cs] = (dact * (gate * sg)).astype(BF16)
        dhn = _dot(dgate_ref[...], wg_ref[...]) + _dot(dup_ref[...], wu_ref[...])
        _, xhat, rstd = _rms_fwd(h1_ref[...], g2_ref[...])
        dx, dg_rows = _rms_bwd(dhn, xhat, rstd, g2_ref[...])
        dg2_ref[...] += jnp.sum(dg_rows, axis=0, keepdims=True)
        dh1_ref[...] = dh2_ref[...] + dx

    return pl.pallas_call(
        body, name="ffn_bwd", grid=(r // tm,),
        in_specs=[_row(tm, D_MODEL), _row(tm, D_MODEL), _row(tm, D_FF), _row(tm, D_FF), _row(tm, D_MODEL), _const((1, D_MODEL))]
        + [_resident((D_FF, D_MODEL))] * 3,
        out_specs=[_row(tm, D_FF), _row(tm, D_FF), _row(tm, D_MODEL), _const((1, D_MODEL))],
        out_shape=[jax.ShapeDtypeStruct((r, D_FF), BF16), jax.ShapeDtypeStruct((r, D_FF), BF16),
                   jax.ShapeDtypeStruct((r, D_MODEL), F32), jax.ShapeDtypeStruct((1, D_MODEL), F32)],
        compiler_params=_cparams(),
    )(dh2, dh2b, gate, up, h1, g2, wg_t, wu_t, wd_b)


FF_HALF = D_FF // 2


def _ffn_wgrad_gu(hn2, dgate, dup, tk):
    r = hn2.shape[0]
    n_k = r // tk

    def body(hn_ref, dg_ref, du_ref, wg_ref, wu_ref, accg_ref, accu_ref):
        k = pl.program_id(1)

        @pl.when(k == 0)
        def _():
            accg_ref[...] = jnp.zeros_like(accg_ref)
            accu_ref[...] = jnp.zeros_like(accu_ref)

        hn = hn_ref[...]
        accg_ref[...] += _dot_tn(dg_ref[...], hn)
        accu_ref[...] += _dot_tn(du_ref[...], hn)

        @pl.when(k == n_k - 1)
        def _():
            wg_ref[...] = accg_ref[...].astype(BF16)
            wu_ref[...] = accu_ref[...].astype(BF16)

    col = pl.BlockSpec((tk, FF_HALF), lambda j, k: (k, j))
    out = pl.BlockSpec((FF_HALF, D_MODEL), lambda j, k: (j, 0))
    return pl.pallas_call(
        body, name="ffn_wgrad_gu", grid=(2, n_k),
        in_specs=[pl.BlockSpec((tk, D_MODEL), lambda j, k: (k, 0)), col, col],
        out_specs=[out, out],
        out_shape=[jax.ShapeDtypeStruct((D_FF, D_MODEL), BF16)] * 2,
        scratch_shapes=[pltpu.VMEM((FF_HALF, D_MODEL), F32)] * 2,
        compiler_params=_cparams(2),
    )(hn2, dgate, dup)


def _ffn_wgrad_d(act, dh2b, tk):
    r = act.shape[0]
    n_k = r // tk

    def body(a_ref, dy_ref, wd_ref, acc_ref):
        k = pl.program_id(1)

        @pl.when(k == 0)
        def _():
            acc_ref[...] = jnp.zeros_like(acc_ref)

        acc_ref[...] += _dot_tn(a_ref[...], dy_ref[...])

        @pl.when(k == n_k - 1)
        def _():
            wd_ref[...] = acc_ref[...].astype(BF16)

    return pl.pallas_call(
        body, name="ffn_wgrad_d", grid=(2, n_k),
        in_specs=[pl.BlockSpec((tk, FF_HALF), lambda j, k: (k, j)), pl.BlockSpec((tk, D_MODEL), lambda j, k: (k, 0))],
        out_specs=pl.BlockSpec((FF_HALF, D_MODEL), lambda j, k: (j, 0)),
        out_shape=jax.ShapeDtypeStruct((D_FF, D_MODEL), BF16),
        scratch_shapes=[pltpu.VMEM((FF_HALF, D_MODEL), F32)],
        compiler_params=_cparams(2),
    )(act, dh2b)


def _out_proj_bwd(dh1, oa, oc, ga, gc, w_out_b, tm):
    r = dh1.shape[0]

    def body(dh_ref, oa_ref, oc_ref, ga_ref, gc_ref, w_ref, doa_ref, doc_ref, dw_ref, dga_ref, dgc_ref, acc_ref):
        i = pl.program_id(0)

        @pl.when(i == 0)
        def _():
            acc_ref[...] = jnp.zeros_like(acc_ref)
            dga_ref[...] = jnp.zeros_like(dga_ref)
            dgc_ref[...] = jnp.zeros_like(dgc_ref)

        dhb = dh_ref[...].astype(BF16)
        dmix = _dot_nt(dhb, w_ref[...])
        ma, xa, ra = _rms_fwd(oa_ref[...].astype(F32), ga_ref[...])
        mc, xc, rc = _rms_fwd(oc_ref[...].astype(F32), gc_ref[...])
        acc_ref[...] += _dot_tn(jnp.concatenate([ma.astype(BF16), mc.astype(BF16)], axis=1), dhb)

        @pl.when(i == r // tm - 1)
        def _():
            dw_ref[...] = acc_ref[...].astype(BF16)

        doa, dga_rows = _rms_bwd(dmix[:, 0:ATTN_W], xa, ra, ga_ref[...])
        doc, dgc_rows = _rms_bwd(dmix[:, ATTN_W:ATTN_W + CONV_W], xc, rc, gc_ref[...])
        doa_ref[...] = doa.astype(BF16)
        doc_ref[...] = doc.astype(BF16)
        dga_ref[...] += jnp.sum(dga_rows, axis=0, keepdims=True)
        dgc_ref[...] += jnp.sum(dgc_rows, axis=0, keepdims=True)

    return pl.pallas_call(
        body, name="out_proj_bwd", grid=(r // tm,),
        in_specs=[_row(tm, D_MODEL), _row(tm, ATTN_W), _row(tm, CONV_W), _const((1, ATTN_W)), _const((1, CONV_W)),
                  _const((D_MODEL, D_MODEL))],
        out_specs=[_row(tm, ATTN_W), _row(tm, CONV_W), _const((D_MODEL, D_MODEL)), _const((1, ATTN_W)), _const((1, CONV_W))],
        out_shape=[jax.ShapeDtypeStruct((r, ATTN_W), BF16), jax.ShapeDtypeStruct((r, CONV_W), BF16),
                   jax.ShapeDtypeStruct((D_MODEL, D_MODEL), BF16), jax.ShapeDtypeStruct((1, ATTN_W), F32),
                   jax.ShapeDtypeStruct((1, CONV_W), F32)],
        scratch_shapes=[pltpu.VMEM((D_MODEL, D_MODEL), F32)],
        compiler_params=_cparams(),
    )(dh1, oa, oc, ga, gc, w_out_b)


def _conv_bwd_params(doc, y, cacg, lg, lb, tm, parts):
    r = cacg.shape[0]
    n_steps = r // tm
    n = len(parts)

    def body(do_ref, y_ref, c_ref, cp_ref, lg_ref, lb_ref, *rest):
        src = rest[:n]
        dy_ref, dcw_ref, dcb_ref, dlg_ref, dlb_ref = rest[n:n + 5]
        dst = rest[n + 5:2 * n + 5]
        ub_ref, accw_ref, send_sems, recv_sems = rest[2 * n + 5:]
        i = pl.program_id(0)

        @pl.when(i == 0)
        def _():
            for cp in _scatter(src, dst, send_sems, recv_sems):
                cp.start()
            accw_ref[...] = jnp.zeros_like(accw_ref)
            dcb_ref[...] = jnp.zeros_like(dcb_ref)
            dlg_ref[...] = jnp.zeros_like(dlg_ref)
            dlb_ref[...] = jnp.zeros_like(dlb_ref)

        _shifted_copies(ub_ref, _glu_window(cp_ref, c_ref, i))

        def chunk(ci, carry):
            r0 = pl.multiple_of(ci * CONV_CHUNK, CONV_CHUNK)
            y = y_ref[pl.ds(r0, CONV_CHUNK), :]
            yc = y - jnp.mean(y, axis=-1, keepdims=True)
            rs = lax.rsqrt(jnp.mean(yc * yc, axis=-1, keepdims=True) + NORM_EPS)
            xhat = yc * rs
            yn = xhat * lg_ref[...] + lb_ref[...]
            sg = _sigmoid(yn)
            dyn = do_ref[pl.ds(r0, CONV_CHUNK), :].astype(F32) * (sg * (1.0 + yn * (1.0 - sg)))
            dlg_ref[...] += jnp.sum(dyn * xhat, axis=0, keepdims=True)
            dlb_ref[...] += jnp.sum(dyn, axis=0, keepdims=True)
            dxh = dyn * lg_ref[...]
            dy = rs * (dxh - jnp.mean(dxh, axis=-1, keepdims=True) - xhat * jnp.mean(dxh * xhat, axis=-1, keepdims=True))
            dcb_ref[...] += jnp.sum(dy, axis=0, keepdims=True)
            dy_ref[pl.ds(r0, CONV_CHUNK), :] = dy
            for j in range(CONV_K):
                a, b = divmod(FWD_SHIFTS[j], 8)
                prod = dy * ub_ref[b, pl.ds(r0 + 8 * a, CONV_CHUNK), :]
                accw_ref[j] += jnp.sum(prod.reshape(CONV_CHUNK // 8, 8, CONV_W), axis=0)
            return carry

        lax.fori_loop(0, tm // CONV_CHUNK, chunk, 0, unroll=6)

        @pl.when(i == n_steps - 1)
        def _():
            for j in range(32):
                dcw_ref[j:j + 1, :] = jnp.sum(accw_ref[j], axis=0, keepdims=True)
            for cp in _scatter(src, dst, send_sems, recv_sems):
                cp.wait()

    vec = _const((1, CONV_W))
    return pl.pallas_call(
        body, name="conv_bwd_params", grid=(n_steps,),
        in_specs=[_row(tm, CONV_W), _row(tm, CONV_W), _row(tm, 2 * CONV_W), _halo_before(tm, 2 * CONV_W), vec, vec] + [ANY] * n,
        out_specs=[_row(tm, CONV_W), _const((32, CONV_W)), vec, vec, vec] + [ANY] * n,
        out_shape=[jax.ShapeDtypeStruct((r, CONV_W), F32), jax.ShapeDtypeStruct((32, CONV_W), F32)]
        + [jax.ShapeDtypeStruct((1, CONV_W), F32)] * 3 + _scatter_landing(parts),
        scratch_shapes=[pltpu.VMEM((8, tm + HALO, CONV_W), F32), pltpu.VMEM((32, 8, CONV_W), F32),
                        pltpu.SemaphoreType.DMA((7 * n,)), pltpu.SemaphoreType.DMA((7 * n,))],
        compiler_params=_cparams(),
    )(doc, y, cacg, cacg, lg, lb, *parts)


def _conv_bwd_data(dy, cacg, cw, tm):
    r = cacg.shape[0]
    n_steps = r // tm

    def body(dy_ref, dyn_ref, c_ref, w_ref, dc_ref, ub_ref):
        last = (jnp.zeros((HALO, CONV_W), jnp.int32) + pl.program_id(0)) == n_steps - 1
        win = jnp.concatenate([dy_ref[...], jnp.where(last, 0.0, dyn_ref[...])], axis=0)
        _shifted_copies(ub_ref, win)

        def chunk(ci, carry):
            r0 = pl.multiple_of(ci * CONV_CHUNK, CONV_CHUNK)
            du = _conv_chunk(ub_ref, w_ref, r0, BWD_SHIFTS)
            ca = c_ref[pl.ds(r0, CONV_CHUNK), 0:CONV_W]
            sg = _sigmoid(c_ref[pl.ds(r0, CONV_CHUNK), CONV_W:2 * CONV_W])
            dc_ref[pl.ds(r0, CONV_CHUNK), 0:CONV_W] = (du * sg).astype(BF16)
            dc_ref[pl.ds(r0, CONV_CHUNK), CONV_W:2 * CONV_W] = (du * ca * sg * (1.0 - sg)).astype(BF16)
            return carry

        lax.fori_loop(0, tm // CONV_CHUNK, chunk, 0, unroll=6)

    halo_after = pl.BlockSpec((HALO, CONV_W), lambda i: (jnp.minimum((i + 1) * (tm // HALO), r // HALO - 1), 0))
    return pl.pallas_call(
        body, name="conv_bwd_data", grid=(n_steps,),
        in_specs=[_row(tm, CONV_W), halo_after, _row(tm, 2 * CONV_W), _const((32, CONV_W))],
        out_specs=_row(tm, 2 * CONV_W),
        out_shape=jax.ShapeDtypeStruct((r, 2 * CONV_W), BF16),
        scratch_shapes=[pltpu.VMEM((8, tm + HALO, CONV_W), F32)],
        compiler_params=_cparams(),
    )(dy, dy, cacg, cw)


def _attn_bwd(q, kv, o, do, lse, sinks, parts):
    r = q.shape[0]
    nb = r // BLOCK
    n = len(parts)

    def body(sink_ref, q_ref, kvc_ref, kvp_ref, kvm_ref, o_ref, do_ref, lse_ref, *rest):
        src = rest[:n]
        dq_ref, dkv_ref, dmeta_ref, dsink_ref = rest[n:n + 4]
        dst = rest[n + 4:2 * n + 4]
        hold_ref, bias_ref, late_ref, send_sems, recv_sems = rest[2 * n + 4:]
        i = pl.program_id(0)

        @pl.when(i == 0)
        def _():
            for cp in _scatter(src, dst, send_sems, recv_sems):
                cp.start()
            _attn_bias_init(bias_ref, late_ref)
            dmeta_ref[...] = jnp.zeros_like(dmeta_ref)
            dsink_ref[...] = jnp.zeros_like(dsink_ref)
            hold_ref[...] = jnp.zeros_like(hold_ref)

        @pl.when(i < nb)
        def _():
            lane = lax.broadcasted_iota(jnp.int32, (BLOCK, BLOCK), 1)
            lse_tile = lse_ref[...]
            zero = jnp.zeros((BLOCK, BLOCK), F32)
            block_bias = _attn_block_bias(late_ref, i)
            prev_part = late_ref[...] > 0.5
            for g in range(N_KV):
                kcat, vcat = _kv_cat(kvm_ref, kvp_ref, kvc_ref, g)
                heads = range(g * GROUP, (g + 1) * GROUP)
                qs = _stack_heads(q_ref, g)
                dosb = _stack_heads(do_ref, g)
                dos = dosb.astype(F32)
                lse = jnp.concatenate(
                    [jnp.sum(jnp.where(lane == h, lse_tile, 0.0), axis=-1, keepdims=True) + zero for h in heads], axis=0)
                delta = jnp.sum(dos * _stack_heads(o_ref, g).astype(F32), axis=-1, keepdims=True) + jnp.zeros((GROUP * BLOCK, BLOCK), F32)
                band_bias = bias_ref[g, :, BLOCK:2 * BLOCK] + block_bias[1]
                bias = [bias_ref[g, :, 0:BLOCK] + block_bias[0], jnp.where(prev_part, band_bias, NEG), jnp.where(prev_part, NEG, band_bias)]
                s = _dot_nt(qs, kcat)
                dp = _dot_nt(dosb, vcat)
                ps = [jnp.exp(s[:, k * BLOCK:(k + 1) * BLOCK] * SCALE + bias[k] - lse) for k in range(3)]
                p = jnp.concatenate(ps, axis=1)
                ds = jnp.concatenate(
                    [(ps[k] * (dp[:, k * BLOCK:(k + 1) * BLOCK] - delta)) * SCALE for k in range(3)], axis=1).astype(BF16)
                sink_term = jnp.exp(_head_rows([sink_ref[h] for h in heads]) - lse)[:, 0:1] * delta[:, 0:1]
                dq = _dot(ds, kcat).astype(BF16)
                for j, h in enumerate(heads):
                    dsink_ref[h:h + 1, :] += -jnp.sum(sink_term[j * BLOCK:(j + 1) * BLOCK])
                    dq_ref[:, h * HEAD_DIM:(h + 1) * HEAD_DIM] = dq[j * BLOCK:(j + 1) * BLOCK]
                dk_t = _dot_tn(qs, ds)
                dv_t = _dot_tn(dosb, p.astype(BF16))
                ks = slice(g * HEAD_DIM, (g + 1) * HEAD_DIM)
                vs = slice(KV_W + g * HEAD_DIM, KV_W + (g + 1) * HEAD_DIM)
                for sl, grad_t in ((ks, dk_t), (vs, dv_t)):
                    dmeta_ref[:, sl] += grad_t[:, 0:BLOCK].T
                    dkv_ref[:, sl] = hold_ref[:, sl] + grad_t[:, BLOCK:2 * BLOCK].T
                    hold_ref[:, sl] = grad_t[:, 2 * BLOCK:3 * BLOCK].T

        @pl.when(i == nb)
        def _():
            dkv_ref[...] = hold_ref[...]
            for cp in _scatter(src, dst, send_sems, recv_sems):
                cp.wait()

    def cur(i):
        return jnp.minimum(i, nb - 1)

    return pl.pallas_call(
        body, name="attn_bwd", grid=(nb + 1,),
        in_specs=[pl.BlockSpec(memory_space=pltpu.SMEM),
                  pl.BlockSpec((BLOCK, ATTN_W), lambda i: (cur(i), 0)),
                  pl.BlockSpec((BLOCK, 2 * KV_W), lambda i: (cur(i), 0)),
                  pl.BlockSpec((BLOCK, 2 * KV_W), lambda i: (jnp.maximum(cur(i) - 1, 0), 0)),
                  _const((BLOCK, 2 * KV_W)),
                  pl.BlockSpec((BLOCK, ATTN_W), lambda i: (cur(i), 0)),
                  pl.BlockSpec((BLOCK, ATTN_W), lambda i: (cur(i), 0)),
                  pl.BlockSpec((BLOCK, BLOCK), lambda i: (cur(i), 0))] + [ANY] * n,
        out_specs=[pl.BlockSpec((BLOCK, ATTN_W), lambda i: (cur(i), 0)),
                   pl.BlockSpec((BLOCK, 2 * KV_W), lambda i: (jnp.maximum(i - 1, 0), 0)),
                   _const((BLOCK, 2 * KV_W)), _const((N_HEADS, BLOCK))] + [ANY] * n,
        out_shape=[jax.ShapeDtypeStruct((r, ATTN_W), BF16), jax.ShapeDtypeStruct((r, 2 * KV_W), F32),
                   jax.ShapeDtypeStruct((BLOCK, 2 * KV_W), F32), jax.ShapeDtypeStruct((N_HEADS, BLOCK), F32)] + _scatter_landing(parts),
        scratch_shapes=[pltpu.VMEM((BLOCK, 2 * KV_W), F32), pltpu.VMEM((N_KV, GROUP * BLOCK, 2 * BLOCK), F32),
                        pltpu.VMEM((GROUP * BLOCK, BLOCK), F32), pltpu.SemaphoreType.DMA((7 * n,)), pltpu.SemaphoreType.DMA((7 * n,))],
        compiler_params=_cparams(),
    )(sinks, q, kv, kv, kv, o, do, lse, *parts)


def _in_proj_bwd(dq, dkv, dkv_meta, dc, dh1, h0, g1, w_in_t, seq, tm):
    r = h0.shape[0]
    n_tiles = r // tm
    n_out = -(-seq // tm)

    def body(dq_ref, dkv_ref, dm_ref, dc_ref, dh1_ref, h_ref, g_ref, w_ref, gx_ref, lead_ref, dwt_ref, dg_ref, dw_ref, hold_ref):
        i = pl.program_id(0)

        @pl.when(i == 0)
        def _():
            dw_ref[...] = jnp.zeros_like(dw_ref)
            dg_ref[...] = jnp.zeros_like(dg_ref)

        @pl.when(i < n_tiles)
        def _():
            meta = jnp.concatenate([dm_ref[...], jnp.zeros((tm - BLOCK, 2 * KV_W), F32)], axis=0) if tm > BLOCK else dm_ref[...]
            first = (jnp.zeros((tm, 2 * KV_W), jnp.int32) + i) == 0
            dkvb = (dkv_ref[...] + jnp.where(first, meta, 0.0)).astype(BF16)
            hn, xhat, rstd = _rms_fwd(h_ref[...], g_ref[...])
            dproj = jnp.concatenate([dq_ref[...], dkvb, dc_ref[...]], axis=1)
            dhn = _dot(dproj, w_ref[...])
            dw_ref[...] += _dot_tn(dproj, hn.astype(BF16))
            dx, dg_rows = _rms_bwd(dhn, xhat, rstd, g_ref[...])
            dg_ref[...] += jnp.sum(dg_rows, axis=0, keepdims=True)
            dh0 = dh1_ref[...] + dx

            @pl.when(i == 0)
            def _():
                lead_ref[...] = dh0[0:BLOCK]

            @pl.when((i >= 1) & (i <= n_out))
            def _():
                gx_ref[0:tm - BLOCK, :] = hold_ref[...]
                gx_ref[tm - BLOCK:tm, :] = dh0[0:BLOCK]

            hold_ref[...] = dh0[BLOCK:tm]

        @pl.when((i == n_tiles) & (n_tiles <= n_out))
        def _():
            gx_ref[0:tm - BLOCK, :] = hold_ref[...]

        @pl.when(i == n_tiles - 1)
        def _():
            dwt_ref[...] = dw_ref[...].astype(BF16)

    def tile(n):
        return pl.BlockSpec((tm, n), lambda i: (jnp.minimum(i, n_tiles - 1), 0))

    return pl.pallas_call(
        body, name="in_proj_bwd", grid=(n_tiles + 1,),
        in_specs=[tile(ATTN_W), tile(2 * KV_W), _const((BLOCK, 2 * KV_W)), tile(2 * CONV_W), tile(D_MODEL), tile(D_MODEL),
                  _const((1, D_MODEL)), _const((IN_COLS, D_MODEL))],
        out_specs=[pl.BlockSpec((tm, D_MODEL), lambda i: (jnp.clip(i - 1, 0, n_out - 1), 0)), _const((BLOCK, D_MODEL)),
                   _const((IN_COLS, D_MODEL)), _const((1, D_MODEL))],
        out_shape=[jax.ShapeDtypeStruct((seq, D_MODEL), F32), jax.ShapeDtypeStruct((BLOCK, D_MODEL), F32),
                   jax.ShapeDtypeStruct((IN_COLS, D_MODEL), BF16), jax.ShapeDtypeStruct((1, D_MODEL), F32)],
        scratch_shapes=[pltpu.VMEM((IN_COLS, D_MODEL), F32), pltpu.VMEM((tm - BLOCK, D_MODEL), F32)],
        compiler_params=_cparams(),
    )(dq, dkv, dkv_meta, dc, dh1, h0, g1, w_in_t)


def _adamw_update(w_ref, g_ref, m_ref, v_ref, d_ref, nm_ref, nv_ref):
    g = g_ref[...]
    m = ADAM_B1 * m_ref[...] + (1.0 - ADAM_B1) * g
    v = ADAM_B2 * v_ref[...] + (1.0 - ADAM_B2) * (g * g)
    m_hat = m / (1.0 - ADAM_B1 ** ADAM_STEP)
    v_hat = v / (1.0 - ADAM_B2 ** ADAM_STEP)
    d_ref[...] = -ADAM_LR * (m_hat / (jnp.sqrt(v_hat) + ADAM_EPS) + ADAM_WD * w_ref[...])
    nm_ref[...] = m
    nv_ref[...] = v


def _adamw(w, g, m, v, name):
    rows, cols = w.shape
    tr = rows
    for cand in (256, 176, 128, 64, 32, 16, 8):
        if rows % cand == 0:
            tr = cand
            break

    def body(*refs):
        _adamw_update(*refs)

    spec = _row(tr, cols)
    return pl.pallas_call(
        body, name=name, grid=(rows // tr,), in_specs=[spec] * 4, out_specs=[spec] * 3,
        out_shape=[jax.ShapeDtypeStruct((rows, cols), F32)] * 3, compiler_params=_cparams(),
    )(w, g, m, v)


MESH = pl.DeviceIdType.MESH
ANY = pl.BlockSpec(memory_space=pl.ANY)


def _place():
    x, y, c = lax.axis_index("x"), lax.axis_index("y"), lax.axis_index("c")
    chips = [(1 - x, y), (x, 1 - y), (1 - x, 1 - y)]
    return x, y, c, chips


def _gather_ici(dst, send_sems, recv_sems):
    x, y, c, chips = _place()
    sends, arrivals = [], []
    for k in range(len(dst)):
        rows = dst[k].shape[1] // 2
        half = pl.ds(c * rows, rows)
        mine = dst[k].at[2 * x + y, half]
        for p, chip in enumerate(chips):
            sems = dict(send_sem=send_sems.at[3 * k + p], recv_sem=recv_sems.at[3 * k + p], device_id=(chip[0], chip[1], c),
                        device_id_type=MESH)
            sends.append(pltpu.make_async_remote_copy(src_ref=mine, dst_ref=mine, **sems))
            theirs = dst[k].at[2 * chip[0] + chip[1], half]
            arrivals.append(pltpu.make_async_remote_copy(src_ref=theirs, dst_ref=theirs, **sems))
    return sends, arrivals


def _gather_d2d(dst, send_sems, recv_sems):
    x, y, c, chips = _place()
    sends, arrivals = [], []
    for k in range(len(dst)):
        rows = dst[k].shape[1] // 2
        for p, chip in enumerate(chips):
            sems = dict(send_sem=send_sems.at[3 * k + p], recv_sem=recv_sems.at[3 * k + p], device_id=(x, y, 1 - c),
                        device_id_type=MESH)
            mine = dst[k].at[2 * chip[0] + chip[1], pl.ds(c * rows, rows)]
            sends.append(pltpu.make_async_remote_copy(src_ref=mine, dst_ref=mine, **sems))
            theirs = dst[k].at[2 * chip[0] + chip[1], pl.ds((1 - c) * rows, rows)]
            arrivals.append(pltpu.make_async_remote_copy(src_ref=theirs, dst_ref=theirs, **sems))
    return sends, arrivals


def _own_slots(shard, dtype):
    return jnp.broadcast_to(shard[None], (N_SHARD,) + shard.shape).astype(dtype)


def _gather_weights(slots):
    n = len(slots)

    def body(*refs):
        dst = refs[n:2 * n]
        ici_send, ici_recv, d2d_send, d2d_recv = refs[2 * n:]
        sends, arrivals = _gather_ici(dst, ici_send, ici_recv)
        for cp in sends:
            cp.start()
        for cp in arrivals:
            cp.wait_recv()
        forwards, from_sibling = _gather_d2d(dst, d2d_send, d2d_recv)
        for cp in forwards:
            cp.start()
        for cp in from_sibling:
            cp.wait_recv()
        for cp in sends + forwards:
            cp.wait_send()

    return pl.pallas_call(
        body, name="gather_weights",
        in_specs=[ANY] * n, out_specs=[ANY] * n,
        out_shape=[jax.ShapeDtypeStruct(s.shape, s.dtype) for s in slots],
        input_output_aliases={k: k for k in range(n)},
        scratch_shapes=[pltpu.SemaphoreType.DMA((3 * n,))] * 4,
    )(*slots)


VMEM_WHOLE = pl.BlockSpec(memory_space=pltpu.VMEM)


SMALL_ROWS = 48
SMALL_PLACES = ((0, (0, 1), 0, 0), (1, (0, 1), 1, 0), (2, (0, 1), 2, 0), (3, (0, 16), 8, 0), (4, (0, 1), 3, 0), (5, (0, 1), 3, 512),
                (6, (0, 1), 4, 0), (7, (0, 1), 4, 512), (8, (0, 1), 5, 0), (9, (0, 16), 24, 0), (9, (16, 32), 24, 512),
                (10, (0, 1), 5, 512), (11, (0, 8), 40, 0))


def _pack_small(parts):
    n = len(parts)

    def body(*refs):
        out = refs[n]
        out[...] = jnp.zeros_like(out)
        for k, (lo, hi), r0, c0 in SMALL_PLACES:
            out[r0:r0 + hi - lo, c0:c0 + parts[k].shape[1]] = refs[k][lo:hi, :]

    return pl.pallas_call(body, name="pack_small", in_specs=[VMEM_WHOLE] * n, out_specs=VMEM_WHOLE,
                          out_shape=jax.ShapeDtypeStruct((SMALL_ROWS, D_MODEL), F32))(*parts)


def _sum_small(own, landed, parts):
    n = len(parts)

    def body(own_ref, landed_ref, *outs):
        x, y, c = lax.axis_index("x"), lax.axis_index("y"), lax.axis_index("c")
        total = jnp.zeros((SMALL_ROWS, D_MODEL), F32)
        for d in range(N_DEV):
            j = 4 * ((x + (d >> 2)) % 2) + 2 * ((y + ((d >> 1) & 1)) % 2) + (c + (d & 1)) % 2
            mine = (jnp.zeros((SMALL_ROWS, D_MODEL), jnp.int32) + j) == 0
            total = total + jnp.where(mine, own_ref[...], landed_ref[jnp.maximum(j - 1, 0)])
        for k, (lo, hi), r0, c0 in SMALL_PLACES:
            outs[k][lo:hi, :] = total[r0:r0 + hi - lo, c0:c0 + parts[k].shape[1]]

    return pl.pallas_call(body, name="sum_small", in_specs=[VMEM_WHOLE] * 2, out_specs=[VMEM_WHOLE] * n,
                          out_shape=[jax.ShapeDtypeStruct(p.shape, F32) for p in parts])(own, landed)


def _adamw_small(ws, gs, ms, vs):
    n = len(ws)

    def body(*refs):
        for k in range(n):
            w_ref, g_ref, m_ref, v_ref = (refs[j * n + k] for j in range(4))
            _adamw_update(w_ref, g_ref, m_ref, v_ref, *(refs[(4 + j) * n + k] for j in range(3)))

    shapes = [jax.ShapeDtypeStruct(w.shape, F32) for w in ws]
    out = pl.pallas_call(
        body, name="adamw_small", in_specs=[VMEM_WHOLE] * (4 * n), out_specs=[VMEM_WHOLE] * (3 * n), out_shape=shapes * 3,
    )(*ws, *gs, *ms, *vs)
    return out[:n], out[n:2 * n], out[2 * n:]


def _scatter(src, dst, send_sems, recv_sems, whole=False):
    x, y, c = lax.axis_index("x"), lax.axis_index("y"), lax.axis_index("c")
    copies = []
    for k in range(len(src)):
        for j in range(1, N_DEV):
            px, py, pc = (x + (j >> 2)) % 2, (y + ((j >> 1) & 1)) % 2, (c + (j & 1)) % 2
            rows = src[k].shape[1] // 2
            piece = src[k] if whole else src[k].at[2 * px + py, pl.ds(pc * rows, rows)]
            copies.append(pltpu.make_async_remote_copy(
                src_ref=piece, dst_ref=dst[k].at[j - 1],
                send_sem=send_sems.at[7 * k + j - 1], recv_sem=recv_sems.at[7 * k + j - 1], device_id=(px, py, pc),
                device_id_type=MESH))
    return copies


def _scatter_landing(parts, whole=False):
    return [jax.ShapeDtypeStruct((N_DEV - 1,) + (p.shape if whole else (p.shape[1] // 2, p.shape[2])), p.dtype) for p in parts]


HBM = pl.BlockSpec(memory_space=pltpu.HBM)
SEMAPHORES = pl.BlockSpec(memory_space=pltpu.SEMAPHORE)
SPLIT_COPY = pltpu.CompilerParams(has_side_effects=pltpu.SideEffectType.DATAFLOW_SIDE_EFFECTING)


def _scatter_start(part, whole, name):
    landing, = _scatter_landing([part], whole)

    def body(src_ref, land_ref, send_sems, recv_sems, src_thru, land_thru, token_ref):
        for cp in _scatter([src_ref], [land_ref], send_sems, recv_sems, whole):
            cp.start()
        token_ref[...] = jnp.zeros_like(token_ref)

    return pl.pallas_call(
        body, name=name,
        out_shape=(pltpu.SemaphoreType.DMA((N_DEV - 1,)), pltpu.SemaphoreType.DMA((N_DEV - 1,)), pltpu.HBM(part.shape, part.dtype),
                   pltpu.HBM(landing.shape, landing.dtype), jax.ShapeDtypeStruct((8, 128), F32)),
        in_specs=(HBM, HBM), out_specs=(SEMAPHORES, SEMAPHORES, HBM, HBM, VMEM_WHOLE), input_output_aliases={0: 2, 1: 3},
        compiler_params=SPLIT_COPY,
    )(pltpu.with_memory_space_constraint(part, pltpu.HBM),
      pltpu.with_memory_space_constraint(lax.empty(landing.shape, landing.dtype), pltpu.HBM))


def _scatter_wait(send_sems, recv_sems, part_thru, land_thru, after, whole, name):
    def body(src_ref, land_ref, send_ref, recv_ref, after_ref, src_dead, got_ref):
        for cp in _scatter([src_ref], [land_ref], send_ref, recv_ref, whole):
            cp.wait_send()
            cp.wait_recv()

    return pl.pallas_call(
        body, name=name,
        out_shape=(pltpu.HBM(part_thru.shape, part_thru.dtype), pltpu.HBM(land_thru.shape, land_thru.dtype)),
        in_specs=(HBM, HBM, SEMAPHORES, SEMAPHORES, ANY), out_specs=(HBM, HBM), input_output_aliases={0: 0, 1: 1},
        compiler_params=SPLIT_COPY,
    )(part_thru, land_thru, send_sems, recv_sems, after)


def _sum_pieces(own, landed, behind, name):
    n = len(own)

    def body(*refs):
        for k in range(n):
            got = refs[n + k]
            total = refs[k][...].astype(F32)
            for j in range(N_DEV - 1):
                total = total + got[j].astype(F32)
            refs[2 * n + 1 + k][...] = total

    in_specs, out_specs = [], []
    for o in own:
        in_specs.append(_row(o.shape[0] // 2, o.shape[1]))
    for o in own:
        in_specs.append(pl.BlockSpec((N_DEV - 1, o.shape[0] // 2, o.shape[1]), lambda i: (0, i, 0)))
        out_specs.append(_row(o.shape[0] // 2, o.shape[1]))
    return pl.pallas_call(
        body, name=name, grid=(2,), in_specs=in_specs + [ANY], out_specs=out_specs,
        out_shape=[jax.ShapeDtypeStruct(o.shape, F32) for o in own], compiler_params=_cparams(),
    )(*own, *landed, behind)


def _swap_with_sibling(halves, name):
    n = len(halves)

    def body(*refs):
        x, y, c = lax.axis_index("x"), lax.axis_index("y"), lax.axis_index("c")
        copies = [pltpu.make_async_remote_copy(
            src_ref=refs[k], dst_ref=refs[n + k], send_sem=refs[2 * n].at[k], recv_sem=refs[2 * n + 1].at[k],
            device_id=(x, y, 1 - c), device_id_type=MESH) for k in range(n)]
        for cp in copies:
            cp.start()
        for cp in copies:
            cp.wait()

    return pl.pallas_call(
        body, name=name, in_specs=[ANY] * n, out_specs=[ANY] * n,
        out_shape=[jax.ShapeDtypeStruct(h.shape, h.dtype) for h in halves],
        scratch_shapes=[pltpu.SemaphoreType.DMA((n,)), pltpu.SemaphoreType.DMA((n,))],
    )(*halves)


def _own_piece(part):
    rows = part.shape[1] // 2
    s = 2 * lax.axis_index("x") + lax.axis_index("y")
    return lax.dynamic_slice(part, (s, lax.axis_index("c") * rows, 0), (1, rows, part.shape[2]))[0]


def _both_halves(mine, theirs):
    south = lax.axis_index("c") == 0
    return jnp.concatenate([jnp.where(south, mine, theirs), jnp.where(south, theirs, mine)], axis=0)


def _from_col_shards(g):
    return g.transpose(1, 0, 2).reshape(g.shape[1], -1)


def kernel(x, meta_tokens, attn_norm_g, w_in, attn_sinks, conv_w, conv_b, conv_ln_g, conv_ln_b, attn_out_g, conv_out_g, w_out, ffn_norm_g, w_gate, w_up, w_down, final_norm_g, loss_target, m_meta_tokens, m_attn_norm_g, m_w_in, m_attn_sinks, m_conv_w, m_conv_b, m_conv_ln_g, m_conv_ln_b, m_attn_out_g, m_conv_out_g, m_w_out, m_ffn_norm_g, m_w_gate, m_w_up, m_w_down, m_final_norm_g, v_meta_tokens, v_attn_norm_g, v_w_in, v_attn_sinks, v_conv_w, v_conv_b, v_conv_ln_g, v_conv_ln_b, v_attn_out_g, v_conv_out_g, v_w_out, v_ffn_norm_g, v_w_gate, v_w_up, v_w_down, v_final_norm_g):
    seq = x.shape[1]
    r = -(-(seq + BLOCK) // ROW_QUANTUM) * ROW_QUANTUM
    tm_wide = 768 if seq >= 768 else 256
    shard = 2 * lax.axis_index("x") + lax.axis_index("y")

    conv_w32 = jnp.pad(conv_w[0], ((0, 1), (0, 0)))
    small_shard = jnp.concatenate([meta_tokens, conv_w32.reshape(16, 256)], axis=0)
    g_in, g_small = _gather_weights([_own_slots(w_in[0].T, BF16), _own_slots(small_shard, F32)])
    later = [_own_slots(w, BF16) for w in (w_gate[0].T, w_up[0].T, w_out[0], w_down[0])]
    w_in_t = g_in.reshape(IN_COLS, D_MODEL)
    meta_full = _from_col_shards(g_small[:, 0:N_META])
    cw_full = _from_col_shards(g_small[:, N_META:].reshape(N_SHARD, 32, 128))

    g1, ga, gc, g2 = attn_norm_g, attn_out_g, conv_out_g, ffn_norm_g
    gf = final_norm_g.reshape(1, D_MODEL)
    sinks = attn_sinks[0]

    lead = jnp.concatenate([jnp.zeros((LEAD, D_MODEL), F32), meta_full], axis=0)
    h0, q, kv, cacg = _in_proj(x[0], lead, g1, w_in_t, r, 768)
    oa, lse, *gathered = _attn_fwd(q, kv, sinks, later)
    oc, yc, g_gate, g_up, g_out, g_down = _conv_fwd(cacg, cw_full, conv_b, conv_ln_g, conv_ln_b, 384, gathered)
    wg_t, wu_t, wd_b = g_gate.reshape(D_FF, D_MODEL), g_up.reshape(D_FF, D_MODEL), g_down.reshape(D_FF, D_MODEL)
    w_out_b = g_out.reshape(D_MODEL, D_MODEL)
    h1, hn2 = _out_proj(oa, oc, h0, ga, gc, g2, w_out_b, 768)
    gate, up, act, dh2, dh2b, loss_p, dgf = _ffn_fwd(hn2, h1, loss_target[0], gf, wg_t, wu_t, wd_b, 384)

    def by_shard(dw):
        return dw.reshape(N_SHARD, dw.shape[0] // N_SHARD, D_MODEL)

    dgate, dup, dh1, dg2 = _ffn_bwd(dh2, dh2b, gate, up, h1, g2, wg_t, wu_t, wd_b, 384)
    p_gate, p_up = [by_shard(dw) for dw in _ffn_wgrad_gu(hn2, dgate, dup, 768)]
    p_down = by_shard(_ffn_wgrad_d(act, dh2b, 768))
    doa, doc, dwo, dga, dgc = _out_proj_bwd(dh1, oa, oc, ga, gc, w_out_b, 768)
    p_out = by_shard(dwo)
    dy, dcw, dcb, dlg, dlb, l_gate, l_up = _conv_bwd_params(doc, yc, cacg, conv_ln_g, conv_ln_b, 384, [p_gate, p_up])
    dc = _conv_bwd_data(dy, cacg, cw_full, 384)
    dq, dkv, dkv_meta, dsink, l_out, l_down = _attn_bwd(q, kv, oa, doa, lse, sinks, [p_out, p_down])
    grad_x, dlead, dwi_t, dg1 = _in_proj_bwd(dq, dkv, dkv_meta, dc, dh1, h0, g1, w_in_t, seq, tm_wide)
    in_send, in_recv, p_in_thru, l_in_thru, started = _scatter_start(by_shard(dwi_t), False, "scatter_start")
    small_parts = [dgf, dg1, dg2, dlead[LEAD:BLOCK], dga, dgc, dcb, dlg, dlb, dcw, loss_p, dsink]
    sm_send, sm_recv, pack_thru, packs_thru, _ = _scatter_start(_pack_small(small_parts), True, "gather_small_start")

    big = ("w_gate", "w_up", "w_out", "w_down", "w_in")
    transposed = ("w_in", "w_gate", "w_up")
    grads = {}
    params = {
        "meta_tokens": (meta_tokens, m_meta_tokens, v_meta_tokens), "attn_norm_g": (attn_norm_g, m_attn_norm_g, v_attn_norm_g),
        "w_in": (w_in, m_w_in, v_w_in), "attn_sinks": (attn_sinks, m_attn_sinks, v_attn_sinks), "conv_w": (conv_w, m_conv_w, v_conv_w),
        "conv_b": (conv_b, m_conv_b, v_conv_b), "conv_ln_g": (conv_ln_g, m_conv_ln_g, v_conv_ln_g),
        "conv_ln_b": (conv_ln_b, m_conv_ln_b, v_conv_ln_b), "attn_out_g": (attn_out_g, m_attn_out_g, v_attn_out_g),
        "conv_out_g": (conv_out_g, m_conv_out_g, v_conv_out_g), "w_out": (w_out, m_w_out, v_w_out),
        "ffn_norm_g": (ffn_norm_g, m_ffn_norm_g, v_ffn_norm_g), "w_gate": (w_gate, m_w_gate, v_w_gate), "w_up": (w_up, m_w_up, v_w_up),
        "w_down": (w_down, m_w_down, v_w_down), "final_norm_g": (final_norm_g, m_final_norm_g, v_final_norm_g)}
    names = list(params)
    delta, new_m, new_v = {}, {}, {}

    def finish(group, parts, landed, behind, tag):
        halves = _sum_pieces([_own_piece(p) for p in parts], landed, behind, "sum_pieces_" + tag)
        for name, mine, theirs in zip(group, halves, _swap_with_sibling(halves, "swap_with_sibling_" + tag)):
            flip = (lambda a: a.T) if name in transposed else (lambda a: a)
            g = _both_halves(mine, theirs)
            w, m, v = params[name]
            outs = _adamw(flip(w[0]), g, flip(m[0]), flip(v[0]), "adamw_" + name)
            grads[name], delta[name], new_m[name], new_v[name] = [flip(a)[None] for a in (g, *outs)]
        return outs[0]

    done = finish(big[:4], [p_gate, p_up, p_out, p_down], [l_gate, l_up, l_out, l_down], started, "ffn_out")

    red_names = ("final_norm_g", "attn_norm_g", "ffn_norm_g", "meta_tokens", "attn_out_g", "conv_out_g", "conv_b", "conv_ln_g",
                 "conv_ln_b", "conv_w", "loss", "attn_sinks")
    pack, packs = _scatter_wait(sm_send, sm_recv, pack_thru, packs_thru, done, True, "gather_small_wait")
    red = dict(zip(red_names, _sum_small(pack, packs, small_parts)))
    loss = red["loss"][0, 0]
    for name in ("attn_norm_g", "conv_b", "conv_ln_g", "conv_ln_b", "attn_out_g", "conv_out_g", "ffn_norm_g"):
        grads[name] = red[name]
    grads["final_norm_g"] = red["final_norm_g"].reshape(D_MODEL)
    grads["attn_sinks"] = red["attn_sinks"][:, 0].reshape(1, N_HEADS)
    grads["meta_tokens"] = lax.dynamic_slice_in_dim(red["meta_tokens"], shard * (D_MODEL // N_SHARD), D_MODEL // N_SHARD, axis=1)
    grads["conv_w"] = lax.dynamic_slice_in_dim(
        red["conv_w"][0:CONV_K], shard * (CONV_W // N_SHARD), CONV_W // N_SHARD, axis=1)[None]

    p_in, l_in = _scatter_wait(in_send, in_recv, p_in_thru, l_in_thru, red["loss"], False, "scatter_wait")
    finish(big[4:], [p_in], [l_in], l_in, "in")
    rest = [name for name in names if name not in big]

    def rows_of(a):
        return a.reshape(-1, a.shape[-1])

    small = _adamw_small([rows_of(params[n][0]) for n in rest], [rows_of(grads[n]) for n in rest],
                         [rows_of(params[n][1]) for n in rest], [rows_of(params[n][2]) for n in rest])
    for dst, outs in zip((delta, new_m, new_v), small):
        for name, out in zip(rest, outs):
            dst[name] = out.reshape(params[name][0].shape)

    return (loss, grad_x[None], *[grads[n] for n in names], *[delta[n] for n in names], *[new_m[n] for n in names],
            *[new_v[n] for n in names])
```

```python
import functools
import math

import jax
import jax.numpy as jnp
from jax import lax
from jax.experimental import pallas as pl
from jax.experimental.pallas import tpu as pltpu

F32 = jnp.float32
BF16 = jnp.bfloat16

D_MODEL = 1024
N_META = 16
ATTN_W = 512
CONV_W = 512
HEAD_DIM = 64
N_HEADS = 8
N_KV = 2
GROUP = N_HEADS // N_KV
KV_W = N_KV * HEAD_DIM
BLOCK = 128
LEAD = BLOCK - N_META
CONV_K = 31
D_FF = 2816
IN_COLS = ATTN_W + 2 * KV_W + 2 * CONV_W
Q0, KV0, C0 = 0, ATTN_W, ATTN_W + 2 * KV_W
NORM_EPS = 1e-5
SCALE = 1.0 / math.sqrt(HEAD_DIM)
SLOPES = tuple(2.0 ** (-(8.0 / N_HEADS) * (h + 1)) for h in range(N_HEADS))
NEG = -1e30

ADAM_LR, ADAM_B1, ADAM_B2, ADAM_EPS, ADAM_WD, ADAM_STEP = 0.001, 0.9, 0.999, 1e-08, 0.01, 10

N_SHARD = 4
N_DEV = 8
ROW_QUANTUM = 768
HALO = 32
CONV_CHUNK = 32
FF_CHUNK = 256
FF_CHUNKS = tuple(slice(c, c + FF_CHUNK) for c in range(0, D_FF, FF_CHUNK))
VMEM_LIMIT = 60 * 1024 * 1024


def _cparams(n_axes=1):
    return pltpu.CompilerParams(dimension_semantics=("arbitrary",) * n_axes, vmem_limit_bytes=VMEM_LIMIT)


def _dot(a, b):
    return jnp.dot(a, b, preferred_element_type=F32)


def _dot_nt(a, b):
    return lax.dot_general(a, b, (((1,), (1,)), ((), ())), preferred_element_type=F32)


def _dot_tn(a, b):
    return lax.dot_general(a, b, (((0,), (0,)), ((), ())), preferred_element_type=F32)


def _sigmoid(x):
    return 1.0 / (1.0 + jnp.exp(-x))


def _row(tm, n):
    return pl.BlockSpec((tm, n), lambda i: (i, 0))


def _const(shape):
    return pl.BlockSpec(shape, lambda i: (0,) * len(shape))


def _resident(shape):
    return pl.BlockSpec(shape, lambda i: (0,) * len(shape), pipeline_mode=pl.Buffered(1))


def _rms_fwd(x, g):
    rstd = lax.rsqrt(jnp.mean(x * x, axis=-1, keepdims=True) + NORM_EPS)
    xhat = x * rstd
    return xhat * g, xhat, rstd


def _rms_bwd(dy, xhat, rstd, g):
    dxh = dy * g
    dx = rstd * (dxh - xhat * jnp.mean(dxh * xhat, axis=-1, keepdims=True))
    return dx, dy * xhat


def _in_proj(x, lead, g1, w_in_t, r, tm):
    seq = x.shape[0]
    n_sub = tm // BLOCK

    def body(*refs):
        x_refs = refs[:n_sub]
        lead_ref, g_ref, w_ref, h0_ref, q_ref, kv_ref, c_ref = refs[n_sub:]
        i = pl.program_id(0)
        pieces = []
        for k, x_ref in enumerate(x_refs):
            at = jnp.zeros((BLOCK, D_MODEL), jnp.int32) + (i * tm + (k - 1) * BLOCK)
            piece = jnp.where((at >= 0) & (at < seq), x_ref[...], 0.0)
            pieces.append(jnp.where(at < 0, lead_ref[...], piece) if k == 0 else piece)
        h = jnp.concatenate(pieces, axis=0)
        h0_ref[...] = h
        hn = _rms_fwd(h, g_ref[...])[0].astype(BF16)
        q_ref[...] = _dot_nt(hn, w_ref[Q0:KV0, :]).astype(BF16)
        kv_ref[...] = _dot_nt(hn, w_ref[KV0:C0, :]).astype(BF16)
        c_ref[...] = _dot_nt(hn, w_ref[C0:IN_COLS, :])

    def x_block(k):
        return pl.BlockSpec((BLOCK, D_MODEL), lambda i: (jnp.clip(n_sub * i - 1 + k, 0, seq // BLOCK - 1), 0))

    return pl.pallas_call(
        body, name="in_proj", grid=(r // tm,),
        in_specs=[x_block(k) for k in range(n_sub)] + [_const((BLOCK, D_MODEL)), _const((1, D_MODEL)), _const((IN_COLS, D_MODEL))],
        out_specs=[_row(tm, D_MODEL), _row(tm, ATTN_W), _row(tm, 2 * KV_W), _row(tm, 2 * CONV_W)],
        out_shape=[jax.ShapeDtypeStruct((r, D_MODEL), F32), jax.ShapeDtypeStruct((r, ATTN_W), BF16),
                   jax.ShapeDtypeStruct((r, 2 * KV_W), BF16), jax.ShapeDtypeStruct((r, 2 * CONV_W), F32)],
        compiler_params=_cparams(),
    )(*[x] * n_sub, lead, g1, w_in_t)


def _attn_bias_init(bias_ref, late_ref):
    row = lax.broadcasted_iota(jnp.int32, (GROUP * BLOCK, BLOCK), 0) & (BLOCK - 1)
    col = lax.broadcasted_iota(jnp.int32, (GROUP * BLOCK, BLOCK), 1)
    late_ref[...] = jnp.where(col > row, 1.0, 0.0)
    for g in range(N_KV):
        slope = jnp.concatenate([jnp.zeros((BLOCK, BLOCK), F32) + SLOPES[g * GROUP + j] for j in range(GROUP)], axis=0)
        bias_ref[g, :, 0:BLOCK] = jnp.where(col >= LEAD, 0.0, NEG)
        bias_ref[g, :, BLOCK:2 * BLOCK] = -slope * jnp.where(col > row, row - col + BLOCK, row - col).astype(F32)


def _attn_block_bias(late_ref, i):
    late = late_ref[...]
    meta0 = jnp.where(i == 0, NEG, 0.0)
    no_prev = jnp.where(i >= 2, 0.0, NEG)
    no_cur = jnp.where(i >= 1, 0.0, NEG)
    return late * meta0, late * no_prev + no_cur


def _attn_logits(s3, bias_ref, block_bias, prev_part, g):
    meta = s3[:, 0:BLOCK] * SCALE + (bias_ref[g, :, 0:BLOCK] + block_bias[0])
    band = jnp.where(prev_part, s3[:, BLOCK:2 * BLOCK], s3[:, 2 * BLOCK:3 * BLOCK]) * SCALE + (bias_ref[g, :, BLOCK:2 * BLOCK] + block_bias[1])
    return meta, band


def _split_band(meta, band, prev_part):
    return jnp.concatenate([meta, jnp.where(prev_part, band, 0.0), jnp.where(prev_part, 0.0, band)], axis=1)


def _head_rows(vals):
    return jnp.concatenate([jnp.zeros((BLOCK, BLOCK), F32) + v for v in vals], axis=0)


def _stack_heads(ref, g):
    return jnp.concatenate([ref[:, (g * GROUP + j) * HEAD_DIM:(g * GROUP + j + 1) * HEAD_DIM] for j in range(GROUP)], axis=0)


def _kv_cat(kvm_ref, kvp_ref, kvc_ref, g):
    ks = slice(g * HEAD_DIM, (g + 1) * HEAD_DIM)
    vs = slice(KV_W + g * HEAD_DIM, KV_W + (g + 1) * HEAD_DIM)
    kcat = jnp.concatenate([kvm_ref[:, ks], kvp_ref[:, ks], kvc_ref[:, ks]], axis=0)
    vcat = jnp.concatenate([kvm_ref[:, vs], kvp_ref[:, vs], kvc_ref[:, vs]], axis=0)
    return kcat, vcat


def _attn_fwd(q, kv, sinks, gathered):
    r = q.shape[0]
    nb = r // BLOCK
    n = len(gathered)

    def body(sink_ref, q_ref, kvc_ref, kvp_ref, kvm_ref, *rest):
        o_ref, lse_ref = rest[n:n + 2]
        dst = rest[n + 2:2 * n + 2]
        bias_ref, late_ref, send_sems, recv_sems = rest[2 * n + 2:]
        i = pl.program_id(0)

        @pl.when(i == 0)
        def _():
            for cp in _gather_ici(dst, send_sems, recv_sems)[0]:
                cp.start()
            _attn_bias_init(bias_ref, late_ref)

        lane = lax.broadcasted_iota(jnp.int32, (BLOCK, BLOCK), 1)
        lse_tile = jnp.zeros((BLOCK, BLOCK), F32)
        block_bias = _attn_block_bias(late_ref, i)
        prev_part = late_ref[...] > 0.5
        for g in range(N_KV):
            kcat, vcat = _kv_cat(kvm_ref, kvp_ref, kvc_ref, g)
            heads = range(g * GROUP, (g + 1) * GROUP)
            meta, band = _attn_logits(_dot_nt(_stack_heads(q_ref, g), kcat), bias_ref, block_bias, prev_part, g)
            sink = _head_rows([sink_ref[h] for h in heads])
            m = jnp.maximum(jnp.max(jnp.maximum(meta, band), axis=-1, keepdims=True), sink)
            p_meta, p_band = jnp.exp(meta - m), jnp.exp(band - m)
            l = jnp.sum(p_meta + p_band, axis=-1, keepdims=True) + jnp.exp(sink - m)
            o = _dot(_split_band(p_meta, p_band, prev_part).astype(BF16), vcat) * (1.0 / l)[:, 0:HEAD_DIM]
            lse = m + jnp.log(l)
            for j, h in enumerate(heads):
                o_ref[:, h * HEAD_DIM:(h + 1) * HEAD_DIM] = o[j * BLOCK:(j + 1) * BLOCK].astype(BF16)
                lse_tile = jnp.where(lane == h, lse[j * BLOCK:(j + 1) * BLOCK], lse_tile)
        lse_ref[...] = lse_tile

        @pl.when(i == nb - 1)
        def _():
            sends, arrivals = _gather_ici(dst, send_sems, recv_sems)
            for cp in arrivals:
                cp.wait_recv()
            for cp in sends:
                cp.wait_send()

    return pl.pallas_call(
        body, name="attn_fwd", grid=(nb,),
        in_specs=[pl.BlockSpec(memory_space=pltpu.SMEM), _row(BLOCK, ATTN_W), _row(BLOCK, 2 * KV_W),
                  pl.BlockSpec((BLOCK, 2 * KV_W), lambda i: (jnp.maximum(i - 1, 0), 0)), _const((BLOCK, 2 * KV_W))] + [ANY] * n,
        out_specs=[_row(BLOCK, ATTN_W), _row(BLOCK, BLOCK)] + [ANY] * n,
        out_shape=[jax.ShapeDtypeStruct((r, ATTN_W), BF16), jax.ShapeDtypeStruct((r, BLOCK), F32)]
        + [jax.ShapeDtypeStruct(g.shape, g.dtype) for g in gathered],
        input_output_aliases={5 + k: 2 + k for k in range(n)},
        scratch_shapes=[pltpu.VMEM((N_KV, GROUP * BLOCK, 2 * BLOCK), F32), pltpu.VMEM((GROUP * BLOCK, BLOCK), F32),
                        pltpu.SemaphoreType.DMA((3 * n,)), pltpu.SemaphoreType.DMA((3 * n,))],
        compiler_params=_cparams(),
    )(sinks, q, kv, kv, kv, *gathered)


def _shifted_copies(ub_ref, win):
    w = win.shape[0]
    ub_ref[0] = win
    for b in range(1, 8):
        ub_ref[b] = pltpu.roll(win, shift=w - b, axis=0)


def _conv_chunk(ub_ref, w_ref, r0, shifts):
    acc = jnp.zeros((CONV_CHUNK, CONV_W), F32)
    for j in range(CONV_K):
        a, b = divmod(shifts[j], 8)
        acc = acc + w_ref[j:j + 1, :] * ub_ref[b, pl.ds(r0 + 8 * a, CONV_CHUNK), :]
    return acc


FWD_SHIFTS = tuple(HALO - (CONV_K - 1) + j for j in range(CONV_K))
BWD_SHIFTS = tuple(CONV_K - 1 - j for j in range(CONV_K))


def _glu_window(cp_ref, c_ref, i):
    tile = c_ref[:, 0:CONV_W] * _sigmoid(c_ref[:, CONV_W:2 * CONV_W])
    halo = cp_ref[:, 0:CONV_W] * _sigmoid(cp_ref[:, CONV_W:2 * CONV_W])
    first = (jnp.zeros((HALO, CONV_W), jnp.int32) + i) == 0
    return jnp.concatenate([jnp.where(first, 0.0, halo), tile], axis=0)


def _halo_before(tm, n):
    return pl.BlockSpec((HALO, n), lambda i: (jnp.maximum(i * (tm // HALO) - 1, 0), 0))


def _conv_fwd(cacg, cw, cb, lg, lb, tm, gathered):
    r = cacg.shape[0]
    n = len(gathered)

    def body(c_ref, cp_ref, w_ref, cb_ref, lg_ref, lb_ref, *rest):
        o_ref, y_ref = rest[n:n + 2]
        dst = rest[n + 2:2 * n + 2]
        ub_ref, send_sems, recv_sems = rest[2 * n + 2:]
        i = pl.program_id(0)

        @pl.when(i == 0)
        def _():
            for cp in _gather_d2d(dst, send_sems, recv_sems)[0]:
                cp.start()

        _shifted_copies(ub_ref, _glu_window(cp_ref, c_ref, i))

        def chunk(ci, carry):
            r0 = pl.multiple_of(ci * CONV_CHUNK, CONV_CHUNK)
            y = _conv_chunk(ub_ref, w_ref, r0, FWD_SHIFTS) + cb_ref[...]
            yc = y - jnp.mean(y, axis=-1, keepdims=True)
            rs = lax.rsqrt(jnp.mean(yc * yc, axis=-1, keepdims=True) + NORM_EPS)
            yn = yc * rs * lg_ref[...] + lb_ref[...]
            o_ref[pl.ds(r0, CONV_CHUNK), :] = (yn * _sigmoid(yn)).astype(BF16)
            y_ref[pl.ds(r0, CONV_CHUNK), :] = y
            return carry

        lax.fori_loop(0, tm // CONV_CHUNK, chunk, 0, unroll=6)

        @pl.when(i == r // tm - 1)
        def _():
            sends, arrivals = _gather_d2d(dst, send_sems, recv_sems)
            for cp in arrivals:
                cp.wait_recv()
            for cp in sends:
                cp.wait_send()

    return pl.pallas_call(
        body, name="conv_fwd", grid=(r // tm,),
        in_specs=[_row(tm, 2 * CONV_W), _halo_before(tm, 2 * CONV_W), _const((32, CONV_W)), _const((1, CONV_W)),
                  _const((1, CONV_W)), _const((1, CONV_W))] + [ANY] * n,
        out_specs=[_row(tm, CONV_W), _row(tm, CONV_W)] + [ANY] * n,
        out_shape=[jax.ShapeDtypeStruct((r, CONV_W), BF16), jax.ShapeDtypeStruct((r, CONV_W), F32)]
        + [jax.ShapeDtypeStruct(g.shape, g.dtype) for g in gathered],
        input_output_aliases={6 + k: 2 + k for k in range(n)},
        scratch_shapes=[pltpu.VMEM((8, tm + HALO, CONV_W), F32), pltpu.SemaphoreType.DMA((3 * n,)), pltpu.SemaphoreType.DMA((3 * n,))],
        compiler_params=_cparams(),
    )(cacg, cacg, cw, cb, lg, lb, *gathered)


def _out_proj(oa, oc, h0, ga, gc, g2, w_out_b, tm):
    r = h0.shape[0]

    def body(oa_ref, oc_ref, h_ref, ga_ref, gc_ref, g2_ref, w_ref, h1_ref, hn2_ref):
        ma = _rms_fwd(oa_ref[...].astype(F32), ga_ref[...])[0].astype(BF16)
        mc = _rms_fwd(oc_ref[...].astype(F32), gc_ref[...])[0].astype(BF16)
        h1 = h_ref[...] + _dot(jnp.concatenate([ma, mc], axis=1), w_ref[...])
        h1_ref[...] = h1
        hn2_ref[...] = _rms_fwd(h1, g2_ref[...])[0].astype(BF16)

    return pl.pallas_call(
        body, name="out_proj", grid=(r // tm,),
        in_specs=[_row(tm, ATTN_W), _row(tm, CONV_W), _row(tm, D_MODEL), _const((1, ATTN_W)), _const((1, CONV_W)),
                  _const((1, D_MODEL)), _const((D_MODEL, D_MODEL))],
        out_specs=[_row(tm, D_MODEL), _row(tm, D_MODEL)],
        out_shape=[jax.ShapeDtypeStruct((r, D_MODEL), F32), jax.ShapeDtypeStruct((r, D_MODEL), BF16)],
        compiler_params=_cparams(),
    )(oa, oc, h0, ga, gc, g2, w_out_b)


def _ffn_fwd(oa, oc, h0, ga, gc, g2, w_out_b, target, gf, wg_t, wu_t, wd_b, tm):
    r = h0.shape[0]
    seq = target.shape[0]
    n_sub = tm // BLOCK

    def body(oa_ref, oc_ref, h0_ref, ga_ref, gc_ref, g2_ref, wo_ref, *rest):
        t_refs = rest[:n_sub]
        (gf_ref, wg_ref, wu_ref, wd_ref, h1_ref, hn_ref, gate_ref, up_ref, act_ref, dh2_ref, dh2b_ref, loss_ref,
         dgf_ref) = rest[n_sub:]
        i = pl.program_id(0)

        @pl.when(i == 0)
        def _():
            loss_ref[...] = jnp.zeros_like(loss_ref)
            dgf_ref[...] = jnp.zeros_like(dgf_ref)

        ma = _rms_fwd(oa_ref[...].astype(F32), ga_ref[...])[0].astype(BF16)
        mc = _rms_fwd(oc_ref[...].astype(F32), gc_ref[...])[0].astype(BF16)
        h1_ref[...] = h0_ref[...] + _dot(jnp.concatenate([ma, mc], axis=1), wo_ref[...])
        hn_ref[...] = _rms_fwd(h1_ref[...], g2_ref[...])[0].astype(BF16)
        hn = hn_ref[...]
        for cs in FF_CHUNKS:
            gate = _dot_nt(hn, wg_ref[cs, :])
            up = _dot_nt(hn, wu_ref[cs, :])
            gate_ref[:, cs] = gate.astype(BF16)
            up_ref[:, cs] = up.astype(BF16)
            act_ref[:, cs] = (gate * _sigmoid(gate) * up).astype(BF16)
        y, xhat, rstd = _rms_fwd(h1_ref[...] + _dot(act_ref[...], wd_ref[...]), gf_ref[...])
        rows = lax.broadcasted_iota(jnp.int32, (tm, D_MODEL), 0) + i * tm
        real = (rows >= BLOCK) & (rows < BLOCK + seq)
        err = jnp.where(real, y - jnp.concatenate([t[...] for t in t_refs], axis=0), 0.0)
        loss_ref[...] += jnp.sum(err * err) * (0.5 / D_MODEL)
        dy = err * (1.0 / D_MODEL)
        dh2, dg_rows = _rms_bwd(dy, xhat, rstd, gf_ref[...])
        dgf_ref[...] += jnp.sum(dg_rows, axis=0, keepdims=True)
        dh2_ref[...] = dh2
        dh2b_ref[...] = dh2.astype(BF16)

    def target_block(k):
        return pl.BlockSpec((BLOCK, D_MODEL), lambda i: (jnp.clip(n_sub * i - 1 + k, 0, seq // BLOCK - 1), 0))

    return pl.pallas_call(
        body, name="ffn_fwd", grid=(r // tm,),
        in_specs=[_row(tm, ATTN_W), _row(tm, CONV_W), _row(tm, D_MODEL), _const((1, ATTN_W)), _const((1, CONV_W)),
                  _const((1, D_MODEL)), _resident((D_MODEL, D_MODEL))] + [target_block(k) for k in range(n_sub)]
        + [_const((1, D_MODEL))] + [_resident((D_FF, D_MODEL))] * 3,
        out_specs=[_row(tm, D_MODEL), _row(tm, D_MODEL)] + [_row(tm, D_FF)] * 3
        + [_row(tm, D_MODEL), _row(tm, D_MODEL), _const((1, BLOCK)), _const((1, D_MODEL))],
        out_shape=[jax.ShapeDtypeStruct((r, D_MODEL), F32), jax.ShapeDtypeStruct((r, D_MODEL), BF16)]
        + [jax.ShapeDtypeStruct((r, D_FF), BF16)] * 3
        + [jax.ShapeDtypeStruct((r, D_MODEL), F32), jax.ShapeDtypeStruct((r, D_MODEL), BF16),
           jax.ShapeDtypeStruct((1, BLOCK), F32), jax.ShapeDtypeStruct((1, D_MODEL), F32)],
        compiler_params=_cparams(),
    )(oa, oc, h0, ga, gc, g2, w_out_b, *[target] * n_sub, gf, wg_t, wu_t, wd_b)


def _ffn_bwd(dh2, dh2b, gate, up, h1, g2, wg_t, wu_t, wd_b, tm):
    r = h1.shape[0]

    def body(dh2_ref, dh2b_ref, gate_ref, up_ref, h1_ref, g2_ref, wg_ref, wu_ref, wd_ref, dgate_ref, dup_ref, dh1_ref, dg2_ref):
        @pl.when(pl.program_id(0) == 0)
        def _():
            dg2_ref[...] = jnp.zeros_like(dg2_ref)

        dyb = dh2b_ref[...]
        for cs in FF_CHUNKS:
            dact = _dot_nt(dyb, wd_ref[cs, :])
            gate = gate_ref[:, cs].astype(F32)
            up = up_ref[:, cs].astype(F32)
            sg = _sigmoid(gate)
            dgate_ref[:, cs] = (dact * up * (sg * (1.0 + gate * (1.0 - sg)))).astype(BF16)
            dup_ref[:, cs] = (dact * (gate * sg)).astype(BF16)
        dhn = _dot(dgate_ref[...], wg_ref[...]) + _dot(dup_ref[...], wu_ref[...])
        _, xhat, rstd = _rms_fwd(h1_ref[...], g2_ref[...])
        dx, dg_rows = _rms_bwd(dhn, xhat, rstd, g2_ref[...])
        dg2_ref[...] += jnp.sum(dg_rows, axis=0, keepdims=True)
        dh1_ref[...] = dh2_ref[...] + dx

    return pl.pallas_call(
        body, name="ffn_bwd", grid=(r // tm,),
        in_specs=[_row(tm, D_MODEL), _row(tm, D_MODEL), _row(tm, D_FF), _row(tm, D_FF), _row(tm, D_MODEL), _const((1, D_MODEL))]
        + [_resident((D_FF, D_MODEL))] * 3,
        out_specs=[_row(tm, D_FF), _row(tm, D_FF), _row(tm, D_MODEL), _const((1, D_MODEL))],
        out_shape=[jax.ShapeDtypeStruct((r, D_FF), BF16), jax.ShapeDtypeStruct((r, D_FF), BF16),
                   jax.ShapeDtypeStruct((r, D_MODEL), F32), jax.ShapeDtypeStruct((1, D_MODEL), F32)],
        compiler_params=_cparams(),
    )(dh2, dh2b, gate, up, h1, g2, wg_t, wu_t, wd_b)


FF_HALF = D_FF // 2


def _ffn_wgrad_gu(hn2, dgate, dup, tk):
    r = hn2.shape[0]
    n_k = r // tk

    def body(hn_ref, dg_ref, du_ref, wg_ref, wu_ref, accg_ref, accu_ref):
        k = pl.program_id(1)

        @pl.when(k == 0)
        def _():
            accg_ref[...] = jnp.zeros_like(accg_ref)
            accu_ref[...] = jnp.zeros_like(accu_ref)

        hn = hn_ref[...]
        accg_ref[...] += _dot_tn(dg_ref[...], hn)
        accu_ref[...] += _dot_tn(du_ref[...], hn)

        @pl.when(k == n_k - 1)
        def _():
            wg_ref[...] = accg_ref[...].astype(BF16)
            wu_ref[...] = accu_ref[...].astype(BF16)

    col = pl.BlockSpec((tk, FF_HALF), lambda j, k: (k, j))
    out = pl.BlockSpec((FF_HALF, D_MODEL), lambda j, k: (j, 0))
    return pl.pallas_call(
        body, name="ffn_wgrad_gu", grid=(2, n_k),
        in_specs=[pl.BlockSpec((tk, D_MODEL), lambda j, k: (k, 0)), col, col],
        out_specs=[out, out],
        out_shape=[jax.ShapeDtypeStruct((D_FF, D_MODEL), BF16)] * 2,
        scratch_shapes=[pltpu.VMEM((FF_HALF, D_MODEL), F32)] * 2,
        compiler_params=_cparams(2),
    )(hn2, dgate, dup)


def _ffn_wgrad_d(act, dh2b, tk):
    r = act.shape[0]
    n_k = r // tk

    def body(a_ref, dy_ref, wd_ref, acc_ref):
        k = pl.program_id(1)

        @pl.when(k == 0)
        def _():
            acc_ref[...] = jnp.zeros_like(acc_ref)

        acc_ref[...] += _dot_tn(a_ref[...], dy_ref[...])

        @pl.when(k == n_k - 1)
        def _():
            wd_ref[...] = acc_ref[...].astype(BF16)

    return pl.pallas_call(
        body, name="ffn_wgrad_d", grid=(2, n_k),
        in_specs=[pl.BlockSpec((tk, FF_HALF), lambda j, k: (k, j)), pl.BlockSpec((tk, D_MODEL), lambda j, k: (k, 0))],
        out_specs=pl.BlockSpec((FF_HALF, D_MODEL), lambda j, k: (j, 0)),
        out_shape=jax.ShapeDtypeStruct((D_FF, D_MODEL), BF16),
        scratch_shapes=[pltpu.VMEM((FF_HALF, D_MODEL), F32)],
        compiler_params=_cparams(2),
    )(act, dh2b)


def _out_proj_bwd(dh1, oa, oc, ga, gc, w_out_b, tm):
    r = dh1.shape[0]

    def body(dh_ref, oa_ref, oc_ref, ga_ref, gc_ref, w_ref, doa_ref, doc_ref, dw_ref, dga_ref, dgc_ref, acc_ref):
        i = pl.program_id(0)

        @pl.when(i == 0)
        def _():
            acc_ref[...] = jnp.zeros_like(acc_ref)
            dga_ref[...] = jnp.zeros_like(dga_ref)
            dgc_ref[...] = jnp.zeros_like(dgc_ref)

        dhb = dh_ref[...].astype(BF16)
        dmix = _dot_nt(dhb, w_ref[...])
        ma, xa, ra = _rms_fwd(oa_ref[...].astype(F32), ga_ref[...])
        mc, xc, rc = _rms_fwd(oc_ref[...].astype(F32), gc_ref[...])
        acc_ref[...] += _dot_tn(jnp.concatenate([ma.astype(BF16), mc.astype(BF16)], axis=1), dhb)

        @pl.when(i == r // tm - 1)
        def _():
            dw_ref[...] = acc_ref[...].astype(BF16)

        doa, dga_rows = _rms_bwd(dmix[:, 0:ATTN_W], xa, ra, ga_ref[...])
        doc, dgc_rows = _rms_bwd(dmix[:, ATTN_W:ATTN_W + CONV_W], xc, rc, gc_ref[...])
        doa_ref[...] = doa.astype(BF16)
        doc_ref[...] = doc.astype(BF16)
        dga_ref[...] += jnp.sum(dga_rows, axis=0, keepdims=True)
        dgc_ref[...] += jnp.sum(dgc_rows, axis=0, keepdims=True)

    return pl.pallas_call(
        body, name="out_proj_bwd", grid=(r // tm,),
        in_specs=[_row(tm, D_MODEL), _row(tm, ATTN_W), _row(tm, CONV_W), _const((1, ATTN_W)), _const((1, CONV_W)),
                  _const((D_MODEL, D_MODEL))],
        out_specs=[_row(tm, ATTN_W), _row(tm, CONV_W), _const((D_MODEL, D_MODEL)), _const((1, ATTN_W)), _const((1, CONV_W))],
        out_shape=[jax.ShapeDtypeStruct((r, ATTN_W), BF16), jax.ShapeDtypeStruct((r, CONV_W), BF16),
                   jax.ShapeDtypeStruct((D_MODEL, D_MODEL), BF16), jax.ShapeDtypeStruct((1, ATTN_W), F32),
                   jax.ShapeDtypeStruct((1, CONV_W), F32)],
        scratch_shapes=[pltpu.VMEM((D_MODEL, D_MODEL), F32)],
        compiler_params=_cparams(),
    )(dh1, oa, oc, ga, gc, w_out_b)


def _conv_bwd_params(doc, y, cacg, lg, lb, tm, parts):
    r = cacg.shape[0]
    n_steps = r // tm
    n = len(parts)

    def body(do_ref, y_ref, c_ref, cp_ref, lg_ref, lb_ref, *rest):
        src = rest[:n]
        dy_ref, dcw_ref, dcb_ref, dlg_ref, dlb_ref = rest[n:n + 5]
        dst = rest[n + 5:2 * n + 5]
        ub_ref, accw_ref, send_sems, recv_sems = rest[2 * n + 5:]
        i = pl.program_id(0)

        @pl.when(i == 0)
        def _():
            for cp in _scatter(src, dst, send_sems, recv_sems):
                cp.start()
            accw_ref[...] = jnp.zeros_like(accw_ref)
            dcb_ref[...] = jnp.zeros_like(dcb_ref)
            dlg_ref[...] = jnp.zeros_like(dlg_ref)
            dlb_ref[...] = jnp.zeros_like(dlb_ref)

        _shifted_copies(ub_ref, _glu_window(cp_ref, c_ref, i))

        def chunk(ci, carry):
            r0 = pl.multiple_of(ci * CONV_CHUNK, CONV_CHUNK)
            y = y_ref[pl.ds(r0, CONV_CHUNK), :]
            yc = y - jnp.mean(y, axis=-1, keepdims=True)
            rs = lax.rsqrt(jnp.mean(yc * yc, axis=-1, keepdims=True) + NORM_EPS)
            xhat = yc * rs
            yn = xhat * lg_ref[...] + lb_ref[...]
            sg = _sigmoid(yn)
            dyn = do_ref[pl.ds(r0, CONV_CHUNK), :].astype(F32) * (sg * (1.0 + yn * (1.0 - sg)))
            dlg_ref[...] += jnp.sum(dyn * xhat, axis=0, keepdims=True)
            dlb_ref[...] += jnp.sum(dyn, axis=0, keepdims=True)
            dxh = dyn * lg_ref[...]
            dy = rs * (dxh - jnp.mean(dxh, axis=-1, keepdims=True) - xhat * jnp.mean(dxh * xhat, axis=-1, keepdims=True))
            dcb_ref[...] += jnp.sum(dy, axis=0, keepdims=True)
            dy_ref[pl.ds(r0, CONV_CHUNK), :] = dy
            for j in range(CONV_K):
                a, b = divmod(FWD_SHIFTS[j], 8)
                prod = dy * ub_ref[b, pl.ds(r0 + 8 * a, CONV_CHUNK), :]
                accw_ref[j] += jnp.sum(prod.reshape(CONV_CHUNK // 8, 8, CONV_W), axis=0)
            return carry

        lax.fori_loop(0, tm // CONV_CHUNK, chunk, 0, unroll=6)

        @pl.when(i == n_steps - 1)
        def _():
            for j in range(32):
                dcw_ref[j:j + 1, :] = jnp.sum(accw_ref[j], axis=0, keepdims=True)
            for cp in _scatter(src, dst, send_sems, recv_sems):
                cp.wait()

    vec = _const((1, CONV_W))
    return pl.pallas_call(
        body, name="conv_bwd_params", grid=(n_steps,),
        in_specs=[_row(tm, CONV_W), _row(tm, CONV_W), _row(tm, 2 * CONV_W), _halo_before(tm, 2 * CONV_W), vec, vec] + [ANY] * n,
        out_specs=[_row(tm, CONV_W), _const((32, CONV_W)), vec, vec, vec] + [ANY] * n,
        out_shape=[jax.ShapeDtypeStruct((r, CONV_W), F32), jax.ShapeDtypeStruct((32, CONV_W), F32)]
        + [jax.ShapeDtypeStruct((1, CONV_W), F32)] * 3 + _scatter_landing(parts),
        scratch_shapes=[pltpu.VMEM((8, tm + HALO, CONV_W), F32), pltpu.VMEM((32, 8, CONV_W), F32),
                        pltpu.SemaphoreType.DMA((7 * n,)), pltpu.SemaphoreType.DMA((7 * n,))],
        compiler_params=_cparams(),
    )(doc, y, cacg, cacg, lg, lb, *parts)


def _conv_bwd_data(dy, cacg, cw, tm):
    r = cacg.shape[0]
    n_steps = r // tm

    def body(dy_ref, dyn_ref, c_ref, w_ref, dc_ref, ub_ref):
        last = (jnp.zeros((HALO, CONV_W), jnp.int32) + pl.program_id(0)) == n_steps - 1
        win = jnp.concatenate([dy_ref[...], jnp.where(last, 0.0, dyn_ref[...])], axis=0)
        _shifted_copies(ub_ref, win)

        def chunk(ci, carry):
            r0 = pl.multiple_of(ci * CONV_CHUNK, CONV_CHUNK)
            du = _conv_chunk(ub_ref, w_ref, r0, BWD_SHIFTS)
            ca = c_ref[pl.ds(r0, CONV_CHUNK), 0:CONV_W]
            sg = _sigmoid(c_ref[pl.ds(r0, CONV_CHUNK), CONV_W:2 * CONV_W])
            dc_ref[pl.ds(r0, CONV_CHUNK), 0:CONV_W] = (du * sg).astype(BF16)
            dc_ref[pl.ds(r0, CONV_CHUNK), CONV_W:2 * CONV_W] = (du * ca * sg * (1.0 - sg)).astype(BF16)
            return carry

        lax.fori_loop(0, tm // CONV_CHUNK, chunk, 0, unroll=6)

    halo_after = pl.BlockSpec((HALO, CONV_W), lambda i: (jnp.minimum((i + 1) * (tm // HALO), r // HALO - 1), 0))
    return pl.pallas_call(
        body, name="conv_bwd_data", grid=(n_steps,),
        in_specs=[_row(tm, CONV_W), halo_after, _row(tm, 2 * CONV_W), _const((32, CONV_W))],
        out_specs=_row(tm, 2 * CONV_W),
        out_shape=jax.ShapeDtypeStruct((r, 2 * CONV_W), BF16),
        scratch_shapes=[pltpu.VMEM((8, tm + HALO, CONV_W), F32)],
        compiler_params=_cparams(),
    )(dy, dy, cacg, cw)


def _attn_bwd(q, kv, o, do, lse, sinks, parts):
    r = q.shape[0]
    nb = r // BLOCK
    n = len(parts)

    def body(sink_ref, q_ref, kvc_ref, kvp_ref, kvm_ref, o_ref, do_ref, lse_ref, *rest):
        src = rest[:n]
        dq_ref, dkv_ref, dmeta_ref, dsink_ref = rest[n:n + 4]
        dst = rest[n + 4:2 * n + 4]
        hold_ref, bias_ref, late_ref, send_sems, recv_sems = rest[2 * n + 4:]
        i = pl.program_id(0)

        @pl.when(i == 0)
        def _():
            for cp in _scatter(src, dst, send_sems, recv_sems):
                cp.start()
            _attn_bias_init(bias_ref, late_ref)
            dmeta_ref[...] = jnp.zeros_like(dmeta_ref)
            dsink_ref[...] = jnp.zeros_like(dsink_ref)
            hold_ref[...] = jnp.zeros_like(hold_ref)

        @pl.when(i < nb)
        def _():
            lane = lax.broadcasted_iota(jnp.int32, (BLOCK, BLOCK), 1)
            lse_tile = lse_ref[...]
            zero = jnp.zeros((BLOCK, BLOCK), F32)
            block_bias = _attn_block_bias(late_ref, i)
            prev_part = late_ref[...] > 0.5
            for g in range(N_KV):
                kcat, vcat = _kv_cat(kvm_ref, kvp_ref, kvc_ref, g)
                heads = range(g * GROUP, (g + 1) * GROUP)
                qs = _stack_heads(q_ref, g)
                dosb = _stack_heads(do_ref, g)
                dos = dosb.astype(F32)
                lse = jnp.concatenate(
                    [jnp.sum(jnp.where(lane == h, lse_tile, 0.0), axis=-1, keepdims=True) + zero for h in heads], axis=0)
                delta = jnp.sum(dos * _stack_heads(o_ref, g).astype(F32), axis=-1, keepdims=True) + jnp.zeros((GROUP * BLOCK, BLOCK), F32)
                band_bias = bias_ref[g, :, BLOCK:2 * BLOCK] + block_bias[1]
                bias = [bias_ref[g, :, 0:BLOCK] + block_bias[0], jnp.where(prev_part, band_bias, NEG), jnp.where(prev_part, NEG, band_bias)]
                s = _dot_nt(qs, kcat)
                dp = _dot_nt(dosb, vcat)
                ps = [jnp.exp(s[:, k * BLOCK:(k + 1) * BLOCK] * SCALE + bias[k] - lse) for k in range(3)]
                p = jnp.concatenate(ps, axis=1)
                ds = jnp.concatenate(
                    [(ps[k] * (dp[:, k * BLOCK:(k + 1) * BLOCK] - delta)) * SCALE for k in range(3)], axis=1).astype(BF16)
                sink_term = jnp.exp(_head_rows([sink_ref[h] for h in heads]) - lse)[:, 0:1] * delta[:, 0:1]
                dq = _dot(ds, kcat).astype(BF16)
                for j, h in enumerate(heads):
                    dsink_ref[h:h + 1, :] += -jnp.sum(sink_term[j * BLOCK:(j + 1) * BLOCK])
                    dq_ref[:, h * HEAD_DIM:(h + 1) * HEAD_DIM] = dq[j * BLOCK:(j + 1) * BLOCK]
                dk_t = _dot_tn(qs, ds)
                dv_t = _dot_tn(dosb, p.astype(BF16))
                ks = slice(g * HEAD_DIM, (g + 1) * HEAD_DIM)
                vs = slice(KV_W + g * HEAD_DIM, KV_W + (g + 1) * HEAD_DIM)
                for sl, grad_t in ((ks, dk_t), (vs, dv_t)):
                    dmeta_ref[:, sl] += grad_t[:, 0:BLOCK].T
                    dkv_ref[:, sl] = hold_ref[:, sl] + grad_t[:, BLOCK:2 * BLOCK].T
                    hold_ref[:, sl] = grad_t[:, 2 * BLOCK:3 * BLOCK].T

        @pl.when(i == nb)
        def _():
            dkv_ref[...] = hold_ref[...]
            for cp in _scatter(src, dst, send_sems, recv_sems):
                cp.wait()

    def cur(i):
        return jnp.minimum(i, nb - 1)

    return pl.pallas_call(
        body, name="attn_bwd", grid=(nb + 1,),
        in_specs=[pl.BlockSpec(memory_space=pltpu.SMEM),
                  pl.BlockSpec((BLOCK, ATTN_W), lambda i: (cur(i), 0)),
                  pl.BlockSpec((BLOCK, 2 * KV_W), lambda i: (cur(i), 0)),
                  pl.BlockSpec((BLOCK, 2 * KV_W), lambda i: (jnp.maximum(cur(i) - 1, 0), 0)),
                  _const((BLOCK, 2 * KV_W)),
                  pl.BlockSpec((BLOCK, ATTN_W), lambda i: (cur(i), 0)),
                  pl.BlockSpec((BLOCK, ATTN_W), lambda i: (cur(i), 0)),
                  pl.BlockSpec((BLOCK, BLOCK), lambda i: (cur(i), 0))] + [ANY] * n,
        out_specs=[pl.BlockSpec((BLOCK, ATTN_W), lambda i: (cur(i), 0)),
                   pl.BlockSpec((BLOCK, 2 * KV_W), lambda i: (jnp.maximum(i - 1, 0), 0)),
                   _const((BLOCK, 2 * KV_W)), _const((N_HEADS, BLOCK))] + [ANY] * n,
        out_shape=[jax.ShapeDtypeStruct((r, ATTN_W), BF16), jax.ShapeDtypeStruct((r, 2 * KV_W), F32),
                   jax.ShapeDtypeStruct((BLOCK, 2 * KV_W), F32), jax.ShapeDtypeStruct((N_HEADS, BLOCK), F32)] + _scatter_landing(parts),
        scratch_shapes=[pltpu.VMEM((BLOCK, 2 * KV_W), F32), pltpu.VMEM((N_KV, GROUP * BLOCK, 2 * BLOCK), F32),
                        pltpu.VMEM((GROUP * BLOCK, BLOCK), F32), pltpu.SemaphoreType.DMA((7 * n,)), pltpu.SemaphoreType.DMA((7 * n,))],
        compiler_params=_cparams(),
    )(sinks, q, kv, kv, kv, o, do, lse, *parts)


def _in_proj_bwd(dq, dkv, dkv_meta, dc, dh1, h0, g1, w_in_t, seq, tm):
    r = h0.shape[0]
    n_tiles = r // tm
    n_out = -(-seq // tm)

    def body(dq_ref, dkv_ref, dm_ref, dc_ref, dh1_ref, h_ref, g_ref, w_ref, gx_ref, lead_ref, dwt_ref, dg_ref, dw_ref, hold_ref):
        i = pl.program_id(0)

        @pl.when(i == 0)
        def _():
            dw_ref[...] = jnp.zeros_like(dw_ref)
            dg_ref[...] = jnp.zeros_like(dg_ref)

        @pl.when(i < n_tiles)
        def _():
            meta = jnp.concatenate([dm_ref[...], jnp.zeros((tm - BLOCK, 2 * KV_W), F32)], axis=0) if tm > BLOCK else dm_ref[...]
            first = (jnp.zeros((tm, 2 * KV_W), jnp.int32) + i) == 0
            dkvb = (dkv_ref[...] + jnp.where(first, meta, 0.0)).astype(BF16)
            hn, xhat, rstd = _rms_fwd(h_ref[...], g_ref[...])
            dproj = jnp.concatenate([dq_ref[...], dkvb, dc_ref[...]], axis=1)
            dhn = _dot(dproj, w_ref[...])
            dw_ref[...] += _dot_tn(dproj, hn.astype(BF16))
            dx, dg_rows = _rms_bwd(dhn, xhat, rstd, g_ref[...])
            dg_ref[...] += jnp.sum(dg_rows, axis=0, keepdims=True)
            dh0 = dh1_ref[...] + dx

            @pl.when(i == 0)
            def _():
                lead_ref[...] = dh0[0:BLOCK]

            @pl.when((i >= 1) & (i <= n_out))
            def _():
                gx_ref[0:tm - BLOCK, :] = hold_ref[...]
                gx_ref[tm - BLOCK:tm, :] = dh0[0:BLOCK]

            hold_ref[...] = dh0[BLOCK:tm]

        @pl.when((i == n_tiles) & (n_tiles <= n_out))
        def _():
            gx_ref[0:tm - BLOCK, :] = hold_ref[...]

        @pl.when(i == n_tiles - 1)
        def _():
            dwt_ref[...] = dw_ref[...].astype(BF16)

    def tile(n):
        return pl.BlockSpec((tm, n), lambda i: (jnp.minimum(i, n_tiles - 1), 0))

    return pl.pallas_call(
        body, name="in_proj_bwd", grid=(n_tiles + 1,),
        in_specs=[tile(ATTN_W), tile(2 * KV_W), _const((BLOCK, 2 * KV_W)), tile(2 * CONV_W), tile(D_MODEL), tile(D_MODEL),
                  _const((1, D_MODEL)), _const((IN_COLS, D_MODEL))],
        out_specs=[pl.BlockSpec((tm, D_MODEL), lambda i: (jnp.clip(i - 1, 0, n_out - 1), 0)), _const((BLOCK, D_MODEL)),
                   _const((IN_COLS, D_MODEL)), _const((1, D_MODEL))],
        out_shape=[jax.ShapeDtypeStruct((seq, D_MODEL), F32), jax.ShapeDtypeStruct((BLOCK, D_MODEL), F32),
                   jax.ShapeDtypeStruct((IN_COLS, D_MODEL), BF16), jax.ShapeDtypeStruct((1, D_MODEL), F32)],
        scratch_shapes=[pltpu.VMEM((IN_COLS, D_MODEL), F32), pltpu.VMEM((tm - BLOCK, D_MODEL), F32)],
        compiler_params=_cparams(),
    )(dq, dkv, dkv_meta, dc, dh1, h0, g1, w_in_t)


def _adamw_update(w_ref, g_ref, m_ref, v_ref, d_ref, nm_ref, nv_ref):
    g = g_ref[...]
    m = ADAM_B1 * m_ref[...] + (1.0 - ADAM_B1) * g
    v = ADAM_B2 * v_ref[...] + (1.0 - ADAM_B2) * (g * g)
    m_hat = m / (1.0 - ADAM_B1 ** ADAM_STEP)
    v_hat = v / (1.0 - ADAM_B2 ** ADAM_STEP)
    d_ref[...] = -ADAM_LR * (m_hat / (jnp.sqrt(v_hat) + ADAM_EPS) + ADAM_WD * w_ref[...])
    nm_ref[...] = m
    nv_ref[...] = v


def _adamw(w, g, m, v, name):
    rows, cols = w.shape
    tr = rows
    for cand in (256, 176, 128, 64, 32, 16, 8):
        if rows % cand == 0:
            tr = cand
            break

    def body(*refs):
        _adamw_update(*refs)

    spec = _row(tr, cols)
    return pl.pallas_call(
        body, name=name, grid=(rows // tr,), in_specs=[spec] * 4, out_specs=[spec] * 3,
        out_shape=[jax.ShapeDtypeStruct((rows, cols), F32)] * 3, compiler_params=_cparams(),
    )(w, g, m, v)


MESH = pl.DeviceIdType.MESH
ANY = pl.BlockSpec(memory_space=pl.ANY)


def _place():
    x, y, c = lax.axis_index("x"), lax.axis_index("y"), lax.axis_index("c")
    chips = [(1 - x, y), (x, 1 - y), (1 - x, 1 - y)]
    return x, y, c, chips


def _gather_ici(dst, send_sems, recv_sems):
    x, y, c, chips = _place()
    sends, arrivals = [], []
    for k in range(len(dst)):
        rows = dst[k].shape[1] // 2
        half = pl.ds(c * rows, rows)
        mine = dst[k].at[2 * x + y, half]
        for p, chip in enumerate(chips):
            sems = dict(send_sem=send_sems.at[3 * k + p], recv_sem=recv_sems.at[3 * k + p], device_id=(chip[0], chip[1], c),
                        device_id_type=MESH)
            sends.append(pltpu.make_async_remote_copy(src_ref=mine, dst_ref=mine, **sems))
            theirs = dst[k].at[2 * chip[0] + chip[1], half]
            arrivals.append(pltpu.make_async_remote_copy(src_ref=theirs, dst_ref=theirs, **sems))
    return sends, arrivals


def _gather_d2d(dst, send_sems, recv_sems):
    x, y, c, chips = _place()
    sends, arrivals = [], []
    for k in range(len(dst)):
        rows = dst[k].shape[1] // 2
        for p, chip in enumerate(chips):
            sems = dict(send_sem=send_sems.at[3 * k + p], recv_sem=recv_sems.at[3 * k + p], device_id=(x, y, 1 - c),
                        device_id_type=MESH)
            mine = dst[k].at[2 * chip[0] + chip[1], pl.ds(c * rows, rows)]
            sends.append(pltpu.make_async_remote_copy(src_ref=mine, dst_ref=mine, **sems))
            theirs = dst[k].at[2 * chip[0] + chip[1], pl.ds((1 - c) * rows, rows)]
            arrivals.append(pltpu.make_async_remote_copy(src_ref=theirs, dst_ref=theirs, **sems))
    return sends, arrivals


def _own_slots(shard, dtype):
    return jnp.broadcast_to(shard[None], (N_SHARD,) + shard.shape).astype(dtype)


def _gather_weights(slots):
    n = len(slots)

    def body(*refs):
        dst = refs[n:2 * n]
        ici_send, ici_recv, d2d_send, d2d_recv = refs[2 * n:]
        sends, arrivals = _gather_ici(dst, ici_send, ici_recv)
        for cp in sends:
            cp.start()
        for cp in arrivals:
            cp.wait_recv()
        forwards, from_sibling = _gather_d2d(dst, d2d_send, d2d_recv)
        for cp in forwards:
            cp.start()
        for cp in from_sibling:
            cp.wait_recv()
        for cp in sends + forwards:
            cp.wait_send()

    return pl.pallas_call(
        body, name="gather_weights",
        in_specs=[ANY] * n, out_specs=[ANY] * n,
        out_shape=[jax.ShapeDtypeStruct(s.shape, s.dtype) for s in slots],
        input_output_aliases={k: k for k in range(n)},
        scratch_shapes=[pltpu.SemaphoreType.DMA((3 * n,))] * 4,
    )(*slots)


VMEM_WHOLE = pl.BlockSpec(memory_space=pltpu.VMEM)


SMALL_ROWS = 48
SMALL_PLACES = ((0, (0, 1), 0, 0), (1, (0, 1), 1, 0), (2, (0, 1), 2, 0), (3, (0, 16), 8, 0), (4, (0, 1), 3, 0), (5, (0, 1), 3, 512),
                (6, (0, 1), 4, 0), (7, (0, 1), 4, 512), (8, (0, 1), 5, 0), (9, (0, 16), 24, 0), (9, (16, 32), 24, 512),
                (10, (0, 1), 5, 512), (11, (0, 8), 40, 0))


def _pack_small(parts):
    n = len(parts)

    def body(*refs):
        out = refs[n]
        out[...] = jnp.zeros_like(out)
        for k, (lo, hi), r0, c0 in SMALL_PLACES:
            out[r0:r0 + hi - lo, c0:c0 + parts[k].shape[1]] = refs[k][lo:hi, :]

    return pl.pallas_call(body, name="pack_small", in_specs=[VMEM_WHOLE] * n, out_specs=VMEM_WHOLE,
                          out_shape=jax.ShapeDtypeStruct((SMALL_ROWS, D_MODEL), F32))(*parts)


def _sum_small(own, landed, parts):
    n = len(parts)

    def body(own_ref, landed_ref, *outs):
        x, y, c = lax.axis_index("x"), lax.axis_index("y"), lax.axis_index("c")
        total = jnp.zeros((SMALL_ROWS, D_MODEL), F32)
        for d in range(N_DEV):
            j = 4 * ((x + (d >> 2)) % 2) + 2 * ((y + ((d >> 1) & 1)) % 2) + (c + (d & 1)) % 2
            mine = (jnp.zeros((SMALL_ROWS, D_MODEL), jnp.int32) + j) == 0
            total = total + jnp.where(mine, own_ref[...], landed_ref[jnp.maximum(j - 1, 0)])
        for k, (lo, hi), r0, c0 in SMALL_PLACES:
            outs[k][lo:hi, :] = total[r0:r0 + hi - lo, c0:c0 + parts[k].shape[1]]

    return pl.pallas_call(body, name="sum_small", in_specs=[VMEM_WHOLE] * 2, out_specs=[VMEM_WHOLE] * n,
                          out_shape=[jax.ShapeDtypeStruct(p.shape, F32) for p in parts])(own, landed)


def _adamw_small(ws, gs, ms, vs):
    n = len(ws)

    def body(*refs):
        for k in range(n):
            w_ref, g_ref, m_ref, v_ref = (refs[j * n + k] for j in range(4))
            _adamw_update(w_ref, g_ref, m_ref, v_ref, *(refs[(4 + j) * n + k] for j in range(3)))

    shapes = [jax.ShapeDtypeStruct(w.shape, F32) for w in ws]
    out = pl.pallas_call(
        body, name="adamw_small", in_specs=[VMEM_WHOLE] * (4 * n), out_specs=[VMEM_WHOLE] * (3 * n), out_shape=shapes * 3,
    )(*ws, *gs, *ms, *vs)
    return out[:n], out[n:2 * n], out[2 * n:]


def _scatter(src, dst, send_sems, recv_sems, whole=False):
    x, y, c = lax.axis_index("x"), lax.axis_index("y"), lax.axis_index("c")
    copies = []
    for k in range(len(src)):
        for j in range(1, N_DEV):
            px, py, pc = (x + (j >> 2)) % 2, (y + ((j >> 1) & 1)) % 2, (c + (j & 1)) % 2
            rows = src[k].shape[1] // 2
            piece = src[k] if whole else src[k].at[2 * px + py, pl.ds(pc * rows, rows)]
            copies.append(pltpu.make_async_remote_copy(
                src_ref=piece, dst_ref=dst[k].at[j - 1],
                send_sem=send_sems.at[7 * k + j - 1], recv_sem=recv_sems.at[7 * k + j - 1], device_id=(px, py, pc),
                device_id_type=MESH))
    return copies


def _scatter_landing(parts, whole=False):
    return [jax.ShapeDtypeStruct((N_DEV - 1,) + (p.shape if whole else (p.shape[1] // 2, p.shape[2])), p.dtype) for p in parts]


HBM = pl.BlockSpec(memory_space=pltpu.HBM)
SEMAPHORES = pl.BlockSpec(memory_space=pltpu.SEMAPHORE)
SPLIT_COPY = pltpu.CompilerParams(has_side_effects=pltpu.SideEffectType.DATAFLOW_SIDE_EFFECTING)


def _scatter_start(part, whole, name):
    landing, = _scatter_landing([part], whole)

    def body(src_ref, land_ref, send_sems, recv_sems, src_thru, land_thru, token_ref):
        for cp in _scatter([src_ref], [land_ref], send_sems, recv_sems, whole):
            cp.start()
        token_ref[...] = jnp.zeros_like(token_ref)

    return pl.pallas_call(
        body, name=name,
        out_shape=(pltpu.SemaphoreType.DMA((N_DEV - 1,)), pltpu.SemaphoreType.DMA((N_DEV - 1,)), pltpu.HBM(part.shape, part.dtype),
                   pltpu.HBM(landing.shape, landing.dtype), jax.ShapeDtypeStruct((8, 128), F32)),
        in_specs=(HBM, HBM), out_specs=(SEMAPHORES, SEMAPHORES, HBM, HBM, VMEM_WHOLE), input_output_aliases={0: 2, 1: 3},
        compiler_params=SPLIT_COPY,
    )(pltpu.with_memory_space_constraint(part, pltpu.HBM),
      pltpu.with_memory_space_constraint(lax.empty(landing.shape, landing.dtype), pltpu.HBM))


def _scatter_wait(send_sems, recv_sems, part_thru, land_thru, after, whole, name):
    def body(src_ref, land_ref, send_ref, recv_ref, after_ref, src_dead, got_ref):
        for cp in _scatter([src_ref], [land_ref], send_ref, recv_ref, whole):
            cp.wait_send()
            cp.wait_recv()

    return pl.pallas_call(
        body, name=name,
        out_shape=(pltpu.HBM(part_thru.shape, part_thru.dtype), pltpu.HBM(land_thru.shape, land_thru.dtype)),
        in_specs=(HBM, HBM, SEMAPHORES, SEMAPHORES, ANY), out_specs=(HBM, HBM), input_output_aliases={0: 0, 1: 1},
        compiler_params=SPLIT_COPY,
    )(part_thru, land_thru, send_sems, recv_sems, after)


def _sum_pieces(own, landed, behind, name):
    n = len(own)

    def body(*refs):
        for k in range(n):
            got = refs[n + k]
            total = refs[k][...].astype(F32)
            for j in range(N_DEV - 1):
                total = total + got[j].astype(F32)
            refs[2 * n + 1 + k][...] = total

    in_specs, out_specs = [], []
    for o in own:
        in_specs.append(_row(o.shape[0] // 2, o.shape[1]))
    for o in own:
        in_specs.append(pl.BlockSpec((N_DEV - 1, o.shape[0] // 2, o.shape[1]), lambda i: (0, i, 0)))
        out_specs.append(_row(o.shape[0] // 2, o.shape[1]))
    return pl.pallas_call(
        body, name=name, grid=(2,), in_specs=in_specs + [ANY], out_specs=out_specs,
        out_shape=[jax.ShapeDtypeStruct(o.shape, F32) for o in own], compiler_params=_cparams(),
    )(*own, *landed, behind)


def _swap_with_sibling(halves, name):
    n = len(halves)

    def body(*refs):
        x, y, c = lax.axis_index("x"), lax.axis_index("y"), lax.axis_index("c")
        copies = [pltpu.make_async_remote_copy(
            src_ref=refs[k], dst_ref=refs[n + k], send_sem=refs[2 * n].at[k], recv_sem=refs[2 * n + 1].at[k],
            device_id=(x, y, 1 - c), device_id_type=MESH) for k in range(n)]
        for cp in copies:
            cp.start()
        for cp in copies:
            cp.wait()

    return pl.pallas_call(
        body, name=name, in_specs=[ANY] * n, out_specs=[ANY] * n,
        out_shape=[jax.ShapeDtypeStruct(h.shape, h.dtype) for h in halves],
        scratch_shapes=[pltpu.SemaphoreType.DMA((n,)), pltpu.SemaphoreType.DMA((n,))],
    )(*halves)


def _own_piece(part):
    rows = part.shape[1] // 2
    s = 2 * lax.axis_index("x") + lax.axis_index("y")
    return lax.dynamic_slice(part, (s, lax.axis_index("c") * rows, 0), (1, rows, part.shape[2]))[0]


def _both_halves(mine, theirs):
    south = lax.axis_index("c") == 0
    return jnp.concatenate([jnp.where(south, mine, theirs), jnp.where(south, theirs, mine)], axis=0)


def _from_col_shards(g):
    return g.transpose(1, 0, 2).reshape(g.shape[1], -1)


def kernel(x, meta_tokens, attn_norm_g, w_in, attn_sinks, conv_w, conv_b, conv_ln_g, conv_ln_b, attn_out_g, conv_out_g, w_out, ffn_norm_g, w_gate, w_up, w_down, final_norm_g, loss_target, m_meta_tokens, m_attn_norm_g, m_w_in, m_attn_sinks, m_conv_w, m_conv_b, m_conv_ln_g, m_conv_ln_b, m_attn_out_g, m_conv_out_g, m_w_out, m_ffn_norm_g, m_w_gate, m_w_up, m_w_down, m_final_norm_g, v_meta_tokens, v_attn_norm_g, v_w_in, v_attn_sinks, v_conv_w, v_conv_b, v_conv_ln_g, v_conv_ln_b, v_attn_out_g, v_conv_out_g, v_w_out, v_ffn_norm_g, v_w_gate, v_w_up, v_w_down, v_final_norm_g):
    seq = x.shape[1]
    r = -(-(seq + BLOCK) // ROW_QUANTUM) * ROW_QUANTUM
    tm_wide = 768 if seq >= 768 else 256
    shard = 2 * lax.axis_index("x") + lax.axis_index("y")

    conv_w32 = jnp.pad(conv_w[0], ((0, 1), (0, 0)))
    small_shard = jnp.concatenate([meta_tokens, conv_w32.reshape(16, 256)], axis=0)
    g_in, g_small = _gather_weights([_own_slots(w_in[0].T, BF16), _own_slots(small_shard, F32)])
    later = [_own_slots(w, BF16) for w in (w_gate[0].T, w_up[0].T, w_out[0], w_down[0])]
    w_in_t = g_in.reshape(IN_COLS, D_MODEL)
    meta_full = _from_col_shards(g_small[:, 0:N_META])
    cw_full = _from_col_shards(g_small[:, N_META:].reshape(N_SHARD, 32, 128))

    g1, ga, gc, g2 = attn_norm_g, attn_out_g, conv_out_g, ffn_norm_g
    gf = final_norm_g.reshape(1, D_MODEL)
    sinks = attn_sinks[0]

    lead = jnp.concatenate([jnp.zeros((LEAD, D_MODEL), F32), meta_full], axis=0)
    h0, q, kv, cacg = _in_proj(x[0], lead, g1, w_in_t, r, 768)
    oa, lse, *gathered = _attn_fwd(q, kv, sinks, later)
    oc, yc, g_gate, g_up, g_out, g_down = _conv_fwd(cacg, cw_full, conv_b, conv_ln_g, conv_ln_b, 384, gathered)
    wg_t, wu_t, wd_b = g_gate.reshape(D_FF, D_MODEL), g_up.reshape(D_FF, D_MODEL), g_down.reshape(D_FF, D_MODEL)
    w_out_b = g_out.reshape(D_MODEL, D_MODEL)
    h1, hn2, gate, up, act, dh2, dh2b, loss_p, dgf = _ffn_fwd(
        oa, oc, h0, ga, gc, g2, w_out_b, loss_target[0], gf, wg_t, wu_t, wd_b, 384)

    def by_shard(dw):
        return dw.reshape(N_SHARD, dw.shape[0] // N_SHARD, D_MODEL)

    dgate, dup, dh1, dg2 = _ffn_bwd(dh2, dh2b, gate, up, h1, g2, wg_t, wu_t, wd_b, 384)
    p_gate, p_up = [by_shard(dw) for dw in _ffn_wgrad_gu(hn2, dgate, dup, 768)]
    p_down = by_shard(_ffn_wgrad_d(act, dh2b, 768))
    doa, doc, dwo, dga, dgc = _out_proj_bwd(dh1, oa, oc, ga, gc, w_out_b, 768)
    p_out = by_shard(dwo)
    dy, dcw, dcb, dlg, dlb, l_gate, l_up = _conv_bwd_params(doc, yc, cacg, conv_ln_g, conv_ln_b, 384, [p_gate, p_up])
    dc = _conv_bwd_data(dy, cacg, cw_full, 384)
    dq, dkv, dkv_meta, dsink, l_out, l_down = _attn_bwd(q, kv, oa, doa, lse, sinks, [p_out, p_down])
    grad_x, dlead, dwi_t, dg1 = _in_proj_bwd(dq, dkv, dkv_meta, dc, dh1, h0, g1, w_in_t, seq, tm_wide)
    in_send, in_recv, p_in_thru, l_in_thru, started = _scatter_start(by_shard(dwi_t), False, "scatter_start")
    small_parts = [dgf, dg1, dg2, dlead[LEAD:BLOCK], dga, dgc, dcb, dlg, dlb, dcw, loss_p, dsink]
    sm_send, sm_recv, pack_thru, packs_thru, _ = _scatter_start(_pack_small(small_parts), True, "gather_small_start")

    big = ("w_gate", "w_up", "w_out", "w_down", "w_in")
    transposed = ("w_in", "w_gate", "w_up")
    grads = {}
    params = {
        "meta_tokens": (meta_tokens, m_meta_tokens, v_meta_tokens), "attn_norm_g": (attn_norm_g, m_attn_norm_g, v_attn_norm_g),
        "w_in": (w_in, m_w_in, v_w_in), "attn_sinks": (attn_sinks, m_attn_sinks, v_attn_sinks), "conv_w": (conv_w, m_conv_w, v_conv_w),
        "conv_b": (conv_b, m_conv_b, v_conv_b), "conv_ln_g": (conv_ln_g, m_conv_ln_g, v_conv_ln_g),
        "conv_ln_b": (conv_ln_b, m_conv_ln_b, v_conv_ln_b), "attn_out_g": (attn_out_g, m_attn_out_g, v_attn_out_g),
        "conv_out_g": (conv_out_g, m_conv_out_g, v_conv_out_g), "w_out": (w_out, m_w_out, v_w_out),
        "ffn_norm_g": (ffn_norm_g, m_ffn_norm_g, v_ffn_norm_g), "w_gate": (w_gate, m_w_gate, v_w_gate), "w_up": (w_up, m_w_up, v_w_up),
        "w_down": (w_down, m_w_down, v_w_down), "final_norm_g": (final_norm_g, m_final_norm_g, v_final_norm_g)}
    names = list(params)
    delta, new_m, new_v = {}, {}, {}

    def finish(group, parts, landed, behind, tag):
        halves = _sum_pieces([_own_piece(p) for p in parts], landed, behind, "sum_pieces_" + tag)
        for name, mine, theirs in zip(group, halves, _swap_with_sibling(halves, "swap_with_sibling_" + tag)):
            flip = (lambda a: a.T) if name in transposed else (lambda a: a)
            g = _both_halves(mine, theirs)
            w, m, v = params[name]
            outs = _adamw(flip(w[0]), g, flip(m[0]), flip(v[0]), "adamw_" + name)
            grads[name], delta[name], new_m[name], new_v[name] = [flip(a)[None] for a in (g, *outs)]
        return outs[0]

    done = finish(big[:4], [p_gate, p_up, p_out, p_down], [l_gate, l_up, l_out, l_down], started, "ffn_out")

    red_names = ("final_norm_g", "attn_norm_g", "ffn_norm_g", "meta_tokens", "attn_out_g", "conv_out_g", "conv_b", "conv_ln_g",
                 "conv_ln_b", "conv_w", "loss", "attn_sinks")
    pack, packs = _scatter_wait(sm_send, sm_recv, pack_thru, packs_thru, done, True, "gather_small_wait")
    red = dict(zip(red_names, _sum_small(pack, packs, small_parts)))
    loss = red["loss"][0, 0]
    for name in ("attn_norm_g", "conv_b", "conv_ln_g", "conv_ln_b", "attn_out_g", "conv_out_g", "ffn_norm_g"):
        grads[name] = red[name]
    grads["final_norm_g"] = red["final_norm_g"].reshape(D_MODEL)
    grads["attn_sinks"] = red["attn_sinks"][:, 0].reshape(1, N_HEADS)
    grads["meta_tokens"] = lax.dynamic_slice_in_dim(red["meta_tokens"], shard * (D_MODEL // N_SHARD), D_MODEL // N_SHARD, axis=1)
    grads["conv_w"] = lax.dynamic_slice_in_dim(
        red["conv_w"][0:CONV_K], shard * (CONV_W // N_SHARD), CONV_W // N_SHARD, axis=1)[None]

    p_in, l_in = _scatter_wait(in_send, in_recv, p_in_thru, l_in_thru, red["loss"], False, "scatter_wait")
    finish(big[4:], [p_in], [l_in], l_in, "in")
    rest = [name for name in names if name not in big]

    def rows_of(a):
        return a.reshape(-1, a.shape[-1])

    small = _adamw_small([rows_of(params[n][0]) for n in rest], [rows_of(grads[n]) for n in rest],
                         [rows_of(params[n][1]) for n in rest], [rows_of(params[n][2]) for n in rest])
    for dst, outs in zip((delta, new_m, new_v), small):
        for name, out in zip(rest, outs):
            dst[name] = out.reshape(params[name][0].shape)

    return (loss, grad_x[None], *[grads[n] for n in names], *[delta[n] for n in names], *[new_m[n] for n in names],
            *[new_v[n] for n in names])
```
